```python
import jax, jax.numpy as jnp
from jax import lax
import numpy as np

D_MODEL = 1024
BATCH = 32
SEQ = 2048
DEPTH = 4

GRID_W = 64
CTX_LEN = 256
HEAD_DIM = 64
ROPE_THETA = 10000.0
A_HEADS = 6
A_KV_HEADS = 2
B_CH = 256
CONV_W = 31
C_HEADS = 6
C_KV_HEADS = 2
WINDOW = 128
BLOCK = 128
A_Q = A_HEADS * HEAD_DIM
A_KV = A_KV_HEADS * HEAD_DIM
C_Q = C_HEADS * HEAD_DIM
C_KV = C_KV_HEADS * HEAD_DIM
D_MIX = A_Q + B_CH + C_Q
D_IN = A_Q + 2 * A_KV + 2 * B_CH + C_Q + 2 * C_KV
N_GROUPS = 4
EXPERTS_PER_GROUP = 8
N_EXPERTS = N_GROUPS * EXPERTS_PER_GROUP
TOP_K = 2
D_EXPERT = 512
MOE_CHUNK = 256
N_MOD = 6
EPS = 1e-6
ATTN_SCALE = HEAD_DIM ** -0.5

kernel_name = "hymba_diffusion_hybrid_hmoe"


def rms_norm(x, g):
    xf = x.astype(jnp.float32)
    y = xf * lax.rsqrt(jnp.mean(xf * xf, axis=-1, keepdims=True) + EPS)
    return (y * g.astype(jnp.float32)).astype(x.dtype)


def layer_norm(x, g, b):
    xf = x.astype(jnp.float32)
    mu = jnp.mean(xf, axis=-1, keepdims=True)
    xc = xf - mu
    var = jnp.mean(xc * xc, axis=-1, keepdims=True)
    return (xc * lax.rsqrt(var + EPS) * g.astype(jnp.float32) + b.astype(jnp.float32)).astype(x.dtype)


def modulate(h, shift, scale):
    return h * (1 + scale) + shift


def heads(t):
    return t.reshape(*t.shape[:-1], -1, HEAD_DIM)


def split_proj(p):
    sizes = [A_Q, A_KV, A_KV, 2 * B_CH, C_Q, C_KV, C_KV]
    idx = [int(i) for i in np.cumsum(sizes)[:-1]]
    return jnp.split(p, idx, axis=-1)


def grid_rope(n_tokens):
    rows = n_tokens // GRID_W
    pos_row = jnp.repeat(jnp.arange(rows, dtype=jnp.float32), GRID_W)
    pos_col = jnp.tile(jnp.arange(GRID_W, dtype=jnp.float32), rows)
    n_freq = HEAD_DIM // 4
    inv = ROPE_THETA ** (-jnp.arange(n_freq, dtype=jnp.float32) / n_freq)
    ang = jnp.concatenate([pos_row[:, None] * inv, pos_col[:, None] * inv], axis=-1)
    return jnp.cos(ang), jnp.sin(ang)


def apply_rope(x, cos, sin):
    b, s, h, d = x.shape
    xr = x.reshape(b, s, h, 2, 2, d // 4)
    x1, x2 = xr[..., 0, :], xr[..., 1, :]
    c = cos.reshape(s, 1, 2, d // 4).astype(x.dtype)
    sn = sin.reshape(s, 1, 2, d // 4).astype(x.dtype)
    return jnp.stack([x1 * c - x2 * sn, x2 * c + x1 * sn], axis=-2).reshape(b, s, h, d)


def global_attention(q, k, v, k_ctx, v_ctx):
    b, s, h, d = q.shape
    kv = k.shape[2]
    g = h // kv
    nb = s // BLOCK
    k_all = jnp.concatenate([k, k_ctx], axis=1)
    v_all = jnp.concatenate([v, v_ctx], axis=1)
    qb = q.reshape(b, nb, BLOCK, kv, g, d).transpose(1, 0, 2, 3, 4, 5)

    def one_block(qblk):
        sc = jnp.einsum('bqhgd,bkhd->bhgqk', qblk, k_all).astype(jnp.float32) * ATTN_SCALE
        p = jax.nn.softmax(sc, axis=-1).astype(v_all.dtype)
        return jnp.einsum('bhgqk,bkhd->bqhgd', p, v_all)

    o = lax.map(one_block, qb)
    return o.transpose(1, 0, 2, 3, 4, 5).reshape(b, s, h * d)


def window_attention(q, k, v, k_ctx, v_ctx, sink):
    b, s, h, d = q.shape
    kv = k.shape[2]
    g = h // kv
    nb = s // BLOCK
    lk = k_ctx.shape[1]
    nband = 3 * BLOCK

    def bands(t):
        tp = jnp.pad(t, ((0, 0), (WINDOW, WINDOW), (0, 0), (0, 0)))
        tb = tp.reshape(b, nb + 2, BLOCK, kv, d).transpose(1, 0, 2, 3, 4)
        return jnp.concatenate([tb[:-2], tb[1:-1], tb[2:]], axis=2)

    qpos = jnp.arange(s).reshape(nb, BLOCK)
    kpos = jnp.arange(-WINDOW, s + WINDOW).reshape(nb + 2, BLOCK)
    kpos = jnp.concatenate([kpos[:-2], kpos[1:-1], kpos[2:]], axis=1)
    rel = kpos[:, None, :] - qpos[:, :, None]
    mask = (jnp.abs(rel) <= WINDOW) & (kpos[:, None, :] >= 0) & (kpos[:, None, :] < s)
    qb = q.reshape(b, nb, BLOCK, kv, g, d).transpose(1, 0, 2, 3, 4, 5)
    sink_l = sink.astype(jnp.float32).reshape(kv, g, 1, 1)

    def one_block(args):
        qblk, kblk, vblk, mblk = args
        sl = jnp.einsum('bqhgd,bkhd->bhgqk', qblk, kblk).astype(jnp.float32) * ATTN_SCALE
        sl = jnp.where(mblk, sl, -jnp.inf)
        sc = jnp.einsum('bqhgd,bkhd->bhgqk', qblk, k_ctx).astype(jnp.float32) * ATTN_SCALE
        sk = jnp.broadcast_to(sink_l, sl.shape[:-1] + (1,))
        p = jax.nn.softmax(jnp.concatenate([sl, sc, sk], axis=-1), axis=-1).astype(vblk.dtype)
        return (jnp.einsum('bhgqk,bkhd->bqhgd', p[..., :nband], vblk)
                + jnp.einsum('bhgqk,bkhd->bqhgd', p[..., nband:nband + lk], v_ctx))

    o = lax.map(one_block, (qb, bands(k), bands(v), mask))
    return o.transpose(1, 0, 2, 3, 4, 5).reshape(b, s, h * d)


def context_attention(q, k, v, sink=None):
    b, l, h, d = q.shape
    kv = k.shape[2]
    g = h // kv
    lk = k.shape[1]
    qg = q.reshape(b, l, kv, g, d)
    sc = jnp.einsum('bqhgd,bkhd->bhgqk', qg, k).astype(jnp.float32) * ATTN_SCALE
    if sink is not None:
        sk = jnp.broadcast_to(sink.astype(jnp.float32).reshape(kv, g, 1, 1), sc.shape[:-1] + (1,))
        sc = jnp.concatenate([sc, sk], axis=-1)
    p = jax.nn.softmax(sc, axis=-1)[..., :lk].astype(v.dtype)
    o = jnp.einsum('bhgqk,bkhd->bqhgd', p, v)
    return o.reshape(b, l, h * d)


def conformer_conv(u, conv_w, conv_b, ln_g, ln_b):
    a, gt = jnp.split(u, 2, axis=-1)
    hgl = a * jax.nn.sigmoid(gt)
    hc = lax.conv_general_dilated(hgl, conv_w, window_strides=(1,),
                                  padding=[(CONV_W // 2, CONV_W // 2)],
                                  dimension_numbers=('NWC', 'WIO', 'NWC'),
                                  feature_group_count=B_CH) + conv_b
    return jax.nn.silu(layer_norm(hc, ln_g, ln_b))


def token_mixers(hl, hc, w_in, q_g, k_g, conv_w, conv_b, ln_g, ln_b, sink, cos, sin, with_ctx_out):
    aq, ak, av, bu, cq, ck, cv = split_proj(hl @ w_in)
    aqc, akc, avc, buc, cqc, ckc, cvc = split_proj(hc @ w_in)
    ka_c = rms_norm(heads(akc), k_g)
    va_c = heads(avc)
    qa = apply_rope(rms_norm(heads(aq), q_g), cos, sin)
    ka = apply_rope(rms_norm(heads(ak), k_g), cos, sin)
    out_a = global_attention(qa, ka, heads(av), ka_c, va_c)
    out_b = conformer_conv(bu, conv_w, conv_b, ln_g, ln_b)
    kc_c = heads(ckc)
    vc_c = heads(cvc)
    out_c = window_attention(apply_rope(heads(cq), cos, sin), apply_rope(heads(ck), cos, sin),
                             heads(cv), kc_c, vc_c, sink)
    lat = jnp.concatenate([out_a, out_b, out_c], axis=-1)
    if not with_ctx_out:
        return lat, None
    ctx_a = context_attention(rms_norm(heads(aqc), q_g), ka_c, va_c)
    ctx_b = conformer_conv(buc, conv_w, conv_b, ln_g, ln_b)
    ctx_c = context_attention(heads(cqc), kc_c, vc_c, sink)
    return lat, jnp.concatenate([ctx_a, ctx_b, ctx_c], axis=-1)


def hierarchical_moe(h, w_group, b_group, w_expert, b_expert, w_gate, w_up, w_down):
    t, d = h.shape
    gp = jax.nn.softmax((h @ w_group).astype(jnp.float32) + b_group.astype(jnp.float32), axis=-1)
    g_val, g_idx = lax.top_k(gp, 1)
    el = ((h @ w_expert).astype(jnp.float32) + b_expert.astype(jnp.float32)).reshape(t, N_GROUPS, EXPERTS_PER_GROUP)
    el = jnp.take_along_axis(el, g_idx[:, :, None], axis=1)[:, 0]
    e_val, e_loc = lax.top_k(el, TOP_K)
    e_w = jax.nn.softmax(e_val, axis=-1) * g_val
    e_idx = g_idx * EXPERTS_PER_GROUP + e_loc
    n_assign = t * TOP_K
    flat_e = e_idx.reshape(-1)
    order = jnp.argsort(flat_e)
    sorted_e = flat_e[order]
    tok = order // TOP_K
    counts = jnp.bincount(flat_e, length=N_EXPERTS)
    starts = jnp.cumsum(counts) - counts
    padded = (counts + MOE_CHUNK - 1) // MOE_CHUNK * MOE_CHUNK
    pends = jnp.cumsum(padded)
    pstarts = pends - padded
    dest = pstarts[sorted_e] + (jnp.arange(n_assign) - starts[sorted_e])
    n_chunks = -(-n_assign // MOE_CHUNK) + N_EXPERTS
    buf = jnp.zeros((n_chunks * MOE_CHUNK, d), h.dtype).at[dest].set(h[tok])
    chunk_expert = jnp.clip(jnp.searchsorted(pends, jnp.arange(n_chunks) * MOE_CHUNK, side='right'), 0, N_EXPERTS - 1)

    def run(args):
        xb, e = args
        return (jax.nn.silu(xb @ w_gate[e]) * (xb @ w_up[e])) @ w_down[e]

    out = lax.map(run, (buf.reshape(n_chunks, MOE_CHUNK, d), chunk_expert)).reshape(-1, d)
    w_sorted = e_w.reshape(-1)[order].astype(h.dtype)
    return jnp.zeros_like(h).at[tok].add(out[dest] * w_sorted[:, None])


def setup_inputs(seed: int = 0) -> dict:
    key = jax.random.key(seed)
    ks = jax.random.split(key, 32)
    f32 = jnp.float32

    def nrm(k, shape, scale):
        return jax.random.normal(k, shape, f32) * scale

    D = D_MODEL
    return {
        "x": nrm(ks[0], (BATCH, SEQ, D), 1.0),
        "c": nrm(ks[1], (BATCH, D), 1.0),
        "ctx": nrm(ks[2], (BATCH, CTX_LEN, D), 1.0),
        "c_ctx": nrm(ks[3], (D,), 1.0),
        "norm1_g": 1.0 + nrm(ks[4], (DEPTH, D), 0.05),
        "norm2_g": 1.0 + nrm(ks[5], (DEPTH, D), 0.05),
        "w_mod": nrm(ks[6], (DEPTH, D, N_MOD * D), 0.5 * D ** -0.5),
        "b_mod": nrm(ks[7], (DEPTH, N_MOD * D), 0.02),
        "w_in": nrm(ks[8], (DEPTH, D, D_IN), D ** -0.5),
        "q_norm_g": 1.0 + nrm(ks[9], (DEPTH, HEAD_DIM), 0.05),
        "k_norm_g": 1.0 + nrm(ks[10], (DEPTH, HEAD_DIM), 0.05),
        "conv_w": nrm(ks[11], (DEPTH, CONV_W, 1, B_CH), CONV_W ** -0.5),
        "conv_b": nrm(ks[12], (DEPTH, B_CH), 0.02),
        "conv_ln_g": 1.0 + nrm(ks[13], (DEPTH, B_CH), 0.05),
        "conv_ln_b": nrm(ks[14], (DEPTH, B_CH), 0.02),
        "sink": nrm(ks[15], (DEPTH, C_HEADS), 0.5),
        "w_out": nrm(ks[16], (DEPTH, D_MIX, D), D_MIX ** -0.5),
        "w_group": nrm(ks[17], (DEPTH, D, N_GROUPS), D ** -0.5),
        "b_group": nrm(ks[18], (DEPTH, N_GROUPS), 0.01),
        "w_expert": nrm(ks[19], (DEPTH, D, N_EXPERTS), D ** -0.5),
        "b_expert": nrm(ks[20], (DEPTH, N_EXPERTS), 0.01),
        "w_gate": nrm(ks[21], (DEPTH, N_EXPERTS, D, D_EXPERT), D ** -0.5),
        "w_up": nrm(ks[22], (DEPTH, N_EXPERTS, D, D_EXPERT), D ** -0.5),
        "w_down": nrm(ks[23], (DEPTH, N_EXPERTS, D_EXPERT, D), D_EXPERT ** -0.5),
        "final_g": 1.0 + nrm(ks[24], (D,), 0.05),
    }


def reference(x, c, ctx, c_ctx, norm1_g, norm2_g, w_mod, b_mod, w_in, q_norm_g, k_norm_g,
              conv_w, conv_b, conv_ln_g, conv_ln_b, sink, w_out, w_group, b_group,
              w_expert, b_expert, w_gate, w_up, w_down, final_g):
    b, s, d = x.shape
    cos, sin = grid_rope(s)
    xc = ctx
    for i in range(DEPTH):
        with_ctx_out = i < DEPTH - 1
        sh1, sc1, g1, sh2, sc2, g2 = [m[:, None, :] for m in
                                      jnp.split(jax.nn.silu(c) @ w_mod[i] + b_mod[i], N_MOD, axis=-1)]
        csh1, csc1, cg1, csh2, csc2, cg2 = jnp.split(jax.nn.silu(c_ctx) @ w_mod[i] + b_mod[i], N_MOD, axis=-1)
        hl = modulate(rms_norm(x, norm1_g[i]), sh1, sc1)
        hc = modulate(rms_norm(xc, norm1_g[i]), csh1, csc1)
        lat_mix, ctx_mix = token_mixers(hl, hc, w_in[i], q_norm_g[i], k_norm_g[i], conv_w[i], conv_b[i],
                                        conv_ln_g[i], conv_ln_b[i], sink[i], cos, sin, with_ctx_out)
        x = x + g1 * (lat_mix @ w_out[i])
        moe_w = (w_group[i], b_group[i], w_expert[i], b_expert[i], w_gate[i], w_up[i], w_down[i])
        hl = modulate(rms_norm(x, norm2_g[i]), sh2, sc2).reshape(-1, d)
        if with_ctx_out:
            xc = xc + cg1 * (ctx_mix @ w_out[i])
            hc = modulate(rms_norm(xc, norm2_g[i]), csh2, csc2).reshape(-1, d)
            y = hierarchical_moe(jnp.concatenate([hl, hc], axis=0), *moe_w)
            n_lat = hl.shape[0]
            x = x + g2 * y[:n_lat].reshape(x.shape)
            xc = xc + cg2 * y[n_lat:].reshape(xc.shape)
        else:
            x = x + g2 * hierarchical_moe(hl, *moe_w).reshape(x.shape)
    return rms_norm(x, final_g)
```

```python
import functools

import jax
import jax.numpy as jnp
import numpy as np
from jax import lax
from jax.experimental import pallas as pl
from jax.experimental.pallas import tpu as pltpu
from jax.experimental.pallas import tpu_sc as plsc

HEAD_DIM = 64
GRID_W = 64
ROPE_THETA = 10000.0
A_HEADS, A_KV_HEADS = 6, 2
C_HEADS, C_KV_HEADS = 6, 2
B_CH = 256
CONV_W = 31
WINDOW = 128
N_GROUPS = 4
EXPERTS_PER_GROUP = 8
N_EXPERTS = N_GROUPS * EXPERTS_PER_GROUP
N_MOD = 6
EPS = 1e-6
ATTN_SCALE = HEAD_DIM ** -0.5

A_Q = A_HEADS * HEAD_DIM
A_KV = A_KV_HEADS * HEAD_DIM
C_Q = C_HEADS * HEAD_DIM
C_KV = C_KV_HEADS * HEAD_DIM
D_MIX = A_Q + B_CH + C_Q
OFF_AQ = 0
OFF_AK = OFF_AQ + A_Q
OFF_AV = OFF_AK + A_KV
OFF_BU = OFF_AV + A_KV
OFF_CQ = OFF_BU + 2 * B_CH
OFF_CK = OFF_CQ + C_Q
OFF_CV = OFF_CK + C_KV
D_IN = OFF_CV + C_KV

LANES = 128
EXPERT_TILE = 256
SC_WINDOW = 128
VMEM_LIMIT = 56 * 1024 * 1024
HI_MASK = -65536


def _cparams(*sem):
    return pltpu.CompilerParams(dimension_semantics=sem, vmem_limit_bytes=VMEM_LIMIT)


def _pick(n, cands):
    for c in cands:
        if n % c == 0:
            return c
    raise ValueError(f"no tile in {cands} divides {n}")


def _pack_pairs(x):
    w = x.shape[1] // 2
    lo = lax.bitcast_convert_type(x[:, :w].astype(jnp.bfloat16).astype(jnp.float32), jnp.int32)
    hi = lax.bitcast_convert_type(x[:, w:].astype(jnp.bfloat16).astype(jnp.float32), jnp.int32)
    return (hi & HI_MASK) | lax.shift_right_logical(lo, 16)


def _unpack_pairs(p):
    lo = lax.bitcast_convert_type(lax.shift_left(p, 16), jnp.float32)
    hi = lax.bitcast_convert_type(p & HI_MASK, jnp.float32)
    return lo, hi


def _mod_kernel(c_ref, w_ref, b_ref, o_ref):
    c = c_ref[...]
    a = c * jax.nn.sigmoid(c)
    o_ref[0] = jnp.dot(a, w_ref[0], preferred_element_type=jnp.float32,
                       precision=lax.Precision.HIGHEST) + b_ref[0]


def _modulation(c_all, w_mod, b_mod):
    depth, d, n = w_mod.shape
    r = c_all.shape[0]
    tn = _pick(n, (1024, 512, 256, 128))
    return pl.pallas_call(
        _mod_kernel,
        grid=(depth, n // tn),
        in_specs=[pl.BlockSpec((r, d), lambda l, j: (0, 0)),
                  pl.BlockSpec((1, d, tn), lambda l, j: (l, 0, j)),
                  pl.BlockSpec((1, 1, tn), lambda l, j: (l, 0, j))],
        out_specs=pl.BlockSpec((1, r, tn), lambda l, j: (l, 0, j)),
        out_shape=jax.ShapeDtypeStruct((depth, r, n), jnp.float32),
        compiler_params=_cparams("arbitrary", "arbitrary"),
        name="modulation",
    )(c_all, w_mod, b_mod.reshape(depth, 1, n))


def _head_mean_sq(blk, gsum):
    sq = blk * blk
    hi = sq.astype(jnp.bfloat16)
    lo = (sq - hi.astype(jnp.float32)).astype(jnp.bfloat16)
    return (jnp.dot(hi, gsum, preferred_element_type=jnp.float32)
            + jnp.dot(lo, gsum, preferred_element_type=jnp.float32))


def _inproj_kernel(x_ref, mod_ref, g_ref, w_ref, gsum_ref, qg_ref, kg_ref, cos_ref, sin_ref,
                   qa_ref, kva_ref, hgl_ref, qc_ref, kvc_ref):
    x = x_ref[...]
    tm = x.shape[0]
    r = lax.rsqrt(jnp.mean(x * x, axis=-1, keepdims=True) + EPS)
    h = x * r * g_ref[...]
    h = h * (1.0 + mod_ref[0, 1:2, :]) + mod_ref[0, 0:1, :]
    p = jnp.dot(h.astype(jnp.bfloat16), w_ref[...], preferred_element_type=jnp.float32)

    lane = lax.broadcasted_iota(jnp.int32, (tm, LANES), 1)
    first16 = (lane % 32) < 16
    lo64 = lane < HEAD_DIM
    cos = cos_ref[...]
    sin = sin_ref[...]
    gsum = gsum_ref[...]

    def blk(off):
        return p[:, off:off + LANES]

    def rope(t):
        sw = jnp.where(first16, pltpu.roll(t, LANES - 16, axis=1), pltpu.roll(t, 16, axis=1))
        return t * cos + sw * sin

    def norm(t, g):
        return t * lax.rsqrt(_head_mean_sq(t, gsum) + EPS) * g

    def dup(t):
        sw = pltpu.roll(t, HEAD_DIM, axis=1)
        return jnp.where(lo64, t, sw), jnp.where(lo64, sw, t)

    qg = qg_ref[...]
    kg = kg_ref[...]
    for i in range(A_Q // LANES):
        t = rope(norm(blk(OFF_AQ + i * LANES), qg)) * ATTN_SCALE
        qa_ref[:, i * LANES:(i + 1) * LANES] = t.astype(jnp.bfloat16)
    k0, k1 = dup(rope(norm(blk(OFF_AK), kg)))
    v0, v1 = dup(blk(OFF_AV))
    kva_ref[:, 0:128] = k0.astype(jnp.bfloat16)
    kva_ref[:, 128:256] = k1.astype(jnp.bfloat16)
    kva_ref[:, 256:384] = v0.astype(jnp.bfloat16)
    kva_ref[:, 384:512] = v1.astype(jnp.bfloat16)
    for i in range(B_CH // LANES):
        a = blk(OFF_BU + i * LANES)
        gt = blk(OFF_BU + B_CH + i * LANES)
        hgl_ref[:, i * LANES:(i + 1) * LANES] = a * jax.nn.sigmoid(gt)
    for i in range(C_Q // LANES):
        t = rope(blk(OFF_CQ + i * LANES)) * ATTN_SCALE
        qc_ref[:, i * LANES:(i + 1) * LANES] = t.astype(jnp.bfloat16)
    k0, k1 = dup(rope(blk(OFF_CK)))
    v0, v1 = dup(blk(OFF_CV))
    kvc_ref[:, 0:128] = k0.astype(jnp.bfloat16)
    kvc_ref[:, 128:256] = k1.astype(jnp.bfloat16)
    kvc_ref[:, 256:384] = v0.astype(jnp.bfloat16)
    kvc_ref[:, 384:512] = v1.astype(jnp.bfloat16)


def _inproj(xall, mods, g1, w_in_bf, gsum, qg, kg, cos_t, sin_t, *, tm, n_lat_tiles, s, nb):
    n, d = xall.shape
    s_tiles = s // tm

    def bidx(i):
        return jnp.where(i < n_lat_tiles, (i * tm) // s, nb)

    def ridx(i):
        return jnp.where(i < n_lat_tiles, i % s_tiles, s_tiles)

    row = lambda w: pl.BlockSpec((tm, w), lambda i: (i, 0))
    const = lambda a: pl.BlockSpec(a.shape, lambda i: (0,) * a.ndim)
    bf = jnp.bfloat16
    return pl.pallas_call(
        _inproj_kernel,
        grid=(n // tm,),
        in_specs=[row(d),
                  pl.BlockSpec((1, N_MOD, d), lambda i: (bidx(i), 0, 0)),
                  const(g1), const(w_in_bf), const(gsum), const(qg), const(kg),
                  pl.BlockSpec((tm, LANES), lambda i: (ridx(i), 0)),
                  pl.BlockSpec((tm, LANES), lambda i: (ridx(i), 0))],
        out_specs=[row(A_Q), row(4 * A_KV), row(B_CH), row(C_Q), row(4 * C_KV)],
        out_shape=[jax.ShapeDtypeStruct((n, A_Q), bf), jax.ShapeDtypeStruct((n, 4 * A_KV), bf),
                   jax.ShapeDtypeStruct((n, B_CH), jnp.float32),
                   jax.ShapeDtypeStruct((n, C_Q), bf), jax.ShapeDtypeStruct((n, 4 * C_KV), bf)],
        compiler_params=_cparams("arbitrary"),
        name="inproj",
    )(xall, mods, g1, w_in_bf, gsum, qg, kg, cos_t, sin_t)


_NT = (((1,), (1,)), ((), ()))


def _stack_heads(q_ref, tq):
    lane = lax.broadcasted_iota(jnp.int32, (tq, LANES), 1)
    lo = lane < HEAD_DIM
    qb = [q_ref[:, i * LANES:(i + 1) * LANES] for i in range(3)]
    zero = jnp.zeros_like(qb[0])
    keep_lo = lambda t: jnp.where(lo, t, zero)
    keep_hi = lambda t: jnp.where(lo, zero, t)
    s0 = jnp.concatenate([keep_lo(qb[0]), keep_hi(qb[0]), keep_lo(qb[1])], axis=0)
    s1 = jnp.concatenate([keep_hi(qb[1]), keep_lo(qb[2]), keep_hi(qb[2])], axis=0)
    return s0, s1, lo


def _unstack_store(o_ref, o0, o1, lo, tq):
    bf = jnp.bfloat16
    o_ref[:, 0:128] = jnp.where(lo, o0[0:tq], o0[tq:2 * tq]).astype(bf)
    o_ref[:, 128:256] = jnp.where(lo, o0[2 * tq:3 * tq], o1[0:tq]).astype(bf)
    o_ref[:, 256:384] = jnp.where(lo, o1[tq:2 * tq], o1[2 * tq:3 * tq]).astype(bf)


def _attn_a_kernel(q_ref, kvl_ref, kvc_ref, o_ref, *, n_lat, tq):
    j = pl.program_id(1)

    def run(srcs):
        s0, s1, lo = _stack_heads(q_ref, tq)
        outs = []
        for kv, qs in enumerate((s0, s1)):
            ss = [lax.dot_general(qs, src[:, kv * LANES:(kv + 1) * LANES], _NT,
                                  preferred_element_type=jnp.float32) for src in srcs]
            m = ss[0].max(axis=-1, keepdims=True)
            for t in ss[1:]:
                m = jnp.maximum(m, t.max(axis=-1, keepdims=True))
            ps = [jnp.exp(t - m) for t in ss]
            l = ps[0].sum(axis=-1, keepdims=True)
            for t in ps[1:]:
                l = l + t.sum(axis=-1, keepdims=True)
            o = None
            for t, src in zip(ps, srcs):
                c = jnp.dot(t.astype(jnp.bfloat16), src[:, (2 + kv) * LANES:(3 + kv) * LANES],
                            preferred_element_type=jnp.float32)
                o = c if o is None else o + c
            outs.append(o / l)
        _unstack_store(o_ref, outs[0], outs[1], lo, tq)

    @pl.when(j < n_lat)
    def _():
        run([kvl_ref, kvc_ref])

    @pl.when(j >= n_lat)
    def _():
        run([kvc_ref])


def _attn_a(qa, kva, *, nb, s, l, tq, with_ctx):
    n = qa.shape[0]
    n_lat, n_ctx = s // tq, (l // tq if with_ctx else 0)
    lat_rows = nb * s

    def qidx(b, j):
        return jnp.where(j < n_lat, b * n_lat + j, lat_rows // tq + b * (l // tq) + (j - n_lat))

    return pl.pallas_call(
        functools.partial(_attn_a_kernel, n_lat=n_lat, tq=tq),
        grid=(nb, n_lat + n_ctx),
        in_specs=[pl.BlockSpec((tq, A_Q), lambda b, j: (qidx(b, j), 0)),
                  pl.BlockSpec((s, 4 * A_KV), lambda b, j: (b, 0)),
                  pl.BlockSpec((l, 4 * A_KV), lambda b, j: (lat_rows // l + b, 0))],
        out_specs=pl.BlockSpec((tq, A_Q), lambda b, j: (qidx(b, j), 0)),
        out_shape=jax.ShapeDtypeStruct((n, A_Q), jnp.bfloat16),
        compiler_params=_cparams("arbitrary", "arbitrary"),
        name="attn_global",
    )(qa, kva, kva)


def _attn_c_kernel(sink_ref, q_ref, kvl_ref, kvc_ref, o_ref, *, n_lat, s):
    j = pl.program_id(1)
    tq = WINDOW
    band = 3 * WINDOW

    def run(is_lat):
        s0, s1, lo = _stack_heads(q_ref, tq)
        if is_lat:
            start = jnp.clip((j - 1) * WINDOW, 0, s - band)
            start = pl.multiple_of(start, WINDOW)
            qpos = j * WINDOW + (lax.broadcasted_iota(jnp.int32, (3 * tq, band), 0) & (tq - 1))
            kpos = start + lax.broadcasted_iota(jnp.int32, (3 * tq, band), 1)
            mask = jnp.abs(kpos - qpos) <= WINDOW
        outs = []
        for kv, qs in enumerate((s0, s1)):
            kcol = slice(kv * LANES, (kv + 1) * LANES)
            vcol = slice((2 + kv) * LANES, (3 + kv) * LANES)
            sk = jnp.concatenate(
                [jnp.full((tq, 1), sink_ref[3 * kv + g], jnp.float32) for g in range(3)], axis=0)
            sc = lax.dot_general(qs, kvc_ref[:, kcol], _NT, preferred_element_type=jnp.float32)
            m = jnp.maximum(sc.max(axis=-1, keepdims=True), sk)
            if is_lat:
                kb = kvl_ref[pl.ds(start, band), kcol]
                vb = kvl_ref[pl.ds(start, band), vcol]
                sl = lax.dot_general(qs, kb, _NT, preferred_element_type=jnp.float32)
                sl = jnp.where(mask, sl, -jnp.inf)
                m = jnp.maximum(m, sl.max(axis=-1, keepdims=True))
            pc = jnp.exp(sc - m)
            den = pc.sum(axis=-1, keepdims=True) + jnp.exp(sk - m)
            o = jnp.dot(pc.astype(jnp.bfloat16), kvc_ref[:, vcol], preferred_element_type=jnp.float32)
            if is_lat:
                pb = jnp.exp(sl - m)
                den = den + pb.sum(axis=-1, keepdims=True)
                o = o + jnp.dot(pb.astype(jnp.bfloat16), vb, preferred_element_type=jnp.float32)
            outs.append(o / den)
        _unstack_store(o_ref, outs[0], outs[1], lo, tq)

    @pl.when(j < n_lat)
    def _():
        run(True)

    @pl.when(j >= n_lat)
    def _():
        run(False)


def _attn_c(qc, kvc, sink, *, nb, s, l, with_ctx):
    n = qc.shape[0]
    tq = WINDOW
    n_lat, n_ctx = s // tq, (l // tq if with_ctx else 0)
    lat_rows = nb * s

    def qidx(b, j):
        return jnp.where(j < n_lat, b * n_lat + j, lat_rows // tq + b * (l // tq) + (j - n_lat))

    return pl.pallas_call(
        functools.partial(_attn_c_kernel, n_lat=n_lat, s=s),
        grid=(nb, n_lat + n_ctx),
        in_specs=[pl.BlockSpec(memory_space=pltpu.SMEM),
                  pl.BlockSpec((tq, C_Q), lambda b, j: (qidx(b, j), 0)),
                  pl.BlockSpec((s, 4 * C_KV), lambda b, j: (b, 0)),
                  pl.BlockSpec((l, 4 * C_KV), lambda b, j: (lat_rows // l + b, 0))],
        out_specs=pl.BlockSpec((tq, C_Q), lambda b, j: (qidx(b, j), 0)),
        out_shape=jax.ShapeDtypeStruct((n, C_Q), jnp.bfloat16),
        compiler_params=_cparams("arbitrary", "arbitrary"),
        name="attn_window",
    )(sink, qc, kvc, kvc)


CONV_HALO = 16


def _conv_kernel(prev_ref, cur_ref, next_ref, w_ref, b_ref, g_ref, beta_ref, o_ref, xp_ref, *,
                 n_lat, lat_chunks, ctx_chunks, tc):
    j = pl.program_id(1)
    is_lat = j < n_lat
    pos = jnp.where(is_lat, j, j - n_lat)
    last = jnp.where(is_lat, lat_chunks - 1, ctx_chunks - 1)
    has_prev = (pos > 0).astype(jnp.float32)
    has_next = (pos < last).astype(jnp.float32)
    xp_ref[0:CONV_HALO, :] = prev_ref[tc - CONV_HALO:tc, :] * has_prev
    xp_ref[CONV_HALO:CONV_HALO + tc, :] = cur_ref[...]
    xp_ref[CONV_HALO + tc:2 * CONV_HALO + tc, :] = next_ref[0:CONV_HALO, :] * has_next
    acc = jnp.zeros((tc, B_CH), jnp.float32)
    base = CONV_HALO - CONV_W // 2
    for k in range(CONV_W):
        acc = acc + xp_ref[base + k:base + k + tc, :] * w_ref[k:k + 1, :]
    hc = acc + b_ref[...]
    mu = jnp.mean(hc, axis=-1, keepdims=True)
    xc = hc - mu
    var = jnp.mean(xc * xc, axis=-1, keepdims=True)
    y = xc * lax.rsqrt(var + EPS) * g_ref[...] + beta_ref[...]
    o_ref[...] = (y * jax.nn.sigmoid(y)).astype(o_ref.dtype)


def _conv(hgl, w, b, g, beta, *, nb, s, l, tc, with_ctx):
    n = hgl.shape[0]
    lat_chunks, ctx_chunks = s // tc, l // tc
    n_lat, n_ctx = lat_chunks, (ctx_chunks if with_ctx else 0)
    lat_rows = nb * s

    def idx(b_, j, delta):
        lat = b_ * lat_chunks + jnp.clip(j + delta, 0, lat_chunks - 1)
        ctx = lat_rows // tc + b_ * ctx_chunks + jnp.clip(j - n_lat + delta, 0, ctx_chunks - 1)
        return jnp.where(j < n_lat, lat, ctx)

    blk = lambda delta: pl.BlockSpec((tc, B_CH), lambda b_, j: (idx(b_, j, delta), 0))
    const = lambda a: pl.BlockSpec(a.shape, lambda b_, j: (0,) * a.ndim)
    return pl.pallas_call(
        functools.partial(_conv_kernel, n_lat=n_lat, lat_chunks=lat_chunks, ctx_chunks=ctx_chunks, tc=tc),
        grid=(nb, n_lat + n_ctx),
        in_specs=[blk(-1), blk(0), blk(1), const(w), const(b), const(g), const(beta)],
        out_specs=blk(0),
        out_shape=jax.ShapeDtypeStruct((n, B_CH), jnp.bfloat16),
        scratch_shapes=[pltpu.VMEM((tc + 2 * CONV_HALO, B_CH), jnp.float32)],
        compiler_params=_cparams("arbitrary", "arbitrary"),
        name="conformer_conv",
    )(hgl, hgl, hgl, w, b, g, beta)


def _outproj_kernel(oa_ref, ob_ref, oc_ref, x_ref, mod_ref, g_ref, w_ref, wr_ref, br_ref, tri_ref,
                    xo_ref, hp_ref, ri_ref, rw_ref, cnt_ref):
    i = pl.program_id(0)
    tm = x_ref.shape[0]
    f32 = jnp.float32
    mix = (jnp.dot(oa_ref[...], w_ref[0:A_Q, :], preferred_element_type=f32)
           + jnp.dot(ob_ref[...], w_ref[A_Q:A_Q + B_CH, :], preferred_element_type=f32)
           + jnp.dot(oc_ref[...], w_ref[A_Q + B_CH:D_MIX, :], preferred_element_type=f32))
    x = x_ref[...] + mod_ref[0, 2:3, :] * mix
    xo_ref[...] = x
    r = lax.rsqrt(jnp.mean(x * x, axis=-1, keepdims=True) + EPS)
    h = x * r * g_ref[...]
    h = h * (1.0 + mod_ref[0, 4:5, :]) + mod_ref[0, 3:4, :]
    hp_ref[...] = _pack_pairs(h)

    lg = jnp.dot(h, wr_ref[...], preferred_element_type=f32, precision=lax.Precision.HIGHEST) + br_ref[...]
    lane = lax.broadcasted_iota(jnp.int32, (tm, LANES), 1).astype(f32)
    big = float(LANES)
    ninf = -jnp.inf
    glog = jnp.where(lane < N_GROUPS, lg, ninf)
    gmax = glog.max(axis=-1, keepdims=True)
    g_val = 1.0 / jnp.exp(glog - gmax).sum(axis=-1, keepdims=True)
    g_idx = jnp.where(glog == gmax, lane, big).min(axis=-1, keepdims=True)
    e_lo = N_GROUPS + EXPERTS_PER_GROUP * g_idx
    el = jnp.where((lane >= e_lo) & (lane < e_lo + EXPERTS_PER_GROUP), lg, ninf)
    v0 = el.max(axis=-1, keepdims=True)
    i0 = jnp.where(el == v0, lane, big).min(axis=-1, keepdims=True)
    el1 = jnp.where(lane == i0, ninf, el)
    v1 = el1.max(axis=-1, keepdims=True)
    i1 = jnp.where(el1 == v1, lane, big).min(axis=-1, keepdims=True)
    t = jnp.exp(v1 - v0)
    w0 = g_val / (1.0 + t)
    w1 = g_val * t / (1.0 + t)
    e0 = i0 - N_GROUPS
    e1 = i1 - N_GROUPS

    @pl.when(i == 0)
    def _():
        cnt_ref[...] = jnp.zeros_like(cnt_ref)

    tri = tri_ref[...]
    ranks = []
    for e in (e0, e1):
        oh = lane == e
        ohf = oh.astype(f32)
        pre = jnp.dot(tri, ohf.astype(jnp.bfloat16), preferred_element_type=f32) + cnt_ref[0:1, :]
        ranks.append(jnp.where(oh, pre, 0.0).sum(axis=-1, keepdims=True))
        cnt_ref[0:1, :] = cnt_ref[0:1, :] + ohf.sum(axis=0, keepdims=True)
    ri = jnp.where(lane == 0, e0, jnp.where(lane == 1, e1,
                   jnp.where(lane == 2, ranks[0], jnp.where(lane == 3, ranks[1], 0.0))))
    ri_ref[...] = ri.astype(jnp.int32)
    rw_ref[...] = jnp.where(lane == 0, w0, jnp.where(lane == 1, w1, 0.0))


def _outproj(oa, ob, oc, xall, mods, g2, w_out_bf, wr, br, tri, *, tm, n_tiles, n_lat_tiles, s, nb):
    n, d = xall.shape
    rows = n_tiles * tm

    def bidx(i):
        return jnp.where(i < n_lat_tiles, (i * tm) // s, nb)

    row = lambda w: pl.BlockSpec((tm, w), lambda i: (i, 0))
    const = lambda a: pl.BlockSpec(a.shape, lambda i: (0,) * a.ndim)
    return pl.pallas_call(
        _outproj_kernel,
        grid=(n_tiles,),
        in_specs=[row(A_Q), row(B_CH), row(C_Q), row(d),
                  pl.BlockSpec((1, N_MOD, d), lambda i: (bidx(i), 0, 0)),
                  const(g2), const(w_out_bf), const(wr), const(br), const(tri)],
        out_specs=[row(d), row(d // 2), row(LANES), row(LANES),
                   pl.BlockSpec((8, LANES), lambda i: (0, 0))],
        out_shape=[jax.ShapeDtypeStruct((n, d), jnp.float32),
                   jax.ShapeDtypeStruct((rows, d // 2), jnp.int32),
                   jax.ShapeDtypeStruct((rows, LANES), jnp.int32),
                   jax.ShapeDtypeStruct((rows, LANES), jnp.float32),
                   jax.ShapeDtypeStruct((8, LANES), jnp.float32)],
        input_output_aliases={3: 0},
        compiler_params=_cparams("arbitrary"),
        name="outproj_router",
    )(oa, ob, oc, xall, mods, g2, w_out_bf, wr, br, tri)


def _sc_mesh():
    return plsc.VectorSubcoreMesh(core_axis_name="core", subcore_axis_name="subcore")


def sc_gather_rows(table, idx):
    r = idx.shape[0]
    w = table.shape[1]
    half = r // SC_WINDOW // 2
    idx2 = idx.reshape(1, r)

    @functools.partial(pl.kernel, out_type=jax.ShapeDtypeStruct((r, w), table.dtype), mesh=_sc_mesh())
    def k(x_hbm, i_hbm, o_hbm):
        def body(i_vmem, o_vmem):
            pltpu.sync_copy(x_hbm.at[i_vmem.at[0]], o_vmem)

        pltpu.emit_pipeline(
            body,
            grid=(2, half),
            in_specs=[pl.BlockSpec((1, SC_WINDOW), lambda c, i: (0, c * half + i))],
            out_specs=[pl.BlockSpec((SC_WINDOW, w), lambda c, i: (c * half + i, 0),
                                    pipeline_mode=pl.Buffered(1))],
            core_axis_name=("core", "subcore"),
            dimension_semantics=(pltpu.PARALLEL, pltpu.PARALLEL),
        )(i_hbm, o_hbm)

    return k(table, idx2)


def sc_scatter_rows2(rows, idx_a, idx_b, n_out):
    r, w = rows.shape
    half = r // SC_WINDOW // 2
    ia = idx_a.reshape(1, r)
    ib = idx_b.reshape(1, r)

    @functools.partial(pl.kernel, out_type=jax.ShapeDtypeStruct((n_out, w), rows.dtype), mesh=_sc_mesh(),
                       scratch_types=[])
    def k(x_hbm, ia_hbm, ib_hbm, o_hbm):
        def body(x_vmem, ia_vmem, ib_vmem):
            pltpu.sync_copy(x_vmem, o_hbm.at[ia_vmem.at[0]])
            pltpu.sync_copy(x_vmem, o_hbm.at[ib_vmem.at[0]])

        idx_spec = pl.BlockSpec((1, SC_WINDOW), lambda c, i: (0, c * half + i))
        pltpu.emit_pipeline(
            body,
            grid=(2, half),
            in_specs=[pl.BlockSpec((SC_WINDOW, w), lambda c, i: (c * half + i, 0),
                                   pipeline_mode=pl.Buffered(1)),
                      idx_spec, idx_spec],
            out_specs=[],
            core_axis_name=("core", "subcore"),
            dimension_semantics=(pltpu.PARALLEL, pltpu.PARALLEL),
        )(x_hbm, ia_hbm, ib_hbm)

    return k(rows, ia, ib)


def _expert_kernel(te_ref, nv_ref, x_ref, wg_ref, wu_ref, wd_ref, o_ref):
    t = pl.program_id(0)
    nvalid = nv_ref[t]

    @pl.when(nvalid > 0)
    def _():
        rows = lax.broadcasted_iota(jnp.int32, x_ref.shape, 0)
        lo, hi = _unpack_pairs(jnp.where(rows < nvalid, x_ref[...], 0))
        xb = jnp.concatenate([lo, hi], axis=1).astype(jnp.bfloat16)
        g = jnp.dot(xb, wg_ref[0], preferred_element_type=jnp.float32)
        u = jnp.dot(xb, wu_ref[0], preferred_element_type=jnp.float32)
        a = (g * jax.nn.sigmoid(g) * u).astype(jnp.bfloat16)
        o_ref[...] = _pack_pairs(jnp.dot(a, wd_ref[0], preferred_element_type=jnp.float32))

    @pl.when(nvalid == 0)
    def _():
        o_ref[...] = jnp.zeros_like(o_ref)


def _experts(buf, tile_expert, tile_nvalid, wg, wu, wd):
    rows, wp = buf.shape
    _, d, f = wg.shape
    n_tiles = rows // EXPERT_TILE
    grid_spec = pltpu.PrefetchScalarGridSpec(
        num_scalar_prefetch=2,
        grid=(n_tiles,),
        in_specs=[pl.BlockSpec((EXPERT_TILE, wp), lambda t, te, nv: (t, 0)),
                  pl.BlockSpec((1, d, f), lambda t, te, nv: (te[t], 0, 0)),
                  pl.BlockSpec((1, d, f), lambda t, te, nv: (te[t], 0, 0)),
                  pl.BlockSpec((1, f, d), lambda t, te, nv: (te[t], 0, 0))],
        out_specs=pl.BlockSpec((EXPERT_TILE, wp), lambda t, te, nv: (t, 0)),
    )
    return pl.pallas_call(
        _expert_kernel,
        grid_spec=grid_spec,
        out_shape=jax.ShapeDtypeStruct((rows, wp), jnp.int32),
        compiler_params=_cparams("arbitrary"),
        name="expert_ffn",
    )(tile_expert, tile_nvalid, buf, wg, wu, wd)


def _combine_kernel(y0_ref, y1_ref, rw_ref, x_ref, mod_ref, fg_ref, o_ref, *, final):
    rw = rw_ref[...]
    w0 = rw[:, 0:1]
    w1 = rw[:, 1:2]
    a_lo, a_hi = _unpack_pairs(y0_ref[...])
    b_lo, b_hi = _unpack_pairs(y1_ref[...])
    y = jnp.concatenate([a_lo * w0 + b_lo * w1, a_hi * w0 + b_hi * w1], axis=1)
    x = x_ref[...] + mod_ref[0, 5:6, :] * y
    if final:
        r = lax.rsqrt(jnp.mean(x * x, axis=-1, keepdims=True) + EPS)
        x = x * r * fg_ref[...]
    o_ref[...] = x


def _combine(y, rw, xall, mods, final_g, *, tm, n_tiles, n_lat_tiles, s, nb, final):
    n, d = xall.shape

    def bidx(i):
        return jnp.where(i < n_lat_tiles, (i * tm) // s, nb)

    row = lambda w: pl.BlockSpec((tm, w), lambda i: (i, 0))
    out_rows = n_tiles * tm if final else n
    return pl.pallas_call(
        functools.partial(_combine_kernel, final=final),
        grid=(n_tiles,),
        in_specs=[row(d // 2),
                  pl.BlockSpec((tm, d // 2), lambda i: (i + n_tiles, 0)),
                  row(LANES), row(d),
                  pl.BlockSpec((1, N_MOD, d), lambda i: (bidx(i), 0, 0)),
                  pl.BlockSpec(final_g.shape, lambda i: (0, 0))],
        out_specs=row(d),
        out_shape=jax.ShapeDtypeStruct((out_rows, d), jnp.float32),
        input_output_aliases={} if final else {3: 0},
        compiler_params=_cparams("arbitrary"),
        name="moe_combine",
    )(y, y, rw, xall, mods, final_g)


def _dispatch_plan(ri, counts, n_rows_buf):
    e0, e1, r0, r1 = ri[:, 0], ri[:, 1], ri[:, 2], ri[:, 3]
    cnt = counts[0, :N_EXPERTS].astype(jnp.int32)
    padded = (cnt + EXPERT_TILE - 1) // EXPERT_TILE * EXPERT_TILE
    pends = jnp.cumsum(padded)
    pstarts = pends - padded
    dest0 = pstarts[e0] + r0
    dest1 = pstarts[e1] + r1
    tile_start = jnp.arange(n_rows_buf // EXPERT_TILE, dtype=jnp.int32) * EXPERT_TILE
    te = jnp.clip(jnp.searchsorted(pends, tile_start, side="right"), 0, N_EXPERTS - 1).astype(jnp.int32)
    nvalid = jnp.clip(cnt[te] - (tile_start - pstarts[te]), 0, EXPERT_TILE).astype(jnp.int32)
    return dest0, dest1, te, nvalid


def _rope_tables(s, tm):
    pos = np.arange(s)
    pos_row = jnp.asarray(pos // GRID_W, jnp.float32)
    pos_col = jnp.asarray(pos % GRID_W, jnp.float32)
    n_freq = HEAD_DIM // 4
    inv = ROPE_THETA ** (-jnp.arange(n_freq, dtype=jnp.float32) / n_freq)
    ang_row = pos_row[:, None] * inv
    ang_col = pos_col[:, None] * inv
    ang = jnp.concatenate([ang_row, ang_row, ang_col, ang_col] * (LANES // HEAD_DIM), axis=-1)
    sign = np.where((np.arange(LANES) % 32) < 16, -1.0, 1.0).astype(np.float32)
    cos_t = jnp.concatenate([jnp.cos(ang), jnp.ones((tm, LANES), jnp.float32)], axis=0)
    sin_t = jnp.concatenate([jnp.sin(ang) * sign, jnp.zeros((tm, LANES), jnp.float32)], axis=0)
    return cos_t, sin_t


def kernel(x, c, ctx, c_ctx, norm1_g, norm2_g, w_mod, b_mod, w_in, q_norm_g, k_norm_g, conv_w, conv_b, conv_ln_g, conv_ln_b, sink, w_out, w_group, b_group, w_expert, b_expert, w_gate, w_up, w_down, final_g):
    nb, s, d = x.shape
    l = ctx.shape[1]
    depth = w_in.shape[0]
    assert w_in.shape[2] == D_IN and w_out.shape[1] == D_MIX
    assert s % GRID_W == 0 and s >= 3 * WINDOW and s % WINDOW == 0 and l % WINDOW == 0
    n_lat, n_ctx = nb * s, nb * l
    tm = _pick(np.gcd(s, n_ctx), (512, 256, 128))
    tq = _pick(np.gcd(s, l), (256, 128))
    tc = _pick(np.gcd(s, l), (256, 128))
    assert n_lat % l == 0
    bf = jnp.bfloat16
    f32 = jnp.float32

    xall = jnp.concatenate([x.reshape(n_lat, d), ctx.reshape(n_ctx, d)], axis=0)
    c_all = jnp.concatenate([c, c_ctx[None, :]], axis=0)
    mods_all = _modulation(c_all, w_mod, b_mod).reshape(depth, nb + 1, N_MOD, d)
    cos_t, sin_t = _rope_tables(s, tm)
    head_id = np.arange(LANES) // HEAD_DIM
    gsum = jnp.asarray((head_id[:, None] == head_id[None, :]) / HEAD_DIM, bf)
    tri = jnp.asarray(np.tril(np.ones((tm, tm), np.float32), -1), bf)
    n_lat_tiles = n_lat // tm

    out = None
    for i in range(depth):
        last = i == depth - 1
        with_ctx = not last
        mods = mods_all[i]
        qg = jnp.tile(q_norm_g[i], LANES // HEAD_DIM)[None, :]
        kg = jnp.tile(k_norm_g[i], LANES // HEAD_DIM)[None, :]
        qa, kva, hgl, qc, kvc = _inproj(xall, mods, norm1_g[i][None, :], w_in[i].astype(bf), gsum, qg, kg,
                                        cos_t, sin_t, tm=tm, n_lat_tiles=n_lat_tiles, s=s, nb=nb)
        oa = _attn_a(qa, kva, nb=nb, s=s, l=l, tq=tq, with_ctx=with_ctx)
        oc = _attn_c(qc, kvc, sink[i], nb=nb, s=s, l=l, with_ctx=with_ctx)
        ob = _conv(hgl, conv_w[i].reshape(CONV_W, B_CH), conv_b[i][None, :], conv_ln_g[i][None, :],
                   conv_ln_b[i][None, :], nb=nb, s=s, l=l, tc=tc, with_ctx=with_ctx)

        n_tok = n_lat + n_ctx if with_ctx else n_lat
        n_tiles = n_tok // tm
        wr = jnp.zeros((d, LANES), f32).at[:, :N_GROUPS].set(w_group[i])
        wr = wr.at[:, N_GROUPS:N_GROUPS + N_EXPERTS].set(w_expert[i])
        br = jnp.zeros((1, LANES), f32).at[0, :N_GROUPS].set(b_group[i])
        br = br.at[0, N_GROUPS:N_GROUPS + N_EXPERTS].set(b_expert[i])
        xall, hp, ri, rw, counts = _outproj(oa, ob, oc, xall, mods, norm2_g[i][None, :], w_out[i].astype(bf),
                                            wr, br, tri, tm=tm, n_tiles=n_tiles, n_lat_tiles=n_lat_tiles,
                                            s=s, nb=nb)
        n_rows_buf = 2 * n_tok + N_EXPERTS * EXPERT_TILE
        dest0, dest1, te, nvalid = _dispatch_plan(ri, counts, n_rows_buf)
        buf = sc_scatter_rows2(hp, dest0, dest1, n_rows_buf)
        eo = _experts(buf, te, nvalid, w_gate[i].astype(bf), w_up[i].astype(bf), w_down[i].astype(bf))
        y = sc_gather_rows(eo, jnp.concatenate([dest0, dest1]))
        res = _combine(y, rw, xall, mods, final_g[None, :], tm=tm, n_tiles=n_tiles,
                       n_lat_tiles=n_lat_tiles, s=s, nb=nb, final=last)
        if last:
            out = res
        else:
            xall = res
    return out.reshape(nb, s, d)
```

```python
import functools

import jax
import jax.numpy as jnp
import numpy as np
from jax import lax
from jax.experimental import pallas as pl
from jax.experimental.pallas import tpu as pltpu
from jax.experimental.pallas import tpu_sc as plsc

HEAD_DIM = 64
GRID_W = 64
ROPE_THETA = 10000.0
A_HEADS, A_KV_HEADS = 6, 2
C_HEADS, C_KV_HEADS = 6, 2
B_CH = 256
CONV_W = 31
WINDOW = 128
N_GROUPS = 4
EXPERTS_PER_GROUP = 8
N_EXPERTS = N_GROUPS * EXPERTS_PER_GROUP
N_MOD = 6
EPS = 1e-6
ATTN_SCALE = HEAD_DIM ** -0.5
LOG2E = 1.4426950408889634
Q_SCALE = ATTN_SCALE * LOG2E

A_Q = A_HEADS * HEAD_DIM
A_KV = A_KV_HEADS * HEAD_DIM
C_Q = C_HEADS * HEAD_DIM
C_KV = C_KV_HEADS * HEAD_DIM
D_MIX = A_Q + B_CH + C_Q
OFF_AQ = 0
OFF_AK = OFF_AQ + A_Q
OFF_AV = OFF_AK + A_KV
OFF_BU = OFF_AV + A_KV
OFF_CQ = OFF_BU + 2 * B_CH
OFF_CK = OFF_CQ + C_Q
OFF_CV = OFF_CK + C_KV
D_IN = OFF_CV + C_KV

LANES = 128
EXPERT_TILE = 256
SC_WINDOW = 128
VMEM_LIMIT = 56 * 1024 * 1024
HI_MASK = -65536


def _cparams(*sem):
    return pltpu.CompilerParams(dimension_semantics=sem, vmem_limit_bytes=VMEM_LIMIT)


def _pick(n, cands):
    for c in cands:
        if n % c == 0:
            return c
    raise ValueError(f"no tile in {cands} divides {n}")


def _pack_pairs(x):
    w = x.shape[1] // 2
    lo = lax.bitcast_convert_type(x[:, :w].astype(jnp.bfloat16).astype(jnp.float32), jnp.int32)
    hi = lax.bitcast_convert_type(x[:, w:].astype(jnp.bfloat16).astype(jnp.float32), jnp.int32)
    return (hi & HI_MASK) | lax.shift_right_logical(lo, 16)


def _unpack_pairs(p):
    lo = lax.bitcast_convert_type(lax.shift_left(p, 16), jnp.float32)
    hi = lax.bitcast_convert_type(p & HI_MASK, jnp.float32)
    return lo, hi


def _mod_kernel(c_ref, w_ref, b_ref, o_ref):
    c = c_ref[...]
    a = c * jax.nn.sigmoid(c)
    o_ref[0] = jnp.dot(a, w_ref[0], preferred_element_type=jnp.float32,
                       precision=lax.Precision.HIGHEST) + b_ref[0]


def _modulation(c_all, w_mod, b_mod):
    depth, d, n = w_mod.shape
    r = c_all.shape[0]
    tn = _pick(n, (1024, 512, 256, 128))
    return pl.pallas_call(
        _mod_kernel,
        grid=(depth, n // tn),
        in_specs=[pl.BlockSpec((r, d), lambda l, j: (0, 0)),
                  pl.BlockSpec((1, d, tn), lambda l, j: (l, 0, j)),
                  pl.BlockSpec((1, 1, tn), lambda l, j: (l, 0, j))],
        out_specs=pl.BlockSpec((1, r, tn), lambda l, j: (l, 0, j)),
        out_shape=jax.ShapeDtypeStruct((depth, r, n), jnp.float32),
        compiler_params=_cparams("arbitrary", "arbitrary"),
        name="modulation",
    )(c_all, w_mod, b_mod.reshape(depth, 1, n))


def _head_mean_sq(blk, gsum):
    sq = blk * blk
    hi = sq.astype(jnp.bfloat16)
    lo = (sq - hi.astype(jnp.float32)).astype(jnp.bfloat16)
    return (jnp.dot(hi, gsum, preferred_element_type=jnp.float32)
            + jnp.dot(lo, gsum, preferred_element_type=jnp.float32))


def _inproj_kernel(x_ref, mod_ref, g_ref, w_ref, gsum_ref, qg_ref, kg_ref, cos_ref, sin_ref,
                   qa_ref, kva_ref, hgl_ref, qc_ref, kvc_ref):
    x = x_ref[...]
    tm = x.shape[0]
    r = lax.rsqrt(jnp.mean(x * x, axis=-1, keepdims=True) + EPS)
    h = x * r * g_ref[...]
    h = h * (1.0 + mod_ref[0, 1:2, :]) + mod_ref[0, 0:1, :]
    p = jnp.dot(h.astype(jnp.bfloat16), w_ref[...], preferred_element_type=jnp.float32)

    lane = lax.broadcasted_iota(jnp.int32, (tm, LANES), 1)
    first16 = (lane % 32) < 16
    lo64 = lane < HEAD_DIM
    cos = cos_ref[...]
    sin = sin_ref[...]
    gsum = gsum_ref[...]

    def blk(off):
        return p[:, off:off + LANES]

    def rope(t):
        sw = jnp.where(first16, pltpu.roll(t, LANES - 16, axis=1), pltpu.roll(t, 16, axis=1))
        return t * cos + sw * sin

    def norm(t, g):
        return t * lax.rsqrt(_head_mean_sq(t, gsum) + EPS) * g

    def dup(t):
        sw = pltpu.roll(t, HEAD_DIM, axis=1)
        return jnp.where(lo64, t, sw), jnp.where(lo64, sw, t)

    qg = qg_ref[...]
    kg = kg_ref[...]
    for i in range(A_Q // LANES):
        t = rope(norm(blk(OFF_AQ + i * LANES), qg)) * Q_SCALE
        qa_ref[:, i * LANES:(i + 1) * LANES] = t.astype(jnp.bfloat16)
    k0, k1 = dup(rope(norm(blk(OFF_AK), kg)))
    v0, v1 = dup(blk(OFF_AV))
    kva_ref[:, 0:128] = k0.astype(jnp.bfloat16)
    kva_ref[:, 128:256] = k1.astype(jnp.bfloat16)
    kva_ref[:, 256:384] = v0.astype(jnp.bfloat16)
    kva_ref[:, 384:512] = v1.astype(jnp.bfloat16)
    for i in range(B_CH // LANES):
        a = blk(OFF_BU + i * LANES)
        gt = blk(OFF_BU + B_CH + i * LANES)
        hgl_ref[:, i * LANES:(i + 1) * LANES] = a * jax.nn.sigmoid(gt)
    for i in range(C_Q // LANES):
        t = rope(blk(OFF_CQ + i * LANES)) * Q_SCALE
        qc_ref[:, i * LANES:(i + 1) * LANES] = t.astype(jnp.bfloat16)
    k0, k1 = dup(rope(blk(OFF_CK)))
    v0, v1 = dup(blk(OFF_CV))
    kvc_ref[:, 0:128] = k0.astype(jnp.bfloat16)
    kvc_ref[:, 128:256] = k1.astype(jnp.bfloat16)
    kvc_ref[:, 256:384] = v0.astype(jnp.bfloat16)
    kvc_ref[:, 384:512] = v1.astype(jnp.bfloat16)


def _inproj(xall, mods, g1, w_in_bf, gsum, qg, kg, cos_t, sin_t, *, tm, n_lat_tiles, s, nb):
    n, d = xall.shape
    s_tiles = s // tm

    def bidx(i):
        return jnp.where(i < n_lat_tiles, (i * tm) // s, nb)

    def ridx(i):
        return jnp.where(i < n_lat_tiles, i % s_tiles, s_tiles)

    row = lambda w: pl.BlockSpec((tm, w), lambda i: (i, 0))
    const = lambda a: pl.BlockSpec(a.shape, lambda i: (0,) * a.ndim)
    bf = jnp.bfloat16
    return pl.pallas_call(
        _inproj_kernel,
        grid=(n // tm,),
        in_specs=[row(d),
                  pl.BlockSpec((1, N_MOD, d), lambda i: (bidx(i), 0, 0)),
                  const(g1), const(w_in_bf), const(gsum), const(qg), const(kg),
                  pl.BlockSpec((tm, LANES), lambda i: (ridx(i), 0)),
                  pl.BlockSpec((tm, LANES), lambda i: (ridx(i), 0))],
        out_specs=[row(A_Q), row(4 * A_KV), row(B_CH), row(C_Q), row(4 * C_KV)],
        out_shape=[jax.ShapeDtypeStruct((n, A_Q), bf), jax.ShapeDtypeStruct((n, 4 * A_KV), bf),
                   jax.ShapeDtypeStruct((n, B_CH), jnp.float32),
                   jax.ShapeDtypeStruct((n, C_Q), bf), jax.ShapeDtypeStruct((n, 4 * C_KV), bf)],
        compiler_params=_cparams("arbitrary"),
        name="inproj",
    )(xall, mods, g1, w_in_bf, gsum, qg, kg, cos_t, sin_t)


_NT = (((1,), (1,)), ((), ()))


def _stack_heads(q_ref, tq):
    lane = lax.broadcasted_iota(jnp.int32, (tq, LANES), 1)
    lo = lane < HEAD_DIM
    qb = [q_ref[:, i * LANES:(i + 1) * LANES] for i in range(3)]
    zero = jnp.zeros_like(qb[0])
    keep_lo = lambda t: jnp.where(lo, t, zero)
    keep_hi = lambda t: jnp.where(lo, zero, t)
    s0 = jnp.concatenate([keep_lo(qb[0]), keep_hi(qb[0]), keep_lo(qb[1])], axis=0)
    s1 = jnp.concatenate([keep_hi(qb[1]), keep_lo(qb[2]), keep_hi(qb[2])], axis=0)
    return s0, s1, lo


def _unstack_store(o_ref, o0, o1, lo, tq):
    bf = jnp.bfloat16
    o_ref[:, 0:128] = jnp.where(lo, o0[0:tq], o0[tq:2 * tq]).astype(bf)
    o_ref[:, 128:256] = jnp.where(lo, o0[2 * tq:3 * tq], o1[0:tq]).astype(bf)
    o_ref[:, 256:384] = jnp.where(lo, o1[tq:2 * tq], o1[2 * tq:3 * tq]).astype(bf)


def _softmax_pv(scores, values, extra=None):
    m = scores[0].max(axis=-1, keepdims=True)
    for t in scores[1:]:
        m = jnp.maximum(m, t.max(axis=-1, keepdims=True))
    if extra is not None:
        m = jnp.maximum(m, extra)
    den = None if extra is None else jnp.exp2(extra - m)
    o = None
    for t, v in zip(scores, values):
        p = jnp.exp2(t - m)
        ps = p.sum(axis=-1, keepdims=True)
        den = ps if den is None else den + ps
        c = jnp.dot(p.astype(jnp.bfloat16), v, preferred_element_type=jnp.float32)
        o = c if o is None else o + c
    return o / den


def _attn_a_kernel(q_ref, kvl_ref, kvc_ref, o_ref, *, tq):
    s0, s1, lo = _stack_heads(q_ref, tq)
    outs = []
    for kv, qs in enumerate((s0, s1)):
        kcol = slice(kv * LANES, (kv + 1) * LANES)
        vcol = slice((2 + kv) * LANES, (3 + kv) * LANES)
        ss = [lax.dot_general(qs, src[:, kcol], _NT, preferred_element_type=jnp.float32)
              for src in (kvl_ref, kvc_ref)]
        outs.append(_softmax_pv(ss, [kvl_ref[:, vcol], kvc_ref[:, vcol]]))
    _unstack_store(o_ref, outs[0], outs[1], lo, tq)


def _attn_a(qa, kva, *, nb, s, l, tq):
    n = qa.shape[0]
    n_q = s // tq
    lat_rows = nb * s
    return pl.pallas_call(
        functools.partial(_attn_a_kernel, tq=tq),
        grid=(nb, n_q),
        in_specs=[pl.BlockSpec((tq, A_Q), lambda b, j: (b * n_q + j, 0)),
                  pl.BlockSpec((s, 4 * A_KV), lambda b, j: (b, 0)),
                  pl.BlockSpec((l, 4 * A_KV), lambda b, j: (lat_rows // l + b, 0))],
        out_specs=pl.BlockSpec((tq, A_Q), lambda b, j: (b * n_q + j, 0)),
        out_shape=jax.ShapeDtypeStruct((n, A_Q), jnp.bfloat16),
        compiler_params=_cparams("arbitrary", "arbitrary"),
        name="attn_global",
    )(qa, kva, kva)


WIN_BLOCKS = 4


def _sink_column(sink_ref, kv, rows):
    return jnp.concatenate([jnp.full((rows, 1), sink_ref[3 * kv + g], jnp.float32) for g in range(3)], axis=0)


def _attn_c_kernel(sink_ref, bias_ref, q_ref, kvl_ref, kvc_ref, o_ref, *, s, blocks):
    j = pl.program_id(1)
    tq = WINDOW
    band = 3 * WINDOW
    sk = [_sink_column(sink_ref, kv, tq) for kv in range(C_KV_HEADS)]
    for blk in range(blocks):
        jb = j * blocks + blk
        rows = slice(blk * tq, (blk + 1) * tq)
        s0, s1, lo = _stack_heads(q_ref.at[rows, :], tq)
        start = pl.multiple_of(jnp.clip((jb - 1) * WINDOW, 0, s - band), WINDOW)
        bias = bias_ref[jb - start // WINDOW]
        outs = []
        for kv, qs in enumerate((s0, s1)):
            kcol = slice(kv * LANES, (kv + 1) * LANES)
            vcol = slice((2 + kv) * LANES, (3 + kv) * LANES)
            sl = lax.dot_general(qs, kvl_ref[pl.ds(start, band), kcol], _NT,
                                 preferred_element_type=jnp.float32) + bias
            sc = lax.dot_general(qs, kvc_ref[:, kcol], _NT, preferred_element_type=jnp.float32)
            outs.append(_softmax_pv([sl, sc], [kvl_ref[pl.ds(start, band), vcol], kvc_ref[:, vcol]], sk[kv]))
        _unstack_store(o_ref.at[rows, :], outs[0], outs[1], lo, tq)


def _window_bias():
    r = np.arange(3 * WINDOW)[:, None] % WINDOW
    col = np.arange(3 * WINDOW)[None, :]
    tabs = [np.where(np.abs(col - r - WINDOW * off) <= WINDOW, 0.0, -np.inf) for off in range(3)]
    return jnp.asarray(np.stack(tabs), jnp.float32)


def _attn_c(qc, kvc, sink2, *, nb, s, l):
    n = qc.shape[0]
    blocks = _pick(s // WINDOW, (WIN_BLOCKS, 2, 1))
    tq = blocks * WINDOW
    n_q = s // tq
    lat_rows = nb * s
    bias = _window_bias()
    return pl.pallas_call(
        functools.partial(_attn_c_kernel, s=s, blocks=blocks),
        grid=(nb, n_q),
        in_specs=[pl.BlockSpec(memory_space=pltpu.SMEM),
                  pl.BlockSpec(bias.shape, lambda b, j: (0, 0, 0)),
                  pl.BlockSpec((tq, C_Q), lambda b, j: (b * n_q + j, 0)),
                  pl.BlockSpec((s, 4 * C_KV), lambda b, j: (b, 0)),
                  pl.BlockSpec((l, 4 * C_KV), lambda b, j: (lat_rows // l + b, 0))],
        out_specs=pl.BlockSpec((tq, C_Q), lambda b, j: (b * n_q + j, 0)),
        out_shape=jax.ShapeDtypeStruct((n, C_Q), jnp.bfloat16),
        compiler_params=_cparams("arbitrary", "arbitrary"),
        name="attn_window",
    )(sink2, bias, qc, kvc, kvc)


def _attn_ctx_kernel(sink_ref, qa_ref, kva_ref, qc_ref, kvc_ref, oa_in, oc_in, oa_ref, oc_ref, *, l):
    del oa_in, oc_in
    for q_ref, kv_ref, o_ref, with_sink in ((qa_ref, kva_ref, oa_ref, False), (qc_ref, kvc_ref, oc_ref, True)):
        s0, s1, lo = _stack_heads(q_ref, l)
        outs = []
        for kv, qs in enumerate((s0, s1)):
            kcol = slice(kv * LANES, (kv + 1) * LANES)
            vcol = slice((2 + kv) * LANES, (3 + kv) * LANES)
            sc = lax.dot_general(qs, kv_ref[:, kcol], _NT, preferred_element_type=jnp.float32)
            sk = _sink_column(sink_ref, kv, l) if with_sink else None
            outs.append(_softmax_pv([sc], [kv_ref[:, vcol]], sk))
        _unstack_store(o_ref, outs[0], outs[1], lo, l)


def _attn_ctx(qa, kva, qc, kvc, sink2, oa, oc, *, nb, s, l):
    base = nb * s // l
    row = lambda w: pl.BlockSpec((l, w), lambda b: (base + b, 0))
    anyspec = pl.BlockSpec(memory_space=pl.ANY)
    return pl.pallas_call(
        functools.partial(_attn_ctx_kernel, l=l),
        grid=(nb,),
        in_specs=[pl.BlockSpec(memory_space=pltpu.SMEM), row(A_Q), row(4 * A_KV), row(C_Q), row(4 * C_KV),
                  anyspec, anyspec],
        out_specs=[row(A_Q), row(C_Q)],
        out_shape=[jax.ShapeDtypeStruct(oa.shape, oa.dtype), jax.ShapeDtypeStruct(oc.shape, oc.dtype)],
        input_output_aliases={5: 0, 6: 1},
        compiler_params=_cparams("arbitrary"),
        name="attn_context",
    )(sink2, qa, kva, qc, kvc, oa, oc)


CONV_HALO = 16


def _conv_kernel(prev_ref, cur_ref, next_ref, w_ref, b_ref, g_ref, beta_ref, *rest, chunks, tc):
    o_ref, xp_ref = rest[-2:]
    j = pl.program_id(1)
    has_prev = (j > 0).astype(jnp.float32)
    has_next = (j < chunks - 1).astype(jnp.float32)
    xp_ref[0:CONV_HALO, :] = prev_ref[tc - CONV_HALO:tc, :] * has_prev
    xp_ref[CONV_HALO:CONV_HALO + tc, :] = cur_ref[...]
    xp_ref[CONV_HALO + tc:2 * CONV_HALO + tc, :] = next_ref[0:CONV_HALO, :] * has_next
    acc = jnp.zeros((tc, B_CH), jnp.float32)
    base = CONV_HALO - CONV_W // 2
    for k in range(CONV_W):
        acc = acc + xp_ref[base + k:base + k + tc, :] * w_ref[k:k + 1, :]
    hc = acc + b_ref[...]
    mu = jnp.mean(hc, axis=-1, keepdims=True)
    xc = hc - mu
    var = jnp.mean(xc * xc, axis=-1, keepdims=True)
    y = xc * lax.rsqrt(var + EPS) * g_ref[...] + beta_ref[...]
    o_ref[...] = (y * jax.nn.sigmoid(y)).astype(o_ref.dtype)


def _conv(hgl, w, b, g, beta, *, nb, seq, base_rows, tc, prev_out=None):
    n = hgl.shape[0]
    chunks = seq // tc
    base = base_rows // tc

    def idx(b_, j, delta):
        return base + b_ * chunks + jnp.clip(j + delta, 0, chunks - 1)

    blk = lambda delta: pl.BlockSpec((tc, B_CH), lambda b_, j: (idx(b_, j, delta), 0))
    const = lambda a: pl.BlockSpec(a.shape, lambda b_, j: (0,) * a.ndim)
    in_specs = [blk(-1), blk(0), blk(1), const(w), const(b), const(g), const(beta)]
    args = [hgl, hgl, hgl, w, b, g, beta]
    aliases = {}
    if prev_out is not None:
        in_specs.append(pl.BlockSpec(memory_space=pl.ANY))
        args.append(prev_out)
        aliases = {len(args) - 1: 0}
    return pl.pallas_call(
        functools.partial(_conv_kernel, chunks=chunks, tc=tc),
        grid=(nb, chunks),
        in_specs=in_specs,
        out_specs=blk(0),
        out_shape=jax.ShapeDtypeStruct((n, B_CH), jnp.bfloat16),
        scratch_shapes=[pltpu.VMEM((tc + 2 * CONV_HALO, B_CH), jnp.float32)],
        input_output_aliases=aliases,
        compiler_params=_cparams("arbitrary", "arbitrary"),
        name="conformer_conv",
    )(*args)


META_ROWS = 8


def _outproj_kernel(oa_ref, ob_ref, oc_ref, x_ref, mod_ref, g_ref, w_ref, wr_ref, br_ref, tri_ref,
                    xo_ref, hp_ref, meta_ref, rw_ref, cnt_ref):
    i = pl.program_id(0)
    tm = x_ref.shape[0]
    f32 = jnp.float32
    mix = (jnp.dot(oa_ref[...], w_ref[0:A_Q, :], preferred_element_type=f32)
           + jnp.dot(ob_ref[...], w_ref[A_Q:A_Q + B_CH, :], preferred_element_type=f32)
           + jnp.dot(oc_ref[...], w_ref[A_Q + B_CH:D_MIX, :], preferred_element_type=f32))
    x = x_ref[...] + mod_ref[0, 2:3, :] * mix
    xo_ref[...] = x
    r = lax.rsqrt(jnp.mean(x * x, axis=-1, keepdims=True) + EPS)
    h = x * r * g_ref[...]
    h = h * (1.0 + mod_ref[0, 4:5, :]) + mod_ref[0, 3:4, :]
    h_hi = h.astype(jnp.bfloat16)
    hp_ref[...] = _pack_pairs(h_hi)

    h_lo = (h - h_hi.astype(f32)).astype(jnp.bfloat16)
    r_hi = jnp.dot(h_hi, wr_ref[...], preferred_element_type=f32)
    r_lo = jnp.dot(h_lo, wr_ref[:, 0:LANES], preferred_element_type=f32)
    lg = r_hi[:, 0:LANES] + r_hi[:, LANES:2 * LANES] + r_lo + br_ref[...]
    lane = lax.broadcasted_iota(jnp.int32, (tm, LANES), 1).astype(f32)
    big = float(LANES)
    ninf = -jnp.inf
    glog = jnp.where(lane < N_GROUPS, lg, ninf)
    gmax = glog.max(axis=-1, keepdims=True)
    g_val = 1.0 / jnp.exp(glog - gmax).sum(axis=-1, keepdims=True)
    g_idx = jnp.where(glog == gmax, lane, big).min(axis=-1, keepdims=True)
    e_lo = N_GROUPS + EXPERTS_PER_GROUP * g_idx
    el = jnp.where((lane >= e_lo) & (lane < e_lo + EXPERTS_PER_GROUP), lg, ninf)
    v0 = el.max(axis=-1, keepdims=True)
    i0 = jnp.where(el == v0, lane, big).min(axis=-1, keepdims=True)
    el1 = jnp.where(lane == i0, ninf, el)
    v1 = el1.max(axis=-1, keepdims=True)
    i1 = jnp.where(el1 == v1, lane, big).min(axis=-1, keepdims=True)
    t = jnp.exp(v1 - v0)
    w0 = g_val / (1.0 + t)
    w1 = g_val * t / (1.0 + t)
    e0 = i0 - N_GROUPS
    e1 = i1 - N_GROUPS

    @pl.when(i == 0)
    def _():
        cnt_ref[...] = jnp.zeros_like(cnt_ref)

    tri = tri_ref[...]
    ranks = []
    for e in (e0, e1):
        oh = lane == e
        ohf = oh.astype(f32)
        pre = jnp.dot(tri, ohf.astype(jnp.bfloat16), preferred_element_type=f32) + cnt_ref[0:1, :]
        ranks.append(jnp.where(oh, pre, 0.0).sum(axis=-1, keepdims=True))
        cnt_ref[0:1, :] = cnt_ref[0:1, :] + ohf.sum(axis=0, keepdims=True)
    rw = jnp.where(lane == 0, w0, jnp.where(lane == 1, w1, 0.0))
    rw_ref[...] = rw
    rec = jnp.where(lane == 0, e0, jnp.where(lane == 1, e1, jnp.where(lane == 2, ranks[0],
                    jnp.where(lane == 3, ranks[1], jnp.where(lane == 4, w0, jnp.where(lane == 5, w1, 0.0))))))
    meta_ref[...] = rec.T[0:META_ROWS, :]


def _outproj(oa, ob, oc, xall, mods, g2, w_out_bf, wr, br, tri, *, tm, n_tiles, n_lat_tiles, s, nb):
    n, d = xall.shape
    rows = n_tiles * tm

    def bidx(i):
        return jnp.where(i < n_lat_tiles, (i * tm) // s, nb)

    row = lambda w: pl.BlockSpec((tm, w), lambda i: (i, 0))
    const = lambda a: pl.BlockSpec(a.shape, lambda i: (0,) * a.ndim)
    return pl.pallas_call(
        _outproj_kernel,
        grid=(n_tiles,),
        in_specs=[row(A_Q), row(B_CH), row(C_Q), row(d),
                  pl.BlockSpec((1, N_MOD, d), lambda i: (bidx(i), 0, 0)),
                  const(g2), const(w_out_bf), const(wr), const(br), const(tri)],
        out_specs=[row(d), row(d // 2), pl.BlockSpec((META_ROWS, tm), lambda i: (0, i)), row(LANES),
                   pl.BlockSpec((8, LANES), lambda i: (0, 0))],
        out_shape=[jax.ShapeDtypeStruct((n, d), jnp.float32),
                   jax.ShapeDtypeStruct((rows, d // 2), jnp.int32),
                   jax.ShapeDtypeStruct((META_ROWS, rows), jnp.float32),
                   jax.ShapeDtypeStruct((rows, LANES), jnp.float32),
                   jax.ShapeDtypeStruct((8, LANES), jnp.float32)],
        input_output_aliases={3: 0},
        compiler_params=_cparams("arbitrary"),
        name="outproj_router",
    )(oa, ob, oc, xall, mods, g2, w_out_bf, wr, br, tri)


def _sc_mesh():
    return plsc.VectorSubcoreMesh(core_axis_name="core", subcore_axis_name="subcore")


def sc_gather_rows(table, idx):
    r = idx.shape[0]
    w = table.shape[1]
    half = r // SC_WINDOW // 2
    idx2 = idx.reshape(1, r)

    @functools.partial(pl.kernel, out_type=jax.ShapeDtypeStruct((r, w), table.dtype), mesh=_sc_mesh())
    def k(x_hbm, i_hbm, o_hbm):
        def body(i_vmem, o_vmem):
            pltpu.sync_copy(x_hbm.at[i_vmem.at[0]], o_vmem)

        pltpu.emit_pipeline(
            body,
            grid=(2, half),
            in_specs=[pl.BlockSpec((1, SC_WINDOW), lambda c, i: (0, c * half + i))],
            out_specs=[pl.BlockSpec((SC_WINDOW, w), lambda c, i: (c * half + i, 0),
                                    pipeline_mode=pl.Buffered(1))],
            core_axis_name=("core", "subcore"),
            dimension_semantics=(pltpu.PARALLEL, pltpu.PARALLEL),
        )(i_hbm, o_hbm)

    return k(table, idx2)


def sc_scatter_rows2(rows, idx_a, idx_b, n_out):
    r, w = rows.shape
    half = r // SC_WINDOW // 2
    ia = idx_a.reshape(1, r)
    ib = idx_b.reshape(1, r)

    @functools.partial(pl.kernel, out_type=jax.ShapeDtypeStruct((n_out, w), rows.dtype), mesh=_sc_mesh(),
                       scratch_types=[])
    def k(x_hbm, ia_hbm, ib_hbm, o_hbm):
        def body(x_vmem, ia_vmem, ib_vmem):
            pltpu.sync_copy(x_vmem, o_hbm.at[ia_vmem.at[0]])
            pltpu.sync_copy(x_vmem, o_hbm.at[ib_vmem.at[0]])

        idx_spec = pl.BlockSpec((1, SC_WINDOW), lambda c, i: (0, c * half + i))
        pltpu.emit_pipeline(
            body,
            grid=(2, half),
            in_specs=[pl.BlockSpec((SC_WINDOW, w), lambda c, i: (c * half + i, 0),
                                   pipeline_mode=pl.Buffered(1)),
                      idx_spec, idx_spec],
            out_specs=[],
            core_axis_name=("core", "subcore"),
            dimension_semantics=(pltpu.PARALLEL, pltpu.PARALLEL),
        )(x_hbm, ia_hbm, ib_hbm)

    return k(rows, ia, ib)


def _expert_kernel(te_ref, nv_ref, x_ref, wg_ref, wu_ref, wd_ref, o_ref):
    t = pl.program_id(0)
    nvalid = nv_ref[t]

    @pl.when(nvalid > 0)
    def _():
        rows = lax.broadcasted_iota(jnp.int32, x_ref.shape, 0)
        lo, hi = _unpack_pairs(jnp.where(rows < nvalid, x_ref[...], 0))
        xb = jnp.concatenate([lo, hi], axis=1).astype(jnp.bfloat16)
        g = jnp.dot(xb, wg_ref[0], preferred_element_type=jnp.float32)
        u = jnp.dot(xb, wu_ref[0], preferred_element_type=jnp.float32)
        a = (g * jax.nn.sigmoid(g) * u).astype(jnp.bfloat16)
        o_ref[...] = _pack_pairs(jnp.dot(a, wd_ref[0], preferred_element_type=jnp.float32))

    @pl.when(nvalid == 0)
    def _():
        o_ref[...] = jnp.zeros_like(o_ref)


def _experts(buf, tile_expert, tile_nvalid, wg, wu, wd):
    rows, wp = buf.shape
    _, d, f = wg.shape
    n_tiles = rows // EXPERT_TILE
    grid_spec = pltpu.PrefetchScalarGridSpec(
        num_scalar_prefetch=2,
        grid=(n_tiles,),
        in_specs=[pl.BlockSpec((EXPERT_TILE, wp), lambda t, te, nv: (t, 0)),
                  pl.BlockSpec((1, d, f), lambda t, te, nv: (te[t], 0, 0)),
                  pl.BlockSpec((1, d, f), lambda t, te, nv: (te[t], 0, 0)),
                  pl.BlockSpec((1, f, d), lambda t, te, nv: (te[t], 0, 0))],
        out_specs=pl.BlockSpec((EXPERT_TILE, wp), lambda t, te, nv: (t, 0)),
    )
    return pl.pallas_call(
        _expert_kernel,
        grid_spec=grid_spec,
        out_shape=jax.ShapeDtypeStruct((rows, wp), jnp.int32),
        compiler_params=_cparams("arbitrary"),
        name="expert_ffn",
    )(tile_expert, tile_nvalid, buf, wg, wu, wd)


def _combine_kernel(y0_ref, y1_ref, rw_ref, x_ref, mod_ref, fg_ref, o_ref, *, final):
    rw = rw_ref[...]
    w0 = rw[:, 0:1]
    w1 = rw[:, 1:2]
    a_lo, a_hi = _unpack_pairs(y0_ref[...])
    b_lo, b_hi = _unpack_pairs(y1_ref[...])
    y = jnp.concatenate([a_lo * w0 + b_lo * w1, a_hi * w0 + b_hi * w1], axis=1)
    x = x_ref[...] + mod_ref[0, 5:6, :] * y
    if final:
        r = lax.rsqrt(jnp.mean(x * x, axis=-1, keepdims=True) + EPS)
        x = x * r * fg_ref[...]
    o_ref[...] = x


def _combine(y, rw, xall, mods, final_g, *, tm, n_tiles, n_lat_tiles, s, nb, final):
    n, d = xall.shape

    def bidx(i):
        return jnp.where(i < n_lat_tiles, (i * tm) // s, nb)

    row = lambda w: pl.BlockSpec((tm, w), lambda i: (i, 0))
    out_rows = n_tiles * tm if final else n
    return pl.pallas_call(
        functools.partial(_combine_kernel, final=final),
        grid=(n_tiles,),
        in_specs=[row(d // 2),
                  pl.BlockSpec((tm, d // 2), lambda i: (i + n_tiles, 0)),
                  row(LANES), row(d),
                  pl.BlockSpec((1, N_MOD, d), lambda i: (bidx(i), 0, 0)),
                  pl.BlockSpec(final_g.shape, lambda i: (0, 0))],
        out_specs=row(d),
        out_shape=jax.ShapeDtypeStruct((out_rows, d), jnp.float32),
        input_output_aliases={} if final else {3: 0},
        compiler_params=_cparams("arbitrary"),
        name="moe_combine",
    )(y, y, rw, xall, mods, final_g)


def _dispatch_plan(meta, counts, n_rows_buf):
    e0, e1, r0, r1 = (meta[k].astype(jnp.int32) for k in range(4))
    cnt = counts[0, :N_EXPERTS].astype(jnp.int32)
    padded = (cnt + EXPERT_TILE - 1) // EXPERT_TILE * EXPERT_TILE
    pends = jnp.cumsum(padded)
    pstarts = pends - padded
    dest0 = pstarts[e0] + r0
    dest1 = pstarts[e1] + r1
    tile_start = jnp.arange(n_rows_buf // EXPERT_TILE, dtype=jnp.int32) * EXPERT_TILE
    te = jnp.sum((tile_start[:, None] >= pends[None, :]).astype(jnp.int32), axis=1)
    te = jnp.minimum(te, N_EXPERTS - 1)
    nvalid = jnp.clip(cnt[te] - (tile_start - pstarts[te]), 0, EXPERT_TILE).astype(jnp.int32)
    return dest0, dest1, te, nvalid


def _rope_tables(s, tm):
    pos = np.arange(s)
    pos_row = jnp.asarray(pos // GRID_W, jnp.float32)
    pos_col = jnp.asarray(pos % GRID_W, jnp.float32)
    n_freq = HEAD_DIM // 4
    inv = ROPE_THETA ** (-jnp.arange(n_freq, dtype=jnp.float32) / n_freq)
    ang_row = pos_row[:, None] * inv
    ang_col = pos_col[:, None] * inv
    ang = jnp.concatenate([ang_row, ang_row, ang_col, ang_col] * (LANES // HEAD_DIM), axis=-1)
    sign = np.where((np.arange(LANES) % 32) < 16, -1.0, 1.0).astype(np.float32)
    cos_t = jnp.concatenate([jnp.cos(ang), jnp.ones((tm, LANES), jnp.float32)], axis=0)
    sin_t = jnp.concatenate([jnp.sin(ang) * sign, jnp.zeros((tm, LANES), jnp.float32)], axis=0)
    return cos_t, sin_t


def kernel(x, c, ctx, c_ctx, norm1_g, norm2_g, w_mod, b_mod, w_in, q_norm_g, k_norm_g, conv_w, conv_b, conv_ln_g, conv_ln_b, sink, w_out, w_group, b_group, w_expert, b_expert, w_gate, w_up, w_down, final_g):
    nb, s, d = x.shape
    l = ctx.shape[1]
    depth = w_in.shape[0]
    assert w_in.shape[2] == D_IN and w_out.shape[1] == D_MIX
    assert s % GRID_W == 0 and s >= 3 * WINDOW and s % WINDOW == 0 and l % WINDOW == 0
    n_lat, n_ctx = nb * s, nb * l
    tm = _pick(np.gcd(s, n_ctx), (512, 256, 128))
    tq = _pick(np.gcd(s, l), (256, 128))
    tc = _pick(np.gcd(s, l), (256, 128))
    assert n_lat % l == 0
    bf = jnp.bfloat16
    f32 = jnp.float32

    xall = jnp.concatenate([x.reshape(n_lat, d), ctx.reshape(n_ctx, d)], axis=0)
    c_all = jnp.concatenate([c, c_ctx[None, :]], axis=0)
    mods_all = _modulation(c_all, w_mod, b_mod).reshape(depth, nb + 1, N_MOD, d)
    cos_t, sin_t = _rope_tables(s, tm)
    head_id = np.arange(LANES) // HEAD_DIM
    gsum = jnp.asarray((head_id[:, None] == head_id[None, :]) / HEAD_DIM, bf)
    tri = jnp.asarray(np.tril(np.ones((tm, tm), np.float32), -1), bf)
    n_lat_tiles = n_lat // tm

    out = None
    for i in range(depth):
        last = i == depth - 1
        with_ctx = not last
        mods = mods_all[i]
        qg = jnp.tile(q_norm_g[i], LANES // HEAD_DIM)[None, :]
        kg = jnp.tile(k_norm_g[i], LANES // HEAD_DIM)[None, :]
        qa, kva, hgl, qc, kvc = _inproj(xall, mods, norm1_g[i][None, :], w_in[i].astype(bf), gsum, qg, kg,
                                        cos_t, sin_t, tm=tm, n_lat_tiles=n_lat_tiles, s=s, nb=nb)
        sink2 = sink[i] * LOG2E
        conv_args = (conv_w[i].reshape(CONV_W, B_CH), conv_b[i][None, :], conv_ln_g[i][None, :],
                     conv_ln_b[i][None, :])
        oa = _attn_a(qa, kva, nb=nb, s=s, l=l, tq=tq)
        oc = _attn_c(qc, kvc, sink2, nb=nb, s=s, l=l)
        ob = _conv(hgl, *conv_args, nb=nb, seq=s, base_rows=0, tc=tc)
        if with_ctx:
            oa, oc = _attn_ctx(qa, kva, qc, kvc, sink2, oa, oc, nb=nb, s=s, l=l)
            ob = _conv(hgl, *conv_args, nb=nb, seq=l, base_rows=n_lat, tc=tc, prev_out=ob)

        n_tok = n_lat + n_ctx if with_ctx else n_lat
        n_tiles = n_tok // tm
        wr32 = jnp.zeros((d, LANES), f32).at[:, :N_GROUPS].set(w_group[i])
        wr32 = wr32.at[:, N_GROUPS:N_GROUPS + N_EXPERTS].set(w_expert[i])
        wr_hi = wr32.astype(bf)
        wr = jnp.concatenate([wr_hi, (wr32 - wr_hi.astype(f32)).astype(bf)], axis=1)
        br = jnp.zeros((1, LANES), f32).at[0, :N_GROUPS].set(b_group[i])
        br = br.at[0, N_GROUPS:N_GROUPS + N_EXPERTS].set(b_expert[i])
        xall, hp, meta, rw, counts = _outproj(oa, ob, oc, xall, mods, norm2_g[i][None, :], w_out[i].astype(bf),
                                              wr, br, tri, tm=tm, n_tiles=n_tiles, n_lat_tiles=n_lat_tiles,
                                              s=s, nb=nb)
        n_rows_buf = 2 * n_tok + N_EXPERTS * EXPERT_TILE
        dest0, dest1, te, nvalid = _dispatch_plan(meta, counts, n_rows_buf)
        buf = sc_scatter_rows2(hp, dest0, dest1, n_rows_buf)
        eo = _experts(buf, te, nvalid, w_gate[i].astype(bf), w_up[i].astype(bf), w_down[i].astype(bf))
        y = sc_gather_rows(eo, jnp.concatenate([dest0, dest1]))
        res = _combine(y, rw, xall, mods, final_g[None, :], tm=tm, n_tiles=n_tiles,
                       n_lat_tiles=n_lat_tiles, s=s, nb=nb, final=last)
        if last:
            out = res
        else:
            xall = res
    return out.reshape(nb, s, d)
```

```python
import functools

import jax
import jax.numpy as jnp
import numpy as np
from jax import lax
from jax.experimental import pallas as pl
from jax.experimental.pallas import tpu as pltpu
from jax.experimental.pallas import tpu_sc as plsc

HEAD_DIM = 64
GRID_W = 64
ROPE_THETA = 10000.0
A_HEADS, A_KV_HEADS = 6, 2
C_HEADS, C_KV_HEADS = 6, 2
B_CH = 256
CONV_W = 31
WINDOW = 128
N_GROUPS = 4
EXPERTS_PER_GROUP = 8
N_EXPERTS = N_GROUPS * EXPERTS_PER_GROUP
N_MOD = 6
EPS = 1e-6
ATTN_SCALE = HEAD_DIM ** -0.5
LOG2E = 1.4426950408889634
Q_SCALE = ATTN_SCALE * LOG2E

A_Q = A_HEADS * HEAD_DIM
A_KV = A_KV_HEADS * HEAD_DIM
C_Q = C_HEADS * HEAD_DIM
C_KV = C_KV_HEADS * HEAD_DIM
D_MIX = A_Q + B_CH + C_Q
OFF_AQ = 0
OFF_AK = OFF_AQ + A_Q
OFF_AV = OFF_AK + A_KV
OFF_BU = OFF_AV + A_KV
OFF_CQ = OFF_BU + 2 * B_CH
OFF_CK = OFF_CQ + C_Q
OFF_CV = OFF_CK + C_KV
D_IN = OFF_CV + C_KV

LANES = 128
KT_ROWS = 2 * LANES
V_W = 4 * LANES
EXPERT_TILE = 256
SC_WINDOW = 128
VMEM_LIMIT = 56 * 1024 * 1024
HI_MASK = -65536


def _cparams(*sem):
    return pltpu.CompilerParams(dimension_semantics=sem, vmem_limit_bytes=VMEM_LIMIT)


def _pick(n, cands):
    for c in cands:
        if n % c == 0:
            return c
    raise ValueError(f"no tile in {cands} divides {n}")


def _pack_pairs(x):
    w = x.shape[1] // 2
    lo = lax.bitcast_convert_type(x[:, :w].astype(jnp.bfloat16).astype(jnp.float32), jnp.int32)
    hi = lax.bitcast_convert_type(x[:, w:].astype(jnp.bfloat16).astype(jnp.float32), jnp.int32)
    return (hi & HI_MASK) | lax.shift_right_logical(lo, 16)


def _unpack_pairs(p):
    lo = lax.bitcast_convert_type(lax.shift_left(p, 16), jnp.float32)
    hi = lax.bitcast_convert_type(p & HI_MASK, jnp.float32)
    return lo, hi


def _mod_kernel(c_ref, w_ref, b_ref, o_ref):
    c = c_ref[...]
    a = c * jax.nn.sigmoid(c)
    o_ref[0] = jnp.dot(a, w_ref[0], preferred_element_type=jnp.float32,
                       precision=lax.Precision.HIGHEST) + b_ref[0]


def _modulation(c_all, w_mod, b_mod):
    depth, d, n = w_mod.shape
    r = c_all.shape[0]
    tn = _pick(n, (1024, 512, 256, 128))
    return pl.pallas_call(
        _mod_kernel,
        grid=(depth, n // tn),
        in_specs=[pl.BlockSpec((r, d), lambda l, j: (0, 0)),
                  pl.BlockSpec((1, d, tn), lambda l, j: (l, 0, j)),
                  pl.BlockSpec((1, 1, tn), lambda l, j: (l, 0, j))],
        out_specs=pl.BlockSpec((1, r, tn), lambda l, j: (l, 0, j)),
        out_shape=jax.ShapeDtypeStruct((depth, r, n), jnp.float32),
        compiler_params=_cparams("arbitrary", "arbitrary"),
        name="modulation",
    )(c_all, w_mod, b_mod.reshape(depth, 1, n))


def _head_mean_sq(blk, gsum):
    sq = blk * blk
    hi = sq.astype(jnp.bfloat16)
    lo = (sq - hi.astype(jnp.float32)).astype(jnp.bfloat16)
    return (jnp.dot(hi, gsum, preferred_element_type=jnp.float32)
            + jnp.dot(lo, gsum, preferred_element_type=jnp.float32))


def _inproj_kernel(x_ref, mod_ref, g_ref, w_ref, gsum_ref, qg_ref, kg_ref, cos_ref, sin_ref,
                   qa_ref, kta_ref, va_ref, hgl_ref, qc_ref, ktc_ref, vc_ref):
    x = x_ref[...]
    tm = x.shape[0]
    r = lax.rsqrt(jnp.mean(x * x, axis=-1, keepdims=True) + EPS)
    h = x * r * g_ref[...]
    h = h * (1.0 + mod_ref[0, 1:2, :]) + mod_ref[0, 0:1, :]
    p = jnp.dot(h.astype(jnp.bfloat16), w_ref[...], preferred_element_type=jnp.float32)

    lane = lax.broadcasted_iota(jnp.int32, (tm, LANES), 1)
    first16 = (lane % 32) < 16
    lo64 = lane < HEAD_DIM
    cos = cos_ref[...]
    sin = sin_ref[...]
    gsum = gsum_ref[...]

    def blk(off):
        return p[:, off:off + LANES]

    def rope(t):
        sw = jnp.where(first16, pltpu.roll(t, LANES - 16, axis=1), pltpu.roll(t, 16, axis=1))
        return t * cos + sw * sin

    def norm(t, g):
        return t * lax.rsqrt(_head_mean_sq(t, gsum) + EPS) * g

    def dup(t):
        sw = pltpu.roll(t, HEAD_DIM, axis=1)
        return jnp.where(lo64, t, sw), jnp.where(lo64, sw, t)

    qg = qg_ref[...]
    kg = kg_ref[...]
    for i in range(A_Q // LANES):
        t = rope(norm(blk(OFF_AQ + i * LANES), qg)) * Q_SCALE
        qa_ref[:, i * LANES:(i + 1) * LANES] = t.astype(jnp.bfloat16)
    def store_kv(kt_ref, v_ref, k_blk, v_blk):
        bf = jnp.bfloat16
        ones = jnp.ones((tm, LANES), bf)
        k0, k1 = dup(k_blk)
        v0, v1 = dup(v_blk)
        kt_ref[0:128, :] = k0.T.astype(bf)
        kt_ref[128:256, :] = k1.T.astype(bf)
        v_ref[:, 0:128] = v0.astype(bf)
        v_ref[:, 128:256] = ones
        v_ref[:, 256:384] = v1.astype(bf)
        v_ref[:, 384:512] = ones

    store_kv(kta_ref, va_ref, rope(norm(blk(OFF_AK), kg)), blk(OFF_AV))
    for i in range(B_CH // LANES):
        a = blk(OFF_BU + i * LANES)
        gt = blk(OFF_BU + B_CH + i * LANES)
        hgl_ref[:, i * LANES:(i + 1) * LANES] = a * jax.nn.sigmoid(gt)
    for i in range(C_Q // LANES):
        t = rope(blk(OFF_CQ + i * LANES)) * Q_SCALE
        qc_ref[:, i * LANES:(i + 1) * LANES] = t.astype(jnp.bfloat16)
    store_kv(ktc_ref, vc_ref, rope(blk(OFF_CK)), blk(OFF_CV))


def _inproj(xall, mods, g1, w_in_bf, gsum, qg, kg, cos_t, sin_t, *, tm, n_lat_tiles, s, nb):
    n, d = xall.shape
    s_tiles = s // tm

    def bidx(i):
        return jnp.where(i < n_lat_tiles, (i * tm) // s, nb)

    def ridx(i):
        return jnp.where(i < n_lat_tiles, i % s_tiles, s_tiles)

    row = lambda w: pl.BlockSpec((tm, w), lambda i: (i, 0))
    ktspec = pl.BlockSpec((KT_ROWS, tm), lambda i: (0, i))
    const = lambda a: pl.BlockSpec(a.shape, lambda i: (0,) * a.ndim)
    bf = jnp.bfloat16
    return pl.pallas_call(
        _inproj_kernel,
        grid=(n // tm,),
        in_specs=[row(d),
                  pl.BlockSpec((1, N_MOD, d), lambda i: (bidx(i), 0, 0)),
                  const(g1), const(w_in_bf), const(gsum), const(qg), const(kg),
                  pl.BlockSpec((tm, LANES), lambda i: (ridx(i), 0)),
                  pl.BlockSpec((tm, LANES), lambda i: (ridx(i), 0))],
        out_specs=[row(A_Q), ktspec, row(V_W), row(B_CH), row(C_Q), ktspec, row(V_W)],
        out_shape=[jax.ShapeDtypeStruct((n, A_Q), bf), jax.ShapeDtypeStruct((KT_ROWS, n), bf),
                   jax.ShapeDtypeStruct((n, V_W), bf), jax.ShapeDtypeStruct((n, B_CH), jnp.float32),
                   jax.ShapeDtypeStruct((n, C_Q), bf), jax.ShapeDtypeStruct((KT_ROWS, n), bf),
                   jax.ShapeDtypeStruct((n, V_W), bf)],
        compiler_params=_cparams("arbitrary"),
        name="inproj",
    )(xall, mods, g1, w_in_bf, gsum, qg, kg, cos_t, sin_t)


def _stack_heads(q_ref, tq):
    lane = lax.broadcasted_iota(jnp.int32, (tq, LANES), 1)
    lo = lane < HEAD_DIM
    qb = [q_ref[:, i * LANES:(i + 1) * LANES] for i in range(3)]
    zero = jnp.zeros_like(qb[0])
    keep_lo = lambda t: jnp.where(lo, t, zero)
    keep_hi = lambda t: jnp.where(lo, zero, t)
    s0 = jnp.concatenate([keep_lo(qb[0]), keep_hi(qb[0]), keep_lo(qb[1])], axis=0)
    s1 = jnp.concatenate([keep_hi(qb[1]), keep_lo(qb[2]), keep_hi(qb[2])], axis=0)
    return s0, s1, lo


def _unstack_store(o_ref, o0, o1, lo, tq):
    bf = jnp.bfloat16
    o_ref[:, 0:128] = jnp.where(lo, o0[0:tq], o0[tq:2 * tq]).astype(bf)
    o_ref[:, 128:256] = jnp.where(lo, o0[2 * tq:3 * tq], o1[0:tq]).astype(bf)
    o_ref[:, 256:384] = jnp.where(lo, o1[tq:2 * tq], o1[2 * tq:3 * tq]).astype(bf)


def _krow(kv):
    return slice(kv * LANES, (kv + 1) * LANES)


def _vcol(kv):
    return slice(2 * kv * LANES, (2 * kv + 2) * LANES)


def _softmax_pv(scores, values, extra=None):
    m = None
    for t in scores:
        for c in range(0, t.shape[1], LANES):
            blk = t[:, c:c + LANES]
            m = blk if m is None else jnp.maximum(m, blk)
    m = m.max(axis=-1, keepdims=True)
    if extra is not None:
        m = jnp.maximum(m, extra)
    acc = None
    for t, v in zip(scores, values):
        c = jnp.dot(jnp.exp2(t - m).astype(jnp.bfloat16), v, preferred_element_type=jnp.float32)
        acc = c if acc is None else acc + c
    den = acc[:, LANES:2 * LANES]
    if extra is not None:
        den = den + jnp.exp2(extra - m)
    return acc[:, 0:LANES] / den


ATTN_CHAIN_ROWS = 128


def _qk(q, kt):
    return jnp.dot(q, kt, preferred_element_type=jnp.float32)


def _attn_a_kernel(q_ref, ktl_ref, ktc_ref, vl_ref, vc_ref, o_ref, *, tq):
    s0, s1, lo = _stack_heads(q_ref, tq)
    outs = []
    for kv, qs in enumerate((s0, s1)):
        krow = _krow(kv)
        vcol = _vcol(kv)
        parts = []
        for r0 in range(0, 3 * tq, ATTN_CHAIN_ROWS):
            qr = qs[r0:r0 + ATTN_CHAIN_ROWS]
            ss = [_qk(qr, ktl_ref[krow, :]), _qk(qr, ktc_ref[krow, :])]
            parts.append(_softmax_pv(ss, [vl_ref[:, vcol], vc_ref[:, vcol]]))
        outs.append(jnp.concatenate(parts, axis=0))
    _unstack_store(o_ref, outs[0], outs[1], lo, tq)


def _kv_specs(nb, s, l):
    ctx0 = nb * s // l
    return [pl.BlockSpec((KT_ROWS, s), lambda b, j: (0, b)),
            pl.BlockSpec((KT_ROWS, l), lambda b, j: (0, ctx0 + b)),
            pl.BlockSpec((s, V_W), lambda b, j: (b, 0)),
            pl.BlockSpec((l, V_W), lambda b, j: (ctx0 + b, 0))]


def _attn_a(qa, kt, v, *, nb, s, l, tq):
    n = qa.shape[0]
    n_q = s // tq
    return pl.pallas_call(
        functools.partial(_attn_a_kernel, tq=tq),
        grid=(nb, n_q),
        in_specs=[pl.BlockSpec((tq, A_Q), lambda b, j: (b * n_q + j, 0))] + _kv_specs(nb, s, l),
        out_specs=pl.BlockSpec((tq, A_Q), lambda b, j: (b * n_q + j, 0)),
        out_shape=jax.ShapeDtypeStruct((n, A_Q), jnp.bfloat16),
        compiler_params=_cparams("arbitrary", "arbitrary"),
        name="attn_global",
    )(qa, kt, kt, v, v)


WIN_BLOCKS = 4


def _sink_column(sink_ref, kv, rows):
    return jnp.concatenate([jnp.full((rows, 1), sink_ref[3 * kv + g], jnp.float32) for g in range(3)], axis=0)


def _attn_c_kernel(sink_ref, bias_ref, q_ref, ktl_ref, ktc_ref, vl_ref, vc_ref, o_ref, *, s, blocks):
    j = pl.program_id(1)
    tq = WINDOW
    band = 3 * WINDOW
    sk = [_sink_column(sink_ref, kv, tq) for kv in range(C_KV_HEADS)]
    for blk in range(blocks):
        jb = j * blocks + blk
        rows = slice(blk * tq, (blk + 1) * tq)
        s0, s1, lo = _stack_heads(q_ref.at[rows, :], tq)
        start = pl.multiple_of(jnp.clip((jb - 1) * WINDOW, 0, s - band), WINDOW)
        bias = bias_ref[jb - start // WINDOW]
        outs = []
        for kv, qs in enumerate((s0, s1)):
            krow = _krow(kv)
            vcol = _vcol(kv)
            sl = _qk(qs, ktl_ref[krow, pl.ds(start, band)]) + bias
            sc = _qk(qs, ktc_ref[krow, :])
            outs.append(_softmax_pv([sl, sc], [vl_ref[pl.ds(start, band), vcol], vc_ref[:, vcol]], sk[kv]))
        _unstack_store(o_ref.at[rows, :], outs[0], outs[1], lo, tq)


def _window_bias():
    r = np.arange(3 * WINDOW)[:, None] % WINDOW
    col = np.arange(3 * WINDOW)[None, :]
    tabs = [np.where(np.abs(col - r - WINDOW * off) <= WINDOW, 0.0, -np.inf) for off in range(3)]
    return jnp.asarray(np.stack(tabs), jnp.float32)


def _attn_c(qc, kt, v, sink2, *, nb, s, l):
    n = qc.shape[0]
    blocks = _pick(s // WINDOW, (WIN_BLOCKS, 2, 1))
    tq = blocks * WINDOW
    n_q = s // tq
    bias = _window_bias()
    return pl.pallas_call(
        functools.partial(_attn_c_kernel, s=s, blocks=blocks),
        grid=(nb, n_q),
        in_specs=[pl.BlockSpec(memory_space=pltpu.SMEM),
                  pl.BlockSpec(bias.shape, lambda b, j: (0, 0, 0)),
                  pl.BlockSpec((tq, C_Q), lambda b, j: (b * n_q + j, 0))] + _kv_specs(nb, s, l),
        out_specs=pl.BlockSpec((tq, C_Q), lambda b, j: (b * n_q + j, 0)),
        out_shape=jax.ShapeDtypeStruct((n, C_Q), jnp.bfloat16),
        compiler_params=_cparams("arbitrary", "arbitrary"),
        name="attn_window",
    )(sink2, bias, qc, kt, kt, v, v)


def _attn_ctx_kernel(sink_ref, qa_ref, kta_ref, va_ref, qc_ref, ktc_ref, vc_ref, oa_in, oc_in, oa_ref, oc_ref, *, l):
    del oa_in, oc_in
    for q_ref, kt_ref, v_ref, o_ref, with_sink in ((qa_ref, kta_ref, va_ref, oa_ref, False),
                                                   (qc_ref, ktc_ref, vc_ref, oc_ref, True)):
        s0, s1, lo = _stack_heads(q_ref, l)
        outs = []
        for kv, qs in enumerate((s0, s1)):
            sk = _sink_column(sink_ref, kv, l) if with_sink else None
            outs.append(_softmax_pv([_qk(qs, kt_ref[_krow(kv), :])], [v_ref[:, _vcol(kv)]], sk))
        _unstack_store(o_ref, outs[0], outs[1], lo, l)


def _attn_ctx(qa, kta, va, qc, ktc, vc, sink2, oa, oc, *, nb, s, l):
    base = nb * s // l
    row = lambda w: pl.BlockSpec((l, w), lambda b: (base + b, 0))
    ktspec = pl.BlockSpec((KT_ROWS, l), lambda b: (0, base + b))
    anyspec = pl.BlockSpec(memory_space=pl.ANY)
    return pl.pallas_call(
        functools.partial(_attn_ctx_kernel, l=l),
        grid=(nb,),
        in_specs=[pl.BlockSpec(memory_space=pltpu.SMEM), row(A_Q), ktspec, row(V_W), row(C_Q), ktspec, row(V_W),
                  anyspec, anyspec],
        out_specs=[row(A_Q), row(C_Q)],
        out_shape=[jax.ShapeDtypeStruct(oa.shape, oa.dtype), jax.ShapeDtypeStruct(oc.shape, oc.dtype)],
        input_output_aliases={7: 0, 8: 1},
        compiler_params=_cparams("arbitrary"),
        name="attn_context",
    )(sink2, qa, kta, va, qc, ktc, vc, oa, oc)


CONV_HALO = 16


def _conv_kernel(prev_ref, cur_ref, next_ref, w_ref, b_ref, g_ref, beta_ref, *rest, chunks, tc):
    o_ref, xp_ref = rest[-2:]
    j = pl.program_id(1)
    has_prev = (j > 0).astype(jnp.float32)
    has_next = (j < chunks - 1).astype(jnp.float32)
    xp_ref[0:CONV_HALO, :] = prev_ref[tc - CONV_HALO:tc, :] * has_prev
    xp_ref[CONV_HALO:CONV_HALO + tc, :] = cur_ref[...]
    xp_ref[CONV_HALO + tc:2 * CONV_HALO + tc, :] = next_ref[0:CONV_HALO, :] * has_next
    acc = jnp.zeros((tc, B_CH), jnp.float32)
    base = CONV_HALO - CONV_W // 2
    for k in range(CONV_W):
        acc = acc + xp_ref[base + k:base + k + tc, :] * w_ref[k:k + 1, :]
    hc = acc + b_ref[...]
    mu = jnp.mean(hc, axis=-1, keepdims=True)
    xc = hc - mu
    var = jnp.mean(xc * xc, axis=-1, keepdims=True)
    y = xc * lax.rsqrt(var + EPS) * g_ref[...] + beta_ref[...]
    o_ref[...] = (y * jax.nn.sigmoid(y)).astype(o_ref.dtype)


def _conv(hgl, w, b, g, beta, *, nb, seq, base_rows, tc, prev_out=None):
    n = hgl.shape[0]
    chunks = seq // tc
    base = base_rows // tc

    def idx(b_, j, delta):
        return base + b_ * chunks + jnp.clip(j + delta, 0, chunks - 1)

    blk = lambda delta: pl.BlockSpec((tc, B_CH), lambda b_, j: (idx(b_, j, delta), 0))
    const = lambda a: pl.BlockSpec(a.shape, lambda b_, j: (0,) * a.ndim)
    in_specs = [blk(-1), blk(0), blk(1), const(w), const(b), const(g), const(beta)]
    args = [hgl, hgl, hgl, w, b, g, beta]
    aliases = {}
    if prev_out is not None:
        in_specs.append(pl.BlockSpec(memory_space=pl.ANY))
        args.append(prev_out)
        aliases = {len(args) - 1: 0}
    return pl.pallas_call(
        functools.partial(_conv_kernel, chunks=chunks, tc=tc),
        grid=(nb, chunks),
        in_specs=in_specs,
        out_specs=blk(0),
        out_shape=jax.ShapeDtypeStruct((n, B_CH), jnp.bfloat16),
        scratch_shapes=[pltpu.VMEM((tc + 2 * CONV_HALO, B_CH), jnp.float32)],
        input_output_aliases=aliases,
        compiler_params=_cparams("arbitrary", "arbitrary"),
        name="conformer_conv",
    )(*args)


META_ROWS = 8


def _outproj_kernel(oa_ref, ob_ref, oc_ref, x_ref, mod_ref, g_ref, w_ref, wr_ref, br_ref, tri_ref,
                    xo_ref, hp_ref, meta_ref, rw_ref, cnt_ref):
    i = pl.program_id(0)
    tm = x_ref.shape[0]
    f32 = jnp.float32
    mix = (jnp.dot(oa_ref[...], w_ref[0:A_Q, :], preferred_element_type=f32)
           + jnp.dot(ob_ref[...], w_ref[A_Q:A_Q + B_CH, :], preferred_element_type=f32)
           + jnp.dot(oc_ref[...], w_ref[A_Q + B_CH:D_MIX, :], preferred_element_type=f32))
    x = x_ref[...] + mod_ref[0, 2:3, :] * mix
    xo_ref[...] = x
    r = lax.rsqrt(jnp.mean(x * x, axis=-1, keepdims=True) + EPS)
    h = x * r * g_ref[...]
    h = h * (1.0 + mod_ref[0, 4:5, :]) + mod_ref[0, 3:4, :]
    h_hi = h.astype(jnp.bfloat16)
    hp_ref[...] = _pack_pairs(h_hi)

    h_lo = (h - h_hi.astype(f32)).astype(jnp.bfloat16)
    r_hi = jnp.dot(h_hi, wr_ref[...], preferred_element_type=f32)
    r_lo = jnp.dot(h_lo, wr_ref[:, 0:LANES], preferred_element_type=f32)
    lg = r_hi[:, 0:LANES] + r_hi[:, LANES:2 * LANES] + r_lo + br_ref[...]
    lane = lax.broadcasted_iota(jnp.int32, (tm, LANES), 1).astype(f32)
    big = float(LANES)
    ninf = -jnp.inf
    glog = jnp.where(lane < N_GROUPS, lg, ninf)
    gmax = glog.max(axis=-1, keepdims=True)
    g_val = 1.0 / jnp.exp(glog - gmax).sum(axis=-1, keepdims=True)
    g_idx = jnp.where(glog == gmax, lane, big).min(axis=-1, keepdims=True)
    e_lo = N_GROUPS + EXPERTS_PER_GROUP * g_idx
    el = jnp.where((lane >= e_lo) & (lane < e_lo + EXPERTS_PER_GROUP), lg, ninf)
    v0 = el.max(axis=-1, keepdims=True)
    i0 = jnp.where(el == v0, lane, big).min(axis=-1, keepdims=True)
    el1 = jnp.where(lane == i0, ninf, el)
    v1 = el1.max(axis=-1, keepdims=True)
    i1 = jnp.where(el1 == v1, lane, big).min(axis=-1, keepdims=True)
    t = jnp.exp(v1 - v0)
    w0 = g_val / (1.0 + t)
    w1 = g_val * t / (1.0 + t)
    e0 = i0 - N_GROUPS
    e1 = i1 - N_GROUPS

    @pl.when(i == 0)
    def _():
        cnt_ref[...] = jnp.zeros_like(cnt_ref)

    tri = tri_ref[...]
    ranks = []
    for e in (e0, e1):
        oh = lane == e
        ohf = oh.astype(f32)
        pre = jnp.dot(tri, ohf.astype(jnp.bfloat16), preferred_element_type=f32) + cnt_ref[0:1, :]
        ranks.append(jnp.where(oh, pre, 0.0).sum(axis=-1, keepdims=True))
        cnt_ref[0:1, :] = cnt_ref[0:1, :] + ohf.sum(axis=0, keepdims=True)
    rw = jnp.where(lane == 0, w0, jnp.where(lane == 1, w1, 0.0))
    rw_ref[...] = rw
    rec = jnp.where(lane == 0, e0, jnp.where(lane == 1, e1, jnp.where(lane == 2, ranks[0],
                    jnp.where(lane == 3, ranks[1], jnp.where(lane == 4, w0, jnp.where(lane == 5, w1, 0.0))))))
    meta_ref[...] = rec.T[0:META_ROWS, :]


def _outproj(oa, ob, oc, xall, mods, g2, w_out_bf, wr, br, tri, *, tm, n_tiles, n_lat_tiles, s, nb):
    n, d = xall.shape
    rows = n_tiles * tm

    def bidx(i):
        return jnp.where(i < n_lat_tiles, (i * tm) // s, nb)

    row = lambda w: pl.BlockSpec((tm, w), lambda i: (i, 0))
    const = lambda a: pl.BlockSpec(a.shape, lambda i: (0,) * a.ndim)
    return pl.pallas_call(
        _outproj_kernel,
        grid=(n_tiles,),
        in_specs=[row(A_Q), row(B_CH), row(C_Q), row(d),
                  pl.BlockSpec((1, N_MOD, d), lambda i: (bidx(i), 0, 0)),
                  const(g2), const(w_out_bf), const(wr), const(br), const(tri)],
        out_specs=[row(d), row(d // 2), pl.BlockSpec((META_ROWS, tm), lambda i: (0, i)), row(LANES),
                   pl.BlockSpec((8, LANES), lambda i: (0, 0))],
        out_shape=[jax.ShapeDtypeStruct((n, d), jnp.float32),
                   jax.ShapeDtypeStruct((rows, d // 2), jnp.int32),
                   jax.ShapeDtypeStruct((META_ROWS, rows), jnp.float32),
                   jax.ShapeDtypeStruct((rows, LANES), jnp.float32),
                   jax.ShapeDtypeStruct((8, LANES), jnp.float32)],
        input_output_aliases={3: 0},
        compiler_params=_cparams("arbitrary"),
        name="outproj_router",
    )(oa, ob, oc, xall, mods, g2, w_out_bf, wr, br, tri)


def _sc_mesh():
    return plsc.VectorSubcoreMesh(core_axis_name="core", subcore_axis_name="subcore")


def sc_gather_rows(table, idx):
    r = idx.shape[0]
    w = table.shape[1]
    half = r // SC_WINDOW // 2
    idx2 = idx.reshape(1, r)

    @functools.partial(pl.kernel, out_type=jax.ShapeDtypeStruct((r, w), table.dtype), mesh=_sc_mesh())
    def k(x_hbm, i_hbm, o_hbm):
        def body(i_vmem, o_vmem):
            pltpu.sync_copy(x_hbm.at[i_vmem.at[0]], o_vmem)

        pltpu.emit_pipeline(
            body,
            grid=(2, half),
            in_specs=[pl.BlockSpec((1, SC_WINDOW), lambda c, i: (0, c * half + i))],
            out_specs=[pl.BlockSpec((SC_WINDOW, w), lambda c, i: (c * half + i, 0),
                                    pipeline_mode=pl.Buffered(1))],
            core_axis_name=("core", "subcore"),
            dimension_semantics=(pltpu.PARALLEL, pltpu.PARALLEL),
        )(i_hbm, o_hbm)

    return k(table, idx2)


def sc_scatter_rows2(rows, idx_a, idx_b, n_out):
    r, w = rows.shape
    half = r // SC_WINDOW // 2
    ia = idx_a.reshape(1, r)
    ib = idx_b.reshape(1, r)

    @functools.partial(pl.kernel, out_type=jax.ShapeDtypeStruct((n_out, w), rows.dtype), mesh=_sc_mesh(),
                       scratch_types=[])
    def k(x_hbm, ia_hbm, ib_hbm, o_hbm):
        def body(x_vmem, ia_vmem, ib_vmem):
            pltpu.sync_copy(x_vmem, o_hbm.at[ia_vmem.at[0]])
            pltpu.sync_copy(x_vmem, o_hbm.at[ib_vmem.at[0]])

        idx_spec = pl.BlockSpec((1, SC_WINDOW), lambda c, i: (0, c * half + i))
        pltpu.emit_pipeline(
            body,
            grid=(2, half),
            in_specs=[pl.BlockSpec((SC_WINDOW, w), lambda c, i: (c * half + i, 0),
                                   pipeline_mode=pl.Buffered(1)),
                      idx_spec, idx_spec],
            out_specs=[],
            core_axis_name=("core", "subcore"),
            dimension_semantics=(pltpu.PARALLEL, pltpu.PARALLEL),
        )(x_hbm, ia_hbm, ib_hbm)

    return k(rows, ia, ib)


def _expert_kernel(te_ref, nv_ref, x_ref, wg_ref, wu_ref, wd_ref, o_ref):
    t = pl.program_id(0)
    nvalid = nv_ref[t]

    @pl.when(nvalid > 0)
    def _():
        rows = lax.broadcasted_iota(jnp.int32, x_ref.shape, 0)
        lo, hi = _unpack_pairs(jnp.where(rows < nvalid, x_ref[...], 0))
        xb = jnp.concatenate([lo, hi], axis=1).astype(jnp.bfloat16)
        g = jnp.dot(xb, wg_ref[0], preferred_element_type=jnp.float32)
        u = jnp.dot(xb, wu_ref[0], preferred_element_type=jnp.float32)
        a = (g * jax.nn.sigmoid(g) * u).astype(jnp.bfloat16)
        o_ref[...] = _pack_pairs(jnp.dot(a, wd_ref[0], preferred_element_type=jnp.float32))

    @pl.when(nvalid == 0)
    def _():
        o_ref[...] = jnp.zeros_like(o_ref)


def _experts(buf, tile_expert, tile_nvalid, wg, wu, wd):
    rows, wp = buf.shape
    _, d, f = wg.shape
    n_tiles = rows // EXPERT_TILE
    grid_spec = pltpu.PrefetchScalarGridSpec(
        num_scalar_prefetch=2,
        grid=(n_tiles,),
        in_specs=[pl.BlockSpec((EXPERT_TILE, wp), lambda t, te, nv: (t, 0)),
                  pl.BlockSpec((1, d, f), lambda t, te, nv: (te[t], 0, 0)),
                  pl.BlockSpec((1, d, f), lambda t, te, nv: (te[t], 0, 0)),
                  pl.BlockSpec((1, f, d), lambda t, te, nv: (te[t], 0, 0))],
        out_specs=pl.BlockSpec((EXPERT_TILE, wp), lambda t, te, nv: (t, 0)),
    )
    return pl.pallas_call(
        _expert_kernel,
        grid_spec=grid_spec,
        out_shape=jax.ShapeDtypeStruct((rows, wp), jnp.int32),
        compiler_params=_cparams("arbitrary"),
        name="expert_ffn",
    )(tile_expert, tile_nvalid, buf, wg, wu, wd)


def _combine_kernel(y0_ref, y1_ref, rw_ref, x_ref, mod_ref, fg_ref, o_ref, *, final):
    rw = rw_ref[...]
    w0 = rw[:, 0:1]
    w1 = rw[:, 1:2]
    a_lo, a_hi = _unpack_pairs(y0_ref[...])
    b_lo, b_hi = _unpack_pairs(y1_ref[...])
    y = jnp.concatenate([a_lo * w0 + b_lo * w1, a_hi * w0 + b_hi * w1], axis=1)
    x = x_ref[...] + mod_ref[0, 5:6, :] * y
    if final:
        r = lax.rsqrt(jnp.mean(x * x, axis=-1, keepdims=True) + EPS)
        x = x * r * fg_ref[...]
    o_ref[...] = x


def _combine(y, rw, xall, mods, final_g, *, tm, n_tiles, n_lat_tiles, s, nb, final):
    n, d = xall.shape

    def bidx(i):
        return jnp.where(i < n_lat_tiles, (i * tm) // s, nb)

    row = lambda w: pl.BlockSpec((tm, w), lambda i: (i, 0))
    out_rows = n_tiles * tm if final else n
    return pl.pallas_call(
        functools.partial(_combine_kernel, final=final),
        grid=(n_tiles,),
        in_specs=[row(d // 2),
                  pl.BlockSpec((tm, d // 2), lambda i: (i + n_tiles, 0)),
                  row(LANES), row(d),
                  pl.BlockSpec((1, N_MOD, d), lambda i: (bidx(i), 0, 0)),
                  pl.BlockSpec(final_g.shape, lambda i: (0, 0))],
        out_specs=row(d),
        out_shape=jax.ShapeDtypeStruct((out_rows, d), jnp.float32),
        input_output_aliases={} if final else {3: 0},
        compiler_params=_cparams("arbitrary"),
        name="moe_combine",
    )(y, y, rw, xall, mods, final_g)


def _dispatch_plan(meta, counts, n_rows_buf):
    e0, e1, r0, r1 = (meta[k].astype(jnp.int32) for k in range(4))
    cnt = counts[0, :N_EXPERTS].astype(jnp.int32)
    padded = (cnt + EXPERT_TILE - 1) // EXPERT_TILE * EXPERT_TILE
    pends = jnp.cumsum(padded)
    pstarts = pends - padded
    dest0 = pstarts[e0] + r0
    dest1 = pstarts[e1] + r1
    tile_start = jnp.arange(n_rows_buf // EXPERT_TILE, dtype=jnp.int32) * EXPERT_TILE
    te = jnp.sum((tile_start[:, None] >= pends[None, :]).astype(jnp.int32), axis=1)
    te = jnp.minimum(te, N_EXPERTS - 1)
    nvalid = jnp.clip(cnt[te] - (tile_start - pstarts[te]), 0, EXPERT_TILE).astype(jnp.int32)
    return dest0, dest1, te, nvalid


def _rope_tables(s, tm):
    pos = np.arange(s)
    pos_row = jnp.asarray(pos // GRID_W, jnp.float32)
    pos_col = jnp.asarray(pos % GRID_W, jnp.float32)
    n_freq = HEAD_DIM // 4
    inv = ROPE_THETA ** (-jnp.arange(n_freq, dtype=jnp.float32) / n_freq)
    ang_row = pos_row[:, None] * inv
    ang_col = pos_col[:, None] * inv
    ang = jnp.concatenate([ang_row, ang_row, ang_col, ang_col] * (LANES // HEAD_DIM), axis=-1)
    sign = np.where((np.arange(LANES) % 32) < 16, -1.0, 1.0).astype(np.float32)
    cos_t = jnp.concatenate([jnp.cos(ang), jnp.ones((tm, LANES), jnp.float32)], axis=0)
    sin_t = jnp.concatenate([jnp.sin(ang) * sign, jnp.zeros((tm, LANES), jnp.float32)], axis=0)
    return cos_t, sin_t


def kernel(x, c, ctx, c_ctx, norm1_g, norm2_g, w_mod, b_mod, w_in, q_norm_g, k_norm_g, conv_w, conv_b, conv_ln_g, conv_ln_b, sink, w_out, w_group, b_group, w_expert, b_expert, w_gate, w_up, w_down, final_g):
    nb, s, d = x.shape
    l = ctx.shape[1]
    depth = w_in.shape[0]
    assert w_in.shape[2] == D_IN and w_out.shape[1] == D_MIX
    assert s % GRID_W == 0 and s >= 3 * WINDOW and s % WINDOW == 0 and l % WINDOW == 0
    n_lat, n_ctx = nb * s, nb * l
    tm = _pick(np.gcd(s, n_ctx), (512, 256, 128))
    tq = _pick(np.gcd(s, l), (256, 128))
    tc = _pick(np.gcd(s, l), (256, 128))
    assert n_lat % l == 0
    bf = jnp.bfloat16
    f32 = jnp.float32

    xall = jnp.concatenate([x.reshape(n_lat, d), ctx.reshape(n_ctx, d)], axis=0)
    c_all = jnp.concatenate([c, c_ctx[None, :]], axis=0)
    mods_all = _modulation(c_all, w_mod, b_mod).reshape(depth, nb + 1, N_MOD, d)
    cos_t, sin_t = _rope_tables(s, tm)
    head_id = np.arange(LANES) // HEAD_DIM
    gsum = jnp.asarray((head_id[:, None] == head_id[None, :]) / HEAD_DIM, bf)
    tri = jnp.asarray(np.tril(np.ones((tm, tm), np.float32), -1), bf)
    n_lat_tiles = n_lat // tm

    out = None
    for i in range(depth):
        last = i == depth - 1
        with_ctx = not last
        mods = mods_all[i]
        qg = jnp.tile(q_norm_g[i], LANES // HEAD_DIM)[None, :]
        kg = jnp.tile(k_norm_g[i], LANES // HEAD_DIM)[None, :]
        qa, kta, va, hgl, qc, ktc, vc = _inproj(xall, mods, norm1_g[i][None, :], w_in[i].astype(bf), gsum, qg,
                                                kg, cos_t, sin_t, tm=tm, n_lat_tiles=n_lat_tiles, s=s, nb=nb)
        sink2 = sink[i] * LOG2E
        conv_args = (conv_w[i].reshape(CONV_W, B_CH), conv_b[i][None, :], conv_ln_g[i][None, :],
                     conv_ln_b[i][None, :])
        oa = _attn_a(qa, kta, va, nb=nb, s=s, l=l, tq=tq)
        oc = _attn_c(qc, ktc, vc, sink2, nb=nb, s=s, l=l)
        ob = _conv(hgl, *conv_args, nb=nb, seq=s, base_rows=0, tc=tc)
        if with_ctx:
            oa, oc = _attn_ctx(qa, kta, va, qc, ktc, vc, sink2, oa, oc, nb=nb, s=s, l=l)
            ob = _conv(hgl, *conv_args, nb=nb, seq=l, base_rows=n_lat, tc=tc, prev_out=ob)

        n_tok = n_lat + n_ctx if with_ctx else n_lat
        n_tiles = n_tok // tm
        wr32 = jnp.zeros((d, LANES), f32).at[:, :N_GROUPS].set(w_group[i])
        wr32 = wr32.at[:, N_GROUPS:N_GROUPS + N_EXPERTS].set(w_expert[i])
        wr_hi = wr32.astype(bf)
        wr = jnp.concatenate([wr_hi, (wr32 - wr_hi.astype(f32)).astype(bf)], axis=1)
        br = jnp.zeros((1, LANES), f32).at[0, :N_GROUPS].set(b_group[i])
        br = br.at[0, N_GROUPS:N_GROUPS + N_EXPERTS].set(b_expert[i])
        xall, hp, meta, rw, counts = _outproj(oa, ob, oc, xall, mods, norm2_g[i][None, :], w_out[i].astype(bf),
                                              wr, br, tri, tm=tm, n_tiles=n_tiles, n_lat_tiles=n_lat_tiles,
                                              s=s, nb=nb)
        n_rows_buf = 2 * n_tok + N_EXPERTS * EXPERT_TILE
        dest0, dest1, te, nvalid = _dispatch_plan(meta, counts, n_rows_buf)
        buf = sc_scatter_rows2(hp, dest0, dest1, n_rows_buf)
        eo = _experts(buf, te, nvalid, w_gate[i].astype(bf), w_up[i].astype(bf), w_down[i].astype(bf))
        y = sc_gather_rows(eo, jnp.concatenate([dest0, dest1]))
        res = _combine(y, rw, xall, mods, final_g[None, :], tm=tm, n_tiles=n_tiles,
                       n_lat_tiles=n_lat_tiles, s=s, nb=nb, final=last)
        if last:
            out = res
        else:
            xall = res
    return out.reshape(nb, s, d)
```

```python
import functools

import jax
import jax.numpy as jnp
import numpy as np
from jax import lax
from jax.experimental import pallas as pl
from jax.experimental.pallas import tpu as pltpu
from jax.experimental.pallas import tpu_sc as plsc

HEAD_DIM = 64
GRID_W = 64
ROPE_THETA = 10000.0
A_HEADS, A_KV_HEADS = 6, 2
C_HEADS, C_KV_HEADS = 6, 2
B_CH = 256
CONV_W = 31
WINDOW = 128
N_GROUPS = 4
EXPERTS_PER_GROUP = 8
N_EXPERTS = N_GROUPS * EXPERTS_PER_GROUP
N_MOD = 6
EPS = 1e-6
ATTN_SCALE = HEAD_DIM ** -0.5
LOG2E = 1.4426950408889634
Q_SCALE = ATTN_SCALE * LOG2E

A_Q = A_HEADS * HEAD_DIM
A_KV = A_KV_HEADS * HEAD_DIM
C_Q = C_HEADS * HEAD_DIM
C_KV = C_KV_HEADS * HEAD_DIM
D_MIX = A_Q + B_CH + C_Q
OFF_AQ = 0
OFF_AK = OFF_AQ + A_Q
OFF_AV = OFF_AK + A_KV
OFF_BU = OFF_AV + A_KV
OFF_CQ = OFF_BU + 2 * B_CH
OFF_CK = OFF_CQ + C_Q
OFF_CV = OFF_CK + C_KV
D_IN = OFF_CV + C_KV

LANES = 128
SUBLANES = 8
KT_ROWS = 2 * LANES
V_W = 4 * LANES
EXPERT_TILE = 512
SC_WINDOW = 128
VMEM_LIMIT = 56 * 1024 * 1024
HI_MASK = -65536


def _cparams(*sem):
    return pltpu.CompilerParams(dimension_semantics=sem, vmem_limit_bytes=VMEM_LIMIT)


def _pick(n, cands):
    for c in cands:
        if n % c == 0:
            return c
    raise ValueError(f"no tile in {cands} divides {n}")


def _pack_pairs(x):
    w = x.shape[1] // 2
    lo = lax.bitcast_convert_type(x[:, :w].astype(jnp.bfloat16).astype(jnp.float32), jnp.int32)
    hi = lax.bitcast_convert_type(x[:, w:].astype(jnp.bfloat16).astype(jnp.float32), jnp.int32)
    return (hi & HI_MASK) | lax.shift_right_logical(lo, 16)


def _unpack_pairs(p):
    lo = lax.bitcast_convert_type(lax.shift_left(p, 16), jnp.float32)
    hi = lax.bitcast_convert_type(p & HI_MASK, jnp.float32)
    return lo, hi


def _mod_kernel(c_ref, w_ref, b_ref, o_ref):
    c = c_ref[...]
    a = c * jax.nn.sigmoid(c)
    o_ref[0] = jnp.dot(a, w_ref[0], preferred_element_type=jnp.float32,
                       precision=lax.Precision.HIGHEST) + b_ref[0]


def _modulation(c_all, w_mod, b_mod):
    depth, d, n = w_mod.shape
    r = c_all.shape[0]
    tn = _pick(n, (1024, 512, 256, 128))
    return pl.pallas_call(
        _mod_kernel,
        grid=(depth, n // tn),
        in_specs=[pl.BlockSpec((r, d), lambda l, j: (0, 0)),
                  pl.BlockSpec((1, d, tn), lambda l, j: (l, 0, j)),
                  pl.BlockSpec((1, 1, tn), lambda l, j: (l, 0, j))],
        out_specs=pl.BlockSpec((1, r, tn), lambda l, j: (l, 0, j)),
        out_shape=jax.ShapeDtypeStruct((depth, r, n), jnp.float32),
        compiler_params=_cparams("arbitrary", "arbitrary"),
        name="modulation",
    )(c_all, w_mod, b_mod.reshape(depth, 1, n))


def _head_mean_sq(blk, gsum):
    sq = blk * blk
    hi = sq.astype(jnp.bfloat16)
    lo = (sq - hi.astype(jnp.float32)).astype(jnp.bfloat16)
    return (jnp.dot(hi, gsum, preferred_element_type=jnp.float32)
            + jnp.dot(lo, gsum, preferred_element_type=jnp.float32))


def _x_specs(xs, tm, n_lat_tiles):
    d = xs[0].shape[1]
    if len(xs) == 1:
        return [pl.BlockSpec((tm, d), lambda i: (i, 0))]
    return [pl.BlockSpec((tm, d), lambda i: (jnp.minimum(i, n_lat_tiles - 1), 0)),
            pl.BlockSpec((tm, d), lambda i: (jnp.maximum(i - n_lat_tiles, 0), 0))]


def _load_x(x_refs, n_lat_tiles):
    if len(x_refs) == 1:
        return x_refs[0][...]
    return jnp.where(pl.program_id(0) < n_lat_tiles, x_refs[0][...], x_refs[1][...])


def _inproj_kernel(*refs, n_x, n_lat_tiles):
    x_refs = refs[:n_x]
    (mod_ref, g_ref, w_ref, gsum_ref, qg_ref, kg_ref, cos_ref, sin_ref,
     qa_ref, kta_ref, va_ref, hgl_ref, qc_ref, ktc_ref, vc_ref) = refs[n_x:]
    x = _load_x(x_refs, n_lat_tiles)
    tm = x.shape[0]
    r = lax.rsqrt(jnp.mean(x * x, axis=-1, keepdims=True) + EPS)
    h = x * r * g_ref[...]
    h = h * (1.0 + mod_ref[0, 1:2, :]) + mod_ref[0, 0:1, :]
    p = jnp.dot(h.astype(jnp.bfloat16), w_ref[...], preferred_element_type=jnp.float32)

    lane = lax.broadcasted_iota(jnp.int32, (tm, LANES), 1)
    first16 = (lane % 32) < 16
    lo64 = lane < HEAD_DIM
    cos = cos_ref[...]
    sin = sin_ref[...]
    gsum = gsum_ref[...]

    def blk(off):
        return p[:, off:off + LANES]

    def rope(t):
        sw = jnp.where(first16, pltpu.roll(t, LANES - 16, axis=1), pltpu.roll(t, 16, axis=1))
        return t * cos + sw * sin

    def norm(t, g):
        return t * lax.rsqrt(_head_mean_sq(t, gsum) + EPS) * g

    def dup(t):
        sw = pltpu.roll(t, HEAD_DIM, axis=1)
        return jnp.where(lo64, t, sw), jnp.where(lo64, sw, t)

    qg = qg_ref[...]
    kg = kg_ref[...]
    for i in range(A_Q // LANES):
        t = rope(norm(blk(OFF_AQ + i * LANES), qg)) * Q_SCALE
        qa_ref[:, i * LANES:(i + 1) * LANES] = t.astype(jnp.bfloat16)
    def store_kv(kt_ref, v_ref, k_blk, v_blk):
        bf = jnp.bfloat16
        ones = jnp.ones((tm, LANES), bf)
        k0, k1 = dup(k_blk)
        v0, v1 = dup(v_blk)
        kt_ref[0:128, :] = k0.T.astype(bf)
        kt_ref[128:256, :] = k1.T.astype(bf)
        v_ref[:, 0:128] = v0.astype(bf)
        v_ref[:, 128:256] = ones
        v_ref[:, 256:384] = v1.astype(bf)
        v_ref[:, 384:512] = ones

    store_kv(kta_ref, va_ref, rope(norm(blk(OFF_AK), kg)), blk(OFF_AV))
    for i in range(B_CH // LANES):
        a = blk(OFF_BU + i * LANES)
        gt = blk(OFF_BU + B_CH + i * LANES)
        hgl_ref[:, i * LANES:(i + 1) * LANES] = a * jax.nn.sigmoid(gt)
    for i in range(C_Q // LANES):
        t = rope(blk(OFF_CQ + i * LANES)) * Q_SCALE
        qc_ref[:, i * LANES:(i + 1) * LANES] = t.astype(jnp.bfloat16)
    store_kv(ktc_ref, vc_ref, rope(blk(OFF_CK)), blk(OFF_CV))


def _inproj(xs, mods, g1, w_in_bf, gsum, qg, kg, cos_t, sin_t, *, tm, n_lat_tiles, s, nb):
    n = sum(a.shape[0] for a in xs)
    d = xs[0].shape[1]
    s_tiles = s // tm

    def bidx(i):
        return jnp.where(i < n_lat_tiles, (i * tm) // s, nb)

    def ridx(i):
        return jnp.where(i < n_lat_tiles, i % s_tiles, s_tiles)

    row = lambda w: pl.BlockSpec((tm, w), lambda i: (i, 0))
    ktspec = pl.BlockSpec((KT_ROWS, tm), lambda i: (0, i))
    const = lambda a: pl.BlockSpec(a.shape, lambda i: (0,) * a.ndim)
    bf = jnp.bfloat16
    return pl.pallas_call(
        functools.partial(_inproj_kernel, n_x=len(xs), n_lat_tiles=n_lat_tiles),
        grid=(n // tm,),
        in_specs=_x_specs(xs, tm, n_lat_tiles) + [
                  pl.BlockSpec((1, N_MOD, d), lambda i: (bidx(i), 0, 0)),
                  const(g1), const(w_in_bf), const(gsum), const(qg), const(kg),
                  pl.BlockSpec((tm, LANES), lambda i: (ridx(i), 0)),
                  pl.BlockSpec((tm, LANES), lambda i: (ridx(i), 0))],
        out_specs=[row(A_Q), ktspec, row(V_W), row(B_CH), row(C_Q), ktspec, row(V_W)],
        out_shape=[jax.ShapeDtypeStruct((n, A_Q), bf), jax.ShapeDtypeStruct((KT_ROWS, n), bf),
                   jax.ShapeDtypeStruct((n, V_W), bf), jax.ShapeDtypeStruct((n, B_CH), jnp.float32),
                   jax.ShapeDtypeStruct((n, C_Q), bf), jax.ShapeDtypeStruct((KT_ROWS, n), bf),
                   jax.ShapeDtypeStruct((n, V_W), bf)],
        compiler_params=_cparams("arbitrary"),
        name="inproj",
    )(*xs, mods, g1, w_in_bf, gsum, qg, kg, cos_t, sin_t)


def _stack_heads(q_ref, tq):
    lane = lax.broadcasted_iota(jnp.int32, (tq, LANES), 1)
    lo = lane < HEAD_DIM
    qb = [q_ref[:, i * LANES:(i + 1) * LANES] for i in range(3)]
    zero = jnp.zeros_like(qb[0])
    keep_lo = lambda t: jnp.where(lo, t, zero)
    keep_hi = lambda t: jnp.where(lo, zero, t)
    s0 = jnp.concatenate([keep_lo(qb[0]), keep_hi(qb[0]), keep_lo(qb[1])], axis=0)
    s1 = jnp.concatenate([keep_hi(qb[1]), keep_lo(qb[2]), keep_hi(qb[2])], axis=0)
    return s0, s1, lo


def _unstack_store(o_ref, o0, o1, lo, tq):
    bf = jnp.bfloat16
    o_ref[:, 0:128] = jnp.where(lo, o0[0:tq], o0[tq:2 * tq]).astype(bf)
    o_ref[:, 128:256] = jnp.where(lo, o0[2 * tq:3 * tq], o1[0:tq]).astype(bf)
    o_ref[:, 256:384] = jnp.where(lo, o1[tq:2 * tq], o1[2 * tq:3 * tq]).astype(bf)


def _krow(kv):
    return slice(kv * LANES, (kv + 1) * LANES)


def _vcol(kv):
    return slice(2 * kv * LANES, (2 * kv + 2) * LANES)


def _softmax_pv(scores, values, extra=None):
    m = None
    for t in scores:
        for c in range(0, t.shape[1], LANES):
            blk = t[:, c:c + LANES]
            m = blk if m is None else jnp.maximum(m, blk)
    m = m.max(axis=-1, keepdims=True)
    if extra is not None:
        m = jnp.maximum(m, extra)
    acc = None
    for t, v in zip(scores, values):
        c = jnp.dot(jnp.exp2(t - m).astype(jnp.bfloat16), v, preferred_element_type=jnp.float32)
        acc = c if acc is None else acc + c
    den = acc[:, LANES:2 * LANES]
    if extra is not None:
        den = den + jnp.exp2(extra - m)
    return acc[:, 0:LANES] / den


ATTN_CHAIN_ROWS = 128


def _qk(q, kt):
    return jnp.dot(q, kt, preferred_element_type=jnp.float32)


def _attn_a_kernel(q_ref, ktl_ref, ktc_ref, vl_ref, vc_ref, o_ref, *, tq):
    s0, s1, lo = _stack_heads(q_ref, tq)
    outs = []
    for kv, qs in enumerate((s0, s1)):
        krow = _krow(kv)
        vcol = _vcol(kv)
        parts = []
        for r0 in range(0, 3 * tq, ATTN_CHAIN_ROWS):
            qr = qs[r0:r0 + ATTN_CHAIN_ROWS]
            ss = [_qk(qr, ktl_ref[krow, :]), _qk(qr, ktc_ref[krow, :])]
            parts.append(_softmax_pv(ss, [vl_ref[:, vcol], vc_ref[:, vcol]]))
        outs.append(jnp.concatenate(parts, axis=0))
    _unstack_store(o_ref, outs[0], outs[1], lo, tq)


def _kv_specs(nb, s, l):
    ctx0 = nb * s // l
    return [pl.BlockSpec((KT_ROWS, s), lambda b, j: (0, b)),
            pl.BlockSpec((KT_ROWS, l), lambda b, j: (0, ctx0 + b)),
            pl.BlockSpec((s, V_W), lambda b, j: (b, 0)),
            pl.BlockSpec((l, V_W), lambda b, j: (ctx0 + b, 0))]


def _attn_a(qa, kt, v, *, nb, s, l, tq):
    n = qa.shape[0]
    n_q = s // tq
    return pl.pallas_call(
        functools.partial(_attn_a_kernel, tq=tq),
        grid=(nb, n_q),
        in_specs=[pl.BlockSpec((tq, A_Q), lambda b, j: (b * n_q + j, 0))] + _kv_specs(nb, s, l),
        out_specs=pl.BlockSpec((tq, A_Q), lambda b, j: (b * n_q + j, 0)),
        out_shape=jax.ShapeDtypeStruct((n, A_Q), jnp.bfloat16),
        compiler_params=_cparams("arbitrary", "arbitrary"),
        name="attn_global",
    )(qa, kt, kt, v, v)


WIN_BLOCKS = 4


def _sink_column(sink_ref, kv, rows):
    return jnp.concatenate([jnp.full((rows, 1), sink_ref[3 * kv + g], jnp.float32) for g in range(3)], axis=0)


def _attn_c_kernel(sink_ref, bias_ref, q_ref, ktl_ref, ktc_ref, vl_ref, vc_ref, o_ref, *, s, blocks):
    j = pl.program_id(1)
    tq = WINDOW
    band = 3 * WINDOW
    sk = [_sink_column(sink_ref, kv, tq) for kv in range(C_KV_HEADS)]
    for blk in range(blocks):
        jb = j * blocks + blk
        rows = slice(blk * tq, (blk + 1) * tq)
        s0, s1, lo = _stack_heads(q_ref.at[rows, :], tq)
        start = pl.multiple_of(jnp.clip((jb - 1) * WINDOW, 0, s - band), WINDOW)
        bias = bias_ref[jb - start // WINDOW]
        outs = []
        for kv, qs in enumerate((s0, s1)):
            krow = _krow(kv)
            vcol = _vcol(kv)
            sl = _qk(qs, ktl_ref[krow, pl.ds(start, band)]) + bias
            sc = _qk(qs, ktc_ref[krow, :])
            outs.append(_softmax_pv([sl, sc], [vl_ref[pl.ds(start, band), vcol], vc_ref[:, vcol]], sk[kv]))
        _unstack_store(o_ref.at[rows, :], outs[0], outs[1], lo, tq)


def _window_bias():
    r = np.arange(3 * WINDOW)[:, None] % WINDOW
    col = np.arange(3 * WINDOW)[None, :]
    tabs = [np.where(np.abs(col - r - WINDOW * off) <= WINDOW, 0.0, -np.inf) for off in range(3)]
    return jnp.asarray(np.stack(tabs), jnp.float32)


def _attn_c(qc, kt, v, sink2, *, nb, s, l):
    n = qc.shape[0]
    blocks = _pick(s // WINDOW, (WIN_BLOCKS, 2, 1))
    tq = blocks * WINDOW
    n_q = s // tq
    bias = _window_bias()
    return pl.pallas_call(
        functools.partial(_attn_c_kernel, s=s, blocks=blocks),
        grid=(nb, n_q),
        in_specs=[pl.BlockSpec(memory_space=pltpu.SMEM),
                  pl.BlockSpec(bias.shape, lambda b, j: (0, 0, 0)),
                  pl.BlockSpec((tq, C_Q), lambda b, j: (b * n_q + j, 0))] + _kv_specs(nb, s, l),
        out_specs=pl.BlockSpec((tq, C_Q), lambda b, j: (b * n_q + j, 0)),
        out_shape=jax.ShapeDtypeStruct((n, C_Q), jnp.bfloat16),
        compiler_params=_cparams("arbitrary", "arbitrary"),
        name="attn_window",
    )(sink2, bias, qc, kt, kt, v, v)


def _attn_ctx_kernel(sink_ref, qa_ref, kta_ref, va_ref, qc_ref, ktc_ref, vc_ref, oa_in, oc_in, oa_ref, oc_ref, *, l):
    del oa_in, oc_in
    for q_ref, kt_ref, v_ref, o_ref, with_sink in ((qa_ref, kta_ref, va_ref, oa_ref, False),
                                                   (qc_ref, ktc_ref, vc_ref, oc_ref, True)):
        s0, s1, lo = _stack_heads(q_ref, l)
        outs = []
        for kv, qs in enumerate((s0, s1)):
            sk = _sink_column(sink_ref, kv, l) if with_sink else None
            outs.append(_softmax_pv([_qk(qs, kt_ref[_krow(kv), :])], [v_ref[:, _vcol(kv)]], sk))
        _unstack_store(o_ref, outs[0], outs[1], lo, l)


def _attn_ctx(qa, kta, va, qc, ktc, vc, sink2, oa, oc, *, nb, s, l):
    base = nb * s // l
    row = lambda w: pl.BlockSpec((l, w), lambda b: (base + b, 0))
    ktspec = pl.BlockSpec((KT_ROWS, l), lambda b: (0, base + b))
    anyspec = pl.BlockSpec(memory_space=pl.ANY)
    return pl.pallas_call(
        functools.partial(_attn_ctx_kernel, l=l),
        grid=(nb,),
        in_specs=[pl.BlockSpec(memory_space=pltpu.SMEM), row(A_Q), ktspec, row(V_W), row(C_Q), ktspec, row(V_W),
                  anyspec, anyspec],
        out_specs=[row(A_Q), row(C_Q)],
        out_shape=[jax.ShapeDtypeStruct(oa.shape, oa.dtype), jax.ShapeDtypeStruct(oc.shape, oc.dtype)],
        input_output_aliases={7: 0, 8: 1},
        compiler_params=_cparams("arbitrary"),
        name="attn_context",
    )(sink2, qa, kta, va, qc, ktc, vc, oa, oc)


CONV_HALO = 16
CONV_ROWS = 64


def _conv_kernel(prev_ref, cur_ref, next_ref, w_ref, b_ref, g_ref, beta_ref, *rest, chunks, tc):
    o_ref, sh_ref = rest[-2:]
    j = pl.program_id(1)
    has_prev = (j > 0).astype(jnp.float32)
    has_next = (j < chunks - 1).astype(jnp.float32)
    rows = tc + 2 * CONV_HALO
    sh_ref[0, 0:CONV_HALO, :] = prev_ref[tc - CONV_HALO:tc, :] * has_prev
    sh_ref[0, CONV_HALO:CONV_HALO + tc, :] = cur_ref[...]
    sh_ref[0, CONV_HALO + tc:rows, :] = next_ref[0:CONV_HALO, :] * has_next
    for b in range(1, SUBLANES):
        sh_ref[b, 0:rows - SUBLANES, :] = sh_ref[0, b:b + rows - SUBLANES, :]
    base = CONV_HALO - CONV_W // 2
    for r0 in range(0, tc, CONV_ROWS):
        acc = None
        for k in range(CONV_W):
            a, b = divmod(base + k, SUBLANES)
            term = sh_ref[b, SUBLANES * a + r0:SUBLANES * a + r0 + CONV_ROWS, :] * w_ref[k:k + 1, :]
            acc = term if acc is None else acc + term
        hc = acc + b_ref[...]
        mu = jnp.mean(hc, axis=-1, keepdims=True)
        xc = hc - mu
        var = jnp.mean(xc * xc, axis=-1, keepdims=True)
        y = xc * lax.rsqrt(var + EPS) * g_ref[...] + beta_ref[...]
        o_ref[r0:r0 + CONV_ROWS, :] = (y * jax.nn.sigmoid(y)).astype(o_ref.dtype)


def _conv(hgl, w, b, g, beta, *, nb, seq, base_rows, tc, prev_out=None):
    n = hgl.shape[0]
    chunks = seq // tc
    base = base_rows // tc

    def idx(b_, j, delta):
        return base + b_ * chunks + jnp.clip(j + delta, 0, chunks - 1)

    blk = lambda delta: pl.BlockSpec((tc, B_CH), lambda b_, j: (idx(b_, j, delta), 0))
    const = lambda a: pl.BlockSpec(a.shape, lambda b_, j: (0,) * a.ndim)
    in_specs = [blk(-1), blk(0), blk(1), const(w), const(b), const(g), const(beta)]
    args = [hgl, hgl, hgl, w, b, g, beta]
    aliases = {}
    if prev_out is not None:
        in_specs.append(pl.BlockSpec(memory_space=pl.ANY))
        args.append(prev_out)
        aliases = {len(args) - 1: 0}
    return pl.pallas_call(
        functools.partial(_conv_kernel, chunks=chunks, tc=tc),
        grid=(nb, chunks),
        in_specs=in_specs,
        out_specs=blk(0),
        out_shape=jax.ShapeDtypeStruct((n, B_CH), jnp.bfloat16),
        scratch_shapes=[pltpu.VMEM((SUBLANES, tc + 2 * CONV_HALO, B_CH), jnp.float32)],
        input_output_aliases=aliases,
        compiler_params=_cparams("arbitrary", "arbitrary"),
        name="conformer_conv",
    )(*args)


META_ROWS = 8


def _outproj_kernel(*refs, n_x, n_lat_tiles):
    x_refs = refs[:n_x]
    (oa_ref, ob_ref, oc_ref, mod_ref, g_ref, w_ref, wr_ref, br_ref, tri_ref,
     xo_ref, hp_ref, meta_ref, rw_ref, cnt_ref) = refs[n_x:]
    i = pl.program_id(0)
    tm = xo_ref.shape[0]
    f32 = jnp.float32
    mix = (jnp.dot(oa_ref[...], w_ref[0:A_Q, :], preferred_element_type=f32)
           + jnp.dot(ob_ref[...], w_ref[A_Q:A_Q + B_CH, :], preferred_element_type=f32)
           + jnp.dot(oc_ref[...], w_ref[A_Q + B_CH:D_MIX, :], preferred_element_type=f32))
    x = _load_x(x_refs, n_lat_tiles) + mod_ref[0, 2:3, :] * mix
    xo_ref[...] = x
    r = lax.rsqrt(jnp.mean(x * x, axis=-1, keepdims=True) + EPS)
    h = x * r * g_ref[...]
    h = h * (1.0 + mod_ref[0, 4:5, :]) + mod_ref[0, 3:4, :]
    h_hi = h.astype(jnp.bfloat16)
    hp_ref[...] = _pack_pairs(h_hi)

    h_lo = (h - h_hi.astype(f32)).astype(jnp.bfloat16)
    r_hi = jnp.dot(h_hi, wr_ref[...], preferred_element_type=f32)
    r_lo = jnp.dot(h_lo, wr_ref[:, 0:LANES], preferred_element_type=f32)
    lg = r_hi[:, 0:LANES] + r_hi[:, LANES:2 * LANES] + r_lo + br_ref[...]
    lane = lax.broadcasted_iota(jnp.int32, (tm, LANES), 1).astype(f32)
    big = float(LANES)
    ninf = -jnp.inf
    glog = jnp.where(lane < N_GROUPS, lg, ninf)
    gmax = glog.max(axis=-1, keepdims=True)
    g_val = 1.0 / jnp.exp(glog - gmax).sum(axis=-1, keepdims=True)
    g_idx = jnp.where(glog == gmax, lane, big).min(axis=-1, keepdims=True)
    e_lo = N_GROUPS + EXPERTS_PER_GROUP * g_idx
    el = jnp.where((lane >= e_lo) & (lane < e_lo + EXPERTS_PER_GROUP), lg, ninf)
    v0 = el.max(axis=-1, keepdims=True)
    i0 = jnp.where(el == v0, lane, big).min(axis=-1, keepdims=True)
    el1 = jnp.where(lane == i0, ninf, el)
    v1 = el1.max(axis=-1, keepdims=True)
    i1 = jnp.where(el1 == v1, lane, big).min(axis=-1, keepdims=True)
    t = jnp.exp(v1 - v0)
    w0 = g_val / (1.0 + t)
    w1 = g_val * t / (1.0 + t)
    e0 = i0 - N_GROUPS
    e1 = i1 - N_GROUPS

    @pl.when(i == 0)
    def _():
        cnt_ref[...] = jnp.zeros_like(cnt_ref)

    tri = tri_ref[...]
    ranks = []
    for e in (e0, e1):
        oh = lane == e
        ohf = oh.astype(f32)
        pre = jnp.dot(tri, ohf.astype(jnp.bfloat16), preferred_element_type=f32) + cnt_ref[0:1, :]
        ranks.append(jnp.where(oh, pre, 0.0).sum(axis=-1, keepdims=True))
        cnt_ref[0:1, :] = cnt_ref[0:1, :] + ohf.sum(axis=0, keepdims=True)
    rw = jnp.where(lane == 0, w0, jnp.where(lane == 1, w1, 0.0))
    rw_ref[...] = rw
    rec = jnp.where(lane == 0, e0, jnp.where(lane == 1, e1, jnp.where(lane == 2, ranks[0],
                    jnp.where(lane == 3, ranks[1], jnp.where(lane == 4, w0, jnp.where(lane == 5, w1, 0.0))))))
    meta_ref[...] = rec.T[0:META_ROWS, :]


def _outproj(xs, oa, ob, oc, mods, g2, w_out_bf, wr, br, tri, *, tm, n_tiles, n_lat_tiles, s, nb):
    n = sum(a.shape[0] for a in xs)
    d = xs[0].shape[1]
    rows = n_tiles * tm

    def bidx(i):
        return jnp.where(i < n_lat_tiles, (i * tm) // s, nb)

    row = lambda w: pl.BlockSpec((tm, w), lambda i: (i, 0))
    const = lambda a: pl.BlockSpec(a.shape, lambda i: (0,) * a.ndim)
    return pl.pallas_call(
        functools.partial(_outproj_kernel, n_x=len(xs), n_lat_tiles=n_lat_tiles),
        grid=(n_tiles,),
        in_specs=_x_specs(xs, tm, n_lat_tiles) + [
                  row(A_Q), row(B_CH), row(C_Q),
                  pl.BlockSpec((1, N_MOD, d), lambda i: (bidx(i), 0, 0)),
                  const(g2), const(w_out_bf), const(wr), const(br), const(tri)],
        out_specs=[row(d), row(d // 2), pl.BlockSpec((META_ROWS, tm), lambda i: (0, i)), row(LANES),
                   pl.BlockSpec((8, LANES), lambda i: (0, 0))],
        out_shape=[jax.ShapeDtypeStruct((n, d), jnp.float32),
                   jax.ShapeDtypeStruct((rows, d // 2), jnp.int32),
                   jax.ShapeDtypeStruct((META_ROWS, rows), jnp.float32),
                   jax.ShapeDtypeStruct((rows, LANES), jnp.float32),
                   jax.ShapeDtypeStruct((8, LANES), jnp.float32)],
        input_output_aliases={0: 0} if len(xs) == 1 else {},
        compiler_params=_cparams("arbitrary"),
        name="outproj_router",
    )(*xs, oa, ob, oc, mods, g2, w_out_bf, wr, br, tri)


def _sc_mesh():
    return plsc.VectorSubcoreMesh(core_axis_name="core", subcore_axis_name="subcore")


def sc_gather_rows(table, idx):
    r = idx.shape[0]
    w = table.shape[1]
    half = r // SC_WINDOW // 2
    idx2 = idx.reshape(1, r)

    @functools.partial(pl.kernel, out_type=jax.ShapeDtypeStruct((r, w), table.dtype), mesh=_sc_mesh())
    def k(x_hbm, i_hbm, o_hbm):
        def body(i_vmem, o_vmem):
            pltpu.sync_copy(x_hbm.at[i_vmem.at[0]], o_vmem)

        pltpu.emit_pipeline(
            body,
            grid=(2, half),
            in_specs=[pl.BlockSpec((1, SC_WINDOW), lambda c, i: (0, c * half + i))],
            out_specs=[pl.BlockSpec((SC_WINDOW, w), lambda c, i: (c * half + i, 0),
                                    pipeline_mode=pl.Buffered(1))],
            core_axis_name=("core", "subcore"),
            dimension_semantics=(pltpu.PARALLEL, pltpu.PARALLEL),
        )(i_hbm, o_hbm)

    return k(table, idx2)


def sc_scatter_rows2(rows, idx_a, idx_b, n_out):
    r, w = rows.shape
    half = r // SC_WINDOW // 2
    ia = idx_a.reshape(1, r)
    ib = idx_b.reshape(1, r)

    @functools.partial(pl.kernel, out_type=jax.ShapeDtypeStruct((n_out, w), rows.dtype), mesh=_sc_mesh(),
                       scratch_types=[])
    def k(x_hbm, ia_hbm, ib_hbm, o_hbm):
        def body(x_vmem, ia_vmem, ib_vmem):
            pltpu.sync_copy(x_vmem, o_hbm.at[ia_vmem.at[0]])
            pltpu.sync_copy(x_vmem, o_hbm.at[ib_vmem.at[0]])

        idx_spec = pl.BlockSpec((1, SC_WINDOW), lambda c, i: (0, c * half + i))
        pltpu.emit_pipeline(
            body,
            grid=(2, half),
            in_specs=[pl.BlockSpec((SC_WINDOW, w), lambda c, i: (c * half + i, 0),
                                   pipeline_mode=pl.Buffered(1)),
                      idx_spec, idx_spec],
            out_specs=[],
            core_axis_name=("core", "subcore"),
            dimension_semantics=(pltpu.PARALLEL, pltpu.PARALLEL),
        )(x_hbm, ia_hbm, ib_hbm)

    return k(rows, ia, ib)


def _expert_kernel(te_ref, nv_ref, x_ref, wg_ref, wu_ref, wd_ref, o_ref, wgb_ref, wub_ref, wdb_ref):
    t = pl.program_id(0)
    nvalid = nv_ref[t]

    @pl.when((t == 0) | (te_ref[t] != te_ref[jnp.maximum(t - 1, 0)]))
    def _():
        wgb_ref[...] = wg_ref[0].astype(jnp.bfloat16)
        wub_ref[...] = wu_ref[0].astype(jnp.bfloat16)
        wdb_ref[...] = wd_ref[0].astype(jnp.bfloat16)

    @pl.when(nvalid > 0)
    def _():
        rows = lax.broadcasted_iota(jnp.int32, x_ref.shape, 0)
        lo, hi = _unpack_pairs(jnp.where(rows < nvalid, x_ref[...], 0))
        xb = jnp.concatenate([lo, hi], axis=1).astype(jnp.bfloat16)
        g = jnp.dot(xb, wgb_ref[...], preferred_element_type=jnp.float32)
        u = jnp.dot(xb, wub_ref[...], preferred_element_type=jnp.float32)
        a = (g * jax.nn.sigmoid(g) * u).astype(jnp.bfloat16)
        o_ref[...] = _pack_pairs(jnp.dot(a, wdb_ref[...], preferred_element_type=jnp.float32))

    @pl.when(nvalid == 0)
    def _():
        o_ref[...] = jnp.zeros_like(o_ref)


def _experts(buf, tile_expert, tile_nvalid, wg, wu, wd, *, layer):
    rows, wp = buf.shape
    _, _, d, f = wg.shape
    n_tiles = rows // EXPERT_TILE
    grid_spec = pltpu.PrefetchScalarGridSpec(
        num_scalar_prefetch=2,
        grid=(n_tiles,),
        in_specs=[pl.BlockSpec((EXPERT_TILE, wp), lambda t, te, nv: (t, 0)),
                  pl.BlockSpec((None, 1, d, f), lambda t, te, nv: (layer, te[t], 0, 0)),
                  pl.BlockSpec((None, 1, d, f), lambda t, te, nv: (layer, te[t], 0, 0)),
                  pl.BlockSpec((None, 1, f, d), lambda t, te, nv: (layer, te[t], 0, 0))],
        out_specs=pl.BlockSpec((EXPERT_TILE, wp), lambda t, te, nv: (t, 0)),
        scratch_shapes=[pltpu.VMEM((d, f), jnp.bfloat16), pltpu.VMEM((d, f), jnp.bfloat16),
                        pltpu.VMEM((f, d), jnp.bfloat16)],
    )
    return pl.pallas_call(
        _expert_kernel,
        grid_spec=grid_spec,
        out_shape=jax.ShapeDtypeStruct((rows, wp), jnp.int32),
        compiler_params=_cparams("arbitrary"),
        name="expert_ffn",
    )(tile_expert, tile_nvalid, buf, wg, wu, wd)


def _combine_kernel(y0_ref, y1_ref, rw_ref, x_ref, mod_ref, fg_ref, o_ref, *, final):
    rw = rw_ref[...]
    w0 = rw[:, 0:1]
    w1 = rw[:, 1:2]
    a_lo, a_hi = _unpack_pairs(y0_ref[...])
    b_lo, b_hi = _unpack_pairs(y1_ref[...])
    y = jnp.concatenate([a_lo * w0 + b_lo * w1, a_hi * w0 + b_hi * w1], axis=1)
    x = x_ref[...] + mod_ref[0, 5:6, :] * y
    if final:
        r = lax.rsqrt(jnp.mean(x * x, axis=-1, keepdims=True) + EPS)
        x = x * r * fg_ref[...]
    o_ref[...] = x


def _combine(y, rw, xall, mods, final_g, *, tm, n_tiles, n_lat_tiles, s, nb, final):
    n, d = xall.shape

    def bidx(i):
        return jnp.where(i < n_lat_tiles, (i * tm) // s, nb)

    row = lambda w: pl.BlockSpec((tm, w), lambda i: (i, 0))
    out_rows = n_tiles * tm if final else n
    return pl.pallas_call(
        functools.partial(_combine_kernel, final=final),
        grid=(n_tiles,),
        in_specs=[row(d // 2),
                  pl.BlockSpec((tm, d // 2), lambda i: (i + n_tiles, 0)),
                  row(LANES), row(d),
                  pl.BlockSpec((1, N_MOD, d), lambda i: (bidx(i), 0, 0)),
                  pl.BlockSpec(final_g.shape, lambda i: (0, 0))],
        out_specs=row(d),
        out_shape=jax.ShapeDtypeStruct((out_rows, d), jnp.float32),
        input_output_aliases={} if final else {3: 0},
        compiler_params=_cparams("arbitrary"),
        name="moe_combine",
    )(y, y, rw, xall, mods, final_g)


def _dispatch_plan(meta, counts, n_rows_buf):
    e0, e1, r0, r1 = (meta[k].astype(jnp.int32) for k in range(4))
    cnt = counts[0, :N_EXPERTS].astype(jnp.int32)
    padded = (cnt + EXPERT_TILE - 1) // EXPERT_TILE * EXPERT_TILE
    pends = jnp.cumsum(padded)
    pstarts = pends - padded
    dest0 = pstarts[e0] + r0
    dest1 = pstarts[e1] + r1
    tile_start = jnp.arange(n_rows_buf // EXPERT_TILE, dtype=jnp.int32) * EXPERT_TILE
    te = jnp.sum((tile_start[:, None] >= pends[None, :]).astype(jnp.int32), axis=1)
    te = jnp.minimum(te, N_EXPERTS - 1)
    nvalid = jnp.clip(cnt[te] - (tile_start - pstarts[te]), 0, EXPERT_TILE).astype(jnp.int32)
    return dest0, dest1, te, nvalid


def _rope_tables(s, tm):
    pos = np.arange(s)
    pos_row = jnp.asarray(pos // GRID_W, jnp.float32)
    pos_col = jnp.asarray(pos % GRID_W, jnp.float32)
    n_freq = HEAD_DIM // 4
    inv = ROPE_THETA ** (-jnp.arange(n_freq, dtype=jnp.float32) / n_freq)
    ang_row = pos_row[:, None] * inv
    ang_col = pos_col[:, None] * inv
    ang = jnp.concatenate([ang_row, ang_row, ang_col, ang_col] * (LANES // HEAD_DIM), axis=-1)
    sign = np.where((np.arange(LANES) % 32) < 16, -1.0, 1.0).astype(np.float32)
    cos_t = jnp.concatenate([jnp.cos(ang), jnp.ones((tm, LANES), jnp.float32)], axis=0)
    sin_t = jnp.concatenate([jnp.sin(ang) * sign, jnp.zeros((tm, LANES), jnp.float32)], axis=0)
    return cos_t, sin_t


def kernel(x, c, ctx, c_ctx, norm1_g, norm2_g, w_mod, b_mod, w_in, q_norm_g, k_norm_g, conv_w, conv_b, conv_ln_g, conv_ln_b, sink, w_out, w_group, b_group, w_expert, b_expert, w_gate, w_up, w_down, final_g):
    nb, s, d = x.shape
    l = ctx.shape[1]
    depth = w_in.shape[0]
    assert w_in.shape[2] == D_IN and w_out.shape[1] == D_MIX
    assert s % GRID_W == 0 and s >= 3 * WINDOW and s % WINDOW == 0 and l % WINDOW == 0
    n_lat, n_ctx = nb * s, nb * l
    tm = _pick(np.gcd(s, n_ctx), (512, 256, 128))
    tq = _pick(np.gcd(s, l), (256, 128))
    tc = _pick(np.gcd(s, l), (256, 128))
    assert n_lat % l == 0
    bf = jnp.bfloat16
    f32 = jnp.float32

    xs = (x.reshape(n_lat, d), ctx.reshape(n_ctx, d))
    c_all = jnp.concatenate([c, c_ctx[None, :]], axis=0)
    mods_all = _modulation(c_all, w_mod, b_mod).reshape(depth, nb + 1, N_MOD, d)
    cos_t, sin_t = _rope_tables(s, tm)
    head_id = np.arange(LANES) // HEAD_DIM
    gsum = jnp.asarray((head_id[:, None] == head_id[None, :]) / HEAD_DIM, bf)
    tri = jnp.asarray(np.tril(np.ones((tm, tm), np.float32), -1), bf)
    n_lat_tiles = n_lat // tm

    out = None
    for i in range(depth):
        last = i == depth - 1
        with_ctx = not last
        mods = mods_all[i]
        qg = jnp.tile(q_norm_g[i], LANES // HEAD_DIM)[None, :]
        kg = jnp.tile(k_norm_g[i], LANES // HEAD_DIM)[None, :]
        qa, kta, va, hgl, qc, ktc, vc = _inproj(xs, mods, norm1_g[i][None, :], w_in[i].astype(bf), gsum, qg,
                                                kg, cos_t, sin_t, tm=tm, n_lat_tiles=n_lat_tiles, s=s, nb=nb)
        sink2 = sink[i] * LOG2E
        conv_args = (conv_w[i].reshape(CONV_W, B_CH), conv_b[i][None, :], conv_ln_g[i][None, :],
                     conv_ln_b[i][None, :])
        oa = _attn_a(qa, kta, va, nb=nb, s=s, l=l, tq=tq)
        oc = _attn_c(qc, ktc, vc, sink2, nb=nb, s=s, l=l)
        ob = _conv(hgl, *conv_args, nb=nb, seq=s, base_rows=0, tc=tc)
        if with_ctx:
            oa, oc = _attn_ctx(qa, kta, va, qc, ktc, vc, sink2, oa, oc, nb=nb, s=s, l=l)
            ob = _conv(hgl, *conv_args, nb=nb, seq=l, base_rows=n_lat, tc=tc, prev_out=ob)

        n_tok = n_lat + n_ctx if with_ctx else n_lat
        n_tiles = n_tok // tm
        wr32 = jnp.zeros((d, LANES), f32).at[:, :N_GROUPS].set(w_group[i])
        wr32 = wr32.at[:, N_GROUPS:N_GROUPS + N_EXPERTS].set(w_expert[i])
        wr_hi = wr32.astype(bf)
        wr = jnp.concatenate([wr_hi, (wr32 - wr_hi.astype(f32)).astype(bf)], axis=1)
        br = jnp.zeros((1, LANES), f32).at[0, :N_GROUPS].set(b_group[i])
        br = br.at[0, N_GROUPS:N_GROUPS + N_EXPERTS].set(b_expert[i])
        xall, hp, meta, rw, counts = _outproj(xs, oa, ob, oc, mods, norm2_g[i][None, :], w_out[i].astype(bf),
                                              wr, br, tri, tm=tm, n_tiles=n_tiles, n_lat_tiles=n_lat_tiles,
                                              s=s, nb=nb)
        n_rows_buf = 2 * n_tok + N_EXPERTS * EXPERT_TILE
        dest0, dest1, te, nvalid = _dispatch_plan(meta, counts, n_rows_buf)
        buf = sc_scatter_rows2(hp, dest0, dest1, n_rows_buf)
        eo = _experts(buf, te, nvalid, w_gate, w_up, w_down, layer=i)
        y = sc_gather_rows(eo, jnp.concatenate([dest0, dest1]))
        res = _combine(y, rw, xall, mods, final_g[None, :], tm=tm, n_tiles=n_tiles,
                       n_lat_tiles=n_lat_tiles, s=s, nb=nb, final=last)
        if last:
            out = res
        else:
            xs = (res,)
    return out.reshape(nb, s, d)
```

```python
import functools

import jax
import jax.numpy as jnp
import numpy as np
from jax import lax
from jax.experimental import pallas as pl
from jax.experimental.pallas import tpu as pltpu
from jax.experimental.pallas import tpu_sc as plsc

HEAD_DIM = 64
GRID_W = 64
ROPE_THETA = 10000.0
A_HEADS, A_KV_HEADS = 6, 2
C_HEADS, C_KV_HEADS = 6, 2
B_CH = 256
CONV_W = 31
WINDOW = 128
N_GROUPS = 4
EXPERTS_PER_GROUP = 8
N_EXPERTS = N_GROUPS * EXPERTS_PER_GROUP
N_MOD = 6
EPS = 1e-6
ATTN_SCALE = HEAD_DIM ** -0.5
LOG2E = 1.4426950408889634
Q_SCALE = ATTN_SCALE * LOG2E

A_Q = A_HEADS * HEAD_DIM
A_KV = A_KV_HEADS * HEAD_DIM
C_Q = C_HEADS * HEAD_DIM
C_KV = C_KV_HEADS * HEAD_DIM
D_MIX = A_Q + B_CH + C_Q
OFF_AQ = 0
OFF_AK = OFF_AQ + A_Q
OFF_AV = OFF_AK + A_KV
OFF_BU = OFF_AV + A_KV
OFF_CQ = OFF_BU + 2 * B_CH
OFF_CK = OFF_CQ + C_Q
OFF_CV = OFF_CK + C_KV
D_IN = OFF_CV + C_KV

LANES = 128
SUBLANES = 8
KT_ROWS = 2 * LANES
V_W = 4 * LANES
EXPERT_TILE = 512
SC_WINDOW = 128
VMEM_LIMIT = 56 * 1024 * 1024
HI_MASK = -65536
ROW_CHAIN = 256


def _cparams(*sem):
    return pltpu.CompilerParams(dimension_semantics=sem, vmem_limit_bytes=VMEM_LIMIT)


def _pick(n, cands):
    for c in cands:
        if n % c == 0:
            return c
    raise ValueError(f"no tile in {cands} divides {n}")


def _pack_pairs(x):
    w = x.shape[1] // 2
    lo = lax.bitcast_convert_type(x[:, :w].astype(jnp.bfloat16).astype(jnp.float32), jnp.int32)
    hi = lax.bitcast_convert_type(x[:, w:].astype(jnp.bfloat16).astype(jnp.float32), jnp.int32)
    return (hi & HI_MASK) | lax.shift_right_logical(lo, 16)


def _unpack_pairs(p):
    lo = lax.bitcast_convert_type(lax.shift_left(p, 16), jnp.float32)
    hi = lax.bitcast_convert_type(p & HI_MASK, jnp.float32)
    return lo, hi


def _mod_kernel(c_ref, w_ref, b_ref, o_ref):
    c = c_ref[...]
    a = c * jax.nn.sigmoid(c)
    o_ref[0] = jnp.dot(a, w_ref[0], preferred_element_type=jnp.float32,
                       precision=lax.Precision.HIGHEST) + b_ref[0]


def _modulation(c_all, w_mod, b_mod):
    depth, d, n = w_mod.shape
    r = c_all.shape[0]
    tn = _pick(n, (1024, 512, 256, 128))
    return pl.pallas_call(
        _mod_kernel,
        grid=(depth, n // tn),
        in_specs=[pl.BlockSpec((r, d), lambda l, j: (0, 0)),
                  pl.BlockSpec((1, d, tn), lambda l, j: (l, 0, j)),
                  pl.BlockSpec((1, 1, tn), lambda l, j: (l, 0, j))],
        out_specs=pl.BlockSpec((1, r, tn), lambda l, j: (l, 0, j)),
        out_shape=jax.ShapeDtypeStruct((depth, r, n), jnp.float32),
        compiler_params=_cparams("arbitrary", "arbitrary"),
        name="modulation",
    )(c_all, w_mod, b_mod.reshape(depth, 1, n))


def _head_mean_sq(blk, gsum):
    sq = blk * blk
    hi = sq.astype(jnp.bfloat16)
    lo = (sq - hi.astype(jnp.float32)).astype(jnp.bfloat16)
    return (jnp.dot(hi, gsum, preferred_element_type=jnp.float32)
            + jnp.dot(lo, gsum, preferred_element_type=jnp.float32))


def _x_specs(xs, tm, n_lat_tiles):
    d = xs[0].shape[1]
    if len(xs) == 1:
        return [pl.BlockSpec((tm, d), lambda i: (i, 0))]
    return [pl.BlockSpec((tm, d), lambda i: (jnp.minimum(i, n_lat_tiles - 1), 0)),
            pl.BlockSpec((tm, d), lambda i: (jnp.maximum(i - n_lat_tiles, 0), 0))]


def _load_x(x_refs, n_lat_tiles, rows=slice(None)):
    if len(x_refs) == 1:
        return x_refs[0][rows, :]
    return jnp.where(pl.program_id(0) < n_lat_tiles, x_refs[0][rows, :], x_refs[1][rows, :])


def _moe_mix(y0_ref, y1_ref, rw_ref, rows):
    rw = rw_ref[rows, :]
    w0 = rw[:, 0:1]
    w1 = rw[:, 1:2]
    a_lo, a_hi = _unpack_pairs(y0_ref[rows, :])
    b_lo, b_hi = _unpack_pairs(y1_ref[rows, :])
    return jnp.concatenate([a_lo * w0 + b_lo * w1, a_hi * w0 + b_hi * w1], axis=1)


def _inproj_kernel(*refs, n_x, n_lat_tiles, fused_combine):
    x_refs = refs[:n_x]
    refs = refs[n_x:]
    if fused_combine:
        y0_ref, y1_ref, rw_ref, modp_ref = refs[:4]
        refs = refs[4:]
        xo_ref = refs[-1]
        refs = refs[:-1]
    (mod_ref, g_ref, w_ref, gsum_ref, qg_ref, kg_ref, cos_ref, sin_ref,
     qa_ref, kta_ref, va_ref, hgl_ref, qc_ref, ktc_ref, vc_ref) = refs
    tm = qa_ref.shape[0]
    tr = min(tm, ROW_CHAIN)
    bf = jnp.bfloat16
    lane = lax.broadcasted_iota(jnp.int32, (tr, LANES), 1)
    first16 = (lane % 32) < 16
    lo64 = lane < HEAD_DIM
    gsum = gsum_ref[...]
    qg = qg_ref[...]
    kg = kg_ref[...]
    scale = g_ref[...] * (1.0 + mod_ref[0, 1:2, :])
    shift = mod_ref[0, 0:1, :]
    ones = jnp.ones((tr, LANES), bf)

    for r0 in range(0, tm, tr):
        rows = slice(r0, r0 + tr)
        x = _load_x(x_refs, n_lat_tiles, rows)
        if fused_combine:
            x = x + modp_ref[0, 5:6, :] * _moe_mix(y0_ref, y1_ref, rw_ref, rows)
            xo_ref[rows, :] = x
        r = lax.rsqrt(jnp.mean(x * x, axis=-1, keepdims=True) + EPS)
        h = (x * r) * scale + shift
        p = jnp.dot(h.astype(bf), w_ref[...], preferred_element_type=jnp.float32)
        cos = cos_ref[rows, :]
        sin = sin_ref[rows, :]

        def blk(off):
            return p[:, off:off + LANES]

        def rope(t):
            sw = jnp.where(first16, pltpu.roll(t, LANES - 16, axis=1), pltpu.roll(t, 16, axis=1))
            return t * cos + sw * sin

        def norm(t, g):
            return t * lax.rsqrt(_head_mean_sq(t, gsum) + EPS) * g

        def dup(t):
            sw = pltpu.roll(t, HEAD_DIM, axis=1)
            return jnp.where(lo64, t, sw), jnp.where(lo64, sw, t)

        def store_kv(kt_ref, v_ref, k_blk, v_blk):
            k0, k1 = dup(k_blk)
            v0, v1 = dup(v_blk)
            kt_ref[0:128, rows] = k0.T.astype(bf)
            kt_ref[128:256, rows] = k1.T.astype(bf)
            v_ref[rows, 0:128] = v0.astype(bf)
            v_ref[rows, 128:256] = ones
            v_ref[rows, 256:384] = v1.astype(bf)
            v_ref[rows, 384:512] = ones

        for i in range(A_Q // LANES):
            t = rope(norm(blk(OFF_AQ + i * LANES), qg)) * Q_SCALE
            qa_ref[rows, i * LANES:(i + 1) * LANES] = t.astype(bf)
        store_kv(kta_ref, va_ref, rope(norm(blk(OFF_AK), kg)), blk(OFF_AV))
        for i in range(B_CH // LANES):
            a = blk(OFF_BU + i * LANES)
            gt = blk(OFF_BU + B_CH + i * LANES)
            hgl_ref[rows, i * LANES:(i + 1) * LANES] = a * jax.nn.sigmoid(gt)
        for i in range(C_Q // LANES):
            t = rope(blk(OFF_CQ + i * LANES)) * Q_SCALE
            qc_ref[rows, i * LANES:(i + 1) * LANES] = t.astype(bf)
        store_kv(ktc_ref, vc_ref, rope(blk(OFF_CK)), blk(OFF_CV))


def _inproj(xs, mods, g1, w_in_bf, gsum, qg, kg, cos_t, sin_t, *, tm, n_lat_tiles, s, nb, combine=None):
    n = sum(a.shape[0] for a in xs)
    d = xs[0].shape[1]
    s_tiles = s // tm
    n_tiles = n // tm

    def bidx(i):
        return jnp.where(i < n_lat_tiles, (i * tm) // s, nb)

    def ridx(i):
        return jnp.where(i < n_lat_tiles, i % s_tiles, s_tiles)

    row = lambda w: pl.BlockSpec((tm, w), lambda i: (i, 0))
    ktspec = pl.BlockSpec((KT_ROWS, tm), lambda i: (0, i))
    const = lambda a: pl.BlockSpec(a.shape, lambda i: (0,) * a.ndim)
    bf = jnp.bfloat16
    modspec = pl.BlockSpec((1, N_MOD, d), lambda i: (bidx(i), 0, 0))
    in_specs = _x_specs(xs, tm, n_lat_tiles)
    args = list(xs)
    out_specs = [row(A_Q), ktspec, row(V_W), row(B_CH), row(C_Q), ktspec, row(V_W)]
    out_shape = [jax.ShapeDtypeStruct((n, A_Q), bf), jax.ShapeDtypeStruct((KT_ROWS, n), bf),
                 jax.ShapeDtypeStruct((n, V_W), bf), jax.ShapeDtypeStruct((n, B_CH), jnp.float32),
                 jax.ShapeDtypeStruct((n, C_Q), bf), jax.ShapeDtypeStruct((KT_ROWS, n), bf),
                 jax.ShapeDtypeStruct((n, V_W), bf)]
    aliases = {}
    if combine is not None:
        y, rw, mods_prev = combine
        assert len(xs) == 1 and y.shape[0] == 2 * n
        in_specs += [row(d // 2), pl.BlockSpec((tm, d // 2), lambda i: (i + n_tiles, 0)), row(LANES), modspec]
        args += [y, y, rw, mods_prev]
        out_specs.append(row(d))
        out_shape.append(jax.ShapeDtypeStruct((n, d), jnp.float32))
        aliases = {0: len(out_shape) - 1}
    in_specs += [modspec, const(g1), const(w_in_bf), const(gsum), const(qg), const(kg),
                 pl.BlockSpec((tm, LANES), lambda i: (ridx(i), 0)),
                 pl.BlockSpec((tm, LANES), lambda i: (ridx(i), 0))]
    args += [mods, g1, w_in_bf, gsum, qg, kg, cos_t, sin_t]
    outs = pl.pallas_call(
        functools.partial(_inproj_kernel, n_x=len(xs), n_lat_tiles=n_lat_tiles,
                          fused_combine=combine is not None),
        grid=(n_tiles,),
        in_specs=in_specs,
        out_specs=out_specs,
        out_shape=out_shape,
        input_output_aliases=aliases,
        compiler_params=_cparams("arbitrary"),
        name="inproj",
    )(*args)
    if combine is not None:
        return (outs[-1],) + tuple(outs[:-1])
    return (None,) + tuple(outs)


def _stack_heads(q_ref, tq):
    lane = lax.broadcasted_iota(jnp.int32, (tq, LANES), 1)
    lo = lane < HEAD_DIM
    qb = [q_ref[:, i * LANES:(i + 1) * LANES] for i in range(3)]
    zero = jnp.zeros_like(qb[0])
    keep_lo = lambda t: jnp.where(lo, t, zero)
    keep_hi = lambda t: jnp.where(lo, zero, t)
    s0 = jnp.concatenate([keep_lo(qb[0]), keep_hi(qb[0]), keep_lo(qb[1])], axis=0)
    s1 = jnp.concatenate([keep_hi(qb[1]), keep_lo(qb[2]), keep_hi(qb[2])], axis=0)
    return s0, s1, lo


def _unstack_store(o_ref, o0, o1, lo, tq):
    bf = jnp.bfloat16
    o_ref[:, 0:128] = jnp.where(lo, o0[0:tq], o0[tq:2 * tq]).astype(bf)
    o_ref[:, 128:256] = jnp.where(lo, o0[2 * tq:3 * tq], o1[0:tq]).astype(bf)
    o_ref[:, 256:384] = jnp.where(lo, o1[tq:2 * tq], o1[2 * tq:3 * tq]).astype(bf)


def _krow(kv):
    return slice(kv * LANES, (kv + 1) * LANES)


def _vcol(kv):
    return slice(2 * kv * LANES, (2 * kv + 2) * LANES)


def _softmax_pv(scores, values, extra=None):
    m = None
    for t in scores:
        for c in range(0, t.shape[1], LANES):
            blk = t[:, c:c + LANES]
            m = blk if m is None else jnp.maximum(m, blk)
    m = m.max(axis=-1, keepdims=True)
    if extra is not None:
        m = jnp.maximum(m, extra)
    acc = None
    for t, v in zip(scores, values):
        c = jnp.dot(jnp.exp2(t - m).astype(jnp.bfloat16), v, preferred_element_type=jnp.float32)
        acc = c if acc is None else acc + c
    den = acc[:, LANES:2 * LANES]
    if extra is not None:
        den = den + jnp.exp2(extra - m)
    return acc[:, 0:LANES] / den


ATTN_CHAIN_ROWS = 128


def _qk(q, kt):
    return jnp.dot(q, kt, preferred_element_type=jnp.float32)


def _attn_a_kernel(q_ref, ktl_ref, ktc_ref, vl_ref, vc_ref, o_ref, *, tq):
    s0, s1, lo = _stack_heads(q_ref, tq)
    outs = []
    for kv, qs in enumerate((s0, s1)):
        krow = _krow(kv)
        vcol = _vcol(kv)
        parts = []
        for r0 in range(0, 3 * tq, ATTN_CHAIN_ROWS):
            qr = qs[r0:r0 + ATTN_CHAIN_ROWS]
            ss = [_qk(qr, ktl_ref[krow, :]), _qk(qr, ktc_ref[krow, :])]
            parts.append(_softmax_pv(ss, [vl_ref[:, vcol], vc_ref[:, vcol]]))
        outs.append(jnp.concatenate(parts, axis=0))
    _unstack_store(o_ref, outs[0], outs[1], lo, tq)


def _kv_specs(nb, s, l):
    ctx0 = nb * s // l
    return [pl.BlockSpec((KT_ROWS, s), lambda b, j: (0, b)),
            pl.BlockSpec((KT_ROWS, l), lambda b, j: (0, ctx0 + b)),
            pl.BlockSpec((s, V_W), lambda b, j: (b, 0)),
            pl.BlockSpec((l, V_W), lambda b, j: (ctx0 + b, 0))]


def _attn_a(qa, kt, v, *, nb, s, l, tq):
    n = qa.shape[0]
    n_q = s // tq
    return pl.pallas_call(
        functools.partial(_attn_a_kernel, tq=tq),
        grid=(nb, n_q),
        in_specs=[pl.BlockSpec((tq, A_Q), lambda b, j: (b * n_q + j, 0))] + _kv_specs(nb, s, l),
        out_specs=pl.BlockSpec((tq, A_Q), lambda b, j: (b * n_q + j, 0)),
        out_shape=jax.ShapeDtypeStruct((n, A_Q), jnp.bfloat16),
        compiler_params=_cparams("arbitrary", "arbitrary"),
        name="attn_global",
    )(qa, kt, kt, v, v)


WIN_BLOCKS = 4


def _sink_column(sink_ref, kv, rows):
    return jnp.concatenate([jnp.full((rows, 1), sink_ref[3 * kv + g], jnp.float32) for g in range(3)], axis=0)


def _attn_c_kernel(sink_ref, bias_ref, q_ref, ktl_ref, ktc_ref, vl_ref, vc_ref, o_ref, *, s, blocks):
    j = pl.program_id(1)
    tq = WINDOW
    band = 3 * WINDOW
    sk = [_sink_column(sink_ref, kv, tq) for kv in range(C_KV_HEADS)]
    for blk in range(blocks):
        jb = j * blocks + blk
        rows = slice(blk * tq, (blk + 1) * tq)
        s0, s1, lo = _stack_heads(q_ref.at[rows, :], tq)
        start = pl.multiple_of(jnp.clip((jb - 1) * WINDOW, 0, s - band), WINDOW)
        bias = bias_ref[jb - start // WINDOW]
        outs = []
        for kv, qs in enumerate((s0, s1)):
            krow = _krow(kv)
            vcol = _vcol(kv)
            sl = _qk(qs, ktl_ref[krow, pl.ds(start, band)]) + bias
            sc = _qk(qs, ktc_ref[krow, :])
            outs.append(_softmax_pv([sl, sc], [vl_ref[pl.ds(start, band), vcol], vc_ref[:, vcol]], sk[kv]))
        _unstack_store(o_ref.at[rows, :], outs[0], outs[1], lo, tq)


def _window_bias():
    r = np.arange(3 * WINDOW)[:, None] % WINDOW
    col = np.arange(3 * WINDOW)[None, :]
    tabs = [np.where(np.abs(col - r - WINDOW * off) <= WINDOW, 0.0, -np.inf) for off in range(3)]
    return jnp.asarray(np.stack(tabs), jnp.float32)


def _attn_c(qc, kt, v, sink2, *, nb, s, l):
    n = qc.shape[0]
    blocks = _pick(s // WINDOW, (WIN_BLOCKS, 2, 1))
    tq = blocks * WINDOW
    n_q = s // tq
    bias = _window_bias()
    return pl.pallas_call(
        functools.partial(_attn_c_kernel, s=s, blocks=blocks),
        grid=(nb, n_q),
        in_specs=[pl.BlockSpec(memory_space=pltpu.SMEM),
                  pl.BlockSpec(bias.shape, lambda b, j: (0, 0, 0)),
                  pl.BlockSpec((tq, C_Q), lambda b, j: (b * n_q + j, 0))] + _kv_specs(nb, s, l),
        out_specs=pl.BlockSpec((tq, C_Q), lambda b, j: (b * n_q + j, 0)),
        out_shape=jax.ShapeDtypeStruct((n, C_Q), jnp.bfloat16),
        compiler_params=_cparams("arbitrary", "arbitrary"),
        name="attn_window",
    )(sink2, bias, qc, kt, kt, v, v)


def _attn_ctx_kernel(sink_ref, qa_ref, kta_ref, va_ref, qc_ref, ktc_ref, vc_ref, oa_in, oc_in, oa_ref, oc_ref, *, l):
    del oa_in, oc_in
    for q_ref, kt_ref, v_ref, o_ref, with_sink in ((qa_ref, kta_ref, va_ref, oa_ref, False),
                                                   (qc_ref, ktc_ref, vc_ref, oc_ref, True)):
        s0, s1, lo = _stack_heads(q_ref, l)
        outs = []
        for kv, qs in enumerate((s0, s1)):
            sk = _sink_column(sink_ref, kv, l) if with_sink else None
            outs.append(_softmax_pv([_qk(qs, kt_ref[_krow(kv), :])], [v_ref[:, _vcol(kv)]], sk))
        _unstack_store(o_ref, outs[0], outs[1], lo, l)


def _attn_ctx(qa, kta, va, qc, ktc, vc, sink2, oa, oc, *, nb, s, l):
    base = nb * s // l
    row = lambda w: pl.BlockSpec((l, w), lambda b: (base + b, 0))
    ktspec = pl.BlockSpec((KT_ROWS, l), lambda b: (0, base + b))
    anyspec = pl.BlockSpec(memory_space=pl.ANY)
    return pl.pallas_call(
        functools.partial(_attn_ctx_kernel, l=l),
        grid=(nb,),
        in_specs=[pl.BlockSpec(memory_space=pltpu.SMEM), row(A_Q), ktspec, row(V_W), row(C_Q), ktspec, row(V_W),
                  anyspec, anyspec],
        out_specs=[row(A_Q), row(C_Q)],
        out_shape=[jax.ShapeDtypeStruct(oa.shape, oa.dtype), jax.ShapeDtypeStruct(oc.shape, oc.dtype)],
        input_output_aliases={7: 0, 8: 1},
        compiler_params=_cparams("arbitrary"),
        name="attn_context",
    )(sink2, qa, kta, va, qc, ktc, vc, oa, oc)


CONV_HALO = 16
CONV_ROWS = 64


def _conv_kernel(prev_ref, cur_ref, next_ref, w_ref, b_ref, g_ref, beta_ref, *rest, chunks, tc):
    o_ref, sh_ref = rest[-2:]
    j = pl.program_id(1)
    has_prev = (j > 0).astype(jnp.float32)
    has_next = (j < chunks - 1).astype(jnp.float32)
    rows = tc + 2 * CONV_HALO
    sh_ref[0, 0:CONV_HALO, :] = prev_ref[tc - CONV_HALO:tc, :] * has_prev
    sh_ref[0, CONV_HALO:CONV_HALO + tc, :] = cur_ref[...]
    sh_ref[0, CONV_HALO + tc:rows, :] = next_ref[0:CONV_HALO, :] * has_next
    for b in range(1, SUBLANES):
        sh_ref[b, 0:rows - SUBLANES, :] = sh_ref[0, b:b + rows - SUBLANES, :]
    base = CONV_HALO - CONV_W // 2
    for r0 in range(0, tc, CONV_ROWS):
        acc = None
        for k in range(CONV_W):
            a, b = divmod(base + k, SUBLANES)
            term = sh_ref[b, SUBLANES * a + r0:SUBLANES * a + r0 + CONV_ROWS, :] * w_ref[k:k + 1, :]
            acc = term if acc is None else acc + term
        hc = acc + b_ref[...]
        mu = jnp.mean(hc, axis=-1, keepdims=True)
        xc = hc - mu
        var = jnp.mean(xc * xc, axis=-1, keepdims=True)
        y = xc * lax.rsqrt(var + EPS) * g_ref[...] + beta_ref[...]
        o_ref[r0:r0 + CONV_ROWS, :] = (y * jax.nn.sigmoid(y)).astype(o_ref.dtype)


def _conv(hgl, w, b, g, beta, *, nb, seq, base_rows, tc, prev_out=None):
    n = hgl.shape[0]
    chunks = seq // tc
    base = base_rows // tc

    def idx(b_, j, delta):
        return base + b_ * chunks + jnp.clip(j + delta, 0, chunks - 1)

    blk = lambda delta: pl.BlockSpec((tc, B_CH), lambda b_, j: (idx(b_, j, delta), 0))
    const = lambda a: pl.BlockSpec(a.shape, lambda b_, j: (0,) * a.ndim)
    in_specs = [blk(-1), blk(0), blk(1), const(w), const(b), const(g), const(beta)]
    args = [hgl, hgl, hgl, w, b, g, beta]
    aliases = {}
    if prev_out is not None:
        in_specs.append(pl.BlockSpec(memory_space=pl.ANY))
        args.append(prev_out)
        aliases = {len(args) - 1: 0}
    return pl.pallas_call(
        functools.partial(_conv_kernel, chunks=chunks, tc=tc),
        grid=(nb, chunks),
        in_specs=in_specs,
        out_specs=blk(0),
        out_shape=jax.ShapeDtypeStruct((n, B_CH), jnp.bfloat16),
        scratch_shapes=[pltpu.VMEM((SUBLANES, tc + 2 * CONV_HALO, B_CH), jnp.float32)],
        input_output_aliases=aliases,
        compiler_params=_cparams("arbitrary", "arbitrary"),
        name="conformer_conv",
    )(*args)


META_ROWS = 8


def _outproj_kernel(*refs, n_x, n_lat_tiles):
    x_refs = refs[:n_x]
    (oa_ref, ob_ref, oc_ref, mod_ref, g_ref, w_ref, wr_ref, br_ref, tri_ref,
     xo_ref, hp_ref, meta_ref, rw_ref, cnt_ref) = refs[n_x:]
    i = pl.program_id(0)
    tm = xo_ref.shape[0]
    f32 = jnp.float32
    tr = tri_ref.shape[0]
    gate = mod_ref[0, 2:3, :]
    scale = g_ref[...] * (1.0 + mod_ref[0, 4:5, :])
    shift = mod_ref[0, 3:4, :]
    lane = lax.broadcasted_iota(jnp.int32, (tr, LANES), 1).astype(f32)
    big = float(LANES)
    ninf = -jnp.inf
    tri = tri_ref[...]

    @pl.when(i == 0)
    def _():
        cnt_ref[...] = jnp.zeros_like(cnt_ref)

    cnt = cnt_ref[0:1, :]
    for r0 in range(0, tm, tr):
        rows = slice(r0, r0 + tr)
        mix = (jnp.dot(oa_ref[rows, :], w_ref[0:A_Q, :], preferred_element_type=f32)
               + jnp.dot(ob_ref[rows, :], w_ref[A_Q:A_Q + B_CH, :], preferred_element_type=f32)
               + jnp.dot(oc_ref[rows, :], w_ref[A_Q + B_CH:D_MIX, :], preferred_element_type=f32))
        x = _load_x(x_refs, n_lat_tiles, rows) + gate * mix
        xo_ref[rows, :] = x
        r = lax.rsqrt(jnp.mean(x * x, axis=-1, keepdims=True) + EPS)
        h = (x * r) * scale + shift
        h_hi = h.astype(jnp.bfloat16)
        hp_ref[rows, :] = _pack_pairs(h_hi)
        h_lo = (h - h_hi.astype(f32)).astype(jnp.bfloat16)
        r_hi = jnp.dot(h_hi, wr_ref[...], preferred_element_type=f32)
        r_lo = jnp.dot(h_lo, wr_ref[:, 0:LANES], preferred_element_type=f32)
        lg = r_hi[:, 0:LANES] + r_hi[:, LANES:2 * LANES] + r_lo + br_ref[...]
        glog = jnp.where(lane < N_GROUPS, lg, ninf)
        gmax = glog.max(axis=-1, keepdims=True)
        g_val = 1.0 / jnp.exp(glog - gmax).sum(axis=-1, keepdims=True)
        g_idx = jnp.where(glog == gmax, lane, big).min(axis=-1, keepdims=True)
        e_lo = N_GROUPS + EXPERTS_PER_GROUP * g_idx
        el = jnp.where((lane >= e_lo) & (lane < e_lo + EXPERTS_PER_GROUP), lg, ninf)
        v0 = el.max(axis=-1, keepdims=True)
        i0 = jnp.where(el == v0, lane, big).min(axis=-1, keepdims=True)
        el1 = jnp.where(lane == i0, ninf, el)
        v1 = el1.max(axis=-1, keepdims=True)
        i1 = jnp.where(el1 == v1, lane, big).min(axis=-1, keepdims=True)
        t = jnp.exp(v1 - v0)
        w0 = g_val / (1.0 + t)
        w1 = g_val * t / (1.0 + t)
        e0 = i0 - N_GROUPS
        e1 = i1 - N_GROUPS

        ranks = []
        for e in (e0, e1):
            oh = lane == e
            ohf = oh.astype(f32)
            pre = jnp.dot(tri, ohf.astype(jnp.bfloat16), preferred_element_type=f32) + cnt
            ranks.append(jnp.where(oh, pre, 0.0).sum(axis=-1, keepdims=True))
            cnt = cnt + ohf.sum(axis=0, keepdims=True)
        rw_ref[rows, :] = jnp.where(lane == 0, w0, jnp.where(lane == 1, w1, 0.0))
        rec = jnp.where(lane == 0, e0, jnp.where(lane == 1, e1, jnp.where(lane == 2, ranks[0],
                        jnp.where(lane == 3, ranks[1], jnp.where(lane == 4, w0, jnp.where(lane == 5, w1, 0.0))))))
        meta_ref[:, rows] = rec.T[0:META_ROWS, :]
    cnt_ref[0:1, :] = cnt


def _outproj(xs, oa, ob, oc, mods, g2, w_out_bf, wr, br, tri, *, tm, n_tiles, n_lat_tiles, s, nb):
    n = sum(a.shape[0] for a in xs)
    d = xs[0].shape[1]
    rows = n_tiles * tm

    def bidx(i):
        return jnp.where(i < n_lat_tiles, (i * tm) // s, nb)

    row = lambda w: pl.BlockSpec((tm, w), lambda i: (i, 0))
    const = lambda a: pl.BlockSpec(a.shape, lambda i: (0,) * a.ndim)
    return pl.pallas_call(
        functools.partial(_outproj_kernel, n_x=len(xs), n_lat_tiles=n_lat_tiles),
        grid=(n_tiles,),
        in_specs=_x_specs(xs, tm, n_lat_tiles) + [
                  row(A_Q), row(B_CH), row(C_Q),
                  pl.BlockSpec((1, N_MOD, d), lambda i: (bidx(i), 0, 0)),
                  const(g2), const(w_out_bf), const(wr), const(br), const(tri)],
        out_specs=[row(d), row(d // 2), pl.BlockSpec((META_ROWS, tm), lambda i: (0, i)), row(LANES),
                   pl.BlockSpec((8, LANES), lambda i: (0, 0))],
        out_shape=[jax.ShapeDtypeStruct((n, d), jnp.float32),
                   jax.ShapeDtypeStruct((rows, d // 2), jnp.int32),
                   jax.ShapeDtypeStruct((META_ROWS, rows), jnp.float32),
                   jax.ShapeDtypeStruct((rows, LANES), jnp.float32),
                   jax.ShapeDtypeStruct((8, LANES), jnp.float32)],
        input_output_aliases={0: 0} if len(xs) == 1 else {},
        compiler_params=_cparams("arbitrary"),
        name="outproj_router",
    )(*xs, oa, ob, oc, mods, g2, w_out_bf, wr, br, tri)


def _sc_mesh():
    return plsc.VectorSubcoreMesh(core_axis_name="core", subcore_axis_name="subcore")


def sc_gather_rows(table, idx):
    r = idx.shape[0]
    w = table.shape[1]
    half = r // SC_WINDOW // 2
    idx2 = idx.reshape(1, r)

    @functools.partial(pl.kernel, out_type=jax.ShapeDtypeStruct((r, w), table.dtype), mesh=_sc_mesh())
    def k(x_hbm, i_hbm, o_hbm):
        def body(i_vmem, o_vmem):
            pltpu.sync_copy(x_hbm.at[i_vmem.at[0]], o_vmem)

        pltpu.emit_pipeline(
            body,
            grid=(2, half),
            in_specs=[pl.BlockSpec((1, SC_WINDOW), lambda c, i: (0, c * half + i))],
            out_specs=[pl.BlockSpec((SC_WINDOW, w), lambda c, i: (c * half + i, 0),
                                    pipeline_mode=pl.Buffered(1))],
            core_axis_name=("core", "subcore"),
            dimension_semantics=(pltpu.PARALLEL, pltpu.PARALLEL),
        )(i_hbm, o_hbm)

    return k(table, idx2)


def sc_scatter_rows2(rows, idx_a, idx_b, n_out):
    r, w = rows.shape
    half = r // SC_WINDOW // 2
    ia = idx_a.reshape(1, r)
    ib = idx_b.reshape(1, r)

    @functools.partial(pl.kernel, out_type=jax.ShapeDtypeStruct((n_out, w), rows.dtype), mesh=_sc_mesh(),
                       scratch_types=[])
    def k(x_hbm, ia_hbm, ib_hbm, o_hbm):
        def body(x_vmem, ia_vmem, ib_vmem):
            pltpu.sync_copy(x_vmem, o_hbm.at[ia_vmem.at[0]])
            pltpu.sync_copy(x_vmem, o_hbm.at[ib_vmem.at[0]])

        idx_spec = pl.BlockSpec((1, SC_WINDOW), lambda c, i: (0, c * half + i))
        pltpu.emit_pipeline(
            body,
            grid=(2, half),
            in_specs=[pl.BlockSpec((SC_WINDOW, w), lambda c, i: (c * half + i, 0),
                                   pipeline_mode=pl.Buffered(1)),
                      idx_spec, idx_spec],
            out_specs=[],
            core_axis_name=("core", "subcore"),
            dimension_semantics=(pltpu.PARALLEL, pltpu.PARALLEL),
        )(x_hbm, ia_hbm, ib_hbm)

    return k(rows, ia, ib)


def _expert_kernel(te_ref, nv_ref, x_ref, wg_ref, wu_ref, wd_ref, o_ref, wgb_ref, wub_ref, wdb_ref):
    t = pl.program_id(0)
    nvalid = nv_ref[t]

    @pl.when((t == 0) | (te_ref[t] != te_ref[jnp.maximum(t - 1, 0)]))
    def _():
        wgb_ref[...] = wg_ref[0].astype(jnp.bfloat16)
        wub_ref[...] = wu_ref[0].astype(jnp.bfloat16)
        wdb_ref[...] = wd_ref[0].astype(jnp.bfloat16)

    @pl.when(nvalid > 0)
    def _():
        rows = lax.broadcasted_iota(jnp.int32, x_ref.shape, 0)
        lo, hi = _unpack_pairs(jnp.where(rows < nvalid, x_ref[...], 0))
        xb = jnp.concatenate([lo, hi], axis=1).astype(jnp.bfloat16)
        g = jnp.dot(xb, wgb_ref[...], preferred_element_type=jnp.float32)
        u = jnp.dot(xb, wub_ref[...], preferred_element_type=jnp.float32)
        a = (g * jax.nn.sigmoid(g) * u).astype(jnp.bfloat16)
        o_ref[...] = _pack_pairs(jnp.dot(a, wdb_ref[...], preferred_element_type=jnp.float32))

    @pl.when(nvalid == 0)
    def _():
        o_ref[...] = jnp.zeros_like(o_ref)


def _experts(buf, tile_expert, tile_nvalid, wg, wu, wd, *, layer):
    rows, wp = buf.shape
    _, _, d, f = wg.shape
    n_tiles = rows // EXPERT_TILE
    grid_spec = pltpu.PrefetchScalarGridSpec(
        num_scalar_prefetch=2,
        grid=(n_tiles,),
        in_specs=[pl.BlockSpec((EXPERT_TILE, wp), lambda t, te, nv: (t, 0)),
                  pl.BlockSpec((None, 1, d, f), lambda t, te, nv: (layer, te[t], 0, 0)),
                  pl.BlockSpec((None, 1, d, f), lambda t, te, nv: (layer, te[t], 0, 0)),
                  pl.BlockSpec((None, 1, f, d), lambda t, te, nv: (layer, te[t], 0, 0))],
        out_specs=pl.BlockSpec((EXPERT_TILE, wp), lambda t, te, nv: (t, 0)),
        scratch_shapes=[pltpu.VMEM((d, f), jnp.bfloat16), pltpu.VMEM((d, f), jnp.bfloat16),
                        pltpu.VMEM((f, d), jnp.bfloat16)],
    )
    return pl.pallas_call(
        _expert_kernel,
        grid_spec=grid_spec,
        out_shape=jax.ShapeDtypeStruct((rows, wp), jnp.int32),
        compiler_params=_cparams("arbitrary"),
        name="expert_ffn",
    )(tile_expert, tile_nvalid, buf, wg, wu, wd)


def _final_kernel(y0_ref, y1_ref, rw_ref, x_ref, mod_ref, fg_ref, o_ref):
    x = x_ref[...] + mod_ref[0, 5:6, :] * _moe_mix(y0_ref, y1_ref, rw_ref, slice(None))
    r = lax.rsqrt(jnp.mean(x * x, axis=-1, keepdims=True) + EPS)
    o_ref[...] = x * r * fg_ref[...]


def _final_combine(y, rw, xall, mods, final_g, *, tm, n_tiles, s):
    d = xall.shape[1]
    row = lambda w: pl.BlockSpec((tm, w), lambda i: (i, 0))
    return pl.pallas_call(
        _final_kernel,
        grid=(n_tiles,),
        in_specs=[row(d // 2),
                  pl.BlockSpec((tm, d // 2), lambda i: (i + n_tiles, 0)),
                  row(LANES), row(d),
                  pl.BlockSpec((1, N_MOD, d), lambda i: ((i * tm) // s, 0, 0)),
                  pl.BlockSpec(final_g.shape, lambda i: (0, 0))],
        out_specs=row(d),
        out_shape=jax.ShapeDtypeStruct((n_tiles * tm, d), jnp.float32),
        compiler_params=_cparams("arbitrary"),
        name="moe_combine_final",
    )(y, y, rw, xall, mods, final_g)


def _dispatch_plan(meta, counts, n_rows_buf):
    e0, e1, r0, r1 = (meta[k].astype(jnp.int32) for k in range(4))
    cnt = counts[0, :N_EXPERTS].astype(jnp.int32)
    padded = (cnt + EXPERT_TILE - 1) // EXPERT_TILE * EXPERT_TILE
    pends = jnp.cumsum(padded)
    pstarts = pends - padded
    dest0 = pstarts[e0] + r0
    dest1 = pstarts[e1] + r1
    tile_start = jnp.arange(n_rows_buf // EXPERT_TILE, dtype=jnp.int32) * EXPERT_TILE
    te = jnp.sum((tile_start[:, None] >= pends[None, :]).astype(jnp.int32), axis=1)
    te = jnp.minimum(te, N_EXPERTS - 1)
    nvalid = jnp.clip(cnt[te] - (tile_start - pstarts[te]), 0, EXPERT_TILE).astype(jnp.int32)
    return dest0, dest1, te, nvalid


def _rope_tables(s, tm):
    pos = np.arange(s)
    pos_row = jnp.asarray(pos // GRID_W, jnp.float32)
    pos_col = jnp.asarray(pos % GRID_W, jnp.float32)
    n_freq = HEAD_DIM // 4
    inv = ROPE_THETA ** (-jnp.arange(n_freq, dtype=jnp.float32) / n_freq)
    ang_row = pos_row[:, None] * inv
    ang_col = pos_col[:, None] * inv
    ang = jnp.concatenate([ang_row, ang_row, ang_col, ang_col] * (LANES // HEAD_DIM), axis=-1)
    sign = np.where((np.arange(LANES) % 32) < 16, -1.0, 1.0).astype(np.float32)
    cos_t = jnp.concatenate([jnp.cos(ang), jnp.ones((tm, LANES), jnp.float32)], axis=0)
    sin_t = jnp.concatenate([jnp.sin(ang) * sign, jnp.zeros((tm, LANES), jnp.float32)], axis=0)
    return cos_t, sin_t


def kernel(x, c, ctx, c_ctx, norm1_g, norm2_g, w_mod, b_mod, w_in, q_norm_g, k_norm_g, conv_w, conv_b, conv_ln_g, conv_ln_b, sink, w_out, w_group, b_group, w_expert, b_expert, w_gate, w_up, w_down, final_g):
    nb, s, d = x.shape
    l = ctx.shape[1]
    depth = w_in.shape[0]
    assert w_in.shape[2] == D_IN and w_out.shape[1] == D_MIX
    assert s % GRID_W == 0 and s >= 3 * WINDOW and s % WINDOW == 0 and l % WINDOW == 0
    n_lat, n_ctx = nb * s, nb * l
    tm = _pick(np.gcd(s, n_ctx), (512, 256, 128))
    tq = _pick(np.gcd(s, l), (256, 128))
    tc = _pick(np.gcd(s, l), (256, 128))
    assert n_lat % l == 0
    bf = jnp.bfloat16
    f32 = jnp.float32

    xs = (x.reshape(n_lat, d), ctx.reshape(n_ctx, d))
    c_all = jnp.concatenate([c, c_ctx[None, :]], axis=0)
    mods_all = _modulation(c_all, w_mod, b_mod).reshape(depth, nb + 1, N_MOD, d)
    cos_t, sin_t = _rope_tables(s, tm)
    head_id = np.arange(LANES) // HEAD_DIM
    gsum = jnp.asarray((head_id[:, None] == head_id[None, :]) / HEAD_DIM, bf)
    tri = jnp.asarray(np.tril(np.ones((tm, tm), np.float32), -1), bf)
    n_lat_tiles = n_lat // tm

    pending = None
    for i in range(depth):
        last = i == depth - 1
        with_ctx = not last
        mods = mods_all[i]
        qg = jnp.tile(q_norm_g[i], LANES // HEAD_DIM)[None, :]
        kg = jnp.tile(k_norm_g[i], LANES // HEAD_DIM)[None, :]
        x_new, qa, kta, va, hgl, qc, ktc, vc = _inproj(
            xs, mods, norm1_g[i][None, :], w_in[i].astype(bf), gsum, qg, kg, cos_t, sin_t,
            tm=tm, n_lat_tiles=n_lat_tiles, s=s, nb=nb, combine=pending)
        if pending is not None:
            xs = (x_new,)
        sink2 = sink[i] * LOG2E
        conv_args = (conv_w[i].reshape(CONV_W, B_CH), conv_b[i][None, :], conv_ln_g[i][None, :],
                     conv_ln_b[i][None, :])
        oa = _attn_a(qa, kta, va, nb=nb, s=s, l=l, tq=tq)
        oc = _attn_c(qc, ktc, vc, sink2, nb=nb, s=s, l=l)
        ob = _conv(hgl, *conv_args, nb=nb, seq=s, base_rows=0, tc=tc)
        if with_ctx:
            oa, oc = _attn_ctx(qa, kta, va, qc, ktc, vc, sink2, oa, oc, nb=nb, s=s, l=l)
            ob = _conv(hgl, *conv_args, nb=nb, seq=l, base_rows=n_lat, tc=tc, prev_out=ob)

        n_tok = n_lat + n_ctx if with_ctx else n_lat
        n_tiles = n_tok // tm
        wr32 = jnp.zeros((d, LANES), f32).at[:, :N_GROUPS].set(w_group[i])
        wr32 = wr32.at[:, N_GROUPS:N_GROUPS + N_EXPERTS].set(w_expert[i])
        wr_hi = wr32.astype(bf)
        wr = jnp.concatenate([wr_hi, (wr32 - wr_hi.astype(f32)).astype(bf)], axis=1)
        br = jnp.zeros((1, LANES), f32).at[0, :N_GROUPS].set(b_group[i])
        br = br.at[0, N_GROUPS:N_GROUPS + N_EXPERTS].set(b_expert[i])
        xall, hp, meta, rw, counts = _outproj(xs, oa, ob, oc, mods, norm2_g[i][None, :], w_out[i].astype(bf),
                                              wr, br, tri, tm=tm, n_tiles=n_tiles, n_lat_tiles=n_lat_tiles,
                                              s=s, nb=nb)
        n_rows_buf = 2 * n_tok + N_EXPERTS * EXPERT_TILE
        dest0, dest1, te, nvalid = _dispatch_plan(meta, counts, n_rows_buf)
        buf = sc_scatter_rows2(hp, dest0, dest1, n_rows_buf)
        eo = _experts(buf, te, nvalid, w_gate, w_up, w_down, layer=i)
        y = sc_gather_rows(eo, jnp.concatenate([dest0, dest1]))
        xs = (xall,)
        pending = (y, rw, mods)
    y, rw, mods = pending
    out = _final_combine(y, rw, xs[0], mods, final_g[None, :], tm=tm, n_tiles=n_lat // tm, s=s)
    return out.reshape(nb, s, d)
```

```python
import functools

import jax
import jax.numpy as jnp
import numpy as np
from jax import lax
from jax.experimental import pallas as pl
from jax.experimental.pallas import tpu as pltpu
from jax.experimental.pallas import tpu_sc as plsc

HEAD_DIM = 64
GRID_W = 64
ROPE_THETA = 10000.0
A_HEADS, A_KV_HEADS = 6, 2
C_HEADS, C_KV_HEADS = 6, 2
B_CH = 256
CONV_W = 31
WINDOW = 128
N_GROUPS = 4
EXPERTS_PER_GROUP = 8
N_EXPERTS = N_GROUPS * EXPERTS_PER_GROUP
TOP_K = 2
N_MOD = 6
EPS = 1e-6
ATTN_SCALE = HEAD_DIM ** -0.5
LOG2E = 1.4426950408889634
Q_SCALE = ATTN_SCALE * LOG2E

A_Q = A_HEADS * HEAD_DIM
A_KV = A_KV_HEADS * HEAD_DIM
C_Q = C_HEADS * HEAD_DIM
C_KV = C_KV_HEADS * HEAD_DIM
D_MIX = A_Q + B_CH + C_Q
OFF_AQ = 0
OFF_AK = OFF_AQ + A_Q
OFF_AV = OFF_AK + A_KV
OFF_BU = OFF_AV + A_KV
OFF_CQ = OFF_BU + 2 * B_CH
OFF_CK = OFF_CQ + C_Q
OFF_CV = OFF_CK + C_KV
D_IN = OFF_CV + C_KV

LANES = 128
SUBLANES = 8
KT_ROWS = 2 * LANES
V_W = 4 * LANES
EXPERT_TILE = 512
SC_WINDOW = 128
VMEM_LIMIT = 56 * 1024 * 1024
HI_MASK = -65536
ROW_CHAIN = 256


def _cparams(*sem):
    return pltpu.CompilerParams(dimension_semantics=sem, vmem_limit_bytes=VMEM_LIMIT)


def _pick(n, cands):
    for c in cands:
        if n % c == 0:
            return c
    raise ValueError(f"no tile in {cands} divides {n}")


def _pack_pairs(x):
    w = x.shape[1] // 2
    lo = lax.bitcast_convert_type(x[:, :w].astype(jnp.bfloat16).astype(jnp.float32), jnp.int32)
    hi = lax.bitcast_convert_type(x[:, w:].astype(jnp.bfloat16).astype(jnp.float32), jnp.int32)
    return (hi & HI_MASK) | lax.shift_right_logical(lo, 16)


def _unpack_pairs(p):
    lo = lax.bitcast_convert_type(lax.shift_left(p, 16), jnp.float32)
    hi = lax.bitcast_convert_type(p & HI_MASK, jnp.float32)
    return lo, hi


def _mod_kernel(c_ref, w_ref, b_ref, o_ref):
    c = c_ref[...]
    a = c * jax.nn.sigmoid(c)
    o_ref[0] = jnp.dot(a, w_ref[0], preferred_element_type=jnp.float32,
                       precision=lax.Precision.HIGHEST) + b_ref[0]


def _modulation(c_all, w_mod, b_mod):
    depth, d, n = w_mod.shape
    r = c_all.shape[0]
    tn = _pick(n, (1024, 512, 256, 128))
    return pl.pallas_call(
        _mod_kernel,
        grid=(depth, n // tn),
        in_specs=[pl.BlockSpec((r, d), lambda l, j: (0, 0)),
                  pl.BlockSpec((1, d, tn), lambda l, j: (l, 0, j)),
                  pl.BlockSpec((1, 1, tn), lambda l, j: (l, 0, j))],
        out_specs=pl.BlockSpec((1, r, tn), lambda l, j: (l, 0, j)),
        out_shape=jax.ShapeDtypeStruct((depth, r, n), jnp.float32),
        compiler_params=_cparams("arbitrary", "arbitrary"),
        name="modulation",
    )(c_all, w_mod, b_mod.reshape(depth, 1, n))


def _head_mean_sq(blk, gsum):
    sq = blk * blk
    hi = sq.astype(jnp.bfloat16)
    lo = (sq - hi.astype(jnp.float32)).astype(jnp.bfloat16)
    return (jnp.dot(hi, gsum, preferred_element_type=jnp.float32)
            + jnp.dot(lo, gsum, preferred_element_type=jnp.float32))


def _x_specs(xs, tm, n_lat_tiles):
    d = xs[0].shape[1]
    if len(xs) == 1:
        return [pl.BlockSpec((tm, d), lambda i: (i, 0))]
    return [pl.BlockSpec((tm, d), lambda i: (jnp.minimum(i, n_lat_tiles - 1), 0)),
            pl.BlockSpec((tm, d), lambda i: (jnp.maximum(i - n_lat_tiles, 0), 0))]


def _load_x(x_refs, n_lat_tiles, rows=slice(None)):
    if len(x_refs) == 1:
        return x_refs[0][rows, :]
    return jnp.where(pl.program_id(0) < n_lat_tiles, x_refs[0][rows, :], x_refs[1][rows, :])


def _moe_mix(y0_ref, y1_ref, rw_ref, rows):
    rw = rw_ref[rows, :]
    w0 = rw[:, 0:1]
    w1 = rw[:, 1:2]
    a_lo, a_hi = _unpack_pairs(y0_ref[rows, :])
    b_lo, b_hi = _unpack_pairs(y1_ref[rows, :])
    return jnp.concatenate([a_lo * w0 + b_lo * w1, a_hi * w0 + b_hi * w1], axis=1)


def _inproj_kernel(*refs, n_x, n_lat_tiles, fused_combine):
    x_refs = refs[:n_x]
    refs = refs[n_x:]
    if fused_combine:
        y0_ref, y1_ref, rw_ref, modp_ref = refs[:4]
        refs = refs[4:]
        xo_ref = refs[-1]
        refs = refs[:-1]
    (mod_ref, g_ref, w_ref, gsum_ref, qg_ref, kg_ref, cos_ref, sin_ref,
     qa_ref, kta_ref, va_ref, hgl_ref, qc_ref, ktc_ref, vc_ref) = refs
    tm = qa_ref.shape[0]
    tr = min(tm, ROW_CHAIN)
    bf = jnp.bfloat16
    lane = lax.broadcasted_iota(jnp.int32, (tr, LANES), 1)
    first16 = (lane % 32) < 16
    lo64 = lane < HEAD_DIM
    gsum = gsum_ref[...]
    qg = qg_ref[...]
    kg = kg_ref[...]
    scale = g_ref[...] * (1.0 + mod_ref[0, 1:2, :])
    shift = mod_ref[0, 0:1, :]
    ones = jnp.ones((tr, LANES), bf)

    for r0 in range(0, tm, tr):
        rows = slice(r0, r0 + tr)
        x = _load_x(x_refs, n_lat_tiles, rows)
        if fused_combine:
            x = x + modp_ref[0, 5:6, :] * _moe_mix(y0_ref, y1_ref, rw_ref, rows)
            xo_ref[rows, :] = x
        r = lax.rsqrt(jnp.mean(x * x, axis=-1, keepdims=True) + EPS)
        h = (x * r) * scale + shift
        p = jnp.dot(h.astype(bf), w_ref[...], preferred_element_type=jnp.float32)
        cos = cos_ref[rows, :]
        sin = sin_ref[rows, :]

        def blk(off):
            return p[:, off:off + LANES]

        def rope(t):
            sw = jnp.where(first16, pltpu.roll(t, LANES - 16, axis=1), pltpu.roll(t, 16, axis=1))
            return t * cos + sw * sin

        def norm(t, g):
            return t * lax.rsqrt(_head_mean_sq(t, gsum) + EPS) * g

        def dup(t):
            sw = pltpu.roll(t, HEAD_DIM, axis=1)
            return jnp.where(lo64, t, sw), jnp.where(lo64, sw, t)

        def store_kv(kt_ref, v_ref, k_blk, v_blk):
            k0, k1 = dup(k_blk)
            v0, v1 = dup(v_blk)
            kt_ref[0:128, rows] = k0.T.astype(bf)
            kt_ref[128:256, rows] = k1.T.astype(bf)
            v_ref[rows, 0:128] = v0.astype(bf)
            v_ref[rows, 128:256] = ones
            v_ref[rows, 256:384] = v1.astype(bf)
            v_ref[rows, 384:512] = ones

        for i in range(A_Q // LANES):
            t = rope(norm(blk(OFF_AQ + i * LANES), qg)) * Q_SCALE
            qa_ref[rows, i * LANES:(i + 1) * LANES] = t.astype(bf)
        store_kv(kta_ref, va_ref, rope(norm(blk(OFF_AK), kg)), blk(OFF_AV))
        for i in range(B_CH // LANES):
            a = blk(OFF_BU + i * LANES)
            gt = blk(OFF_BU + B_CH + i * LANES)
            hgl_ref[rows, i * LANES:(i + 1) * LANES] = a * jax.nn.sigmoid(gt)
        for i in range(C_Q // LANES):
            t = rope(blk(OFF_CQ + i * LANES)) * Q_SCALE
            qc_ref[rows, i * LANES:(i + 1) * LANES] = t.astype(bf)
        store_kv(ktc_ref, vc_ref, rope(blk(OFF_CK)), blk(OFF_CV))


def _inproj(xs, mods, g1, w_in_bf, gsum, qg, kg, cos_t, sin_t, *, tm, n_lat_tiles, s, nb, combine=None):
    n = sum(a.shape[0] for a in xs)
    d = xs[0].shape[1]
    s_tiles = s // tm
    n_tiles = n // tm

    def bidx(i):
        return jnp.where(i < n_lat_tiles, (i * tm) // s, nb)

    def ridx(i):
        return jnp.where(i < n_lat_tiles, i % s_tiles, s_tiles)

    row = lambda w: pl.BlockSpec((tm, w), lambda i: (i, 0))
    ktspec = pl.BlockSpec((KT_ROWS, tm), lambda i: (0, i))
    const = lambda a: pl.BlockSpec(a.shape, lambda i: (0,) * a.ndim)
    bf = jnp.bfloat16
    modspec = pl.BlockSpec((1, N_MOD, d), lambda i: (bidx(i), 0, 0))
    in_specs = _x_specs(xs, tm, n_lat_tiles)
    args = list(xs)
    out_specs = [row(A_Q), ktspec, row(V_W), row(B_CH), row(C_Q), ktspec, row(V_W)]
    out_shape = [jax.ShapeDtypeStruct((n, A_Q), bf), jax.ShapeDtypeStruct((KT_ROWS, n), bf),
                 jax.ShapeDtypeStruct((n, V_W), bf), jax.ShapeDtypeStruct((n, B_CH), jnp.float32),
                 jax.ShapeDtypeStruct((n, C_Q), bf), jax.ShapeDtypeStruct((KT_ROWS, n), bf),
                 jax.ShapeDtypeStruct((n, V_W), bf)]
    aliases = {}
    if combine is not None:
        y, rw, mods_prev = combine
        assert len(xs) == 1 and y.shape[0] == 2 * n
        in_specs += [row(d // 2), pl.BlockSpec((tm, d // 2), lambda i: (i + n_tiles, 0)), row(LANES), modspec]
        args += [y, y, rw, mods_prev]
        out_specs.append(row(d))
        out_shape.append(jax.ShapeDtypeStruct((n, d), jnp.float32))
        aliases = {0: len(out_shape) - 1}
    in_specs += [modspec, const(g1), const(w_in_bf), const(gsum), const(qg), const(kg),
                 pl.BlockSpec((tm, LANES), lambda i: (ridx(i), 0)),
                 pl.BlockSpec((tm, LANES), lambda i: (ridx(i), 0))]
    args += [mods, g1, w_in_bf, gsum, qg, kg, cos_t, sin_t]
    outs = pl.pallas_call(
        functools.partial(_inproj_kernel, n_x=len(xs), n_lat_tiles=n_lat_tiles,
                          fused_combine=combine is not None),
        grid=(n_tiles,),
        in_specs=in_specs,
        out_specs=out_specs,
        out_shape=out_shape,
        input_output_aliases=aliases,
        compiler_params=_cparams("arbitrary"),
        name="inproj",
    )(*args)
    if combine is not None:
        return (outs[-1],) + tuple(outs[:-1])
    return (None,) + tuple(outs)


def _stack_heads(q_ref, tq):
    lane = lax.broadcasted_iota(jnp.int32, (tq, LANES), 1)
    lo = lane < HEAD_DIM
    qb = [q_ref[:, i * LANES:(i + 1) * LANES] for i in range(3)]
    zero = jnp.zeros_like(qb[0])
    keep_lo = lambda t: jnp.where(lo, t, zero)
    keep_hi = lambda t: jnp.where(lo, zero, t)
    s0 = jnp.concatenate([keep_lo(qb[0]), keep_hi(qb[0]), keep_lo(qb[1])], axis=0)
    s1 = jnp.concatenate([keep_hi(qb[1]), keep_lo(qb[2]), keep_hi(qb[2])], axis=0)
    return s0, s1, lo


def _unstack_store(o_ref, o0, o1, lo, tq):
    bf = jnp.bfloat16
    o_ref[:, 0:128] = jnp.where(lo, o0[0:tq], o0[tq:2 * tq]).astype(bf)
    o_ref[:, 128:256] = jnp.where(lo, o0[2 * tq:3 * tq], o1[0:tq]).astype(bf)
    o_ref[:, 256:384] = jnp.where(lo, o1[tq:2 * tq], o1[2 * tq:3 * tq]).astype(bf)


def _krow(kv):
    return slice(kv * LANES, (kv + 1) * LANES)


def _vcol(kv):
    return slice(2 * kv * LANES, (2 * kv + 2) * LANES)


def _softmax_pv(scores, values, extra=None):
    m = None
    for t in scores:
        for c in range(0, t.shape[1], LANES):
            blk = t[:, c:c + LANES]
            m = blk if m is None else jnp.maximum(m, blk)
    m = m.max(axis=-1, keepdims=True)
    if extra is not None:
        m = jnp.maximum(m, extra)
    acc = None
    for t, v in zip(scores, values):
        c = jnp.dot(jnp.exp2(t - m).astype(jnp.bfloat16), v, preferred_element_type=jnp.float32)
        acc = c if acc is None else acc + c
    den = acc[:, LANES:2 * LANES]
    if extra is not None:
        den = den + jnp.exp2(extra - m)
    return acc[:, 0:LANES] / den


ATTN_CHAIN_ROWS = 128


def _qk(q, kt):
    return jnp.dot(q, kt, preferred_element_type=jnp.float32)


def _attn_a_kernel(q_ref, ktl_ref, ktc_ref, vl_ref, vc_ref, o_ref, *, tq):
    s0, s1, lo = _stack_heads(q_ref, tq)
    outs = []
    for kv, qs in enumerate((s0, s1)):
        krow = _krow(kv)
        vcol = _vcol(kv)
        parts = []
        for r0 in range(0, 3 * tq, ATTN_CHAIN_ROWS):
            qr = qs[r0:r0 + ATTN_CHAIN_ROWS]
            ss = [_qk(qr, ktl_ref[krow, :]), _qk(qr, ktc_ref[krow, :])]
            parts.append(_softmax_pv(ss, [vl_ref[:, vcol], vc_ref[:, vcol]]))
        outs.append(jnp.concatenate(parts, axis=0))
    _unstack_store(o_ref, outs[0], outs[1], lo, tq)


def _kv_specs(nb, s, l):
    ctx0 = nb * s // l
    return [pl.BlockSpec((KT_ROWS, s), lambda b, j: (0, b)),
            pl.BlockSpec((KT_ROWS, l), lambda b, j: (0, ctx0 + b)),
            pl.BlockSpec((s, V_W), lambda b, j: (b, 0)),
            pl.BlockSpec((l, V_W), lambda b, j: (ctx0 + b, 0))]


def _attn_a(qa, kt, v, *, nb, s, l, tq):
    n = qa.shape[0]
    n_q = s // tq
    return pl.pallas_call(
        functools.partial(_attn_a_kernel, tq=tq),
        grid=(nb, n_q),
        in_specs=[pl.BlockSpec((tq, A_Q), lambda b, j: (b * n_q + j, 0))] + _kv_specs(nb, s, l),
        out_specs=pl.BlockSpec((tq, A_Q), lambda b, j: (b * n_q + j, 0)),
        out_shape=jax.ShapeDtypeStruct((n, A_Q), jnp.bfloat16),
        compiler_params=_cparams("arbitrary", "arbitrary"),
        name="attn_global",
    )(qa, kt, kt, v, v)


WIN_BLOCKS = 8


def _sink_column(sink_ref, kv, rows):
    return jnp.concatenate([jnp.full((rows, 1), sink_ref[3 * kv + g], jnp.float32) for g in range(3)], axis=0)


def _attn_c_kernel(sink_ref, bias_ref, q_ref, ktl_ref, ktc_ref, vl_ref, vc_ref, o_ref, *, s, blocks):
    j = pl.program_id(1)
    tq = WINDOW
    band = 3 * WINDOW
    sk = [_sink_column(sink_ref, kv, tq) for kv in range(C_KV_HEADS)]
    for blk in range(blocks):
        jb = j * blocks + blk
        rows = slice(blk * tq, (blk + 1) * tq)
        s0, s1, lo = _stack_heads(q_ref.at[rows, :], tq)
        start = pl.multiple_of(jnp.clip((jb - 1) * WINDOW, 0, s - band), WINDOW)
        bias = bias_ref[jb - start // WINDOW]
        outs = []
        for kv, qs in enumerate((s0, s1)):
            krow = _krow(kv)
            vcol = _vcol(kv)
            sl = _qk(qs, ktl_ref[krow, pl.ds(start, band)]) + bias
            sc = _qk(qs, ktc_ref[krow, :])
            outs.append(_softmax_pv([sl, sc], [vl_ref[pl.ds(start, band), vcol], vc_ref[:, vcol]], sk[kv]))
        _unstack_store(o_ref.at[rows, :], outs[0], outs[1], lo, tq)


def _window_bias():
    r = np.arange(3 * WINDOW)[:, None] % WINDOW
    col = np.arange(3 * WINDOW)[None, :]
    tabs = [np.where(np.abs(col - r - WINDOW * off) <= WINDOW, 0.0, -np.inf) for off in range(3)]
    return jnp.asarray(np.stack(tabs), jnp.float32)


def _attn_c(qc, kt, v, sink2, *, nb, s, l):
    n = qc.shape[0]
    blocks = _pick(s // WINDOW, (WIN_BLOCKS, 4, 2, 1))
    tq = blocks * WINDOW
    n_q = s // tq
    bias = _window_bias()
    return pl.pallas_call(
        functools.partial(_attn_c_kernel, s=s, blocks=blocks),
        grid=(nb, n_q),
        in_specs=[pl.BlockSpec(memory_space=pltpu.SMEM),
                  pl.BlockSpec(bias.shape, lambda b, j: (0, 0, 0)),
                  pl.BlockSpec((tq, C_Q), lambda b, j: (b * n_q + j, 0))] + _kv_specs(nb, s, l),
        out_specs=pl.BlockSpec((tq, C_Q), lambda b, j: (b * n_q + j, 0)),
        out_shape=jax.ShapeDtypeStruct((n, C_Q), jnp.bfloat16),
        compiler_params=_cparams("arbitrary", "arbitrary"),
        name="attn_window",
    )(sink2, bias, qc, kt, kt, v, v)


def _attn_ctx_kernel(sink_ref, qa_ref, kta_ref, va_ref, qc_ref, ktc_ref, vc_ref, oa_in, oc_in, oa_ref, oc_ref, *, l):
    del oa_in, oc_in
    for q_ref, kt_ref, v_ref, o_ref, with_sink in ((qa_ref, kta_ref, va_ref, oa_ref, False),
                                                   (qc_ref, ktc_ref, vc_ref, oc_ref, True)):
        s0, s1, lo = _stack_heads(q_ref, l)
        outs = []
        for kv, qs in enumerate((s0, s1)):
            sk = _sink_column(sink_ref, kv, l) if with_sink else None
            outs.append(_softmax_pv([_qk(qs, kt_ref[_krow(kv), :])], [v_ref[:, _vcol(kv)]], sk))
        _unstack_store(o_ref, outs[0], outs[1], lo, l)


def _attn_ctx(qa, kta, va, qc, ktc, vc, sink2, oa, oc, *, nb, s, l):
    base = nb * s // l
    row = lambda w: pl.BlockSpec((l, w), lambda b: (base + b, 0))
    ktspec = pl.BlockSpec((KT_ROWS, l), lambda b: (0, base + b))
    anyspec = pl.BlockSpec(memory_space=pl.ANY)
    return pl.pallas_call(
        functools.partial(_attn_ctx_kernel, l=l),
        grid=(nb,),
        in_specs=[pl.BlockSpec(memory_space=pltpu.SMEM), row(A_Q), ktspec, row(V_W), row(C_Q), ktspec, row(V_W),
                  anyspec, anyspec],
        out_specs=[row(A_Q), row(C_Q)],
        out_shape=[jax.ShapeDtypeStruct(oa.shape, oa.dtype), jax.ShapeDtypeStruct(oc.shape, oc.dtype)],
        input_output_aliases={7: 0, 8: 1},
        compiler_params=_cparams("arbitrary"),
        name="attn_context",
    )(sink2, qa, kta, va, qc, ktc, vc, oa, oc)


CONV_HALO = 16
CONV_ROWS = 64


def _conv_kernel(prev_ref, cur_ref, next_ref, w_ref, b_ref, g_ref, beta_ref, *rest, chunks, tc):
    o_ref, sh_ref = rest[-2:]
    j = pl.program_id(1)
    has_prev = (j > 0).astype(jnp.float32)
    has_next = (j < chunks - 1).astype(jnp.float32)
    rows = tc + 2 * CONV_HALO
    sh_ref[0, 0:CONV_HALO, :] = prev_ref[tc - CONV_HALO:tc, :] * has_prev
    sh_ref[0, CONV_HALO:CONV_HALO + tc, :] = cur_ref[...]
    sh_ref[0, CONV_HALO + tc:rows, :] = next_ref[0:CONV_HALO, :] * has_next
    for b in range(1, SUBLANES):
        sh_ref[b, 0:rows - SUBLANES, :] = sh_ref[0, b:b + rows - SUBLANES, :]
    base = CONV_HALO - CONV_W // 2
    for r0 in range(0, tc, CONV_ROWS):
        acc = None
        for k in range(CONV_W):
            a, b = divmod(base + k, SUBLANES)
            term = sh_ref[b, SUBLANES * a + r0:SUBLANES * a + r0 + CONV_ROWS, :] * w_ref[k:k + 1, :]
            acc = term if acc is None else acc + term
        hc = acc + b_ref[...]
        mu = jnp.mean(hc, axis=-1, keepdims=True)
        xc = hc - mu
        var = jnp.mean(xc * xc, axis=-1, keepdims=True)
        y = xc * lax.rsqrt(var + EPS) * g_ref[...] + beta_ref[...]
        o_ref[r0:r0 + CONV_ROWS, :] = (y * jax.nn.sigmoid(y)).astype(o_ref.dtype)


def _conv(hgl, w, b, g, beta, *, nb, seq, base_rows, tc, prev_out=None):
    n = hgl.shape[0]
    chunks = seq // tc
    base = base_rows // tc

    def idx(b_, j, delta):
        return base + b_ * chunks + jnp.clip(j + delta, 0, chunks - 1)

    blk = lambda delta: pl.BlockSpec((tc, B_CH), lambda b_, j: (idx(b_, j, delta), 0))
    const = lambda a: pl.BlockSpec(a.shape, lambda b_, j: (0,) * a.ndim)
    in_specs = [blk(-1), blk(0), blk(1), const(w), const(b), const(g), const(beta)]
    args = [hgl, hgl, hgl, w, b, g, beta]
    aliases = {}
    if prev_out is not None:
        in_specs.append(pl.BlockSpec(memory_space=pl.ANY))
        args.append(prev_out)
        aliases = {len(args) - 1: 0}
    return pl.pallas_call(
        functools.partial(_conv_kernel, chunks=chunks, tc=tc),
        grid=(nb, chunks),
        in_specs=in_specs,
        out_specs=blk(0),
        out_shape=jax.ShapeDtypeStruct((n, B_CH), jnp.bfloat16),
        scratch_shapes=[pltpu.VMEM((SUBLANES, tc + 2 * CONV_HALO, B_CH), jnp.float32)],
        input_output_aliases=aliases,
        compiler_params=_cparams("arbitrary", "arbitrary"),
        name="conformer_conv",
    )(*args)


META_ROWS = 8


def _outproj_kernel(*refs, n_x, n_lat_tiles):
    x_refs = refs[:n_x]
    (oa_ref, ob_ref, oc_ref, mod_ref, g_ref, w_ref, wr_ref, br_ref, tri_ref,
     xo_ref, hp_ref, meta_ref, rw_ref, cnt_ref) = refs[n_x:]
    i = pl.program_id(0)
    tm = xo_ref.shape[0]
    f32 = jnp.float32
    tr = tri_ref.shape[0]
    gate = mod_ref[0, 2:3, :]
    scale = g_ref[...] * (1.0 + mod_ref[0, 4:5, :])
    shift = mod_ref[0, 3:4, :]
    lane = lax.broadcasted_iota(jnp.int32, (tr, LANES), 1).astype(f32)
    big = float(LANES)
    ninf = -jnp.inf
    tri = tri_ref[...]

    @pl.when(i == 0)
    def _():
        cnt_ref[...] = jnp.zeros_like(cnt_ref)

    cnt = cnt_ref[0:1, :]
    for r0 in range(0, tm, tr):
        rows = slice(r0, r0 + tr)
        lat = jnp.concatenate([oa_ref[rows, :], ob_ref[rows, :], oc_ref[rows, :]], axis=1)
        mix = jnp.dot(lat, w_ref[...], preferred_element_type=f32)
        x = _load_x(x_refs, n_lat_tiles, rows) + gate * mix
        xo_ref[rows, :] = x
        r = lax.rsqrt(jnp.mean(x * x, axis=-1, keepdims=True) + EPS)
        h = (x * r) * scale + shift
        h_hi = h.astype(jnp.bfloat16)
        hp_ref[rows, :] = _pack_pairs(h_hi)
        h_lo = (h - h_hi.astype(f32)).astype(jnp.bfloat16)
        r_hi = jnp.dot(h_hi, wr_ref[...], preferred_element_type=f32)
        r_lo = jnp.dot(h_lo, wr_ref[:, 0:LANES], preferred_element_type=f32)
        lg = r_hi[:, 0:LANES] + r_hi[:, LANES:2 * LANES] + r_lo + br_ref[...]
        glog = jnp.where(lane < N_GROUPS, lg, ninf)
        gmax = glog.max(axis=-1, keepdims=True)
        g_val = 1.0 / jnp.exp(glog - gmax).sum(axis=-1, keepdims=True)
        g_idx = jnp.where(glog == gmax, lane, big).min(axis=-1, keepdims=True)
        e_lo = N_GROUPS + EXPERTS_PER_GROUP * g_idx
        el = jnp.where((lane >= e_lo) & (lane < e_lo + EXPERTS_PER_GROUP), lg, ninf)
        v0 = el.max(axis=-1, keepdims=True)
        i0 = jnp.where(el == v0, lane, big).min(axis=-1, keepdims=True)
        el1 = jnp.where(lane == i0, ninf, el)
        v1 = el1.max(axis=-1, keepdims=True)
        i1 = jnp.where(el1 == v1, lane, big).min(axis=-1, keepdims=True)
        t = jnp.exp(v1 - v0)
        w0 = g_val / (1.0 + t)
        w1 = g_val * t / (1.0 + t)
        e0 = i0 - N_GROUPS
        e1 = i1 - N_GROUPS

        ranks = []
        for e in (e0, e1):
            oh = lane == e
            ohf = oh.astype(f32)
            pre = jnp.dot(tri, ohf.astype(jnp.bfloat16), preferred_element_type=f32) + cnt
            ranks.append(jnp.where(oh, pre, 0.0).sum(axis=-1, keepdims=True))
            cnt = cnt + ohf.sum(axis=0, keepdims=True)
        rw_ref[rows, :] = jnp.where(lane == 0, w0, jnp.where(lane == 1, w1, 0.0))
        rec = jnp.where(lane == 0, e0, jnp.where(lane == 1, e1, jnp.where(lane == 2, ranks[0],
                        jnp.where(lane == 3, ranks[1], jnp.where(lane == 4, w0, jnp.where(lane == 5, w1, 0.0))))))
        meta_ref[:, rows] = rec.T[0:META_ROWS, :]
    cnt_ref[0:1, :] = cnt


def _outproj(xs, oa, ob, oc, mods, g2, w_out_bf, wr, br, tri, *, tm, n_tiles, n_lat_tiles, s, nb):
    n = sum(a.shape[0] for a in xs)
    d = xs[0].shape[1]
    rows = n_tiles * tm

    def bidx(i):
        return jnp.where(i < n_lat_tiles, (i * tm) // s, nb)

    row = lambda w: pl.BlockSpec((tm, w), lambda i: (i, 0))
    const = lambda a: pl.BlockSpec(a.shape, lambda i: (0,) * a.ndim)
    return pl.pallas_call(
        functools.partial(_outproj_kernel, n_x=len(xs), n_lat_tiles=n_lat_tiles),
        grid=(n_tiles,),
        in_specs=_x_specs(xs, tm, n_lat_tiles) + [
                  row(A_Q), row(B_CH), row(C_Q),
                  pl.BlockSpec((1, N_MOD, d), lambda i: (bidx(i), 0, 0)),
                  const(g2), const(w_out_bf), const(wr), const(br), const(tri)],
        out_specs=[row(d), row(d // 2), pl.BlockSpec((META_ROWS, tm), lambda i: (0, i)), row(LANES),
                   pl.BlockSpec((8, LANES), lambda i: (0, 0))],
        out_shape=[jax.ShapeDtypeStruct((n, d), jnp.float32),
                   jax.ShapeDtypeStruct((rows, d // 2), jnp.int32),
                   jax.ShapeDtypeStruct((META_ROWS, rows), jnp.float32),
                   jax.ShapeDtypeStruct((rows, LANES), jnp.float32),
                   jax.ShapeDtypeStruct((8, LANES), jnp.float32)],
        input_output_aliases={0: 0} if len(xs) == 1 else {},
        compiler_params=_cparams("arbitrary"),
        name="outproj_router",
    )(*xs, oa, ob, oc, mods, g2, w_out_bf, wr, br, tri)


def _sc_mesh():
    return plsc.VectorSubcoreMesh(core_axis_name="core", subcore_axis_name="subcore")


def sc_gather_rows(table, idx):
    r = idx.shape[0]
    w = table.shape[1]
    half = r // SC_WINDOW // 2
    idx2 = idx.reshape(1, r)

    @functools.partial(pl.kernel, out_type=jax.ShapeDtypeStruct((r, w), table.dtype), mesh=_sc_mesh())
    def k(x_hbm, i_hbm, o_hbm):
        def body(i_vmem, o_vmem):
            pltpu.sync_copy(x_hbm.at[i_vmem.at[0]], o_vmem)

        pltpu.emit_pipeline(
            body,
            grid=(2, half),
            in_specs=[pl.BlockSpec((1, SC_WINDOW), lambda c, i: (0, c * half + i))],
            out_specs=[pl.BlockSpec((SC_WINDOW, w), lambda c, i: (c * half + i, 0),
                                    pipeline_mode=pl.Buffered(1))],
            core_axis_name=("core", "subcore"),
            dimension_semantics=(pltpu.PARALLEL, pltpu.PARALLEL),
        )(i_hbm, o_hbm)

    return k(table, idx2)


def sc_scatter_rows2(rows, idx_a, idx_b, n_out):
    r, w = rows.shape
    half = r // SC_WINDOW // 2
    ia = idx_a.reshape(1, r)
    ib = idx_b.reshape(1, r)

    @functools.partial(pl.kernel, out_type=jax.ShapeDtypeStruct((n_out, w), rows.dtype), mesh=_sc_mesh(),
                       scratch_types=[])
    def k(x_hbm, ia_hbm, ib_hbm, o_hbm):
        def body(x_vmem, ia_vmem, ib_vmem):
            pltpu.sync_copy(x_vmem, o_hbm.at[ia_vmem.at[0]])
            pltpu.sync_copy(x_vmem, o_hbm.at[ib_vmem.at[0]])

        idx_spec = pl.BlockSpec((1, SC_WINDOW), lambda c, i: (0, c * half + i))
        pltpu.emit_pipeline(
            body,
            grid=(2, half),
            in_specs=[pl.BlockSpec((SC_WINDOW, w), lambda c, i: (c * half + i, 0),
                                   pipeline_mode=pl.Buffered(1)),
                      idx_spec, idx_spec],
            out_specs=[],
            core_axis_name=("core", "subcore"),
            dimension_semantics=(pltpu.PARALLEL, pltpu.PARALLEL),
        )(x_hbm, ia_hbm, ib_hbm)

    return k(rows, ia, ib)


def _expert_kernel(te_ref, nv_ref, x_ref, wg_ref, wu_ref, wd_ref, o_ref, wgb_ref, wub_ref, wdb_ref):
    t = pl.program_id(0)
    nvalid = nv_ref[t]

    @pl.when((t == 0) | (te_ref[t] != te_ref[jnp.maximum(t - 1, 0)]))
    def _():
        wgb_ref[...] = wg_ref[0].astype(jnp.bfloat16)
        wub_ref[...] = wu_ref[0].astype(jnp.bfloat16)
        wdb_ref[...] = wd_ref[0].astype(jnp.bfloat16)

    @pl.when(nvalid > 0)
    def _():
        rows = lax.broadcasted_iota(jnp.int32, x_ref.shape, 0)
        lo, hi = _unpack_pairs(jnp.where(rows < nvalid, x_ref[...], 0))
        xb = jnp.concatenate([lo, hi], axis=1).astype(jnp.bfloat16)
        g = jnp.dot(xb, wgb_ref[...], preferred_element_type=jnp.float32)
        u = jnp.dot(xb, wub_ref[...], preferred_element_type=jnp.float32)
        a = (g * jax.nn.sigmoid(g) * u).astype(jnp.bfloat16)
        o_ref[...] = _pack_pairs(jnp.dot(a, wdb_ref[...], preferred_element_type=jnp.float32))

    @pl.when(nvalid == 0)
    def _():
        o_ref[...] = jnp.zeros_like(o_ref)


def _experts(buf, tile_expert, tile_nvalid, wg, wu, wd, *, layer):
    rows, wp = buf.shape
    _, _, d, f = wg.shape
    n_tiles = rows // EXPERT_TILE
    grid_spec = pltpu.PrefetchScalarGridSpec(
        num_scalar_prefetch=2,
        grid=(n_tiles,),
        in_specs=[pl.BlockSpec((EXPERT_TILE, wp), lambda t, te, nv: (t, 0)),
                  pl.BlockSpec((None, 1, d, f), lambda t, te, nv: (layer, te[t], 0, 0)),
                  pl.BlockSpec((None, 1, d, f), lambda t, te, nv: (layer, te[t], 0, 0)),
                  pl.BlockSpec((None, 1, f, d), lambda t, te, nv: (layer, te[t], 0, 0))],
        out_specs=pl.BlockSpec((EXPERT_TILE, wp), lambda t, te, nv: (t, 0)),
        scratch_shapes=[pltpu.VMEM((d, f), jnp.bfloat16), pltpu.VMEM((d, f), jnp.bfloat16),
                        pltpu.VMEM((f, d), jnp.bfloat16)],
    )
    return pl.pallas_call(
        _expert_kernel,
        grid_spec=grid_spec,
        out_shape=jax.ShapeDtypeStruct((rows, wp), jnp.int32),
        compiler_params=_cparams("arbitrary"),
        name="expert_ffn",
    )(tile_expert, tile_nvalid, buf, wg, wu, wd)


def _final_kernel(y0_ref, y1_ref, rw_ref, x_ref, mod_ref, fg_ref, o_ref):
    x = x_ref[...] + mod_ref[0, 5:6, :] * _moe_mix(y0_ref, y1_ref, rw_ref, slice(None))
    r = lax.rsqrt(jnp.mean(x * x, axis=-1, keepdims=True) + EPS)
    o_ref[...] = x * r * fg_ref[...]


def _final_combine(y, rw, xall, mods, final_g, *, tm, n_tiles, s):
    d = xall.shape[1]
    row = lambda w: pl.BlockSpec((tm, w), lambda i: (i, 0))
    return pl.pallas_call(
        _final_kernel,
        grid=(n_tiles,),
        in_specs=[row(d // 2),
                  pl.BlockSpec((tm, d // 2), lambda i: (i + n_tiles, 0)),
                  row(LANES), row(d),
                  pl.BlockSpec((1, N_MOD, d), lambda i: ((i * tm) // s, 0, 0)),
                  pl.BlockSpec(final_g.shape, lambda i: (0, 0))],
        out_specs=row(d),
        out_shape=jax.ShapeDtypeStruct((n_tiles * tm, d), jnp.float32),
        compiler_params=_cparams("arbitrary"),
        name="moe_combine_final",
    )(y, y, rw, xall, mods, final_g)


def _dest_kernel(ps_ref, meta_ref, o_ref):
    e = meta_ref[0:TOP_K, :]
    d = meta_ref[TOP_K:2 * TOP_K, :]
    for k in range(N_EXPERTS):
        d = d + jnp.where(e == float(k), ps_ref[k], 0.0)
    o_ref[...] = d.astype(jnp.int32)


def _dest_rows(meta, pstarts, tcols):
    n = meta.shape[1]
    return pl.pallas_call(
        _dest_kernel,
        grid=(n // tcols,),
        in_specs=[pl.BlockSpec(memory_space=pltpu.SMEM),
                  pl.BlockSpec((META_ROWS, tcols), lambda i: (0, i))],
        out_specs=pl.BlockSpec((TOP_K, tcols), lambda i: (0, i)),
        out_shape=jax.ShapeDtypeStruct((TOP_K, n), jnp.int32),
        compiler_params=_cparams("arbitrary"),
        name="moe_dest",
    )(pstarts.astype(jnp.float32), meta)


def _dispatch_plan(meta, counts, n_rows_buf, tcols):
    cnt = counts[0, :N_EXPERTS].astype(jnp.int32)
    padded = (cnt + EXPERT_TILE - 1) // EXPERT_TILE * EXPERT_TILE
    pends = jnp.cumsum(padded)
    pstarts = pends - padded
    dest = _dest_rows(meta, pstarts, tcols)
    dest0, dest1 = dest[0], dest[1]
    tile_start = jnp.arange(n_rows_buf // EXPERT_TILE, dtype=jnp.int32) * EXPERT_TILE
    te = jnp.sum((tile_start[:, None] >= pends[None, :]).astype(jnp.int32), axis=1)
    te = jnp.minimum(te, N_EXPERTS - 1)
    nvalid = jnp.clip(cnt[te] - (tile_start - pstarts[te]), 0, EXPERT_TILE).astype(jnp.int32)
    return dest0, dest1, te, nvalid


def _rope_tables(s, tm):
    pos = np.arange(s)
    pos_row = jnp.asarray(pos // GRID_W, jnp.float32)
    pos_col = jnp.asarray(pos % GRID_W, jnp.float32)
    n_freq = HEAD_DIM // 4
    inv = ROPE_THETA ** (-jnp.arange(n_freq, dtype=jnp.float32) / n_freq)
    ang_row = pos_row[:, None] * inv
    ang_col = pos_col[:, None] * inv
    ang = jnp.concatenate([ang_row, ang_row, ang_col, ang_col] * (LANES // HEAD_DIM), axis=-1)
    sign = np.where((np.arange(LANES) % 32) < 16, -1.0, 1.0).astype(np.float32)
    cos_t = jnp.concatenate([jnp.cos(ang), jnp.ones((tm, LANES), jnp.float32)], axis=0)
    sin_t = jnp.concatenate([jnp.sin(ang) * sign, jnp.zeros((tm, LANES), jnp.float32)], axis=0)
    return cos_t, sin_t


def kernel(x, c, ctx, c_ctx, norm1_g, norm2_g, w_mod, b_mod, w_in, q_norm_g, k_norm_g, conv_w, conv_b, conv_ln_g, conv_ln_b, sink, w_out, w_group, b_group, w_expert, b_expert, w_gate, w_up, w_down, final_g):
    nb, s, d = x.shape
    l = ctx.shape[1]
    depth = w_in.shape[0]
    assert w_in.shape[2] == D_IN and w_out.shape[1] == D_MIX
    assert s % GRID_W == 0 and s >= 3 * WINDOW and s % WINDOW == 0 and l % WINDOW == 0
    n_lat, n_ctx = nb * s, nb * l
    tm = _pick(np.gcd(s, n_ctx), (512, 256, 128))
    tq = _pick(np.gcd(s, l), (256, 128))
    tc = _pick(np.gcd(s, l), (256, 128))
    assert n_lat % l == 0
    bf = jnp.bfloat16
    f32 = jnp.float32

    xs = (x.reshape(n_lat, d), ctx.reshape(n_ctx, d))
    c_all = jnp.concatenate([c, c_ctx[None, :]], axis=0)
    mods_all = _modulation(c_all, w_mod, b_mod).reshape(depth, nb + 1, N_MOD, d)
    cos_t, sin_t = _rope_tables(s, tm)
    head_id = np.arange(LANES) // HEAD_DIM
    gsum = jnp.asarray((head_id[:, None] == head_id[None, :]) / HEAD_DIM, bf)
    tri = jnp.asarray(np.tril(np.ones((tm, tm), np.float32), -1), bf)
    n_lat_tiles = n_lat // tm

    pending = None
    for i in range(depth):
        last = i == depth - 1
        with_ctx = not last
        mods = mods_all[i]
        qg = jnp.tile(q_norm_g[i], LANES // HEAD_DIM)[None, :]
        kg = jnp.tile(k_norm_g[i], LANES // HEAD_DIM)[None, :]
        x_new, qa, kta, va, hgl, qc, ktc, vc = _inproj(
            xs, mods, norm1_g[i][None, :], w_in[i].astype(bf), gsum, qg, kg, cos_t, sin_t,
            tm=tm, n_lat_tiles=n_lat_tiles, s=s, nb=nb, combine=pending)
        if pending is not None:
            xs = (x_new,)
        sink2 = sink[i] * LOG2E
        conv_args = (conv_w[i].reshape(CONV_W, B_CH), conv_b[i][None, :], conv_ln_g[i][None, :],
                     conv_ln_b[i][None, :])
        oa = _attn_a(qa, kta, va, nb=nb, s=s, l=l, tq=tq)
        oc = _attn_c(qc, ktc, vc, sink2, nb=nb, s=s, l=l)
        ob = _conv(hgl, *conv_args, nb=nb, seq=s, base_rows=0, tc=tc)
        if with_ctx:
            oa, oc = _attn_ctx(qa, kta, va, qc, ktc, vc, sink2, oa, oc, nb=nb, s=s, l=l)
            ob = _conv(hgl, *conv_args, nb=nb, seq=l, base_rows=n_lat, tc=tc, prev_out=ob)

        n_tok = n_lat + n_ctx if with_ctx else n_lat
        n_tiles = n_tok // tm
        wr32 = jnp.zeros((d, LANES), f32).at[:, :N_GROUPS].set(w_group[i])
        wr32 = wr32.at[:, N_GROUPS:N_GROUPS + N_EXPERTS].set(w_expert[i])
        wr_hi = wr32.astype(bf)
        wr = jnp.concatenate([wr_hi, (wr32 - wr_hi.astype(f32)).astype(bf)], axis=1)
        br = jnp.zeros((1, LANES), f32).at[0, :N_GROUPS].set(b_group[i])
        br = br.at[0, N_GROUPS:N_GROUPS + N_EXPERTS].set(b_expert[i])
        xall, hp, meta, rw, counts = _outproj(xs, oa, ob, oc, mods, norm2_g[i][None, :], w_out[i].astype(bf),
                                              wr, br, tri, tm=tm, n_tiles=n_tiles, n_lat_tiles=n_lat_tiles,
                                              s=s, nb=nb)
        n_rows_buf = 2 * n_tok + N_EXPERTS * EXPERT_TILE
        dest0, dest1, te, nvalid = _dispatch_plan(meta, counts, n_rows_buf,
                                                  _pick(n_tok, (8192, 4096, 2048, 1024, 512, 256, 128)))
        buf = sc_scatter_rows2(hp, dest0, dest1, n_rows_buf)
        eo = _experts(buf, te, nvalid, w_gate, w_up, w_down, layer=i)
        y = sc_gather_rows(eo, jnp.concatenate([dest0, dest1]))
        xs = (xall,)
        pending = (y, rw, mods)
    y, rw, mods = pending
    out = _final_combine(y, rw, xs[0], mods, final_g[None, :], tm=tm, n_tiles=n_lat // tm, s=s)
    return out.reshape(nb, s, d)
```

```python
import functools

import jax
import jax.numpy as jnp
import numpy as np
from jax import lax
from jax.experimental import pallas as pl
from jax.experimental.pallas import tpu as pltpu
from jax.experimental.pallas import tpu_sc as plsc

HEAD_DIM = 64
GRID_W = 64
ROPE_THETA = 10000.0
A_HEADS, A_KV_HEADS = 6, 2
C_HEADS, C_KV_HEADS = 6, 2
B_CH = 256
CONV_W = 31
WINDOW = 128
N_GROUPS = 4
EXPERTS_PER_GROUP = 8
N_EXPERTS = N_GROUPS * EXPERTS_PER_GROUP
TOP_K = 2
N_MOD = 6
EPS = 1e-6
ATTN_SCALE = HEAD_DIM ** -0.5
LOG2E = 1.4426950408889634
Q_SCALE = ATTN_SCALE * LOG2E

A_Q = A_HEADS * HEAD_DIM
A_KV = A_KV_HEADS * HEAD_DIM
C_Q = C_HEADS * HEAD_DIM
C_KV = C_KV_HEADS * HEAD_DIM
D_MIX = A_Q + B_CH + C_Q
OFF_AQ = 0
OFF_AK = OFF_AQ + A_Q
OFF_AV = OFF_AK + A_KV
OFF_BU = OFF_AV + A_KV
OFF_CQ = OFF_BU + 2 * B_CH
OFF_CK = OFF_CQ + C_Q
OFF_CV = OFF_CK + C_KV
D_IN = OFF_CV + C_KV

LANES = 128
SUBLANES = 8
KT_ROWS = 2 * LANES
V_W = 4 * LANES
EXPERT_TILE = 512
SC_WINDOW = 128
VMEM_LIMIT = 56 * 1024 * 1024
HI_MASK = -65536
BATCH_GROUPS = 2
ROW_CHAIN = 256


def _cparams(*sem):
    return pltpu.CompilerParams(dimension_semantics=sem, vmem_limit_bytes=VMEM_LIMIT)


def _pick(n, cands):
    for c in cands:
        if n % c == 0:
            return c
    raise ValueError(f"no tile in {cands} divides {n}")


def _pack_pairs(x):
    w = x.shape[1] // 2
    lo = lax.bitcast_convert_type(x[:, :w].astype(jnp.bfloat16).astype(jnp.float32), jnp.int32)
    hi = lax.bitcast_convert_type(x[:, w:].astype(jnp.bfloat16).astype(jnp.float32), jnp.int32)
    return (hi & HI_MASK) | lax.shift_right_logical(lo, 16)


def _unpack_pairs(p):
    lo = lax.bitcast_convert_type(lax.shift_left(p, 16), jnp.float32)
    hi = lax.bitcast_convert_type(p & HI_MASK, jnp.float32)
    return lo, hi


def _mod_kernel(c_ref, w_ref, b_ref, o_ref):
    c = c_ref[...]
    a = c * jax.nn.sigmoid(c)
    o_ref[0] = jnp.dot(a, w_ref[0], preferred_element_type=jnp.float32,
                       precision=lax.Precision.HIGHEST) + b_ref[0]


def _modulation(c_all, w_mod, b_mod):
    depth, d, n = w_mod.shape
    r = c_all.shape[0]
    tn = _pick(n, (1024, 512, 256, 128))
    return pl.pallas_call(
        _mod_kernel,
        grid=(depth, n // tn),
        in_specs=[pl.BlockSpec((r, d), lambda l, j: (0, 0)),
                  pl.BlockSpec((1, d, tn), lambda l, j: (l, 0, j)),
                  pl.BlockSpec((1, 1, tn), lambda l, j: (l, 0, j))],
        out_specs=pl.BlockSpec((1, r, tn), lambda l, j: (l, 0, j)),
        out_shape=jax.ShapeDtypeStruct((depth, r, n), jnp.float32),
        compiler_params=_cparams("arbitrary", "arbitrary"),
        name="modulation",
    )(c_all, w_mod, b_mod.reshape(depth, 1, n))


def _head_mean_sq(blk, gsum):
    sq = blk * blk
    hi = sq.astype(jnp.bfloat16)
    lo = (sq - hi.astype(jnp.float32)).astype(jnp.bfloat16)
    return (jnp.dot(hi, gsum, preferred_element_type=jnp.float32)
            + jnp.dot(lo, gsum, preferred_element_type=jnp.float32))


def _x_specs(xs, tm, n_lat_tiles, offsets=(0, 0)):
    d = xs[0].shape[1]
    if len(xs) == 1:
        return [pl.BlockSpec((tm, d), lambda i: (i, 0))]
    lat0, ctx0 = offsets
    return [pl.BlockSpec((tm, d), lambda i: (lat0 + jnp.minimum(i, n_lat_tiles - 1), 0)),
            pl.BlockSpec((tm, d), lambda i: (ctx0 + jnp.maximum(i - n_lat_tiles, 0), 0))]


def _load_x(x_refs, n_lat_tiles, rows=slice(None)):
    if len(x_refs) == 1:
        return x_refs[0][rows, :]
    return jnp.where(pl.program_id(0) < n_lat_tiles, x_refs[0][rows, :], x_refs[1][rows, :])


def _moe_mix(y0_ref, y1_ref, rw_ref, rows):
    rw = rw_ref[rows, :]
    w0 = rw[:, 0:1]
    w1 = rw[:, 1:2]
    a_lo, a_hi = _unpack_pairs(y0_ref[rows, :])
    b_lo, b_hi = _unpack_pairs(y1_ref[rows, :])
    return jnp.concatenate([a_lo * w0 + b_lo * w1, a_hi * w0 + b_hi * w1], axis=1)


def _inproj_kernel(*refs, n_x, n_lat_tiles, fused_combine):
    x_refs = refs[:n_x]
    refs = refs[n_x:]
    if fused_combine:
        y0_ref, y1_ref, rw_ref, modp_ref = refs[:4]
        refs = refs[4:]
        xo_ref = refs[-1]
        refs = refs[:-1]
    (mod_ref, g_ref, w_ref, gsum_ref, qg_ref, kg_ref, cos_ref, sin_ref,
     qa_ref, kta_ref, va_ref, hgl_ref, qc_ref, ktc_ref, vc_ref) = refs
    tm = qa_ref.shape[0]
    tr = min(tm, ROW_CHAIN)
    bf = jnp.bfloat16
    lane = lax.broadcasted_iota(jnp.int32, (tr, LANES), 1)
    first16 = (lane % 32) < 16
    lo64 = lane < HEAD_DIM
    gsum = gsum_ref[...]
    qg = qg_ref[...]
    kg = kg_ref[...]
    scale = g_ref[...] * (1.0 + mod_ref[0, 1:2, :])
    shift = mod_ref[0, 0:1, :]
    ones = jnp.ones((tr, LANES), bf)

    for r0 in range(0, tm, tr):
        rows = slice(r0, r0 + tr)
        x = _load_x(x_refs, n_lat_tiles, rows)
        if fused_combine:
            x = x + modp_ref[0, 5:6, :] * _moe_mix(y0_ref, y1_ref, rw_ref, rows)
            xo_ref[rows, :] = x
        r = lax.rsqrt(jnp.mean(x * x, axis=-1, keepdims=True) + EPS)
        h = (x * r) * scale + shift
        p = jnp.dot(h.astype(bf), w_ref[...], preferred_element_type=jnp.float32)
        cos = cos_ref[rows, :]
        sin = sin_ref[rows, :]

        def blk(off):
            return p[:, off:off + LANES]

        def rope(t):
            sw = jnp.where(first16, pltpu.roll(t, LANES - 16, axis=1), pltpu.roll(t, 16, axis=1))
            return t * cos + sw * sin

        def norm(t, g):
            return t * lax.rsqrt(_head_mean_sq(t, gsum) + EPS) * g

        def dup(t):
            sw = pltpu.roll(t, HEAD_DIM, axis=1)
            return jnp.where(lo64, t, sw), jnp.where(lo64, sw, t)

        def store_kv(kt_ref, v_ref, k_blk, v_blk):
            k0, k1 = dup(k_blk)
            v0, v1 = dup(v_blk)
            kt_ref[0:128, rows] = k0.T.astype(bf)
            kt_ref[128:256, rows] = k1.T.astype(bf)
            v_ref[rows, 0:128] = v0.astype(bf)
            v_ref[rows, 128:256] = ones
            v_ref[rows, 256:384] = v1.astype(bf)
            v_ref[rows, 384:512] = ones

        for i in range(A_Q // LANES):
            t = rope(norm(blk(OFF_AQ + i * LANES), qg)) * Q_SCALE
            qa_ref[rows, i * LANES:(i + 1) * LANES] = t.astype(bf)
        store_kv(kta_ref, va_ref, rope(norm(blk(OFF_AK), kg)), blk(OFF_AV))
        for i in range(B_CH // LANES):
            a = blk(OFF_BU + i * LANES)
            gt = blk(OFF_BU + B_CH + i * LANES)
            hgl_ref[rows, i * LANES:(i + 1) * LANES] = a * jax.nn.sigmoid(gt)
        for i in range(C_Q // LANES):
            t = rope(blk(OFF_CQ + i * LANES)) * Q_SCALE
            qc_ref[rows, i * LANES:(i + 1) * LANES] = t.astype(bf)
        store_kv(ktc_ref, vc_ref, rope(blk(OFF_CK)), blk(OFF_CV))


def _inproj(xs, mods, g1, w_in_bf, gsum, qg, kg, cos_t, sin_t, *, n, tm, n_lat_tiles, s, nb, x_offsets=(0, 0),
            combine=None):
    d = xs[0].shape[1]
    s_tiles = s // tm
    n_tiles = n // tm

    def bidx(i):
        return jnp.where(i < n_lat_tiles, (i * tm) // s, nb)

    def ridx(i):
        return jnp.where(i < n_lat_tiles, i % s_tiles, s_tiles)

    row = lambda w: pl.BlockSpec((tm, w), lambda i: (i, 0))
    ktspec = pl.BlockSpec((KT_ROWS, tm), lambda i: (0, i))
    const = lambda a: pl.BlockSpec(a.shape, lambda i: (0,) * a.ndim)
    bf = jnp.bfloat16
    modspec = pl.BlockSpec((1, N_MOD, d), lambda i: (bidx(i), 0, 0))
    in_specs = _x_specs(xs, tm, n_lat_tiles, x_offsets)
    args = list(xs)
    out_specs = [row(A_Q), ktspec, row(V_W), row(B_CH), row(C_Q), ktspec, row(V_W)]
    out_shape = [jax.ShapeDtypeStruct((n, A_Q), bf), jax.ShapeDtypeStruct((KT_ROWS, n), bf),
                 jax.ShapeDtypeStruct((n, V_W), bf), jax.ShapeDtypeStruct((n, B_CH), jnp.float32),
                 jax.ShapeDtypeStruct((n, C_Q), bf), jax.ShapeDtypeStruct((KT_ROWS, n), bf),
                 jax.ShapeDtypeStruct((n, V_W), bf)]
    aliases = {}
    if combine is not None:
        y, rw, mods_prev = combine
        assert len(xs) == 1 and y.shape[0] == 2 * n
        in_specs += [row(d // 2), pl.BlockSpec((tm, d // 2), lambda i: (i + n_tiles, 0)), row(LANES), modspec]
        args += [y, y, rw, mods_prev]
        out_specs.append(row(d))
        out_shape.append(jax.ShapeDtypeStruct((n, d), jnp.float32))
        aliases = {0: len(out_shape) - 1}
    in_specs += [modspec, const(g1), const(w_in_bf), const(gsum), const(qg), const(kg),
                 pl.BlockSpec((tm, LANES), lambda i: (ridx(i), 0)),
                 pl.BlockSpec((tm, LANES), lambda i: (ridx(i), 0))]
    args += [mods, g1, w_in_bf, gsum, qg, kg, cos_t, sin_t]
    outs = pl.pallas_call(
        functools.partial(_inproj_kernel, n_x=len(xs), n_lat_tiles=n_lat_tiles,
                          fused_combine=combine is not None),
        grid=(n_tiles,),
        in_specs=in_specs,
        out_specs=out_specs,
        out_shape=out_shape,
        input_output_aliases=aliases,
        compiler_params=_cparams("arbitrary"),
        name="inproj",
    )(*args)
    if combine is not None:
        return (outs[-1],) + tuple(outs[:-1])
    return (None,) + tuple(outs)


def _stack_heads(q_ref, tq):
    lane = lax.broadcasted_iota(jnp.int32, (tq, LANES), 1)
    lo = lane < HEAD_DIM
    qb = [q_ref[:, i * LANES:(i + 1) * LANES] for i in range(3)]
    zero = jnp.zeros_like(qb[0])
    keep_lo = lambda t: jnp.where(lo, t, zero)
    keep_hi = lambda t: jnp.where(lo, zero, t)
    s0 = jnp.concatenate([keep_lo(qb[0]), keep_hi(qb[0]), keep_lo(qb[1])], axis=0)
    s1 = jnp.concatenate([keep_hi(qb[1]), keep_lo(qb[2]), keep_hi(qb[2])], axis=0)
    return s0, s1, lo


def _unstack_store(o_ref, o0, o1, lo, tq):
    bf = jnp.bfloat16
    o_ref[:, 0:128] = jnp.where(lo, o0[0:tq], o0[tq:2 * tq]).astype(bf)
    o_ref[:, 128:256] = jnp.where(lo, o0[2 * tq:3 * tq], o1[0:tq]).astype(bf)
    o_ref[:, 256:384] = jnp.where(lo, o1[tq:2 * tq], o1[2 * tq:3 * tq]).astype(bf)


def _krow(kv):
    return slice(kv * LANES, (kv + 1) * LANES)


def _vcol(kv):
    return slice(2 * kv * LANES, (2 * kv + 2) * LANES)


def _softmax_pv(scores, values, extra=None):
    m = None
    for t in scores:
        for c in range(0, t.shape[1], LANES):
            blk = t[:, c:c + LANES]
            m = blk if m is None else jnp.maximum(m, blk)
    m = m.max(axis=-1, keepdims=True)
    if extra is not None:
        m = jnp.maximum(m, extra)
    acc = None
    for t, v in zip(scores, values):
        c = jnp.dot(jnp.exp2(t - m).astype(jnp.bfloat16), v, preferred_element_type=jnp.float32)
        acc = c if acc is None else acc + c
    den = acc[:, LANES:2 * LANES]
    if extra is not None:
        den = den + jnp.exp2(extra - m)
    return acc[:, 0:LANES] / den


ATTN_CHAIN_ROWS = 128


def _qk(q, kt):
    return jnp.dot(q, kt, preferred_element_type=jnp.float32)


def _attn_a_kernel(q_ref, ktl_ref, ktc_ref, vl_ref, vc_ref, o_ref, *, tq):
    s0, s1, lo = _stack_heads(q_ref, tq)
    outs = []
    for kv, qs in enumerate((s0, s1)):
        krow = _krow(kv)
        vcol = _vcol(kv)
        parts = []
        for r0 in range(0, 3 * tq, ATTN_CHAIN_ROWS):
            qr = qs[r0:r0 + ATTN_CHAIN_ROWS]
            ss = [_qk(qr, ktl_ref[krow, :]), _qk(qr, ktc_ref[krow, :])]
            parts.append(_softmax_pv(ss, [vl_ref[:, vcol], vc_ref[:, vcol]]))
        outs.append(jnp.concatenate(parts, axis=0))
    _unstack_store(o_ref, outs[0], outs[1], lo, tq)


def _kv_specs(nb, s, l):
    ctx0 = nb * s // l
    return [pl.BlockSpec((KT_ROWS, s), lambda b, j: (0, b)),
            pl.BlockSpec((KT_ROWS, l), lambda b, j: (0, ctx0 + b)),
            pl.BlockSpec((s, V_W), lambda b, j: (b, 0)),
            pl.BlockSpec((l, V_W), lambda b, j: (ctx0 + b, 0))]


def _attn_a(qa, kt, v, *, nb, s, l, tq):
    n = qa.shape[0]
    n_q = s // tq
    return pl.pallas_call(
        functools.partial(_attn_a_kernel, tq=tq),
        grid=(nb, n_q),
        in_specs=[pl.BlockSpec((tq, A_Q), lambda b, j: (b * n_q + j, 0))] + _kv_specs(nb, s, l),
        out_specs=pl.BlockSpec((tq, A_Q), lambda b, j: (b * n_q + j, 0)),
        out_shape=jax.ShapeDtypeStruct((n, A_Q), jnp.bfloat16),
        compiler_params=_cparams("arbitrary", "arbitrary"),
        name="attn_global",
    )(qa, kt, kt, v, v)


WIN_BLOCKS = 8


def _sink_column(sink_ref, kv, rows):
    return jnp.concatenate([jnp.full((rows, 1), sink_ref[3 * kv + g], jnp.float32) for g in range(3)], axis=0)


def _attn_c_kernel(sink_ref, bias_ref, q_ref, ktl_ref, ktc_ref, vl_ref, vc_ref, o_ref, *, s, blocks):
    j = pl.program_id(1)
    tq = WINDOW
    band = 3 * WINDOW
    sk = [_sink_column(sink_ref, kv, tq) for kv in range(C_KV_HEADS)]
    for blk in range(blocks):
        jb = j * blocks + blk
        rows = slice(blk * tq, (blk + 1) * tq)
        s0, s1, lo = _stack_heads(q_ref.at[rows, :], tq)
        start = pl.multiple_of(jnp.clip((jb - 1) * WINDOW, 0, s - band), WINDOW)
        bias = bias_ref[jb - start // WINDOW]
        outs = []
        for kv, qs in enumerate((s0, s1)):
            krow = _krow(kv)
            vcol = _vcol(kv)
            sl = _qk(qs, ktl_ref[krow, pl.ds(start, band)]) + bias
            sc = _qk(qs, ktc_ref[krow, :])
            outs.append(_softmax_pv([sl, sc], [vl_ref[pl.ds(start, band), vcol], vc_ref[:, vcol]], sk[kv]))
        _unstack_store(o_ref.at[rows, :], outs[0], outs[1], lo, tq)


def _window_bias():
    r = np.arange(3 * WINDOW)[:, None] % WINDOW
    col = np.arange(3 * WINDOW)[None, :]
    tabs = [np.where(np.abs(col - r - WINDOW * off) <= WINDOW, 0.0, -np.inf) for off in range(3)]
    return jnp.asarray(np.stack(tabs), jnp.float32)


def _attn_c(qc, kt, v, sink2, *, nb, s, l):
    n = qc.shape[0]
    blocks = _pick(s // WINDOW, (WIN_BLOCKS, 4, 2, 1))
    tq = blocks * WINDOW
    n_q = s // tq
    bias = _window_bias()
    return pl.pallas_call(
        functools.partial(_attn_c_kernel, s=s, blocks=blocks),
        grid=(nb, n_q),
        in_specs=[pl.BlockSpec(memory_space=pltpu.SMEM),
                  pl.BlockSpec(bias.shape, lambda b, j: (0, 0, 0)),
                  pl.BlockSpec((tq, C_Q), lambda b, j: (b * n_q + j, 0))] + _kv_specs(nb, s, l),
        out_specs=pl.BlockSpec((tq, C_Q), lambda b, j: (b * n_q + j, 0)),
        out_shape=jax.ShapeDtypeStruct((n, C_Q), jnp.bfloat16),
        compiler_params=_cparams("arbitrary", "arbitrary"),
        name="attn_window",
    )(sink2, bias, qc, kt, kt, v, v)


def _attn_ctx_kernel(sink_ref, qa_ref, kta_ref, va_ref, qc_ref, ktc_ref, vc_ref, oa_in, oc_in, oa_ref, oc_ref, *, l):
    del oa_in, oc_in
    for q_ref, kt_ref, v_ref, o_ref, with_sink in ((qa_ref, kta_ref, va_ref, oa_ref, False),
                                                   (qc_ref, ktc_ref, vc_ref, oc_ref, True)):
        s0, s1, lo = _stack_heads(q_ref, l)
        outs = []
        for kv, qs in enumerate((s0, s1)):
            sk = _sink_column(sink_ref, kv, l) if with_sink else None
            outs.append(_softmax_pv([_qk(qs, kt_ref[_krow(kv), :])], [v_ref[:, _vcol(kv)]], sk))
        _unstack_store(o_ref, outs[0], outs[1], lo, l)


def _attn_ctx(qa, kta, va, qc, ktc, vc, sink2, oa, oc, *, nb, s, l):
    base = nb * s // l
    row = lambda w: pl.BlockSpec((l, w), lambda b: (base + b, 0))
    ktspec = pl.BlockSpec((KT_ROWS, l), lambda b: (0, base + b))
    anyspec = pl.BlockSpec(memory_space=pl.ANY)
    return pl.pallas_call(
        functools.partial(_attn_ctx_kernel, l=l),
        grid=(nb,),
        in_specs=[pl.BlockSpec(memory_space=pltpu.SMEM), row(A_Q), ktspec, row(V_W), row(C_Q), ktspec, row(V_W),
                  anyspec, anyspec],
        out_specs=[row(A_Q), row(C_Q)],
        out_shape=[jax.ShapeDtypeStruct(oa.shape, oa.dtype), jax.ShapeDtypeStruct(oc.shape, oc.dtype)],
        input_output_aliases={7: 0, 8: 1},
        compiler_params=_cparams("arbitrary"),
        name="attn_context",
    )(sink2, qa, kta, va, qc, ktc, vc, oa, oc)


CONV_HALO = 16
CONV_ROWS = 64


def _conv_kernel(prev_ref, cur_ref, next_ref, w_ref, b_ref, g_ref, beta_ref, *rest, chunks, tc):
    o_ref, sh_ref = rest[-2:]
    j = pl.program_id(1)
    has_prev = (j > 0).astype(jnp.float32)
    has_next = (j < chunks - 1).astype(jnp.float32)
    rows = tc + 2 * CONV_HALO
    sh_ref[0, 0:CONV_HALO, :] = prev_ref[tc - CONV_HALO:tc, :] * has_prev
    sh_ref[0, CONV_HALO:CONV_HALO + tc, :] = cur_ref[...]
    sh_ref[0, CONV_HALO + tc:rows, :] = next_ref[0:CONV_HALO, :] * has_next
    for b in range(1, SUBLANES):
        sh_ref[b, 0:rows - SUBLANES, :] = sh_ref[0, b:b + rows - SUBLANES, :]
    base = CONV_HALO - CONV_W // 2
    for r0 in range(0, tc, CONV_ROWS):
        acc = None
        for k in range(CONV_W):
            a, b = divmod(base + k, SUBLANES)
            term = sh_ref[b, SUBLANES * a + r0:SUBLANES * a + r0 + CONV_ROWS, :] * w_ref[k:k + 1, :]
            acc = term if acc is None else acc + term
        hc = acc + b_ref[...]
        mu = jnp.mean(hc, axis=-1, keepdims=True)
        xc = hc - mu
        var = jnp.mean(xc * xc, axis=-1, keepdims=True)
        y = xc * lax.rsqrt(var + EPS) * g_ref[...] + beta_ref[...]
        o_ref[r0:r0 + CONV_ROWS, :] = (y * jax.nn.sigmoid(y)).astype(o_ref.dtype)


def _conv(hgl, w, b, g, beta, *, nb, seq, base_rows, tc, prev_out=None):
    n = hgl.shape[0]
    chunks = seq // tc
    base = base_rows // tc

    def idx(b_, j, delta):
        return base + b_ * chunks + jnp.clip(j + delta, 0, chunks - 1)

    blk = lambda delta: pl.BlockSpec((tc, B_CH), lambda b_, j: (idx(b_, j, delta), 0))
    const = lambda a: pl.BlockSpec(a.shape, lambda b_, j: (0,) * a.ndim)
    in_specs = [blk(-1), blk(0), blk(1), const(w), const(b), const(g), const(beta)]
    args = [hgl, hgl, hgl, w, b, g, beta]
    aliases = {}
    if prev_out is not None:
        in_specs.append(pl.BlockSpec(memory_space=pl.ANY))
        args.append(prev_out)
        aliases = {len(args) - 1: 0}
    return pl.pallas_call(
        functools.partial(_conv_kernel, chunks=chunks, tc=tc),
        grid=(nb, chunks),
        in_specs=in_specs,
        out_specs=blk(0),
        out_shape=jax.ShapeDtypeStruct((n, B_CH), jnp.bfloat16),
        scratch_shapes=[pltpu.VMEM((SUBLANES, tc + 2 * CONV_HALO, B_CH), jnp.float32)],
        input_output_aliases=aliases,
        compiler_params=_cparams("arbitrary", "arbitrary"),
        name="conformer_conv",
    )(*args)


META_ROWS = 8


def _outproj_kernel(*refs, n_x, n_lat_tiles):
    x_refs = refs[:n_x]
    (oa_ref, ob_ref, oc_ref, mod_ref, g_ref, w_ref, wr_ref, br_ref, tri_ref,
     xo_ref, hp_ref, meta_ref, rw_ref, cnt_ref) = refs[n_x:]
    i = pl.program_id(0)
    tm = xo_ref.shape[0]
    f32 = jnp.float32
    tr = tri_ref.shape[0]
    gate = mod_ref[0, 2:3, :]
    scale = g_ref[...] * (1.0 + mod_ref[0, 4:5, :])
    shift = mod_ref[0, 3:4, :]
    lane = lax.broadcasted_iota(jnp.int32, (tr, LANES), 1).astype(f32)
    big = float(LANES)
    ninf = -jnp.inf
    tri = tri_ref[...]

    @pl.when(i == 0)
    def _():
        cnt_ref[...] = jnp.zeros_like(cnt_ref)

    cnt = cnt_ref[0:1, :]
    for r0 in range(0, tm, tr):
        rows = slice(r0, r0 + tr)
        lat = jnp.concatenate([oa_ref[rows, :], ob_ref[rows, :], oc_ref[rows, :]], axis=1)
        mix = jnp.dot(lat, w_ref[...], preferred_element_type=f32)
        x = _load_x(x_refs, n_lat_tiles, rows) + gate * mix
        xo_ref[rows, :] = x
        r = lax.rsqrt(jnp.mean(x * x, axis=-1, keepdims=True) + EPS)
        h = (x * r) * scale + shift
        h_hi = h.astype(jnp.bfloat16)
        hp_ref[rows, :] = _pack_pairs(h_hi)
        h_lo = (h - h_hi.astype(f32)).astype(jnp.bfloat16)
        r_hi = jnp.dot(h_hi, wr_ref[...], preferred_element_type=f32)
        r_lo = jnp.dot(h_lo, wr_ref[:, 0:LANES], preferred_element_type=f32)
        lg = r_hi[:, 0:LANES] + r_hi[:, LANES:2 * LANES] + r_lo + br_ref[...]
        glog = jnp.where(lane < N_GROUPS, lg, ninf)
        gmax = glog.max(axis=-1, keepdims=True)
        g_val = 1.0 / jnp.exp(glog - gmax).sum(axis=-1, keepdims=True)
        g_idx = jnp.where(glog == gmax, lane, big).min(axis=-1, keepdims=True)
        e_lo = N_GROUPS + EXPERTS_PER_GROUP * g_idx
        el = jnp.where((lane >= e_lo) & (lane < e_lo + EXPERTS_PER_GROUP), lg, ninf)
        v0 = el.max(axis=-1, keepdims=True)
        i0 = jnp.where(el == v0, lane, big).min(axis=-1, keepdims=True)
        el1 = jnp.where(lane == i0, ninf, el)
        v1 = el1.max(axis=-1, keepdims=True)
        i1 = jnp.where(el1 == v1, lane, big).min(axis=-1, keepdims=True)
        t = jnp.exp(v1 - v0)
        w0 = g_val / (1.0 + t)
        w1 = g_val * t / (1.0 + t)
        e0 = i0 - N_GROUPS
        e1 = i1 - N_GROUPS

        ranks = []
        for e in (e0, e1):
            oh = lane == e
            ohf = oh.astype(f32)
            pre = jnp.dot(tri, ohf.astype(jnp.bfloat16), preferred_element_type=f32) + cnt
            ranks.append(jnp.where(oh, pre, 0.0).sum(axis=-1, keepdims=True))
            cnt = cnt + ohf.sum(axis=0, keepdims=True)
        rw_ref[rows, :] = jnp.where(lane == 0, w0, jnp.where(lane == 1, w1, 0.0))
        rec = jnp.where(lane == 0, e0, jnp.where(lane == 1, e1, jnp.where(lane == 2, ranks[0],
                        jnp.where(lane == 3, ranks[1], jnp.where(lane == 4, w0, jnp.where(lane == 5, w1, 0.0))))))
        meta_ref[:, rows] = rec.T[0:META_ROWS, :]
    cnt_ref[0:1, :] = cnt


def _outproj(xs, oa, ob, oc, mods, g2, w_out_bf, wr, br, tri, *, n, tm, n_tiles, n_lat_tiles, s, nb,
             x_offsets=(0, 0)):
    d = xs[0].shape[1]
    rows = n_tiles * tm

    def bidx(i):
        return jnp.where(i < n_lat_tiles, (i * tm) // s, nb)

    row = lambda w: pl.BlockSpec((tm, w), lambda i: (i, 0))
    const = lambda a: pl.BlockSpec(a.shape, lambda i: (0,) * a.ndim)
    return pl.pallas_call(
        functools.partial(_outproj_kernel, n_x=len(xs), n_lat_tiles=n_lat_tiles),
        grid=(n_tiles,),
        in_specs=_x_specs(xs, tm, n_lat_tiles, x_offsets) + [
                  row(A_Q), row(B_CH), row(C_Q),
                  pl.BlockSpec((1, N_MOD, d), lambda i: (bidx(i), 0, 0)),
                  const(g2), const(w_out_bf), const(wr), const(br), const(tri)],
        out_specs=[row(d), row(d // 2), pl.BlockSpec((META_ROWS, tm), lambda i: (0, i)), row(LANES),
                   pl.BlockSpec((8, LANES), lambda i: (0, 0))],
        out_shape=[jax.ShapeDtypeStruct((n, d), jnp.float32),
                   jax.ShapeDtypeStruct((rows, d // 2), jnp.int32),
                   jax.ShapeDtypeStruct((META_ROWS, rows), jnp.float32),
                   jax.ShapeDtypeStruct((rows, LANES), jnp.float32),
                   jax.ShapeDtypeStruct((8, LANES), jnp.float32)],
        input_output_aliases={0: 0} if len(xs) == 1 else {},
        compiler_params=_cparams("arbitrary"),
        name="outproj_router",
    )(*xs, oa, ob, oc, mods, g2, w_out_bf, wr, br, tri)


def _sc_mesh():
    return plsc.VectorSubcoreMesh(core_axis_name="core", subcore_axis_name="subcore")


def sc_gather_rows(table, idx):
    r = idx.shape[0]
    w = table.shape[1]
    half = r // SC_WINDOW // 2
    idx2 = idx.reshape(1, r)

    @functools.partial(pl.kernel, out_type=jax.ShapeDtypeStruct((r, w), table.dtype), mesh=_sc_mesh())
    def k(x_hbm, i_hbm, o_hbm):
        def body(i_vmem, o_vmem):
            pltpu.sync_copy(x_hbm.at[i_vmem.at[0]], o_vmem)

        pltpu.emit_pipeline(
            body,
            grid=(2, half),
            in_specs=[pl.BlockSpec((1, SC_WINDOW), lambda c, i: (0, c * half + i))],
            out_specs=[pl.BlockSpec((SC_WINDOW, w), lambda c, i: (c * half + i, 0),
                                    pipeline_mode=pl.Buffered(1))],
            core_axis_name=("core", "subcore"),
            dimension_semantics=(pltpu.PARALLEL, pltpu.PARALLEL),
        )(i_hbm, o_hbm)

    return k(table, idx2)


def sc_scatter_rows2(rows, idx_a, idx_b, n_out):
    r, w = rows.shape
    half = r // SC_WINDOW // 2
    ia = idx_a.reshape(1, r)
    ib = idx_b.reshape(1, r)

    @functools.partial(pl.kernel, out_type=jax.ShapeDtypeStruct((n_out, w), rows.dtype), mesh=_sc_mesh(),
                       scratch_types=[])
    def k(x_hbm, ia_hbm, ib_hbm, o_hbm):
        def body(x_vmem, ia_vmem, ib_vmem):
            pltpu.sync_copy(x_vmem, o_hbm.at[ia_vmem.at[0]])
            pltpu.sync_copy(x_vmem, o_hbm.at[ib_vmem.at[0]])

        idx_spec = pl.BlockSpec((1, SC_WINDOW), lambda c, i: (0, c * half + i))
        pltpu.emit_pipeline(
            body,
            grid=(2, half),
            in_specs=[pl.BlockSpec((SC_WINDOW, w), lambda c, i: (c * half + i, 0),
                                   pipeline_mode=pl.Buffered(1)),
                      idx_spec, idx_spec],
            out_specs=[],
            core_axis_name=("core", "subcore"),
            dimension_semantics=(pltpu.PARALLEL, pltpu.PARALLEL),
        )(x_hbm, ia_hbm, ib_hbm)

    return k(rows, ia, ib)


def _expert_kernel(te_ref, nv_ref, x_ref, wg_ref, wu_ref, wd_ref, o_ref, wgb_ref, wub_ref, wdb_ref):
    t = pl.program_id(0)
    nvalid = nv_ref[t]

    @pl.when((t == 0) | (te_ref[t] != te_ref[jnp.maximum(t - 1, 0)]))
    def _():
        wgb_ref[...] = wg_ref[0].astype(jnp.bfloat16)
        wub_ref[...] = wu_ref[0].astype(jnp.bfloat16)
        wdb_ref[...] = wd_ref[0].astype(jnp.bfloat16)

    @pl.when(nvalid > 0)
    def _():
        rows = lax.broadcasted_iota(jnp.int32, x_ref.shape, 0)
        lo, hi = _unpack_pairs(jnp.where(rows < nvalid, x_ref[...], 0))
        xb = jnp.concatenate([lo, hi], axis=1).astype(jnp.bfloat16)
        g = jnp.dot(xb, wgb_ref[...], preferred_element_type=jnp.float32)
        u = jnp.dot(xb, wub_ref[...], preferred_element_type=jnp.float32)
        a = (g * jax.nn.sigmoid(g) * u).astype(jnp.bfloat16)
        o_ref[...] = _pack_pairs(jnp.dot(a, wdb_ref[...], preferred_element_type=jnp.float32))

    @pl.when(nvalid == 0)
    def _():
        o_ref[...] = jnp.zeros_like(o_ref)


def _experts(buf, tile_expert, tile_nvalid, wg, wu, wd, *, layer):
    rows, wp = buf.shape
    _, _, d, f = wg.shape
    n_tiles = rows // EXPERT_TILE
    grid_spec = pltpu.PrefetchScalarGridSpec(
        num_scalar_prefetch=2,
        grid=(n_tiles,),
        in_specs=[pl.BlockSpec((EXPERT_TILE, wp), lambda t, te, nv: (t, 0)),
                  pl.BlockSpec((None, 1, d, f), lambda t, te, nv: (layer, te[t], 0, 0)),
                  pl.BlockSpec((None, 1, d, f), lambda t, te, nv: (layer, te[t], 0, 0)),
                  pl.BlockSpec((None, 1, f, d), lambda t, te, nv: (layer, te[t], 0, 0))],
        out_specs=pl.BlockSpec((EXPERT_TILE, wp), lambda t, te, nv: (t, 0)),
        scratch_shapes=[pltpu.VMEM((d, f), jnp.bfloat16), pltpu.VMEM((d, f), jnp.bfloat16),
                        pltpu.VMEM((f, d), jnp.bfloat16)],
    )
    return pl.pallas_call(
        _expert_kernel,
        grid_spec=grid_spec,
        out_shape=jax.ShapeDtypeStruct((rows, wp), jnp.int32),
        compiler_params=_cparams("arbitrary"),
        name="expert_ffn",
    )(tile_expert, tile_nvalid, buf, wg, wu, wd)


def _final_kernel(y0_ref, y1_ref, rw_ref, x_ref, mod_ref, fg_ref, o_ref):
    x = x_ref[...] + mod_ref[0, 5:6, :] * _moe_mix(y0_ref, y1_ref, rw_ref, slice(None))
    r = lax.rsqrt(jnp.mean(x * x, axis=-1, keepdims=True) + EPS)
    o_ref[...] = x * r * fg_ref[...]


def _final_kernel_into(y0_ref, y1_ref, rw_ref, x_ref, mod_ref, fg_ref, prev_ref, o_ref):
    del prev_ref
    _final_kernel(y0_ref, y1_ref, rw_ref, x_ref, mod_ref, fg_ref, o_ref)


def _final_combine(y, rw, xall, mods, final_g, *, tm, n_tiles, s, out_rows, out_tile0, prev_out=None):
    d = xall.shape[1]
    row = lambda w: pl.BlockSpec((tm, w), lambda i: (i, 0))
    in_specs = [row(d // 2),
                pl.BlockSpec((tm, d // 2), lambda i: (i + n_tiles, 0)),
                row(LANES), row(d),
                pl.BlockSpec((1, N_MOD, d), lambda i: ((i * tm) // s, 0, 0)),
                pl.BlockSpec(final_g.shape, lambda i: (0, 0))]
    args = [y, y, rw, xall, mods, final_g]
    aliases = {}
    body = _final_kernel
    if prev_out is not None:
        in_specs.append(pl.BlockSpec(memory_space=pl.ANY))
        args.append(prev_out)
        aliases = {len(args) - 1: 0}
        body = _final_kernel_into
    return pl.pallas_call(
        body,
        grid=(n_tiles,),
        in_specs=in_specs,
        out_specs=pl.BlockSpec((tm, d), lambda i: (out_tile0 + i, 0)),
        out_shape=jax.ShapeDtypeStruct((out_rows, d), jnp.float32),
        input_output_aliases=aliases,
        compiler_params=_cparams("arbitrary"),
        name="moe_combine_final",
    )(*args)


def _dest_kernel(ps_ref, meta_ref, o_ref):
    e = meta_ref[0:TOP_K, :]
    d = meta_ref[TOP_K:2 * TOP_K, :]
    for k in range(N_EXPERTS):
        d = d + jnp.where(e == float(k), ps_ref[k], 0.0)
    o_ref[...] = d.astype(jnp.int32)


def _dest_rows(meta, pstarts, tcols):
    n = meta.shape[1]
    return pl.pallas_call(
        _dest_kernel,
        grid=(n // tcols,),
        in_specs=[pl.BlockSpec(memory_space=pltpu.SMEM),
                  pl.BlockSpec((META_ROWS, tcols), lambda i: (0, i))],
        out_specs=pl.BlockSpec((TOP_K, tcols), lambda i: (0, i)),
        out_shape=jax.ShapeDtypeStruct((TOP_K, n), jnp.int32),
        compiler_params=_cparams("arbitrary"),
        name="moe_dest",
    )(pstarts.astype(jnp.float32), meta)


def _dispatch_plan(meta, counts, n_rows_buf, tcols):
    cnt = counts[0, :N_EXPERTS].astype(jnp.int32)
    padded = (cnt + EXPERT_TILE - 1) // EXPERT_TILE * EXPERT_TILE
    pends = jnp.cumsum(padded)
    pstarts = pends - padded
    dest = _dest_rows(meta, pstarts, tcols)
    dest0, dest1 = dest[0], dest[1]
    tile_start = jnp.arange(n_rows_buf // EXPERT_TILE, dtype=jnp.int32) * EXPERT_TILE
    te = jnp.sum((tile_start[:, None] >= pends[None, :]).astype(jnp.int32), axis=1)
    te = jnp.minimum(te, N_EXPERTS - 1)
    nvalid = jnp.clip(cnt[te] - (tile_start - pstarts[te]), 0, EXPERT_TILE).astype(jnp.int32)
    return dest0, dest1, te, nvalid


def _rope_tables(s, tm):
    pos = np.arange(s)
    pos_row = jnp.asarray(pos // GRID_W, jnp.float32)
    pos_col = jnp.asarray(pos % GRID_W, jnp.float32)
    n_freq = HEAD_DIM // 4
    inv = ROPE_THETA ** (-jnp.arange(n_freq, dtype=jnp.float32) / n_freq)
    ang_row = pos_row[:, None] * inv
    ang_col = pos_col[:, None] * inv
    ang = jnp.concatenate([ang_row, ang_row, ang_col, ang_col] * (LANES // HEAD_DIM), axis=-1)
    sign = np.where((np.arange(LANES) % 32) < 16, -1.0, 1.0).astype(np.float32)
    cos_t = jnp.concatenate([jnp.cos(ang), jnp.ones((tm, LANES), jnp.float32)], axis=0)
    sin_t = jnp.concatenate([jnp.sin(ang) * sign, jnp.zeros((tm, LANES), jnp.float32)], axis=0)
    return cos_t, sin_t


def kernel(x, c, ctx, c_ctx, norm1_g, norm2_g, w_mod, b_mod, w_in, q_norm_g, k_norm_g, conv_w, conv_b, conv_ln_g, conv_ln_b, sink, w_out, w_group, b_group, w_expert, b_expert, w_gate, w_up, w_down, final_g):
    nb, s, d = x.shape
    l = ctx.shape[1]
    depth = w_in.shape[0]
    assert w_in.shape[2] == D_IN and w_out.shape[1] == D_MIX
    assert s % GRID_W == 0 and s >= 3 * WINDOW and s % WINDOW == 0 and l % WINDOW == 0
    groups = BATCH_GROUPS if nb % BATCH_GROUPS == 0 else 1
    nbg = nb // groups
    n_lat, n_ctx = nbg * s, nbg * l
    tm = _pick(np.gcd(s, n_ctx), (512, 256, 128))
    tq = _pick(np.gcd(s, l), (256, 128))
    tc = _pick(np.gcd(s, l), (256, 128))
    assert n_lat % l == 0
    bf = jnp.bfloat16
    f32 = jnp.float32

    x2d = x.reshape(nb * s, d)
    ctx2d = ctx.reshape(nb * l, d)
    c_all = jnp.concatenate([c, c_ctx[None, :]], axis=0)
    mods_all = _modulation(c_all, w_mod, b_mod).reshape(depth, nb + 1, N_MOD, d)
    cos_t, sin_t = _rope_tables(s, tm)
    head_id = np.arange(LANES) // HEAD_DIM
    gsum = jnp.asarray((head_id[:, None] == head_id[None, :]) / HEAD_DIM, bf)
    tri = jnp.asarray(np.tril(np.ones((tm, tm), np.float32), -1), bf)
    n_lat_tiles = n_lat // tm

    xs = [(x2d, ctx2d)] * groups
    pending = [None] * groups
    for i in range(depth):
        last = i == depth - 1
        with_ctx = not last
        qg = jnp.tile(q_norm_g[i], LANES // HEAD_DIM)[None, :]
        kg = jnp.tile(k_norm_g[i], LANES // HEAD_DIM)[None, :]
        w_in_bf = w_in[i].astype(bf)
        w_out_bf = w_out[i].astype(bf)
        sink2 = sink[i] * LOG2E
        conv_args = (conv_w[i].reshape(CONV_W, B_CH), conv_b[i][None, :], conv_ln_g[i][None, :],
                     conv_ln_b[i][None, :])
        wr32 = jnp.zeros((d, LANES), f32).at[:, :N_GROUPS].set(w_group[i])
        wr32 = wr32.at[:, N_GROUPS:N_GROUPS + N_EXPERTS].set(w_expert[i])
        wr_hi = wr32.astype(bf)
        wr = jnp.concatenate([wr_hi, (wr32 - wr_hi.astype(f32)).astype(bf)], axis=1)
        br = jnp.zeros((1, LANES), f32).at[0, :N_GROUPS].set(b_group[i])
        br = br.at[0, N_GROUPS:N_GROUPS + N_EXPERTS].set(b_expert[i])
        n_tok = n_lat + n_ctx if with_ctx else n_lat
        n_tiles = n_tok // tm
        n_rows_buf = 2 * n_tok + N_EXPERTS * EXPERT_TILE
        for g in range(groups):
            b0 = g * nbg
            mods = jnp.concatenate([mods_all[i, b0:b0 + nbg], mods_all[i, nb:nb + 1]], axis=0)
            x_offsets = (b0 * s // tm, b0 * l // tm)
            x_new, qa, kta, va, hgl, qc, ktc, vc = _inproj(
                xs[g], mods, norm1_g[i][None, :], w_in_bf, gsum, qg, kg, cos_t, sin_t, n=n_lat + n_ctx,
                tm=tm, n_lat_tiles=n_lat_tiles, s=s, nb=nbg, x_offsets=x_offsets, combine=pending[g])
            if pending[g] is not None:
                xs[g] = (x_new,)
            oa = _attn_a(qa, kta, va, nb=nbg, s=s, l=l, tq=tq)
            oc = _attn_c(qc, ktc, vc, sink2, nb=nbg, s=s, l=l)
            ob = _conv(hgl, *conv_args, nb=nbg, seq=s, base_rows=0, tc=tc)
            if with_ctx:
                oa, oc = _attn_ctx(qa, kta, va, qc, ktc, vc, sink2, oa, oc, nb=nbg, s=s, l=l)
                ob = _conv(hgl, *conv_args, nb=nbg, seq=l, base_rows=n_lat, tc=tc, prev_out=ob)
            xall, hp, meta, rw, counts = _outproj(xs[g], oa, ob, oc, mods, norm2_g[i][None, :], w_out_bf,
                                                  wr, br, tri, n=n_lat + n_ctx, tm=tm, n_tiles=n_tiles,
                                                  n_lat_tiles=n_lat_tiles, s=s, nb=nbg, x_offsets=x_offsets)
            dest0, dest1, te, nvalid = _dispatch_plan(meta, counts, n_rows_buf,
                                                      _pick(n_tok, (8192, 4096, 2048, 1024, 512, 256, 128)))
            buf = sc_scatter_rows2(hp, dest0, dest1, n_rows_buf)
            eo = _experts(buf, te, nvalid, w_gate, w_up, w_down, layer=i)
            y = sc_gather_rows(eo, jnp.concatenate([dest0, dest1]))
            xs[g] = (xall,)
            pending[g] = (y, rw, mods)
    out = None
    for g in range(groups):
        y, rw, mods = pending[g]
        out = _final_combine(y, rw, xs[g][0], mods, final_g[None, :], tm=tm, n_tiles=n_lat_tiles, s=s,
                             out_rows=nb * s, out_tile0=g * n_lat_tiles, prev_out=out)
    return out.reshape(nb, s, d)
```

```python
import functools

import jax
import jax.numpy as jnp
import numpy as np
from jax import lax
from jax.experimental import pallas as pl
from jax.experimental.pallas import tpu as pltpu
from jax.experimental.pallas import tpu_sc as plsc

HEAD_DIM = 64
GRID_W = 64
ROPE_THETA = 10000.0
A_HEADS, A_KV_HEADS = 6, 2
C_HEADS, C_KV_HEADS = 6, 2
B_CH = 256
CONV_W = 31
WINDOW = 128
N_GROUPS = 4
EXPERTS_PER_GROUP = 8
N_EXPERTS = N_GROUPS * EXPERTS_PER_GROUP
TOP_K = 2
N_MOD = 6
EPS = 1e-6
ATTN_SCALE = HEAD_DIM ** -0.5
LOG2E = 1.4426950408889634
Q_SCALE = ATTN_SCALE * LOG2E

A_Q = A_HEADS * HEAD_DIM
A_KV = A_KV_HEADS * HEAD_DIM
C_Q = C_HEADS * HEAD_DIM
C_KV = C_KV_HEADS * HEAD_DIM
D_MIX = A_Q + B_CH + C_Q
OFF_AQ = 0
OFF_AK = OFF_AQ + A_Q
OFF_AV = OFF_AK + A_KV
OFF_BU = OFF_AV + A_KV
OFF_CQ = OFF_BU + 2 * B_CH
OFF_CK = OFF_CQ + C_Q
OFF_CV = OFF_CK + C_KV
D_IN = OFF_CV + C_KV

LANES = 128
SUBLANES = 8
KT_ROWS = 2 * LANES
V_W = 4 * LANES
EXPERT_TILE = 512
SC_WINDOW = 128
VMEM_LIMIT = 56 * 1024 * 1024
HI_MASK = -65536
BATCH_GROUPS = 2
ROW_CHAIN = 256


def _cparams(*sem):
    return pltpu.CompilerParams(dimension_semantics=sem, vmem_limit_bytes=VMEM_LIMIT)


def _pick(n, cands):
    for c in cands:
        if n % c == 0:
            return c
    raise ValueError(f"no tile in {cands} divides {n}")


def _pack_pairs(x):
    w = x.shape[1] // 2
    lo = lax.bitcast_convert_type(x[:, :w].astype(jnp.bfloat16).astype(jnp.float32), jnp.int32)
    hi = lax.bitcast_convert_type(x[:, w:].astype(jnp.bfloat16).astype(jnp.float32), jnp.int32)
    return (hi & HI_MASK) | lax.shift_right_logical(lo, 16)


def _unpack_pairs(p):
    lo = lax.bitcast_convert_type(lax.shift_left(p, 16), jnp.float32)
    hi = lax.bitcast_convert_type(p & HI_MASK, jnp.float32)
    return lo, hi


def _mod_kernel(c_ref, w_ref, b_ref, o_ref):
    c = c_ref[...]
    a = c * jax.nn.sigmoid(c)
    o_ref[0] = jnp.dot(a, w_ref[0], preferred_element_type=jnp.float32,
                       precision=lax.Precision.HIGHEST) + b_ref[0]


def _modulation(c_all, w_mod, b_mod):
    depth, d, n = w_mod.shape
    r = c_all.shape[0]
    tn = _pick(n, (1024, 512, 256, 128))
    return pl.pallas_call(
        _mod_kernel,
        grid=(depth, n // tn),
        in_specs=[pl.BlockSpec((r, d), lambda l, j: (0, 0)),
                  pl.BlockSpec((1, d, tn), lambda l, j: (l, 0, j)),
                  pl.BlockSpec((1, 1, tn), lambda l, j: (l, 0, j))],
        out_specs=pl.BlockSpec((1, r, tn), lambda l, j: (l, 0, j)),
        out_shape=jax.ShapeDtypeStruct((depth, r, n), jnp.float32),
        compiler_params=_cparams("arbitrary", "arbitrary"),
        name="modulation",
    )(c_all, w_mod, b_mod.reshape(depth, 1, n))


def _head_mean_sq(blk, gsum):
    sq = blk * blk
    hi = sq.astype(jnp.bfloat16)
    lo = (sq - hi.astype(jnp.float32)).astype(jnp.bfloat16)
    return (jnp.dot(hi, gsum, preferred_element_type=jnp.float32)
            + jnp.dot(lo, gsum, preferred_element_type=jnp.float32))


def _x_specs(xs, tm, n_lat_tiles, offsets=(0, 0)):
    d = xs[0].shape[1]
    if len(xs) == 1:
        return [pl.BlockSpec((tm, d), lambda i: (i, 0))]
    lat0, ctx0 = offsets
    return [pl.BlockSpec((tm, d), lambda i: (lat0 + jnp.minimum(i, n_lat_tiles - 1), 0)),
            pl.BlockSpec((tm, d), lambda i: (ctx0 + jnp.maximum(i - n_lat_tiles, 0), 0))]


def _load_x(x_refs, n_lat_tiles, rows=slice(None)):
    if len(x_refs) == 1:
        return x_refs[0][rows, :]
    return jnp.where(pl.program_id(0) < n_lat_tiles, x_refs[0][rows, :], x_refs[1][rows, :])


def _moe_mix(y0_ref, y1_ref, rw_ref, rows):
    rw = rw_ref[rows, :]
    w0 = rw[:, 0:1]
    w1 = rw[:, 1:2]
    a_lo, a_hi = _unpack_pairs(y0_ref[rows, :])
    b_lo, b_hi = _unpack_pairs(y1_ref[rows, :])
    return jnp.concatenate([a_lo * w0 + b_lo * w1, a_hi * w0 + b_hi * w1], axis=1)


def _inproj_kernel(*refs, n_x, n_lat_tiles, fused_combine):
    x_refs = refs[:n_x]
    refs = refs[n_x:]
    if fused_combine:
        y0_ref, y1_ref, rw_ref, modp_ref = refs[:4]
        refs = refs[4:]
        xo_ref = refs[-1]
        refs = refs[:-1]
    (mod_ref, g_ref, w_ref, gsum_ref, qg_ref, kg_ref, cos_ref, sin_ref,
     qa_ref, kta_ref, va_ref, hgl_ref, qc_ref, ktc_ref, vc_ref) = refs
    tm = qa_ref.shape[0]
    tr = min(tm, ROW_CHAIN)
    bf = jnp.bfloat16
    lane = lax.broadcasted_iota(jnp.int32, (tr, LANES), 1)
    first16 = (lane % 32) < 16
    lo64 = lane < HEAD_DIM
    gsum = gsum_ref[...]
    qg = qg_ref[...]
    kg = kg_ref[...]
    scale = g_ref[...] * (1.0 + mod_ref[0, 1:2, :])
    shift = mod_ref[0, 0:1, :]
    ones = jnp.ones((tr, LANES), bf)

    for r0 in range(0, tm, tr):
        rows = slice(r0, r0 + tr)
        x = _load_x(x_refs, n_lat_tiles, rows)
        if fused_combine:
            x = x + modp_ref[0, 5:6, :] * _moe_mix(y0_ref, y1_ref, rw_ref, rows)
            xo_ref[rows, :] = x
        r = lax.rsqrt(jnp.mean(x * x, axis=-1, keepdims=True) + EPS)
        h = (x * r) * scale + shift
        p = jnp.dot(h.astype(bf), w_ref[...], preferred_element_type=jnp.float32)
        cos = cos_ref[rows, :]
        sin = sin_ref[rows, :]

        def blk(off):
            return p[:, off:off + LANES]

        def rope(t):
            sw = jnp.where(first16, pltpu.roll(t, LANES - 16, axis=1), pltpu.roll(t, 16, axis=1))
            return t * cos + sw * sin

        def norm(t, g):
            return t * lax.rsqrt(_head_mean_sq(t, gsum) + EPS) * g

        def dup(t):
            sw = pltpu.roll(t, HEAD_DIM, axis=1)
            return jnp.where(lo64, t, sw), jnp.where(lo64, sw, t)

        def store_kv(kt_ref, v_ref, k_blk, v_blk):
            k0, k1 = dup(k_blk)
            v0, v1 = dup(v_blk)
            kt_ref[0:128, rows] = k0.T.astype(bf)
            kt_ref[128:256, rows] = k1.T.astype(bf)
            v_ref[rows, 0:128] = v0.astype(bf)
            v_ref[rows, 128:256] = ones
            v_ref[rows, 256:384] = v1.astype(bf)
            v_ref[rows, 384:512] = ones

        for i in range(A_Q // LANES):
            t = rope(norm(blk(OFF_AQ + i * LANES), qg)) * Q_SCALE
            qa_ref[rows, i * LANES:(i + 1) * LANES] = t.astype(bf)
        store_kv(kta_ref, va_ref, rope(norm(blk(OFF_AK), kg)), blk(OFF_AV))
        for i in range(B_CH // LANES):
            a = blk(OFF_BU + i * LANES)
            gt = blk(OFF_BU + B_CH + i * LANES)
            hgl_ref[rows, i * LANES:(i + 1) * LANES] = a * jax.nn.sigmoid(gt)
        for i in range(C_Q // LANES):
            t = rope(blk(OFF_CQ + i * LANES)) * Q_SCALE
            qc_ref[rows, i * LANES:(i + 1) * LANES] = t.astype(bf)
        store_kv(ktc_ref, vc_ref, rope(blk(OFF_CK)), blk(OFF_CV))


def _inproj(xs, mods, g1, w_in_bf, gsum, qg, kg, cos_t, sin_t, *, n, tm, n_lat_tiles, s, nb, x_offsets=(0, 0),
            combine=None):
    d = xs[0].shape[1]
    s_tiles = s // tm
    n_tiles = n // tm

    def bidx(i):
        return jnp.where(i < n_lat_tiles, (i * tm) // s, nb)

    def ridx(i):
        return jnp.where(i < n_lat_tiles, i % s_tiles, s_tiles)

    row = lambda w: pl.BlockSpec((tm, w), lambda i: (i, 0))
    ktspec = pl.BlockSpec((KT_ROWS, tm), lambda i: (0, i))
    const = lambda a: pl.BlockSpec(a.shape, lambda i: (0,) * a.ndim)
    bf = jnp.bfloat16
    modspec = pl.BlockSpec((1, N_MOD, d), lambda i: (bidx(i), 0, 0))
    in_specs = _x_specs(xs, tm, n_lat_tiles, x_offsets)
    args = list(xs)
    out_specs = [row(A_Q), ktspec, row(V_W), row(B_CH), row(C_Q), ktspec, row(V_W)]
    out_shape = [jax.ShapeDtypeStruct((n, A_Q), bf), jax.ShapeDtypeStruct((KT_ROWS, n), bf),
                 jax.ShapeDtypeStruct((n, V_W), bf), jax.ShapeDtypeStruct((n, B_CH), jnp.float32),
                 jax.ShapeDtypeStruct((n, C_Q), bf), jax.ShapeDtypeStruct((KT_ROWS, n), bf),
                 jax.ShapeDtypeStruct((n, V_W), bf)]
    aliases = {}
    if combine is not None:
        y, rw, mods_prev = combine
        assert len(xs) == 1 and y.shape[0] == 2 * n
        in_specs += [row(d // 2), pl.BlockSpec((tm, d // 2), lambda i: (i + n_tiles, 0)), row(LANES), modspec]
        args += [y, y, rw, mods_prev]
        out_specs.append(row(d))
        out_shape.append(jax.ShapeDtypeStruct((n, d), jnp.float32))
        aliases = {0: len(out_shape) - 1}
    in_specs += [modspec, const(g1), const(w_in_bf), const(gsum), const(qg), const(kg),
                 pl.BlockSpec((tm, LANES), lambda i: (ridx(i), 0)),
                 pl.BlockSpec((tm, LANES), lambda i: (ridx(i), 0))]
    args += [mods, g1, w_in_bf, gsum, qg, kg, cos_t, sin_t]
    outs = pl.pallas_call(
        functools.partial(_inproj_kernel, n_x=len(xs), n_lat_tiles=n_lat_tiles,
                          fused_combine=combine is not None),
        grid=(n_tiles,),
        in_specs=in_specs,
        out_specs=out_specs,
        out_shape=out_shape,
        input_output_aliases=aliases,
        compiler_params=_cparams("arbitrary"),
        name="inproj",
    )(*args)
    if combine is not None:
        return (outs[-1],) + tuple(outs[:-1])
    return (None,) + tuple(outs)


def _stack_heads(q_ref, tq):
    lane = lax.broadcasted_iota(jnp.int32, (tq, LANES), 1)
    lo = lane < HEAD_DIM
    qb = [q_ref[:, i * LANES:(i + 1) * LANES] for i in range(3)]
    zero = jnp.zeros_like(qb[0])
    keep_lo = lambda t: jnp.where(lo, t, zero)
    keep_hi = lambda t: jnp.where(lo, zero, t)
    s0 = jnp.concatenate([keep_lo(qb[0]), keep_hi(qb[0]), keep_lo(qb[1])], axis=0)
    s1 = jnp.concatenate([keep_hi(qb[1]), keep_lo(qb[2]), keep_hi(qb[2])], axis=0)
    return s0, s1, lo


def _unstack_store(o_ref, o0, o1, lo, tq):
    bf = jnp.bfloat16
    o_ref[:, 0:128] = jnp.where(lo, o0[0:tq], o0[tq:2 * tq]).astype(bf)
    o_ref[:, 128:256] = jnp.where(lo, o0[2 * tq:3 * tq], o1[0:tq]).astype(bf)
    o_ref[:, 256:384] = jnp.where(lo, o1[tq:2 * tq], o1[2 * tq:3 * tq]).astype(bf)


def _krow(kv):
    return slice(kv * LANES, (kv + 1) * LANES)


def _vcol(kv):
    return slice(2 * kv * LANES, (2 * kv + 2) * LANES)


def _softmax_pv(scores, values, extra=None):
    m = None
    for t in scores:
        for c in range(0, t.shape[1], LANES):
            blk = t[:, c:c + LANES]
            m = blk if m is None else jnp.maximum(m, blk)
    m = m.max(axis=-1, keepdims=True)
    if extra is not None:
        m = jnp.maximum(m, extra)
    acc = None
    for t, v in zip(scores, values):
        c = jnp.dot(jnp.exp2(t - m).astype(jnp.bfloat16), v, preferred_element_type=jnp.float32)
        acc = c if acc is None else acc + c
    den = acc[:, LANES:2 * LANES]
    if extra is not None:
        den = den + jnp.exp2(extra - m)
    return acc[:, 0:LANES] / den


ATTN_CHAIN_ROWS = 128


def _qk(q, kt):
    return jnp.dot(q, kt, preferred_element_type=jnp.float32)


def _attn_a_kernel(q_ref, ktl_ref, ktc_ref, vl_ref, vc_ref, o_ref, *, tq):
    s0, s1, lo = _stack_heads(q_ref, tq)
    outs = []
    for kv, qs in enumerate((s0, s1)):
        krow = _krow(kv)
        vcol = _vcol(kv)
        parts = []
        for r0 in range(0, 3 * tq, ATTN_CHAIN_ROWS):
            qr = qs[r0:r0 + ATTN_CHAIN_ROWS]
            ss = [_qk(qr, ktl_ref[krow, :]), _qk(qr, ktc_ref[krow, :])]
            parts.append(_softmax_pv(ss, [vl_ref[:, vcol], vc_ref[:, vcol]]))
        outs.append(jnp.concatenate(parts, axis=0))
    _unstack_store(o_ref, outs[0], outs[1], lo, tq)


def _kv_specs(nb, s, l):
    ctx0 = nb * s // l
    return [pl.BlockSpec((KT_ROWS, s), lambda b, j: (0, b)),
            pl.BlockSpec((KT_ROWS, l), lambda b, j: (0, ctx0 + b)),
            pl.BlockSpec((s, V_W), lambda b, j: (b, 0)),
            pl.BlockSpec((l, V_W), lambda b, j: (ctx0 + b, 0))]


def _attn_a(qa, kt, v, *, nb, s, l, tq):
    n = qa.shape[0]
    n_q = s // tq
    return pl.pallas_call(
        functools.partial(_attn_a_kernel, tq=tq),
        grid=(nb, n_q),
        in_specs=[pl.BlockSpec((tq, A_Q), lambda b, j: (b * n_q + j, 0))] + _kv_specs(nb, s, l),
        out_specs=pl.BlockSpec((tq, A_Q), lambda b, j: (b * n_q + j, 0)),
        out_shape=jax.ShapeDtypeStruct((n, A_Q), jnp.bfloat16),
        compiler_params=_cparams("arbitrary", "arbitrary"),
        name="attn_global",
    )(qa, kt, kt, v, v)


WIN_BLOCKS = 8


def _sink_column(sink_ref, kv, rows):
    return jnp.concatenate([jnp.full((rows, 1), sink_ref[3 * kv + g], jnp.float32) for g in range(3)], axis=0)


def _attn_c_kernel(sink_ref, bias_ref, q_ref, ktl_ref, ktc_ref, vl_ref, vc_ref, o_ref, *, s, blocks):
    j = pl.program_id(1)
    tq = WINDOW
    band = 3 * WINDOW
    sk = [_sink_column(sink_ref, kv, tq) for kv in range(C_KV_HEADS)]
    for blk in range(blocks):
        jb = j * blocks + blk
        rows = slice(blk * tq, (blk + 1) * tq)
        s0, s1, lo = _stack_heads(q_ref.at[rows, :], tq)
        start = pl.multiple_of(jnp.clip((jb - 1) * WINDOW, 0, s - band), WINDOW)
        bias = bias_ref[jb - start // WINDOW]
        outs = []
        for kv, qs in enumerate((s0, s1)):
            krow = _krow(kv)
            vcol = _vcol(kv)
            sl = _qk(qs, ktl_ref[krow, pl.ds(start, band)]) + bias
            sc = _qk(qs, ktc_ref[krow, :])
            outs.append(_softmax_pv([sl, sc], [vl_ref[pl.ds(start, band), vcol], vc_ref[:, vcol]], sk[kv]))
        _unstack_store(o_ref.at[rows, :], outs[0], outs[1], lo, tq)


def _window_bias():
    r = np.arange(3 * WINDOW)[:, None] % WINDOW
    col = np.arange(3 * WINDOW)[None, :]
    tabs = [np.where(np.abs(col - r - WINDOW * off) <= WINDOW, 0.0, -np.inf) for off in range(3)]
    return jnp.asarray(np.stack(tabs), jnp.float32)


def _attn_c(qc, kt, v, sink2, *, nb, s, l):
    n = qc.shape[0]
    blocks = _pick(s // WINDOW, (WIN_BLOCKS, 4, 2, 1))
    tq = blocks * WINDOW
    n_q = s // tq
    bias = _window_bias()
    return pl.pallas_call(
        functools.partial(_attn_c_kernel, s=s, blocks=blocks),
        grid=(nb, n_q),
        in_specs=[pl.BlockSpec(memory_space=pltpu.SMEM),
                  pl.BlockSpec(bias.shape, lambda b, j: (0, 0, 0)),
                  pl.BlockSpec((tq, C_Q), lambda b, j: (b * n_q + j, 0))] + _kv_specs(nb, s, l),
        out_specs=pl.BlockSpec((tq, C_Q), lambda b, j: (b * n_q + j, 0)),
        out_shape=jax.ShapeDtypeStruct((n, C_Q), jnp.bfloat16),
        compiler_params=_cparams("arbitrary", "arbitrary"),
        name="attn_window",
    )(sink2, bias, qc, kt, kt, v, v)


def _attn_ctx_kernel(sink_ref, qa_ref, kta_ref, va_ref, qc_ref, ktc_ref, vc_ref, oa_in, oc_in, oa_ref, oc_ref, *, l):
    del oa_in, oc_in
    for q_ref, kt_ref, v_ref, o_ref, with_sink in ((qa_ref, kta_ref, va_ref, oa_ref, False),
                                                   (qc_ref, ktc_ref, vc_ref, oc_ref, True)):
        s0, s1, lo = _stack_heads(q_ref, l)
        outs = []
        for kv, qs in enumerate((s0, s1)):
            sk = _sink_column(sink_ref, kv, l) if with_sink else None
            outs.append(_softmax_pv([_qk(qs, kt_ref[_krow(kv), :])], [v_ref[:, _vcol(kv)]], sk))
        _unstack_store(o_ref, outs[0], outs[1], lo, l)


def _attn_ctx(qa, kta, va, qc, ktc, vc, sink2, oa, oc, *, nb, s, l):
    base = nb * s // l
    row = lambda w: pl.BlockSpec((l, w), lambda b: (base + b, 0))
    ktspec = pl.BlockSpec((KT_ROWS, l), lambda b: (0, base + b))
    anyspec = pl.BlockSpec(memory_space=pl.ANY)
    return pl.pallas_call(
        functools.partial(_attn_ctx_kernel, l=l),
        grid=(nb,),
        in_specs=[pl.BlockSpec(memory_space=pltpu.SMEM), row(A_Q), ktspec, row(V_W), row(C_Q), ktspec, row(V_W),
                  anyspec, anyspec],
        out_specs=[row(A_Q), row(C_Q)],
        out_shape=[jax.ShapeDtypeStruct(oa.shape, oa.dtype), jax.ShapeDtypeStruct(oc.shape, oc.dtype)],
        input_output_aliases={7: 0, 8: 1},
        compiler_params=_cparams("arbitrary"),
        name="attn_context",
    )(sink2, qa, kta, va, qc, ktc, vc, oa, oc)


CONV_HALO = 16
CONV_ROWS = 64


def _conv_kernel(prev_ref, cur_ref, next_ref, w_ref, b_ref, g_ref, beta_ref, *rest, chunks, tc):
    o_ref, sh_ref = rest[-2:]
    j = pl.program_id(1)
    has_prev = j > 0
    has_next = j < chunks - 1
    rows = tc + 2 * CONV_HALO
    sh_ref[0, 0:CONV_HALO, :] = jnp.where(has_prev, prev_ref[...], 0.0)
    sh_ref[0, CONV_HALO:CONV_HALO + tc, :] = cur_ref[...]
    sh_ref[0, CONV_HALO + tc:rows, :] = jnp.where(has_next, next_ref[...], 0.0)
    for b in range(1, SUBLANES):
        sh_ref[b, 0:rows - SUBLANES, :] = sh_ref[0, b:b + rows - SUBLANES, :]
    base = CONV_HALO - CONV_W // 2
    for r0 in range(0, tc, CONV_ROWS):
        acc = None
        for k in range(CONV_W):
            a, b = divmod(base + k, SUBLANES)
            term = sh_ref[b, SUBLANES * a + r0:SUBLANES * a + r0 + CONV_ROWS, :] * w_ref[k:k + 1, :]
            acc = term if acc is None else acc + term
        hc = acc + b_ref[...]
        mu = jnp.mean(hc, axis=-1, keepdims=True)
        xc = hc - mu
        var = jnp.mean(xc * xc, axis=-1, keepdims=True)
        y = xc * lax.rsqrt(var + EPS) * g_ref[...] + beta_ref[...]
        o_ref[r0:r0 + CONV_ROWS, :] = (y * jax.nn.sigmoid(y)).astype(o_ref.dtype)


def _conv(hgl, w, b, g, beta, *, nb, seq, base_rows, tc, prev_out=None):
    n = hgl.shape[0]
    chunks = seq // tc
    base = base_rows // tc

    def idx(b_, j, delta):
        return base + b_ * chunks + jnp.clip(j + delta, 0, chunks - 1)

    blk = lambda delta: pl.BlockSpec((tc, B_CH), lambda b_, j: (idx(b_, j, delta), 0))
    const = lambda a: pl.BlockSpec(a.shape, lambda b_, j: (0,) * a.ndim)
    hb = tc // CONV_HALO
    last_halo = n // CONV_HALO - 1
    prev_halo = pl.BlockSpec((CONV_HALO, B_CH), lambda b_, j: (jnp.maximum(idx(b_, j, 0) * hb - 1, 0), 0))
    next_halo = pl.BlockSpec((CONV_HALO, B_CH),
                             lambda b_, j: (jnp.minimum((idx(b_, j, 0) + 1) * hb, last_halo), 0))
    in_specs = [prev_halo, blk(0), next_halo, const(w), const(b), const(g), const(beta)]
    args = [hgl, hgl, hgl, w, b, g, beta]
    aliases = {}
    if prev_out is not None:
        in_specs.append(pl.BlockSpec(memory_space=pl.ANY))
        args.append(prev_out)
        aliases = {len(args) - 1: 0}
    return pl.pallas_call(
        functools.partial(_conv_kernel, chunks=chunks, tc=tc),
        grid=(nb, chunks),
        in_specs=in_specs,
        out_specs=blk(0),
        out_shape=jax.ShapeDtypeStruct((n, B_CH), jnp.bfloat16),
        scratch_shapes=[pltpu.VMEM((SUBLANES, tc + 2 * CONV_HALO, B_CH), jnp.float32)],
        input_output_aliases=aliases,
        compiler_params=_cparams("arbitrary", "arbitrary"),
        name="conformer_conv",
    )(*args)


META_ROWS = 8


def _outproj_kernel(*refs, n_x, n_lat_tiles):
    x_refs = refs[:n_x]
    (oa_ref, ob_ref, oc_ref, mod_ref, g_ref, w_ref, wr_ref, br_ref, tri_ref,
     xo_ref, hp_ref, meta_ref, rw_ref, cnt_ref) = refs[n_x:]
    i = pl.program_id(0)
    tm = xo_ref.shape[0]
    f32 = jnp.float32
    tr = tri_ref.shape[0]
    gate = mod_ref[0, 2:3, :]
    scale = g_ref[...] * (1.0 + mod_ref[0, 4:5, :])
    shift = mod_ref[0, 3:4, :]
    lane = lax.broadcasted_iota(jnp.int32, (tr, LANES), 1).astype(f32)
    big = float(LANES)
    ninf = -jnp.inf
    tri = tri_ref[...]

    @pl.when(i == 0)
    def _():
        cnt_ref[...] = jnp.zeros_like(cnt_ref)

    cnt = cnt_ref[0:1, :]
    for r0 in range(0, tm, tr):
        rows = slice(r0, r0 + tr)
        lat = jnp.concatenate([oa_ref[rows, :], ob_ref[rows, :], oc_ref[rows, :]], axis=1)
        mix = jnp.dot(lat, w_ref[...], preferred_element_type=f32)
        x = _load_x(x_refs, n_lat_tiles, rows) + gate * mix
        xo_ref[rows, :] = x
        r = lax.rsqrt(jnp.mean(x * x, axis=-1, keepdims=True) + EPS)
        h = (x * r) * scale + shift
        h_hi = h.astype(jnp.bfloat16)
        hp_ref[rows, :] = _pack_pairs(h_hi)
        h_lo = (h - h_hi.astype(f32)).astype(jnp.bfloat16)
        r_hi = jnp.dot(h_hi, wr_ref[...], preferred_element_type=f32)
        r_lo = jnp.dot(h_lo, wr_ref[:, 0:LANES], preferred_element_type=f32)
        lg = r_hi[:, 0:LANES] + r_hi[:, LANES:2 * LANES] + r_lo + br_ref[...]
        glog = jnp.where(lane < N_GROUPS, lg, ninf)
        gmax = glog.max(axis=-1, keepdims=True)
        g_val = 1.0 / jnp.exp(glog - gmax).sum(axis=-1, keepdims=True)
        g_idx = jnp.where(glog == gmax, lane, big).min(axis=-1, keepdims=True)
        e_lo = N_GROUPS + EXPERTS_PER_GROUP * g_idx
        el = jnp.where((lane >= e_lo) & (lane < e_lo + EXPERTS_PER_GROUP), lg, ninf)
        v0 = el.max(axis=-1, keepdims=True)
        i0 = jnp.where(el == v0, lane, big).min(axis=-1, keepdims=True)
        el1 = jnp.where(lane == i0, ninf, el)
        v1 = el1.max(axis=-1, keepdims=True)
        i1 = jnp.where(el1 == v1, lane, big).min(axis=-1, keepdims=True)
        t = jnp.exp(v1 - v0)
        w0 = g_val / (1.0 + t)
        w1 = g_val * t / (1.0 + t)
        e0 = i0 - N_GROUPS
        e1 = i1 - N_GROUPS

        ranks = []
        for e in (e0, e1):
            oh = lane == e
            ohf = oh.astype(f32)
            pre = jnp.dot(tri, ohf.astype(jnp.bfloat16), preferred_element_type=f32) + cnt
            ranks.append(jnp.where(oh, pre, 0.0).sum(axis=-1, keepdims=True))
            cnt = cnt + ohf.sum(axis=0, keepdims=True)
        rw_ref[rows, :] = jnp.where(lane == 0, w0, jnp.where(lane == 1, w1, 0.0))
        rec = jnp.where(lane == 0, e0, jnp.where(lane == 1, e1, jnp.where(lane == 2, ranks[0],
                        jnp.where(lane == 3, ranks[1], jnp.where(lane == 4, w0, jnp.where(lane == 5, w1, 0.0))))))
        meta_ref[:, rows] = rec.T[0:META_ROWS, :]
    cnt_ref[0:1, :] = cnt


def _outproj(xs, oa, ob, oc, mods, g2, w_out_bf, wr, br, tri, *, n, tm, n_tiles, n_lat_tiles, s, nb,
             x_offsets=(0, 0)):
    d = xs[0].shape[1]
    rows = n_tiles * tm

    def bidx(i):
        return jnp.where(i < n_lat_tiles, (i * tm) // s, nb)

    row = lambda w: pl.BlockSpec((tm, w), lambda i: (i, 0))
    const = lambda a: pl.BlockSpec(a.shape, lambda i: (0,) * a.ndim)
    return pl.pallas_call(
        functools.partial(_outproj_kernel, n_x=len(xs), n_lat_tiles=n_lat_tiles),
        grid=(n_tiles,),
        in_specs=_x_specs(xs, tm, n_lat_tiles, x_offsets) + [
                  row(A_Q), row(B_CH), row(C_Q),
                  pl.BlockSpec((1, N_MOD, d), lambda i: (bidx(i), 0, 0)),
                  const(g2), const(w_out_bf), const(wr), const(br), const(tri)],
        out_specs=[row(d), row(d // 2), pl.BlockSpec((META_ROWS, tm), lambda i: (0, i)), row(LANES),
                   pl.BlockSpec((8, LANES), lambda i: (0, 0))],
        out_shape=[jax.ShapeDtypeStruct((n, d), jnp.float32),
                   jax.ShapeDtypeStruct((rows, d // 2), jnp.int32),
                   jax.ShapeDtypeStruct((META_ROWS, rows), jnp.float32),
                   jax.ShapeDtypeStruct((rows, LANES), jnp.float32),
                   jax.ShapeDtypeStruct((8, LANES), jnp.float32)],
        input_output_aliases={0: 0} if len(xs) == 1 else {},
        compiler_params=_cparams("arbitrary"),
        name="outproj_router",
    )(*xs, oa, ob, oc, mods, g2, w_out_bf, wr, br, tri)


def _sc_mesh():
    return plsc.VectorSubcoreMesh(core_axis_name="core", subcore_axis_name="subcore")


def sc_gather_rows(table, idx):
    r = idx.shape[0]
    w = table.shape[1]
    half = r // SC_WINDOW // 2
    idx2 = idx.reshape(1, r)

    @functools.partial(pl.kernel, out_type=jax.ShapeDtypeStruct((r, w), table.dtype), mesh=_sc_mesh())
    def k(x_hbm, i_hbm, o_hbm):
        def body(i_vmem, o_vmem):
            pltpu.sync_copy(x_hbm.at[i_vmem.at[0]], o_vmem)

        pltpu.emit_pipeline(
            body,
            grid=(2, half),
            in_specs=[pl.BlockSpec((1, SC_WINDOW), lambda c, i: (0, c * half + i))],
            out_specs=[pl.BlockSpec((SC_WINDOW, w), lambda c, i: (c * half + i, 0),
                                    pipeline_mode=pl.Buffered(1))],
            core_axis_name=("core", "subcore"),
            dimension_semantics=(pltpu.PARALLEL, pltpu.PARALLEL),
        )(i_hbm, o_hbm)

    return k(table, idx2)


def sc_scatter_rows2(rows, idx_a, idx_b, n_out):
    r, w = rows.shape
    half = r // SC_WINDOW // 2
    ia = idx_a.reshape(1, r)
    ib = idx_b.reshape(1, r)

    @functools.partial(pl.kernel, out_type=jax.ShapeDtypeStruct((n_out, w), rows.dtype), mesh=_sc_mesh(),
                       scratch_types=[])
    def k(x_hbm, ia_hbm, ib_hbm, o_hbm):
        def body(x_vmem, ia_vmem, ib_vmem):
            pltpu.sync_copy(x_vmem, o_hbm.at[ia_vmem.at[0]])
            pltpu.sync_copy(x_vmem, o_hbm.at[ib_vmem.at[0]])

        idx_spec = pl.BlockSpec((1, SC_WINDOW), lambda c, i: (0, c * half + i))
        pltpu.emit_pipeline(
            body,
            grid=(2, half),
            in_specs=[pl.BlockSpec((SC_WINDOW, w), lambda c, i: (c * half + i, 0),
                                   pipeline_mode=pl.Buffered(1)),
                      idx_spec, idx_spec],
            out_specs=[],
            core_axis_name=("core", "subcore"),
            dimension_semantics=(pltpu.PARALLEL, pltpu.PARALLEL),
        )(x_hbm, ia_hbm, ib_hbm)

    return k(rows, ia, ib)


def _expert_kernel(te_ref, nv_ref, x_ref, wg_ref, wu_ref, wd_ref, o_ref, wgb_ref, wub_ref, wdb_ref):
    t = pl.program_id(0)
    nvalid = nv_ref[t]

    @pl.when((t == 0) | (te_ref[t] != te_ref[jnp.maximum(t - 1, 0)]))
    def _():
        wgb_ref[...] = wg_ref[0].astype(jnp.bfloat16)
        wub_ref[...] = wu_ref[0].astype(jnp.bfloat16)
        wdb_ref[...] = wd_ref[0].astype(jnp.bfloat16)

    @pl.when(nvalid > 0)
    def _():
        rows = lax.broadcasted_iota(jnp.int32, x_ref.shape, 0)
        lo, hi = _unpack_pairs(jnp.where(rows < nvalid, x_ref[...], 0))
        xb = jnp.concatenate([lo, hi], axis=1).astype(jnp.bfloat16)
        g = jnp.dot(xb, wgb_ref[...], preferred_element_type=jnp.float32)
        u = jnp.dot(xb, wub_ref[...], preferred_element_type=jnp.float32)
        a = (g * jax.nn.sigmoid(g) * u).astype(jnp.bfloat16)
        o_ref[...] = _pack_pairs(jnp.dot(a, wdb_ref[...], preferred_element_type=jnp.float32))

    @pl.when(nvalid == 0)
    def _():
        o_ref[...] = jnp.zeros_like(o_ref)


def _experts(buf, tile_expert, tile_nvalid, wg, wu, wd, *, layer):
    rows, wp = buf.shape
    _, _, d, f = wg.shape
    n_tiles = rows // EXPERT_TILE
    grid_spec = pltpu.PrefetchScalarGridSpec(
        num_scalar_prefetch=2,
        grid=(n_tiles,),
        in_specs=[pl.BlockSpec((EXPERT_TILE, wp), lambda t, te, nv: (t, 0)),
                  pl.BlockSpec((None, 1, d, f), lambda t, te, nv: (layer, te[t], 0, 0)),
                  pl.BlockSpec((None, 1, d, f), lambda t, te, nv: (layer, te[t], 0, 0)),
                  pl.BlockSpec((None, 1, f, d), lambda t, te, nv: (layer, te[t], 0, 0))],
        out_specs=pl.BlockSpec((EXPERT_TILE, wp), lambda t, te, nv: (t, 0)),
        scratch_shapes=[pltpu.VMEM((d, f), jnp.bfloat16), pltpu.VMEM((d, f), jnp.bfloat16),
                        pltpu.VMEM((f, d), jnp.bfloat16)],
    )
    return pl.pallas_call(
        _expert_kernel,
        grid_spec=grid_spec,
        out_shape=jax.ShapeDtypeStruct((rows, wp), jnp.int32),
        compiler_params=_cparams("arbitrary"),
        name="expert_ffn",
    )(tile_expert, tile_nvalid, buf, wg, wu, wd)


def _final_kernel(y0_ref, y1_ref, rw_ref, x_ref, mod_ref, fg_ref, o_ref):
    x = x_ref[...] + mod_ref[0, 5:6, :] * _moe_mix(y0_ref, y1_ref, rw_ref, slice(None))
    r = lax.rsqrt(jnp.mean(x * x, axis=-1, keepdims=True) + EPS)
    o_ref[...] = x * r * fg_ref[...]


def _final_kernel_into(y0_ref, y1_ref, rw_ref, x_ref, mod_ref, fg_ref, prev_ref, o_ref):
    del prev_ref
    _final_kernel(y0_ref, y1_ref, rw_ref, x_ref, mod_ref, fg_ref, o_ref)


def _final_combine(y, rw, xall, mods, final_g, *, tm, n_tiles, s, out_rows, out_tile0, prev_out=None):
    d = xall.shape[1]
    row = lambda w: pl.BlockSpec((tm, w), lambda i: (i, 0))
    in_specs = [row(d // 2),
                pl.BlockSpec((tm, d // 2), lambda i: (i + n_tiles, 0)),
                row(LANES), row(d),
                pl.BlockSpec((1, N_MOD, d), lambda i: ((i * tm) // s, 0, 0)),
                pl.BlockSpec(final_g.shape, lambda i: (0, 0))]
    args = [y, y, rw, xall, mods, final_g]
    aliases = {}
    body = _final_kernel
    if prev_out is not None:
        in_specs.append(pl.BlockSpec(memory_space=pl.ANY))
        args.append(prev_out)
        aliases = {len(args) - 1: 0}
        body = _final_kernel_into
    return pl.pallas_call(
        body,
        grid=(n_tiles,),
        in_specs=in_specs,
        out_specs=pl.BlockSpec((tm, d), lambda i: (out_tile0 + i, 0)),
        out_shape=jax.ShapeDtypeStruct((out_rows, d), jnp.float32),
        input_output_aliases=aliases,
        compiler_params=_cparams("arbitrary"),
        name="moe_combine_final",
    )(*args)


def _dest_kernel(ps_ref, meta_ref, o_ref):
    e = meta_ref[0:TOP_K, :]
    d = meta_ref[TOP_K:2 * TOP_K, :]
    for k in range(N_EXPERTS):
        d = d + jnp.where(e == float(k), ps_ref[k], 0.0)
    o_ref[...] = d.astype(jnp.int32)


def _dest_rows(meta, pstarts, tcols):
    n = meta.shape[1]
    return pl.pallas_call(
        _dest_kernel,
        grid=(n // tcols,),
        in_specs=[pl.BlockSpec(memory_space=pltpu.SMEM),
                  pl.BlockSpec((META_ROWS, tcols), lambda i: (0, i))],
        out_specs=pl.BlockSpec((TOP_K, tcols), lambda i: (0, i)),
        out_shape=jax.ShapeDtypeStruct((TOP_K, n), jnp.int32),
        compiler_params=_cparams("arbitrary"),
        name="moe_dest",
    )(pstarts.astype(jnp.float32), meta)


def _dispatch_plan(meta, counts, n_rows_buf, tcols):
    cnt = counts[0, :N_EXPERTS].astype(jnp.int32)
    padded = (cnt + EXPERT_TILE - 1) // EXPERT_TILE * EXPERT_TILE
    pends = jnp.cumsum(padded)
    pstarts = pends - padded
    dest = _dest_rows(meta, pstarts, tcols)
    dest0, dest1 = dest[0], dest[1]
    tile_start = jnp.arange(n_rows_buf // EXPERT_TILE, dtype=jnp.int32) * EXPERT_TILE
    te = jnp.sum((tile_start[:, None] >= pends[None, :]).astype(jnp.int32), axis=1)
    te = jnp.minimum(te, N_EXPERTS - 1)
    nvalid = jnp.clip(cnt[te] - (tile_start - pstarts[te]), 0, EXPERT_TILE).astype(jnp.int32)
    return dest0, dest1, te, nvalid


def _rope_tables(s, tm):
    pos = np.arange(s)
    pos_row = jnp.asarray(pos // GRID_W, jnp.float32)
    pos_col = jnp.asarray(pos % GRID_W, jnp.float32)
    n_freq = HEAD_DIM // 4
    inv = ROPE_THETA ** (-jnp.arange(n_freq, dtype=jnp.float32) / n_freq)
    ang_row = pos_row[:, None] * inv
    ang_col = pos_col[:, None] * inv
    ang = jnp.concatenate([ang_row, ang_row, ang_col, ang_col] * (LANES // HEAD_DIM), axis=-1)
    sign = np.where((np.arange(LANES) % 32) < 16, -1.0, 1.0).astype(np.float32)
    cos_t = jnp.concatenate([jnp.cos(ang), jnp.ones((tm, LANES), jnp.float32)], axis=0)
    sin_t = jnp.concatenate([jnp.sin(ang) * sign, jnp.zeros((tm, LANES), jnp.float32)], axis=0)
    return cos_t, sin_t


def kernel(x, c, ctx, c_ctx, norm1_g, norm2_g, w_mod, b_mod, w_in, q_norm_g, k_norm_g, conv_w, conv_b, conv_ln_g, conv_ln_b, sink, w_out, w_group, b_group, w_expert, b_expert, w_gate, w_up, w_down, final_g):
    nb, s, d = x.shape
    l = ctx.shape[1]
    depth = w_in.shape[0]
    assert w_in.shape[2] == D_IN and w_out.shape[1] == D_MIX
    assert s % GRID_W == 0 and s >= 3 * WINDOW and s % WINDOW == 0 and l % WINDOW == 0
    groups = BATCH_GROUPS if nb % BATCH_GROUPS == 0 else 1
    nbg = nb // groups
    n_lat, n_ctx = nbg * s, nbg * l
    tm = _pick(np.gcd(s, n_ctx), (512, 256, 128))
    tq = _pick(s, (512, 256, 128))
    tc = _pick(np.gcd(s, l), (256, 128))
    assert n_lat % l == 0
    bf = jnp.bfloat16
    f32 = jnp.float32

    x2d = x.reshape(nb * s, d)
    ctx2d = ctx.reshape(nb * l, d)
    c_all = jnp.concatenate([c, c_ctx[None, :]], axis=0)
    mods_all = _modulation(c_all, w_mod, b_mod).reshape(depth, nb + 1, N_MOD, d)
    cos_t, sin_t = _rope_tables(s, tm)
    head_id = np.arange(LANES) // HEAD_DIM
    gsum = jnp.asarray((head_id[:, None] == head_id[None, :]) / HEAD_DIM, bf)
    tri = jnp.asarray(np.tril(np.ones((tm, tm), np.float32), -1), bf)
    n_lat_tiles = n_lat // tm

    xs = [(x2d, ctx2d)] * groups
    pending = [None] * groups
    for i in range(depth):
        last = i == depth - 1
        with_ctx = not last
        qg = jnp.tile(q_norm_g[i], LANES // HEAD_DIM)[None, :]
        kg = jnp.tile(k_norm_g[i], LANES // HEAD_DIM)[None, :]
        w_in_bf = w_in[i].astype(bf)
        w_out_bf = w_out[i].astype(bf)
        sink2 = sink[i] * LOG2E
        conv_args = (conv_w[i].reshape(CONV_W, B_CH), conv_b[i][None, :], conv_ln_g[i][None, :],
                     conv_ln_b[i][None, :])
        wr32 = jnp.zeros((d, LANES), f32).at[:, :N_GROUPS].set(w_group[i])
        wr32 = wr32.at[:, N_GROUPS:N_GROUPS + N_EXPERTS].set(w_expert[i])
        wr_hi = wr32.astype(bf)
        wr = jnp.concatenate([wr_hi, (wr32 - wr_hi.astype(f32)).astype(bf)], axis=1)
        br = jnp.zeros((1, LANES), f32).at[0, :N_GROUPS].set(b_group[i])
        br = br.at[0, N_GROUPS:N_GROUPS + N_EXPERTS].set(b_expert[i])
        n_tok = n_lat + n_ctx if with_ctx else n_lat
        n_tiles = n_tok // tm
        n_rows_buf = 2 * n_tok + N_EXPERTS * EXPERT_TILE
        for g in range(groups):
            b0 = g * nbg
            mods = jnp.concatenate([mods_all[i, b0:b0 + nbg], mods_all[i, nb:nb + 1]], axis=0)
            x_offsets = (b0 * s // tm, b0 * l // tm)
            x_new, qa, kta, va, hgl, qc, ktc, vc = _inproj(
                xs[g], mods, norm1_g[i][None, :], w_in_bf, gsum, qg, kg, cos_t, sin_t, n=n_lat + n_ctx,
                tm=tm, n_lat_tiles=n_lat_tiles, s=s, nb=nbg, x_offsets=x_offsets, combine=pending[g])
            if pending[g] is not None:
                xs[g] = (x_new,)
            oa = _attn_a(qa, kta, va, nb=nbg, s=s, l=l, tq=tq)
            oc = _attn_c(qc, ktc, vc, sink2, nb=nbg, s=s, l=l)
            ob = _conv(hgl, *conv_args, nb=nbg, seq=s, base_rows=0, tc=tc)
            if with_ctx:
                oa, oc = _attn_ctx(qa, kta, va, qc, ktc, vc, sink2, oa, oc, nb=nbg, s=s, l=l)
                ob = _conv(hgl, *conv_args, nb=nbg, seq=l, base_rows=n_lat, tc=tc, prev_out=ob)
            xall, hp, meta, rw, counts = _outproj(xs[g], oa, ob, oc, mods, norm2_g[i][None, :], w_out_bf,
                                                  wr, br, tri, n=n_lat + n_ctx, tm=tm, n_tiles=n_tiles,
                                                  n_lat_tiles=n_lat_tiles, s=s, nb=nbg, x_offsets=x_offsets)
            dest0, dest1, te, nvalid = _dispatch_plan(meta, counts, n_rows_buf,
                                                      _pick(n_tok, (8192, 4096, 2048, 1024, 512, 256, 128)))
            buf = sc_scatter_rows2(hp, dest0, dest1, n_rows_buf)
            eo = _experts(buf, te, nvalid, w_gate, w_up, w_down, layer=i)
            y = sc_gather_rows(eo, jnp.concatenate([dest0, dest1]))
            xs[g] = (xall,)
            pending[g] = (y, rw, mods)
    out = None
    for g in range(groups):
        y, rw, mods = pending[g]
        out = _final_combine(y, rw, xs[g][0], mods, final_g[None, :], tm=tm, n_tiles=n_lat_tiles, s=s,
                             out_rows=nb * s, out_tile0=g * n_lat_tiles, prev_out=out)
    return out.reshape(nb, s, d)
```

```python
import functools

import jax
import jax.numpy as jnp
import numpy as np
from jax import lax
from jax.experimental import pallas as pl
from jax.experimental.pallas import tpu as pltpu
from jax.experimental.pallas import tpu_sc as plsc

HEAD_DIM = 64
GRID_W = 64
ROPE_THETA = 10000.0
A_HEADS, A_KV_HEADS = 6, 2
C_HEADS, C_KV_HEADS = 6, 2
B_CH = 256
CONV_W = 31
WINDOW = 128
N_GROUPS = 4
EXPERTS_PER_GROUP = 8
N_EXPERTS = N_GROUPS * EXPERTS_PER_GROUP
TOP_K = 2
N_MOD = 6
EPS = 1e-6
ATTN_SCALE = HEAD_DIM ** -0.5
LOG2E = 1.4426950408889634
Q_SCALE = ATTN_SCALE * LOG2E

A_Q = A_HEADS * HEAD_DIM
A_KV = A_KV_HEADS * HEAD_DIM
C_Q = C_HEADS * HEAD_DIM
C_KV = C_KV_HEADS * HEAD_DIM
D_MIX = A_Q + B_CH + C_Q
OFF_AQ = 0
OFF_AK = OFF_AQ + A_Q
OFF_AV = OFF_AK + A_KV
OFF_BU = OFF_AV + A_KV
OFF_CQ = OFF_BU + 2 * B_CH
OFF_CK = OFF_CQ + C_Q
OFF_CV = OFF_CK + C_KV
D_IN = OFF_CV + C_KV

LANES = 128
SUBLANES = 8
KT_ROWS = 2 * LANES
V_W = 4 * LANES
EXPERT_TILE = 512
SC_WINDOW = 128
VMEM_LIMIT = 56 * 1024 * 1024
HI_MASK = -65536
BATCH_GROUPS = 2
ROW_CHAIN = 256


def _cparams(*sem):
    return pltpu.CompilerParams(dimension_semantics=sem, vmem_limit_bytes=VMEM_LIMIT)


def _pick(n, cands):
    for c in cands:
        if n % c == 0:
            return c
    raise ValueError(f"no tile in {cands} divides {n}")


def _pack_pairs(x):
    w = x.shape[1] // 2
    lo = lax.bitcast_convert_type(x[:, :w].astype(jnp.bfloat16).astype(jnp.float32), jnp.int32)
    hi = lax.bitcast_convert_type(x[:, w:].astype(jnp.bfloat16).astype(jnp.float32), jnp.int32)
    return (hi & HI_MASK) | lax.shift_right_logical(lo, 16)


def _unpack_pairs(p):
    lo = lax.bitcast_convert_type(lax.shift_left(p, 16), jnp.float32)
    hi = lax.bitcast_convert_type(p & HI_MASK, jnp.float32)
    return lo, hi


def _mod_kernel(c_ref, w_ref, b_ref, o_ref):
    c = c_ref[...]
    a = c * jax.nn.sigmoid(c)
    o_ref[0] = jnp.dot(a, w_ref[0], preferred_element_type=jnp.float32,
                       precision=lax.Precision.HIGHEST) + b_ref[0]


def _modulation(c_all, w_mod, b_mod):
    depth, d, n = w_mod.shape
    r = c_all.shape[0]
    tn = _pick(n, (1024, 512, 256, 128))
    return pl.pallas_call(
        _mod_kernel,
        grid=(depth, n // tn),
        in_specs=[pl.BlockSpec((r, d), lambda l, j: (0, 0)),
                  pl.BlockSpec((1, d, tn), lambda l, j: (l, 0, j)),
                  pl.BlockSpec((1, 1, tn), lambda l, j: (l, 0, j))],
        out_specs=pl.BlockSpec((1, r, tn), lambda l, j: (l, 0, j)),
        out_shape=jax.ShapeDtypeStruct((depth, r, n), jnp.float32),
        compiler_params=_cparams("arbitrary", "arbitrary"),
        name="modulation",
    )(c_all, w_mod, b_mod.reshape(depth, 1, n))


def _head_mean_sq(blk, gsum):
    sq = blk * blk
    hi = sq.astype(jnp.bfloat16)
    lo = (sq - hi.astype(jnp.float32)).astype(jnp.bfloat16)
    return (jnp.dot(hi, gsum, preferred_element_type=jnp.float32)
            + jnp.dot(lo, gsum, preferred_element_type=jnp.float32))


def _x_specs(xs, tm, n_lat_tiles, offsets=(0, 0)):
    d = xs[0].shape[1]
    if len(xs) == 1:
        return [pl.BlockSpec((tm, d), lambda i: (i, 0))]
    lat0, ctx0 = offsets
    return [pl.BlockSpec((tm, d), lambda i: (lat0 + jnp.minimum(i, n_lat_tiles - 1), 0)),
            pl.BlockSpec((tm, d), lambda i: (ctx0 + jnp.maximum(i - n_lat_tiles, 0), 0))]


def _load_x(x_refs, n_lat_tiles, rows=slice(None)):
    if len(x_refs) == 1:
        return x_refs[0][rows, :]
    return jnp.where(pl.program_id(0) < n_lat_tiles, x_refs[0][rows, :], x_refs[1][rows, :])


def _moe_mix(y0_ref, y1_ref, rw_ref, rows):
    rw = rw_ref[rows, :]
    w0 = rw[:, 0:1]
    w1 = rw[:, 1:2]
    a_lo, a_hi = _unpack_pairs(y0_ref[rows, :])
    b_lo, b_hi = _unpack_pairs(y1_ref[rows, :])
    return jnp.concatenate([a_lo * w0 + b_lo * w1, a_hi * w0 + b_hi * w1], axis=1)


def _inproj_kernel(*refs, n_x, n_lat_tiles, fused_combine):
    x_refs = refs[:n_x]
    refs = refs[n_x:]
    if fused_combine:
        y0_ref, y1_ref, rw_ref, modp_ref = refs[:4]
        refs = refs[4:]
        xo_ref = refs[-1]
        refs = refs[:-1]
    (mod_ref, g_ref, w_ref, gsum_ref, qg_ref, kg_ref, cos_ref, sin_ref,
     qa_ref, kta_ref, va_ref, hgl_ref, qc_ref, ktc_ref, vc_ref) = refs
    tm = qa_ref.shape[0]
    tr = min(tm, ROW_CHAIN)
    bf = jnp.bfloat16
    lane = lax.broadcasted_iota(jnp.int32, (tr, LANES), 1)
    first16 = (lane % 32) < 16
    lo64 = lane < HEAD_DIM
    gsum = gsum_ref[...]
    qg = qg_ref[...]
    kg = kg_ref[...]
    scale = g_ref[...] * (1.0 + mod_ref[0, 1:2, :])
    shift = mod_ref[0, 0:1, :]
    ones = jnp.ones((tr, LANES), bf)

    for r0 in range(0, tm, tr):
        rows = slice(r0, r0 + tr)
        x = _load_x(x_refs, n_lat_tiles, rows)
        if fused_combine:
            x = x + modp_ref[0, 5:6, :] * _moe_mix(y0_ref, y1_ref, rw_ref, rows)
            xo_ref[rows, :] = x
        r = lax.rsqrt(jnp.mean(x * x, axis=-1, keepdims=True) + EPS)
        h = (x * r) * scale + shift
        p = jnp.dot(h.astype(bf), w_ref[...], preferred_element_type=jnp.float32)
        cos = cos_ref[rows, :]
        sin = sin_ref[rows, :]

        def blk(off):
            return p[:, off:off + LANES]

        def rope(t):
            sw = jnp.where(first16, pltpu.roll(t, LANES - 16, axis=1), pltpu.roll(t, 16, axis=1))
            return t * cos + sw * sin

        def norm(t, g):
            return t * lax.rsqrt(_head_mean_sq(t, gsum) + EPS) * g

        def dup(t):
            sw = pltpu.roll(t, HEAD_DIM, axis=1)
            return jnp.where(lo64, t, sw), jnp.where(lo64, sw, t)

        def store_kv(kt_ref, v_ref, k_blk, v_blk):
            k0, k1 = dup(k_blk)
            v0, v1 = dup(v_blk)
            kt_ref[0:128, rows] = k0.T.astype(bf)
            kt_ref[128:256, rows] = k1.T.astype(bf)
            v_ref[rows, 0:128] = v0.astype(bf)
            v_ref[rows, 128:256] = ones
            v_ref[rows, 256:384] = v1.astype(bf)
            v_ref[rows, 384:512] = ones

        for i in range(A_Q // LANES):
            t = rope(norm(blk(OFF_AQ + i * LANES), qg)) * Q_SCALE
            qa_ref[rows, i * LANES:(i + 1) * LANES] = t.astype(bf)
        store_kv(kta_ref, va_ref, rope(norm(blk(OFF_AK), kg)), blk(OFF_AV))
        for i in range(B_CH // LANES):
            a = blk(OFF_BU + i * LANES)
            gt = blk(OFF_BU + B_CH + i * LANES)
            hgl_ref[rows, i * LANES:(i + 1) * LANES] = a * jax.nn.sigmoid(gt)
        for i in range(C_Q // LANES):
            t = rope(blk(OFF_CQ + i * LANES)) * Q_SCALE
            qc_ref[rows, i * LANES:(i + 1) * LANES] = t.astype(bf)
        store_kv(ktc_ref, vc_ref, rope(blk(OFF_CK)), blk(OFF_CV))


def _inproj(xs, mods, g1, w_in_bf, gsum, qg, kg, cos_t, sin_t, *, n, tm, n_lat_tiles, s, nb, x_offsets=(0, 0),
            combine=None):
    d = xs[0].shape[1]
    s_tiles = s // tm
    n_tiles = n // tm

    def bidx(i):
        return jnp.where(i < n_lat_tiles, (i * tm) // s, nb)

    def ridx(i):
        return jnp.where(i < n_lat_tiles, i % s_tiles, s_tiles)

    row = lambda w: pl.BlockSpec((tm, w), lambda i: (i, 0))
    ktspec = pl.BlockSpec((KT_ROWS, tm), lambda i: (0, i))
    const = lambda a: pl.BlockSpec(a.shape, lambda i: (0,) * a.ndim)
    bf = jnp.bfloat16
    modspec = pl.BlockSpec((1, N_MOD, d), lambda i: (bidx(i), 0, 0))
    in_specs = _x_specs(xs, tm, n_lat_tiles, x_offsets)
    args = list(xs)
    out_specs = [row(A_Q), ktspec, row(V_W), row(B_CH), row(C_Q), ktspec, row(V_W)]
    out_shape = [jax.ShapeDtypeStruct((n, A_Q), bf), jax.ShapeDtypeStruct((KT_ROWS, n), bf),
                 jax.ShapeDtypeStruct((n, V_W), bf), jax.ShapeDtypeStruct((n, B_CH), jnp.float32),
                 jax.ShapeDtypeStruct((n, C_Q), bf), jax.ShapeDtypeStruct((KT_ROWS, n), bf),
                 jax.ShapeDtypeStruct((n, V_W), bf)]
    aliases = {}
    if combine is not None:
        y, rw, mods_prev = combine
        assert len(xs) == 1 and y.shape[0] == 2 * n
        in_specs += [row(d // 2), pl.BlockSpec((tm, d // 2), lambda i: (i + n_tiles, 0)), row(LANES), modspec]
        args += [y, y, rw, mods_prev]
        out_specs.append(row(d))
        out_shape.append(jax.ShapeDtypeStruct((n, d), jnp.float32))
        aliases = {0: len(out_shape) - 1}
    in_specs += [modspec, const(g1), const(w_in_bf), const(gsum), const(qg), const(kg),
                 pl.BlockSpec((tm, LANES), lambda i: (ridx(i), 0)),
                 pl.BlockSpec((tm, LANES), lambda i: (ridx(i), 0))]
    args += [mods, g1, w_in_bf, gsum, qg, kg, cos_t, sin_t]
    outs = pl.pallas_call(
        functools.partial(_inproj_kernel, n_x=len(xs), n_lat_tiles=n_lat_tiles,
                          fused_combine=combine is not None),
        grid=(n_tiles,),
        in_specs=in_specs,
        out_specs=out_specs,
        out_shape=out_shape,
        input_output_aliases=aliases,
        compiler_params=_cparams("arbitrary"),
        name="inproj",
    )(*args)
    if combine is not None:
        return (outs[-1],) + tuple(outs[:-1])
    return (None,) + tuple(outs)


def _stack_heads(q_ref, tq):
    lane = lax.broadcasted_iota(jnp.int32, (tq, LANES), 1)
    lo = lane < HEAD_DIM
    qb = [q_ref[:, i * LANES:(i + 1) * LANES] for i in range(3)]
    zero = jnp.zeros_like(qb[0])
    keep_lo = lambda t: jnp.where(lo, t, zero)
    keep_hi = lambda t: jnp.where(lo, zero, t)
    s0 = jnp.concatenate([keep_lo(qb[0]), keep_hi(qb[0]), keep_lo(qb[1])], axis=0)
    s1 = jnp.concatenate([keep_hi(qb[1]), keep_lo(qb[2]), keep_hi(qb[2])], axis=0)
    return s0, s1, lo


def _unstack_store(o_ref, o0, o1, lo, tq):
    bf = jnp.bfloat16
    o_ref[:, 0:128] = jnp.where(lo, o0[0:tq], o0[tq:2 * tq]).astype(bf)
    o_ref[:, 128:256] = jnp.where(lo, o0[2 * tq:3 * tq], o1[0:tq]).astype(bf)
    o_ref[:, 256:384] = jnp.where(lo, o1[tq:2 * tq], o1[2 * tq:3 * tq]).astype(bf)


def _krow(kv):
    return slice(kv * LANES, (kv + 1) * LANES)


def _vcol(kv):
    return slice(2 * kv * LANES, (2 * kv + 2) * LANES)


def _row_max(scores):
    m = None
    for t in scores:
        for c in range(0, t.shape[1], LANES):
            blk = t[:, c:c + LANES]
            m = blk if m is None else jnp.maximum(m, blk)
    return m.max(axis=-1, keepdims=True)


def _softmax_pv(scores, values, extra=None):
    m = _row_max(scores)
    if extra is not None:
        m = jnp.maximum(m, extra)
    acc = None
    for t, v in zip(scores, values):
        c = jnp.dot(jnp.exp2(t - m).astype(jnp.bfloat16), v, preferred_element_type=jnp.float32)
        acc = c if acc is None else acc + c
    den = acc[:, LANES:2 * LANES]
    if extra is not None:
        den = den + jnp.exp2(extra - m)
    return acc[:, 0:LANES] / den


ATTN_CHAIN_ROWS = 128


def _qk(q, kt):
    return jnp.dot(q, kt, preferred_element_type=jnp.float32)


def _attn_a_kernel(q_ref, ktl_ref, ktc_ref, vl_ref, vc_ref, o_ref, *, tq):
    s0, s1, lo = _stack_heads(q_ref, tq)
    outs = []
    for kv, qs in enumerate((s0, s1)):
        krow = _krow(kv)
        vcol = _vcol(kv)
        parts = []
        for r0 in range(0, 3 * tq, ATTN_CHAIN_ROWS):
            qr = qs[r0:r0 + ATTN_CHAIN_ROWS]
            ss = [_qk(qr, ktl_ref[krow, :]), _qk(qr, ktc_ref[krow, :])]
            parts.append(_softmax_pv(ss, [vl_ref[:, vcol], vc_ref[:, vcol]]))
        outs.append(jnp.concatenate(parts, axis=0))
    _unstack_store(o_ref, outs[0], outs[1], lo, tq)


def _kv_specs(nb, s, l):
    ctx0 = nb * s // l
    return [pl.BlockSpec((KT_ROWS, s), lambda b, j: (0, b)),
            pl.BlockSpec((KT_ROWS, l), lambda b, j: (0, ctx0 + b)),
            pl.BlockSpec((s, V_W), lambda b, j: (b, 0)),
            pl.BlockSpec((l, V_W), lambda b, j: (ctx0 + b, 0))]


def _attn_a(qa, kt, v, *, nb, s, l, tq):
    n = qa.shape[0]
    n_q = s // tq
    return pl.pallas_call(
        functools.partial(_attn_a_kernel, tq=tq),
        grid=(nb, n_q),
        in_specs=[pl.BlockSpec((tq, A_Q), lambda b, j: (b * n_q + j, 0))] + _kv_specs(nb, s, l),
        out_specs=pl.BlockSpec((tq, A_Q), lambda b, j: (b * n_q + j, 0)),
        out_shape=jax.ShapeDtypeStruct((n, A_Q), jnp.bfloat16),
        compiler_params=_cparams("arbitrary", "arbitrary"),
        name="attn_global",
    )(qa, kt, kt, v, v)


WIN_BLOCKS = 8


def _sink_column(sink_ref, kv, rows):
    return jnp.concatenate([jnp.full((rows, 1), sink_ref[3 * kv + g], jnp.float32) for g in range(3)], axis=0)


def _attn_c_kernel(sink_ref, bias_ref, q_ref, ktl_ref, ktc_ref, vl_ref, vc_ref, o_ref, *, s, blocks):
    j = pl.program_id(1)
    tq = WINDOW
    band = 3 * WINDOW
    bf = jnp.bfloat16
    f32 = jnp.float32
    m3 = 3 * tq
    stacks = [_stack_heads(q_ref.at[blk * tq:(blk + 1) * tq, :], tq) for blk in range(blocks)]
    lo = stacks[0][2]
    starts = [pl.multiple_of(jnp.clip((j * blocks + blk - 1) * WINDOW, 0, s - band), WINDOW)
              for blk in range(blocks)]
    outs = [[None, None] for _ in range(blocks)]
    for kv in range(C_KV_HEADS):
        krow = _krow(kv)
        vcol = _vcol(kv)
        sk = _sink_column(sink_ref, kv, tq)
        sc_all = _qk(jnp.concatenate([st[kv] for st in stacks], axis=0), ktc_ref[krow, :])
        accs, pcs, ms = [], [], []
        for blk in range(blocks):
            start = starts[blk]
            sl = _qk(stacks[blk][kv], ktl_ref[krow, pl.ds(start, band)]) + bias_ref[j * blocks + blk - start // WINDOW]
            sc = sc_all[blk * m3:(blk + 1) * m3]
            m = jnp.maximum(_row_max([sl, sc]), sk)
            accs.append(jnp.dot(jnp.exp2(sl - m).astype(bf), vl_ref[pl.ds(start, band), vcol],
                                preferred_element_type=f32))
            pcs.append(jnp.exp2(sc - m).astype(bf))
            ms.append(m)
        acc_ctx = jnp.dot(jnp.concatenate(pcs, axis=0), vc_ref[:, vcol], preferred_element_type=f32)
        for blk in range(blocks):
            acc = accs[blk] + acc_ctx[blk * m3:(blk + 1) * m3]
            den = acc[:, LANES:2 * LANES] + jnp.exp2(sk - ms[blk])
            outs[blk][kv] = acc[:, 0:LANES] / den
    for blk in range(blocks):
        _unstack_store(o_ref.at[blk * tq:(blk + 1) * tq, :], outs[blk][0], outs[blk][1], lo, tq)


def _window_bias():
    r = np.arange(3 * WINDOW)[:, None] % WINDOW
    col = np.arange(3 * WINDOW)[None, :]
    tabs = [np.where(np.abs(col - r - WINDOW * off) <= WINDOW, 0.0, -np.inf) for off in range(3)]
    return jnp.asarray(np.stack(tabs), jnp.float32)


def _attn_c(qc, kt, v, sink2, *, nb, s, l):
    n = qc.shape[0]
    blocks = _pick(s // WINDOW, (WIN_BLOCKS, 4, 2, 1))
    tq = blocks * WINDOW
    n_q = s // tq
    bias = _window_bias()
    return pl.pallas_call(
        functools.partial(_attn_c_kernel, s=s, blocks=blocks),
        grid=(nb, n_q),
        in_specs=[pl.BlockSpec(memory_space=pltpu.SMEM),
                  pl.BlockSpec(bias.shape, lambda b, j: (0, 0, 0)),
                  pl.BlockSpec((tq, C_Q), lambda b, j: (b * n_q + j, 0))] + _kv_specs(nb, s, l),
        out_specs=pl.BlockSpec((tq, C_Q), lambda b, j: (b * n_q + j, 0)),
        out_shape=jax.ShapeDtypeStruct((n, C_Q), jnp.bfloat16),
        compiler_params=_cparams("arbitrary", "arbitrary"),
        name="attn_window",
    )(sink2, bias, qc, kt, kt, v, v)


def _attn_ctx_kernel(sink_ref, qa_ref, kta_ref, va_ref, qc_ref, ktc_ref, vc_ref, oa_in, oc_in, oa_ref, oc_ref, *, l):
    del oa_in, oc_in
    for q_ref, kt_ref, v_ref, o_ref, with_sink in ((qa_ref, kta_ref, va_ref, oa_ref, False),
                                                   (qc_ref, ktc_ref, vc_ref, oc_ref, True)):
        s0, s1, lo = _stack_heads(q_ref, l)
        outs = []
        for kv, qs in enumerate((s0, s1)):
            sk = _sink_column(sink_ref, kv, l) if with_sink else None
            outs.append(_softmax_pv([_qk(qs, kt_ref[_krow(kv), :])], [v_ref[:, _vcol(kv)]], sk))
        _unstack_store(o_ref, outs[0], outs[1], lo, l)


def _attn_ctx(qa, kta, va, qc, ktc, vc, sink2, oa, oc, *, nb, s, l):
    base = nb * s // l
    row = lambda w: pl.BlockSpec((l, w), lambda b: (base + b, 0))
    ktspec = pl.BlockSpec((KT_ROWS, l), lambda b: (0, base + b))
    anyspec = pl.BlockSpec(memory_space=pl.ANY)
    return pl.pallas_call(
        functools.partial(_attn_ctx_kernel, l=l),
        grid=(nb,),
        in_specs=[pl.BlockSpec(memory_space=pltpu.SMEM), row(A_Q), ktspec, row(V_W), row(C_Q), ktspec, row(V_W),
                  anyspec, anyspec],
        out_specs=[row(A_Q), row(C_Q)],
        out_shape=[jax.ShapeDtypeStruct(oa.shape, oa.dtype), jax.ShapeDtypeStruct(oc.shape, oc.dtype)],
        input_output_aliases={7: 0, 8: 1},
        compiler_params=_cparams("arbitrary"),
        name="attn_context",
    )(sink2, qa, kta, va, qc, ktc, vc, oa, oc)


CONV_HALO = 16
CONV_ROWS = 64


def _conv_kernel(prev_ref, cur_ref, next_ref, w_ref, b_ref, g_ref, beta_ref, *rest, chunks, tc):
    o_ref, sh_ref = rest[-2:]
    j = pl.program_id(1)
    has_prev = j > 0
    has_next = j < chunks - 1
    rows = tc + 2 * CONV_HALO
    sh_ref[0, 0:CONV_HALO, :] = jnp.where(has_prev, prev_ref[...], 0.0)
    sh_ref[0, CONV_HALO:CONV_HALO + tc, :] = cur_ref[...]
    sh_ref[0, CONV_HALO + tc:rows, :] = jnp.where(has_next, next_ref[...], 0.0)
    for b in range(1, SUBLANES):
        sh_ref[b, 0:rows - SUBLANES, :] = sh_ref[0, b:b + rows - SUBLANES, :]
    base = CONV_HALO - CONV_W // 2
    for r0 in range(0, tc, CONV_ROWS):
        acc = None
        for k in range(CONV_W):
            a, b = divmod(base + k, SUBLANES)
            term = sh_ref[b, SUBLANES * a + r0:SUBLANES * a + r0 + CONV_ROWS, :] * w_ref[k:k + 1, :]
            acc = term if acc is None else acc + term
        hc = acc + b_ref[...]
        mu = jnp.mean(hc, axis=-1, keepdims=True)
        xc = hc - mu
        var = jnp.mean(xc * xc, axis=-1, keepdims=True)
        y = xc * lax.rsqrt(var + EPS) * g_ref[...] + beta_ref[...]
        o_ref[r0:r0 + CONV_ROWS, :] = (y * jax.nn.sigmoid(y)).astype(o_ref.dtype)


def _conv(hgl, w, b, g, beta, *, nb, seq, base_rows, tc, prev_out=None):
    n = hgl.shape[0]
    chunks = seq // tc
    base = base_rows // tc

    def idx(b_, j, delta):
        return base + b_ * chunks + jnp.clip(j + delta, 0, chunks - 1)

    blk = lambda delta: pl.BlockSpec((tc, B_CH), lambda b_, j: (idx(b_, j, delta), 0))
    const = lambda a: pl.BlockSpec(a.shape, lambda b_, j: (0,) * a.ndim)
    hb = tc // CONV_HALO
    last_halo = n // CONV_HALO - 1
    prev_halo = pl.BlockSpec((CONV_HALO, B_CH), lambda b_, j: (jnp.maximum(idx(b_, j, 0) * hb - 1, 0), 0))
    next_halo = pl.BlockSpec((CONV_HALO, B_CH),
                             lambda b_, j: (jnp.minimum((idx(b_, j, 0) + 1) * hb, last_halo), 0))
    in_specs = [prev_halo, blk(0), next_halo, const(w), const(b), const(g), const(beta)]
    args = [hgl, hgl, hgl, w, b, g, beta]
    aliases = {}
    if prev_out is not None:
        in_specs.append(pl.BlockSpec(memory_space=pl.ANY))
        args.append(prev_out)
        aliases = {len(args) - 1: 0}
    return pl.pallas_call(
        functools.partial(_conv_kernel, chunks=chunks, tc=tc),
        grid=(nb, chunks),
        in_specs=in_specs,
        out_specs=blk(0),
        out_shape=jax.ShapeDtypeStruct((n, B_CH), jnp.bfloat16),
        scratch_shapes=[pltpu.VMEM((SUBLANES, tc + 2 * CONV_HALO, B_CH), jnp.float32)],
        input_output_aliases=aliases,
        compiler_params=_cparams("arbitrary", "arbitrary"),
        name="conformer_conv",
    )(*args)


META_ROWS = 8


def _outproj_kernel(*refs, n_x, n_lat_tiles):
    x_refs = refs[:n_x]
    (oa_ref, ob_ref, oc_ref, mod_ref, g_ref, w_ref, wr_ref, br_ref, tri_ref,
     xo_ref, hp_ref, meta_ref, rw_ref, cnt_ref) = refs[n_x:]
    i = pl.program_id(0)
    tm = xo_ref.shape[0]
    f32 = jnp.float32
    tr = tri_ref.shape[0]
    gate = mod_ref[0, 2:3, :]
    scale = g_ref[...] * (1.0 + mod_ref[0, 4:5, :])
    shift = mod_ref[0, 3:4, :]
    lane = lax.broadcasted_iota(jnp.int32, (tr, LANES), 1).astype(f32)
    big = float(LANES)
    ninf = -jnp.inf
    tri = tri_ref[...]

    @pl.when(i == 0)
    def _():
        cnt_ref[...] = jnp.zeros_like(cnt_ref)

    cnt = cnt_ref[0:1, :]
    for r0 in range(0, tm, tr):
        rows = slice(r0, r0 + tr)
        lat = jnp.concatenate([oa_ref[rows, :], ob_ref[rows, :], oc_ref[rows, :]], axis=1)
        mix = jnp.dot(lat, w_ref[...], preferred_element_type=f32)
        x = _load_x(x_refs, n_lat_tiles, rows) + gate * mix
        xo_ref[rows, :] = x
        r = lax.rsqrt(jnp.mean(x * x, axis=-1, keepdims=True) + EPS)
        h = (x * r) * scale + shift
        h_hi = h.astype(jnp.bfloat16)
        hp_ref[rows, :] = _pack_pairs(h_hi)
        h_lo = (h - h_hi.astype(f32)).astype(jnp.bfloat16)
        r_hi = jnp.dot(h_hi, wr_ref[...], preferred_element_type=f32)
        r_lo = jnp.dot(h_lo, wr_ref[:, 0:LANES], preferred_element_type=f32)
        lg = r_hi[:, 0:LANES] + r_hi[:, LANES:2 * LANES] + r_lo + br_ref[...]
        glog = jnp.where(lane < N_GROUPS, lg, ninf)
        gmax = glog.max(axis=-1, keepdims=True)
        g_val = 1.0 / jnp.exp(glog - gmax).sum(axis=-1, keepdims=True)
        g_idx = jnp.where(glog == gmax, lane, big).min(axis=-1, keepdims=True)
        e_lo = N_GROUPS + EXPERTS_PER_GROUP * g_idx
        el = jnp.where((lane >= e_lo) & (lane < e_lo + EXPERTS_PER_GROUP), lg, ninf)
        v0 = el.max(axis=-1, keepdims=True)
        i0 = jnp.where(el == v0, lane, big).min(axis=-1, keepdims=True)
        el1 = jnp.where(lane == i0, ninf, el)
        v1 = el1.max(axis=-1, keepdims=True)
        i1 = jnp.where(el1 == v1, lane, big).min(axis=-1, keepdims=True)
        t = jnp.exp(v1 - v0)
        w0 = g_val / (1.0 + t)
        w1 = g_val * t / (1.0 + t)
        e0 = i0 - N_GROUPS
        e1 = i1 - N_GROUPS

        ranks = []
        for e in (e0, e1):
            oh = lane == e
            ohf = oh.astype(f32)
            pre = jnp.dot(tri, ohf.astype(jnp.bfloat16), preferred_element_type=f32) + cnt
            ranks.append(jnp.where(oh, pre, 0.0).sum(axis=-1, keepdims=True))
            cnt = cnt + ohf.sum(axis=0, keepdims=True)
        rw_ref[rows, :] = jnp.where(lane == 0, w0, jnp.where(lane == 1, w1, 0.0))
        rec = jnp.where(lane == 0, e0, jnp.where(lane == 1, e1, jnp.where(lane == 2, ranks[0],
                        jnp.where(lane == 3, ranks[1], jnp.where(lane == 4, w0, jnp.where(lane == 5, w1, 0.0))))))
        meta_ref[:, rows] = rec.T[0:META_ROWS, :]
    cnt_ref[0:1, :] = cnt


def _outproj(xs, oa, ob, oc, mods, g2, w_out_bf, wr, br, tri, *, n, tm, n_tiles, n_lat_tiles, s, nb,
             x_offsets=(0, 0)):
    d = xs[0].shape[1]
    rows = n_tiles * tm

    def bidx(i):
        return jnp.where(i < n_lat_tiles, (i * tm) // s, nb)

    row = lambda w: pl.BlockSpec((tm, w), lambda i: (i, 0))
    const = lambda a: pl.BlockSpec(a.shape, lambda i: (0,) * a.ndim)
    return pl.pallas_call(
        functools.partial(_outproj_kernel, n_x=len(xs), n_lat_tiles=n_lat_tiles),
        grid=(n_tiles,),
        in_specs=_x_specs(xs, tm, n_lat_tiles, x_offsets) + [
                  row(A_Q), row(B_CH), row(C_Q),
                  pl.BlockSpec((1, N_MOD, d), lambda i: (bidx(i), 0, 0)),
                  const(g2), const(w_out_bf), const(wr), const(br), const(tri)],
        out_specs=[row(d), row(d // 2), pl.BlockSpec((META_ROWS, tm), lambda i: (0, i)), row(LANES),
                   pl.BlockSpec((8, LANES), lambda i: (0, 0))],
        out_shape=[jax.ShapeDtypeStruct((n, d), jnp.float32),
                   jax.ShapeDtypeStruct((rows, d // 2), jnp.int32),
                   jax.ShapeDtypeStruct((META_ROWS, rows), jnp.float32),
                   jax.ShapeDtypeStruct((rows, LANES), jnp.float32),
                   jax.ShapeDtypeStruct((8, LANES), jnp.float32)],
        input_output_aliases={0: 0} if len(xs) == 1 else {},
        compiler_params=_cparams("arbitrary"),
        name="outproj_router",
    )(*xs, oa, ob, oc, mods, g2, w_out_bf, wr, br, tri)


def _sc_mesh():
    return plsc.VectorSubcoreMesh(core_axis_name="core", subcore_axis_name="subcore")


def sc_gather_rows(table, idx2):
    r = idx2.shape[1]
    w = table.shape[1]
    assert r % (2 * SC_WINDOW) == 0
    half = r // SC_WINDOW // 2

    @functools.partial(pl.kernel, out_type=jax.ShapeDtypeStruct((r, w), table.dtype), mesh=_sc_mesh())
    def k(x_hbm, i_hbm, o_hbm):
        def body(i_vmem, o_vmem):
            pltpu.sync_copy(x_hbm.at[i_vmem.at[0]], o_vmem)

        pltpu.emit_pipeline(
            body,
            grid=(2, half),
            in_specs=[pl.BlockSpec((1, SC_WINDOW), lambda c, i: (0, c * half + i))],
            out_specs=[pl.BlockSpec((SC_WINDOW, w), lambda c, i: (c * half + i, 0),
                                    pipeline_mode=pl.Buffered(1))],
            core_axis_name=("core", "subcore"),
            dimension_semantics=(pltpu.PARALLEL, pltpu.PARALLEL),
        )(i_hbm, o_hbm)

    return k(table, idx2)


def sc_scatter_rows2(rows, idx2, n_out):
    r, w = rows.shape
    assert idx2.shape == (1, 2 * r) and r % (2 * SC_WINDOW) == 0
    windows = r // SC_WINDOW
    half = windows // 2

    @functools.partial(pl.kernel, out_type=jax.ShapeDtypeStruct((n_out, w), rows.dtype), mesh=_sc_mesh(),
                       scratch_types=[])
    def k(x_hbm, ia_hbm, ib_hbm, o_hbm):
        def body(x_vmem, ia_vmem, ib_vmem):
            pltpu.sync_copy(x_vmem, o_hbm.at[ia_vmem.at[0]])
            pltpu.sync_copy(x_vmem, o_hbm.at[ib_vmem.at[0]])

        pltpu.emit_pipeline(
            body,
            grid=(2, half),
            in_specs=[pl.BlockSpec((SC_WINDOW, w), lambda c, i: (c * half + i, 0),
                                   pipeline_mode=pl.Buffered(1)),
                      pl.BlockSpec((1, SC_WINDOW), lambda c, i: (0, c * half + i)),
                      pl.BlockSpec((1, SC_WINDOW), lambda c, i: (0, windows + c * half + i))],
            out_specs=[],
            core_axis_name=("core", "subcore"),
            dimension_semantics=(pltpu.PARALLEL, pltpu.PARALLEL),
        )(x_hbm, ia_hbm, ib_hbm)

    return k(rows, idx2, idx2)


def _expert_kernel(te_ref, nv_ref, x_ref, wg_ref, wu_ref, wd_ref, o_ref, wgb_ref, wub_ref, wdb_ref):
    t = pl.program_id(0)
    nvalid = nv_ref[t]

    @pl.when((t == 0) | (te_ref[t] != te_ref[jnp.maximum(t - 1, 0)]))
    def _():
        wgb_ref[...] = wg_ref[0].astype(jnp.bfloat16)
        wub_ref[...] = wu_ref[0].astype(jnp.bfloat16)
        wdb_ref[...] = wd_ref[0].astype(jnp.bfloat16)

    @pl.when(nvalid > 0)
    def _():
        rows = lax.broadcasted_iota(jnp.int32, x_ref.shape, 0)
        lo, hi = _unpack_pairs(jnp.where(rows < nvalid, x_ref[...], 0))
        xb = jnp.concatenate([lo, hi], axis=1).astype(jnp.bfloat16)
        g = jnp.dot(xb, wgb_ref[...], preferred_element_type=jnp.float32)
        u = jnp.dot(xb, wub_ref[...], preferred_element_type=jnp.float32)
        a = (g * jax.nn.sigmoid(g) * u).astype(jnp.bfloat16)
        o_ref[...] = _pack_pairs(jnp.dot(a, wdb_ref[...], preferred_element_type=jnp.float32))

    @pl.when(nvalid == 0)
    def _():
        o_ref[...] = jnp.zeros_like(o_ref)


def _experts(buf, tile_expert, tile_nvalid, wg, wu, wd, *, layer):
    rows, wp = buf.shape
    _, _, d, f = wg.shape
    n_tiles = rows // EXPERT_TILE
    grid_spec = pltpu.PrefetchScalarGridSpec(
        num_scalar_prefetch=2,
        grid=(n_tiles,),
        in_specs=[pl.BlockSpec((EXPERT_TILE, wp), lambda t, te, nv: (t, 0)),
                  pl.BlockSpec((None, 1, d, f), lambda t, te, nv: (layer, te[t], 0, 0)),
                  pl.BlockSpec((None, 1, d, f), lambda t, te, nv: (layer, te[t], 0, 0)),
                  pl.BlockSpec((None, 1, f, d), lambda t, te, nv: (layer, te[t], 0, 0))],
        out_specs=pl.BlockSpec((EXPERT_TILE, wp), lambda t, te, nv: (t, 0)),
        scratch_shapes=[pltpu.VMEM((d, f), jnp.bfloat16), pltpu.VMEM((d, f), jnp.bfloat16),
                        pltpu.VMEM((f, d), jnp.bfloat16)],
    )
    return pl.pallas_call(
        _expert_kernel,
        grid_spec=grid_spec,
        out_shape=jax.ShapeDtypeStruct((rows, wp), jnp.int32),
        compiler_params=_cparams("arbitrary"),
        name="expert_ffn",
    )(tile_expert, tile_nvalid, buf, wg, wu, wd)


def _final_kernel(y0_ref, y1_ref, rw_ref, x_ref, mod_ref, fg_ref, o_ref):
    x = x_ref[...] + mod_ref[0, 5:6, :] * _moe_mix(y0_ref, y1_ref, rw_ref, slice(None))
    r = lax.rsqrt(jnp.mean(x * x, axis=-1, keepdims=True) + EPS)
    o_ref[...] = x * r * fg_ref[...]


def _final_kernel_into(y0_ref, y1_ref, rw_ref, x_ref, mod_ref, fg_ref, prev_ref, o_ref):
    del prev_ref
    _final_kernel(y0_ref, y1_ref, rw_ref, x_ref, mod_ref, fg_ref, o_ref)


def _final_combine(y, rw, xall, mods, final_g, *, tm, n_tiles, s, out_rows, out_tile0, prev_out=None):
    d = xall.shape[1]
    row = lambda w: pl.BlockSpec((tm, w), lambda i: (i, 0))
    in_specs = [row(d // 2),
                pl.BlockSpec((tm, d // 2), lambda i: (i + n_tiles, 0)),
                row(LANES), row(d),
                pl.BlockSpec((1, N_MOD, d), lambda i: ((i * tm) // s, 0, 0)),
                pl.BlockSpec(final_g.shape, lambda i: (0, 0))]
    args = [y, y, rw, xall, mods, final_g]
    aliases = {}
    body = _final_kernel
    if prev_out is not None:
        in_specs.append(pl.BlockSpec(memory_space=pl.ANY))
        args.append(prev_out)
        aliases = {len(args) - 1: 0}
        body = _final_kernel_into
    return pl.pallas_call(
        body,
        grid=(n_tiles,),
        in_specs=in_specs,
        out_specs=pl.BlockSpec((tm, d), lambda i: (out_tile0 + i, 0)),
        out_shape=jax.ShapeDtypeStruct((out_rows, d), jnp.float32),
        input_output_aliases=aliases,
        compiler_params=_cparams("arbitrary"),
        name="moe_combine_final",
    )(*args)


def _dest_kernel(ps_ref, meta_ref, o_ref):
    slot = pl.program_id(0)
    e = meta_ref[pl.ds(slot, 1), :]
    d = meta_ref[pl.ds(TOP_K + slot, 1), :]
    for k in range(N_EXPERTS):
        d = d + jnp.where(e == float(k), ps_ref[k], 0.0)
    o_ref[...] = d.astype(jnp.int32)


def _dest_rows(meta, pstarts, tcols):
    n = meta.shape[1]
    nt = n // tcols
    return pl.pallas_call(
        _dest_kernel,
        grid=(TOP_K, nt),
        in_specs=[pl.BlockSpec(memory_space=pltpu.SMEM),
                  pl.BlockSpec((META_ROWS, tcols), lambda k, i: (0, i))],
        out_specs=pl.BlockSpec((1, tcols), lambda k, i: (0, k * nt + i)),
        out_shape=jax.ShapeDtypeStruct((1, TOP_K * n), jnp.int32),
        compiler_params=_cparams("arbitrary", "arbitrary"),
        name="moe_dest",
    )(pstarts.astype(jnp.float32), meta)


def _dispatch_plan(meta, counts, n_rows_buf, tcols):
    cnt = counts[0, :N_EXPERTS].astype(jnp.int32)
    padded = (cnt + EXPERT_TILE - 1) // EXPERT_TILE * EXPERT_TILE
    pends = jnp.cumsum(padded)
    pstarts = pends - padded
    dest = _dest_rows(meta, pstarts, tcols)
    tile_start = jnp.arange(n_rows_buf // EXPERT_TILE, dtype=jnp.int32) * EXPERT_TILE
    te = jnp.sum((tile_start[:, None] >= pends[None, :]).astype(jnp.int32), axis=1)
    te = jnp.minimum(te, N_EXPERTS - 1)
    nvalid = jnp.clip(cnt[te] - (tile_start - pstarts[te]), 0, EXPERT_TILE).astype(jnp.int32)
    return dest, te, nvalid


def _rope_tables(s, tm):
    pos = np.arange(s)
    pos_row = jnp.asarray(pos // GRID_W, jnp.float32)
    pos_col = jnp.asarray(pos % GRID_W, jnp.float32)
    n_freq = HEAD_DIM // 4
    inv = ROPE_THETA ** (-jnp.arange(n_freq, dtype=jnp.float32) / n_freq)
    ang_row = pos_row[:, None] * inv
    ang_col = pos_col[:, None] * inv
    ang = jnp.concatenate([ang_row, ang_row, ang_col, ang_col] * (LANES // HEAD_DIM), axis=-1)
    sign = np.where((np.arange(LANES) % 32) < 16, -1.0, 1.0).astype(np.float32)
    cos_t = jnp.concatenate([jnp.cos(ang), jnp.ones((tm, LANES), jnp.float32)], axis=0)
    sin_t = jnp.concatenate([jnp.sin(ang) * sign, jnp.zeros((tm, LANES), jnp.float32)], axis=0)
    return cos_t, sin_t


def kernel(x, c, ctx, c_ctx, norm1_g, norm2_g, w_mod, b_mod, w_in, q_norm_g, k_norm_g, conv_w, conv_b, conv_ln_g, conv_ln_b, sink, w_out, w_group, b_group, w_expert, b_expert, w_gate, w_up, w_down, final_g):
    nb, s, d = x.shape
    l = ctx.shape[1]
    depth = w_in.shape[0]
    assert w_in.shape[2] == D_IN and w_out.shape[1] == D_MIX
    assert s % GRID_W == 0 and s >= 3 * WINDOW and s % WINDOW == 0 and l % WINDOW == 0
    groups = BATCH_GROUPS if nb % BATCH_GROUPS == 0 else 1
    nbg = nb // groups
    n_lat, n_ctx = nbg * s, nbg * l
    tm = _pick(np.gcd(s, n_ctx), (512, 256, 128))
    tq = _pick(s, (512, 256, 128))
    tc = _pick(np.gcd(s, l), (256, 128))
    assert n_lat % l == 0
    bf = jnp.bfloat16
    f32 = jnp.float32

    x2d = x.reshape(nb * s, d)
    ctx2d = ctx.reshape(nb * l, d)
    c_all = jnp.concatenate([c, c_ctx[None, :]], axis=0)
    mods_all = _modulation(c_all, w_mod, b_mod).reshape(depth, nb + 1, N_MOD, d)
    cos_t, sin_t = _rope_tables(s, tm)
    head_id = np.arange(LANES) // HEAD_DIM
    gsum = jnp.asarray((head_id[:, None] == head_id[None, :]) / HEAD_DIM, bf)
    tri = jnp.asarray(np.tril(np.ones((tm, tm), np.float32), -1), bf)
    n_lat_tiles = n_lat // tm

    xs = [(x2d, ctx2d)] * groups
    pending = [None] * groups
    for i in range(depth):
        last = i == depth - 1
        with_ctx = not last
        qg = jnp.tile(q_norm_g[i], LANES // HEAD_DIM)[None, :]
        kg = jnp.tile(k_norm_g[i], LANES // HEAD_DIM)[None, :]
        w_in_bf = w_in[i].astype(bf)
        w_out_bf = w_out[i].astype(bf)
        sink2 = sink[i] * LOG2E
        conv_args = (conv_w[i].reshape(CONV_W, B_CH), conv_b[i][None, :], conv_ln_g[i][None, :],
                     conv_ln_b[i][None, :])
        wr32 = jnp.zeros((d, LANES), f32).at[:, :N_GROUPS].set(w_group[i])
        wr32 = wr32.at[:, N_GROUPS:N_GROUPS + N_EXPERTS].set(w_expert[i])
        wr_hi = wr32.astype(bf)
        wr = jnp.concatenate([wr_hi, (wr32 - wr_hi.astype(f32)).astype(bf)], axis=1)
        br = jnp.zeros((1, LANES), f32).at[0, :N_GROUPS].set(b_group[i])
        br = br.at[0, N_GROUPS:N_GROUPS + N_EXPERTS].set(b_expert[i])
        n_tok = n_lat + n_ctx if with_ctx else n_lat
        n_tiles = n_tok // tm
        n_rows_buf = 2 * n_tok + N_EXPERTS * EXPERT_TILE
        for g in range(groups):
            b0 = g * nbg
            mods = jnp.concatenate([mods_all[i, b0:b0 + nbg], mods_all[i, nb:nb + 1]], axis=0)
            x_offsets = (b0 * s // tm, b0 * l // tm)
            x_new, qa, kta, va, hgl, qc, ktc, vc = _inproj(
                xs[g], mods, norm1_g[i][None, :], w_in_bf, gsum, qg, kg, cos_t, sin_t, n=n_lat + n_ctx,
                tm=tm, n_lat_tiles=n_lat_tiles, s=s, nb=nbg, x_offsets=x_offsets, combine=pending[g])
            if pending[g] is not None:
                xs[g] = (x_new,)
            oa = _attn_a(qa, kta, va, nb=nbg, s=s, l=l, tq=tq)
            oc = _attn_c(qc, ktc, vc, sink2, nb=nbg, s=s, l=l)
            ob = _conv(hgl, *conv_args, nb=nbg, seq=s, base_rows=0, tc=tc)
            if with_ctx:
                oa, oc = _attn_ctx(qa, kta, va, qc, ktc, vc, sink2, oa, oc, nb=nbg, s=s, l=l)
                ob = _conv(hgl, *conv_args, nb=nbg, seq=l, base_rows=n_lat, tc=tc, prev_out=ob)
            xall, hp, meta, rw, counts = _outproj(xs[g], oa, ob, oc, mods, norm2_g[i][None, :], w_out_bf,
                                                  wr, br, tri, n=n_lat + n_ctx, tm=tm, n_tiles=n_tiles,
                                                  n_lat_tiles=n_lat_tiles, s=s, nb=nbg, x_offsets=x_offsets)
            dest, te, nvalid = _dispatch_plan(meta, counts, n_rows_buf,
                                              _pick(n_tok, (8192, 4096, 2048, 1024, 512, 256, 128)))
            buf = sc_scatter_rows2(hp, dest, n_rows_buf)
            eo = _experts(buf, te, nvalid, w_gate, w_up, w_down, layer=i)
            y = sc_gather_rows(eo, dest)
            xs[g] = (xall,)
            pending[g] = (y, rw, mods)
    out = None
    for g in range(groups):
        y, rw, mods = pending[g]
        out = _final_combine(y, rw, xs[g][0], mods, final_g[None, :], tm=tm, n_tiles=n_lat_tiles, s=s,
                             out_rows=nb * s, out_tile0=g * n_lat_tiles, prev_out=out)
    return out.reshape(nb, s, d)
```

```python
import functools

import jax
import jax.numpy as jnp
import numpy as np
from jax import lax
from jax.experimental import pallas as pl
from jax.experimental.pallas import tpu as pltpu
from jax.experimental.pallas import tpu_sc as plsc

HEAD_DIM = 64
GRID_W = 64
ROPE_THETA = 10000.0
A_HEADS, A_KV_HEADS = 6, 2
C_HEADS, C_KV_HEADS = 6, 2
B_CH = 256
CONV_W = 31
WINDOW = 128
N_GROUPS = 4
EXPERTS_PER_GROUP = 8
N_EXPERTS = N_GROUPS * EXPERTS_PER_GROUP
TOP_K = 2
N_MOD = 6
EPS = 1e-6
ATTN_SCALE = HEAD_DIM ** -0.5
LOG2E = 1.4426950408889634
Q_SCALE = ATTN_SCALE * LOG2E

A_Q = A_HEADS * HEAD_DIM
A_KV = A_KV_HEADS * HEAD_DIM
C_Q = C_HEADS * HEAD_DIM
C_KV = C_KV_HEADS * HEAD_DIM
D_MIX = A_Q + B_CH + C_Q
OFF_AQ = 0
OFF_AK = OFF_AQ + A_Q
OFF_AV = OFF_AK + A_KV
OFF_BU = OFF_AV + A_KV
OFF_CQ = OFF_BU + 2 * B_CH
OFF_CK = OFF_CQ + C_Q
OFF_CV = OFF_CK + C_KV
D_IN = OFF_CV + C_KV

LANES = 128
SUBLANES = 8
KT_ROWS = 2 * LANES
V_W = 4 * LANES
EXPERT_TILE = 512
SC_WINDOW = 128
VMEM_LIMIT = 56 * 1024 * 1024
HI_MASK = -65536
BATCH_GROUPS = 2
ROW_CHAIN = 256


def _cparams(*sem):
    return pltpu.CompilerParams(dimension_semantics=sem, vmem_limit_bytes=VMEM_LIMIT)


def _pick(n, cands):
    for c in cands:
        if n % c == 0:
            return c
    raise ValueError(f"no tile in {cands} divides {n}")


def _pack_pairs(x):
    w = x.shape[1] // 2
    lo = lax.bitcast_convert_type(x[:, :w].astype(jnp.bfloat16).astype(jnp.float32), jnp.int32)
    hi = lax.bitcast_convert_type(x[:, w:].astype(jnp.bfloat16).astype(jnp.float32), jnp.int32)
    return (hi & HI_MASK) | lax.shift_right_logical(lo, 16)


def _unpack_pairs(p):
    lo = lax.bitcast_convert_type(lax.shift_left(p, 16), jnp.float32)
    hi = lax.bitcast_convert_type(p & HI_MASK, jnp.float32)
    return lo, hi


def _mod_kernel(c_ref, w_ref, b_ref, o_ref):
    c = c_ref[...]
    a = c * jax.nn.sigmoid(c)
    o_ref[0] = jnp.dot(a, w_ref[0], preferred_element_type=jnp.float32,
                       precision=lax.Precision.HIGHEST) + b_ref[0]


def _modulation(c_all, w_mod, b_mod):
    depth, d, n = w_mod.shape
    r = c_all.shape[0]
    tn = _pick(n, (1024, 512, 256, 128))
    return pl.pallas_call(
        _mod_kernel,
        grid=(depth, n // tn),
        in_specs=[pl.BlockSpec((r, d), lambda l, j: (0, 0)),
                  pl.BlockSpec((1, d, tn), lambda l, j: (l, 0, j)),
                  pl.BlockSpec((1, 1, tn), lambda l, j: (l, 0, j))],
        out_specs=pl.BlockSpec((1, r, tn), lambda l, j: (l, 0, j)),
        out_shape=jax.ShapeDtypeStruct((depth, r, n), jnp.float32),
        compiler_params=_cparams("arbitrary", "arbitrary"),
        name="modulation",
    )(c_all, w_mod, b_mod.reshape(depth, 1, n))


def _head_mean_sq(blk, gsum):
    sq = blk * blk
    hi = sq.astype(jnp.bfloat16)
    lo = (sq - hi.astype(jnp.float32)).astype(jnp.bfloat16)
    return (jnp.dot(hi, gsum, preferred_element_type=jnp.float32)
            + jnp.dot(lo, gsum, preferred_element_type=jnp.float32))


def _x_specs(xs, tm, n_lat_tiles, offsets=(0, 0), tile_of=lambda i: i):
    d = xs[0].shape[1]
    if len(xs) == 1:
        return [pl.BlockSpec((tm, d), lambda i: (tile_of(i), 0))]
    lat0, ctx0 = offsets
    return [pl.BlockSpec((tm, d), lambda i: (lat0 + jnp.minimum(tile_of(i), n_lat_tiles - 1), 0)),
            pl.BlockSpec((tm, d), lambda i: (ctx0 + jnp.maximum(tile_of(i) - n_lat_tiles, 0), 0))]


def _load_x(x_refs, n_lat_tiles, rows=slice(None), tile=None):
    if len(x_refs) == 1:
        return x_refs[0][rows, :]
    tile = pl.program_id(0) if tile is None else tile
    return jnp.where(tile < n_lat_tiles, x_refs[0][rows, :], x_refs[1][rows, :])


def _moe_mix(y0_ref, y1_ref, rw_ref, rows):
    rw = rw_ref[rows, :]
    w0 = rw[:, 0:1]
    w1 = rw[:, 1:2]
    a_lo, a_hi = _unpack_pairs(y0_ref[rows, :])
    b_lo, b_hi = _unpack_pairs(y1_ref[rows, :])
    return jnp.concatenate([a_lo * w0 + b_lo * w1, a_hi * w0 + b_hi * w1], axis=1)


def _inproj_kernel(*refs, n_x, n_lat_tiles, fused_combine):
    x_refs = refs[:n_x]
    refs = refs[n_x:]
    if fused_combine:
        y0_ref, y1_ref, rw_ref, modp_ref = refs[:4]
        refs = refs[4:]
        xo_ref = refs[-1]
        refs = refs[:-1]
    (mod_ref, g_ref, w_ref, gsum_ref, qg_ref, kg_ref, cos_ref, sin_ref,
     qa_ref, kta_ref, va_ref, hgl_ref, qc_ref, ktc_ref, vc_ref) = refs
    tm = qa_ref.shape[0]
    tr = min(tm, ROW_CHAIN)
    bf = jnp.bfloat16
    lane = lax.broadcasted_iota(jnp.int32, (tr, LANES), 1)
    first16 = (lane % 32) < 16
    lo64 = lane < HEAD_DIM
    gsum = gsum_ref[...]
    qg = qg_ref[...]
    kg = kg_ref[...]
    scale = g_ref[...] * (1.0 + mod_ref[0, 1:2, :])
    shift = mod_ref[0, 0:1, :]
    ones = jnp.ones((tr, LANES), bf)

    for r0 in range(0, tm, tr):
        rows = slice(r0, r0 + tr)
        x = _load_x(x_refs, n_lat_tiles, rows)
        if fused_combine:
            x = x + modp_ref[0, 5:6, :] * _moe_mix(y0_ref, y1_ref, rw_ref, rows)
            xo_ref[rows, :] = x
        r = lax.rsqrt(jnp.mean(x * x, axis=-1, keepdims=True) + EPS)
        h = (x * r) * scale + shift
        p = jnp.dot(h.astype(bf), w_ref[...], preferred_element_type=jnp.float32)
        cos = cos_ref[rows, :]
        sin = sin_ref[rows, :]

        def blk(off):
            return p[:, off:off + LANES]

        def rope(t):
            sw = jnp.where(first16, pltpu.roll(t, LANES - 16, axis=1), pltpu.roll(t, 16, axis=1))
            return t * cos + sw * sin

        def norm(t, g):
            return t * lax.rsqrt(_head_mean_sq(t, gsum) + EPS) * g

        def dup(t):
            sw = pltpu.roll(t, HEAD_DIM, axis=1)
            return jnp.where(lo64, t, sw), jnp.where(lo64, sw, t)

        def store_kv(kt_ref, v_ref, k_blk, v_blk):
            k0, k1 = dup(k_blk)
            v0, v1 = dup(v_blk)
            kt_ref[0:128, rows] = k0.T.astype(bf)
            kt_ref[128:256, rows] = k1.T.astype(bf)
            v_ref[rows, 0:128] = v0.astype(bf)
            v_ref[rows, 128:256] = ones
            v_ref[rows, 256:384] = v1.astype(bf)
            v_ref[rows, 384:512] = ones

        for i in range(A_Q // LANES):
            t = rope(norm(blk(OFF_AQ + i * LANES), qg)) * Q_SCALE
            qa_ref[rows, i * LANES:(i + 1) * LANES] = t.astype(bf)
        store_kv(kta_ref, va_ref, rope(norm(blk(OFF_AK), kg)), blk(OFF_AV))
        for i in range(B_CH // LANES):
            a = blk(OFF_BU + i * LANES)
            gt = blk(OFF_BU + B_CH + i * LANES)
            hgl_ref[rows, i * LANES:(i + 1) * LANES] = a * jax.nn.sigmoid(gt)
        for i in range(C_Q // LANES):
            t = rope(blk(OFF_CQ + i * LANES)) * Q_SCALE
            qc_ref[rows, i * LANES:(i + 1) * LANES] = t.astype(bf)
        store_kv(ktc_ref, vc_ref, rope(blk(OFF_CK)), blk(OFF_CV))


def _inproj(xs, mods, g1, w_in_bf, gsum, qg, kg, cos_t, sin_t, *, n, tm, n_lat_tiles, s, nb, x_offsets=(0, 0),
            combine=None):
    d = xs[0].shape[1]
    s_tiles = s // tm
    n_tiles = n // tm

    def bidx(i):
        return jnp.where(i < n_lat_tiles, (i * tm) // s, nb)

    def ridx(i):
        return jnp.where(i < n_lat_tiles, i % s_tiles, s_tiles)

    row = lambda w: pl.BlockSpec((tm, w), lambda i: (i, 0))
    ktspec = pl.BlockSpec((KT_ROWS, tm), lambda i: (0, i))
    const = lambda a: pl.BlockSpec(a.shape, lambda i: (0,) * a.ndim)
    bf = jnp.bfloat16
    modspec = pl.BlockSpec((1, N_MOD, d), lambda i: (bidx(i), 0, 0))
    in_specs = _x_specs(xs, tm, n_lat_tiles, x_offsets)
    args = list(xs)
    out_specs = [row(A_Q), ktspec, row(V_W), row(B_CH), row(C_Q), ktspec, row(V_W)]
    out_shape = [jax.ShapeDtypeStruct((n, A_Q), bf), jax.ShapeDtypeStruct((KT_ROWS, n), bf),
                 jax.ShapeDtypeStruct((n, V_W), bf), jax.ShapeDtypeStruct((n, B_CH), jnp.float32),
                 jax.ShapeDtypeStruct((n, C_Q), bf), jax.ShapeDtypeStruct((KT_ROWS, n), bf),
                 jax.ShapeDtypeStruct((n, V_W), bf)]
    aliases = {}
    if combine is not None:
        y, rw, mods_prev = combine
        assert len(xs) == 1 and y.shape[0] == 2 * n
        in_specs += [row(d // 2), pl.BlockSpec((tm, d // 2), lambda i: (i + n_tiles, 0)), row(LANES), modspec]
        args += [y, y, rw, mods_prev]
        out_specs.append(row(d))
        out_shape.append(jax.ShapeDtypeStruct((n, d), jnp.float32))
        aliases = {0: len(out_shape) - 1}
    in_specs += [modspec, const(g1), const(w_in_bf), const(gsum), const(qg), const(kg),
                 pl.BlockSpec((tm, LANES), lambda i: (ridx(i), 0)),
                 pl.BlockSpec((tm, LANES), lambda i: (ridx(i), 0))]
    args += [mods, g1, w_in_bf, gsum, qg, kg, cos_t, sin_t]
    outs = pl.pallas_call(
        functools.partial(_inproj_kernel, n_x=len(xs), n_lat_tiles=n_lat_tiles,
                          fused_combine=combine is not None),
        grid=(n_tiles,),
        in_specs=in_specs,
        out_specs=out_specs,
        out_shape=out_shape,
        input_output_aliases=aliases,
        compiler_params=_cparams("arbitrary"),
        name="inproj",
    )(*args)
    if combine is not None:
        return (outs[-1],) + tuple(outs[:-1])
    return (None,) + tuple(outs)


def _stack_heads(q_ref, tq):
    lane = lax.broadcasted_iota(jnp.int32, (tq, LANES), 1)
    lo = lane < HEAD_DIM
    qb = [q_ref[:, i * LANES:(i + 1) * LANES] for i in range(3)]
    zero = jnp.zeros_like(qb[0])
    keep_lo = lambda t: jnp.where(lo, t, zero)
    keep_hi = lambda t: jnp.where(lo, zero, t)
    s0 = jnp.concatenate([keep_lo(qb[0]), keep_hi(qb[0]), keep_lo(qb[1])], axis=0)
    s1 = jnp.concatenate([keep_hi(qb[1]), keep_lo(qb[2]), keep_hi(qb[2])], axis=0)
    return s0, s1, lo


def _unstack_store(o_ref, o0, o1, lo, tq):
    bf = jnp.bfloat16
    o_ref[:, 0:128] = jnp.where(lo, o0[0:tq], o0[tq:2 * tq]).astype(bf)
    o_ref[:, 128:256] = jnp.where(lo, o0[2 * tq:3 * tq], o1[0:tq]).astype(bf)
    o_ref[:, 256:384] = jnp.where(lo, o1[tq:2 * tq], o1[2 * tq:3 * tq]).astype(bf)


def _krow(kv):
    return slice(kv * LANES, (kv + 1) * LANES)


def _vcol(kv):
    return slice(2 * kv * LANES, (2 * kv + 2) * LANES)


def _row_max(scores):
    m = None
    for t in scores:
        for c in range(0, t.shape[1], LANES):
            blk = t[:, c:c + LANES]
            m = blk if m is None else jnp.maximum(m, blk)
    return m.max(axis=-1, keepdims=True)


def _softmax_pv(scores, values, extra=None):
    m = _row_max(scores)
    if extra is not None:
        m = jnp.maximum(m, extra)
    acc = None
    for t, v in zip(scores, values):
        c = jnp.dot(jnp.exp2(t - m).astype(jnp.bfloat16), v, preferred_element_type=jnp.float32)
        acc = c if acc is None else acc + c
    den = acc[:, LANES:2 * LANES]
    if extra is not None:
        den = den + jnp.exp2(extra - m)
    return acc[:, 0:LANES] / den


ATTN_CHAIN_ROWS = 128


def _qk(q, kt):
    return jnp.dot(q, kt, preferred_element_type=jnp.float32)


def _attn_a_kernel(q_ref, ktl_ref, ktc_ref, vl_ref, vc_ref, o_ref, *, tq):
    s0, s1, lo = _stack_heads(q_ref, tq)
    outs = []
    for kv, qs in enumerate((s0, s1)):
        krow = _krow(kv)
        vcol = _vcol(kv)
        parts = []
        for r0 in range(0, 3 * tq, ATTN_CHAIN_ROWS):
            qr = qs[r0:r0 + ATTN_CHAIN_ROWS]
            ss = [_qk(qr, ktl_ref[krow, :]), _qk(qr, ktc_ref[krow, :])]
            parts.append(_softmax_pv(ss, [vl_ref[:, vcol], vc_ref[:, vcol]]))
        outs.append(jnp.concatenate(parts, axis=0))
    _unstack_store(o_ref, outs[0], outs[1], lo, tq)


def _kv_specs(nb, s, l):
    ctx0 = nb * s // l
    return [pl.BlockSpec((KT_ROWS, s), lambda b, j: (0, b)),
            pl.BlockSpec((KT_ROWS, l), lambda b, j: (0, ctx0 + b)),
            pl.BlockSpec((s, V_W), lambda b, j: (b, 0)),
            pl.BlockSpec((l, V_W), lambda b, j: (ctx0 + b, 0))]


def _attn_a(qa, kt, v, *, nb, s, l, tq):
    n = qa.shape[0]
    n_q = s // tq
    return pl.pallas_call(
        functools.partial(_attn_a_kernel, tq=tq),
        grid=(nb, n_q),
        in_specs=[pl.BlockSpec((tq, A_Q), lambda b, j: (b * n_q + j, 0))] + _kv_specs(nb, s, l),
        out_specs=pl.BlockSpec((tq, A_Q), lambda b, j: (b * n_q + j, 0)),
        out_shape=jax.ShapeDtypeStruct((n, A_Q), jnp.bfloat16),
        compiler_params=_cparams("arbitrary", "arbitrary"),
        name="attn_global",
    )(qa, kt, kt, v, v)


WIN_BLOCKS = 8


def _sink_column(sink_ref, kv, rows):
    return jnp.concatenate([jnp.full((rows, 1), sink_ref[3 * kv + g], jnp.float32) for g in range(3)], axis=0)


def _attn_c_kernel(sink_ref, bias_ref, q_ref, ktl_ref, ktc_ref, vl_ref, vc_ref, o_ref, *, s, blocks):
    j = pl.program_id(1)
    tq = WINDOW
    band = 3 * WINDOW
    bf = jnp.bfloat16
    f32 = jnp.float32
    m3 = 3 * tq
    stacks = [_stack_heads(q_ref.at[blk * tq:(blk + 1) * tq, :], tq) for blk in range(blocks)]
    lo = stacks[0][2]
    starts = [pl.multiple_of(jnp.clip((j * blocks + blk - 1) * WINDOW, 0, s - band), WINDOW)
              for blk in range(blocks)]
    outs = [[None, None] for _ in range(blocks)]
    for kv in range(C_KV_HEADS):
        krow = _krow(kv)
        vcol = _vcol(kv)
        sk = _sink_column(sink_ref, kv, tq)
        sc_all = _qk(jnp.concatenate([st[kv] for st in stacks], axis=0), ktc_ref[krow, :])
        accs, pcs, ms = [], [], []
        for blk in range(blocks):
            start = starts[blk]
            sl = _qk(stacks[blk][kv], ktl_ref[krow, pl.ds(start, band)]) + bias_ref[j * blocks + blk - start // WINDOW]
            sc = sc_all[blk * m3:(blk + 1) * m3]
            m = jnp.maximum(_row_max([sl, sc]), sk)
            accs.append(jnp.dot(jnp.exp2(sl - m).astype(bf), vl_ref[pl.ds(start, band), vcol],
                                preferred_element_type=f32))
            pcs.append(jnp.exp2(sc - m).astype(bf))
            ms.append(m)
        acc_ctx = jnp.dot(jnp.concatenate(pcs, axis=0), vc_ref[:, vcol], preferred_element_type=f32)
        for blk in range(blocks):
            acc = accs[blk] + acc_ctx[blk * m3:(blk + 1) * m3]
            den = acc[:, LANES:2 * LANES] + jnp.exp2(sk - ms[blk])
            outs[blk][kv] = acc[:, 0:LANES] / den
    for blk in range(blocks):
        _unstack_store(o_ref.at[blk * tq:(blk + 1) * tq, :], outs[blk][0], outs[blk][1], lo, tq)


def _window_bias():
    r = np.arange(3 * WINDOW)[:, None] % WINDOW
    col = np.arange(3 * WINDOW)[None, :]
    tabs = [np.where(np.abs(col - r - WINDOW * off) <= WINDOW, 0.0, -np.inf) for off in range(3)]
    return jnp.asarray(np.stack(tabs), jnp.float32)


def _attn_c(qc, kt, v, sink2, *, nb, s, l):
    n = qc.shape[0]
    blocks = _pick(s // WINDOW, (WIN_BLOCKS, 4, 2, 1))
    tq = blocks * WINDOW
    n_q = s // tq
    bias = _window_bias()
    return pl.pallas_call(
        functools.partial(_attn_c_kernel, s=s, blocks=blocks),
        grid=(nb, n_q),
        in_specs=[pl.BlockSpec(memory_space=pltpu.SMEM),
                  pl.BlockSpec(bias.shape, lambda b, j: (0, 0, 0)),
                  pl.BlockSpec((tq, C_Q), lambda b, j: (b * n_q + j, 0))] + _kv_specs(nb, s, l),
        out_specs=pl.BlockSpec((tq, C_Q), lambda b, j: (b * n_q + j, 0)),
        out_shape=jax.ShapeDtypeStruct((n, C_Q), jnp.bfloat16),
        compiler_params=_cparams("arbitrary", "arbitrary"),
        name="attn_window",
    )(sink2, bias, qc, kt, kt, v, v)


def _attn_ctx_kernel(sink_ref, qa_ref, kta_ref, va_ref, qc_ref, ktc_ref, vc_ref, oa_in, oc_in, oa_ref, oc_ref, *, l):
    del oa_in, oc_in
    for q_ref, kt_ref, v_ref, o_ref, with_sink in ((qa_ref, kta_ref, va_ref, oa_ref, False),
                                                   (qc_ref, ktc_ref, vc_ref, oc_ref, True)):
        s0, s1, lo = _stack_heads(q_ref, l)
        outs = []
        for kv, qs in enumerate((s0, s1)):
            sk = _sink_column(sink_ref, kv, l) if with_sink else None
            outs.append(_softmax_pv([_qk(qs, kt_ref[_krow(kv), :])], [v_ref[:, _vcol(kv)]], sk))
        _unstack_store(o_ref, outs[0], outs[1], lo, l)


def _attn_ctx(qa, kta, va, qc, ktc, vc, sink2, oa, oc, *, nb, s, l):
    base = nb * s // l
    row = lambda w: pl.BlockSpec((l, w), lambda b: (base + b, 0))
    ktspec = pl.BlockSpec((KT_ROWS, l), lambda b: (0, base + b))
    anyspec = pl.BlockSpec(memory_space=pl.ANY)
    return pl.pallas_call(
        functools.partial(_attn_ctx_kernel, l=l),
        grid=(nb,),
        in_specs=[pl.BlockSpec(memory_space=pltpu.SMEM), row(A_Q), ktspec, row(V_W), row(C_Q), ktspec, row(V_W),
                  anyspec, anyspec],
        out_specs=[row(A_Q), row(C_Q)],
        out_shape=[jax.ShapeDtypeStruct(oa.shape, oa.dtype), jax.ShapeDtypeStruct(oc.shape, oc.dtype)],
        input_output_aliases={7: 0, 8: 1},
        compiler_params=_cparams("arbitrary"),
        name="attn_context",
    )(sink2, qa, kta, va, qc, ktc, vc, oa, oc)


CONV_HALO = 16
CONV_ROWS = 64


def _conv_kernel(prev_ref, cur_ref, next_ref, w_ref, b_ref, g_ref, beta_ref, *rest, chunks, tc):
    o_ref, sh_ref = rest[-2:]
    j = pl.program_id(1)
    has_prev = j > 0
    has_next = j < chunks - 1
    rows = tc + 2 * CONV_HALO
    sh_ref[0, 0:CONV_HALO, :] = jnp.where(has_prev, prev_ref[...], 0.0)
    sh_ref[0, CONV_HALO:CONV_HALO + tc, :] = cur_ref[...]
    sh_ref[0, CONV_HALO + tc:rows, :] = jnp.where(has_next, next_ref[...], 0.0)
    for b in range(1, SUBLANES):
        sh_ref[b, 0:rows - SUBLANES, :] = sh_ref[0, b:b + rows - SUBLANES, :]
    base = CONV_HALO - CONV_W // 2
    for r0 in range(0, tc, CONV_ROWS):
        acc = None
        for k in range(CONV_W):
            a, b = divmod(base + k, SUBLANES)
            term = sh_ref[b, SUBLANES * a + r0:SUBLANES * a + r0 + CONV_ROWS, :] * w_ref[k:k + 1, :]
            acc = term if acc is None else acc + term
        hc = acc + b_ref[...]
        mu = jnp.mean(hc, axis=-1, keepdims=True)
        xc = hc - mu
        var = jnp.mean(xc * xc, axis=-1, keepdims=True)
        y = xc * lax.rsqrt(var + EPS) * g_ref[...] + beta_ref[...]
        o_ref[r0:r0 + CONV_ROWS, :] = (y * jax.nn.sigmoid(y)).astype(o_ref.dtype)


def _conv(hgl, w, b, g, beta, *, nb, seq, base_rows, tc, prev_out=None):
    n = hgl.shape[0]
    chunks = seq // tc
    base = base_rows // tc

    def idx(b_, j, delta):
        return base + b_ * chunks + jnp.clip(j + delta, 0, chunks - 1)

    blk = lambda delta: pl.BlockSpec((tc, B_CH), lambda b_, j: (idx(b_, j, delta), 0))
    const = lambda a: pl.BlockSpec(a.shape, lambda b_, j: (0,) * a.ndim)
    hb = tc // CONV_HALO
    last_halo = n // CONV_HALO - 1
    prev_halo = pl.BlockSpec((CONV_HALO, B_CH), lambda b_, j: (jnp.maximum(idx(b_, j, 0) * hb - 1, 0), 0))
    next_halo = pl.BlockSpec((CONV_HALO, B_CH),
                             lambda b_, j: (jnp.minimum((idx(b_, j, 0) + 1) * hb, last_halo), 0))
    in_specs = [prev_halo, blk(0), next_halo, const(w), const(b), const(g), const(beta)]
    args = [hgl, hgl, hgl, w, b, g, beta]
    aliases = {}
    if prev_out is not None:
        in_specs.append(pl.BlockSpec(memory_space=pl.ANY))
        args.append(prev_out)
        aliases = {len(args) - 1: 0}
    return pl.pallas_call(
        functools.partial(_conv_kernel, chunks=chunks, tc=tc),
        grid=(nb, chunks),
        in_specs=in_specs,
        out_specs=blk(0),
        out_shape=jax.ShapeDtypeStruct((n, B_CH), jnp.bfloat16),
        scratch_shapes=[pltpu.VMEM((SUBLANES, tc + 2 * CONV_HALO, B_CH), jnp.float32)],
        input_output_aliases=aliases,
        compiler_params=_cparams("arbitrary", "arbitrary"),
        name="conformer_conv",
    )(*args)


META_ROWS = 8


def _outproj_kernel(*refs, n_x, n_lat_tiles, n_tiles):
    x_refs = refs[:n_x]
    (oa_ref, ob_ref, oc_ref, mod_ref, g_ref, w_ref, wr_ref, br_ref, tri_ref,
     xo_ref, hp_ref, meta_ref, rw_ref, cnt_ref, lg_ref) = refs[n_x:]
    i = pl.program_id(0)
    tm = xo_ref.shape[0]
    f32 = jnp.float32

    @pl.when(i == 0)
    def _():
        cnt_ref[...] = jnp.zeros_like(cnt_ref)
        lg_ref[...] = jnp.zeros_like(lg_ref)

    lat = jnp.concatenate([oa_ref[...], ob_ref[...], oc_ref[...]], axis=1)
    mix = jnp.dot(lat, w_ref[...], preferred_element_type=f32)
    x = _load_x(x_refs, n_lat_tiles, tile=jnp.minimum(i, n_tiles - 1)) + mod_ref[0, 2:3, :] * mix
    xo_ref[...] = x
    r = lax.rsqrt(jnp.mean(x * x, axis=-1, keepdims=True) + EPS)
    h = (x * r) * (g_ref[...] * (1.0 + mod_ref[0, 4:5, :])) + mod_ref[0, 3:4, :]
    h_hi = h.astype(jnp.bfloat16)
    hp_ref[...] = _pack_pairs(h_hi)
    h_lo = (h - h_hi.astype(f32)).astype(jnp.bfloat16)
    r_hi = jnp.dot(h_hi, wr_ref[...], preferred_element_type=f32)
    r_lo = jnp.dot(h_lo, wr_ref[:, 0:LANES], preferred_element_type=f32)
    lg = lg_ref[(i + 1) % 2]
    lg_ref[i % 2] = r_hi[:, 0:LANES] + r_hi[:, LANES:2 * LANES] + r_lo + br_ref[...]

    lane = lax.broadcasted_iota(jnp.int32, (tm, LANES), 1).astype(f32)
    big = float(LANES)
    ninf = -jnp.inf
    glog = jnp.where(lane < N_GROUPS, lg, ninf)
    gmax = glog.max(axis=-1, keepdims=True)
    g_val = 1.0 / jnp.exp(glog - gmax).sum(axis=-1, keepdims=True)
    g_idx = jnp.where(glog == gmax, lane, big).min(axis=-1, keepdims=True)
    e_lo = N_GROUPS + EXPERTS_PER_GROUP * g_idx
    el = jnp.where((lane >= e_lo) & (lane < e_lo + EXPERTS_PER_GROUP), lg, ninf)
    v0 = el.max(axis=-1, keepdims=True)
    i0 = jnp.where(el == v0, lane, big).min(axis=-1, keepdims=True)
    el1 = jnp.where(lane == i0, ninf, el)
    v1 = el1.max(axis=-1, keepdims=True)
    i1 = jnp.where(el1 == v1, lane, big).min(axis=-1, keepdims=True)
    t = jnp.exp(v1 - v0)
    w0 = g_val / (1.0 + t)
    w1 = g_val * t / (1.0 + t)
    e0 = i0 - N_GROUPS
    e1 = i1 - N_GROUPS

    cnt = cnt_ref[0:1, :]
    tri = tri_ref[...]
    ranks = []
    for e in (e0, e1):
        oh = lane == e
        ohf = oh.astype(f32)
        pre = jnp.dot(tri, ohf.astype(jnp.bfloat16), preferred_element_type=f32) + cnt
        ranks.append(jnp.where(oh, pre, 0.0).sum(axis=-1, keepdims=True))
        cnt = cnt + ohf.sum(axis=0, keepdims=True)
    cnt_ref[0:1, :] = jnp.where(i > 0, cnt, cnt_ref[0:1, :])
    rw_ref[...] = jnp.where(lane == 0, w0, jnp.where(lane == 1, w1, 0.0))
    rec = jnp.where(lane == 0, e0, jnp.where(lane == 1, e1, jnp.where(lane == 2, ranks[0],
                    jnp.where(lane == 3, ranks[1], jnp.where(lane == 4, w0, jnp.where(lane == 5, w1, 0.0))))))
    meta_ref[...] = rec.T[0:META_ROWS, :]


def _outproj(xs, oa, ob, oc, mods, g2, w_out_bf, wr, br, tri, *, n, tm, n_tiles, n_lat_tiles, s, nb,
             x_offsets=(0, 0)):
    d = xs[0].shape[1]
    rows = n_tiles * tm
    cur = lambda i: jnp.minimum(i, n_tiles - 1)
    prev = lambda i: jnp.maximum(i - 1, 0)

    def bidx(i):
        return jnp.where(cur(i) < n_lat_tiles, (cur(i) * tm) // s, nb)

    row = lambda w: pl.BlockSpec((tm, w), lambda i: (cur(i), 0))
    const = lambda a: pl.BlockSpec(a.shape, lambda i: (0,) * a.ndim)
    return pl.pallas_call(
        functools.partial(_outproj_kernel, n_x=len(xs), n_lat_tiles=n_lat_tiles, n_tiles=n_tiles),
        grid=(n_tiles + 1,),
        in_specs=_x_specs(xs, tm, n_lat_tiles, x_offsets, cur) + [
                  row(A_Q), row(B_CH), row(C_Q),
                  pl.BlockSpec((1, N_MOD, d), lambda i: (bidx(i), 0, 0)),
                  const(g2), const(w_out_bf), const(wr), const(br), const(tri)],
        out_specs=[row(d), row(d // 2),
                   pl.BlockSpec((META_ROWS, tm), lambda i: (0, prev(i))),
                   pl.BlockSpec((tm, LANES), lambda i: (prev(i), 0)),
                   pl.BlockSpec((8, LANES), lambda i: (0, 0))],
        out_shape=[jax.ShapeDtypeStruct((n, d), jnp.float32),
                   jax.ShapeDtypeStruct((rows, d // 2), jnp.int32),
                   jax.ShapeDtypeStruct((META_ROWS, rows), jnp.float32),
                   jax.ShapeDtypeStruct((rows, LANES), jnp.float32),
                   jax.ShapeDtypeStruct((8, LANES), jnp.float32)],
        scratch_shapes=[pltpu.VMEM((2, tm, LANES), jnp.float32)],
        input_output_aliases={0: 0} if len(xs) == 1 else {},
        compiler_params=_cparams("arbitrary"),
        name="outproj_router",
    )(*xs, oa, ob, oc, mods, g2, w_out_bf, wr, br, tri)


def _sc_mesh():
    return plsc.VectorSubcoreMesh(core_axis_name="core", subcore_axis_name="subcore")


def sc_gather_rows(table, idx2):
    r = idx2.shape[1]
    w = table.shape[1]
    assert r % (2 * SC_WINDOW) == 0
    half = r // SC_WINDOW // 2

    @functools.partial(pl.kernel, out_type=jax.ShapeDtypeStruct((r, w), table.dtype), mesh=_sc_mesh())
    def k(x_hbm, i_hbm, o_hbm):
        def body(i_vmem, o_vmem):
            pltpu.sync_copy(x_hbm.at[i_vmem.at[0]], o_vmem)

        pltpu.emit_pipeline(
            body,
            grid=(2, half),
            in_specs=[pl.BlockSpec((1, SC_WINDOW), lambda c, i: (0, c * half + i))],
            out_specs=[pl.BlockSpec((SC_WINDOW, w), lambda c, i: (c * half + i, 0),
                                    pipeline_mode=pl.Buffered(1))],
            core_axis_name=("core", "subcore"),
            dimension_semantics=(pltpu.PARALLEL, pltpu.PARALLEL),
        )(i_hbm, o_hbm)

    return k(table, idx2)


def sc_scatter_rows2(rows, idx2, n_out):
    r, w = rows.shape
    assert idx2.shape == (1, 2 * r) and r % (2 * SC_WINDOW) == 0
    windows = r // SC_WINDOW
    half = windows // 2

    @functools.partial(pl.kernel, out_type=jax.ShapeDtypeStruct((n_out, w), rows.dtype), mesh=_sc_mesh(),
                       scratch_types=[])
    def k(x_hbm, ia_hbm, ib_hbm, o_hbm):
        def body(x_vmem, ia_vmem, ib_vmem):
            pltpu.sync_copy(x_vmem, o_hbm.at[ia_vmem.at[0]])
            pltpu.sync_copy(x_vmem, o_hbm.at[ib_vmem.at[0]])

        pltpu.emit_pipeline(
            body,
            grid=(2, half),
            in_specs=[pl.BlockSpec((SC_WINDOW, w), lambda c, i: (c * half + i, 0),
                                   pipeline_mode=pl.Buffered(1)),
                      pl.BlockSpec((1, SC_WINDOW), lambda c, i: (0, c * half + i)),
                      pl.BlockSpec((1, SC_WINDOW), lambda c, i: (0, windows + c * half + i))],
            out_specs=[],
            core_axis_name=("core", "subcore"),
            dimension_semantics=(pltpu.PARALLEL, pltpu.PARALLEL),
        )(x_hbm, ia_hbm, ib_hbm)

    return k(rows, idx2, idx2)


def _expert_kernel(te_ref, nv_ref, x_ref, wg_ref, wu_ref, wd_ref, o_ref, wgb_ref, wub_ref, wdb_ref):
    t = pl.program_id(0)
    nvalid = nv_ref[t]

    @pl.when((t == 0) | (te_ref[t] != te_ref[jnp.maximum(t - 1, 0)]))
    def _():
        wgb_ref[...] = wg_ref[0].astype(jnp.bfloat16)
        wub_ref[...] = wu_ref[0].astype(jnp.bfloat16)
        wdb_ref[...] = wd_ref[0].astype(jnp.bfloat16)

    @pl.when(nvalid > 0)
    def _():
        rows = lax.broadcasted_iota(jnp.int32, x_ref.shape, 0)
        lo, hi = _unpack_pairs(jnp.where(rows < nvalid, x_ref[...], 0))
        xb = jnp.concatenate([lo, hi], axis=1).astype(jnp.bfloat16)
        g = jnp.dot(xb, wgb_ref[...], preferred_element_type=jnp.float32)
        u = jnp.dot(xb, wub_ref[...], preferred_element_type=jnp.float32)
        a = (g * jax.nn.sigmoid(g) * u).astype(jnp.bfloat16)
        o_ref[...] = _pack_pairs(jnp.dot(a, wdb_ref[...], preferred_element_type=jnp.float32))

    @pl.when(nvalid == 0)
    def _():
        o_ref[...] = jnp.zeros_like(o_ref)


def _experts(buf, tile_expert, tile_nvalid, wg, wu, wd, *, layer):
    rows, wp = buf.shape
    _, _, d, f = wg.shape
    n_tiles = rows // EXPERT_TILE
    grid_spec = pltpu.PrefetchScalarGridSpec(
        num_scalar_prefetch=2,
        grid=(n_tiles,),
        in_specs=[pl.BlockSpec((EXPERT_TILE, wp), lambda t, te, nv: (t, 0)),
                  pl.BlockSpec((None, 1, d, f), lambda t, te, nv: (layer, te[t], 0, 0)),
                  pl.BlockSpec((None, 1, d, f), lambda t, te, nv: (layer, te[t], 0, 0)),
                  pl.BlockSpec((None, 1, f, d), lambda t, te, nv: (layer, te[t], 0, 0))],
        out_specs=pl.BlockSpec((EXPERT_TILE, wp), lambda t, te, nv: (t, 0)),
        scratch_shapes=[pltpu.VMEM((d, f), jnp.bfloat16), pltpu.VMEM((d, f), jnp.bfloat16),
                        pltpu.VMEM((f, d), jnp.bfloat16)],
    )
    return pl.pallas_call(
        _expert_kernel,
        grid_spec=grid_spec,
        out_shape=jax.ShapeDtypeStruct((rows, wp), jnp.int32),
        compiler_params=_cparams("arbitrary"),
        name="expert_ffn",
    )(tile_expert, tile_nvalid, buf, wg, wu, wd)


def _final_kernel(y0_ref, y1_ref, rw_ref, x_ref, mod_ref, fg_ref, o_ref):
    x = x_ref[...] + mod_ref[0, 5:6, :] * _moe_mix(y0_ref, y1_ref, rw_ref, slice(None))
    r = lax.rsqrt(jnp.mean(x * x, axis=-1, keepdims=True) + EPS)
    o_ref[...] = x * r * fg_ref[...]


def _final_kernel_into(y0_ref, y1_ref, rw_ref, x_ref, mod_ref, fg_ref, prev_ref, o_ref):
    del prev_ref
    _final_kernel(y0_ref, y1_ref, rw_ref, x_ref, mod_ref, fg_ref, o_ref)


def _final_combine(y, rw, xall, mods, final_g, *, tm, n_tiles, s, out_rows, out_tile0, prev_out=None):
    d = xall.shape[1]
    row = lambda w: pl.BlockSpec((tm, w), lambda i: (i, 0))
    in_specs = [row(d // 2),
                pl.BlockSpec((tm, d // 2), lambda i: (i + n_tiles, 0)),
                row(LANES), row(d),
                pl.BlockSpec((1, N_MOD, d), lambda i: ((i * tm) // s, 0, 0)),
                pl.BlockSpec(final_g.shape, lambda i: (0, 0))]
    args = [y, y, rw, xall, mods, final_g]
    aliases = {}
    body = _final_kernel
    if prev_out is not None:
        in_specs.append(pl.BlockSpec(memory_space=pl.ANY))
        args.append(prev_out)
        aliases = {len(args) - 1: 0}
        body = _final_kernel_into
    return pl.pallas_call(
        body,
        grid=(n_tiles,),
        in_specs=in_specs,
        out_specs=pl.BlockSpec((tm, d), lambda i: (out_tile0 + i, 0)),
        out_shape=jax.ShapeDtypeStruct((out_rows, d), jnp.float32),
        input_output_aliases=aliases,
        compiler_params=_cparams("arbitrary"),
        name="moe_combine_final",
    )(*args)


def _dest_kernel(ps_ref, meta_ref, o_ref):
    slot = pl.program_id(0)
    e = meta_ref[pl.ds(slot, 1), :]
    d = meta_ref[pl.ds(TOP_K + slot, 1), :]
    for k in range(N_EXPERTS):
        d = d + jnp.where(e == float(k), ps_ref[k], 0.0)
    o_ref[...] = d.astype(jnp.int32)


def _dest_rows(meta, pstarts, tcols):
    n = meta.shape[1]
    nt = n // tcols
    return pl.pallas_call(
        _dest_kernel,
        grid=(TOP_K, nt),
        in_specs=[pl.BlockSpec(memory_space=pltpu.SMEM),
                  pl.BlockSpec((META_ROWS, tcols), lambda k, i: (0, i))],
        out_specs=pl.BlockSpec((1, tcols), lambda k, i: (0, k * nt + i)),
        out_shape=jax.ShapeDtypeStruct((1, TOP_K * n), jnp.int32),
        compiler_params=_cparams("arbitrary", "arbitrary"),
        name="moe_dest",
    )(pstarts.astype(jnp.float32), meta)


def _dispatch_plan(meta, counts, n_rows_buf, tcols):
    cnt = counts[0, :N_EXPERTS].astype(jnp.int32)
    padded = (cnt + EXPERT_TILE - 1) // EXPERT_TILE * EXPERT_TILE
    pends = jnp.cumsum(padded)
    pstarts = pends - padded
    dest = _dest_rows(meta, pstarts, tcols)
    tile_start = jnp.arange(n_rows_buf // EXPERT_TILE, dtype=jnp.int32) * EXPERT_TILE
    te = jnp.sum((tile_start[:, None] >= pends[None, :]).astype(jnp.int32), axis=1)
    te = jnp.minimum(te, N_EXPERTS - 1)
    onehot = te[:, None] == jnp.arange(N_EXPERTS, dtype=jnp.int32)[None, :]
    cnt_te = jnp.sum(jnp.where(onehot, cnt[None, :], 0), axis=1)
    pstart_te = jnp.sum(jnp.where(onehot, pstarts[None, :], 0), axis=1)
    nvalid = jnp.clip(cnt_te - (tile_start - pstart_te), 0, EXPERT_TILE).astype(jnp.int32)
    return dest, te, nvalid


def _rope_tables(s, tm):
    pos = np.arange(s)
    pos_row = jnp.asarray(pos // GRID_W, jnp.float32)
    pos_col = jnp.asarray(pos % GRID_W, jnp.float32)
    n_freq = HEAD_DIM // 4
    inv = ROPE_THETA ** (-jnp.arange(n_freq, dtype=jnp.float32) / n_freq)
    ang_row = pos_row[:, None] * inv
    ang_col = pos_col[:, None] * inv
    ang = jnp.concatenate([ang_row, ang_row, ang_col, ang_col] * (LANES // HEAD_DIM), axis=-1)
    sign = np.where((np.arange(LANES) % 32) < 16, -1.0, 1.0).astype(np.float32)
    cos_t = jnp.concatenate([jnp.cos(ang), jnp.ones((tm, LANES), jnp.float32)], axis=0)
    sin_t = jnp.concatenate([jnp.sin(ang) * sign, jnp.zeros((tm, LANES), jnp.float32)], axis=0)
    return cos_t, sin_t


def kernel(x, c, ctx, c_ctx, norm1_g, norm2_g, w_mod, b_mod, w_in, q_norm_g, k_norm_g, conv_w, conv_b, conv_ln_g, conv_ln_b, sink, w_out, w_group, b_group, w_expert, b_expert, w_gate, w_up, w_down, final_g):
    nb, s, d = x.shape
    l = ctx.shape[1]
    depth = w_in.shape[0]
    assert w_in.shape[2] == D_IN and w_out.shape[1] == D_MIX
    assert s % GRID_W == 0 and s >= 3 * WINDOW and s % WINDOW == 0 and l % WINDOW == 0
    groups = BATCH_GROUPS if nb % BATCH_GROUPS == 0 else 1
    nbg = nb // groups
    n_lat, n_ctx = nbg * s, nbg * l
    tm = _pick(np.gcd(s, n_ctx), (512, 256, 128))
    tq = _pick(s, (512, 256, 128))
    tc = _pick(np.gcd(s, l), (256, 128))
    assert n_lat % l == 0
    bf = jnp.bfloat16
    f32 = jnp.float32

    x2d = x.reshape(nb * s, d)
    ctx2d = ctx.reshape(nb * l, d)
    c_all = jnp.concatenate([c, c_ctx[None, :]], axis=0)
    mods_all = _modulation(c_all, w_mod, b_mod).reshape(depth, nb + 1, N_MOD, d)
    cos_t, sin_t = _rope_tables(s, tm)
    head_id = np.arange(LANES) // HEAD_DIM
    gsum = jnp.asarray((head_id[:, None] == head_id[None, :]) / HEAD_DIM, bf)
    tri = jnp.asarray(np.tril(np.ones((tm, tm), np.float32), -1), bf)
    n_lat_tiles = n_lat // tm

    xs = [(x2d, ctx2d)] * groups
    pending = [None] * groups
    for i in range(depth):
        last = i == depth - 1
        with_ctx = not last
        qg = jnp.tile(q_norm_g[i], LANES // HEAD_DIM)[None, :]
        kg = jnp.tile(k_norm_g[i], LANES // HEAD_DIM)[None, :]
        w_in_bf = w_in[i].astype(bf)
        w_out_bf = w_out[i].astype(bf)
        sink2 = sink[i] * LOG2E
        conv_args = (conv_w[i].reshape(CONV_W, B_CH), conv_b[i][None, :], conv_ln_g[i][None, :],
                     conv_ln_b[i][None, :])
        wr32 = jnp.zeros((d, LANES), f32).at[:, :N_GROUPS].set(w_group[i])
        wr32 = wr32.at[:, N_GROUPS:N_GROUPS + N_EXPERTS].set(w_expert[i])
        wr_hi = wr32.astype(bf)
        wr = jnp.concatenate([wr_hi, (wr32 - wr_hi.astype(f32)).astype(bf)], axis=1)
        br = jnp.zeros((1, LANES), f32).at[0, :N_GROUPS].set(b_group[i])
        br = br.at[0, N_GROUPS:N_GROUPS + N_EXPERTS].set(b_expert[i])
        n_tok = n_lat + n_ctx if with_ctx else n_lat
        n_tiles = n_tok // tm
        n_rows_buf = 2 * n_tok + N_EXPERTS * EXPERT_TILE
        for g in range(groups):
            b0 = g * nbg
            mods = jnp.concatenate([mods_all[i, b0:b0 + nbg], mods_all[i, nb:nb + 1]], axis=0)
            x_offsets = (b0 * s // tm, b0 * l // tm)
            x_new, qa, kta, va, hgl, qc, ktc, vc = _inproj(
                xs[g], mods, norm1_g[i][None, :], w_in_bf, gsum, qg, kg, cos_t, sin_t, n=n_lat + n_ctx,
                tm=tm, n_lat_tiles=n_lat_tiles, s=s, nb=nbg, x_offsets=x_offsets, combine=pending[g])
            if pending[g] is not None:
                xs[g] = (x_new,)
            oa = _attn_a(qa, kta, va, nb=nbg, s=s, l=l, tq=tq)
            oc = _attn_c(qc, ktc, vc, sink2, nb=nbg, s=s, l=l)
            ob = _conv(hgl, *conv_args, nb=nbg, seq=s, base_rows=0, tc=tc)
            if with_ctx:
                oa, oc = _attn_ctx(qa, kta, va, qc, ktc, vc, sink2, oa, oc, nb=nbg, s=s, l=l)
                ob = _conv(hgl, *conv_args, nb=nbg, seq=l, base_rows=n_lat, tc=tc, prev_out=ob)
            xall, hp, meta, rw, counts = _outproj(xs[g], oa, ob, oc, mods, norm2_g[i][None, :], w_out_bf,
                                                  wr, br, tri, n=n_lat + n_ctx, tm=tm, n_tiles=n_tiles,
                                                  n_lat_tiles=n_lat_tiles, s=s, nb=nbg, x_offsets=x_offsets)
            dest, te, nvalid = _dispatch_plan(meta, counts, n_rows_buf,
                                              _pick(n_tok, (8192, 4096, 2048, 1024, 512, 256, 128)))
            buf = sc_scatter_rows2(hp, dest, n_rows_buf)
            eo = _experts(buf, te, nvalid, w_gate, w_up, w_down, layer=i)
            y = sc_gather_rows(eo, dest)
            xs[g] = (xall,)
            pending[g] = (y, rw, mods)
    out = None
    for g in range(groups):
        y, rw, mods = pending[g]
        out = _final_combine(y, rw, xs[g][0], mods, final_g[None, :], tm=tm, n_tiles=n_lat_tiles, s=s,
                             out_rows=nb * s, out_tile0=g * n_lat_tiles, prev_out=out)
    return out.reshape(nb, s, d)
```

```python
import functools

import jax
import jax.numpy as jnp
import numpy as np
from jax import lax
from jax.experimental import pallas as pl
from jax.experimental.pallas import tpu as pltpu
from jax.experimental.pallas import tpu_sc as plsc

HEAD_DIM = 64
GRID_W = 64
ROPE_THETA = 10000.0
A_HEADS, A_KV_HEADS = 6, 2
C_HEADS, C_KV_HEADS = 6, 2
B_CH = 256
CONV_W = 31
WINDOW = 128
N_GROUPS = 4
EXPERTS_PER_GROUP = 8
N_EXPERTS = N_GROUPS * EXPERTS_PER_GROUP
TOP_K = 2
N_MOD = 6
EPS = 1e-6
ATTN_SCALE = HEAD_DIM ** -0.5
LOG2E = 1.4426950408889634
Q_SCALE = ATTN_SCALE * LOG2E

A_Q = A_HEADS * HEAD_DIM
A_KV = A_KV_HEADS * HEAD_DIM
C_Q = C_HEADS * HEAD_DIM
C_KV = C_KV_HEADS * HEAD_DIM
D_MIX = A_Q + B_CH + C_Q
OFF_AQ = 0
OFF_AK = OFF_AQ + A_Q
OFF_AV = OFF_AK + A_KV
OFF_BU = OFF_AV + A_KV
OFF_CQ = OFF_BU + 2 * B_CH
OFF_CK = OFF_CQ + C_Q
OFF_CV = OFF_CK + C_KV
D_IN = OFF_CV + C_KV

LANES = 128
SUBLANES = 8
KT_ROWS = LANES
V_W = 2 * LANES
HEAD_PERM = np.concatenate([np.arange(HEAD_DIM) + HEAD_DIM * h for b in range(3) for h in (b, b + 3)])
EXPERT_TILE = 512
SC_WINDOW = 128
VMEM_LIMIT = 56 * 1024 * 1024
HI_MASK = -65536
BATCH_GROUPS = 2
ROW_CHAIN = 256


def _cparams(*sem):
    return pltpu.CompilerParams(dimension_semantics=sem, vmem_limit_bytes=VMEM_LIMIT)


def _pick(n, cands):
    for c in cands:
        if n % c == 0:
            return c
    raise ValueError(f"no tile in {cands} divides {n}")


def _pack_pairs(x):
    w = x.shape[1] // 2
    lo = lax.bitcast_convert_type(x[:, :w].astype(jnp.bfloat16).astype(jnp.float32), jnp.int32)
    hi = lax.bitcast_convert_type(x[:, w:].astype(jnp.bfloat16).astype(jnp.float32), jnp.int32)
    return (hi & HI_MASK) | lax.shift_right_logical(lo, 16)


def _unpack_pairs(p):
    lo = lax.bitcast_convert_type(lax.shift_left(p, 16), jnp.float32)
    hi = lax.bitcast_convert_type(p & HI_MASK, jnp.float32)
    return lo, hi


def _mod_kernel(c_ref, w_ref, b_ref, o_ref):
    c = c_ref[...]
    a = c * jax.nn.sigmoid(c)
    o_ref[0] = jnp.dot(a, w_ref[0], preferred_element_type=jnp.float32,
                       precision=lax.Precision.HIGHEST) + b_ref[0]


def _modulation(c_all, w_mod, b_mod):
    depth, d, n = w_mod.shape
    r = c_all.shape[0]
    tn = _pick(n, (1024, 512, 256, 128))
    return pl.pallas_call(
        _mod_kernel,
        grid=(depth, n // tn),
        in_specs=[pl.BlockSpec((r, d), lambda l, j: (0, 0)),
                  pl.BlockSpec((1, d, tn), lambda l, j: (l, 0, j)),
                  pl.BlockSpec((1, 1, tn), lambda l, j: (l, 0, j))],
        out_specs=pl.BlockSpec((1, r, tn), lambda l, j: (l, 0, j)),
        out_shape=jax.ShapeDtypeStruct((depth, r, n), jnp.float32),
        compiler_params=_cparams("arbitrary", "arbitrary"),
        name="modulation",
    )(c_all, w_mod, b_mod.reshape(depth, 1, n))


def _head_mean_sq(blk, gsum):
    sq = blk * blk
    hi = sq.astype(jnp.bfloat16)
    lo = (sq - hi.astype(jnp.float32)).astype(jnp.bfloat16)
    return (jnp.dot(hi, gsum, preferred_element_type=jnp.float32)
            + jnp.dot(lo, gsum, preferred_element_type=jnp.float32))


def _x_specs(xs, tm, n_lat_tiles, offsets=(0, 0), tile_of=lambda i: i):
    d = xs[0].shape[1]
    if len(xs) == 1:
        return [pl.BlockSpec((tm, d), lambda i: (tile_of(i), 0))]
    lat0, ctx0 = offsets
    return [pl.BlockSpec((tm, d), lambda i: (lat0 + jnp.minimum(tile_of(i), n_lat_tiles - 1), 0)),
            pl.BlockSpec((tm, d), lambda i: (ctx0 + jnp.maximum(tile_of(i) - n_lat_tiles, 0), 0))]


def _load_x(x_refs, n_lat_tiles, rows=slice(None), tile=None):
    if len(x_refs) == 1:
        return x_refs[0][rows, :]
    tile = pl.program_id(0) if tile is None else tile
    return jnp.where(tile < n_lat_tiles, x_refs[0][rows, :], x_refs[1][rows, :])


def _moe_mix(y0_ref, y1_ref, rw_ref, rows):
    rw = rw_ref[rows, :]
    w0 = rw[:, 0:1]
    w1 = rw[:, 1:2]
    a_lo, a_hi = _unpack_pairs(y0_ref[rows, :])
    b_lo, b_hi = _unpack_pairs(y1_ref[rows, :])
    return jnp.concatenate([a_lo * w0 + b_lo * w1, a_hi * w0 + b_hi * w1], axis=1)


def _inproj_kernel(*refs, n_x, n_lat_tiles, fused_combine):
    x_refs = refs[:n_x]
    refs = refs[n_x:]
    if fused_combine:
        y0_ref, y1_ref, rw_ref, modp_ref = refs[:4]
        refs = refs[4:]
        xo_ref = refs[-1]
        refs = refs[:-1]
    (mod_ref, g_ref, w_ref, gsum_ref, qg_ref, kg_ref, cos_ref, sin_ref,
     qa_ref, kta_ref, va_ref, hgl_ref, qc_ref, ktc_ref, vc_ref) = refs
    tm = qa_ref.shape[0]
    tr = min(tm, ROW_CHAIN)
    bf = jnp.bfloat16
    lane = lax.broadcasted_iota(jnp.int32, (tr, LANES), 1)
    first16 = (lane % 32) < 16
    gsum = gsum_ref[...]
    qg = qg_ref[...]
    kg = kg_ref[...]
    scale = g_ref[...] * (1.0 + mod_ref[0, 1:2, :])
    shift = mod_ref[0, 0:1, :]
    ones = jnp.ones((tr, LANES), bf)

    for r0 in range(0, tm, tr):
        rows = slice(r0, r0 + tr)
        x = _load_x(x_refs, n_lat_tiles, rows)
        if fused_combine:
            x = x + modp_ref[0, 5:6, :] * _moe_mix(y0_ref, y1_ref, rw_ref, rows)
            xo_ref[rows, :] = x
        r = lax.rsqrt(jnp.mean(x * x, axis=-1, keepdims=True) + EPS)
        h = (x * r) * scale + shift
        p = jnp.dot(h.astype(bf), w_ref[...], preferred_element_type=jnp.float32)
        cos = cos_ref[rows, :]
        sin = sin_ref[rows, :]

        def blk(off):
            return p[:, off:off + LANES]

        def rope(t):
            sw = jnp.where(first16, pltpu.roll(t, LANES - 16, axis=1), pltpu.roll(t, 16, axis=1))
            return t * cos + sw * sin

        def norm(t, g):
            return t * lax.rsqrt(_head_mean_sq(t, gsum) + EPS) * g

        def store_kv(kt_ref, v_ref, k_blk, v_blk):
            kt_ref[:, rows] = k_blk.T.astype(bf)
            v_ref[rows, 0:LANES] = v_blk.astype(bf)
            v_ref[rows, LANES:2 * LANES] = ones

        for i in range(A_Q // LANES):
            t = rope(norm(blk(OFF_AQ + i * LANES), qg)) * Q_SCALE
            qa_ref[rows, i * LANES:(i + 1) * LANES] = t.astype(bf)
        store_kv(kta_ref, va_ref, rope(norm(blk(OFF_AK), kg)), blk(OFF_AV))
        for i in range(B_CH // LANES):
            a = blk(OFF_BU + i * LANES)
            gt = blk(OFF_BU + B_CH + i * LANES)
            hgl_ref[rows, i * LANES:(i + 1) * LANES] = a * jax.nn.sigmoid(gt)
        for i in range(C_Q // LANES):
            t = rope(blk(OFF_CQ + i * LANES)) * Q_SCALE
            qc_ref[rows, i * LANES:(i + 1) * LANES] = t.astype(bf)
        store_kv(ktc_ref, vc_ref, rope(blk(OFF_CK)), blk(OFF_CV))


def _inproj(xs, mods, g1, w_in_bf, gsum, qg, kg, cos_t, sin_t, *, n, tm, n_lat_tiles, s, nb, x_offsets=(0, 0),
            combine=None):
    d = xs[0].shape[1]
    s_tiles = s // tm
    n_tiles = n // tm

    def bidx(i):
        return jnp.where(i < n_lat_tiles, (i * tm) // s, nb)

    def ridx(i):
        return jnp.where(i < n_lat_tiles, i % s_tiles, s_tiles)

    row = lambda w: pl.BlockSpec((tm, w), lambda i: (i, 0))
    ktspec = pl.BlockSpec((KT_ROWS, tm), lambda i: (0, i))
    const = lambda a: pl.BlockSpec(a.shape, lambda i: (0,) * a.ndim)
    bf = jnp.bfloat16
    modspec = pl.BlockSpec((1, N_MOD, d), lambda i: (bidx(i), 0, 0))
    in_specs = _x_specs(xs, tm, n_lat_tiles, x_offsets)
    args = list(xs)
    out_specs = [row(A_Q), ktspec, row(V_W), row(B_CH), row(C_Q), ktspec, row(V_W)]
    out_shape = [jax.ShapeDtypeStruct((n, A_Q), bf), jax.ShapeDtypeStruct((KT_ROWS, n), bf),
                 jax.ShapeDtypeStruct((n, V_W), bf), jax.ShapeDtypeStruct((n, B_CH), jnp.float32),
                 jax.ShapeDtypeStruct((n, C_Q), bf), jax.ShapeDtypeStruct((KT_ROWS, n), bf),
                 jax.ShapeDtypeStruct((n, V_W), bf)]
    aliases = {}
    if combine is not None:
        y, rw, mods_prev = combine
        assert len(xs) == 1 and y.shape[0] == 2 * n
        in_specs += [row(d // 2), pl.BlockSpec((tm, d // 2), lambda i: (i + n_tiles, 0)), row(LANES), modspec]
        args += [y, y, rw, mods_prev]
        out_specs.append(row(d))
        out_shape.append(jax.ShapeDtypeStruct((n, d), jnp.float32))
        aliases = {0: len(out_shape) - 1}
    in_specs += [modspec, const(g1), const(w_in_bf), const(gsum), const(qg), const(kg),
                 pl.BlockSpec((tm, LANES), lambda i: (ridx(i), 0)),
                 pl.BlockSpec((tm, LANES), lambda i: (ridx(i), 0))]
    args += [mods, g1, w_in_bf, gsum, qg, kg, cos_t, sin_t]
    outs = pl.pallas_call(
        functools.partial(_inproj_kernel, n_x=len(xs), n_lat_tiles=n_lat_tiles,
                          fused_combine=combine is not None),
        grid=(n_tiles,),
        in_specs=in_specs,
        out_specs=out_specs,
        out_shape=out_shape,
        input_output_aliases=aliases,
        compiler_params=_cparams("arbitrary"),
        name="inproj",
    )(*args)
    if combine is not None:
        return (outs[-1],) + tuple(outs[:-1])
    return (None,) + tuple(outs)


def _stack_heads(q_ref, tq):
    lane = lax.broadcasted_iota(jnp.int32, (tq, LANES), 1)
    lo = lane < HEAD_DIM
    qb = [q_ref[:, i * LANES:(i + 1) * LANES] for i in range(3)]
    zero = jnp.zeros_like(qb[0])
    s0 = jnp.concatenate([jnp.where(lo, t, zero) for t in qb], axis=0)
    s1 = jnp.concatenate([jnp.where(lo, zero, t) for t in qb], axis=0)
    return s0, s1, lo


def _unstack_store(o_ref, o0, o1, lo, tq):
    for i in range(3):
        rows = slice(i * tq, (i + 1) * tq)
        o_ref[:, i * LANES:(i + 1) * LANES] = jnp.where(lo, o0[rows], o1[rows]).astype(jnp.bfloat16)


def _krow(kv):
    del kv
    return slice(0, KT_ROWS)


def _vcol(kv):
    del kv
    return slice(0, V_W)


def _row_max(scores):
    m = None
    for t in scores:
        for c in range(0, t.shape[1], LANES):
            blk = t[:, c:c + LANES]
            m = blk if m is None else jnp.maximum(m, blk)
    return m.max(axis=-1, keepdims=True)


def _softmax_pv(scores, values, extra=None):
    m = _row_max(scores)
    if extra is not None:
        m = jnp.maximum(m, extra)
    acc = None
    for t, v in zip(scores, values):
        c = jnp.dot(jnp.exp2(t - m).astype(jnp.bfloat16), v, preferred_element_type=jnp.float32)
        acc = c if acc is None else acc + c
    den = acc[:, LANES:2 * LANES]
    if extra is not None:
        den = den + jnp.exp2(extra - m)
    return acc[:, 0:LANES] / den


ATTN_CHAIN_ROWS = 128


def _qk(q, kt):
    return jnp.dot(q, kt, preferred_element_type=jnp.float32)


def _attn_a_kernel(q_ref, ktl_ref, ktc_ref, vl_ref, vc_ref, o_ref, *, tq):
    s0, s1, lo = _stack_heads(q_ref, tq)
    outs = []
    for kv, qs in enumerate((s0, s1)):
        krow = _krow(kv)
        vcol = _vcol(kv)
        parts = []
        for r0 in range(0, 3 * tq, ATTN_CHAIN_ROWS):
            qr = qs[r0:r0 + ATTN_CHAIN_ROWS]
            ss = [_qk(qr, ktl_ref[krow, :]), _qk(qr, ktc_ref[krow, :])]
            parts.append(_softmax_pv(ss, [vl_ref[:, vcol], vc_ref[:, vcol]]))
        outs.append(jnp.concatenate(parts, axis=0))
    _unstack_store(o_ref, outs[0], outs[1], lo, tq)


def _kv_specs(nb, s, l):
    ctx0 = nb * s // l
    return [pl.BlockSpec((KT_ROWS, s), lambda b, j: (0, b)),
            pl.BlockSpec((KT_ROWS, l), lambda b, j: (0, ctx0 + b)),
            pl.BlockSpec((s, V_W), lambda b, j: (b, 0)),
            pl.BlockSpec((l, V_W), lambda b, j: (ctx0 + b, 0))]


def _attn_a(qa, kt, v, *, nb, s, l, tq):
    n = qa.shape[0]
    n_q = s // tq
    return pl.pallas_call(
        functools.partial(_attn_a_kernel, tq=tq),
        grid=(nb, n_q),
        in_specs=[pl.BlockSpec((tq, A_Q), lambda b, j: (b * n_q + j, 0))] + _kv_specs(nb, s, l),
        out_specs=pl.BlockSpec((tq, A_Q), lambda b, j: (b * n_q + j, 0)),
        out_shape=jax.ShapeDtypeStruct((n, A_Q), jnp.bfloat16),
        compiler_params=_cparams("arbitrary", "arbitrary"),
        name="attn_global",
    )(qa, kt, kt, v, v)


WIN_BLOCKS = 8


def _sink_column(sink_ref, kv, rows):
    return jnp.concatenate([jnp.full((rows, 1), sink_ref[3 * kv + g], jnp.float32) for g in range(3)], axis=0)


def _attn_c_kernel(sink_ref, bias_ref, q_ref, ktl_ref, ktc_ref, vl_ref, vc_ref, o_ref, *, s, blocks):
    j = pl.program_id(1)
    tq = WINDOW
    band = 3 * WINDOW
    bf = jnp.bfloat16
    f32 = jnp.float32
    m3 = 3 * tq
    stacks = [_stack_heads(q_ref.at[blk * tq:(blk + 1) * tq, :], tq) for blk in range(blocks)]
    lo = stacks[0][2]
    starts = [pl.multiple_of(jnp.clip((j * blocks + blk - 1) * WINDOW, 0, s - band), WINDOW)
              for blk in range(blocks)]
    outs = [[None, None] for _ in range(blocks)]
    for kv in range(C_KV_HEADS):
        krow = _krow(kv)
        vcol = _vcol(kv)
        sk = _sink_column(sink_ref, kv, tq)
        sc_all = _qk(jnp.concatenate([st[kv] for st in stacks], axis=0), ktc_ref[krow, :])
        accs, pcs, ms = [], [], []
        for blk in range(blocks):
            start = starts[blk]
            sl = _qk(stacks[blk][kv], ktl_ref[krow, pl.ds(start, band)]) + bias_ref[j * blocks + blk - start // WINDOW]
            sc = sc_all[blk * m3:(blk + 1) * m3]
            m = jnp.maximum(_row_max([sl, sc]), sk)
            accs.append(jnp.dot(jnp.exp2(sl - m).astype(bf), vl_ref[pl.ds(start, band), vcol],
                                preferred_element_type=f32))
            pcs.append(jnp.exp2(sc - m).astype(bf))
            ms.append(m)
        acc_ctx = jnp.dot(jnp.concatenate(pcs, axis=0), vc_ref[:, vcol], preferred_element_type=f32)
        for blk in range(blocks):
            acc = accs[blk] + acc_ctx[blk * m3:(blk + 1) * m3]
            den = acc[:, LANES:2 * LANES] + jnp.exp2(sk - ms[blk])
            outs[blk][kv] = acc[:, 0:LANES] / den
    for blk in range(blocks):
        _unstack_store(o_ref.at[blk * tq:(blk + 1) * tq, :], outs[blk][0], outs[blk][1], lo, tq)


def _window_bias():
    r = np.arange(3 * WINDOW)[:, None] % WINDOW
    col = np.arange(3 * WINDOW)[None, :]
    tabs = [np.where(np.abs(col - r - WINDOW * off) <= WINDOW, 0.0, -np.inf) for off in range(3)]
    return jnp.asarray(np.stack(tabs), jnp.float32)


def _attn_c(qc, kt, v, sink2, *, nb, s, l):
    n = qc.shape[0]
    blocks = _pick(s // WINDOW, (WIN_BLOCKS, 8, 4, 2, 1))
    tq = blocks * WINDOW
    n_q = s // tq
    bias = _window_bias()
    return pl.pallas_call(
        functools.partial(_attn_c_kernel, s=s, blocks=blocks),
        grid=(nb, n_q),
        in_specs=[pl.BlockSpec(memory_space=pltpu.SMEM),
                  pl.BlockSpec(bias.shape, lambda b, j: (0, 0, 0)),
                  pl.BlockSpec((tq, C_Q), lambda b, j: (b * n_q + j, 0))] + _kv_specs(nb, s, l),
        out_specs=pl.BlockSpec((tq, C_Q), lambda b, j: (b * n_q + j, 0)),
        out_shape=jax.ShapeDtypeStruct((n, C_Q), jnp.bfloat16),
        compiler_params=_cparams("arbitrary", "arbitrary"),
        name="attn_window",
    )(sink2, bias, qc, kt, kt, v, v)


def _attn_ctx_kernel(sink_ref, qa_ref, kta_ref, va_ref, qc_ref, ktc_ref, vc_ref, oa_in, oc_in, oa_ref, oc_ref, *, l):
    del oa_in, oc_in
    for q_ref, kt_ref, v_ref, o_ref, with_sink in ((qa_ref, kta_ref, va_ref, oa_ref, False),
                                                   (qc_ref, ktc_ref, vc_ref, oc_ref, True)):
        s0, s1, lo = _stack_heads(q_ref, l)
        outs = []
        for kv, qs in enumerate((s0, s1)):
            sk = _sink_column(sink_ref, kv, l) if with_sink else None
            outs.append(_softmax_pv([_qk(qs, kt_ref[_krow(kv), :])], [v_ref[:, _vcol(kv)]], sk))
        _unstack_store(o_ref, outs[0], outs[1], lo, l)


def _attn_ctx(qa, kta, va, qc, ktc, vc, sink2, oa, oc, *, nb, s, l):
    base = nb * s // l
    row = lambda w: pl.BlockSpec((l, w), lambda b: (base + b, 0))
    ktspec = pl.BlockSpec((KT_ROWS, l), lambda b: (0, base + b))
    anyspec = pl.BlockSpec(memory_space=pl.ANY)
    return pl.pallas_call(
        functools.partial(_attn_ctx_kernel, l=l),
        grid=(nb,),
        in_specs=[pl.BlockSpec(memory_space=pltpu.SMEM), row(A_Q), ktspec, row(V_W), row(C_Q), ktspec, row(V_W),
                  anyspec, anyspec],
        out_specs=[row(A_Q), row(C_Q)],
        out_shape=[jax.ShapeDtypeStruct(oa.shape, oa.dtype), jax.ShapeDtypeStruct(oc.shape, oc.dtype)],
        input_output_aliases={7: 0, 8: 1},
        compiler_params=_cparams("arbitrary"),
        name="attn_context",
    )(sink2, qa, kta, va, qc, ktc, vc, oa, oc)


CONV_HALO = 16
CONV_ROWS = 64


def _conv_kernel(prev_ref, cur_ref, next_ref, w_ref, b_ref, g_ref, beta_ref, *rest, chunks, tc):
    o_ref, sh_ref = rest[-2:]
    j = pl.program_id(1)
    has_prev = j > 0
    has_next = j < chunks - 1
    rows = tc + 2 * CONV_HALO
    sh_ref[0, 0:CONV_HALO, :] = jnp.where(has_prev, prev_ref[...], 0.0)
    sh_ref[0, CONV_HALO:CONV_HALO + tc, :] = cur_ref[...]
    sh_ref[0, CONV_HALO + tc:rows, :] = jnp.where(has_next, next_ref[...], 0.0)
    for b in range(1, SUBLANES):
        sh_ref[b, 0:rows - SUBLANES, :] = sh_ref[0, b:b + rows - SUBLANES, :]
    base = CONV_HALO - CONV_W // 2
    for r0 in range(0, tc, CONV_ROWS):
        acc = None
        for k in range(CONV_W):
            a, b = divmod(base + k, SUBLANES)
            term = sh_ref[b, SUBLANES * a + r0:SUBLANES * a + r0 + CONV_ROWS, :] * w_ref[k:k + 1, :]
            acc = term if acc is None else acc + term
        hc = acc + b_ref[...]
        mu = jnp.mean(hc, axis=-1, keepdims=True)
        xc = hc - mu
        var = jnp.mean(xc * xc, axis=-1, keepdims=True)
        y = xc * lax.rsqrt(var + EPS) * g_ref[...] + beta_ref[...]
        o_ref[r0:r0 + CONV_ROWS, :] = (y * jax.nn.sigmoid(y)).astype(o_ref.dtype)


def _conv(hgl, w, b, g, beta, *, nb, seq, base_rows, tc, prev_out=None):
    n = hgl.shape[0]
    chunks = seq // tc
    base = base_rows // tc

    def idx(b_, j, delta):
        return base + b_ * chunks + jnp.clip(j + delta, 0, chunks - 1)

    blk = lambda delta: pl.BlockSpec((tc, B_CH), lambda b_, j: (idx(b_, j, delta), 0))
    const = lambda a: pl.BlockSpec(a.shape, lambda b_, j: (0,) * a.ndim)
    hb = tc // CONV_HALO
    last_halo = n // CONV_HALO - 1
    prev_halo = pl.BlockSpec((CONV_HALO, B_CH), lambda b_, j: (jnp.maximum(idx(b_, j, 0) * hb - 1, 0), 0))
    next_halo = pl.BlockSpec((CONV_HALO, B_CH),
                             lambda b_, j: (jnp.minimum((idx(b_, j, 0) + 1) * hb, last_halo), 0))
    in_specs = [prev_halo, blk(0), next_halo, const(w), const(b), const(g), const(beta)]
    args = [hgl, hgl, hgl, w, b, g, beta]
    aliases = {}
    if prev_out is not None:
        in_specs.append(pl.BlockSpec(memory_space=pl.ANY))
        args.append(prev_out)
        aliases = {len(args) - 1: 0}
    return pl.pallas_call(
        functools.partial(_conv_kernel, chunks=chunks, tc=tc),
        grid=(nb, chunks),
        in_specs=in_specs,
        out_specs=blk(0),
        out_shape=jax.ShapeDtypeStruct((n, B_CH), jnp.bfloat16),
        scratch_shapes=[pltpu.VMEM((SUBLANES, tc + 2 * CONV_HALO, B_CH), jnp.float32)],
        input_output_aliases=aliases,
        compiler_params=_cparams("arbitrary", "arbitrary"),
        name="conformer_conv",
    )(*args)


META_ROWS = 8


def _outproj_kernel(*refs, n_x, n_lat_tiles, n_tiles):
    x_refs = refs[:n_x]
    (oa_ref, ob_ref, oc_ref, mod_ref, g_ref, w_ref, wr_ref, br_ref, tri_ref,
     xo_ref, hp_ref, meta_ref, rw_ref, cnt_ref, lg_ref) = refs[n_x:]
    i = pl.program_id(0)
    tm = xo_ref.shape[0]
    f32 = jnp.float32

    @pl.when(i == 0)
    def _():
        cnt_ref[...] = jnp.zeros_like(cnt_ref)
        lg_ref[...] = jnp.zeros_like(lg_ref)

    lat = jnp.concatenate([oa_ref[...], ob_ref[...], oc_ref[...]], axis=1)
    mix = jnp.dot(lat, w_ref[...], preferred_element_type=f32)
    x = _load_x(x_refs, n_lat_tiles, tile=jnp.minimum(i, n_tiles - 1)) + mod_ref[0, 2:3, :] * mix
    xo_ref[...] = x
    r = lax.rsqrt(jnp.mean(x * x, axis=-1, keepdims=True) + EPS)
    h = (x * r) * (g_ref[...] * (1.0 + mod_ref[0, 4:5, :])) + mod_ref[0, 3:4, :]
    h_hi = h.astype(jnp.bfloat16)
    hp_ref[...] = _pack_pairs(h_hi)
    h_lo = (h - h_hi.astype(f32)).astype(jnp.bfloat16)
    r_hi = jnp.dot(h_hi, wr_ref[...], preferred_element_type=f32)
    r_lo = jnp.dot(h_lo, wr_ref[:, 0:LANES], preferred_element_type=f32)
    lg = lg_ref[(i + 1) % 2]
    lg_ref[i % 2] = r_hi[:, 0:LANES] + r_hi[:, LANES:2 * LANES] + r_lo + br_ref[...]

    lane = lax.broadcasted_iota(jnp.int32, (tm, LANES), 1).astype(f32)
    big = float(LANES)
    ninf = -jnp.inf
    glog = jnp.where(lane < N_GROUPS, lg, ninf)
    gmax = glog.max(axis=-1, keepdims=True)
    g_val = 1.0 / jnp.exp(glog - gmax).sum(axis=-1, keepdims=True)
    g_idx = jnp.where(glog == gmax, lane, big).min(axis=-1, keepdims=True)
    e_lo = N_GROUPS + EXPERTS_PER_GROUP * g_idx
    el = jnp.where((lane >= e_lo) & (lane < e_lo + EXPERTS_PER_GROUP), lg, ninf)
    v0 = el.max(axis=-1, keepdims=True)
    i0 = jnp.where(el == v0, lane, big).min(axis=-1, keepdims=True)
    el1 = jnp.where(lane == i0, ninf, el)
    v1 = el1.max(axis=-1, keepdims=True)
    i1 = jnp.where(el1 == v1, lane, big).min(axis=-1, keepdims=True)
    t = jnp.exp(v1 - v0)
    w0 = g_val / (1.0 + t)
    w1 = g_val * t / (1.0 + t)
    e0 = i0 - N_GROUPS
    e1 = i1 - N_GROUPS

    cnt = cnt_ref[0:1, :]
    tri = tri_ref[...]
    ranks = []
    for e in (e0, e1):
        oh = lane == e
        ohf = oh.astype(f32)
        pre = jnp.dot(tri, ohf.astype(jnp.bfloat16), preferred_element_type=f32) + cnt
        ranks.append(jnp.where(oh, pre, 0.0).sum(axis=-1, keepdims=True))
        cnt = cnt + ohf.sum(axis=0, keepdims=True)
    cnt_ref[0:1, :] = jnp.where(i > 0, cnt, cnt_ref[0:1, :])
    rw_ref[...] = jnp.where(lane == 0, w0, jnp.where(lane == 1, w1, 0.0))
    rec = jnp.where(lane == 0, e0, jnp.where(lane == 1, e1, jnp.where(lane == 2, ranks[0],
                    jnp.where(lane == 3, ranks[1], jnp.where(lane == 4, w0, jnp.where(lane == 5, w1, 0.0))))))
    meta_ref[...] = rec.T[0:META_ROWS, :]


def _outproj(xs, oa, ob, oc, mods, g2, w_out_bf, wr, br, tri, *, n, tm, n_tiles, n_lat_tiles, s, nb,
             x_offsets=(0, 0)):
    d = xs[0].shape[1]
    rows = n_tiles * tm
    cur = lambda i: jnp.minimum(i, n_tiles - 1)
    prev = lambda i: jnp.maximum(i - 1, 0)

    def bidx(i):
        return jnp.where(cur(i) < n_lat_tiles, (cur(i) * tm) // s, nb)

    row = lambda w: pl.BlockSpec((tm, w), lambda i: (cur(i), 0))
    const = lambda a: pl.BlockSpec(a.shape, lambda i: (0,) * a.ndim)
    return pl.pallas_call(
        functools.partial(_outproj_kernel, n_x=len(xs), n_lat_tiles=n_lat_tiles, n_tiles=n_tiles),
        grid=(n_tiles + 1,),
        in_specs=_x_specs(xs, tm, n_lat_tiles, x_offsets, cur) + [
                  row(A_Q), row(B_CH), row(C_Q),
                  pl.BlockSpec((1, N_MOD, d), lambda i: (bidx(i), 0, 0)),
                  const(g2), const(w_out_bf), const(wr), const(br), const(tri)],
        out_specs=[row(d), row(d // 2),
                   pl.BlockSpec((META_ROWS, tm), lambda i: (0, prev(i))),
                   pl.BlockSpec((tm, LANES), lambda i: (prev(i), 0)),
                   pl.BlockSpec((8, LANES), lambda i: (0, 0))],
        out_shape=[jax.ShapeDtypeStruct((n, d), jnp.float32),
                   jax.ShapeDtypeStruct((rows, d // 2), jnp.int32),
                   jax.ShapeDtypeStruct((META_ROWS, rows), jnp.float32),
                   jax.ShapeDtypeStruct((rows, LANES), jnp.float32),
                   jax.ShapeDtypeStruct((8, LANES), jnp.float32)],
        scratch_shapes=[pltpu.VMEM((2, tm, LANES), jnp.float32)],
        input_output_aliases={0: 0} if len(xs) == 1 else {},
        compiler_params=_cparams("arbitrary"),
        name="outproj_router",
    )(*xs, oa, ob, oc, mods, g2, w_out_bf, wr, br, tri)


def _sc_mesh():
    return plsc.VectorSubcoreMesh(core_axis_name="core", subcore_axis_name="subcore")


def sc_gather_rows(table, idx2):
    r = idx2.shape[1]
    w = table.shape[1]
    assert r % (2 * SC_WINDOW) == 0
    half = r // SC_WINDOW // 2

    @functools.partial(pl.kernel, out_type=jax.ShapeDtypeStruct((r, w), table.dtype), mesh=_sc_mesh())
    def k(x_hbm, i_hbm, o_hbm):
        def body(i_vmem, o_vmem):
            pltpu.sync_copy(x_hbm.at[i_vmem.at[0]], o_vmem)

        pltpu.emit_pipeline(
            body,
            grid=(2, half),
            in_specs=[pl.BlockSpec((1, SC_WINDOW), lambda c, i: (0, c * half + i))],
            out_specs=[pl.BlockSpec((SC_WINDOW, w), lambda c, i: (c * half + i, 0),
                                    pipeline_mode=pl.Buffered(1))],
            core_axis_name=("core", "subcore"),
            dimension_semantics=(pltpu.PARALLEL, pltpu.PARALLEL),
        )(i_hbm, o_hbm)

    return k(table, idx2)


def sc_scatter_rows2(rows, idx2, n_out):
    r, w = rows.shape
    assert idx2.shape == (1, 2 * r) and r % (2 * SC_WINDOW) == 0
    windows = r // SC_WINDOW
    half = windows // 2

    @functools.partial(pl.kernel, out_type=jax.ShapeDtypeStruct((n_out, w), rows.dtype), mesh=_sc_mesh(),
                       scratch_types=[])
    def k(x_hbm, ia_hbm, ib_hbm, o_hbm):
        def body(x_vmem, ia_vmem, ib_vmem):
            pltpu.sync_copy(x_vmem, o_hbm.at[ia_vmem.at[0]])
            pltpu.sync_copy(x_vmem, o_hbm.at[ib_vmem.at[0]])

        pltpu.emit_pipeline(
            body,
            grid=(2, half),
            in_specs=[pl.BlockSpec((SC_WINDOW, w), lambda c, i: (c * half + i, 0),
                                   pipeline_mode=pl.Buffered(1)),
                      pl.BlockSpec((1, SC_WINDOW), lambda c, i: (0, c * half + i)),
                      pl.BlockSpec((1, SC_WINDOW), lambda c, i: (0, windows + c * half + i))],
            out_specs=[],
            core_axis_name=("core", "subcore"),
            dimension_semantics=(pltpu.PARALLEL, pltpu.PARALLEL),
        )(x_hbm, ia_hbm, ib_hbm)

    return k(rows, idx2, idx2)


def _expert_kernel(te_ref, nv_ref, x_ref, wg_ref, wu_ref, wd_ref, o_ref, wgb_ref, wub_ref, wdb_ref):
    t = pl.program_id(0)
    nvalid = nv_ref[t]

    @pl.when((t == 0) | (te_ref[t] != te_ref[jnp.maximum(t - 1, 0)]))
    def _():
        wgb_ref[...] = wg_ref[0].astype(jnp.bfloat16)
        wub_ref[...] = wu_ref[0].astype(jnp.bfloat16)
        wdb_ref[...] = wd_ref[0].astype(jnp.bfloat16)

    @pl.when(nvalid > 0)
    def _():
        rows = lax.broadcasted_iota(jnp.int32, x_ref.shape, 0)
        lo, hi = _unpack_pairs(jnp.where(rows < nvalid, x_ref[...], 0))
        xb = jnp.concatenate([lo, hi], axis=1).astype(jnp.bfloat16)
        g = jnp.dot(xb, wgb_ref[...], preferred_element_type=jnp.float32)
        u = jnp.dot(xb, wub_ref[...], preferred_element_type=jnp.float32)
        a = (g * jax.nn.sigmoid(g) * u).astype(jnp.bfloat16)
        o_ref[...] = _pack_pairs(jnp.dot(a, wdb_ref[...], preferred_element_type=jnp.float32))

    @pl.when(nvalid == 0)
    def _():
        o_ref[...] = jnp.zeros_like(o_ref)


def _experts(buf, tile_expert, tile_nvalid, wg, wu, wd, *, layer):
    rows, wp = buf.shape
    _, _, d, f = wg.shape
    n_tiles = rows // EXPERT_TILE
    grid_spec = pltpu.PrefetchScalarGridSpec(
        num_scalar_prefetch=2,
        grid=(n_tiles,),
        in_specs=[pl.BlockSpec((EXPERT_TILE, wp), lambda t, te, nv: (t, 0)),
                  pl.BlockSpec((None, 1, d, f), lambda t, te, nv: (layer, te[t], 0, 0)),
                  pl.BlockSpec((None, 1, d, f), lambda t, te, nv: (layer, te[t], 0, 0)),
                  pl.BlockSpec((None, 1, f, d), lambda t, te, nv: (layer, te[t], 0, 0))],
        out_specs=pl.BlockSpec((EXPERT_TILE, wp), lambda t, te, nv: (t, 0)),
        scratch_shapes=[pltpu.VMEM((d, f), jnp.bfloat16), pltpu.VMEM((d, f), jnp.bfloat16),
                        pltpu.VMEM((f, d), jnp.bfloat16)],
    )
    return pl.pallas_call(
        _expert_kernel,
        grid_spec=grid_spec,
        out_shape=jax.ShapeDtypeStruct((rows, wp), jnp.int32),
        compiler_params=_cparams("arbitrary"),
        name="expert_ffn",
    )(tile_expert, tile_nvalid, buf, wg, wu, wd)


def _final_kernel(y0_ref, y1_ref, rw_ref, x_ref, mod_ref, fg_ref, o_ref):
    x = x_ref[...] + mod_ref[0, 5:6, :] * _moe_mix(y0_ref, y1_ref, rw_ref, slice(None))
    r = lax.rsqrt(jnp.mean(x * x, axis=-1, keepdims=True) + EPS)
    o_ref[...] = x * r * fg_ref[...]


def _final_kernel_into(y0_ref, y1_ref, rw_ref, x_ref, mod_ref, fg_ref, prev_ref, o_ref):
    del prev_ref
    _final_kernel(y0_ref, y1_ref, rw_ref, x_ref, mod_ref, fg_ref, o_ref)


def _final_combine(y, rw, xall, mods, final_g, *, tm, n_tiles, s, out_rows, out_tile0, prev_out=None):
    d = xall.shape[1]
    row = lambda w: pl.BlockSpec((tm, w), lambda i: (i, 0))
    in_specs = [row(d // 2),
                pl.BlockSpec((tm, d // 2), lambda i: (i + n_tiles, 0)),
                row(LANES), row(d),
                pl.BlockSpec((1, N_MOD, d), lambda i: ((i * tm) // s, 0, 0)),
                pl.BlockSpec(final_g.shape, lambda i: (0, 0))]
    args = [y, y, rw, xall, mods, final_g]
    aliases = {}
    body = _final_kernel
    if prev_out is not None:
        in_specs.append(pl.BlockSpec(memory_space=pl.ANY))
        args.append(prev_out)
        aliases = {len(args) - 1: 0}
        body = _final_kernel_into
    return pl.pallas_call(
        body,
        grid=(n_tiles,),
        in_specs=in_specs,
        out_specs=pl.BlockSpec((tm, d), lambda i: (out_tile0 + i, 0)),
        out_shape=jax.ShapeDtypeStruct((out_rows, d), jnp.float32),
        input_output_aliases=aliases,
        compiler_params=_cparams("arbitrary"),
        name="moe_combine_final",
    )(*args)


def _dest_kernel(ps_ref, meta_ref, o_ref):
    slot = pl.program_id(0)
    e = meta_ref[pl.ds(slot, 1), :]
    d = meta_ref[pl.ds(TOP_K + slot, 1), :]
    for k in range(N_EXPERTS):
        d = d + jnp.where(e == float(k), ps_ref[k], 0.0)
    o_ref[...] = d.astype(jnp.int32)


def _dest_rows(meta, pstarts, tcols):
    n = meta.shape[1]
    nt = n // tcols
    return pl.pallas_call(
        _dest_kernel,
        grid=(TOP_K, nt),
        in_specs=[pl.BlockSpec(memory_space=pltpu.SMEM),
                  pl.BlockSpec((META_ROWS, tcols), lambda k, i: (0, i))],
        out_specs=pl.BlockSpec((1, tcols), lambda k, i: (0, k * nt + i)),
        out_shape=jax.ShapeDtypeStruct((1, TOP_K * n), jnp.int32),
        compiler_params=_cparams("arbitrary", "arbitrary"),
        name="moe_dest",
    )(pstarts.astype(jnp.float32), meta)


def _dispatch_plan(meta, counts, n_rows_buf, tcols):
    cnt = counts[0, :N_EXPERTS].astype(jnp.int32)
    padded = (cnt + EXPERT_TILE - 1) // EXPERT_TILE * EXPERT_TILE
    pends = jnp.cumsum(padded)
    pstarts = pends - padded
    dest = _dest_rows(meta, pstarts, tcols)
    tile_start = jnp.arange(n_rows_buf // EXPERT_TILE, dtype=jnp.int32) * EXPERT_TILE
    te = jnp.sum((tile_start[:, None] >= pends[None, :]).astype(jnp.int32), axis=1)
    te = jnp.minimum(te, N_EXPERTS - 1)
    onehot = te[:, None] == jnp.arange(N_EXPERTS, dtype=jnp.int32)[None, :]
    cnt_te = jnp.sum(jnp.where(onehot, cnt[None, :], 0), axis=1)
    pstart_te = jnp.sum(jnp.where(onehot, pstarts[None, :], 0), axis=1)
    nvalid = jnp.clip(cnt_te - (tile_start - pstart_te), 0, EXPERT_TILE).astype(jnp.int32)
    return dest, te, nvalid


def _permute_heads(w, axis):
    heads = [lax.slice_in_dim(w, HEAD_DIM * h, HEAD_DIM * (h + 1), axis=axis)
             for h in HEAD_PERM[::HEAD_DIM] // HEAD_DIM]
    return jnp.concatenate(heads, axis=axis)


def _rope_tables(s, tm):
    pos = np.arange(s)
    pos_row = jnp.asarray(pos // GRID_W, jnp.float32)
    pos_col = jnp.asarray(pos % GRID_W, jnp.float32)
    n_freq = HEAD_DIM // 4
    inv = ROPE_THETA ** (-jnp.arange(n_freq, dtype=jnp.float32) / n_freq)
    ang_row = pos_row[:, None] * inv
    ang_col = pos_col[:, None] * inv
    ang = jnp.concatenate([ang_row, ang_row, ang_col, ang_col] * (LANES // HEAD_DIM), axis=-1)
    sign = np.where((np.arange(LANES) % 32) < 16, -1.0, 1.0).astype(np.float32)
    cos_t = jnp.concatenate([jnp.cos(ang), jnp.ones((tm, LANES), jnp.float32)], axis=0)
    sin_t = jnp.concatenate([jnp.sin(ang) * sign, jnp.zeros((tm, LANES), jnp.float32)], axis=0)
    return cos_t, sin_t


def kernel(x, c, ctx, c_ctx, norm1_g, norm2_g, w_mod, b_mod, w_in, q_norm_g, k_norm_g, conv_w, conv_b, conv_ln_g, conv_ln_b, sink, w_out, w_group, b_group, w_expert, b_expert, w_gate, w_up, w_down, final_g):
    nb, s, d = x.shape
    l = ctx.shape[1]
    depth = w_in.shape[0]
    assert w_in.shape[2] == D_IN and w_out.shape[1] == D_MIX
    assert s % GRID_W == 0 and s >= 3 * WINDOW and s % WINDOW == 0 and l % WINDOW == 0
    groups = BATCH_GROUPS if nb % BATCH_GROUPS == 0 else 1
    nbg = nb // groups
    n_lat, n_ctx = nbg * s, nbg * l
    tm = _pick(np.gcd(s, n_ctx), (512, 256, 128))
    tq = _pick(s, (512, 256, 128))
    tc = _pick(np.gcd(s, l), (256, 128))
    assert n_lat % l == 0
    bf = jnp.bfloat16
    f32 = jnp.float32

    x2d = x.reshape(nb * s, d)
    ctx2d = ctx.reshape(nb * l, d)
    c_all = jnp.concatenate([c, c_ctx[None, :]], axis=0)
    mods_all = _modulation(c_all, w_mod, b_mod).reshape(depth, nb + 1, N_MOD, d)
    cos_t, sin_t = _rope_tables(s, tm)
    head_id = np.arange(LANES) // HEAD_DIM
    gsum = jnp.asarray((head_id[:, None] == head_id[None, :]) / HEAD_DIM, bf)
    tri = jnp.asarray(np.tril(np.ones((tm, tm), np.float32), -1), bf)
    n_lat_tiles = n_lat // tm

    xs = [(x2d, ctx2d)] * groups
    pending = [None] * groups
    for i in range(depth):
        last = i == depth - 1
        with_ctx = not last
        qg = jnp.tile(q_norm_g[i], LANES // HEAD_DIM)[None, :]
        kg = jnp.tile(k_norm_g[i], LANES // HEAD_DIM)[None, :]
        w_in_bf = jnp.concatenate(
            [_permute_heads(w_in[i][:, OFF_AQ:OFF_AK], 1), w_in[i][:, OFF_AK:OFF_CQ],
             _permute_heads(w_in[i][:, OFF_CQ:OFF_CK], 1), w_in[i][:, OFF_CK:]], axis=1).astype(bf)
        w_out_bf = jnp.concatenate(
            [_permute_heads(w_out[i][0:A_Q], 0), w_out[i][A_Q:A_Q + B_CH],
             _permute_heads(w_out[i][A_Q + B_CH:], 0)], axis=0).astype(bf)
        sink2 = sink[i] * LOG2E
        conv_args = (conv_w[i].reshape(CONV_W, B_CH), conv_b[i][None, :], conv_ln_g[i][None, :],
                     conv_ln_b[i][None, :])
        wr32 = jnp.zeros((d, LANES), f32).at[:, :N_GROUPS].set(w_group[i])
        wr32 = wr32.at[:, N_GROUPS:N_GROUPS + N_EXPERTS].set(w_expert[i])
        wr_hi = wr32.astype(bf)
        wr = jnp.concatenate([wr_hi, (wr32 - wr_hi.astype(f32)).astype(bf)], axis=1)
        br = jnp.zeros((1, LANES), f32).at[0, :N_GROUPS].set(b_group[i])
        br = br.at[0, N_GROUPS:N_GROUPS + N_EXPERTS].set(b_expert[i])
        n_tok = n_lat + n_ctx if with_ctx else n_lat
        n_tiles = n_tok // tm
        n_rows_buf = 2 * n_tok + N_EXPERTS * EXPERT_TILE
        for g in range(groups):
            b0 = g * nbg
            mods = jnp.concatenate([mods_all[i, b0:b0 + nbg], mods_all[i, nb:nb + 1]], axis=0)
            x_offsets = (b0 * s // tm, b0 * l // tm)
            x_new, qa, kta, va, hgl, qc, ktc, vc = _inproj(
                xs[g], mods, norm1_g[i][None, :], w_in_bf, gsum, qg, kg, cos_t, sin_t, n=n_lat + n_ctx,
                tm=tm, n_lat_tiles=n_lat_tiles, s=s, nb=nbg, x_offsets=x_offsets, combine=pending[g])
            if pending[g] is not None:
                xs[g] = (x_new,)
            oa = _attn_a(qa, kta, va, nb=nbg, s=s, l=l, tq=tq)
            oc = _attn_c(qc, ktc, vc, sink2, nb=nbg, s=s, l=l)
            ob = _conv(hgl, *conv_args, nb=nbg, seq=s, base_rows=0, tc=tc)
            if with_ctx:
                oa, oc = _attn_ctx(qa, kta, va, qc, ktc, vc, sink2, oa, oc, nb=nbg, s=s, l=l)
                ob = _conv(hgl, *conv_args, nb=nbg, seq=l, base_rows=n_lat, tc=tc, prev_out=ob)
            xall, hp, meta, rw, counts = _outproj(xs[g], oa, ob, oc, mods, norm2_g[i][None, :], w_out_bf,
                                                  wr, br, tri, n=n_lat + n_ctx, tm=tm, n_tiles=n_tiles,
                                                  n_lat_tiles=n_lat_tiles, s=s, nb=nbg, x_offsets=x_offsets)
            dest, te, nvalid = _dispatch_plan(meta, counts, n_rows_buf,
                                              _pick(n_tok, (8192, 4096, 2048, 1024, 512, 256, 128)))
            buf = sc_scatter_rows2(hp, dest, n_rows_buf)
            eo = _experts(buf, te, nvalid, w_gate, w_up, w_down, layer=i)
            y = sc_gather_rows(eo, dest)
            xs[g] = (xall,)
            pending[g] = (y, rw, mods)
    out = None
    for g in range(groups):
        y, rw, mods = pending[g]
        out = _final_combine(y, rw, xs[g][0], mods, final_g[None, :], tm=tm, n_tiles=n_lat_tiles, s=s,
                             out_rows=nb * s, out_tile0=g * n_lat_tiles, prev_out=out)
    return out.reshape(nb, s, d)
```

```python
import functools

import jax
import jax.numpy as jnp
import numpy as np
from jax import lax
from jax.experimental import pallas as pl
from jax.experimental.pallas import tpu as pltpu
from jax.experimental.pallas import tpu_sc as plsc

HEAD_DIM = 64
GRID_W = 64
ROPE_THETA = 10000.0
A_HEADS, A_KV_HEADS = 6, 2
C_HEADS, C_KV_HEADS = 6, 2
B_CH = 256
CONV_W = 31
WINDOW = 128
N_GROUPS = 4
EXPERTS_PER_GROUP = 8
N_EXPERTS = N_GROUPS * EXPERTS_PER_GROUP
TOP_K = 2
N_MOD = 6
EPS = 1e-6
ATTN_SCALE = HEAD_DIM ** -0.5
LOG2E = 1.4426950408889634
Q_SCALE = ATTN_SCALE * LOG2E

A_Q = A_HEADS * HEAD_DIM
A_KV = A_KV_HEADS * HEAD_DIM
C_Q = C_HEADS * HEAD_DIM
C_KV = C_KV_HEADS * HEAD_DIM
D_MIX = A_Q + B_CH + C_Q
OFF_AQ = 0
OFF_AK = OFF_AQ + A_Q
OFF_AV = OFF_AK + A_KV
OFF_BU = OFF_AV + A_KV
OFF_CQ = OFF_BU + 2 * B_CH
OFF_CK = OFF_CQ + C_Q
OFF_CV = OFF_CK + C_KV
D_IN = OFF_CV + C_KV

LANES = 128
SUBLANES = 8
KT_ROWS = LANES
V_W = 2 * LANES
HEAD_PERM = np.concatenate([np.arange(HEAD_DIM) + HEAD_DIM * h for b in range(3) for h in (b, b + 3)])
EXPERT_TILE = 512
SC_WINDOW = 128
VMEM_LIMIT = 56 * 1024 * 1024
HI_MASK = -65536
BATCH_GROUPS = 2
ROW_CHAIN = 256


def _cparams(*sem):
    return pltpu.CompilerParams(dimension_semantics=sem, vmem_limit_bytes=VMEM_LIMIT)


def _pick(n, cands):
    for c in cands:
        if n % c == 0:
            return c
    raise ValueError(f"no tile in {cands} divides {n}")


def _pack_pairs(x):
    w = x.shape[1] // 2
    lo = lax.bitcast_convert_type(x[:, :w].astype(jnp.bfloat16).astype(jnp.float32), jnp.int32)
    hi = lax.bitcast_convert_type(x[:, w:].astype(jnp.bfloat16).astype(jnp.float32), jnp.int32)
    return (hi & HI_MASK) | lax.shift_right_logical(lo, 16)


def _unpack_pairs(p):
    lo = lax.bitcast_convert_type(lax.shift_left(p, 16), jnp.float32)
    hi = lax.bitcast_convert_type(p & HI_MASK, jnp.float32)
    return lo, hi


def _mod_kernel(c_ref, w_ref, b_ref, o_ref):
    c = c_ref[...]
    a = c * jax.nn.sigmoid(c)
    o_ref[0] = jnp.dot(a, w_ref[0], preferred_element_type=jnp.float32,
                       precision=lax.Precision.HIGHEST) + b_ref[0]


def _modulation(c_all, w_mod, b_mod):
    depth, d, n = w_mod.shape
    r = c_all.shape[0]
    tn = _pick(n, (1024, 512, 256, 128))
    return pl.pallas_call(
        _mod_kernel,
        grid=(depth, n // tn),
        in_specs=[pl.BlockSpec((r, d), lambda l, j: (0, 0)),
                  pl.BlockSpec((1, d, tn), lambda l, j: (l, 0, j)),
                  pl.BlockSpec((1, 1, tn), lambda l, j: (l, 0, j))],
        out_specs=pl.BlockSpec((1, r, tn), lambda l, j: (l, 0, j)),
        out_shape=jax.ShapeDtypeStruct((depth, r, n), jnp.float32),
        compiler_params=_cparams("arbitrary", "arbitrary"),
        name="modulation",
    )(c_all, w_mod, b_mod.reshape(depth, 1, n))


def _head_mean_sq(blk, gsum):
    sq = blk * blk
    hi = sq.astype(jnp.bfloat16)
    lo = (sq - hi.astype(jnp.float32)).astype(jnp.bfloat16)
    return (jnp.dot(hi, gsum, preferred_element_type=jnp.float32)
            + jnp.dot(lo, gsum, preferred_element_type=jnp.float32))


def _x_specs(xs, tm, n_lat_tiles, offsets=(0, 0), tile_of=lambda i: i):
    d = xs[0].shape[1]
    if len(xs) == 1:
        return [pl.BlockSpec((tm, d), lambda i: (tile_of(i), 0))]
    lat0, ctx0 = offsets
    return [pl.BlockSpec((tm, d), lambda i: (lat0 + jnp.minimum(tile_of(i), n_lat_tiles - 1), 0)),
            pl.BlockSpec((tm, d), lambda i: (ctx0 + jnp.maximum(tile_of(i) - n_lat_tiles, 0), 0))]


def _load_x(x_refs, n_lat_tiles, rows=slice(None), tile=None):
    if len(x_refs) == 1:
        return x_refs[0][rows, :]
    tile = pl.program_id(0) if tile is None else tile
    return jnp.where(tile < n_lat_tiles, x_refs[0][rows, :], x_refs[1][rows, :])


def _moe_mix(y0_ref, y1_ref, rw_ref, rows):
    rw = rw_ref[rows, :]
    w0 = rw[:, 0:1]
    w1 = rw[:, 1:2]
    a_lo, a_hi = _unpack_pairs(y0_ref[rows, :])
    b_lo, b_hi = _unpack_pairs(y1_ref[rows, :])
    return jnp.concatenate([a_lo * w0 + b_lo * w1, a_hi * w0 + b_hi * w1], axis=1)


def _inproj_kernel(*refs, n_x, n_lat_tiles, fused_combine):
    x_refs = refs[:n_x]
    refs = refs[n_x:]
    if fused_combine:
        y0_ref, y1_ref, rw_ref, modp_ref = refs[:4]
        refs = refs[4:]
        xo_ref = refs[-1]
        refs = refs[:-1]
    (mod_ref, g_ref, w_ref, gsum_ref, qg_ref, kg_ref, cos_ref, sin_ref,
     qa_ref, kta_ref, va_ref, hgl_ref, qc_ref, ktc_ref, vc_ref) = refs
    tm = qa_ref.shape[0]
    tr = min(tm, ROW_CHAIN)
    bf = jnp.bfloat16
    lane = lax.broadcasted_iota(jnp.int32, (tr, LANES), 1)
    first16 = (lane % 32) < 16
    gsum = gsum_ref[...]
    qg = qg_ref[...]
    kg = kg_ref[...]
    scale = g_ref[...] * (1.0 + mod_ref[0, 1:2, :])
    shift = mod_ref[0, 0:1, :]
    ones = jnp.ones((tr, LANES), bf)

    for r0 in range(0, tm, tr):
        rows = slice(r0, r0 + tr)
        x = _load_x(x_refs, n_lat_tiles, rows)
        if fused_combine:
            x = x + modp_ref[0, 5:6, :] * _moe_mix(y0_ref, y1_ref, rw_ref, rows)
            xo_ref[rows, :] = x
        r = lax.rsqrt(jnp.mean(x * x, axis=-1, keepdims=True) + EPS)
        h = (x * r) * scale + shift
        p = jnp.dot(h.astype(bf), w_ref[...], preferred_element_type=jnp.float32)
        cos = cos_ref[rows, :]
        sin = sin_ref[rows, :]

        def blk(off):
            return p[:, off:off + LANES]

        def rope(t):
            sw = jnp.where(first16, pltpu.roll(t, LANES - 16, axis=1), pltpu.roll(t, 16, axis=1))
            return t * cos + sw * sin

        def norm(t, g):
            return t * lax.rsqrt(_head_mean_sq(t, gsum) + EPS) * g

        def store_kv(kt_ref, v_ref, k_blk, v_blk):
            kt_ref[:, rows] = k_blk.T.astype(bf)
            v_ref[rows, 0:LANES] = v_blk.astype(bf)
            v_ref[rows, LANES:2 * LANES] = ones

        for i in range(A_Q // LANES):
            t = rope(norm(blk(OFF_AQ + i * LANES), qg)) * Q_SCALE
            qa_ref[rows, i * LANES:(i + 1) * LANES] = t.astype(bf)
        store_kv(kta_ref, va_ref, rope(norm(blk(OFF_AK), kg)), blk(OFF_AV))
        for i in range(B_CH // LANES):
            a = blk(OFF_BU + i * LANES)
            gt = blk(OFF_BU + B_CH + i * LANES)
            hgl_ref[rows, i * LANES:(i + 1) * LANES] = a * jax.nn.sigmoid(gt)
        for i in range(C_Q // LANES):
            t = rope(blk(OFF_CQ + i * LANES)) * Q_SCALE
            qc_ref[rows, i * LANES:(i + 1) * LANES] = t.astype(bf)
        store_kv(ktc_ref, vc_ref, rope(blk(OFF_CK)), blk(OFF_CV))


def _inproj(xs, mods, g1, w_in_bf, gsum, qg, kg, cos_t, sin_t, *, n, tm, n_lat_tiles, s, nb, x_offsets=(0, 0),
            combine=None):
    d = xs[0].shape[1]
    s_tiles = s // tm
    n_tiles = n // tm

    def bidx(i):
        return jnp.where(i < n_lat_tiles, (i * tm) // s, nb)

    def ridx(i):
        return jnp.where(i < n_lat_tiles, i % s_tiles, s_tiles)

    row = lambda w: pl.BlockSpec((tm, w), lambda i: (i, 0))
    ktspec = pl.BlockSpec((KT_ROWS, tm), lambda i: (0, i))
    const = lambda a: pl.BlockSpec(a.shape, lambda i: (0,) * a.ndim)
    bf = jnp.bfloat16
    modspec = pl.BlockSpec((1, N_MOD, d), lambda i: (bidx(i), 0, 0))
    in_specs = _x_specs(xs, tm, n_lat_tiles, x_offsets)
    args = list(xs)
    out_specs = [row(A_Q), ktspec, row(V_W), row(B_CH), row(C_Q), ktspec, row(V_W)]
    out_shape = [jax.ShapeDtypeStruct((n, A_Q), bf), jax.ShapeDtypeStruct((KT_ROWS, n), bf),
                 jax.ShapeDtypeStruct((n, V_W), bf), jax.ShapeDtypeStruct((n, B_CH), jnp.float32),
                 jax.ShapeDtypeStruct((n, C_Q), bf), jax.ShapeDtypeStruct((KT_ROWS, n), bf),
                 jax.ShapeDtypeStruct((n, V_W), bf)]
    aliases = {}
    if combine is not None:
        y, rw, mods_prev = combine
        assert len(xs) == 1 and y.shape[0] == 2 * n
        in_specs += [row(d // 2), pl.BlockSpec((tm, d // 2), lambda i: (i + n_tiles, 0)), row(LANES), modspec]
        args += [y, y, rw, mods_prev]
        out_specs.append(row(d))
        out_shape.append(jax.ShapeDtypeStruct((n, d), jnp.float32))
        aliases = {0: len(out_shape) - 1}
    in_specs += [modspec, const(g1), const(w_in_bf), const(gsum), const(qg), const(kg),
                 pl.BlockSpec((tm, LANES), lambda i: (ridx(i), 0)),
                 pl.BlockSpec((tm, LANES), lambda i: (ridx(i), 0))]
    args += [mods, g1, w_in_bf, gsum, qg, kg, cos_t, sin_t]
    outs = pl.pallas_call(
        functools.partial(_inproj_kernel, n_x=len(xs), n_lat_tiles=n_lat_tiles,
                          fused_combine=combine is not None),
        grid=(n_tiles,),
        in_specs=in_specs,
        out_specs=out_specs,
        out_shape=out_shape,
        input_output_aliases=aliases,
        compiler_params=_cparams("arbitrary"),
        name="inproj",
    )(*args)
    if combine is not None:
        return (outs[-1],) + tuple(outs[:-1])
    return (None,) + tuple(outs)


def _stack_heads(q_ref, tq):
    lane = lax.broadcasted_iota(jnp.int32, (tq, LANES), 1)
    lo = lane < HEAD_DIM
    qb = [q_ref[:, i * LANES:(i + 1) * LANES] for i in range(3)]
    zero = jnp.zeros_like(qb[0])
    s0 = jnp.concatenate([jnp.where(lo, t, zero) for t in qb], axis=0)
    s1 = jnp.concatenate([jnp.where(lo, zero, t) for t in qb], axis=0)
    return s0, s1, lo


def _unstack_store(o_ref, o0, o1, lo, tq):
    for i in range(3):
        rows = slice(i * tq, (i + 1) * tq)
        o_ref[:, i * LANES:(i + 1) * LANES] = jnp.where(lo, o0[rows], o1[rows]).astype(jnp.bfloat16)


def _krow(kv):
    del kv
    return slice(0, KT_ROWS)


def _vcol(kv):
    del kv
    return slice(0, V_W)


def _row_max(scores):
    m = None
    for t in scores:
        for c in range(0, t.shape[1], LANES):
            blk = t[:, c:c + LANES]
            m = blk if m is None else jnp.maximum(m, blk)
    return m.max(axis=-1, keepdims=True)


def _softmax_pv(scores, values, extra=None):
    m = _row_max(scores)
    if extra is not None:
        m = jnp.maximum(m, extra)
    acc = None
    for t, v in zip(scores, values):
        c = jnp.dot(jnp.exp2(t - m).astype(jnp.bfloat16), v, preferred_element_type=jnp.float32)
        acc = c if acc is None else acc + c
    den = acc[:, LANES:2 * LANES]
    if extra is not None:
        den = den + jnp.exp2(extra - m)
    return acc[:, 0:LANES] / den


ATTN_CHAIN_ROWS = 128


def _qk(q, kt):
    return jnp.dot(q, kt, preferred_element_type=jnp.float32)


def _attn_a_kernel(q_ref, ktl_ref, ktc_ref, vl_ref, vc_ref, *rest, tq, n_q):
    conv_refs = rest[:7]
    o_ref, ob_ref, sh_ref = rest[7:]
    j = pl.program_id(1)
    conv_steps = _conv_steps(conv_refs, ob_ref, sh_ref, j > 0, j < n_q - 1, tq)
    s0, s1, lo = _stack_heads(q_ref, tq)
    n_chains = 2 * (3 * tq // ATTN_CHAIN_ROWS)
    chain = 0
    conv_done = 0
    outs = []
    for kv, qs in enumerate((s0, s1)):
        krow = _krow(kv)
        vcol = _vcol(kv)
        parts = []
        for r0 in range(0, 3 * tq, ATTN_CHAIN_ROWS):
            qr = qs[r0:r0 + ATTN_CHAIN_ROWS]
            ss = [_qk(qr, ktl_ref[krow, :]), _qk(qr, ktc_ref[krow, :])]
            parts.append(_softmax_pv(ss, [vl_ref[:, vcol], vc_ref[:, vcol]]))
            chain += 1
            while conv_done < chain * len(conv_steps) // n_chains:
                conv_steps[conv_done]()
                conv_done += 1
        outs.append(jnp.concatenate(parts, axis=0))
    _unstack_store(o_ref, outs[0], outs[1], lo, tq)


def _kv_specs(nb, s, l):
    ctx0 = nb * s // l
    return [pl.BlockSpec((KT_ROWS, s), lambda b, j: (0, b)),
            pl.BlockSpec((KT_ROWS, l), lambda b, j: (0, ctx0 + b)),
            pl.BlockSpec((s, V_W), lambda b, j: (b, 0)),
            pl.BlockSpec((l, V_W), lambda b, j: (ctx0 + b, 0))]


def _attn_a_conv(qa, kt, v, hgl, conv_params, *, nb, s, l, tq):
    n = qa.shape[0]
    n_q = s // tq
    conv_specs, conv_blk = _conv_specs(n, conv_params, chunks=n_q, base=0, tc=tq)
    qspec = pl.BlockSpec((tq, A_Q), lambda b, j: (b * n_q + j, 0))
    return pl.pallas_call(
        functools.partial(_attn_a_kernel, tq=tq, n_q=n_q),
        grid=(nb, n_q),
        in_specs=[qspec] + _kv_specs(nb, s, l) + conv_specs,
        out_specs=[qspec, conv_blk],
        out_shape=[jax.ShapeDtypeStruct((n, A_Q), jnp.bfloat16), jax.ShapeDtypeStruct((n, B_CH), jnp.bfloat16)],
        scratch_shapes=[pltpu.VMEM((SUBLANES, tq + 2 * CONV_HALO, B_CH), jnp.float32)],
        compiler_params=_cparams("arbitrary", "arbitrary"),
        name="attn_global_conv",
    )(qa, kt, kt, v, v, hgl, hgl, hgl, *conv_params)


WIN_BLOCKS = 8


def _sink_column(sink_ref, kv, rows):
    return jnp.concatenate([jnp.full((rows, 1), sink_ref[3 * kv + g], jnp.float32) for g in range(3)], axis=0)


def _attn_c_kernel(sink_ref, bias_ref, q_ref, ktl_ref, ktc_ref, vl_ref, vc_ref, o_ref, *, s, blocks):
    j = pl.program_id(1)
    tq = WINDOW
    band = 3 * WINDOW
    bf = jnp.bfloat16
    f32 = jnp.float32
    m3 = 3 * tq
    stacks = [_stack_heads(q_ref.at[blk * tq:(blk + 1) * tq, :], tq) for blk in range(blocks)]
    lo = stacks[0][2]
    starts = [pl.multiple_of(jnp.clip((j * blocks + blk - 1) * WINDOW, 0, s - band), WINDOW)
              for blk in range(blocks)]
    outs = [[None, None] for _ in range(blocks)]
    for kv in range(C_KV_HEADS):
        krow = _krow(kv)
        vcol = _vcol(kv)
        sk = _sink_column(sink_ref, kv, tq)
        sc_all = _qk(jnp.concatenate([st[kv] for st in stacks], axis=0), ktc_ref[krow, :])
        accs, pcs, ms = [], [], []
        for blk in range(blocks):
            start = starts[blk]
            sl = _qk(stacks[blk][kv], ktl_ref[krow, pl.ds(start, band)]) + bias_ref[j * blocks + blk - start // WINDOW]
            sc = sc_all[blk * m3:(blk + 1) * m3]
            m = jnp.maximum(_row_max([sl, sc]), sk)
            accs.append(jnp.dot(jnp.exp2(sl - m).astype(bf), vl_ref[pl.ds(start, band), vcol],
                                preferred_element_type=f32))
            pcs.append(jnp.exp2(sc - m).astype(bf))
            ms.append(m)
        acc_ctx = jnp.dot(jnp.concatenate(pcs, axis=0), vc_ref[:, vcol], preferred_element_type=f32)
        for blk in range(blocks):
            acc = accs[blk] + acc_ctx[blk * m3:(blk + 1) * m3]
            den = acc[:, LANES:2 * LANES] + jnp.exp2(sk - ms[blk])
            outs[blk][kv] = acc[:, 0:LANES] / den
    for blk in range(blocks):
        _unstack_store(o_ref.at[blk * tq:(blk + 1) * tq, :], outs[blk][0], outs[blk][1], lo, tq)


def _window_bias():
    r = np.arange(3 * WINDOW)[:, None] % WINDOW
    col = np.arange(3 * WINDOW)[None, :]
    tabs = [np.where(np.abs(col - r - WINDOW * off) <= WINDOW, 0.0, -np.inf) for off in range(3)]
    return jnp.asarray(np.stack(tabs), jnp.float32)


def _attn_c(qc, kt, v, sink2, *, nb, s, l):
    n = qc.shape[0]
    blocks = _pick(s // WINDOW, (WIN_BLOCKS, 8, 4, 2, 1))
    tq = blocks * WINDOW
    n_q = s // tq
    bias = _window_bias()
    return pl.pallas_call(
        functools.partial(_attn_c_kernel, s=s, blocks=blocks),
        grid=(nb, n_q),
        in_specs=[pl.BlockSpec(memory_space=pltpu.SMEM),
                  pl.BlockSpec(bias.shape, lambda b, j: (0, 0, 0)),
                  pl.BlockSpec((tq, C_Q), lambda b, j: (b * n_q + j, 0))] + _kv_specs(nb, s, l),
        out_specs=pl.BlockSpec((tq, C_Q), lambda b, j: (b * n_q + j, 0)),
        out_shape=jax.ShapeDtypeStruct((n, C_Q), jnp.bfloat16),
        compiler_params=_cparams("arbitrary", "arbitrary"),
        name="attn_window",
    )(sink2, bias, qc, kt, kt, v, v)


def _attn_ctx_kernel(sink_ref, qa_ref, kta_ref, va_ref, qc_ref, ktc_ref, vc_ref, oa_in, oc_in, oa_ref, oc_ref, *, l):
    del oa_in, oc_in
    for q_ref, kt_ref, v_ref, o_ref, with_sink in ((qa_ref, kta_ref, va_ref, oa_ref, False),
                                                   (qc_ref, ktc_ref, vc_ref, oc_ref, True)):
        s0, s1, lo = _stack_heads(q_ref, l)
        outs = []
        for kv, qs in enumerate((s0, s1)):
            sk = _sink_column(sink_ref, kv, l) if with_sink else None
            outs.append(_softmax_pv([_qk(qs, kt_ref[_krow(kv), :])], [v_ref[:, _vcol(kv)]], sk))
        _unstack_store(o_ref, outs[0], outs[1], lo, l)


def _attn_ctx(qa, kta, va, qc, ktc, vc, sink2, oa, oc, *, nb, s, l):
    base = nb * s // l
    row = lambda w: pl.BlockSpec((l, w), lambda b: (base + b, 0))
    ktspec = pl.BlockSpec((KT_ROWS, l), lambda b: (0, base + b))
    anyspec = pl.BlockSpec(memory_space=pl.ANY)
    return pl.pallas_call(
        functools.partial(_attn_ctx_kernel, l=l),
        grid=(nb,),
        in_specs=[pl.BlockSpec(memory_space=pltpu.SMEM), row(A_Q), ktspec, row(V_W), row(C_Q), ktspec, row(V_W),
                  anyspec, anyspec],
        out_specs=[row(A_Q), row(C_Q)],
        out_shape=[jax.ShapeDtypeStruct(oa.shape, oa.dtype), jax.ShapeDtypeStruct(oc.shape, oc.dtype)],
        input_output_aliases={7: 0, 8: 1},
        compiler_params=_cparams("arbitrary"),
        name="attn_context",
    )(sink2, qa, kta, va, qc, ktc, vc, oa, oc)


CONV_HALO = 16
CONV_ROWS = 64


def _conv_steps(conv_refs, o_ref, sh_ref, has_prev, has_next, tc):
    prev_ref, cur_ref, next_ref, w_ref, b_ref, g_ref, beta_ref = conv_refs
    rows = tc + 2 * CONV_HALO
    sh_ref[0, 0:CONV_HALO, :] = jnp.where(has_prev, prev_ref[...], 0.0)
    sh_ref[0, CONV_HALO:CONV_HALO + tc, :] = cur_ref[...]
    sh_ref[0, CONV_HALO + tc:rows, :] = jnp.where(has_next, next_ref[...], 0.0)
    for b in range(1, SUBLANES):
        sh_ref[b, 0:rows - SUBLANES, :] = sh_ref[0, b:b + rows - SUBLANES, :]
    base = CONV_HALO - CONV_W // 2

    def step(r0):
        acc = None
        for k in range(CONV_W):
            a, b = divmod(base + k, SUBLANES)
            term = sh_ref[b, SUBLANES * a + r0:SUBLANES * a + r0 + CONV_ROWS, :] * w_ref[k:k + 1, :]
            acc = term if acc is None else acc + term
        hc = acc + b_ref[...]
        mu = jnp.mean(hc, axis=-1, keepdims=True)
        xc = hc - mu
        var = jnp.mean(xc * xc, axis=-1, keepdims=True)
        y = xc * lax.rsqrt(var + EPS) * g_ref[...] + beta_ref[...]
        o_ref[r0:r0 + CONV_ROWS, :] = (y * jax.nn.sigmoid(y)).astype(o_ref.dtype)

    return [functools.partial(step, r0) for r0 in range(0, tc, CONV_ROWS)]


def _conv_kernel(*refs, chunks, tc):
    conv_refs = refs[:7]
    o_ref, sh_ref = refs[-2:]
    j = pl.program_id(1)
    for step in _conv_steps(conv_refs, o_ref, sh_ref, j > 0, j < chunks - 1, tc):
        step()


def _conv_specs(n, params, *, chunks, base, tc):
    hb = tc // CONV_HALO
    last_halo = n // CONV_HALO - 1
    idx = lambda b_, j: base + b_ * chunks + j
    blk = pl.BlockSpec((tc, B_CH), lambda b_, j: (idx(b_, j), 0))
    prev_halo = pl.BlockSpec((CONV_HALO, B_CH), lambda b_, j: (jnp.maximum(idx(b_, j) * hb - 1, 0), 0))
    next_halo = pl.BlockSpec((CONV_HALO, B_CH), lambda b_, j: (jnp.minimum((idx(b_, j) + 1) * hb, last_halo), 0))
    const = lambda a: pl.BlockSpec(a.shape, lambda b_, j: (0,) * a.ndim)
    return [prev_halo, blk, next_halo] + [const(a) for a in params], blk


def _conv(hgl, w, b, g, beta, *, nb, seq, base_rows, tc, prev_out=None):
    n = hgl.shape[0]
    chunks = seq // tc
    in_specs, blk = _conv_specs(n, (w, b, g, beta), chunks=chunks, base=base_rows // tc, tc=tc)
    args = [hgl, hgl, hgl, w, b, g, beta]
    aliases = {}
    if prev_out is not None:
        in_specs.append(pl.BlockSpec(memory_space=pl.ANY))
        args.append(prev_out)
        aliases = {len(args) - 1: 0}
    return pl.pallas_call(
        functools.partial(_conv_kernel, chunks=chunks, tc=tc),
        grid=(nb, chunks),
        in_specs=in_specs,
        out_specs=blk,
        out_shape=jax.ShapeDtypeStruct((n, B_CH), jnp.bfloat16),
        scratch_shapes=[pltpu.VMEM((SUBLANES, tc + 2 * CONV_HALO, B_CH), jnp.float32)],
        input_output_aliases=aliases,
        compiler_params=_cparams("arbitrary", "arbitrary"),
        name="conformer_conv",
    )(*args)


META_ROWS = 8


def _outproj_kernel(*refs, n_x, n_lat_tiles, n_tiles):
    x_refs = refs[:n_x]
    (oa_ref, ob_ref, oc_ref, mod_ref, g_ref, w_ref, wr_ref, br_ref, tri_ref,
     xo_ref, hp_ref, meta_ref, rw_ref, cnt_ref, lg_ref) = refs[n_x:]
    i = pl.program_id(0)
    tm = xo_ref.shape[0]
    f32 = jnp.float32

    @pl.when(i == 0)
    def _():
        cnt_ref[...] = jnp.zeros_like(cnt_ref)
        lg_ref[...] = jnp.zeros_like(lg_ref)

    lat = jnp.concatenate([oa_ref[...], ob_ref[...], oc_ref[...]], axis=1)
    mix = jnp.dot(lat, w_ref[...], preferred_element_type=f32)
    x = _load_x(x_refs, n_lat_tiles, tile=jnp.minimum(i, n_tiles - 1)) + mod_ref[0, 2:3, :] * mix
    xo_ref[...] = x
    r = lax.rsqrt(jnp.mean(x * x, axis=-1, keepdims=True) + EPS)
    h = (x * r) * (g_ref[...] * (1.0 + mod_ref[0, 4:5, :])) + mod_ref[0, 3:4, :]
    h_hi = h.astype(jnp.bfloat16)
    hp_ref[...] = _pack_pairs(h_hi)
    h_lo = (h - h_hi.astype(f32)).astype(jnp.bfloat16)
    r_hi = jnp.dot(h_hi, wr_ref[...], preferred_element_type=f32)
    r_lo = jnp.dot(h_lo, wr_ref[:, 0:LANES], preferred_element_type=f32)
    lg = lg_ref[(i + 1) % 2]
    lg_ref[i % 2] = r_hi[:, 0:LANES] + r_hi[:, LANES:2 * LANES] + r_lo + br_ref[...]

    lane = lax.broadcasted_iota(jnp.int32, (tm, LANES), 1).astype(f32)
    big = float(LANES)
    ninf = -jnp.inf
    glog = jnp.where(lane < N_GROUPS, lg, ninf)
    gmax = glog.max(axis=-1, keepdims=True)
    g_val = 1.0 / jnp.exp(glog - gmax).sum(axis=-1, keepdims=True)
    g_idx = jnp.where(glog == gmax, lane, big).min(axis=-1, keepdims=True)
    e_lo = N_GROUPS + EXPERTS_PER_GROUP * g_idx
    el = jnp.where((lane >= e_lo) & (lane < e_lo + EXPERTS_PER_GROUP), lg, ninf)
    v0 = el.max(axis=-1, keepdims=True)
    i0 = jnp.where(el == v0, lane, big).min(axis=-1, keepdims=True)
    el1 = jnp.where(lane == i0, ninf, el)
    v1 = el1.max(axis=-1, keepdims=True)
    i1 = jnp.where(el1 == v1, lane, big).min(axis=-1, keepdims=True)
    t = jnp.exp(v1 - v0)
    w0 = g_val / (1.0 + t)
    w1 = g_val * t / (1.0 + t)
    e0 = i0 - N_GROUPS
    e1 = i1 - N_GROUPS

    cnt = cnt_ref[0:1, :]
    tri = tri_ref[...]
    ranks = []
    for e in (e0, e1):
        oh = lane == e
        ohf = oh.astype(f32)
        pre = jnp.dot(tri, ohf.astype(jnp.bfloat16), preferred_element_type=f32) + cnt
        ranks.append(jnp.where(oh, pre, 0.0).sum(axis=-1, keepdims=True))
        cnt = cnt + ohf.sum(axis=0, keepdims=True)
    cnt_ref[0:1, :] = jnp.where(i > 0, cnt, cnt_ref[0:1, :])
    rw_ref[...] = jnp.where(lane == 0, w0, jnp.where(lane == 1, w1, 0.0))
    rec = jnp.where(lane == 0, e0, jnp.where(lane == 1, e1, jnp.where(lane == 2, ranks[0],
                    jnp.where(lane == 3, ranks[1], jnp.where(lane == 4, w0, jnp.where(lane == 5, w1, 0.0))))))
    meta_ref[...] = rec.T[0:META_ROWS, :]


def _outproj(xs, oa, ob, oc, mods, g2, w_out_bf, wr, br, tri, *, n, tm, n_tiles, n_lat_tiles, s, nb,
             x_offsets=(0, 0)):
    d = xs[0].shape[1]
    rows = n_tiles * tm
    cur = lambda i: jnp.minimum(i, n_tiles - 1)
    prev = lambda i: jnp.maximum(i - 1, 0)

    def bidx(i):
        return jnp.where(cur(i) < n_lat_tiles, (cur(i) * tm) // s, nb)

    row = lambda w: pl.BlockSpec((tm, w), lambda i: (cur(i), 0))
    const = lambda a: pl.BlockSpec(a.shape, lambda i: (0,) * a.ndim)
    return pl.pallas_call(
        functools.partial(_outproj_kernel, n_x=len(xs), n_lat_tiles=n_lat_tiles, n_tiles=n_tiles),
        grid=(n_tiles + 1,),
        in_specs=_x_specs(xs, tm, n_lat_tiles, x_offsets, cur) + [
                  row(A_Q), row(B_CH), row(C_Q),
                  pl.BlockSpec((1, N_MOD, d), lambda i: (bidx(i), 0, 0)),
                  const(g2), const(w_out_bf), const(wr), const(br), const(tri)],
        out_specs=[row(d), row(d // 2),
                   pl.BlockSpec((META_ROWS, tm), lambda i: (0, prev(i))),
                   pl.BlockSpec((tm, LANES), lambda i: (prev(i), 0)),
                   pl.BlockSpec((8, LANES), lambda i: (0, 0))],
        out_shape=[jax.ShapeDtypeStruct((n, d), jnp.float32),
                   jax.ShapeDtypeStruct((rows, d // 2), jnp.int32),
                   jax.ShapeDtypeStruct((META_ROWS, rows), jnp.float32),
                   jax.ShapeDtypeStruct((rows, LANES), jnp.float32),
                   jax.ShapeDtypeStruct((8, LANES), jnp.float32)],
        scratch_shapes=[pltpu.VMEM((2, tm, LANES), jnp.float32)],
        input_output_aliases={0: 0} if len(xs) == 1 else {},
        compiler_params=_cparams("arbitrary"),
        name="outproj_router",
    )(*xs, oa, ob, oc, mods, g2, w_out_bf, wr, br, tri)


def _sc_mesh():
    return plsc.VectorSubcoreMesh(core_axis_name="core", subcore_axis_name="subcore")


def sc_gather_rows(table, idx2):
    r = idx2.shape[1]
    w = table.shape[1]
    assert r % (2 * SC_WINDOW) == 0
    half = r // SC_WINDOW // 2

    @functools.partial(pl.kernel, out_type=jax.ShapeDtypeStruct((r, w), table.dtype), mesh=_sc_mesh())
    def k(x_hbm, i_hbm, o_hbm):
        def body(i_vmem, o_vmem):
            pltpu.sync_copy(x_hbm.at[i_vmem.at[0]], o_vmem)

        pltpu.emit_pipeline(
            body,
            grid=(2, half),
            in_specs=[pl.BlockSpec((1, SC_WINDOW), lambda c, i: (0, c * half + i))],
            out_specs=[pl.BlockSpec((SC_WINDOW, w), lambda c, i: (c * half + i, 0),
                                    pipeline_mode=pl.Buffered(1))],
            core_axis_name=("core", "subcore"),
            dimension_semantics=(pltpu.PARALLEL, pltpu.PARALLEL),
        )(i_hbm, o_hbm)

    return k(table, idx2)


def sc_scatter_rows2(rows, idx2, n_out):
    r, w = rows.shape
    assert idx2.shape == (1, 2 * r) and r % (2 * SC_WINDOW) == 0
    windows = r // SC_WINDOW
    half = windows // 2

    @functools.partial(pl.kernel, out_type=jax.ShapeDtypeStruct((n_out, w), rows.dtype), mesh=_sc_mesh(),
                       scratch_types=[])
    def k(x_hbm, ia_hbm, ib_hbm, o_hbm):
        def body(x_vmem, ia_vmem, ib_vmem):
            pltpu.sync_copy(x_vmem, o_hbm.at[ia_vmem.at[0]])
            pltpu.sync_copy(x_vmem, o_hbm.at[ib_vmem.at[0]])

        pltpu.emit_pipeline(
            body,
            grid=(2, half),
            in_specs=[pl.BlockSpec((SC_WINDOW, w), lambda c, i: (c * half + i, 0),
                                   pipeline_mode=pl.Buffered(1)),
                      pl.BlockSpec((1, SC_WINDOW), lambda c, i: (0, c * half + i)),
                      pl.BlockSpec((1, SC_WINDOW), lambda c, i: (0, windows + c * half + i))],
            out_specs=[],
            core_axis_name=("core", "subcore"),
            dimension_semantics=(pltpu.PARALLEL, pltpu.PARALLEL),
        )(x_hbm, ia_hbm, ib_hbm)

    return k(rows, idx2, idx2)


def _expert_kernel(te_ref, nv_ref, x_ref, wg_ref, wu_ref, wd_ref, o_ref, wgb_ref, wub_ref, wdb_ref):
    t = pl.program_id(0)
    nvalid = nv_ref[t]

    @pl.when((t == 0) | (te_ref[t] != te_ref[jnp.maximum(t - 1, 0)]))
    def _():
        wgb_ref[...] = wg_ref[0].astype(jnp.bfloat16)
        wub_ref[...] = wu_ref[0].astype(jnp.bfloat16)
        wdb_ref[...] = wd_ref[0].astype(jnp.bfloat16)

    @pl.when(nvalid > 0)
    def _():
        rows = lax.broadcasted_iota(jnp.int32, x_ref.shape, 0)
        lo, hi = _unpack_pairs(jnp.where(rows < nvalid, x_ref[...], 0))
        xb = jnp.concatenate([lo, hi], axis=1).astype(jnp.bfloat16)
        g = jnp.dot(xb, wgb_ref[...], preferred_element_type=jnp.float32)
        u = jnp.dot(xb, wub_ref[...], preferred_element_type=jnp.float32)
        a = (g * jax.nn.sigmoid(g) * u).astype(jnp.bfloat16)
        o_ref[...] = _pack_pairs(jnp.dot(a, wdb_ref[...], preferred_element_type=jnp.float32))

    @pl.when(nvalid == 0)
    def _():
        o_ref[...] = jnp.zeros_like(o_ref)


def _experts(buf, tile_expert, tile_nvalid, wg, wu, wd, *, layer):
    rows, wp = buf.shape
    _, _, d, f = wg.shape
    n_tiles = rows // EXPERT_TILE
    grid_spec = pltpu.PrefetchScalarGridSpec(
        num_scalar_prefetch=2,
        grid=(n_tiles,),
        in_specs=[pl.BlockSpec((EXPERT_TILE, wp), lambda t, te, nv: (t, 0)),
                  pl.BlockSpec((None, 1, d, f), lambda t, te, nv: (layer, te[t], 0, 0)),
                  pl.BlockSpec((None, 1, d, f), lambda t, te, nv: (layer, te[t], 0, 0)),
                  pl.BlockSpec((None, 1, f, d), lambda t, te, nv: (layer, te[t], 0, 0))],
        out_specs=pl.BlockSpec((EXPERT_TILE, wp), lambda t, te, nv: (t, 0)),
        scratch_shapes=[pltpu.VMEM((d, f), jnp.bfloat16), pltpu.VMEM((d, f), jnp.bfloat16),
                        pltpu.VMEM((f, d), jnp.bfloat16)],
    )
    return pl.pallas_call(
        _expert_kernel,
        grid_spec=grid_spec,
        out_shape=jax.ShapeDtypeStruct((rows, wp), jnp.int32),
        compiler_params=_cparams("arbitrary"),
        name="expert_ffn",
    )(tile_expert, tile_nvalid, buf, wg, wu, wd)


def _final_kernel(y0_ref, y1_ref, rw_ref, x_ref, mod_ref, fg_ref, o_ref):
    x = x_ref[...] + mod_ref[0, 5:6, :] * _moe_mix(y0_ref, y1_ref, rw_ref, slice(None))
    r = lax.rsqrt(jnp.mean(x * x, axis=-1, keepdims=True) + EPS)
    o_ref[...] = x * r * fg_ref[...]


def _final_kernel_into(y0_ref, y1_ref, rw_ref, x_ref, mod_ref, fg_ref, prev_ref, o_ref):
    del prev_ref
    _final_kernel(y0_ref, y1_ref, rw_ref, x_ref, mod_ref, fg_ref, o_ref)


def _final_combine(y, rw, xall, mods, final_g, *, tm, n_tiles, s, out_rows, out_tile0, prev_out=None):
    d = xall.shape[1]
    row = lambda w: pl.BlockSpec((tm, w), lambda i: (i, 0))
    in_specs = [row(d // 2),
                pl.BlockSpec((tm, d // 2), lambda i: (i + n_tiles, 0)),
                row(LANES), row(d),
                pl.BlockSpec((1, N_MOD, d), lambda i: ((i * tm) // s, 0, 0)),
                pl.BlockSpec(final_g.shape, lambda i: (0, 0))]
    args = [y, y, rw, xall, mods, final_g]
    aliases = {}
    body = _final_kernel
    if prev_out is not None:
        in_specs.append(pl.BlockSpec(memory_space=pl.ANY))
        args.append(prev_out)
        aliases = {len(args) - 1: 0}
        body = _final_kernel_into
    return pl.pallas_call(
        body,
        grid=(n_tiles,),
        in_specs=in_specs,
        out_specs=pl.BlockSpec((tm, d), lambda i: (out_tile0 + i, 0)),
        out_shape=jax.ShapeDtypeStruct((out_rows, d), jnp.float32),
        input_output_aliases=aliases,
        compiler_params=_cparams("arbitrary"),
        name="moe_combine_final",
    )(*args)


def _dest_kernel(ps_ref, meta_ref, o_ref):
    slot = pl.program_id(0)
    e = meta_ref[pl.ds(slot, 1), :]
    d = meta_ref[pl.ds(TOP_K + slot, 1), :]
    for k in range(N_EXPERTS):
        d = d + jnp.where(e == float(k), ps_ref[k], 0.0)
    o_ref[...] = d.astype(jnp.int32)


def _dest_rows(meta, pstarts, tcols):
    n = meta.shape[1]
    nt = n // tcols
    return pl.pallas_call(
        _dest_kernel,
        grid=(TOP_K, nt),
        in_specs=[pl.BlockSpec(memory_space=pltpu.SMEM),
                  pl.BlockSpec((META_ROWS, tcols), lambda k, i: (0, i))],
        out_specs=pl.BlockSpec((1, tcols), lambda k, i: (0, k * nt + i)),
        out_shape=jax.ShapeDtypeStruct((1, TOP_K * n), jnp.int32),
        compiler_params=_cparams("arbitrary", "arbitrary"),
        name="moe_dest",
    )(pstarts.astype(jnp.float32), meta)


def _dispatch_plan(meta, counts, n_rows_buf, tcols):
    cnt = counts[0, :N_EXPERTS].astype(jnp.int32)
    padded = (cnt + EXPERT_TILE - 1) // EXPERT_TILE * EXPERT_TILE
    pends = jnp.cumsum(padded)
    pstarts = pends - padded
    dest = _dest_rows(meta, pstarts, tcols)
    tile_start = jnp.arange(n_rows_buf // EXPERT_TILE, dtype=jnp.int32) * EXPERT_TILE
    te = jnp.sum((tile_start[:, None] >= pends[None, :]).astype(jnp.int32), axis=1)
    te = jnp.minimum(te, N_EXPERTS - 1)
    onehot = te[:, None] == jnp.arange(N_EXPERTS, dtype=jnp.int32)[None, :]
    cnt_te = jnp.sum(jnp.where(onehot, cnt[None, :], 0), axis=1)
    pstart_te = jnp.sum(jnp.where(onehot, pstarts[None, :], 0), axis=1)
    nvalid = jnp.clip(cnt_te - (tile_start - pstart_te), 0, EXPERT_TILE).astype(jnp.int32)
    return dest, te, nvalid


def _permute_heads(w, axis):
    heads = [lax.slice_in_dim(w, HEAD_DIM * h, HEAD_DIM * (h + 1), axis=axis)
             for h in HEAD_PERM[::HEAD_DIM] // HEAD_DIM]
    return jnp.concatenate(heads, axis=axis)


def _rope_tables(s, tm):
    pos = np.arange(s)
    pos_row = jnp.asarray(pos // GRID_W, jnp.float32)
    pos_col = jnp.asarray(pos % GRID_W, jnp.float32)
    n_freq = HEAD_DIM // 4
    inv = ROPE_THETA ** (-jnp.arange(n_freq, dtype=jnp.float32) / n_freq)
    ang_row = pos_row[:, None] * inv
    ang_col = pos_col[:, None] * inv
    ang = jnp.concatenate([ang_row, ang_row, ang_col, ang_col] * (LANES // HEAD_DIM), axis=-1)
    sign = np.where((np.arange(LANES) % 32) < 16, -1.0, 1.0).astype(np.float32)
    cos_t = jnp.concatenate([jnp.cos(ang), jnp.ones((tm, LANES), jnp.float32)], axis=0)
    sin_t = jnp.concatenate([jnp.sin(ang) * sign, jnp.zeros((tm, LANES), jnp.float32)], axis=0)
    return cos_t, sin_t


def kernel(x, c, ctx, c_ctx, norm1_g, norm2_g, w_mod, b_mod, w_in, q_norm_g, k_norm_g, conv_w, conv_b, conv_ln_g, conv_ln_b, sink, w_out, w_group, b_group, w_expert, b_expert, w_gate, w_up, w_down, final_g):
    nb, s, d = x.shape
    l = ctx.shape[1]
    depth = w_in.shape[0]
    assert w_in.shape[2] == D_IN and w_out.shape[1] == D_MIX
    assert s % GRID_W == 0 and s >= 3 * WINDOW and s % WINDOW == 0 and l % WINDOW == 0
    groups = BATCH_GROUPS if nb % BATCH_GROUPS == 0 else 1
    nbg = nb // groups
    n_lat, n_ctx = nbg * s, nbg * l
    tm = _pick(np.gcd(s, n_ctx), (512, 256, 128))
    tq = _pick(s, (512, 256, 128))
    tc = _pick(np.gcd(s, l), (256, 128))
    assert n_lat % l == 0
    bf = jnp.bfloat16
    f32 = jnp.float32

    x2d = x.reshape(nb * s, d)
    ctx2d = ctx.reshape(nb * l, d)
    c_all = jnp.concatenate([c, c_ctx[None, :]], axis=0)
    mods_all = _modulation(c_all, w_mod, b_mod).reshape(depth, nb + 1, N_MOD, d)
    cos_t, sin_t = _rope_tables(s, tm)
    head_id = np.arange(LANES) // HEAD_DIM
    gsum = jnp.asarray((head_id[:, None] == head_id[None, :]) / HEAD_DIM, bf)
    tri = jnp.asarray(np.tril(np.ones((tm, tm), np.float32), -1), bf)
    n_lat_tiles = n_lat // tm

    xs = [(x2d, ctx2d)] * groups
    pending = [None] * groups
    for i in range(depth):
        last = i == depth - 1
        with_ctx = not last
        qg = jnp.tile(q_norm_g[i], LANES // HEAD_DIM)[None, :]
        kg = jnp.tile(k_norm_g[i], LANES // HEAD_DIM)[None, :]
        w_in_bf = jnp.concatenate(
            [_permute_heads(w_in[i][:, OFF_AQ:OFF_AK], 1), w_in[i][:, OFF_AK:OFF_CQ],
             _permute_heads(w_in[i][:, OFF_CQ:OFF_CK], 1), w_in[i][:, OFF_CK:]], axis=1).astype(bf)
        w_out_bf = jnp.concatenate(
            [_permute_heads(w_out[i][0:A_Q], 0), w_out[i][A_Q:A_Q + B_CH],
             _permute_heads(w_out[i][A_Q + B_CH:], 0)], axis=0).astype(bf)
        sink2 = sink[i] * LOG2E
        conv_args = (conv_w[i].reshape(CONV_W, B_CH), conv_b[i][None, :], conv_ln_g[i][None, :],
                     conv_ln_b[i][None, :])
        wr32 = jnp.zeros((d, LANES), f32).at[:, :N_GROUPS].set(w_group[i])
        wr32 = wr32.at[:, N_GROUPS:N_GROUPS + N_EXPERTS].set(w_expert[i])
        wr_hi = wr32.astype(bf)
        wr = jnp.concatenate([wr_hi, (wr32 - wr_hi.astype(f32)).astype(bf)], axis=1)
        br = jnp.zeros((1, LANES), f32).at[0, :N_GROUPS].set(b_group[i])
        br = br.at[0, N_GROUPS:N_GROUPS + N_EXPERTS].set(b_expert[i])
        n_tok = n_lat + n_ctx if with_ctx else n_lat
        n_tiles = n_tok // tm
        n_rows_buf = 2 * n_tok + N_EXPERTS * EXPERT_TILE
        for g in range(groups):
            b0 = g * nbg
            mods = jnp.concatenate([mods_all[i, b0:b0 + nbg], mods_all[i, nb:nb + 1]], axis=0)
            x_offsets = (b0 * s // tm, b0 * l // tm)
            x_new, qa, kta, va, hgl, qc, ktc, vc = _inproj(
                xs[g], mods, norm1_g[i][None, :], w_in_bf, gsum, qg, kg, cos_t, sin_t, n=n_lat + n_ctx,
                tm=tm, n_lat_tiles=n_lat_tiles, s=s, nb=nbg, x_offsets=x_offsets, combine=pending[g])
            if pending[g] is not None:
                xs[g] = (x_new,)
            oa, ob = _attn_a_conv(qa, kta, va, hgl, conv_args, nb=nbg, s=s, l=l, tq=tq)
            oc = _attn_c(qc, ktc, vc, sink2, nb=nbg, s=s, l=l)
            if with_ctx:
                oa, oc = _attn_ctx(qa, kta, va, qc, ktc, vc, sink2, oa, oc, nb=nbg, s=s, l=l)
                ob = _conv(hgl, *conv_args, nb=nbg, seq=l, base_rows=n_lat, tc=tc, prev_out=ob)
            xall, hp, meta, rw, counts = _outproj(xs[g], oa, ob, oc, mods, norm2_g[i][None, :], w_out_bf,
                                                  wr, br, tri, n=n_lat + n_ctx, tm=tm, n_tiles=n_tiles,
                                                  n_lat_tiles=n_lat_tiles, s=s, nb=nbg, x_offsets=x_offsets)
            dest, te, nvalid = _dispatch_plan(meta, counts, n_rows_buf,
                                              _pick(n_tok, (8192, 4096, 2048, 1024, 512, 256, 128)))
            buf = sc_scatter_rows2(hp, dest, n_rows_buf)
            eo = _experts(buf, te, nvalid, w_gate, w_up, w_down, layer=i)
            y = sc_gather_rows(eo, dest)
            xs[g] = (xall,)
            pending[g] = (y, rw, mods)
    out = None
    for g in range(groups):
        y, rw, mods = pending[g]
        out = _final_combine(y, rw, xs[g][0], mods, final_g[None, :], tm=tm, n_tiles=n_lat_tiles, s=s,
                             out_rows=nb * s, out_tile0=g * n_lat_tiles, prev_out=out)
    return out.reshape(nb, s, d)
```

```python
import functools

import jax
import jax.numpy as jnp
import numpy as np
from jax import lax
from jax.experimental import pallas as pl
from jax.experimental.pallas import tpu as pltpu
from jax.experimental.pallas import tpu_sc as plsc

HEAD_DIM = 64
GRID_W = 64
ROPE_THETA = 10000.0
A_HEADS, A_KV_HEADS = 6, 2
C_HEADS, C_KV_HEADS = 6, 2
B_CH = 256
CONV_W = 31
WINDOW = 128
N_GROUPS = 4
EXPERTS_PER_GROUP = 8
N_EXPERTS = N_GROUPS * EXPERTS_PER_GROUP
TOP_K = 2
N_MOD = 6
EPS = 1e-6
ATTN_SCALE = HEAD_DIM ** -0.5
LOG2E = 1.4426950408889634
Q_SCALE = ATTN_SCALE * LOG2E

A_Q = A_HEADS * HEAD_DIM
A_KV = A_KV_HEADS * HEAD_DIM
C_Q = C_HEADS * HEAD_DIM
C_KV = C_KV_HEADS * HEAD_DIM
D_MIX = A_Q + B_CH + C_Q
OFF_AQ = 0
OFF_AK = OFF_AQ + A_Q
OFF_AV = OFF_AK + A_KV
OFF_BU = OFF_AV + A_KV
OFF_CQ = OFF_BU + 2 * B_CH
OFF_CK = OFF_CQ + C_Q
OFF_CV = OFF_CK + C_KV
D_IN = OFF_CV + C_KV

LANES = 128
SUBLANES = 8
KT_ROWS = LANES
V_W = 2 * LANES
HEAD_PERM = np.concatenate([np.arange(HEAD_DIM) + HEAD_DIM * h for b in range(3) for h in (b, b + 3)])
EXPERT_TILE = 512
SC_WINDOW = 128
VMEM_LIMIT = 56 * 1024 * 1024
HI_MASK = -65536
BATCH_GROUPS = 2
ROW_CHAIN = 256


def _cparams(*sem):
    return pltpu.CompilerParams(dimension_semantics=sem, vmem_limit_bytes=VMEM_LIMIT)


def _pick(n, cands):
    for c in cands:
        if n % c == 0:
            return c
    raise ValueError(f"no tile in {cands} divides {n}")


def _pack_pairs(x):
    w = x.shape[1] // 2
    lo = lax.bitcast_convert_type(x[:, :w].astype(jnp.bfloat16).astype(jnp.float32), jnp.int32)
    hi = lax.bitcast_convert_type(x[:, w:].astype(jnp.bfloat16).astype(jnp.float32), jnp.int32)
    return (hi & HI_MASK) | lax.shift_right_logical(lo, 16)


def _unpack_pairs(p):
    lo = lax.bitcast_convert_type(lax.shift_left(p, 16), jnp.float32)
    hi = lax.bitcast_convert_type(p & HI_MASK, jnp.float32)
    return lo, hi


def _mod_kernel(c_ref, w_ref, b_ref, o_ref):
    c = c_ref[...]
    a = c * jax.nn.sigmoid(c)
    o_ref[0] = jnp.dot(a, w_ref[0], preferred_element_type=jnp.float32,
                       precision=lax.Precision.HIGHEST) + b_ref[0]


def _modulation(c_all, w_mod, b_mod):
    depth, d, n = w_mod.shape
    r = c_all.shape[0]
    tn = _pick(n, (1024, 512, 256, 128))
    return pl.pallas_call(
        _mod_kernel,
        grid=(depth, n // tn),
        in_specs=[pl.BlockSpec((r, d), lambda l, j: (0, 0)),
                  pl.BlockSpec((1, d, tn), lambda l, j: (l, 0, j)),
                  pl.BlockSpec((1, 1, tn), lambda l, j: (l, 0, j))],
        out_specs=pl.BlockSpec((1, r, tn), lambda l, j: (l, 0, j)),
        out_shape=jax.ShapeDtypeStruct((depth, r, n), jnp.float32),
        compiler_params=_cparams("arbitrary", "arbitrary"),
        name="modulation",
    )(c_all, w_mod, b_mod.reshape(depth, 1, n))


def _head_mean_sq(blk, gsum):
    sq = blk * blk
    hi = sq.astype(jnp.bfloat16)
    lo = (sq - hi.astype(jnp.float32)).astype(jnp.bfloat16)
    return (jnp.dot(hi, gsum, preferred_element_type=jnp.float32)
            + jnp.dot(lo, gsum, preferred_element_type=jnp.float32))


def _x_specs(xs, tm, n_lat_tiles, offsets=(0, 0), tile_of=lambda i: i):
    d = xs[0].shape[1]
    if len(xs) == 1:
        return [pl.BlockSpec((tm, d), lambda i: (tile_of(i), 0))]
    lat0, ctx0 = offsets
    return [pl.BlockSpec((tm, d), lambda i: (lat0 + jnp.minimum(tile_of(i), n_lat_tiles - 1), 0)),
            pl.BlockSpec((tm, d), lambda i: (ctx0 + jnp.maximum(tile_of(i) - n_lat_tiles, 0), 0))]


def _load_x(x_refs, n_lat_tiles, rows=slice(None), tile=None):
    if len(x_refs) == 1:
        return x_refs[0][rows, :]
    tile = pl.program_id(0) if tile is None else tile
    return jnp.where(tile < n_lat_tiles, x_refs[0][rows, :], x_refs[1][rows, :])


def _moe_mix(y0_ref, y1_ref, rw_ref, rows):
    rw = rw_ref[rows, :]
    w0 = rw[:, 0:1]
    w1 = rw[:, 1:2]
    a_lo, a_hi = _unpack_pairs(y0_ref[rows, :])
    b_lo, b_hi = _unpack_pairs(y1_ref[rows, :])
    return jnp.concatenate([a_lo * w0 + b_lo * w1, a_hi * w0 + b_hi * w1], axis=1)


def _inproj_kernel(*refs, n_x, n_lat_tiles, fused_combine):
    x_refs = refs[:n_x]
    refs = refs[n_x:]
    if fused_combine:
        y0_ref, y1_ref, rw_ref, modp_ref = refs[:4]
        refs = refs[4:]
        xo_ref = refs[-1]
        refs = refs[:-1]
    (mod_ref, g_ref, w_ref, gsum_ref, qg_ref, kg_ref, cos_ref, sin_ref,
     qa_ref, kta_ref, va_ref, hgl_ref, qc_ref, ktc_ref, vc_ref) = refs
    tm = qa_ref.shape[0]
    tr = min(tm, ROW_CHAIN)
    bf = jnp.bfloat16
    lane = lax.broadcasted_iota(jnp.int32, (tr, LANES), 1)
    first16 = (lane % 32) < 16
    gsum = gsum_ref[...]
    qg = qg_ref[...]
    kg = kg_ref[...]
    scale = g_ref[...] * (1.0 + mod_ref[0, 1:2, :])
    shift = mod_ref[0, 0:1, :]
    ones = jnp.ones((tr, LANES), bf)

    for r0 in range(0, tm, tr):
        rows = slice(r0, r0 + tr)
        x = _load_x(x_refs, n_lat_tiles, rows)
        if fused_combine:
            x = x + modp_ref[0, 5:6, :] * _moe_mix(y0_ref, y1_ref, rw_ref, rows)
            xo_ref[rows, :] = x
        r = lax.rsqrt(jnp.mean(x * x, axis=-1, keepdims=True) + EPS)
        h = (x * r) * scale + shift
        p = jnp.dot(h.astype(bf), w_ref[...], preferred_element_type=jnp.float32)
        cos = cos_ref[rows, :]
        sin = sin_ref[rows, :]

        def blk(off):
            return p[:, off:off + LANES]

        def rope(t):
            sw = jnp.where(first16, pltpu.roll(t, LANES - 16, axis=1), pltpu.roll(t, 16, axis=1))
            return t * cos + sw * sin

        def norm(t, g):
            return t * lax.rsqrt(_head_mean_sq(t, gsum) + EPS) * g

        def store_kv(kt_ref, v_ref, k_blk, v_blk):
            kt_ref[:, rows] = k_blk.T.astype(bf)
            v_ref[rows, 0:LANES] = v_blk.astype(bf)
            v_ref[rows, LANES:2 * LANES] = ones

        for i in range(A_Q // LANES):
            t = rope(norm(blk(OFF_AQ + i * LANES), qg)) * Q_SCALE
            qa_ref[rows, i * LANES:(i + 1) * LANES] = t.astype(bf)
        store_kv(kta_ref, va_ref, rope(norm(blk(OFF_AK), kg)), blk(OFF_AV))
        for i in range(B_CH // LANES):
            a = blk(OFF_BU + i * LANES)
            gt = blk(OFF_BU + B_CH + i * LANES)
            hgl_ref[rows, i * LANES:(i + 1) * LANES] = a * jax.nn.sigmoid(gt)
        for i in range(C_Q // LANES):
            t = rope(blk(OFF_CQ + i * LANES)) * Q_SCALE
            qc_ref[rows, i * LANES:(i + 1) * LANES] = t.astype(bf)
        store_kv(ktc_ref, vc_ref, rope(blk(OFF_CK)), blk(OFF_CV))


def _inproj(xs, mods, g1, w_in_bf, gsum, qg, kg, cos_t, sin_t, *, n, tm, n_lat_tiles, s, nb, x_offsets=(0, 0),
            combine=None):
    d = xs[0].shape[1]
    s_tiles = s // tm
    n_tiles = n // tm

    def bidx(i):
        return jnp.where(i < n_lat_tiles, (i * tm) // s, nb)

    def ridx(i):
        return jnp.where(i < n_lat_tiles, i % s_tiles, s_tiles)

    row = lambda w: pl.BlockSpec((tm, w), lambda i: (i, 0))
    ktspec = pl.BlockSpec((KT_ROWS, tm), lambda i: (0, i))
    const = lambda a: pl.BlockSpec(a.shape, lambda i: (0,) * a.ndim)
    bf = jnp.bfloat16
    modspec = pl.BlockSpec((1, N_MOD, d), lambda i: (bidx(i), 0, 0))
    in_specs = _x_specs(xs, tm, n_lat_tiles, x_offsets)
    args = list(xs)
    out_specs = [row(A_Q), ktspec, row(V_W), row(B_CH), row(C_Q), ktspec, row(V_W)]
    out_shape = [jax.ShapeDtypeStruct((n, A_Q), bf), jax.ShapeDtypeStruct((KT_ROWS, n), bf),
                 jax.ShapeDtypeStruct((n, V_W), bf), jax.ShapeDtypeStruct((n, B_CH), jnp.float32),
                 jax.ShapeDtypeStruct((n, C_Q), bf), jax.ShapeDtypeStruct((KT_ROWS, n), bf),
                 jax.ShapeDtypeStruct((n, V_W), bf)]
    aliases = {}
    if combine is not None:
        y, rw, mods_prev = combine
        assert len(xs) == 1 and y.shape[0] == 2 * n
        in_specs += [row(d // 2), pl.BlockSpec((tm, d // 2), lambda i: (i + n_tiles, 0)), row(LANES), modspec]
        args += [y, y, rw, mods_prev]
        out_specs.append(row(d))
        out_shape.append(jax.ShapeDtypeStruct((n, d), jnp.float32))
        aliases = {0: len(out_shape) - 1}
    in_specs += [modspec, const(g1), const(w_in_bf), const(gsum), const(qg), const(kg),
                 pl.BlockSpec((tm, LANES), lambda i: (ridx(i), 0)),
                 pl.BlockSpec((tm, LANES), lambda i: (ridx(i), 0))]
    args += [mods, g1, w_in_bf, gsum, qg, kg, cos_t, sin_t]
    outs = pl.pallas_call(
        functools.partial(_inproj_kernel, n_x=len(xs), n_lat_tiles=n_lat_tiles,
                          fused_combine=combine is not None),
        grid=(n_tiles,),
        in_specs=in_specs,
        out_specs=out_specs,
        out_shape=out_shape,
        input_output_aliases=aliases,
        compiler_params=_cparams("arbitrary"),
        name="inproj",
    )(*args)
    if combine is not None:
        return (outs[-1],) + tuple(outs[:-1])
    return (None,) + tuple(outs)


def _stack_heads(q_ref, tq):
    lane = lax.broadcasted_iota(jnp.int32, (tq, LANES), 1)
    lo = lane < HEAD_DIM
    qb = [q_ref[:, i * LANES:(i + 1) * LANES] for i in range(3)]
    zero = jnp.zeros_like(qb[0])
    s0 = jnp.concatenate([jnp.where(lo, t, zero) for t in qb], axis=0)
    s1 = jnp.concatenate([jnp.where(lo, zero, t) for t in qb], axis=0)
    return s0, s1, lo


def _unstack_store(o_ref, o0, o1, lo, tq):
    for i in range(3):
        rows = slice(i * tq, (i + 1) * tq)
        o_ref[:, i * LANES:(i + 1) * LANES] = jnp.where(lo, o0[rows], o1[rows]).astype(jnp.bfloat16)


def _krow(kv):
    del kv
    return slice(0, KT_ROWS)


def _vcol(kv):
    del kv
    return slice(0, V_W)


def _row_max(scores):
    m = None
    for t in scores:
        for c in range(0, t.shape[1], LANES):
            blk = t[:, c:c + LANES]
            m = blk if m is None else jnp.maximum(m, blk)
    return m.max(axis=-1, keepdims=True)


def _softmax_pv(scores, values, extra=None):
    m = _row_max(scores)
    if extra is not None:
        m = jnp.maximum(m, extra)
    acc = None
    for t, v in zip(scores, values):
        c = jnp.dot(jnp.exp2(t - m).astype(jnp.bfloat16), v, preferred_element_type=jnp.float32)
        acc = c if acc is None else acc + c
    den = acc[:, LANES:2 * LANES]
    if extra is not None:
        den = den + jnp.exp2(extra - m)
    return acc[:, 0:LANES] / den


ATTN_CHAIN_ROWS = 128


def _qk(q, kt):
    return jnp.dot(q, kt, preferred_element_type=jnp.float32)


def _attn_a_kernel(q_ref, ktl_ref, ktc_ref, vl_ref, vc_ref, *rest, tq, n_q):
    conv_refs = rest[:7]
    o_ref, ob_ref, sh_ref = rest[7:]
    j = pl.program_id(1)
    conv_steps = _conv_steps(conv_refs, ob_ref, sh_ref, j > 0, j < n_q - 1, tq)
    s0, s1, lo = _stack_heads(q_ref, tq)
    n_chains = 2 * (3 * tq // ATTN_CHAIN_ROWS)
    chain = 0
    conv_done = 0
    outs = []
    for kv, qs in enumerate((s0, s1)):
        krow = _krow(kv)
        vcol = _vcol(kv)
        parts = []
        for r0 in range(0, 3 * tq, ATTN_CHAIN_ROWS):
            qr = qs[r0:r0 + ATTN_CHAIN_ROWS]
            ss = [_qk(qr, ktl_ref[krow, :]), _qk(qr, ktc_ref[krow, :])]
            parts.append(_softmax_pv(ss, [vl_ref[:, vcol], vc_ref[:, vcol]]))
            chain += 1
            while conv_done < chain * len(conv_steps) // n_chains:
                conv_steps[conv_done]()
                conv_done += 1
        outs.append(jnp.concatenate(parts, axis=0))
    _unstack_store(o_ref, outs[0], outs[1], lo, tq)


def _kv_specs(nb, s, l):
    ctx0 = nb * s // l
    return [pl.BlockSpec((KT_ROWS, s), lambda b, j: (0, b)),
            pl.BlockSpec((KT_ROWS, l), lambda b, j: (0, ctx0 + b)),
            pl.BlockSpec((s, V_W), lambda b, j: (b, 0)),
            pl.BlockSpec((l, V_W), lambda b, j: (ctx0 + b, 0))]


def _attn_a_conv(qa, kt, v, hgl, conv_params, *, nb, s, l, tq):
    n = qa.shape[0]
    n_q = s // tq
    conv_specs, conv_blk = _conv_specs(n, conv_params, chunks=n_q, base=0, tc=tq)
    qspec = pl.BlockSpec((tq, A_Q), lambda b, j: (b * n_q + j, 0))
    return pl.pallas_call(
        functools.partial(_attn_a_kernel, tq=tq, n_q=n_q),
        grid=(nb, n_q),
        in_specs=[qspec] + _kv_specs(nb, s, l) + conv_specs,
        out_specs=[qspec, conv_blk],
        out_shape=[jax.ShapeDtypeStruct((n, A_Q), jnp.bfloat16), jax.ShapeDtypeStruct((n, B_CH), jnp.bfloat16)],
        scratch_shapes=[pltpu.VMEM((SUBLANES, tq + 2 * CONV_HALO, B_CH), jnp.float32)],
        compiler_params=_cparams("arbitrary", "arbitrary"),
        name="attn_global_conv",
    )(qa, kt, kt, v, v, hgl, hgl, hgl, *conv_params)


WIN_BLOCKS = 8


def _sink_column(sink_ref, kv, rows):
    return jnp.concatenate([jnp.full((rows, 1), sink_ref[3 * kv + g], jnp.float32) for g in range(3)], axis=0)


def _attn_c_kernel(sink_ref, bias_ref, q_ref, ktl_ref, ktc_ref, vl_ref, vc_ref, o_ref, *, s, blocks):
    j = pl.program_id(1)
    tq = WINDOW
    band = 3 * WINDOW
    bf = jnp.bfloat16
    f32 = jnp.float32
    m3 = 3 * tq
    stacks = [_stack_heads(q_ref.at[blk * tq:(blk + 1) * tq, :], tq) for blk in range(blocks)]
    lo = stacks[0][2]
    starts = [pl.multiple_of(jnp.clip((j * blocks + blk - 1) * WINDOW, 0, s - band), WINDOW)
              for blk in range(blocks)]
    outs = [[None, None] for _ in range(blocks)]
    for kv in range(C_KV_HEADS):
        krow = _krow(kv)
        vcol = _vcol(kv)
        sk = _sink_column(sink_ref, kv, tq)
        sc_all = _qk(jnp.concatenate([st[kv] for st in stacks], axis=0), ktc_ref[krow, :])
        accs, pcs, ms = [], [], []
        for blk in range(blocks):
            start = starts[blk]
            sl = _qk(stacks[blk][kv], ktl_ref[krow, pl.ds(start, band)]) + bias_ref[j * blocks + blk - start // WINDOW]
            sc = sc_all[blk * m3:(blk + 1) * m3]
            m = jnp.maximum(_row_max([sl, sc]), sk)
            accs.append(jnp.dot(jnp.exp2(sl - m).astype(bf), vl_ref[pl.ds(start, band), vcol],
                                preferred_element_type=f32))
            pcs.append(jnp.exp2(sc - m).astype(bf))
            ms.append(m)
        acc_ctx = jnp.dot(jnp.concatenate(pcs, axis=0), vc_ref[:, vcol], preferred_element_type=f32)
        for blk in range(blocks):
            acc = accs[blk] + acc_ctx[blk * m3:(blk + 1) * m3]
            den = acc[:, LANES:2 * LANES] + jnp.exp2(sk - ms[blk])
            outs[blk][kv] = acc[:, 0:LANES] / den
    for blk in range(blocks):
        _unstack_store(o_ref.at[blk * tq:(blk + 1) * tq, :], outs[blk][0], outs[blk][1], lo, tq)


def _window_bias():
    r = np.arange(3 * WINDOW)[:, None] % WINDOW
    col = np.arange(3 * WINDOW)[None, :]
    tabs = [np.where(np.abs(col - r - WINDOW * off) <= WINDOW, 0.0, -np.inf) for off in range(3)]
    return jnp.asarray(np.stack(tabs), jnp.float32)


def _attn_c(qc, kt, v, sink2, *, nb, s, l):
    n = qc.shape[0]
    blocks = _pick(s // WINDOW, (WIN_BLOCKS, 8, 4, 2, 1))
    tq = blocks * WINDOW
    n_q = s // tq
    bias = _window_bias()
    return pl.pallas_call(
        functools.partial(_attn_c_kernel, s=s, blocks=blocks),
        grid=(nb, n_q),
        in_specs=[pl.BlockSpec(memory_space=pltpu.SMEM),
                  pl.BlockSpec(bias.shape, lambda b, j: (0, 0, 0)),
                  pl.BlockSpec((tq, C_Q), lambda b, j: (b * n_q + j, 0))] + _kv_specs(nb, s, l),
        out_specs=pl.BlockSpec((tq, C_Q), lambda b, j: (b * n_q + j, 0)),
        out_shape=jax.ShapeDtypeStruct((n, C_Q), jnp.bfloat16),
        compiler_params=_cparams("arbitrary", "arbitrary"),
        name="attn_window",
    )(sink2, bias, qc, kt, kt, v, v)


def _mixers_kernel(*refs, tq, n_q, s):
    sink_ref, bias_ref, qc_ref = refs[0:3]
    kvc_refs = refs[3:7]
    qa_ref = refs[7]
    kva_refs = refs[8:12]
    conv_refs = refs[12:19]
    oa_ref, ob_ref, oc_ref, sh_ref = refs[19:]
    _attn_c_kernel(sink_ref, bias_ref, qc_ref, *kvc_refs, oc_ref, s=s, blocks=tq // WINDOW)
    _attn_a_kernel(qa_ref, *kva_refs, *conv_refs, oa_ref, ob_ref, sh_ref, tq=tq, n_q=n_q)


def _latent_mixers(qa, kta, va, qc, ktc, vc, sink2, hgl, conv_params, *, nb, s, l, tq):
    n = qa.shape[0]
    n_q = s // tq
    bias = _window_bias()
    conv_specs, conv_blk = _conv_specs(n, conv_params, chunks=n_q, base=0, tc=tq)
    qspec = lambda w: pl.BlockSpec((tq, w), lambda b, j: (b * n_q + j, 0))
    bf = jnp.bfloat16
    return pl.pallas_call(
        functools.partial(_mixers_kernel, tq=tq, n_q=n_q, s=s),
        grid=(nb, n_q),
        in_specs=[pl.BlockSpec(memory_space=pltpu.SMEM), pl.BlockSpec(bias.shape, lambda b, j: (0, 0, 0)),
                  qspec(C_Q)] + _kv_specs(nb, s, l) + [qspec(A_Q)] + _kv_specs(nb, s, l) + conv_specs,
        out_specs=[qspec(A_Q), conv_blk, qspec(C_Q)],
        out_shape=[jax.ShapeDtypeStruct((n, A_Q), bf), jax.ShapeDtypeStruct((n, B_CH), bf),
                   jax.ShapeDtypeStruct((n, C_Q), bf)],
        scratch_shapes=[pltpu.VMEM((SUBLANES, tq + 2 * CONV_HALO, B_CH), jnp.float32)],
        compiler_params=_cparams("arbitrary", "arbitrary"),
        name="latent_mixers",
    )(sink2, bias, qc, ktc, ktc, vc, vc, qa, kta, kta, va, va, hgl, hgl, hgl, *conv_params)


def _attn_ctx_kernel(sink_ref, qa_ref, kta_ref, va_ref, qc_ref, ktc_ref, vc_ref, oa_in, oc_in, oa_ref, oc_ref, *, l):
    del oa_in, oc_in
    for q_ref, kt_ref, v_ref, o_ref, with_sink in ((qa_ref, kta_ref, va_ref, oa_ref, False),
                                                   (qc_ref, ktc_ref, vc_ref, oc_ref, True)):
        s0, s1, lo = _stack_heads(q_ref, l)
        outs = []
        for kv, qs in enumerate((s0, s1)):
            sk = _sink_column(sink_ref, kv, l) if with_sink else None
            outs.append(_softmax_pv([_qk(qs, kt_ref[_krow(kv), :])], [v_ref[:, _vcol(kv)]], sk))
        _unstack_store(o_ref, outs[0], outs[1], lo, l)


def _attn_ctx(qa, kta, va, qc, ktc, vc, sink2, oa, oc, *, nb, s, l):
    base = nb * s // l
    row = lambda w: pl.BlockSpec((l, w), lambda b: (base + b, 0))
    ktspec = pl.BlockSpec((KT_ROWS, l), lambda b: (0, base + b))
    anyspec = pl.BlockSpec(memory_space=pl.ANY)
    return pl.pallas_call(
        functools.partial(_attn_ctx_kernel, l=l),
        grid=(nb,),
        in_specs=[pl.BlockSpec(memory_space=pltpu.SMEM), row(A_Q), ktspec, row(V_W), row(C_Q), ktspec, row(V_W),
                  anyspec, anyspec],
        out_specs=[row(A_Q), row(C_Q)],
        out_shape=[jax.ShapeDtypeStruct(oa.shape, oa.dtype), jax.ShapeDtypeStruct(oc.shape, oc.dtype)],
        input_output_aliases={7: 0, 8: 1},
        compiler_params=_cparams("arbitrary"),
        name="attn_context",
    )(sink2, qa, kta, va, qc, ktc, vc, oa, oc)


CONV_HALO = 16
CONV_ROWS = 64


def _conv_steps(conv_refs, o_ref, sh_ref, has_prev, has_next, tc):
    prev_ref, cur_ref, next_ref, w_ref, b_ref, g_ref, beta_ref = conv_refs
    rows = tc + 2 * CONV_HALO
    sh_ref[0, 0:CONV_HALO, :] = jnp.where(has_prev, prev_ref[...], 0.0)
    sh_ref[0, CONV_HALO:CONV_HALO + tc, :] = cur_ref[...]
    sh_ref[0, CONV_HALO + tc:rows, :] = jnp.where(has_next, next_ref[...], 0.0)
    for b in range(1, SUBLANES):
        sh_ref[b, 0:rows - SUBLANES, :] = sh_ref[0, b:b + rows - SUBLANES, :]
    base = CONV_HALO - CONV_W // 2

    def step(r0):
        acc = None
        for k in range(CONV_W):
            a, b = divmod(base + k, SUBLANES)
            term = sh_ref[b, SUBLANES * a + r0:SUBLANES * a + r0 + CONV_ROWS, :] * w_ref[k:k + 1, :]
            acc = term if acc is None else acc + term
        hc = acc + b_ref[...]
        mu = jnp.mean(hc, axis=-1, keepdims=True)
        xc = hc - mu
        var = jnp.mean(xc * xc, axis=-1, keepdims=True)
        y = xc * lax.rsqrt(var + EPS) * g_ref[...] + beta_ref[...]
        o_ref[r0:r0 + CONV_ROWS, :] = (y * jax.nn.sigmoid(y)).astype(o_ref.dtype)

    return [functools.partial(step, r0) for r0 in range(0, tc, CONV_ROWS)]


def _conv_kernel(*refs, chunks, tc):
    conv_refs = refs[:7]
    o_ref, sh_ref = refs[-2:]
    j = pl.program_id(1)
    for step in _conv_steps(conv_refs, o_ref, sh_ref, j > 0, j < chunks - 1, tc):
        step()


def _conv_specs(n, params, *, chunks, base, tc):
    hb = tc // CONV_HALO
    last_halo = n // CONV_HALO - 1
    idx = lambda b_, j: base + b_ * chunks + j
    blk = pl.BlockSpec((tc, B_CH), lambda b_, j: (idx(b_, j), 0))
    prev_halo = pl.BlockSpec((CONV_HALO, B_CH), lambda b_, j: (jnp.maximum(idx(b_, j) * hb - 1, 0), 0))
    next_halo = pl.BlockSpec((CONV_HALO, B_CH), lambda b_, j: (jnp.minimum((idx(b_, j) + 1) * hb, last_halo), 0))
    const = lambda a: pl.BlockSpec(a.shape, lambda b_, j: (0,) * a.ndim)
    return [prev_halo, blk, next_halo] + [const(a) for a in params], blk


def _conv(hgl, w, b, g, beta, *, nb, seq, base_rows, tc, prev_out=None):
    n = hgl.shape[0]
    chunks = seq // tc
    in_specs, blk = _conv_specs(n, (w, b, g, beta), chunks=chunks, base=base_rows // tc, tc=tc)
    args = [hgl, hgl, hgl, w, b, g, beta]
    aliases = {}
    if prev_out is not None:
        in_specs.append(pl.BlockSpec(memory_space=pl.ANY))
        args.append(prev_out)
        aliases = {len(args) - 1: 0}
    return pl.pallas_call(
        functools.partial(_conv_kernel, chunks=chunks, tc=tc),
        grid=(nb, chunks),
        in_specs=in_specs,
        out_specs=blk,
        out_shape=jax.ShapeDtypeStruct((n, B_CH), jnp.bfloat16),
        scratch_shapes=[pltpu.VMEM((SUBLANES, tc + 2 * CONV_HALO, B_CH), jnp.float32)],
        input_output_aliases=aliases,
        compiler_params=_cparams("arbitrary", "arbitrary"),
        name="conformer_conv",
    )(*args)


META_ROWS = 8


def _outproj_kernel(*refs, n_x, n_lat_tiles, n_tiles):
    x_refs = refs[:n_x]
    (oa_ref, ob_ref, oc_ref, mod_ref, g_ref, w_ref, wr_ref, br_ref, tri_ref,
     xo_ref, hp_ref, meta_ref, rw_ref, cnt_ref, lg_ref) = refs[n_x:]
    i = pl.program_id(0)
    tm = xo_ref.shape[0]
    f32 = jnp.float32

    @pl.when(i == 0)
    def _():
        cnt_ref[...] = jnp.zeros_like(cnt_ref)
        lg_ref[...] = jnp.zeros_like(lg_ref)

    lat = jnp.concatenate([oa_ref[...], ob_ref[...], oc_ref[...]], axis=1)
    mix = jnp.dot(lat, w_ref[...], preferred_element_type=f32)
    x = _load_x(x_refs, n_lat_tiles, tile=jnp.minimum(i, n_tiles - 1)) + mod_ref[0, 2:3, :] * mix
    xo_ref[...] = x
    r = lax.rsqrt(jnp.mean(x * x, axis=-1, keepdims=True) + EPS)
    h = (x * r) * (g_ref[...] * (1.0 + mod_ref[0, 4:5, :])) + mod_ref[0, 3:4, :]
    h_hi = h.astype(jnp.bfloat16)
    hp_ref[...] = _pack_pairs(h_hi)
    h_lo = (h - h_hi.astype(f32)).astype(jnp.bfloat16)
    r_hi = jnp.dot(h_hi, wr_ref[...], preferred_element_type=f32)
    r_lo = jnp.dot(h_lo, wr_ref[:, 0:LANES], preferred_element_type=f32)
    lg = lg_ref[(i + 1) % 2]
    lg_ref[i % 2] = r_hi[:, 0:LANES] + r_hi[:, LANES:2 * LANES] + r_lo + br_ref[...]

    lane = lax.broadcasted_iota(jnp.int32, (tm, LANES), 1).astype(f32)
    big = float(LANES)
    ninf = -jnp.inf
    glog = jnp.where(lane < N_GROUPS, lg, ninf)
    gmax = glog.max(axis=-1, keepdims=True)
    g_val = 1.0 / jnp.exp(glog - gmax).sum(axis=-1, keepdims=True)
    g_idx = jnp.where(glog == gmax, lane, big).min(axis=-1, keepdims=True)
    e_lo = N_GROUPS + EXPERTS_PER_GROUP * g_idx
    el = jnp.where((lane >= e_lo) & (lane < e_lo + EXPERTS_PER_GROUP), lg, ninf)
    v0 = el.max(axis=-1, keepdims=True)
    i0 = jnp.where(el == v0, lane, big).min(axis=-1, keepdims=True)
    el1 = jnp.where(lane == i0, ninf, el)
    v1 = el1.max(axis=-1, keepdims=True)
    i1 = jnp.where(el1 == v1, lane, big).min(axis=-1, keepdims=True)
    t = jnp.exp(v1 - v0)
    w0 = g_val / (1.0 + t)
    w1 = g_val * t / (1.0 + t)
    e0 = i0 - N_GROUPS
    e1 = i1 - N_GROUPS

    cnt = cnt_ref[0:1, :]
    tri = tri_ref[...]
    ranks = []
    for e in (e0, e1):
        oh = lane == e
        ohf = oh.astype(f32)
        pre = jnp.dot(tri, ohf.astype(jnp.bfloat16), preferred_element_type=f32) + cnt
        ranks.append(jnp.where(oh, pre, 0.0).sum(axis=-1, keepdims=True))
        cnt = cnt + ohf.sum(axis=0, keepdims=True)
    cnt_ref[0:1, :] = jnp.where(i > 0, cnt, cnt_ref[0:1, :])
    rw_ref[...] = jnp.where(lane == 0, w0, jnp.where(lane == 1, w1, 0.0))
    rec = jnp.where(lane == 0, e0, jnp.where(lane == 1, e1, jnp.where(lane == 2, ranks[0],
                    jnp.where(lane == 3, ranks[1], jnp.where(lane == 4, w0, jnp.where(lane == 5, w1, 0.0))))))
    meta_ref[...] = rec.T[0:META_ROWS, :]


def _outproj(xs, oa, ob, oc, mods, g2, w_out_bf, wr, br, tri, *, n, tm, n_tiles, n_lat_tiles, s, nb,
             x_offsets=(0, 0)):
    d = xs[0].shape[1]
    rows = n_tiles * tm
    cur = lambda i: jnp.minimum(i, n_tiles - 1)
    prev = lambda i: jnp.maximum(i - 1, 0)

    def bidx(i):
        return jnp.where(cur(i) < n_lat_tiles, (cur(i) * tm) // s, nb)

    row = lambda w: pl.BlockSpec((tm, w), lambda i: (cur(i), 0))
    const = lambda a: pl.BlockSpec(a.shape, lambda i: (0,) * a.ndim)
    return pl.pallas_call(
        functools.partial(_outproj_kernel, n_x=len(xs), n_lat_tiles=n_lat_tiles, n_tiles=n_tiles),
        grid=(n_tiles + 1,),
        in_specs=_x_specs(xs, tm, n_lat_tiles, x_offsets, cur) + [
                  row(A_Q), row(B_CH), row(C_Q),
                  pl.BlockSpec((1, N_MOD, d), lambda i: (bidx(i), 0, 0)),
                  const(g2), const(w_out_bf), const(wr), const(br), const(tri)],
        out_specs=[row(d), row(d // 2),
                   pl.BlockSpec((META_ROWS, tm), lambda i: (0, prev(i))),
                   pl.BlockSpec((tm, LANES), lambda i: (prev(i), 0)),
                   pl.BlockSpec((8, LANES), lambda i: (0, 0))],
        out_shape=[jax.ShapeDtypeStruct((n, d), jnp.float32),
                   jax.ShapeDtypeStruct((rows, d // 2), jnp.int32),
                   jax.ShapeDtypeStruct((META_ROWS, rows), jnp.float32),
                   jax.ShapeDtypeStruct((rows, LANES), jnp.float32),
                   jax.ShapeDtypeStruct((8, LANES), jnp.float32)],
        scratch_shapes=[pltpu.VMEM((2, tm, LANES), jnp.float32)],
        input_output_aliases={0: 0} if len(xs) == 1 else {},
        compiler_params=_cparams("arbitrary"),
        name="outproj_router",
    )(*xs, oa, ob, oc, mods, g2, w_out_bf, wr, br, tri)


def _sc_mesh():
    return plsc.VectorSubcoreMesh(core_axis_name="core", subcore_axis_name="subcore")


def sc_gather_rows(table, idx2):
    r = idx2.shape[1]
    w = table.shape[1]
    assert r % (2 * SC_WINDOW) == 0
    half = r // SC_WINDOW // 2

    @functools.partial(pl.kernel, out_type=jax.ShapeDtypeStruct((r, w), table.dtype), mesh=_sc_mesh())
    def k(x_hbm, i_hbm, o_hbm):
        def body(i_vmem, o_vmem):
            pltpu.sync_copy(x_hbm.at[i_vmem.at[0]], o_vmem)

        pltpu.emit_pipeline(
            body,
            grid=(2, half),
            in_specs=[pl.BlockSpec((1, SC_WINDOW), lambda c, i: (0, c * half + i))],
            out_specs=[pl.BlockSpec((SC_WINDOW, w), lambda c, i: (c * half + i, 0),
                                    pipeline_mode=pl.Buffered(1))],
            core_axis_name=("core", "subcore"),
            dimension_semantics=(pltpu.PARALLEL, pltpu.PARALLEL),
        )(i_hbm, o_hbm)

    return k(table, idx2)


def sc_scatter_rows2(rows, idx2, n_out):
    r, w = rows.shape
    assert idx2.shape == (1, 2 * r) and r % (2 * SC_WINDOW) == 0
    windows = r // SC_WINDOW
    half = windows // 2

    @functools.partial(pl.kernel, out_type=jax.ShapeDtypeStruct((n_out, w), rows.dtype), mesh=_sc_mesh(),
                       scratch_types=[])
    def k(x_hbm, ia_hbm, ib_hbm, o_hbm):
        def body(x_vmem, ia_vmem, ib_vmem):
            pltpu.sync_copy(x_vmem, o_hbm.at[ia_vmem.at[0]])
            pltpu.sync_copy(x_vmem, o_hbm.at[ib_vmem.at[0]])

        pltpu.emit_pipeline(
            body,
            grid=(2, half),
            in_specs=[pl.BlockSpec((SC_WINDOW, w), lambda c, i: (c * half + i, 0),
                                   pipeline_mode=pl.Buffered(1)),
                      pl.BlockSpec((1, SC_WINDOW), lambda c, i: (0, c * half + i)),
                      pl.BlockSpec((1, SC_WINDOW), lambda c, i: (0, windows + c * half + i))],
            out_specs=[],
            core_axis_name=("core", "subcore"),
            dimension_semantics=(pltpu.PARALLEL, pltpu.PARALLEL),
        )(x_hbm, ia_hbm, ib_hbm)

    return k(rows, idx2, idx2)


def _expert_kernel(te_ref, nv_ref, x_ref, wg_ref, wu_ref, wd_ref, o_ref, wgb_ref, wub_ref, wdb_ref):
    t = pl.program_id(0)
    nvalid = nv_ref[t]

    @pl.when((t == 0) | (te_ref[t] != te_ref[jnp.maximum(t - 1, 0)]))
    def _():
        wgb_ref[...] = wg_ref[0].astype(jnp.bfloat16)
        wub_ref[...] = wu_ref[0].astype(jnp.bfloat16)
        wdb_ref[...] = wd_ref[0].astype(jnp.bfloat16)

    @pl.when(nvalid > 0)
    def _():
        rows = lax.broadcasted_iota(jnp.int32, x_ref.shape, 0)
        lo, hi = _unpack_pairs(jnp.where(rows < nvalid, x_ref[...], 0))
        xb = jnp.concatenate([lo, hi], axis=1).astype(jnp.bfloat16)
        g = jnp.dot(xb, wgb_ref[...], preferred_element_type=jnp.float32)
        u = jnp.dot(xb, wub_ref[...], preferred_element_type=jnp.float32)
        a = (g * jax.nn.sigmoid(g) * u).astype(jnp.bfloat16)
        o_ref[...] = _pack_pairs(jnp.dot(a, wdb_ref[...], preferred_element_type=jnp.float32))

    @pl.when(nvalid == 0)
    def _():
        o_ref[...] = jnp.zeros_like(o_ref)


def _experts(buf, tile_expert, tile_nvalid, wg, wu, wd, *, layer):
    rows, wp = buf.shape
    _, _, d, f = wg.shape
    n_tiles = rows // EXPERT_TILE
    grid_spec = pltpu.PrefetchScalarGridSpec(
        num_scalar_prefetch=2,
        grid=(n_tiles,),
        in_specs=[pl.BlockSpec((EXPERT_TILE, wp), lambda t, te, nv: (t, 0)),
                  pl.BlockSpec((None, 1, d, f), lambda t, te, nv: (layer, te[t], 0, 0)),
                  pl.BlockSpec((None, 1, d, f), lambda t, te, nv: (layer, te[t], 0, 0)),
                  pl.BlockSpec((None, 1, f, d), lambda t, te, nv: (layer, te[t], 0, 0))],
        out_specs=pl.BlockSpec((EXPERT_TILE, wp), lambda t, te, nv: (t, 0)),
        scratch_shapes=[pltpu.VMEM((d, f), jnp.bfloat16), pltpu.VMEM((d, f), jnp.bfloat16),
                        pltpu.VMEM((f, d), jnp.bfloat16)],
    )
    return pl.pallas_call(
        _expert_kernel,
        grid_spec=grid_spec,
        out_shape=jax.ShapeDtypeStruct((rows, wp), jnp.int32),
        compiler_params=_cparams("arbitrary"),
        name="expert_ffn",
    )(tile_expert, tile_nvalid, buf, wg, wu, wd)


def _final_kernel(y0_ref, y1_ref, rw_ref, x_ref, mod_ref, fg_ref, o_ref):
    x = x_ref[...] + mod_ref[0, 5:6, :] * _moe_mix(y0_ref, y1_ref, rw_ref, slice(None))
    r = lax.rsqrt(jnp.mean(x * x, axis=-1, keepdims=True) + EPS)
    o_ref[...] = x * r * fg_ref[...]


def _final_kernel_into(y0_ref, y1_ref, rw_ref, x_ref, mod_ref, fg_ref, prev_ref, o_ref):
    del prev_ref
    _final_kernel(y0_ref, y1_ref, rw_ref, x_ref, mod_ref, fg_ref, o_ref)


def _final_combine(y, rw, xall, mods, final_g, *, tm, n_tiles, s, out_rows, out_tile0, prev_out=None):
    d = xall.shape[1]
    row = lambda w: pl.BlockSpec((tm, w), lambda i: (i, 0))
    in_specs = [row(d // 2),
                pl.BlockSpec((tm, d // 2), lambda i: (i + n_tiles, 0)),
                row(LANES), row(d),
                pl.BlockSpec((1, N_MOD, d), lambda i: ((i * tm) // s, 0, 0)),
                pl.BlockSpec(final_g.shape, lambda i: (0, 0))]
    args = [y, y, rw, xall, mods, final_g]
    aliases = {}
    body = _final_kernel
    if prev_out is not None:
        in_specs.append(pl.BlockSpec(memory_space=pl.ANY))
        args.append(prev_out)
        aliases = {len(args) - 1: 0}
        body = _final_kernel_into
    return pl.pallas_call(
        body,
        grid=(n_tiles,),
        in_specs=in_specs,
        out_specs=pl.BlockSpec((tm, d), lambda i: (out_tile0 + i, 0)),
        out_shape=jax.ShapeDtypeStruct((out_rows, d), jnp.float32),
        input_output_aliases=aliases,
        compiler_params=_cparams("arbitrary"),
        name="moe_combine_final",
    )(*args)


def _dest_kernel(ps_ref, meta_ref, o_ref):
    slot = pl.program_id(0)
    e = meta_ref[pl.ds(slot, 1), :]
    d = meta_ref[pl.ds(TOP_K + slot, 1), :]
    for k in range(N_EXPERTS):
        d = d + jnp.where(e == float(k), ps_ref[k], 0.0)
    o_ref[...] = d.astype(jnp.int32)


def _dest_rows(meta, pstarts, tcols):
    n = meta.shape[1]
    nt = n // tcols
    return pl.pallas_call(
        _dest_kernel,
        grid=(TOP_K, nt),
        in_specs=[pl.BlockSpec(memory_space=pltpu.SMEM),
                  pl.BlockSpec((META_ROWS, tcols), lambda k, i: (0, i))],
        out_specs=pl.BlockSpec((1, tcols), lambda k, i: (0, k * nt + i)),
        out_shape=jax.ShapeDtypeStruct((1, TOP_K * n), jnp.int32),
        compiler_params=_cparams("arbitrary", "arbitrary"),
        name="moe_dest",
    )(pstarts.astype(jnp.float32), meta)


def _dispatch_plan(meta, counts, n_rows_buf, tcols):
    cnt = counts[0, :N_EXPERTS].astype(jnp.int32)
    padded = (cnt + EXPERT_TILE - 1) // EXPERT_TILE * EXPERT_TILE
    pends = jnp.cumsum(padded)
    pstarts = pends - padded
    dest = _dest_rows(meta, pstarts, tcols)
    tile_start = jnp.arange(n_rows_buf // EXPERT_TILE, dtype=jnp.int32) * EXPERT_TILE
    te = jnp.sum((tile_start[:, None] >= pends[None, :]).astype(jnp.int32), axis=1)
    te = jnp.minimum(te, N_EXPERTS - 1)
    onehot = te[:, None] == jnp.arange(N_EXPERTS, dtype=jnp.int32)[None, :]
    cnt_te = jnp.sum(jnp.where(onehot, cnt[None, :], 0), axis=1)
    pstart_te = jnp.sum(jnp.where(onehot, pstarts[None, :], 0), axis=1)
    nvalid = jnp.clip(cnt_te - (tile_start - pstart_te), 0, EXPERT_TILE).astype(jnp.int32)
    return dest, te, nvalid


def _permute_heads(w, axis):
    heads = [lax.slice_in_dim(w, HEAD_DIM * h, HEAD_DIM * (h + 1), axis=axis)
             for h in HEAD_PERM[::HEAD_DIM] // HEAD_DIM]
    return jnp.concatenate(heads, axis=axis)


def _rope_tables(s, tm):
    pos = np.arange(s)
    pos_row = jnp.asarray(pos // GRID_W, jnp.float32)
    pos_col = jnp.asarray(pos % GRID_W, jnp.float32)
    n_freq = HEAD_DIM // 4
    inv = ROPE_THETA ** (-jnp.arange(n_freq, dtype=jnp.float32) / n_freq)
    ang_row = pos_row[:, None] * inv
    ang_col = pos_col[:, None] * inv
    ang = jnp.concatenate([ang_row, ang_row, ang_col, ang_col] * (LANES // HEAD_DIM), axis=-1)
    sign = np.where((np.arange(LANES) % 32) < 16, -1.0, 1.0).astype(np.float32)
    cos_t = jnp.concatenate([jnp.cos(ang), jnp.ones((tm, LANES), jnp.float32)], axis=0)
    sin_t = jnp.concatenate([jnp.sin(ang) * sign, jnp.zeros((tm, LANES), jnp.float32)], axis=0)
    return cos_t, sin_t


def kernel(x, c, ctx, c_ctx, norm1_g, norm2_g, w_mod, b_mod, w_in, q_norm_g, k_norm_g, conv_w, conv_b, conv_ln_g, conv_ln_b, sink, w_out, w_group, b_group, w_expert, b_expert, w_gate, w_up, w_down, final_g):
    nb, s, d = x.shape
    l = ctx.shape[1]
    depth = w_in.shape[0]
    assert w_in.shape[2] == D_IN and w_out.shape[1] == D_MIX
    assert s % GRID_W == 0 and s >= 3 * WINDOW and s % WINDOW == 0 and l % WINDOW == 0
    groups = BATCH_GROUPS if nb % BATCH_GROUPS == 0 else 1
    nbg = nb // groups
    n_lat, n_ctx = nbg * s, nbg * l
    tm = _pick(np.gcd(s, n_ctx), (512, 256, 128))
    tq = _pick(s, (512, 256, 128))
    tc = _pick(np.gcd(s, l), (256, 128))
    assert n_lat % l == 0
    bf = jnp.bfloat16
    f32 = jnp.float32

    x2d = x.reshape(nb * s, d)
    ctx2d = ctx.reshape(nb * l, d)
    c_all = jnp.concatenate([c, c_ctx[None, :]], axis=0)
    mods_all = _modulation(c_all, w_mod, b_mod).reshape(depth, nb + 1, N_MOD, d)
    cos_t, sin_t = _rope_tables(s, tm)
    head_id = np.arange(LANES) // HEAD_DIM
    gsum = jnp.asarray((head_id[:, None] == head_id[None, :]) / HEAD_DIM, bf)
    tri = jnp.asarray(np.tril(np.ones((tm, tm), np.float32), -1), bf)
    n_lat_tiles = n_lat // tm

    xs = [(x2d, ctx2d)] * groups
    pending = [None] * groups
    for i in range(depth):
        last = i == depth - 1
        with_ctx = not last
        qg = jnp.tile(q_norm_g[i], LANES // HEAD_DIM)[None, :]
        kg = jnp.tile(k_norm_g[i], LANES // HEAD_DIM)[None, :]
        w_in_bf = jnp.concatenate(
            [_permute_heads(w_in[i][:, OFF_AQ:OFF_AK], 1), w_in[i][:, OFF_AK:OFF_CQ],
             _permute_heads(w_in[i][:, OFF_CQ:OFF_CK], 1), w_in[i][:, OFF_CK:]], axis=1).astype(bf)
        w_out_bf = jnp.concatenate(
            [_permute_heads(w_out[i][0:A_Q], 0), w_out[i][A_Q:A_Q + B_CH],
             _permute_heads(w_out[i][A_Q + B_CH:], 0)], axis=0).astype(bf)
        sink2 = sink[i] * LOG2E
        conv_args = (conv_w[i].reshape(CONV_W, B_CH), conv_b[i][None, :], conv_ln_g[i][None, :],
                     conv_ln_b[i][None, :])
        wr32 = jnp.zeros((d, LANES), f32).at[:, :N_GROUPS].set(w_group[i])
        wr32 = wr32.at[:, N_GROUPS:N_GROUPS + N_EXPERTS].set(w_expert[i])
        wr_hi = wr32.astype(bf)
        wr = jnp.concatenate([wr_hi, (wr32 - wr_hi.astype(f32)).astype(bf)], axis=1)
        br = jnp.zeros((1, LANES), f32).at[0, :N_GROUPS].set(b_group[i])
        br = br.at[0, N_GROUPS:N_GROUPS + N_EXPERTS].set(b_expert[i])
        n_tok = n_lat + n_ctx if with_ctx else n_lat
        n_tiles = n_tok // tm
        n_rows_buf = 2 * n_tok + N_EXPERTS * EXPERT_TILE
        for g in range(groups):
            b0 = g * nbg
            mods = jnp.concatenate([mods_all[i, b0:b0 + nbg], mods_all[i, nb:nb + 1]], axis=0)
            x_offsets = (b0 * s // tm, b0 * l // tm)
            x_new, qa, kta, va, hgl, qc, ktc, vc = _inproj(
                xs[g], mods, norm1_g[i][None, :], w_in_bf, gsum, qg, kg, cos_t, sin_t, n=n_lat + n_ctx,
                tm=tm, n_lat_tiles=n_lat_tiles, s=s, nb=nbg, x_offsets=x_offsets, combine=pending[g])
            if pending[g] is not None:
                xs[g] = (x_new,)
            oa, ob, oc = _latent_mixers(qa, kta, va, qc, ktc, vc, sink2, hgl, conv_args,
                                        nb=nbg, s=s, l=l, tq=tq)
            if with_ctx:
                oa, oc = _attn_ctx(qa, kta, va, qc, ktc, vc, sink2, oa, oc, nb=nbg, s=s, l=l)
                ob = _conv(hgl, *conv_args, nb=nbg, seq=l, base_rows=n_lat, tc=tc, prev_out=ob)
            xall, hp, meta, rw, counts = _outproj(xs[g], oa, ob, oc, mods, norm2_g[i][None, :], w_out_bf,
                                                  wr, br, tri, n=n_lat + n_ctx, tm=tm, n_tiles=n_tiles,
                                                  n_lat_tiles=n_lat_tiles, s=s, nb=nbg, x_offsets=x_offsets)
            dest, te, nvalid = _dispatch_plan(meta, counts, n_rows_buf,
                                              _pick(n_tok, (8192, 4096, 2048, 1024, 512, 256, 128)))
            buf = sc_scatter_rows2(hp, dest, n_rows_buf)
            eo = _experts(buf, te, nvalid, w_gate, w_up, w_down, layer=i)
            y = sc_gather_rows(eo, dest)
            xs[g] = (xall,)
            pending[g] = (y, rw, mods)
    out = None
    for g in range(groups):
        y, rw, mods = pending[g]
        out = _final_combine(y, rw, xs[g][0], mods, final_g[None, :], tm=tm, n_tiles=n_lat_tiles, s=s,
                             out_rows=nb * s, out_tile0=g * n_lat_tiles, prev_out=out)
    return out.reshape(nb, s, d)
```

```python
import functools

import jax
import jax.numpy as jnp
import numpy as np
from jax import lax
from jax.experimental import pallas as pl
from jax.experimental.pallas import tpu as pltpu
from jax.experimental.pallas import tpu_sc as plsc

HEAD_DIM = 64
GRID_W = 64
ROPE_THETA = 10000.0
A_HEADS, A_KV_HEADS = 6, 2
C_HEADS, C_KV_HEADS = 6, 2
B_CH = 256
CONV_W = 31
WINDOW = 128
N_GROUPS = 4
EXPERTS_PER_GROUP = 8
N_EXPERTS = N_GROUPS * EXPERTS_PER_GROUP
TOP_K = 2
N_MOD = 6
EPS = 1e-6
ATTN_SCALE = HEAD_DIM ** -0.5
LOG2E = 1.4426950408889634
Q_SCALE = ATTN_SCALE * LOG2E

A_Q = A_HEADS * HEAD_DIM
A_KV = A_KV_HEADS * HEAD_DIM
C_Q = C_HEADS * HEAD_DIM
C_KV = C_KV_HEADS * HEAD_DIM
D_MIX = A_Q + B_CH + C_Q
OFF_AQ = 0
OFF_AK = OFF_AQ + A_Q
OFF_AV = OFF_AK + A_KV
OFF_BU = OFF_AV + A_KV
OFF_CQ = OFF_BU + 2 * B_CH
OFF_CK = OFF_CQ + C_Q
OFF_CV = OFF_CK + C_KV
D_IN = OFF_CV + C_KV

LANES = 128
SUBLANES = 8
KT_ROWS = LANES
V_W = 2 * LANES
HEAD_PERM = np.concatenate([np.arange(HEAD_DIM) + HEAD_DIM * h for b in range(3) for h in (b, b + 3)])
EXPERT_TILE = 512
SC_WINDOW = 128
VMEM_LIMIT = 56 * 1024 * 1024
HI_MASK = -65536
BATCH_GROUPS = 2
ROW_CHAIN = 256


def _cparams(*sem):
    return pltpu.CompilerParams(dimension_semantics=sem, vmem_limit_bytes=VMEM_LIMIT)


def _pick(n, cands):
    for c in cands:
        if n % c == 0:
            return c
    raise ValueError(f"no tile in {cands} divides {n}")


def _pack_pairs(x):
    w = x.shape[1] // 2
    lo = lax.bitcast_convert_type(x[:, :w].astype(jnp.bfloat16).astype(jnp.float32), jnp.int32)
    hi = lax.bitcast_convert_type(x[:, w:].astype(jnp.bfloat16).astype(jnp.float32), jnp.int32)
    return (hi & HI_MASK) | lax.shift_right_logical(lo, 16)


def _unpack_pairs(p):
    lo = lax.bitcast_convert_type(lax.shift_left(p, 16), jnp.float32)
    hi = lax.bitcast_convert_type(p & HI_MASK, jnp.float32)
    return lo, hi


def _mod_kernel(c_ref, w_ref, b_ref, o_ref):
    c = c_ref[...]
    a = c * jax.nn.sigmoid(c)
    o_ref[0] = jnp.dot(a, w_ref[0], preferred_element_type=jnp.float32,
                       precision=lax.Precision.HIGHEST) + b_ref[0]


def _modulation(c_all, w_mod, b_mod):
    depth, d, n = w_mod.shape
    r = c_all.shape[0]
    tn = _pick(n, (1024, 512, 256, 128))
    return pl.pallas_call(
        _mod_kernel,
        grid=(depth, n // tn),
        in_specs=[pl.BlockSpec((r, d), lambda l, j: (0, 0)),
                  pl.BlockSpec((1, d, tn), lambda l, j: (l, 0, j)),
                  pl.BlockSpec((1, 1, tn), lambda l, j: (l, 0, j))],
        out_specs=pl.BlockSpec((1, r, tn), lambda l, j: (l, 0, j)),
        out_shape=jax.ShapeDtypeStruct((depth, r, n), jnp.float32),
        compiler_params=_cparams("arbitrary", "arbitrary"),
        name="modulation",
    )(c_all, w_mod, b_mod.reshape(depth, 1, n))


def _head_mean_sq(blk, gsum):
    sq = blk * blk
    hi = sq.astype(jnp.bfloat16)
    lo = (sq - hi.astype(jnp.float32)).astype(jnp.bfloat16)
    return (jnp.dot(hi, gsum, preferred_element_type=jnp.float32)
            + jnp.dot(lo, gsum, preferred_element_type=jnp.float32))


def _x_specs(xs, tm, n_lat_tiles, offsets=(0, 0), tile_of=lambda i: i):
    d = xs[0].shape[1]
    if len(xs) == 1:
        return [pl.BlockSpec((tm, d), lambda i: (tile_of(i), 0))]
    lat0, ctx0 = offsets
    return [pl.BlockSpec((tm, d), lambda i: (lat0 + jnp.minimum(tile_of(i), n_lat_tiles - 1), 0)),
            pl.BlockSpec((tm, d), lambda i: (ctx0 + jnp.maximum(tile_of(i) - n_lat_tiles, 0), 0))]


def _load_x(x_refs, n_lat_tiles, rows=slice(None), tile=None):
    if len(x_refs) == 1:
        return x_refs[0][rows, :]
    tile = pl.program_id(0) if tile is None else tile
    return jnp.where(tile < n_lat_tiles, x_refs[0][rows, :], x_refs[1][rows, :])


def _moe_mix(y0_ref, y1_ref, rw_ref, rows):
    rw = rw_ref[rows, :]
    w0 = rw[:, 0:1]
    w1 = rw[:, 1:2]
    a_lo, a_hi = _unpack_pairs(y0_ref[rows, :])
    b_lo, b_hi = _unpack_pairs(y1_ref[rows, :])
    return jnp.concatenate([a_lo * w0 + b_lo * w1, a_hi * w0 + b_hi * w1], axis=1)


def _inproj_kernel(*refs, n_x, n_lat_tiles, fused_combine):
    x_refs = refs[:n_x]
    refs = refs[n_x:]
    if fused_combine:
        y0_ref, y1_ref, rw_ref, modp_ref = refs[:4]
        refs = refs[4:]
        xo_ref = refs[-1]
        refs = refs[:-1]
    (mod_ref, g_ref, w_ref, gsum_ref, qg_ref, kg_ref, cos_ref, sin_ref,
     qa_ref, kta_ref, va_ref, hgl_ref, qc_ref, ktc_ref, vc_ref) = refs
    tm = qa_ref.shape[0]
    tr = min(tm, ROW_CHAIN)
    bf = jnp.bfloat16
    lane = lax.broadcasted_iota(jnp.int32, (tr, LANES), 1)
    first16 = (lane % 32) < 16
    gsum = gsum_ref[...]
    qg = qg_ref[...]
    kg = kg_ref[...]
    scale = g_ref[...] * (1.0 + mod_ref[0, 1:2, :])
    shift = mod_ref[0, 0:1, :]
    ones = jnp.ones((tr, LANES), bf)

    for r0 in range(0, tm, tr):
        rows = slice(r0, r0 + tr)
        x = _load_x(x_refs, n_lat_tiles, rows)
        if fused_combine:
            x = x + modp_ref[0, 5:6, :] * _moe_mix(y0_ref, y1_ref, rw_ref, rows)
            xo_ref[rows, :] = x
        r = lax.rsqrt(jnp.mean(x * x, axis=-1, keepdims=True) + EPS)
        h = (x * r) * scale + shift
        p = jnp.dot(h.astype(bf), w_ref[...], preferred_element_type=jnp.float32)
        cos = cos_ref[rows, :]
        sin = sin_ref[rows, :]

        def blk(off):
            return p[:, off:off + LANES]

        def rope(t):
            sw = jnp.where(first16, pltpu.roll(t, LANES - 16, axis=1), pltpu.roll(t, 16, axis=1))
            return t * cos + sw * sin

        def norm(t, g):
            return t * lax.rsqrt(_head_mean_sq(t, gsum) + EPS) * g

        def store_kv(kt_ref, v_ref, k_blk, v_blk):
            kt_ref[:, rows] = k_blk.T.astype(bf)
            v_ref[rows, 0:LANES] = v_blk.astype(bf)
            v_ref[rows, LANES:2 * LANES] = ones

        for i in range(A_Q // LANES):
            t = rope(norm(blk(OFF_AQ + i * LANES), qg)) * Q_SCALE
            qa_ref[rows, i * LANES:(i + 1) * LANES] = t.astype(bf)
        store_kv(kta_ref, va_ref, rope(norm(blk(OFF_AK), kg)), blk(OFF_AV))
        for i in range(B_CH // LANES):
            a = blk(OFF_BU + i * LANES)
            gt = blk(OFF_BU + B_CH + i * LANES)
            hgl_ref[rows, i * LANES:(i + 1) * LANES] = a * jax.nn.sigmoid(gt)
        for i in range(C_Q // LANES):
            t = rope(blk(OFF_CQ + i * LANES)) * Q_SCALE
            qc_ref[rows, i * LANES:(i + 1) * LANES] = t.astype(bf)
        store_kv(ktc_ref, vc_ref, rope(blk(OFF_CK)), blk(OFF_CV))


def _inproj(xs, mods, g1, w_in_bf, gsum, qg, kg, cos_t, sin_t, *, n, tm, n_lat_tiles, s, nb, x_offsets=(0, 0),
            combine=None):
    d = xs[0].shape[1]
    s_tiles = s // tm
    n_tiles = n // tm

    def bidx(i):
        return jnp.where(i < n_lat_tiles, (i * tm) // s, nb)

    def ridx(i):
        return jnp.where(i < n_lat_tiles, i % s_tiles, s_tiles)

    row = lambda w: pl.BlockSpec((tm, w), lambda i: (i, 0))
    ktspec = pl.BlockSpec((KT_ROWS, tm), lambda i: (0, i))
    const = lambda a: pl.BlockSpec(a.shape, lambda i: (0,) * a.ndim)
    bf = jnp.bfloat16
    modspec = pl.BlockSpec((1, N_MOD, d), lambda i: (bidx(i), 0, 0))
    in_specs = _x_specs(xs, tm, n_lat_tiles, x_offsets)
    args = list(xs)
    out_specs = [row(A_Q), ktspec, row(V_W), row(B_CH), row(C_Q), ktspec, row(V_W)]
    out_shape = [jax.ShapeDtypeStruct((n, A_Q), bf), jax.ShapeDtypeStruct((KT_ROWS, n), bf),
                 jax.ShapeDtypeStruct((n, V_W), bf), jax.ShapeDtypeStruct((n, B_CH), jnp.float32),
                 jax.ShapeDtypeStruct((n, C_Q), bf), jax.ShapeDtypeStruct((KT_ROWS, n), bf),
                 jax.ShapeDtypeStruct((n, V_W), bf)]
    aliases = {}
    if combine is not None:
        y, rw, mods_prev = combine
        assert len(xs) == 1 and y.shape[0] == 2 * n
        in_specs += [row(d // 2), pl.BlockSpec((tm, d // 2), lambda i: (i + n_tiles, 0)), row(LANES), modspec]
        args += [y, y, rw, mods_prev]
        out_specs.append(row(d))
        out_shape.append(jax.ShapeDtypeStruct((n, d), jnp.float32))
        aliases = {0: len(out_shape) - 1}
    in_specs += [modspec, const(g1), const(w_in_bf), const(gsum), const(qg), const(kg),
                 pl.BlockSpec((tm, LANES), lambda i: (ridx(i), 0)),
                 pl.BlockSpec((tm, LANES), lambda i: (ridx(i), 0))]
    args += [mods, g1, w_in_bf, gsum, qg, kg, cos_t, sin_t]
    outs = pl.pallas_call(
        functools.partial(_inproj_kernel, n_x=len(xs), n_lat_tiles=n_lat_tiles,
                          fused_combine=combine is not None),
        grid=(n_tiles,),
        in_specs=in_specs,
        out_specs=out_specs,
        out_shape=out_shape,
        input_output_aliases=aliases,
        compiler_params=_cparams("arbitrary"),
        name="inproj",
    )(*args)
    if combine is not None:
        return (outs[-1],) + tuple(outs[:-1])
    return (None,) + tuple(outs)


def _stack_heads(q_ref, tq):
    lane = lax.broadcasted_iota(jnp.int32, (tq, LANES), 1)
    lo = lane < HEAD_DIM
    qb = [q_ref[:, i * LANES:(i + 1) * LANES] for i in range(3)]
    zero = jnp.zeros_like(qb[0])
    s0 = jnp.concatenate([jnp.where(lo, t, zero) for t in qb], axis=0)
    s1 = jnp.concatenate([jnp.where(lo, zero, t) for t in qb], axis=0)
    return s0, s1, lo


def _unstack_store(o_ref, o0, o1, lo, tq):
    for i in range(3):
        rows = slice(i * tq, (i + 1) * tq)
        o_ref[:, i * LANES:(i + 1) * LANES] = jnp.where(lo, o0[rows], o1[rows]).astype(jnp.bfloat16)


def _krow(kv):
    del kv
    return slice(0, KT_ROWS)


def _vcol(kv):
    del kv
    return slice(0, V_W)


def _row_max(scores):
    m = None
    for t in scores:
        for c in range(0, t.shape[1], LANES):
            blk = t[:, c:c + LANES]
            m = blk if m is None else jnp.maximum(m, blk)
    return m.max(axis=-1, keepdims=True)


def _softmax_pv(scores, values, extra=None):
    m = _row_max(scores)
    if extra is not None:
        m = jnp.maximum(m, extra)
    acc = None
    for t, v in zip(scores, values):
        c = jnp.dot(jnp.exp2(t - m).astype(jnp.bfloat16), v, preferred_element_type=jnp.float32)
        acc = c if acc is None else acc + c
    den = acc[:, LANES:2 * LANES]
    if extra is not None:
        den = den + jnp.exp2(extra - m)
    return acc[:, 0:LANES] / den


ATTN_CHAIN_ROWS = 128


def _qk(q, kt):
    return jnp.dot(q, kt, preferred_element_type=jnp.float32)


def _attn_a_kernel(q_ref, ktl_ref, ktc_ref, vl_ref, vc_ref, *rest, tq, n_q):
    conv_refs = rest[:7]
    o_ref, ob_ref, sh_ref = rest[7:]
    j = pl.program_id(1)
    conv_steps = _conv_steps(conv_refs, ob_ref, sh_ref, j > 0, j < n_q - 1, tq)
    s0, s1, lo = _stack_heads(q_ref, tq)
    n_chains = 2 * (3 * tq // ATTN_CHAIN_ROWS)
    chain = 0
    conv_done = 0
    outs = []
    for kv, qs in enumerate((s0, s1)):
        krow = _krow(kv)
        vcol = _vcol(kv)
        parts = []
        for r0 in range(0, 3 * tq, ATTN_CHAIN_ROWS):
            qr = qs[r0:r0 + ATTN_CHAIN_ROWS]
            ss = [_qk(qr, ktl_ref[krow, :]), _qk(qr, ktc_ref[krow, :])]
            parts.append(_softmax_pv(ss, [vl_ref[:, vcol], vc_ref[:, vcol]]))
            chain += 1
            while conv_done < chain * len(conv_steps) // n_chains:
                conv_steps[conv_done]()
                conv_done += 1
        outs.append(jnp.concatenate(parts, axis=0))
    _unstack_store(o_ref, outs[0], outs[1], lo, tq)


def _kv_specs(nb, s, l):
    ctx0 = nb * s // l
    return [pl.BlockSpec((KT_ROWS, s), lambda b, j: (0, b)),
            pl.BlockSpec((KT_ROWS, l), lambda b, j: (0, ctx0 + b)),
            pl.BlockSpec((s, V_W), lambda b, j: (b, 0)),
            pl.BlockSpec((l, V_W), lambda b, j: (ctx0 + b, 0))]


def _sink_column(sink_ref, kv, rows):
    return jnp.concatenate([jnp.full((rows, 1), sink_ref[3 * kv + g], jnp.float32) for g in range(3)], axis=0)


def _attn_c_kernel(sink_ref, bias_ref, q_ref, ktl_ref, ktc_ref, vl_ref, vc_ref, o_ref, *, s, blocks):
    j = pl.program_id(1)
    tq = WINDOW
    band = 3 * WINDOW
    bf = jnp.bfloat16
    f32 = jnp.float32
    m3 = 3 * tq
    stacks = [_stack_heads(q_ref.at[blk * tq:(blk + 1) * tq, :], tq) for blk in range(blocks)]
    lo = stacks[0][2]
    starts = [pl.multiple_of(jnp.clip((j * blocks + blk - 1) * WINDOW, 0, s - band), WINDOW)
              for blk in range(blocks)]
    outs = [[None, None] for _ in range(blocks)]
    for kv in range(C_KV_HEADS):
        krow = _krow(kv)
        vcol = _vcol(kv)
        sk = _sink_column(sink_ref, kv, tq)
        sc_all = _qk(jnp.concatenate([st[kv] for st in stacks], axis=0), ktc_ref[krow, :])
        accs, pcs, ms = [], [], []
        for blk in range(blocks):
            start = starts[blk]
            sl = _qk(stacks[blk][kv], ktl_ref[krow, pl.ds(start, band)]) + bias_ref[j * blocks + blk - start // WINDOW]
            sc = sc_all[blk * m3:(blk + 1) * m3]
            m = jnp.maximum(_row_max([sl, sc]), sk)
            accs.append(jnp.dot(jnp.exp2(sl - m).astype(bf), vl_ref[pl.ds(start, band), vcol],
                                preferred_element_type=f32))
            pcs.append(jnp.exp2(sc - m).astype(bf))
            ms.append(m)
        acc_ctx = jnp.dot(jnp.concatenate(pcs, axis=0), vc_ref[:, vcol], preferred_element_type=f32)
        for blk in range(blocks):
            acc = accs[blk] + acc_ctx[blk * m3:(blk + 1) * m3]
            den = acc[:, LANES:2 * LANES] + jnp.exp2(sk - ms[blk])
            outs[blk][kv] = acc[:, 0:LANES] / den
    for blk in range(blocks):
        _unstack_store(o_ref.at[blk * tq:(blk + 1) * tq, :], outs[blk][0], outs[blk][1], lo, tq)


def _window_bias():
    r = np.arange(3 * WINDOW)[:, None] % WINDOW
    col = np.arange(3 * WINDOW)[None, :]
    tabs = [np.where(np.abs(col - r - WINDOW * off) <= WINDOW, 0.0, -np.inf) for off in range(3)]
    return jnp.asarray(np.stack(tabs), jnp.float32)


def _mixers_kernel(*refs, tq, n_q, s):
    sink_ref, bias_ref, qc_ref = refs[0:3]
    kvc_refs = refs[3:7]
    qa_ref = refs[7]
    kva_refs = refs[8:12]
    conv_refs = refs[12:19]
    oa_ref, ob_ref, oc_ref, sh_ref = refs[19:]
    _attn_c_kernel(sink_ref, bias_ref, qc_ref, *kvc_refs, oc_ref, s=s, blocks=tq // WINDOW)
    _attn_a_kernel(qa_ref, *kva_refs, *conv_refs, oa_ref, ob_ref, sh_ref, tq=tq, n_q=n_q)


def _latent_mixers(qa, kta, va, qc, ktc, vc, sink2, hgl, conv_params, *, nb, s, l, tq):
    n = qa.shape[0]
    n_q = s // tq
    bias = _window_bias()
    conv_specs, conv_blk = _conv_specs(n, conv_params, chunks=n_q, base=0, tc=tq)
    qspec = lambda w: pl.BlockSpec((tq, w), lambda b, j: (b * n_q + j, 0))
    bf = jnp.bfloat16
    return pl.pallas_call(
        functools.partial(_mixers_kernel, tq=tq, n_q=n_q, s=s),
        grid=(nb, n_q),
        in_specs=[pl.BlockSpec(memory_space=pltpu.SMEM), pl.BlockSpec(bias.shape, lambda b, j: (0, 0, 0)),
                  qspec(C_Q)] + _kv_specs(nb, s, l) + [qspec(A_Q)] + _kv_specs(nb, s, l) + conv_specs,
        out_specs=[qspec(A_Q), conv_blk, qspec(C_Q)],
        out_shape=[jax.ShapeDtypeStruct((n, A_Q), bf), jax.ShapeDtypeStruct((n, B_CH), bf),
                   jax.ShapeDtypeStruct((n, C_Q), bf)],
        scratch_shapes=[pltpu.VMEM((SUBLANES, tq + 2 * CONV_HALO, B_CH), jnp.float32)],
        compiler_params=_cparams("arbitrary", "arbitrary"),
        name="latent_mixers",
    )(sink2, bias, qc, ktc, ktc, vc, vc, qa, kta, kta, va, va, hgl, hgl, hgl, *conv_params)


def _attn_ctx_kernel(sink_ref, qa_ref, kta_ref, va_ref, qc_ref, ktc_ref, vc_ref, oa_in, oc_in, oa_ref, oc_ref, *, l):
    del oa_in, oc_in
    for q_ref, kt_ref, v_ref, o_ref, with_sink in ((qa_ref, kta_ref, va_ref, oa_ref, False),
                                                   (qc_ref, ktc_ref, vc_ref, oc_ref, True)):
        s0, s1, lo = _stack_heads(q_ref, l)
        outs = []
        for kv, qs in enumerate((s0, s1)):
            sk = _sink_column(sink_ref, kv, l) if with_sink else None
            outs.append(_softmax_pv([_qk(qs, kt_ref[_krow(kv), :])], [v_ref[:, _vcol(kv)]], sk))
        _unstack_store(o_ref, outs[0], outs[1], lo, l)


def _attn_ctx(qa, kta, va, qc, ktc, vc, sink2, oa, oc, *, nb, s, l):
    base = nb * s // l
    row = lambda w: pl.BlockSpec((l, w), lambda b: (base + b, 0))
    ktspec = pl.BlockSpec((KT_ROWS, l), lambda b: (0, base + b))
    anyspec = pl.BlockSpec(memory_space=pl.ANY)
    return pl.pallas_call(
        functools.partial(_attn_ctx_kernel, l=l),
        grid=(nb,),
        in_specs=[pl.BlockSpec(memory_space=pltpu.SMEM), row(A_Q), ktspec, row(V_W), row(C_Q), ktspec, row(V_W),
                  anyspec, anyspec],
        out_specs=[row(A_Q), row(C_Q)],
        out_shape=[jax.ShapeDtypeStruct(oa.shape, oa.dtype), jax.ShapeDtypeStruct(oc.shape, oc.dtype)],
        input_output_aliases={7: 0, 8: 1},
        compiler_params=_cparams("arbitrary"),
        name="attn_context",
    )(sink2, qa, kta, va, qc, ktc, vc, oa, oc)


CONV_HALO = 16
CONV_ROWS = 64


def _conv_steps(conv_refs, o_ref, sh_ref, has_prev, has_next, tc):
    prev_ref, cur_ref, next_ref, w_ref, b_ref, g_ref, beta_ref = conv_refs
    rows = tc + 2 * CONV_HALO
    sh_ref[0, 0:CONV_HALO, :] = jnp.where(has_prev, prev_ref[...], 0.0)
    sh_ref[0, CONV_HALO:CONV_HALO + tc, :] = cur_ref[...]
    sh_ref[0, CONV_HALO + tc:rows, :] = jnp.where(has_next, next_ref[...], 0.0)
    for b in range(1, SUBLANES):
        sh_ref[b, 0:rows - SUBLANES, :] = sh_ref[0, b:b + rows - SUBLANES, :]
    base = CONV_HALO - CONV_W // 2

    def step(r0):
        acc = None
        for k in range(CONV_W):
            a, b = divmod(base + k, SUBLANES)
            term = sh_ref[b, SUBLANES * a + r0:SUBLANES * a + r0 + CONV_ROWS, :] * w_ref[k:k + 1, :]
            acc = term if acc is None else acc + term
        hc = acc + b_ref[...]
        mu = jnp.mean(hc, axis=-1, keepdims=True)
        xc = hc - mu
        var = jnp.mean(xc * xc, axis=-1, keepdims=True)
        y = xc * lax.rsqrt(var + EPS) * g_ref[...] + beta_ref[...]
        o_ref[r0:r0 + CONV_ROWS, :] = (y * jax.nn.sigmoid(y)).astype(o_ref.dtype)

    return [functools.partial(step, r0) for r0 in range(0, tc, CONV_ROWS)]


def _conv_kernel(*refs, chunks, tc):
    conv_refs = refs[:7]
    o_ref, sh_ref = refs[-2:]
    j = pl.program_id(1)
    for step in _conv_steps(conv_refs, o_ref, sh_ref, j > 0, j < chunks - 1, tc):
        step()


def _conv_specs(n, params, *, chunks, base, tc):
    hb = tc // CONV_HALO
    last_halo = n // CONV_HALO - 1
    idx = lambda b_, j: base + b_ * chunks + j
    blk = pl.BlockSpec((tc, B_CH), lambda b_, j: (idx(b_, j), 0))
    prev_halo = pl.BlockSpec((CONV_HALO, B_CH), lambda b_, j: (jnp.maximum(idx(b_, j) * hb - 1, 0), 0))
    next_halo = pl.BlockSpec((CONV_HALO, B_CH), lambda b_, j: (jnp.minimum((idx(b_, j) + 1) * hb, last_halo), 0))
    const = lambda a: pl.BlockSpec(a.shape, lambda b_, j: (0,) * a.ndim)
    return [prev_halo, blk, next_halo] + [const(a) for a in params], blk


def _conv(hgl, w, b, g, beta, *, nb, seq, base_rows, tc, prev_out=None):
    n = hgl.shape[0]
    chunks = seq // tc
    in_specs, blk = _conv_specs(n, (w, b, g, beta), chunks=chunks, base=base_rows // tc, tc=tc)
    args = [hgl, hgl, hgl, w, b, g, beta]
    aliases = {}
    if prev_out is not None:
        in_specs.append(pl.BlockSpec(memory_space=pl.ANY))
        args.append(prev_out)
        aliases = {len(args) - 1: 0}
    return pl.pallas_call(
        functools.partial(_conv_kernel, chunks=chunks, tc=tc),
        grid=(nb, chunks),
        in_specs=in_specs,
        out_specs=blk,
        out_shape=jax.ShapeDtypeStruct((n, B_CH), jnp.bfloat16),
        scratch_shapes=[pltpu.VMEM((SUBLANES, tc + 2 * CONV_HALO, B_CH), jnp.float32)],
        input_output_aliases=aliases,
        compiler_params=_cparams("arbitrary", "arbitrary"),
        name="conformer_conv",
    )(*args)


META_ROWS = 8


def _outproj_kernel(*refs, n_x, n_lat_tiles, n_tiles):
    x_refs = refs[:n_x]
    (oa_ref, ob_ref, oc_ref, mod_ref, g_ref, w_ref, wr_ref, br_ref, tri_ref,
     xo_ref, hp_ref, meta_ref, rw_ref, cnt_ref, lg_ref) = refs[n_x:]
    i = pl.program_id(0)
    tm = xo_ref.shape[0]
    f32 = jnp.float32

    @pl.when(i == 0)
    def _():
        cnt_ref[...] = jnp.zeros_like(cnt_ref)
        lg_ref[...] = jnp.zeros_like(lg_ref)

    lat = jnp.concatenate([oa_ref[...], ob_ref[...], oc_ref[...]], axis=1)
    mix = jnp.dot(lat, w_ref[...], preferred_element_type=f32)
    x = _load_x(x_refs, n_lat_tiles, tile=jnp.minimum(i, n_tiles - 1)) + mod_ref[0, 2:3, :] * mix
    xo_ref[...] = x
    r = lax.rsqrt(jnp.mean(x * x, axis=-1, keepdims=True) + EPS)
    h = (x * r) * (g_ref[...] * (1.0 + mod_ref[0, 4:5, :])) + mod_ref[0, 3:4, :]
    h_hi = h.astype(jnp.bfloat16)
    hp_ref[...] = _pack_pairs(h_hi)
    h_lo = (h - h_hi.astype(f32)).astype(jnp.bfloat16)
    r_hi = jnp.dot(h_hi, wr_ref[...], preferred_element_type=f32)
    r_lo = jnp.dot(h_lo, wr_ref[:, 0:LANES], preferred_element_type=f32)
    lg = lg_ref[(i + 1) % 2]
    lg_ref[i % 2] = r_hi[:, 0:LANES] + r_hi[:, LANES:2 * LANES] + r_lo + br_ref[...]

    lane = lax.broadcasted_iota(jnp.int32, (tm, LANES), 1).astype(f32)
    big = float(LANES)
    ninf = -jnp.inf
    glog = jnp.where(lane < N_GROUPS, lg, ninf)
    gmax = glog.max(axis=-1, keepdims=True)
    g_val = 1.0 / jnp.exp(glog - gmax).sum(axis=-1, keepdims=True)
    g_idx = jnp.where(glog == gmax, lane, big).min(axis=-1, keepdims=True)
    e_lo = N_GROUPS + EXPERTS_PER_GROUP * g_idx
    el = jnp.where((lane >= e_lo) & (lane < e_lo + EXPERTS_PER_GROUP), lg, ninf)
    v0 = el.max(axis=-1, keepdims=True)
    i0 = jnp.where(el == v0, lane, big).min(axis=-1, keepdims=True)
    el1 = jnp.where(lane == i0, ninf, el)
    v1 = el1.max(axis=-1, keepdims=True)
    i1 = jnp.where(el1 == v1, lane, big).min(axis=-1, keepdims=True)
    t = jnp.exp(v1 - v0)
    w0 = g_val / (1.0 + t)
    w1 = g_val * t / (1.0 + t)
    e0 = i0 - N_GROUPS
    e1 = i1 - N_GROUPS

    cnt = cnt_ref[0:1, :]
    tri = tri_ref[...]
    ranks = []
    for e in (e0, e1):
        oh = lane == e
        ohf = oh.astype(f32)
        pre = jnp.dot(tri, ohf.astype(jnp.bfloat16), preferred_element_type=f32) + cnt
        ranks.append(jnp.where(oh, pre, 0.0).sum(axis=-1, keepdims=True))
        cnt = cnt + ohf.sum(axis=0, keepdims=True)
    cnt_ref[0:1, :] = jnp.where(i > 0, cnt, cnt_ref[0:1, :])
    rw_ref[...] = jnp.where(lane == 0, w0, jnp.where(lane == 1, w1, 0.0))
    rec = jnp.where(lane == 0, e0, jnp.where(lane == 1, e1, jnp.where(lane == 2, ranks[0],
                    jnp.where(lane == 3, ranks[1], jnp.where(lane == 4, w0, jnp.where(lane == 5, w1, 0.0))))))
    meta_ref[...] = rec.T[0:META_ROWS, :]


def _outproj(xs, oa, ob, oc, mods, g2, w_out_bf, wr, br, tri, *, n, tm, n_tiles, n_lat_tiles, s, nb,
             x_offsets=(0, 0)):
    d = xs[0].shape[1]
    rows = n_tiles * tm
    cur = lambda i: jnp.minimum(i, n_tiles - 1)
    prev = lambda i: jnp.maximum(i - 1, 0)

    def bidx(i):
        return jnp.where(cur(i) < n_lat_tiles, (cur(i) * tm) // s, nb)

    row = lambda w: pl.BlockSpec((tm, w), lambda i: (cur(i), 0))
    const = lambda a: pl.BlockSpec(a.shape, lambda i: (0,) * a.ndim)
    return pl.pallas_call(
        functools.partial(_outproj_kernel, n_x=len(xs), n_lat_tiles=n_lat_tiles, n_tiles=n_tiles),
        grid=(n_tiles + 1,),
        in_specs=_x_specs(xs, tm, n_lat_tiles, x_offsets, cur) + [
                  row(A_Q), row(B_CH), row(C_Q),
                  pl.BlockSpec((1, N_MOD, d), lambda i: (bidx(i), 0, 0)),
                  const(g2), const(w_out_bf), const(wr), const(br), const(tri)],
        out_specs=[row(d), row(d // 2),
                   pl.BlockSpec((META_ROWS, tm), lambda i: (0, prev(i))),
                   pl.BlockSpec((tm, LANES), lambda i: (prev(i), 0)),
                   pl.BlockSpec((8, LANES), lambda i: (0, 0))],
        out_shape=[jax.ShapeDtypeStruct((n, d), jnp.float32),
                   jax.ShapeDtypeStruct((rows, d // 2), jnp.int32),
                   jax.ShapeDtypeStruct((META_ROWS, rows), jnp.float32),
                   jax.ShapeDtypeStruct((rows, LANES), jnp.float32),
                   jax.ShapeDtypeStruct((8, LANES), jnp.float32)],
        scratch_shapes=[pltpu.VMEM((2, tm, LANES), jnp.float32)],
        input_output_aliases={0: 0} if len(xs) == 1 else {},
        compiler_params=_cparams("arbitrary"),
        name="outproj_router",
    )(*xs, oa, ob, oc, mods, g2, w_out_bf, wr, br, tri)


def _sc_mesh():
    return plsc.VectorSubcoreMesh(core_axis_name="core", subcore_axis_name="subcore")


def sc_gather_rows(table, idx2):
    r = idx2.shape[1]
    w = table.shape[1]
    assert r % (2 * SC_WINDOW) == 0
    half = r // SC_WINDOW // 2

    @functools.partial(pl.kernel, out_type=jax.ShapeDtypeStruct((r, w), table.dtype), mesh=_sc_mesh())
    def k(x_hbm, i_hbm, o_hbm):
        def body(i_vmem, o_vmem):
            pltpu.sync_copy(x_hbm.at[i_vmem.at[0]], o_vmem)

        pltpu.emit_pipeline(
            body,
            grid=(2, half),
            in_specs=[pl.BlockSpec((1, SC_WINDOW), lambda c, i: (0, c * half + i))],
            out_specs=[pl.BlockSpec((SC_WINDOW, w), lambda c, i: (c * half + i, 0),
                                    pipeline_mode=pl.Buffered(1))],
            core_axis_name=("core", "subcore"),
            dimension_semantics=(pltpu.PARALLEL, pltpu.PARALLEL),
        )(i_hbm, o_hbm)

    return k(table, idx2)


def sc_scatter_rows2(rows, idx2, n_out):
    r, w = rows.shape
    assert idx2.shape == (1, 2 * r) and r % (2 * SC_WINDOW) == 0
    windows = r // SC_WINDOW
    half = windows // 2

    @functools.partial(pl.kernel, out_type=jax.ShapeDtypeStruct((n_out, w), rows.dtype), mesh=_sc_mesh(),
                       scratch_types=[])
    def k(x_hbm, ia_hbm, ib_hbm, o_hbm):
        def body(x_vmem, ia_vmem, ib_vmem):
            pltpu.sync_copy(x_vmem, o_hbm.at[ia_vmem.at[0]])
            pltpu.sync_copy(x_vmem, o_hbm.at[ib_vmem.at[0]])

        pltpu.emit_pipeline(
            body,
            grid=(2, half),
            in_specs=[pl.BlockSpec((SC_WINDOW, w), lambda c, i: (c * half + i, 0),
                                   pipeline_mode=pl.Buffered(1)),
                      pl.BlockSpec((1, SC_WINDOW), lambda c, i: (0, c * half + i)),
                      pl.BlockSpec((1, SC_WINDOW), lambda c, i: (0, windows + c * half + i))],
            out_specs=[],
            core_axis_name=("core", "subcore"),
            dimension_semantics=(pltpu.PARALLEL, pltpu.PARALLEL),
        )(x_hbm, ia_hbm, ib_hbm)

    return k(rows, idx2, idx2)


def _expert_kernel(te_ref, nv_ref, x_ref, wg_ref, wu_ref, wd_ref, o_ref, wgb_ref, wub_ref, wdb_ref):
    t = pl.program_id(0)
    nvalid = nv_ref[t]

    @pl.when((t == 0) | (te_ref[t] != te_ref[jnp.maximum(t - 1, 0)]))
    def _():
        wgb_ref[...] = wg_ref[0].astype(jnp.bfloat16)
        wub_ref[...] = wu_ref[0].astype(jnp.bfloat16)
        wdb_ref[...] = wd_ref[0].astype(jnp.bfloat16)

    @pl.when(nvalid > 0)
    def _():
        rows = lax.broadcasted_iota(jnp.int32, x_ref.shape, 0)
        lo, hi = _unpack_pairs(jnp.where(rows < nvalid, x_ref[...], 0))
        xb = jnp.concatenate([lo, hi], axis=1).astype(jnp.bfloat16)
        g = jnp.dot(xb, wgb_ref[...], preferred_element_type=jnp.float32)
        u = jnp.dot(xb, wub_ref[...], preferred_element_type=jnp.float32)
        a = (g * jax.nn.sigmoid(g) * u).astype(jnp.bfloat16)
        o_ref[...] = _pack_pairs(jnp.dot(a, wdb_ref[...], preferred_element_type=jnp.float32))

    @pl.when(nvalid == 0)
    def _():
        o_ref[...] = jnp.zeros_like(o_ref)


def _experts(buf, tile_expert, tile_nvalid, wg, wu, wd, *, layer):
    rows, wp = buf.shape
    _, _, d, f = wg.shape
    n_tiles = rows // EXPERT_TILE
    grid_spec = pltpu.PrefetchScalarGridSpec(
        num_scalar_prefetch=2,
        grid=(n_tiles,),
        in_specs=[pl.BlockSpec((EXPERT_TILE, wp), lambda t, te, nv: (t, 0)),
                  pl.BlockSpec((None, 1, d, f), lambda t, te, nv: (layer, te[t], 0, 0)),
                  pl.BlockSpec((None, 1, d, f), lambda t, te, nv: (layer, te[t], 0, 0)),
                  pl.BlockSpec((None, 1, f, d), lambda t, te, nv: (layer, te[t], 0, 0))],
        out_specs=pl.BlockSpec((EXPERT_TILE, wp), lambda t, te, nv: (t, 0)),
        scratch_shapes=[pltpu.VMEM((d, f), jnp.bfloat16), pltpu.VMEM((d, f), jnp.bfloat16),
                        pltpu.VMEM((f, d), jnp.bfloat16)],
    )
    return pl.pallas_call(
        _expert_kernel,
        grid_spec=grid_spec,
        out_shape=jax.ShapeDtypeStruct((rows, wp), jnp.int32),
        compiler_params=_cparams("arbitrary"),
        name="expert_ffn",
    )(tile_expert, tile_nvalid, buf, wg, wu, wd)


def _final_kernel(y0_ref, y1_ref, rw_ref, x_ref, mod_ref, fg_ref, o_ref):
    x = x_ref[...] + mod_ref[0, 5:6, :] * _moe_mix(y0_ref, y1_ref, rw_ref, slice(None))
    r = lax.rsqrt(jnp.mean(x * x, axis=-1, keepdims=True) + EPS)
    o_ref[...] = x * r * fg_ref[...]


def _final_kernel_into(y0_ref, y1_ref, rw_ref, x_ref, mod_ref, fg_ref, prev_ref, o_ref):
    del prev_ref
    _final_kernel(y0_ref, y1_ref, rw_ref, x_ref, mod_ref, fg_ref, o_ref)


def _final_combine(y, rw, xall, mods, final_g, *, tm, n_tiles, s, out_rows, out_tile0, prev_out=None):
    d = xall.shape[1]
    row = lambda w: pl.BlockSpec((tm, w), lambda i: (i, 0))
    in_specs = [row(d // 2),
                pl.BlockSpec((tm, d // 2), lambda i: (i + n_tiles, 0)),
                row(LANES), row(d),
                pl.BlockSpec((1, N_MOD, d), lambda i: ((i * tm) // s, 0, 0)),
                pl.BlockSpec(final_g.shape, lambda i: (0, 0))]
    args = [y, y, rw, xall, mods, final_g]
    aliases = {}
    body = _final_kernel
    if prev_out is not None:
        in_specs.append(pl.BlockSpec(memory_space=pl.ANY))
        args.append(prev_out)
        aliases = {len(args) - 1: 0}
        body = _final_kernel_into
    return pl.pallas_call(
        body,
        grid=(n_tiles,),
        in_specs=in_specs,
        out_specs=pl.BlockSpec((tm, d), lambda i: (out_tile0 + i, 0)),
        out_shape=jax.ShapeDtypeStruct((out_rows, d), jnp.float32),
        input_output_aliases=aliases,
        compiler_params=_cparams("arbitrary"),
        name="moe_combine_final",
    )(*args)


def _dest_kernel(ps_ref, meta_ref, o_ref):
    slot = pl.program_id(0)
    e = meta_ref[pl.ds(slot, 1), :]
    d = meta_ref[pl.ds(TOP_K + slot, 1), :]
    for k in range(N_EXPERTS):
        d = d + jnp.where(e == float(k), ps_ref[k], 0.0)
    o_ref[...] = d.astype(jnp.int32)


def _dest_rows(meta, pstarts, tcols):
    n = meta.shape[1]
    nt = n // tcols
    return pl.pallas_call(
        _dest_kernel,
        grid=(TOP_K, nt),
        in_specs=[pl.BlockSpec(memory_space=pltpu.SMEM),
                  pl.BlockSpec((META_ROWS, tcols), lambda k, i: (0, i))],
        out_specs=pl.BlockSpec((1, tcols), lambda k, i: (0, k * nt + i)),
        out_shape=jax.ShapeDtypeStruct((1, TOP_K * n), jnp.int32),
        compiler_params=_cparams("arbitrary", "arbitrary"),
        name="moe_dest",
    )(pstarts.astype(jnp.float32), meta)


def _dispatch_plan(meta, counts, n_rows_buf, tcols):
    cnt = counts[0, :N_EXPERTS].astype(jnp.int32)
    padded = (cnt + EXPERT_TILE - 1) // EXPERT_TILE * EXPERT_TILE
    pends = jnp.cumsum(padded)
    pstarts = pends - padded
    dest = _dest_rows(meta, pstarts, tcols)
    tile_start = jnp.arange(n_rows_buf // EXPERT_TILE, dtype=jnp.int32) * EXPERT_TILE
    te = jnp.sum((tile_start[:, None] >= pends[None, :]).astype(jnp.int32), axis=1)
    te = jnp.minimum(te, N_EXPERTS - 1)
    onehot = te[:, None] == jnp.arange(N_EXPERTS, dtype=jnp.int32)[None, :]
    cnt_te = jnp.sum(jnp.where(onehot, cnt[None, :], 0), axis=1)
    pstart_te = jnp.sum(jnp.where(onehot, pstarts[None, :], 0), axis=1)
    nvalid = jnp.clip(cnt_te - (tile_start - pstart_te), 0, EXPERT_TILE).astype(jnp.int32)
    return dest, te, nvalid


def _permute_heads(w, axis):
    heads = [lax.slice_in_dim(w, HEAD_DIM * h, HEAD_DIM * (h + 1), axis=axis)
             for h in HEAD_PERM[::HEAD_DIM] // HEAD_DIM]
    return jnp.concatenate(heads, axis=axis)


def _rope_tables(s, tm):
    pos = np.arange(s)
    pos_row = jnp.asarray(pos // GRID_W, jnp.float32)
    pos_col = jnp.asarray(pos % GRID_W, jnp.float32)
    n_freq = HEAD_DIM // 4
    inv = ROPE_THETA ** (-jnp.arange(n_freq, dtype=jnp.float32) / n_freq)
    ang_row = pos_row[:, None] * inv
    ang_col = pos_col[:, None] * inv
    ang = jnp.concatenate([ang_row, ang_row, ang_col, ang_col] * (LANES // HEAD_DIM), axis=-1)
    sign = np.where((np.arange(LANES) % 32) < 16, -1.0, 1.0).astype(np.float32)
    cos_t = jnp.concatenate([jnp.cos(ang), jnp.ones((tm, LANES), jnp.float32)], axis=0)
    sin_t = jnp.concatenate([jnp.sin(ang) * sign, jnp.zeros((tm, LANES), jnp.float32)], axis=0)
    return cos_t, sin_t


def kernel(x, c, ctx, c_ctx, norm1_g, norm2_g, w_mod, b_mod, w_in, q_norm_g, k_norm_g, conv_w, conv_b, conv_ln_g, conv_ln_b, sink, w_out, w_group, b_group, w_expert, b_expert, w_gate, w_up, w_down, final_g):
    nb, s, d = x.shape
    l = ctx.shape[1]
    depth = w_in.shape[0]
    assert w_in.shape[2] == D_IN and w_out.shape[1] == D_MIX
    assert s % GRID_W == 0 and s >= 3 * WINDOW and s % WINDOW == 0 and l % WINDOW == 0
    groups = BATCH_GROUPS if nb % BATCH_GROUPS == 0 else 1
    nbg = nb // groups
    n_lat, n_ctx = nbg * s, nbg * l
    tm = _pick(np.gcd(s, n_ctx), (512, 256, 128))
    tq = _pick(s, (512, 256, 128))
    tc = _pick(np.gcd(s, l), (256, 128))
    assert n_lat % l == 0
    bf = jnp.bfloat16
    f32 = jnp.float32

    x2d = x.reshape(nb * s, d)
    ctx2d = ctx.reshape(nb * l, d)
    c_all = jnp.concatenate([c, c_ctx[None, :]], axis=0)
    mods_all = _modulation(c_all, w_mod, b_mod).reshape(depth, nb + 1, N_MOD, d)
    cos_t, sin_t = _rope_tables(s, tm)
    head_id = np.arange(LANES) // HEAD_DIM
    gsum = jnp.asarray((head_id[:, None] == head_id[None, :]) / HEAD_DIM, bf)
    tri = jnp.asarray(np.tril(np.ones((tm, tm), np.float32), -1), bf)
    n_lat_tiles = n_lat // tm

    xs = [(x2d, ctx2d)] * groups
    pending = [None] * groups
    for i in range(depth):
        last = i == depth - 1
        with_ctx = not last
        qg = jnp.tile(q_norm_g[i], LANES // HEAD_DIM)[None, :]
        kg = jnp.tile(k_norm_g[i], LANES // HEAD_DIM)[None, :]
        w_in_bf = jnp.concatenate(
            [_permute_heads(w_in[i][:, OFF_AQ:OFF_AK], 1), w_in[i][:, OFF_AK:OFF_CQ],
             _permute_heads(w_in[i][:, OFF_CQ:OFF_CK], 1), w_in[i][:, OFF_CK:]], axis=1).astype(bf)
        w_out_bf = jnp.concatenate(
            [_permute_heads(w_out[i][0:A_Q], 0), w_out[i][A_Q:A_Q + B_CH],
             _permute_heads(w_out[i][A_Q + B_CH:], 0)], axis=0).astype(bf)
        sink2 = sink[i] * LOG2E
        conv_args = (conv_w[i].reshape(CONV_W, B_CH), conv_b[i][None, :], conv_ln_g[i][None, :],
                     conv_ln_b[i][None, :])
        wr32 = jnp.zeros((d, LANES), f32).at[:, :N_GROUPS].set(w_group[i])
        wr32 = wr32.at[:, N_GROUPS:N_GROUPS + N_EXPERTS].set(w_expert[i])
        wr_hi = wr32.astype(bf)
        wr = jnp.concatenate([wr_hi, (wr32 - wr_hi.astype(f32)).astype(bf)], axis=1)
        br = jnp.zeros((1, LANES), f32).at[0, :N_GROUPS].set(b_group[i])
        br = br.at[0, N_GROUPS:N_GROUPS + N_EXPERTS].set(b_expert[i])
        n_tok = n_lat + n_ctx if with_ctx else n_lat
        n_tiles = n_tok // tm
        n_rows_buf = 2 * n_tok + N_EXPERTS * EXPERT_TILE
        for g in range(groups):
            b0 = g * nbg
            mods = jnp.concatenate([mods_all[i, b0:b0 + nbg], mods_all[i, nb:nb + 1]], axis=0)
            x_offsets = (b0 * s // tm, b0 * l // tm)
            x_new, qa, kta, va, hgl, qc, ktc, vc = _inproj(
                xs[g], mods, norm1_g[i][None, :], w_in_bf, gsum, qg, kg, cos_t, sin_t, n=n_lat + n_ctx,
                tm=tm, n_lat_tiles=n_lat_tiles, s=s, nb=nbg, x_offsets=x_offsets, combine=pending[g])
            if pending[g] is not None:
                xs[g] = (x_new,)
            oa, ob, oc = _latent_mixers(qa, kta, va, qc, ktc, vc, sink2, hgl, conv_args,
                                        nb=nbg, s=s, l=l, tq=tq)
            if with_ctx:
                oa, oc = _attn_ctx(qa, kta, va, qc, ktc, vc, sink2, oa, oc, nb=nbg, s=s, l=l)
                ob = _conv(hgl, *conv_args, nb=nbg, seq=l, base_rows=n_lat, tc=tc, prev_out=ob)
            xall, hp, meta, rw, counts = _outproj(xs[g], oa, ob, oc, mods, norm2_g[i][None, :], w_out_bf,
                                                  wr, br, tri, n=n_lat + n_ctx, tm=tm, n_tiles=n_tiles,
                                                  n_lat_tiles=n_lat_tiles, s=s, nb=nbg, x_offsets=x_offsets)
            dest, te, nvalid = _dispatch_plan(meta, counts, n_rows_buf,
                                              _pick(n_tok, (8192, 4096, 2048, 1024, 512, 256, 128)))
            buf = sc_scatter_rows2(hp, dest, n_rows_buf)
            eo = _experts(buf, te, nvalid, w_gate, w_up, w_down, layer=i)
            y = sc_gather_rows(eo, dest)
            xs[g] = (xall,)
            pending[g] = (y, rw, mods)
    out = None
    for g in range(groups):
        y, rw, mods = pending[g]
        out = _final_combine(y, rw, xs[g][0], mods, final_g[None, :], tm=tm, n_tiles=n_lat_tiles, s=s,
                             out_rows=nb * s, out_tile0=g * n_lat_tiles, prev_out=out)
    return out.reshape(nb, s, d)
```

```python
import functools

import jax
import jax.numpy as jnp
import numpy as np
from jax import lax
from jax.experimental import pallas as pl
from jax.experimental.pallas import tpu as pltpu
from jax.experimental.pallas import tpu_sc as plsc

HEAD_DIM = 64
GRID_W = 64
ROPE_THETA = 10000.0
A_HEADS, A_KV_HEADS = 6, 2
C_HEADS, C_KV_HEADS = 6, 2
B_CH = 256
CONV_W = 31
WINDOW = 128
N_GROUPS = 4
EXPERTS_PER_GROUP = 8
N_EXPERTS = N_GROUPS * EXPERTS_PER_GROUP
TOP_K = 2
N_MOD = 6
EPS = 1e-6
ATTN_SCALE = HEAD_DIM ** -0.5
LOG2E = 1.4426950408889634
Q_SCALE = ATTN_SCALE * LOG2E

A_Q = A_HEADS * HEAD_DIM
A_KV = A_KV_HEADS * HEAD_DIM
C_Q = C_HEADS * HEAD_DIM
C_KV = C_KV_HEADS * HEAD_DIM
D_MIX = A_Q + B_CH + C_Q
OFF_AQ = 0
OFF_AK = OFF_AQ + A_Q
OFF_AV = OFF_AK + A_KV
OFF_BU = OFF_AV + A_KV
OFF_CQ = OFF_BU + 2 * B_CH
OFF_CK = OFF_CQ + C_Q
OFF_CV = OFF_CK + C_KV
D_IN = OFF_CV + C_KV

LANES = 128
SUBLANES = 8
KT_ROWS = LANES
V_W = 2 * LANES
HEAD_PERM = np.concatenate([np.arange(HEAD_DIM) + HEAD_DIM * h for b in range(3) for h in (b, b + 3)])
EXPERT_TILE = 512
SC_WINDOW = 128
VMEM_LIMIT = 56 * 1024 * 1024
HI_MASK = -65536
BATCH_GROUPS = 2
ROW_CHAIN = 256


def _cparams(*sem):
    return pltpu.CompilerParams(dimension_semantics=sem, vmem_limit_bytes=VMEM_LIMIT)


def _pick(n, cands):
    for c in cands:
        if n % c == 0:
            return c
    raise ValueError(f"no tile in {cands} divides {n}")


def _pack_pairs(x):
    w = x.shape[1] // 2
    lo = lax.bitcast_convert_type(x[:, :w].astype(jnp.bfloat16).astype(jnp.float32), jnp.int32)
    hi = lax.bitcast_convert_type(x[:, w:].astype(jnp.bfloat16).astype(jnp.float32), jnp.int32)
    return (hi & HI_MASK) | lax.shift_right_logical(lo, 16)


def _unpack_pairs(p):
    lo = lax.bitcast_convert_type(lax.shift_left(p, 16), jnp.float32)
    hi = lax.bitcast_convert_type(p & HI_MASK, jnp.float32)
    return lo, hi


def _mod_kernel(c_ref, w_ref, b_ref, o_ref):
    c = c_ref[...]
    a = c * jax.nn.sigmoid(c)
    o_ref[0] = jnp.dot(a, w_ref[0], preferred_element_type=jnp.float32,
                       precision=lax.Precision.HIGHEST) + b_ref[0]


def _modulation(c_all, w_mod, b_mod):
    depth, d, n = w_mod.shape
    r = c_all.shape[0]
    tn = _pick(n, (1024, 512, 256, 128))
    return pl.pallas_call(
        _mod_kernel,
        grid=(depth, n // tn),
        in_specs=[pl.BlockSpec((r, d), lambda l, j: (0, 0)),
                  pl.BlockSpec((1, d, tn), lambda l, j: (l, 0, j)),
                  pl.BlockSpec((1, 1, tn), lambda l, j: (l, 0, j))],
        out_specs=pl.BlockSpec((1, r, tn), lambda l, j: (l, 0, j)),
        out_shape=jax.ShapeDtypeStruct((depth, r, n), jnp.float32),
        compiler_params=_cparams("arbitrary", "arbitrary"),
        name="modulation",
    )(c_all, w_mod, b_mod.reshape(depth, 1, n))


def _head_mean_sq(blk, gsum):
    sq = blk * blk
    hi = sq.astype(jnp.bfloat16)
    lo = (sq - hi.astype(jnp.float32)).astype(jnp.bfloat16)
    return (jnp.dot(hi, gsum, preferred_element_type=jnp.float32)
            + jnp.dot(lo, gsum, preferred_element_type=jnp.float32))


def _x_specs(xs, tm, n_lat_tiles, offsets=(0, 0), tile_of=lambda i: i):
    d = xs[0].shape[1]
    if len(xs) == 1:
        return [pl.BlockSpec((tm, d), lambda i: (tile_of(i), 0))]
    lat0, ctx0 = offsets
    return [pl.BlockSpec((tm, d), lambda i: (lat0 + jnp.minimum(tile_of(i), n_lat_tiles - 1), 0)),
            pl.BlockSpec((tm, d), lambda i: (ctx0 + jnp.maximum(tile_of(i) - n_lat_tiles, 0), 0))]


def _load_x(x_refs, n_lat_tiles, rows=slice(None), tile=None):
    if len(x_refs) == 1:
        return x_refs[0][rows, :]
    tile = pl.program_id(0) if tile is None else tile
    return jnp.where(tile < n_lat_tiles, x_refs[0][rows, :], x_refs[1][rows, :])


def _moe_mix(y0_ref, y1_ref, rw_ref, rows):
    rw = rw_ref[rows, :]
    w0 = rw[:, 0:1]
    w1 = rw[:, 1:2]
    a_lo, a_hi = _unpack_pairs(y0_ref[rows, :])
    b_lo, b_hi = _unpack_pairs(y1_ref[rows, :])
    return jnp.concatenate([a_lo * w0 + b_lo * w1, a_hi * w0 + b_hi * w1], axis=1)


def _inproj_kernel(*refs, n_x, n_lat_tiles, fused_combine):
    x_refs = refs[:n_x]
    refs = refs[n_x:]
    if fused_combine:
        y0_ref, y1_ref, rw_ref, modp_ref = refs[:4]
        refs = refs[4:]
        xo_ref = refs[-1]
        refs = refs[:-1]
    (mod_ref, g_ref, w_ref, gsum_ref, qg_ref, kg_ref, cos_ref, sin_ref,
     qa_ref, kta_ref, va_ref, hgl_ref, qc_ref, ktc_ref, vc_ref) = refs
    tm = qa_ref.shape[0]
    tr = min(tm, ROW_CHAIN)
    bf = jnp.bfloat16
    lane = lax.broadcasted_iota(jnp.int32, (tr, LANES), 1)
    first16 = (lane % 32) < 16
    gsum = gsum_ref[...]
    qg = qg_ref[...]
    kg = kg_ref[...]
    scale = g_ref[...] * (1.0 + mod_ref[0, 1:2, :])
    shift = mod_ref[0, 0:1, :]
    ones = jnp.ones((tr, LANES), bf)

    for r0 in range(0, tm, tr):
        rows = slice(r0, r0 + tr)
        x = _load_x(x_refs, n_lat_tiles, rows)
        if fused_combine:
            x = x + modp_ref[0, 5:6, :] * _moe_mix(y0_ref, y1_ref, rw_ref, rows)
            xo_ref[rows, :] = x
        r = lax.rsqrt(jnp.mean(x * x, axis=-1, keepdims=True) + EPS)
        h = (x * r) * scale + shift
        p = jnp.dot(h.astype(bf), w_ref[...], preferred_element_type=jnp.float32)
        cos = cos_ref[rows, :]
        sin = sin_ref[rows, :]

        def blk(off):
            return p[:, off:off + LANES]

        def rope(t):
            sw = jnp.where(first16, pltpu.roll(t, LANES - 16, axis=1), pltpu.roll(t, 16, axis=1))
            return t * cos + sw * sin

        def norm(t, g):
            return t * lax.rsqrt(_head_mean_sq(t, gsum) + EPS) * g

        def store_kv(kt_ref, v_ref, k_blk, v_blk):
            kt_ref[:, rows] = k_blk.T.astype(bf)
            v_ref[rows, 0:LANES] = v_blk.astype(bf)
            v_ref[rows, LANES:2 * LANES] = ones

        for i in range(A_Q // LANES):
            t = rope(norm(blk(OFF_AQ + i * LANES), qg)) * Q_SCALE
            qa_ref[rows, i * LANES:(i + 1) * LANES] = t.astype(bf)
        store_kv(kta_ref, va_ref, rope(norm(blk(OFF_AK), kg)), blk(OFF_AV))
        for i in range(B_CH // LANES):
            a = blk(OFF_BU + i * LANES)
            gt = blk(OFF_BU + B_CH + i * LANES)
            hgl_ref[rows, i * LANES:(i + 1) * LANES] = a * jax.nn.sigmoid(gt)
        for i in range(C_Q // LANES):
            t = rope(blk(OFF_CQ + i * LANES)) * Q_SCALE
            qc_ref[rows, i * LANES:(i + 1) * LANES] = t.astype(bf)
        store_kv(ktc_ref, vc_ref, rope(blk(OFF_CK)), blk(OFF_CV))


def _inproj(xs, mods, g1, w_in_bf, gsum, qg, kg, cos_t, sin_t, *, n, tm, n_lat_tiles, s, nb, x_offsets=(0, 0),
            combine=None):
    d = xs[0].shape[1]
    s_tiles = s // tm
    n_tiles = n // tm

    def bidx(i):
        return jnp.where(i < n_lat_tiles, (i * tm) // s, nb)

    def ridx(i):
        return jnp.where(i < n_lat_tiles, i % s_tiles, s_tiles)

    row = lambda w: pl.BlockSpec((tm, w), lambda i: (i, 0))
    ktspec = pl.BlockSpec((KT_ROWS, tm), lambda i: (0, i))
    const = lambda a: pl.BlockSpec(a.shape, lambda i: (0,) * a.ndim)
    bf = jnp.bfloat16
    modspec = pl.BlockSpec((1, N_MOD, d), lambda i: (bidx(i), 0, 0))
    in_specs = _x_specs(xs, tm, n_lat_tiles, x_offsets)
    args = list(xs)
    out_specs = [row(A_Q), ktspec, row(V_W), row(B_CH), row(C_Q), ktspec, row(V_W)]
    out_shape = [jax.ShapeDtypeStruct((n, A_Q), bf), jax.ShapeDtypeStruct((KT_ROWS, n), bf),
                 jax.ShapeDtypeStruct((n, V_W), bf), jax.ShapeDtypeStruct((n, B_CH), jnp.float32),
                 jax.ShapeDtypeStruct((n, C_Q), bf), jax.ShapeDtypeStruct((KT_ROWS, n), bf),
                 jax.ShapeDtypeStruct((n, V_W), bf)]
    aliases = {}
    if combine is not None:
        y, rw, mods_prev = combine
        assert len(xs) == 1 and y.shape[0] == 2 * n
        in_specs += [row(d // 2), pl.BlockSpec((tm, d // 2), lambda i: (i + n_tiles, 0)), row(LANES), modspec]
        args += [y, y, rw, mods_prev]
        out_specs.append(row(d))
        out_shape.append(jax.ShapeDtypeStruct((n, d), jnp.float32))
        aliases = {0: len(out_shape) - 1}
    in_specs += [modspec, const(g1), const(w_in_bf), const(gsum), const(qg), const(kg),
                 pl.BlockSpec((tm, LANES), lambda i: (ridx(i), 0)),
                 pl.BlockSpec((tm, LANES), lambda i: (ridx(i), 0))]
    args += [mods, g1, w_in_bf, gsum, qg, kg, cos_t, sin_t]
    outs = pl.pallas_call(
        functools.partial(_inproj_kernel, n_x=len(xs), n_lat_tiles=n_lat_tiles,
                          fused_combine=combine is not None),
        grid=(n_tiles,),
        in_specs=in_specs,
        out_specs=out_specs,
        out_shape=out_shape,
        input_output_aliases=aliases,
        compiler_params=_cparams("arbitrary"),
        name="inproj",
    )(*args)
    if combine is not None:
        return (outs[-1],) + tuple(outs[:-1])
    return (None,) + tuple(outs)


def _stack_heads(q_ref, tq):
    lane = lax.broadcasted_iota(jnp.int32, (tq, LANES), 1)
    lo = lane < HEAD_DIM
    qb = [q_ref[:, i * LANES:(i + 1) * LANES] for i in range(3)]
    zero = jnp.zeros_like(qb[0])
    s0 = jnp.concatenate([jnp.where(lo, t, zero) for t in qb], axis=0)
    s1 = jnp.concatenate([jnp.where(lo, zero, t) for t in qb], axis=0)
    return s0, s1, lo


def _unstack_store(o_ref, o0, o1, lo, tq):
    for i in range(3):
        rows = slice(i * tq, (i + 1) * tq)
        o_ref[:, i * LANES:(i + 1) * LANES] = jnp.where(lo, o0[rows], o1[rows]).astype(jnp.bfloat16)


def _krow(kv):
    del kv
    return slice(0, KT_ROWS)


def _vcol(kv):
    del kv
    return slice(0, V_W)


def _row_max(scores):
    m = None
    for t in scores:
        for c in range(0, t.shape[1], LANES):
            blk = t[:, c:c + LANES]
            m = blk if m is None else jnp.maximum(m, blk)
    return m.max(axis=-1, keepdims=True)


def _softmax_pv(scores, values, extra=None):
    m = _row_max(scores)
    if extra is not None:
        m = jnp.maximum(m, extra)
    acc = None
    for t, v in zip(scores, values):
        c = jnp.dot(jnp.exp2(t - m).astype(jnp.bfloat16), v, preferred_element_type=jnp.float32)
        acc = c if acc is None else acc + c
    den = acc[:, LANES:2 * LANES]
    if extra is not None:
        den = den + jnp.exp2(extra - m)
    return acc[:, 0:LANES] / den


ATTN_CHAIN_ROWS = 128


def _qk(q, kt):
    return jnp.dot(q, kt, preferred_element_type=jnp.float32)


def _attn_a_kernel(q_ref, ktl_ref, ktc_ref, vl_ref, vc_ref, *rest, tq, n_q):
    conv_refs = rest[:7]
    o_ref, ob_ref, sh_ref = rest[7:]
    j = pl.program_id(1)
    conv_steps = _conv_steps(conv_refs, ob_ref, sh_ref, j > 0, j < n_q - 1, tq)
    s0, s1, lo = _stack_heads(q_ref, tq)
    n_chains = 2 * (3 * tq // ATTN_CHAIN_ROWS)
    chain = 0
    conv_done = 0
    outs = []
    for kv, qs in enumerate((s0, s1)):
        krow = _krow(kv)
        vcol = _vcol(kv)
        parts = []
        for r0 in range(0, 3 * tq, ATTN_CHAIN_ROWS):
            qr = qs[r0:r0 + ATTN_CHAIN_ROWS]
            ss = [_qk(qr, ktl_ref[krow, :]), _qk(qr, ktc_ref[krow, :])]
            parts.append(_softmax_pv(ss, [vl_ref[:, vcol], vc_ref[:, vcol]]))
            chain += 1
            while conv_done < chain * len(conv_steps) // n_chains:
                conv_steps[conv_done]()
                conv_done += 1
        outs.append(jnp.concatenate(parts, axis=0))
    _unstack_store(o_ref, outs[0], outs[1], lo, tq)


def _kv_specs(nb, s, l):
    ctx0 = nb * s // l
    return [pl.BlockSpec((KT_ROWS, s), lambda b, j: (0, b)),
            pl.BlockSpec((KT_ROWS, l), lambda b, j: (0, ctx0 + b)),
            pl.BlockSpec((s, V_W), lambda b, j: (b, 0)),
            pl.BlockSpec((l, V_W), lambda b, j: (ctx0 + b, 0))]


def _sink_column(sink_ref, kv, rows):
    return jnp.concatenate([jnp.full((rows, 1), sink_ref[3 * kv + g], jnp.float32) for g in range(3)], axis=0)


def _attn_c_kernel(sink_ref, bias_ref, q_ref, ktl_ref, ktc_ref, vl_ref, vc_ref, o_ref, *, s, blocks):
    j = pl.program_id(1)
    tq = WINDOW
    band = 3 * WINDOW
    bf = jnp.bfloat16
    f32 = jnp.float32
    m3 = 3 * tq
    stacks = [_stack_heads(q_ref.at[blk * tq:(blk + 1) * tq, :], tq) for blk in range(blocks)]
    lo = stacks[0][2]
    starts = [pl.multiple_of(jnp.clip((j * blocks + blk - 1) * WINDOW, 0, s - band), WINDOW)
              for blk in range(blocks)]
    outs = [[None, None] for _ in range(blocks)]
    for kv in range(C_KV_HEADS):
        krow = _krow(kv)
        vcol = _vcol(kv)
        sk = _sink_column(sink_ref, kv, tq)
        sc_all = _qk(jnp.concatenate([st[kv] for st in stacks], axis=0), ktc_ref[krow, :])
        accs, pcs, ms = [], [], []
        for blk in range(blocks):
            start = starts[blk]
            sl = _qk(stacks[blk][kv], ktl_ref[krow, pl.ds(start, band)]) + bias_ref[j * blocks + blk - start // WINDOW]
            sc = sc_all[blk * m3:(blk + 1) * m3]
            m = jnp.maximum(_row_max([sl, sc]), sk)
            accs.append(jnp.dot(jnp.exp2(sl - m).astype(bf), vl_ref[pl.ds(start, band), vcol],
                                preferred_element_type=f32))
            pcs.append(jnp.exp2(sc - m).astype(bf))
            ms.append(m)
        acc_ctx = jnp.dot(jnp.concatenate(pcs, axis=0), vc_ref[:, vcol], preferred_element_type=f32)
        for blk in range(blocks):
            acc = accs[blk] + acc_ctx[blk * m3:(blk + 1) * m3]
            den = acc[:, LANES:2 * LANES] + jnp.exp2(sk - ms[blk])
            outs[blk][kv] = acc[:, 0:LANES] / den
    for blk in range(blocks):
        _unstack_store(o_ref.at[blk * tq:(blk + 1) * tq, :], outs[blk][0], outs[blk][1], lo, tq)


def _window_bias():
    r = np.arange(3 * WINDOW)[:, None] % WINDOW
    col = np.arange(3 * WINDOW)[None, :]
    tabs = [np.where(np.abs(col - r - WINDOW * off) <= WINDOW, 0.0, -np.inf) for off in range(3)]
    return jnp.asarray(np.stack(tabs), jnp.float32)


def _mixers_kernel(*refs, tq, n_q, s):
    sink_ref, bias_ref, qc_ref = refs[0:3]
    kvc_refs = refs[3:7]
    qa_ref = refs[7]
    kva_refs = refs[8:12]
    conv_refs = refs[12:19]
    oa_ref, ob_ref, oc_ref, sh_ref = refs[19:]
    _attn_c_kernel(sink_ref, bias_ref, qc_ref, *kvc_refs, oc_ref, s=s, blocks=tq // WINDOW)
    _attn_a_kernel(qa_ref, *kva_refs, *conv_refs, oa_ref, ob_ref, sh_ref, tq=tq, n_q=n_q)


def _latent_mixers(qa, kta, va, qc, ktc, vc, sink2, hgl, conv_params, *, nb, s, l, tq):
    n = qa.shape[0]
    n_q = s // tq
    bias = _window_bias()
    conv_specs, conv_blk = _conv_specs(n, conv_params, chunks=n_q, base=0, tc=tq)
    qspec = lambda w: pl.BlockSpec((tq, w), lambda b, j: (b * n_q + j, 0))
    bf = jnp.bfloat16
    return pl.pallas_call(
        functools.partial(_mixers_kernel, tq=tq, n_q=n_q, s=s),
        grid=(nb, n_q),
        in_specs=[pl.BlockSpec(memory_space=pltpu.SMEM), pl.BlockSpec(bias.shape, lambda b, j: (0, 0, 0)),
                  qspec(C_Q)] + _kv_specs(nb, s, l) + [qspec(A_Q)] + _kv_specs(nb, s, l) + conv_specs,
        out_specs=[qspec(A_Q), conv_blk, qspec(C_Q)],
        out_shape=[jax.ShapeDtypeStruct((n, A_Q), bf), jax.ShapeDtypeStruct((n, B_CH), bf),
                   jax.ShapeDtypeStruct((n, C_Q), bf)],
        scratch_shapes=[pltpu.VMEM((SUBLANES, tq + 2 * CONV_HALO, B_CH), jnp.float32)],
        compiler_params=_cparams("arbitrary", "arbitrary"),
        name="latent_mixers",
    )(sink2, bias, qc, ktc, ktc, vc, vc, qa, kta, kta, va, va, hgl, hgl, hgl, *conv_params)


def _ctx_mixers_kernel(sink_ref, qa_ref, kta_ref, va_ref, qc_ref, ktc_ref, vc_ref, *rest, l):
    conv_refs = rest[:7]
    oa_ref, ob_ref, oc_ref, sh_ref = rest[-4:]
    conv_steps = _conv_steps(conv_refs, ob_ref, sh_ref, False, False, l)
    for q_ref, kt_ref, v_ref, o_ref, with_sink in ((qa_ref, kta_ref, va_ref, oa_ref, False),
                                                   (qc_ref, ktc_ref, vc_ref, oc_ref, True)):
        s0, s1, lo = _stack_heads(q_ref, l)
        outs = []
        for kv, qs in enumerate((s0, s1)):
            sk = _sink_column(sink_ref, kv, l) if with_sink else None
            outs.append(_softmax_pv([_qk(qs, kt_ref[_krow(kv), :])], [v_ref[:, _vcol(kv)]], sk))
            for step in conv_steps[(2 * with_sink + kv) * len(conv_steps) // 4:
                                   (2 * with_sink + kv + 1) * len(conv_steps) // 4]:
                step()
        _unstack_store(o_ref, outs[0], outs[1], lo, l)


def _ctx_mixers(qa, kta, va, qc, ktc, vc, sink2, hgl, conv_params, oa, ob, oc, *, nb, s, l):
    n = qa.shape[0]
    base = nb * s // l
    row = lambda w: pl.BlockSpec((l, w), lambda b, j: (base + b, 0))
    ktspec = pl.BlockSpec((KT_ROWS, l), lambda b, j: (0, base + b))
    conv_specs, conv_blk = _conv_specs(n, conv_params, chunks=1, base=base, tc=l)
    anyspec = pl.BlockSpec(memory_space=pl.ANY)
    n_in = 7 + len(conv_specs)
    return pl.pallas_call(
        functools.partial(_ctx_mixers_kernel, l=l),
        grid=(nb, 1),
        in_specs=[pl.BlockSpec(memory_space=pltpu.SMEM), row(A_Q), ktspec, row(V_W), row(C_Q), ktspec, row(V_W)]
        + conv_specs + [anyspec, anyspec, anyspec],
        out_specs=[row(A_Q), conv_blk, row(C_Q)],
        out_shape=[jax.ShapeDtypeStruct(a.shape, a.dtype) for a in (oa, ob, oc)],
        scratch_shapes=[pltpu.VMEM((SUBLANES, l + 2 * CONV_HALO, B_CH), jnp.float32)],
        input_output_aliases={n_in: 0, n_in + 1: 1, n_in + 2: 2},
        compiler_params=_cparams("arbitrary", "arbitrary"),
        name="context_mixers",
    )(sink2, qa, kta, va, qc, ktc, vc, hgl, hgl, hgl, *conv_params, oa, ob, oc)


CONV_HALO = 16
CONV_ROWS = 64


def _conv_steps(conv_refs, o_ref, sh_ref, has_prev, has_next, tc):
    prev_ref, cur_ref, next_ref, w_ref, b_ref, g_ref, beta_ref = conv_refs
    rows = tc + 2 * CONV_HALO
    sh_ref[0, 0:CONV_HALO, :] = jnp.where(has_prev, prev_ref[...], 0.0)
    sh_ref[0, CONV_HALO:CONV_HALO + tc, :] = cur_ref[...]
    sh_ref[0, CONV_HALO + tc:rows, :] = jnp.where(has_next, next_ref[...], 0.0)
    for b in range(1, SUBLANES):
        sh_ref[b, 0:rows - SUBLANES, :] = sh_ref[0, b:b + rows - SUBLANES, :]
    base = CONV_HALO - CONV_W // 2

    def step(r0):
        acc = None
        for k in range(CONV_W):
            a, b = divmod(base + k, SUBLANES)
            term = sh_ref[b, SUBLANES * a + r0:SUBLANES * a + r0 + CONV_ROWS, :] * w_ref[k:k + 1, :]
            acc = term if acc is None else acc + term
        hc = acc + b_ref[...]
        mu = jnp.mean(hc, axis=-1, keepdims=True)
        xc = hc - mu
        var = jnp.mean(xc * xc, axis=-1, keepdims=True)
        y = xc * lax.rsqrt(var + EPS) * g_ref[...] + beta_ref[...]
        o_ref[r0:r0 + CONV_ROWS, :] = (y * jax.nn.sigmoid(y)).astype(o_ref.dtype)

    return [functools.partial(step, r0) for r0 in range(0, tc, CONV_ROWS)]


def _conv_specs(n, params, *, chunks, base, tc):
    hb = tc // CONV_HALO
    last_halo = n // CONV_HALO - 1
    idx = lambda b_, j: base + b_ * chunks + j
    blk = pl.BlockSpec((tc, B_CH), lambda b_, j: (idx(b_, j), 0))
    prev_halo = pl.BlockSpec((CONV_HALO, B_CH), lambda b_, j: (jnp.maximum(idx(b_, j) * hb - 1, 0), 0))
    next_halo = pl.BlockSpec((CONV_HALO, B_CH), lambda b_, j: (jnp.minimum((idx(b_, j) + 1) * hb, last_halo), 0))
    const = lambda a: pl.BlockSpec(a.shape, lambda b_, j: (0,) * a.ndim)
    return [prev_halo, blk, next_halo] + [const(a) for a in params], blk


META_ROWS = 8


def _outproj_kernel(*refs, n_x, n_lat_tiles, n_tiles):
    x_refs = refs[:n_x]
    (oa_ref, ob_ref, oc_ref, mod_ref, g_ref, w_ref, wr_ref, br_ref, tri_ref,
     xo_ref, hp_ref, meta_ref, rw_ref, cnt_ref, lg_ref) = refs[n_x:]
    i = pl.program_id(0)
    tm = xo_ref.shape[0]
    f32 = jnp.float32

    @pl.when(i == 0)
    def _():
        cnt_ref[...] = jnp.zeros_like(cnt_ref)
        lg_ref[...] = jnp.zeros_like(lg_ref)

    lat = jnp.concatenate([oa_ref[...], ob_ref[...], oc_ref[...]], axis=1)
    mix = jnp.dot(lat, w_ref[...], preferred_element_type=f32)
    x = _load_x(x_refs, n_lat_tiles, tile=jnp.minimum(i, n_tiles - 1)) + mod_ref[0, 2:3, :] * mix
    xo_ref[...] = x
    r = lax.rsqrt(jnp.mean(x * x, axis=-1, keepdims=True) + EPS)
    h = (x * r) * (g_ref[...] * (1.0 + mod_ref[0, 4:5, :])) + mod_ref[0, 3:4, :]
    h_hi = h.astype(jnp.bfloat16)
    hp_ref[...] = _pack_pairs(h_hi)
    h_lo = (h - h_hi.astype(f32)).astype(jnp.bfloat16)
    r_hi = jnp.dot(h_hi, wr_ref[...], preferred_element_type=f32)
    r_lo = jnp.dot(h_lo, wr_ref[:, 0:LANES], preferred_element_type=f32)
    lg = lg_ref[(i + 1) % 2]
    lg_ref[i % 2] = r_hi[:, 0:LANES] + r_hi[:, LANES:2 * LANES] + r_lo + br_ref[...]

    lane = lax.broadcasted_iota(jnp.int32, (tm, LANES), 1).astype(f32)
    big = float(LANES)
    ninf = -jnp.inf
    glog = jnp.where(lane < N_GROUPS, lg, ninf)
    gmax = glog.max(axis=-1, keepdims=True)
    g_val = 1.0 / jnp.exp(glog - gmax).sum(axis=-1, keepdims=True)
    g_idx = jnp.where(glog == gmax, lane, big).min(axis=-1, keepdims=True)
    e_lo = N_GROUPS + EXPERTS_PER_GROUP * g_idx
    el = jnp.where((lane >= e_lo) & (lane < e_lo + EXPERTS_PER_GROUP), lg, ninf)
    v0 = el.max(axis=-1, keepdims=True)
    i0 = jnp.where(el == v0, lane, big).min(axis=-1, keepdims=True)
    el1 = jnp.where(lane == i0, ninf, el)
    v1 = el1.max(axis=-1, keepdims=True)
    i1 = jnp.where(el1 == v1, lane, big).min(axis=-1, keepdims=True)
    t = jnp.exp(v1 - v0)
    w0 = g_val / (1.0 + t)
    w1 = g_val * t / (1.0 + t)
    e0 = i0 - N_GROUPS
    e1 = i1 - N_GROUPS

    cnt = cnt_ref[0:1, :]
    tri = tri_ref[...]
    ranks = []
    for e in (e0, e1):
        oh = lane == e
        ohf = oh.astype(f32)
        pre = jnp.dot(tri, ohf.astype(jnp.bfloat16), preferred_element_type=f32) + cnt
        ranks.append(jnp.where(oh, pre, 0.0).sum(axis=-1, keepdims=True))
        cnt = cnt + ohf.sum(axis=0, keepdims=True)
    cnt_ref[0:1, :] = jnp.where(i > 0, cnt, cnt_ref[0:1, :])
    rw_ref[...] = jnp.where(lane == 0, w0, jnp.where(lane == 1, w1, 0.0))
    rec = jnp.where(lane == 0, e0, jnp.where(lane == 1, e1, jnp.where(lane == 2, ranks[0],
                    jnp.where(lane == 3, ranks[1], jnp.where(lane == 4, w0, jnp.where(lane == 5, w1, 0.0))))))
    meta_ref[...] = rec.T[0:META_ROWS, :]


def _outproj(xs, oa, ob, oc, mods, g2, w_out_bf, wr, br, tri, *, n, tm, n_tiles, n_lat_tiles, s, nb,
             x_offsets=(0, 0)):
    d = xs[0].shape[1]
    rows = n_tiles * tm
    cur = lambda i: jnp.minimum(i, n_tiles - 1)
    prev = lambda i: jnp.maximum(i - 1, 0)

    def bidx(i):
        return jnp.where(cur(i) < n_lat_tiles, (cur(i) * tm) // s, nb)

    row = lambda w: pl.BlockSpec((tm, w), lambda i: (cur(i), 0))
    const = lambda a: pl.BlockSpec(a.shape, lambda i: (0,) * a.ndim)
    return pl.pallas_call(
        functools.partial(_outproj_kernel, n_x=len(xs), n_lat_tiles=n_lat_tiles, n_tiles=n_tiles),
        grid=(n_tiles + 1,),
        in_specs=_x_specs(xs, tm, n_lat_tiles, x_offsets, cur) + [
                  row(A_Q), row(B_CH), row(C_Q),
                  pl.BlockSpec((1, N_MOD, d), lambda i: (bidx(i), 0, 0)),
                  const(g2), const(w_out_bf), const(wr), const(br), const(tri)],
        out_specs=[row(d), row(d // 2),
                   pl.BlockSpec((META_ROWS, tm), lambda i: (0, prev(i))),
                   pl.BlockSpec((tm, LANES), lambda i: (prev(i), 0)),
                   pl.BlockSpec((8, LANES), lambda i: (0, 0))],
        out_shape=[jax.ShapeDtypeStruct((n, d), jnp.float32),
                   jax.ShapeDtypeStruct((rows, d // 2), jnp.int32),
                   jax.ShapeDtypeStruct((META_ROWS, rows), jnp.float32),
                   jax.ShapeDtypeStruct((rows, LANES), jnp.float32),
                   jax.ShapeDtypeStruct((8, LANES), jnp.float32)],
        scratch_shapes=[pltpu.VMEM((2, tm, LANES), jnp.float32)],
        input_output_aliases={0: 0} if len(xs) == 1 else {},
        compiler_params=_cparams("arbitrary"),
        name="outproj_router",
    )(*xs, oa, ob, oc, mods, g2, w_out_bf, wr, br, tri)


def _sc_mesh():
    return plsc.VectorSubcoreMesh(core_axis_name="core", subcore_axis_name="subcore")


def sc_gather_rows(table, idx2):
    r = idx2.shape[1]
    w = table.shape[1]
    assert r % (2 * SC_WINDOW) == 0
    half = r // SC_WINDOW // 2

    @functools.partial(pl.kernel, out_type=jax.ShapeDtypeStruct((r, w), table.dtype), mesh=_sc_mesh())
    def k(x_hbm, i_hbm, o_hbm):
        def body(i_vmem, o_vmem):
            pltpu.sync_copy(x_hbm.at[i_vmem.at[0]], o_vmem)

        pltpu.emit_pipeline(
            body,
            grid=(2, half),
            in_specs=[pl.BlockSpec((1, SC_WINDOW), lambda c, i: (0, c * half + i))],
            out_specs=[pl.BlockSpec((SC_WINDOW, w), lambda c, i: (c * half + i, 0),
                                    pipeline_mode=pl.Buffered(1))],
            core_axis_name=("core", "subcore"),
            dimension_semantics=(pltpu.PARALLEL, pltpu.PARALLEL),
        )(i_hbm, o_hbm)

    return k(table, idx2)


def sc_scatter_rows2(rows, idx2, n_out):
    r, w = rows.shape
    assert idx2.shape == (1, 2 * r) and r % (2 * SC_WINDOW) == 0
    windows = r // SC_WINDOW
    half = windows // 2

    @functools.partial(pl.kernel, out_type=jax.ShapeDtypeStruct((n_out, w), rows.dtype), mesh=_sc_mesh(),
                       scratch_types=[])
    def k(x_hbm, ia_hbm, ib_hbm, o_hbm):
        def body(x_vmem, ia_vmem, ib_vmem):
            pltpu.sync_copy(x_vmem, o_hbm.at[ia_vmem.at[0]])
            pltpu.sync_copy(x_vmem, o_hbm.at[ib_vmem.at[0]])

        pltpu.emit_pipeline(
            body,
            grid=(2, half),
            in_specs=[pl.BlockSpec((SC_WINDOW, w), lambda c, i: (c * half + i, 0),
                                   pipeline_mode=pl.Buffered(1)),
                      pl.BlockSpec((1, SC_WINDOW), lambda c, i: (0, c * half + i)),
                      pl.BlockSpec((1, SC_WINDOW), lambda c, i: (0, windows + c * half + i))],
            out_specs=[],
            core_axis_name=("core", "subcore"),
            dimension_semantics=(pltpu.PARALLEL, pltpu.PARALLEL),
        )(x_hbm, ia_hbm, ib_hbm)

    return k(rows, idx2, idx2)


def _expert_kernel(te_ref, nv_ref, x_ref, wg_ref, wu_ref, wd_ref, o_ref, wgb_ref, wub_ref, wdb_ref):
    t = pl.program_id(0)
    nvalid = nv_ref[t]

    @pl.when((t == 0) | (te_ref[t] != te_ref[jnp.maximum(t - 1, 0)]))
    def _():
        wgb_ref[...] = wg_ref[0].astype(jnp.bfloat16)
        wub_ref[...] = wu_ref[0].astype(jnp.bfloat16)
        wdb_ref[...] = wd_ref[0].astype(jnp.bfloat16)

    @pl.when(nvalid > 0)
    def _():
        rows = lax.broadcasted_iota(jnp.int32, x_ref.shape, 0)
        lo, hi = _unpack_pairs(jnp.where(rows < nvalid, x_ref[...], 0))
        xb = jnp.concatenate([lo, hi], axis=1).astype(jnp.bfloat16)
        g = jnp.dot(xb, wgb_ref[...], preferred_element_type=jnp.float32)
        u = jnp.dot(xb, wub_ref[...], preferred_element_type=jnp.float32)
        a = (g * jax.nn.sigmoid(g) * u).astype(jnp.bfloat16)
        o_ref[...] = _pack_pairs(jnp.dot(a, wdb_ref[...], preferred_element_type=jnp.float32))

    @pl.when(nvalid == 0)
    def _():
        o_ref[...] = jnp.zeros_like(o_ref)


def _experts(buf, tile_expert, tile_nvalid, wg, wu, wd, *, layer):
    rows, wp = buf.shape
    _, _, d, f = wg.shape
    n_tiles = rows // EXPERT_TILE
    grid_spec = pltpu.PrefetchScalarGridSpec(
        num_scalar_prefetch=2,
        grid=(n_tiles,),
        in_specs=[pl.BlockSpec((EXPERT_TILE, wp), lambda t, te, nv: (t, 0)),
                  pl.BlockSpec((None, 1, d, f), lambda t, te, nv: (layer, te[t], 0, 0)),
                  pl.BlockSpec((None, 1, d, f), lambda t, te, nv: (layer, te[t], 0, 0)),
                  pl.BlockSpec((None, 1, f, d), lambda t, te, nv: (layer, te[t], 0, 0))],
        out_specs=pl.BlockSpec((EXPERT_TILE, wp), lambda t, te, nv: (t, 0)),
        scratch_shapes=[pltpu.VMEM((d, f), jnp.bfloat16), pltpu.VMEM((d, f), jnp.bfloat16),
                        pltpu.VMEM((f, d), jnp.bfloat16)],
    )
    return pl.pallas_call(
        _expert_kernel,
        grid_spec=grid_spec,
        out_shape=jax.ShapeDtypeStruct((rows, wp), jnp.int32),
        compiler_params=_cparams("arbitrary"),
        name="expert_ffn",
    )(tile_expert, tile_nvalid, buf, wg, wu, wd)


def _final_kernel(y0_ref, y1_ref, rw_ref, x_ref, mod_ref, fg_ref, o_ref):
    x = x_ref[...] + mod_ref[0, 5:6, :] * _moe_mix(y0_ref, y1_ref, rw_ref, slice(None))
    r = lax.rsqrt(jnp.mean(x * x, axis=-1, keepdims=True) + EPS)
    o_ref[...] = x * r * fg_ref[...]


def _final_kernel_into(y0_ref, y1_ref, rw_ref, x_ref, mod_ref, fg_ref, prev_ref, o_ref):
    del prev_ref
    _final_kernel(y0_ref, y1_ref, rw_ref, x_ref, mod_ref, fg_ref, o_ref)


def _final_combine(y, rw, xall, mods, final_g, *, tm, n_tiles, s, out_rows, out_tile0, prev_out=None):
    d = xall.shape[1]
    row = lambda w: pl.BlockSpec((tm, w), lambda i: (i, 0))
    in_specs = [row(d // 2),
                pl.BlockSpec((tm, d // 2), lambda i: (i + n_tiles, 0)),
                row(LANES), row(d),
                pl.BlockSpec((1, N_MOD, d), lambda i: ((i * tm) // s, 0, 0)),
                pl.BlockSpec(final_g.shape, lambda i: (0, 0))]
    args = [y, y, rw, xall, mods, final_g]
    aliases = {}
    body = _final_kernel
    if prev_out is not None:
        in_specs.append(pl.BlockSpec(memory_space=pl.ANY))
        args.append(prev_out)
        aliases = {len(args) - 1: 0}
        body = _final_kernel_into
    return pl.pallas_call(
        body,
        grid=(n_tiles,),
        in_specs=in_specs,
        out_specs=pl.BlockSpec((tm, d), lambda i: (out_tile0 + i, 0)),
        out_shape=jax.ShapeDtypeStruct((out_rows, d), jnp.float32),
        input_output_aliases=aliases,
        compiler_params=_cparams("arbitrary"),
        name="moe_combine_final",
    )(*args)


def _dest_kernel(ps_ref, meta_ref, o_ref):
    slot = pl.program_id(0)
    e = meta_ref[pl.ds(slot, 1), :]
    d = meta_ref[pl.ds(TOP_K + slot, 1), :]
    for k in range(N_EXPERTS):
        d = d + jnp.where(e == float(k), ps_ref[k], 0.0)
    o_ref[...] = d.astype(jnp.int32)


def _dest_rows(meta, pstarts, tcols):
    n = meta.shape[1]
    nt = n // tcols
    return pl.pallas_call(
        _dest_kernel,
        grid=(TOP_K, nt),
        in_specs=[pl.BlockSpec(memory_space=pltpu.SMEM),
                  pl.BlockSpec((META_ROWS, tcols), lambda k, i: (0, i))],
        out_specs=pl.BlockSpec((1, tcols), lambda k, i: (0, k * nt + i)),
        out_shape=jax.ShapeDtypeStruct((1, TOP_K * n), jnp.int32),
        compiler_params=_cparams("arbitrary", "arbitrary"),
        name="moe_dest",
    )(pstarts.astype(jnp.float32), meta)


def _dispatch_plan(meta, counts, n_rows_buf, tcols):
    cnt = counts[0, :N_EXPERTS].astype(jnp.int32)
    padded = (cnt + EXPERT_TILE - 1) // EXPERT_TILE * EXPERT_TILE
    pends = jnp.cumsum(padded)
    pstarts = pends - padded
    dest = _dest_rows(meta, pstarts, tcols)
    tile_start = jnp.arange(n_rows_buf // EXPERT_TILE, dtype=jnp.int32) * EXPERT_TILE
    te = jnp.sum((tile_start[:, None] >= pends[None, :]).astype(jnp.int32), axis=1)
    te = jnp.minimum(te, N_EXPERTS - 1)
    onehot = te[:, None] == jnp.arange(N_EXPERTS, dtype=jnp.int32)[None, :]
    cnt_te = jnp.sum(jnp.where(onehot, cnt[None, :], 0), axis=1)
    pstart_te = jnp.sum(jnp.where(onehot, pstarts[None, :], 0), axis=1)
    nvalid = jnp.clip(cnt_te - (tile_start - pstart_te), 0, EXPERT_TILE).astype(jnp.int32)
    return dest, te, nvalid


def _permute_heads(w, axis):
    heads = [lax.slice_in_dim(w, HEAD_DIM * h, HEAD_DIM * (h + 1), axis=axis)
             for h in HEAD_PERM[::HEAD_DIM] // HEAD_DIM]
    return jnp.concatenate(heads, axis=axis)


def _rope_tables(s, tm):
    pos = np.arange(s)
    pos_row = jnp.asarray(pos // GRID_W, jnp.float32)
    pos_col = jnp.asarray(pos % GRID_W, jnp.float32)
    n_freq = HEAD_DIM // 4
    inv = ROPE_THETA ** (-jnp.arange(n_freq, dtype=jnp.float32) / n_freq)
    ang_row = pos_row[:, None] * inv
    ang_col = pos_col[:, None] * inv
    ang = jnp.concatenate([ang_row, ang_row, ang_col, ang_col] * (LANES // HEAD_DIM), axis=-1)
    sign = np.where((np.arange(LANES) % 32) < 16, -1.0, 1.0).astype(np.float32)
    cos_t = jnp.concatenate([jnp.cos(ang), jnp.ones((tm, LANES), jnp.float32)], axis=0)
    sin_t = jnp.concatenate([jnp.sin(ang) * sign, jnp.zeros((tm, LANES), jnp.float32)], axis=0)
    return cos_t, sin_t


def kernel(x, c, ctx, c_ctx, norm1_g, norm2_g, w_mod, b_mod, w_in, q_norm_g, k_norm_g, conv_w, conv_b, conv_ln_g, conv_ln_b, sink, w_out, w_group, b_group, w_expert, b_expert, w_gate, w_up, w_down, final_g):
    nb, s, d = x.shape
    l = ctx.shape[1]
    depth = w_in.shape[0]
    assert w_in.shape[2] == D_IN and w_out.shape[1] == D_MIX
    assert s % GRID_W == 0 and s >= 3 * WINDOW and s % WINDOW == 0 and l % WINDOW == 0
    groups = BATCH_GROUPS if nb % BATCH_GROUPS == 0 else 1
    nbg = nb // groups
    n_lat, n_ctx = nbg * s, nbg * l
    tm = _pick(np.gcd(s, n_ctx), (512, 256, 128))
    tq = _pick(s, (512, 256, 128))
    assert n_lat % l == 0
    bf = jnp.bfloat16
    f32 = jnp.float32

    x2d = x.reshape(nb * s, d)
    ctx2d = ctx.reshape(nb * l, d)
    c_all = jnp.concatenate([c, c_ctx[None, :]], axis=0)
    mods_all = _modulation(c_all, w_mod, b_mod).reshape(depth, nb + 1, N_MOD, d)
    cos_t, sin_t = _rope_tables(s, tm)
    head_id = np.arange(LANES) // HEAD_DIM
    gsum = jnp.asarray((head_id[:, None] == head_id[None, :]) / HEAD_DIM, bf)
    tri = jnp.asarray(np.tril(np.ones((tm, tm), np.float32), -1), bf)
    n_lat_tiles = n_lat // tm

    xs = [(x2d, ctx2d)] * groups
    pending = [None] * groups
    for i in range(depth):
        last = i == depth - 1
        with_ctx = not last
        qg = jnp.tile(q_norm_g[i], LANES // HEAD_DIM)[None, :]
        kg = jnp.tile(k_norm_g[i], LANES // HEAD_DIM)[None, :]
        w_in_bf = jnp.concatenate(
            [_permute_heads(w_in[i][:, OFF_AQ:OFF_AK], 1), w_in[i][:, OFF_AK:OFF_CQ],
             _permute_heads(w_in[i][:, OFF_CQ:OFF_CK], 1), w_in[i][:, OFF_CK:]], axis=1).astype(bf)
        w_out_bf = jnp.concatenate(
            [_permute_heads(w_out[i][0:A_Q], 0), w_out[i][A_Q:A_Q + B_CH],
             _permute_heads(w_out[i][A_Q + B_CH:], 0)], axis=0).astype(bf)
        sink2 = sink[i] * LOG2E
        conv_args = (conv_w[i].reshape(CONV_W, B_CH), conv_b[i][None, :], conv_ln_g[i][None, :],
                     conv_ln_b[i][None, :])
        wr32 = jnp.zeros((d, LANES), f32).at[:, :N_GROUPS].set(w_group[i])
        wr32 = wr32.at[:, N_GROUPS:N_GROUPS + N_EXPERTS].set(w_expert[i])
        wr_hi = wr32.astype(bf)
        wr = jnp.concatenate([wr_hi, (wr32 - wr_hi.astype(f32)).astype(bf)], axis=1)
        br = jnp.zeros((1, LANES), f32).at[0, :N_GROUPS].set(b_group[i])
        br = br.at[0, N_GROUPS:N_GROUPS + N_EXPERTS].set(b_expert[i])
        n_tok = n_lat + n_ctx if with_ctx else n_lat
        n_tiles = n_tok // tm
        n_rows_buf = 2 * n_tok + N_EXPERTS * EXPERT_TILE
        for g in range(groups):
            b0 = g * nbg
            mods = jnp.concatenate([mods_all[i, b0:b0 + nbg], mods_all[i, nb:nb + 1]], axis=0)
            x_offsets = (b0 * s // tm, b0 * l // tm)
            x_new, qa, kta, va, hgl, qc, ktc, vc = _inproj(
                xs[g], mods, norm1_g[i][None, :], w_in_bf, gsum, qg, kg, cos_t, sin_t, n=n_lat + n_ctx,
                tm=tm, n_lat_tiles=n_lat_tiles, s=s, nb=nbg, x_offsets=x_offsets, combine=pending[g])
            if pending[g] is not None:
                xs[g] = (x_new,)
            oa, ob, oc = _latent_mixers(qa, kta, va, qc, ktc, vc, sink2, hgl, conv_args,
                                        nb=nbg, s=s, l=l, tq=tq)
            if with_ctx:
                oa, ob, oc = _ctx_mixers(qa, kta, va, qc, ktc, vc, sink2, hgl, conv_args, oa, ob, oc,
                                         nb=nbg, s=s, l=l)
            xall, hp, meta, rw, counts = _outproj(xs[g], oa, ob, oc, mods, norm2_g[i][None, :], w_out_bf,
                                                  wr, br, tri, n=n_lat + n_ctx, tm=tm, n_tiles=n_tiles,
                                                  n_lat_tiles=n_lat_tiles, s=s, nb=nbg, x_offsets=x_offsets)
            dest, te, nvalid = _dispatch_plan(meta, counts, n_rows_buf,
                                              _pick(n_tok, (8192, 4096, 2048, 1024, 512, 256, 128)))
            buf = sc_scatter_rows2(hp, dest, n_rows_buf)
            eo = _experts(buf, te, nvalid, w_gate, w_up, w_down, layer=i)
            y = sc_gather_rows(eo, dest)
            xs[g] = (xall,)
            pending[g] = (y, rw, mods)
    out = None
    for g in range(groups):
        y, rw, mods = pending[g]
        out = _final_combine(y, rw, xs[g][0], mods, final_g[None, :], tm=tm, n_tiles=n_lat_tiles, s=s,
                             out_rows=nb * s, out_tile0=g * n_lat_tiles, prev_out=out)
    return out.reshape(nb, s, d)
```

```python
import functools

import jax
import jax.numpy as jnp
import numpy as np
from jax import lax
from jax.experimental import pallas as pl
from jax.experimental.pallas import tpu as pltpu
from jax.experimental.pallas import tpu_sc as plsc

HEAD_DIM = 64
GRID_W = 64
ROPE_THETA = 10000.0
A_HEADS, A_KV_HEADS = 6, 2
C_HEADS, C_KV_HEADS = 6, 2
B_CH = 256
CONV_W = 31
WINDOW = 128
N_GROUPS = 4
EXPERTS_PER_GROUP = 8
N_EXPERTS = N_GROUPS * EXPERTS_PER_GROUP
TOP_K = 2
N_MOD = 6
EPS = 1e-6
ATTN_SCALE = HEAD_DIM ** -0.5
LOG2E = 1.4426950408889634
Q_SCALE = ATTN_SCALE * LOG2E

A_Q = A_HEADS * HEAD_DIM
A_KV = A_KV_HEADS * HEAD_DIM
C_Q = C_HEADS * HEAD_DIM
C_KV = C_KV_HEADS * HEAD_DIM
D_MIX = A_Q + B_CH + C_Q
OFF_AQ = 0
OFF_AK = OFF_AQ + A_Q
OFF_AV = OFF_AK + A_KV
OFF_BU = OFF_AV + A_KV
OFF_CQ = OFF_BU + 2 * B_CH
OFF_CK = OFF_CQ + C_Q
OFF_CV = OFF_CK + C_KV
D_IN = OFF_CV + C_KV

LANES = 128
SUBLANES = 8
KT_ROWS = LANES
V_W = 2 * LANES
HEAD_PERM = np.concatenate([np.arange(HEAD_DIM) + HEAD_DIM * h for b in range(3) for h in (b, b + 3)])
EXPERT_TILE = 768
SC_WINDOW = 128
VMEM_LIMIT = 56 * 1024 * 1024
HI_MASK = -65536
BATCH_GROUPS = 2
ROW_CHAIN = 256


def _cparams(*sem):
    return pltpu.CompilerParams(dimension_semantics=sem, vmem_limit_bytes=VMEM_LIMIT)


def _pick(n, cands):
    for c in cands:
        if n % c == 0:
            return c
    raise ValueError(f"no tile in {cands} divides {n}")


def _pack_pairs(x):
    w = x.shape[1] // 2
    lo = lax.bitcast_convert_type(x[:, :w].astype(jnp.bfloat16).astype(jnp.float32), jnp.int32)
    hi = lax.bitcast_convert_type(x[:, w:].astype(jnp.bfloat16).astype(jnp.float32), jnp.int32)
    return (hi & HI_MASK) | lax.shift_right_logical(lo, 16)


def _unpack_pairs(p):
    lo = lax.bitcast_convert_type(lax.shift_left(p, 16), jnp.float32)
    hi = lax.bitcast_convert_type(p & HI_MASK, jnp.float32)
    return lo, hi


def _mod_kernel(c_ref, w_ref, b_ref, o_ref):
    c = c_ref[...]
    a = c * jax.nn.sigmoid(c)
    o_ref[0] = jnp.dot(a, w_ref[0], preferred_element_type=jnp.float32,
                       precision=lax.Precision.HIGHEST) + b_ref[0]


def _modulation(c_all, w_mod, b_mod):
    depth, d, n = w_mod.shape
    r = c_all.shape[0]
    tn = _pick(n, (1024, 512, 256, 128))
    return pl.pallas_call(
        _mod_kernel,
        grid=(depth, n // tn),
        in_specs=[pl.BlockSpec((r, d), lambda l, j: (0, 0)),
                  pl.BlockSpec((1, d, tn), lambda l, j: (l, 0, j)),
                  pl.BlockSpec((1, 1, tn), lambda l, j: (l, 0, j))],
        out_specs=pl.BlockSpec((1, r, tn), lambda l, j: (l, 0, j)),
        out_shape=jax.ShapeDtypeStruct((depth, r, n), jnp.float32),
        compiler_params=_cparams("arbitrary", "arbitrary"),
        name="modulation",
    )(c_all, w_mod, b_mod.reshape(depth, 1, n))


def _head_mean_sq(blk, gsum):
    sq = blk * blk
    hi = sq.astype(jnp.bfloat16)
    lo = (sq - hi.astype(jnp.float32)).astype(jnp.bfloat16)
    return (jnp.dot(hi, gsum, preferred_element_type=jnp.float32)
            + jnp.dot(lo, gsum, preferred_element_type=jnp.float32))


def _x_specs(xs, tm, n_lat_tiles, offsets=(0, 0), tile_of=lambda i: i):
    d = xs[0].shape[1]
    if len(xs) == 1:
        return [pl.BlockSpec((tm, d), lambda i: (tile_of(i), 0))]
    lat0, ctx0 = offsets
    return [pl.BlockSpec((tm, d), lambda i: (lat0 + jnp.minimum(tile_of(i), n_lat_tiles - 1), 0)),
            pl.BlockSpec((tm, d), lambda i: (ctx0 + jnp.maximum(tile_of(i) - n_lat_tiles, 0), 0))]


def _load_x(x_refs, n_lat_tiles, rows=slice(None), tile=None):
    if len(x_refs) == 1:
        return x_refs[0][rows, :]
    tile = pl.program_id(0) if tile is None else tile
    return jnp.where(tile < n_lat_tiles, x_refs[0][rows, :], x_refs[1][rows, :])


def _moe_mix(y0_ref, y1_ref, rw_ref, rows):
    rw = rw_ref[rows, :]
    w0 = rw[:, 0:1]
    w1 = rw[:, 1:2]
    a_lo, a_hi = _unpack_pairs(y0_ref[rows, :])
    b_lo, b_hi = _unpack_pairs(y1_ref[rows, :])
    return jnp.concatenate([a_lo * w0 + b_lo * w1, a_hi * w0 + b_hi * w1], axis=1)


def _inproj_kernel(*refs, n_x, n_lat_tiles, fused_combine):
    x_refs = refs[:n_x]
    refs = refs[n_x:]
    if fused_combine:
        y0_ref, y1_ref, rw_ref, modp_ref = refs[:4]
        refs = refs[4:]
        xo_ref = refs[-1]
        refs = refs[:-1]
    (mod_ref, g_ref, w_ref, gsum_ref, qg_ref, kg_ref, cos_ref, sin_ref,
     qa_ref, kta_ref, va_ref, hgl_ref, qc_ref, ktc_ref, vc_ref) = refs
    tm = qa_ref.shape[0]
    tr = min(tm, ROW_CHAIN)
    bf = jnp.bfloat16
    lane = lax.broadcasted_iota(jnp.int32, (tr, LANES), 1)
    first16 = (lane % 32) < 16
    gsum = gsum_ref[...]
    qg = qg_ref[...]
    kg = kg_ref[...]
    scale = g_ref[...] * (1.0 + mod_ref[0, 1:2, :])
    shift = mod_ref[0, 0:1, :]
    ones = jnp.ones((tr, LANES), bf)

    for r0 in range(0, tm, tr):
        rows = slice(r0, r0 + tr)
        x = _load_x(x_refs, n_lat_tiles, rows)
        if fused_combine:
            x = x + modp_ref[0, 5:6, :] * _moe_mix(y0_ref, y1_ref, rw_ref, rows)
            xo_ref[rows, :] = x
        r = lax.rsqrt(jnp.mean(x * x, axis=-1, keepdims=True) + EPS)
        h = (x * r) * scale + shift
        p = jnp.dot(h.astype(bf), w_ref[...], preferred_element_type=jnp.float32)
        cos = cos_ref[rows, :]
        sin = sin_ref[rows, :]

        def blk(off):
            return p[:, off:off + LANES]

        def rope(t):
            sw = jnp.where(first16, pltpu.roll(t, LANES - 16, axis=1), pltpu.roll(t, 16, axis=1))
            return t * cos + sw * sin

        def norm(t, g):
            return t * lax.rsqrt(_head_mean_sq(t, gsum) + EPS) * g

        def store_kv(kt_ref, v_ref, k_blk, v_blk):
            kt_ref[:, rows] = k_blk.T.astype(bf)
            v_ref[rows, 0:LANES] = v_blk.astype(bf)
            v_ref[rows, LANES:2 * LANES] = ones

        for i in range(A_Q // LANES):
            t = rope(norm(blk(OFF_AQ + i * LANES), qg)) * Q_SCALE
            qa_ref[rows, i * LANES:(i + 1) * LANES] = t.astype(bf)
        store_kv(kta_ref, va_ref, rope(norm(blk(OFF_AK), kg)), blk(OFF_AV))
        for i in range(B_CH // LANES):
            a = blk(OFF_BU + i * LANES)
            gt = blk(OFF_BU + B_CH + i * LANES)
            hgl_ref[rows, i * LANES:(i + 1) * LANES] = a * jax.nn.sigmoid(gt)
        for i in range(C_Q // LANES):
            t = rope(blk(OFF_CQ + i * LANES)) * Q_SCALE
            qc_ref[rows, i * LANES:(i + 1) * LANES] = t.astype(bf)
        store_kv(ktc_ref, vc_ref, rope(blk(OFF_CK)), blk(OFF_CV))


def _inproj(xs, mods, g1, w_in_bf, gsum, qg, kg, cos_t, sin_t, *, n, tm, n_lat_tiles, s, nb, x_offsets=(0, 0),
            combine=None):
    d = xs[0].shape[1]
    s_tiles = s // tm
    n_tiles = n // tm

    def bidx(i):
        return jnp.where(i < n_lat_tiles, (i * tm) // s, nb)

    def ridx(i):
        return jnp.where(i < n_lat_tiles, i % s_tiles, s_tiles)

    row = lambda w: pl.BlockSpec((tm, w), lambda i: (i, 0))
    ktspec = pl.BlockSpec((KT_ROWS, tm), lambda i: (0, i))
    const = lambda a: pl.BlockSpec(a.shape, lambda i: (0,) * a.ndim)
    bf = jnp.bfloat16
    modspec = pl.BlockSpec((1, N_MOD, d), lambda i: (bidx(i), 0, 0))
    in_specs = _x_specs(xs, tm, n_lat_tiles, x_offsets)
    args = list(xs)
    out_specs = [row(A_Q), ktspec, row(V_W), row(B_CH), row(C_Q), ktspec, row(V_W)]
    out_shape = [jax.ShapeDtypeStruct((n, A_Q), bf), jax.ShapeDtypeStruct((KT_ROWS, n), bf),
                 jax.ShapeDtypeStruct((n, V_W), bf), jax.ShapeDtypeStruct((n, B_CH), jnp.float32),
                 jax.ShapeDtypeStruct((n, C_Q), bf), jax.ShapeDtypeStruct((KT_ROWS, n), bf),
                 jax.ShapeDtypeStruct((n, V_W), bf)]
    aliases = {}
    if combine is not None:
        y, rw, mods_prev = combine
        assert len(xs) == 1 and y.shape[0] == 2 * n
        in_specs += [row(d // 2), pl.BlockSpec((tm, d // 2), lambda i: (i + n_tiles, 0)), row(LANES), modspec]
        args += [y, y, rw, mods_prev]
        out_specs.append(row(d))
        out_shape.append(jax.ShapeDtypeStruct((n, d), jnp.float32))
        aliases = {0: len(out_shape) - 1}
    in_specs += [modspec, const(g1), const(w_in_bf), const(gsum), const(qg), const(kg),
                 pl.BlockSpec((tm, LANES), lambda i: (ridx(i), 0)),
                 pl.BlockSpec((tm, LANES), lambda i: (ridx(i), 0))]
    args += [mods, g1, w_in_bf, gsum, qg, kg, cos_t, sin_t]
    outs = pl.pallas_call(
        functools.partial(_inproj_kernel, n_x=len(xs), n_lat_tiles=n_lat_tiles,
                          fused_combine=combine is not None),
        grid=(n_tiles,),
        in_specs=in_specs,
        out_specs=out_specs,
        out_shape=out_shape,
        input_output_aliases=aliases,
        compiler_params=_cparams("arbitrary"),
        name="inproj",
    )(*args)
    if combine is not None:
        return (outs[-1],) + tuple(outs[:-1])
    return (None,) + tuple(outs)


def _stack_heads(q_ref, tq):
    lane = lax.broadcasted_iota(jnp.int32, (tq, LANES), 1)
    lo = lane < HEAD_DIM
    qb = [q_ref[:, i * LANES:(i + 1) * LANES] for i in range(3)]
    zero = jnp.zeros_like(qb[0])
    s0 = jnp.concatenate([jnp.where(lo, t, zero) for t in qb], axis=0)
    s1 = jnp.concatenate([jnp.where(lo, zero, t) for t in qb], axis=0)
    return s0, s1, lo


def _unstack_store(o_ref, o0, o1, lo, tq):
    for i in range(3):
        rows = slice(i * tq, (i + 1) * tq)
        o_ref[:, i * LANES:(i + 1) * LANES] = jnp.where(lo, o0[rows], o1[rows]).astype(jnp.bfloat16)


def _krow(kv):
    del kv
    return slice(0, KT_ROWS)


def _vcol(kv):
    del kv
    return slice(0, V_W)


def _row_max(scores):
    m = None
    for t in scores:
        for c in range(0, t.shape[1], LANES):
            blk = t[:, c:c + LANES]
            m = blk if m is None else jnp.maximum(m, blk)
    return m.max(axis=-1, keepdims=True)


def _softmax_pv(scores, values, extra=None):
    m = _row_max(scores)
    if extra is not None:
        m = jnp.maximum(m, extra)
    acc = None
    for t, v in zip(scores, values):
        c = jnp.dot(jnp.exp2(t - m).astype(jnp.bfloat16), v, preferred_element_type=jnp.float32)
        acc = c if acc is None else acc + c
    den = acc[:, LANES:2 * LANES]
    if extra is not None:
        den = den + jnp.exp2(extra - m)
    return acc[:, 0:LANES] / den


ATTN_CHAIN_ROWS = 128


def _qk(q, kt):
    return jnp.dot(q, kt, preferred_element_type=jnp.float32)


def _attn_a_kernel(q_ref, ktl_ref, ktc_ref, vl_ref, vc_ref, *rest, tq, n_q):
    conv_refs = rest[:7]
    o_ref, ob_ref, sh_ref = rest[7:]
    j = pl.program_id(1)
    conv_steps = _conv_steps(conv_refs, ob_ref, sh_ref, j > 0, j < n_q - 1, tq)
    s0, s1, lo = _stack_heads(q_ref, tq)
    n_chains = 2 * (3 * tq // ATTN_CHAIN_ROWS)
    chain = 0
    conv_done = 0
    outs = []
    for kv, qs in enumerate((s0, s1)):
        krow = _krow(kv)
        vcol = _vcol(kv)
        parts = []
        for r0 in range(0, 3 * tq, ATTN_CHAIN_ROWS):
            qr = qs[r0:r0 + ATTN_CHAIN_ROWS]
            ss = [_qk(qr, ktl_ref[krow, :]), _qk(qr, ktc_ref[krow, :])]
            parts.append(_softmax_pv(ss, [vl_ref[:, vcol], vc_ref[:, vcol]]))
            chain += 1
            while conv_done < chain * len(conv_steps) // n_chains:
                conv_steps[conv_done]()
                conv_done += 1
        outs.append(jnp.concatenate(parts, axis=0))
    _unstack_store(o_ref, outs[0], outs[1], lo, tq)


def _kv_specs(nb, s, l):
    ctx0 = nb * s // l
    return [pl.BlockSpec((KT_ROWS, s), lambda b, j: (0, b)),
            pl.BlockSpec((KT_ROWS, l), lambda b, j: (0, ctx0 + b)),
            pl.BlockSpec((s, V_W), lambda b, j: (b, 0)),
            pl.BlockSpec((l, V_W), lambda b, j: (ctx0 + b, 0))]


def _sink_column(sink_ref, kv, rows):
    return jnp.concatenate([jnp.full((rows, 1), sink_ref[3 * kv + g], jnp.float32) for g in range(3)], axis=0)


def _attn_c_kernel(sink_ref, bias_ref, q_ref, ktl_ref, ktc_ref, vl_ref, vc_ref, o_ref, *, s, blocks):
    j = pl.program_id(1)
    tq = WINDOW
    band = 3 * WINDOW
    bf = jnp.bfloat16
    f32 = jnp.float32
    m3 = 3 * tq
    stacks = [_stack_heads(q_ref.at[blk * tq:(blk + 1) * tq, :], tq) for blk in range(blocks)]
    lo = stacks[0][2]
    starts = [pl.multiple_of(jnp.clip((j * blocks + blk - 1) * WINDOW, 0, s - band), WINDOW)
              for blk in range(blocks)]
    outs = [[None, None] for _ in range(blocks)]
    for kv in range(C_KV_HEADS):
        krow = _krow(kv)
        vcol = _vcol(kv)
        sk = _sink_column(sink_ref, kv, tq)
        sc_all = _qk(jnp.concatenate([st[kv] for st in stacks], axis=0), ktc_ref[krow, :])
        accs, pcs, ms = [], [], []
        for blk in range(blocks):
            start = starts[blk]
            sl = _qk(stacks[blk][kv], ktl_ref[krow, pl.ds(start, band)]) + bias_ref[j * blocks + blk - start // WINDOW]
            sc = sc_all[blk * m3:(blk + 1) * m3]
            m = jnp.maximum(_row_max([sl, sc]), sk)
            accs.append(jnp.dot(jnp.exp2(sl - m).astype(bf), vl_ref[pl.ds(start, band), vcol],
                                preferred_element_type=f32))
            pcs.append(jnp.exp2(sc - m).astype(bf))
            ms.append(m)
        acc_ctx = jnp.dot(jnp.concatenate(pcs, axis=0), vc_ref[:, vcol], preferred_element_type=f32)
        for blk in range(blocks):
            acc = accs[blk] + acc_ctx[blk * m3:(blk + 1) * m3]
            den = acc[:, LANES:2 * LANES] + jnp.exp2(sk - ms[blk])
            outs[blk][kv] = acc[:, 0:LANES] / den
    for blk in range(blocks):
        _unstack_store(o_ref.at[blk * tq:(blk + 1) * tq, :], outs[blk][0], outs[blk][1], lo, tq)


def _window_bias():
    r = np.arange(3 * WINDOW)[:, None] % WINDOW
    col = np.arange(3 * WINDOW)[None, :]
    tabs = [np.where(np.abs(col - r - WINDOW * off) <= WINDOW, 0.0, -np.inf) for off in range(3)]
    return jnp.asarray(np.stack(tabs), jnp.float32)


def _mixers_kernel(*refs, tq, n_q, s):
    sink_ref, bias_ref, qc_ref = refs[0:3]
    kvc_refs = refs[3:7]
    qa_ref = refs[7]
    kva_refs = refs[8:12]
    conv_refs = refs[12:19]
    oa_ref, ob_ref, oc_ref, sh_ref = refs[19:]
    _attn_c_kernel(sink_ref, bias_ref, qc_ref, *kvc_refs, oc_ref, s=s, blocks=tq // WINDOW)
    _attn_a_kernel(qa_ref, *kva_refs, *conv_refs, oa_ref, ob_ref, sh_ref, tq=tq, n_q=n_q)


def _latent_mixers(qa, kta, va, qc, ktc, vc, sink2, hgl, conv_params, *, nb, s, l, tq):
    n = qa.shape[0]
    n_q = s // tq
    bias = _window_bias()
    conv_specs, conv_blk = _conv_specs(n, conv_params, chunks=n_q, base=0, tc=tq)
    qspec = lambda w: pl.BlockSpec((tq, w), lambda b, j: (b * n_q + j, 0))
    bf = jnp.bfloat16
    return pl.pallas_call(
        functools.partial(_mixers_kernel, tq=tq, n_q=n_q, s=s),
        grid=(nb, n_q),
        in_specs=[pl.BlockSpec(memory_space=pltpu.SMEM), pl.BlockSpec(bias.shape, lambda b, j: (0, 0, 0)),
                  qspec(C_Q)] + _kv_specs(nb, s, l) + [qspec(A_Q)] + _kv_specs(nb, s, l) + conv_specs,
        out_specs=[qspec(A_Q), conv_blk, qspec(C_Q)],
        out_shape=[jax.ShapeDtypeStruct((n, A_Q), bf), jax.ShapeDtypeStruct((n, B_CH), bf),
                   jax.ShapeDtypeStruct((n, C_Q), bf)],
        scratch_shapes=[pltpu.VMEM((SUBLANES, tq + 2 * CONV_HALO, B_CH), jnp.float32)],
        compiler_params=_cparams("arbitrary", "arbitrary"),
        name="latent_mixers",
    )(sink2, bias, qc, ktc, ktc, vc, vc, qa, kta, kta, va, va, hgl, hgl, hgl, *conv_params)


def _ctx_mixers_kernel(sink_ref, qa_ref, kta_ref, va_ref, qc_ref, ktc_ref, vc_ref, *rest, l):
    conv_refs = rest[:7]
    oa_ref, ob_ref, oc_ref, sh_ref = rest[-4:]
    conv_steps = _conv_steps(conv_refs, ob_ref, sh_ref, False, False, l)
    for q_ref, kt_ref, v_ref, o_ref, with_sink in ((qa_ref, kta_ref, va_ref, oa_ref, False),
                                                   (qc_ref, ktc_ref, vc_ref, oc_ref, True)):
        s0, s1, lo = _stack_heads(q_ref, l)
        outs = []
        for kv, qs in enumerate((s0, s1)):
            sk = _sink_column(sink_ref, kv, l) if with_sink else None
            outs.append(_softmax_pv([_qk(qs, kt_ref[_krow(kv), :])], [v_ref[:, _vcol(kv)]], sk))
            for step in conv_steps[(2 * with_sink + kv) * len(conv_steps) // 4:
                                   (2 * with_sink + kv + 1) * len(conv_steps) // 4]:
                step()
        _unstack_store(o_ref, outs[0], outs[1], lo, l)


def _ctx_mixers(qa, kta, va, qc, ktc, vc, sink2, hgl, conv_params, oa, ob, oc, *, nb, s, l):
    n = qa.shape[0]
    base = nb * s // l
    row = lambda w: pl.BlockSpec((l, w), lambda b, j: (base + b, 0))
    ktspec = pl.BlockSpec((KT_ROWS, l), lambda b, j: (0, base + b))
    conv_specs, conv_blk = _conv_specs(n, conv_params, chunks=1, base=base, tc=l)
    anyspec = pl.BlockSpec(memory_space=pl.ANY)
    n_in = 7 + len(conv_specs)
    return pl.pallas_call(
        functools.partial(_ctx_mixers_kernel, l=l),
        grid=(nb, 1),
        in_specs=[pl.BlockSpec(memory_space=pltpu.SMEM), row(A_Q), ktspec, row(V_W), row(C_Q), ktspec, row(V_W)]
        + conv_specs + [anyspec, anyspec, anyspec],
        out_specs=[row(A_Q), conv_blk, row(C_Q)],
        out_shape=[jax.ShapeDtypeStruct(a.shape, a.dtype) for a in (oa, ob, oc)],
        scratch_shapes=[pltpu.VMEM((SUBLANES, l + 2 * CONV_HALO, B_CH), jnp.float32)],
        input_output_aliases={n_in: 0, n_in + 1: 1, n_in + 2: 2},
        compiler_params=_cparams("arbitrary", "arbitrary"),
        name="context_mixers",
    )(sink2, qa, kta, va, qc, ktc, vc, hgl, hgl, hgl, *conv_params, oa, ob, oc)


CONV_HALO = 16
CONV_ROWS = 64


def _conv_steps(conv_refs, o_ref, sh_ref, has_prev, has_next, tc):
    prev_ref, cur_ref, next_ref, w_ref, b_ref, g_ref, beta_ref = conv_refs
    rows = tc + 2 * CONV_HALO
    sh_ref[0, 0:CONV_HALO, :] = jnp.where(has_prev, prev_ref[...], 0.0)
    sh_ref[0, CONV_HALO:CONV_HALO + tc, :] = cur_ref[...]
    sh_ref[0, CONV_HALO + tc:rows, :] = jnp.where(has_next, next_ref[...], 0.0)
    for b in range(1, SUBLANES):
        sh_ref[b, 0:rows - SUBLANES, :] = sh_ref[0, b:b + rows - SUBLANES, :]
    base = CONV_HALO - CONV_W // 2

    def step(r0):
        acc = None
        for k in range(CONV_W):
            a, b = divmod(base + k, SUBLANES)
            term = sh_ref[b, SUBLANES * a + r0:SUBLANES * a + r0 + CONV_ROWS, :] * w_ref[k:k + 1, :]
            acc = term if acc is None else acc + term
        hc = acc + b_ref[...]
        mu = jnp.mean(hc, axis=-1, keepdims=True)
        xc = hc - mu
        var = jnp.mean(xc * xc, axis=-1, keepdims=True)
        y = xc * lax.rsqrt(var + EPS) * g_ref[...] + beta_ref[...]
        o_ref[r0:r0 + CONV_ROWS, :] = (y * jax.nn.sigmoid(y)).astype(o_ref.dtype)

    return [functools.partial(step, r0) for r0 in range(0, tc, CONV_ROWS)]


def _conv_specs(n, params, *, chunks, base, tc):
    hb = tc // CONV_HALO
    last_halo = n // CONV_HALO - 1
    idx = lambda b_, j: base + b_ * chunks + j
    blk = pl.BlockSpec((tc, B_CH), lambda b_, j: (idx(b_, j), 0))
    prev_halo = pl.BlockSpec((CONV_HALO, B_CH), lambda b_, j: (jnp.maximum(idx(b_, j) * hb - 1, 0), 0))
    next_halo = pl.BlockSpec((CONV_HALO, B_CH), lambda b_, j: (jnp.minimum((idx(b_, j) + 1) * hb, last_halo), 0))
    const = lambda a: pl.BlockSpec(a.shape, lambda b_, j: (0,) * a.ndim)
    return [prev_halo, blk, next_halo] + [const(a) for a in params], blk


META_ROWS = 8


def _outproj_kernel(*refs, n_x, n_lat_tiles, n_tiles):
    x_refs = refs[:n_x]
    (oa_ref, ob_ref, oc_ref, mod_ref, g_ref, w_ref, wr_ref, br_ref, tri_ref,
     xo_ref, hp_ref, meta_ref, rw_ref, cnt_ref, lg_ref) = refs[n_x:]
    i = pl.program_id(0)
    tm = xo_ref.shape[0]
    f32 = jnp.float32

    @pl.when(i == 0)
    def _():
        cnt_ref[...] = jnp.zeros_like(cnt_ref)
        lg_ref[...] = jnp.zeros_like(lg_ref)

    lat = jnp.concatenate([oa_ref[...], ob_ref[...], oc_ref[...]], axis=1)
    mix = jnp.dot(lat, w_ref[...], preferred_element_type=f32)
    x = _load_x(x_refs, n_lat_tiles, tile=jnp.minimum(i, n_tiles - 1)) + mod_ref[0, 2:3, :] * mix
    xo_ref[...] = x
    r = lax.rsqrt(jnp.mean(x * x, axis=-1, keepdims=True) + EPS)
    h = (x * r) * (g_ref[...] * (1.0 + mod_ref[0, 4:5, :])) + mod_ref[0, 3:4, :]
    h_hi = h.astype(jnp.bfloat16)
    hp_ref[...] = _pack_pairs(h_hi)
    h_lo = (h - h_hi.astype(f32)).astype(jnp.bfloat16)
    r_hi = jnp.dot(h_hi, wr_ref[...], preferred_element_type=f32)
    r_lo = jnp.dot(h_lo, wr_ref[:, 0:LANES], preferred_element_type=f32)
    lg = lg_ref[(i + 1) % 2]
    lg_ref[i % 2] = r_hi[:, 0:LANES] + r_hi[:, LANES:2 * LANES] + r_lo + br_ref[...]

    lane = lax.broadcasted_iota(jnp.int32, (tm, LANES), 1).astype(f32)
    big = float(LANES)
    ninf = -jnp.inf
    glog = jnp.where(lane < N_GROUPS, lg, ninf)
    gmax = glog.max(axis=-1, keepdims=True)
    g_val = 1.0 / jnp.exp(glog - gmax).sum(axis=-1, keepdims=True)
    g_idx = jnp.where(glog == gmax, lane, big).min(axis=-1, keepdims=True)
    e_lo = N_GROUPS + EXPERTS_PER_GROUP * g_idx
    el = jnp.where((lane >= e_lo) & (lane < e_lo + EXPERTS_PER_GROUP), lg, ninf)
    v0 = el.max(axis=-1, keepdims=True)
    i0 = jnp.where(el == v0, lane, big).min(axis=-1, keepdims=True)
    el1 = jnp.where(lane == i0, ninf, el)
    v1 = el1.max(axis=-1, keepdims=True)
    i1 = jnp.where(el1 == v1, lane, big).min(axis=-1, keepdims=True)
    t = jnp.exp(v1 - v0)
    w0 = g_val / (1.0 + t)
    w1 = g_val * t / (1.0 + t)
    e0 = i0 - N_GROUPS
    e1 = i1 - N_GROUPS

    cnt = cnt_ref[0:1, :]
    tri = tri_ref[...]
    ranks = []
    for e in (e0, e1):
        oh = lane == e
        ohf = oh.astype(f32)
        pre = jnp.dot(tri, ohf.astype(jnp.bfloat16), preferred_element_type=f32) + cnt
        ranks.append(jnp.where(oh, pre, 0.0).sum(axis=-1, keepdims=True))
        cnt = cnt + ohf.sum(axis=0, keepdims=True)
    cnt_ref[0:1, :] = jnp.where(i > 0, cnt, cnt_ref[0:1, :])
    rw_ref[...] = jnp.where(lane == 0, w0, jnp.where(lane == 1, w1, 0.0))
    rec = jnp.where(lane == 0, e0, jnp.where(lane == 1, e1, jnp.where(lane == 2, ranks[0],
                    jnp.where(lane == 3, ranks[1], jnp.where(lane == 4, w0, jnp.where(lane == 5, w1, 0.0))))))
    meta_ref[...] = rec.T[0:META_ROWS, :]


def _outproj(xs, oa, ob, oc, mods, g2, w_out_bf, wr, br, tri, *, n, tm, n_tiles, n_lat_tiles, s, nb,
             x_offsets=(0, 0)):
    d = xs[0].shape[1]
    rows = n_tiles * tm
    cur = lambda i: jnp.minimum(i, n_tiles - 1)
    prev = lambda i: jnp.maximum(i - 1, 0)

    def bidx(i):
        return jnp.where(cur(i) < n_lat_tiles, (cur(i) * tm) // s, nb)

    row = lambda w: pl.BlockSpec((tm, w), lambda i: (cur(i), 0))
    const = lambda a: pl.BlockSpec(a.shape, lambda i: (0,) * a.ndim)
    return pl.pallas_call(
        functools.partial(_outproj_kernel, n_x=len(xs), n_lat_tiles=n_lat_tiles, n_tiles=n_tiles),
        grid=(n_tiles + 1,),
        in_specs=_x_specs(xs, tm, n_lat_tiles, x_offsets, cur) + [
                  row(A_Q), row(B_CH), row(C_Q),
                  pl.BlockSpec((1, N_MOD, d), lambda i: (bidx(i), 0, 0)),
                  const(g2), const(w_out_bf), const(wr), const(br), const(tri)],
        out_specs=[row(d), row(d // 2),
                   pl.BlockSpec((META_ROWS, tm), lambda i: (0, prev(i))),
                   pl.BlockSpec((tm, LANES), lambda i: (prev(i), 0)),
                   pl.BlockSpec((8, LANES), lambda i: (0, 0))],
        out_shape=[jax.ShapeDtypeStruct((n, d), jnp.float32),
                   jax.ShapeDtypeStruct((rows, d // 2), jnp.int32),
                   jax.ShapeDtypeStruct((META_ROWS, rows), jnp.float32),
                   jax.ShapeDtypeStruct((rows, LANES), jnp.float32),
                   jax.ShapeDtypeStruct((8, LANES), jnp.float32)],
        scratch_shapes=[pltpu.VMEM((2, tm, LANES), jnp.float32)],
        input_output_aliases={0: 0} if len(xs) == 1 else {},
        compiler_params=_cparams("arbitrary"),
        name="outproj_router",
    )(*xs, oa, ob, oc, mods, g2, w_out_bf, wr, br, tri)


def _sc_mesh():
    return plsc.VectorSubcoreMesh(core_axis_name="core", subcore_axis_name="subcore")


def sc_gather_rows(table, idx2):
    r = idx2.shape[1]
    w = table.shape[1]
    assert r % (2 * SC_WINDOW) == 0
    half = r // SC_WINDOW // 2

    @functools.partial(pl.kernel, out_type=jax.ShapeDtypeStruct((r, w), table.dtype), mesh=_sc_mesh())
    def k(x_hbm, i_hbm, o_hbm):
        def body(i_vmem, o_vmem):
            pltpu.sync_copy(x_hbm.at[i_vmem.at[0]], o_vmem)

        pltpu.emit_pipeline(
            body,
            grid=(2, half),
            in_specs=[pl.BlockSpec((1, SC_WINDOW), lambda c, i: (0, c * half + i))],
            out_specs=[pl.BlockSpec((SC_WINDOW, w), lambda c, i: (c * half + i, 0),
                                    pipeline_mode=pl.Buffered(1))],
            core_axis_name=("core", "subcore"),
            dimension_semantics=(pltpu.PARALLEL, pltpu.PARALLEL),
        )(i_hbm, o_hbm)

    return k(table, idx2)


def sc_scatter_rows2(rows, idx2, n_out):
    r, w = rows.shape
    assert idx2.shape == (1, 2 * r) and r % (2 * SC_WINDOW) == 0
    windows = r // SC_WINDOW
    half = windows // 2

    @functools.partial(pl.kernel, out_type=jax.ShapeDtypeStruct((n_out, w), rows.dtype), mesh=_sc_mesh(),
                       scratch_types=[])
    def k(x_hbm, ia_hbm, ib_hbm, o_hbm):
        def body(x_vmem, ia_vmem, ib_vmem):
            pltpu.sync_copy(x_vmem, o_hbm.at[ia_vmem.at[0]])
            pltpu.sync_copy(x_vmem, o_hbm.at[ib_vmem.at[0]])

        pltpu.emit_pipeline(
            body,
            grid=(2, half),
            in_specs=[pl.BlockSpec((SC_WINDOW, w), lambda c, i: (c * half + i, 0),
                                   pipeline_mode=pl.Buffered(1)),
                      pl.BlockSpec((1, SC_WINDOW), lambda c, i: (0, c * half + i)),
                      pl.BlockSpec((1, SC_WINDOW), lambda c, i: (0, windows + c * half + i))],
            out_specs=[],
            core_axis_name=("core", "subcore"),
            dimension_semantics=(pltpu.PARALLEL, pltpu.PARALLEL),
        )(x_hbm, ia_hbm, ib_hbm)

    return k(rows, idx2, idx2)


def _expert_kernel(te_ref, nv_ref, x_ref, wg_ref, wu_ref, wd_ref, o_ref, wgb_ref, wub_ref, wdb_ref):
    t = pl.program_id(0)
    nvalid = nv_ref[t]

    @pl.when((t == 0) | (te_ref[t] != te_ref[jnp.maximum(t - 1, 0)]))
    def _():
        wgb_ref[...] = wg_ref[0].astype(jnp.bfloat16)
        wub_ref[...] = wu_ref[0].astype(jnp.bfloat16)
        wdb_ref[...] = wd_ref[0].astype(jnp.bfloat16)

    @pl.when(nvalid > 0)
    def _():
        rows = lax.broadcasted_iota(jnp.int32, x_ref.shape, 0)
        lo, hi = _unpack_pairs(jnp.where(rows < nvalid, x_ref[...], 0))
        xb = jnp.concatenate([lo, hi], axis=1).astype(jnp.bfloat16)
        g = jnp.dot(xb, wgb_ref[...], preferred_element_type=jnp.float32)
        u = jnp.dot(xb, wub_ref[...], preferred_element_type=jnp.float32)
        a = (g * jax.nn.sigmoid(g) * u).astype(jnp.bfloat16)
        o_ref[...] = _pack_pairs(jnp.dot(a, wdb_ref[...], preferred_element_type=jnp.float32))

    @pl.when(nvalid == 0)
    def _():
        o_ref[...] = jnp.zeros_like(o_ref)


def _experts(buf, tile_expert, tile_nvalid, wg, wu, wd, *, layer):
    rows, wp = buf.shape
    _, _, d, f = wg.shape
    n_tiles = rows // EXPERT_TILE
    grid_spec = pltpu.PrefetchScalarGridSpec(
        num_scalar_prefetch=2,
        grid=(n_tiles,),
        in_specs=[pl.BlockSpec((EXPERT_TILE, wp), lambda t, te, nv: (t, 0)),
                  pl.BlockSpec((None, 1, d, f), lambda t, te, nv: (layer, te[t], 0, 0)),
                  pl.BlockSpec((None, 1, d, f), lambda t, te, nv: (layer, te[t], 0, 0)),
                  pl.BlockSpec((None, 1, f, d), lambda t, te, nv: (layer, te[t], 0, 0))],
        out_specs=pl.BlockSpec((EXPERT_TILE, wp), lambda t, te, nv: (t, 0)),
        scratch_shapes=[pltpu.VMEM((d, f), jnp.bfloat16), pltpu.VMEM((d, f), jnp.bfloat16),
                        pltpu.VMEM((f, d), jnp.bfloat16)],
    )
    return pl.pallas_call(
        _expert_kernel,
        grid_spec=grid_spec,
        out_shape=jax.ShapeDtypeStruct((rows, wp), jnp.int32),
        compiler_params=_cparams("arbitrary"),
        name="expert_ffn",
    )(tile_expert, tile_nvalid, buf, wg, wu, wd)


def _final_kernel(y0_ref, y1_ref, rw_ref, x_ref, mod_ref, fg_ref, o_ref):
    x = x_ref[...] + mod_ref[0, 5:6, :] * _moe_mix(y0_ref, y1_ref, rw_ref, slice(None))
    r = lax.rsqrt(jnp.mean(x * x, axis=-1, keepdims=True) + EPS)
    o_ref[...] = x * r * fg_ref[...]


def _final_kernel_into(y0_ref, y1_ref, rw_ref, x_ref, mod_ref, fg_ref, prev_ref, o_ref):
    del prev_ref
    _final_kernel(y0_ref, y1_ref, rw_ref, x_ref, mod_ref, fg_ref, o_ref)


def _final_combine(y, rw, xall, mods, final_g, *, tm, n_tiles, s, out_rows, out_tile0, prev_out=None):
    d = xall.shape[1]
    row = lambda w: pl.BlockSpec((tm, w), lambda i: (i, 0))
    in_specs = [row(d // 2),
                pl.BlockSpec((tm, d // 2), lambda i: (i + n_tiles, 0)),
                row(LANES), row(d),
                pl.BlockSpec((1, N_MOD, d), lambda i: ((i * tm) // s, 0, 0)),
                pl.BlockSpec(final_g.shape, lambda i: (0, 0))]
    args = [y, y, rw, xall, mods, final_g]
    aliases = {}
    body = _final_kernel
    if prev_out is not None:
        in_specs.append(pl.BlockSpec(memory_space=pl.ANY))
        args.append(prev_out)
        aliases = {len(args) - 1: 0}
        body = _final_kernel_into
    return pl.pallas_call(
        body,
        grid=(n_tiles,),
        in_specs=in_specs,
        out_specs=pl.BlockSpec((tm, d), lambda i: (out_tile0 + i, 0)),
        out_shape=jax.ShapeDtypeStruct((out_rows, d), jnp.float32),
        input_output_aliases=aliases,
        compiler_params=_cparams("arbitrary"),
        name="moe_combine_final",
    )(*args)


def _dest_kernel(ps_ref, meta_ref, o_ref):
    slot = pl.program_id(0)
    e = meta_ref[pl.ds(slot, 1), :]
    d = meta_ref[pl.ds(TOP_K + slot, 1), :]
    for k in range(N_EXPERTS):
        d = d + jnp.where(e == float(k), ps_ref[k], 0.0)
    o_ref[...] = d.astype(jnp.int32)


def _dest_rows(meta, pstarts, tcols):
    n = meta.shape[1]
    nt = n // tcols
    return pl.pallas_call(
        _dest_kernel,
        grid=(TOP_K, nt),
        in_specs=[pl.BlockSpec(memory_space=pltpu.SMEM),
                  pl.BlockSpec((META_ROWS, tcols), lambda k, i: (0, i))],
        out_specs=pl.BlockSpec((1, tcols), lambda k, i: (0, k * nt + i)),
        out_shape=jax.ShapeDtypeStruct((1, TOP_K * n), jnp.int32),
        compiler_params=_cparams("arbitrary", "arbitrary"),
        name="moe_dest",
    )(pstarts.astype(jnp.float32), meta)


def _dispatch_plan(meta, counts, n_rows_buf, tcols):
    cnt = counts[0, :N_EXPERTS].astype(jnp.int32)
    padded = (cnt + EXPERT_TILE - 1) // EXPERT_TILE * EXPERT_TILE
    pends = jnp.cumsum(padded)
    pstarts = pends - padded
    dest = _dest_rows(meta, pstarts, tcols)
    tile_start = jnp.arange(n_rows_buf // EXPERT_TILE, dtype=jnp.int32) * EXPERT_TILE
    te = jnp.sum((tile_start[:, None] >= pends[None, :]).astype(jnp.int32), axis=1)
    te = jnp.minimum(te, N_EXPERTS - 1)
    onehot = te[:, None] == jnp.arange(N_EXPERTS, dtype=jnp.int32)[None, :]
    cnt_te = jnp.sum(jnp.where(onehot, cnt[None, :], 0), axis=1)
    pstart_te = jnp.sum(jnp.where(onehot, pstarts[None, :], 0), axis=1)
    nvalid = jnp.clip(cnt_te - (tile_start - pstart_te), 0, EXPERT_TILE).astype(jnp.int32)
    return dest, te, nvalid


def _permute_heads(w, axis):
    heads = [lax.slice_in_dim(w, HEAD_DIM * h, HEAD_DIM * (h + 1), axis=axis)
             for h in HEAD_PERM[::HEAD_DIM] // HEAD_DIM]
    return jnp.concatenate(heads, axis=axis)


def _rope_tables(s, tm):
    pos = np.arange(s)
    pos_row = jnp.asarray(pos // GRID_W, jnp.float32)
    pos_col = jnp.asarray(pos % GRID_W, jnp.float32)
    n_freq = HEAD_DIM // 4
    inv = ROPE_THETA ** (-jnp.arange(n_freq, dtype=jnp.float32) / n_freq)
    ang_row = pos_row[:, None] * inv
    ang_col = pos_col[:, None] * inv
    ang = jnp.concatenate([ang_row, ang_row, ang_col, ang_col] * (LANES // HEAD_DIM), axis=-1)
    sign = np.where((np.arange(LANES) % 32) < 16, -1.0, 1.0).astype(np.float32)
    cos_t = jnp.concatenate([jnp.cos(ang), jnp.ones((tm, LANES), jnp.float32)], axis=0)
    sin_t = jnp.concatenate([jnp.sin(ang) * sign, jnp.zeros((tm, LANES), jnp.float32)], axis=0)
    return cos_t, sin_t


def kernel(x, c, ctx, c_ctx, norm1_g, norm2_g, w_mod, b_mod, w_in, q_norm_g, k_norm_g, conv_w, conv_b, conv_ln_g, conv_ln_b, sink, w_out, w_group, b_group, w_expert, b_expert, w_gate, w_up, w_down, final_g):
    nb, s, d = x.shape
    l = ctx.shape[1]
    depth = w_in.shape[0]
    assert w_in.shape[2] == D_IN and w_out.shape[1] == D_MIX
    assert s % GRID_W == 0 and s >= 3 * WINDOW and s % WINDOW == 0 and l % WINDOW == 0
    groups = BATCH_GROUPS if nb % BATCH_GROUPS == 0 else 1
    nbg = nb // groups
    n_lat, n_ctx = nbg * s, nbg * l
    tm = _pick(np.gcd(s, n_ctx), (512, 256, 128))
    tq = _pick(s, (512, 256, 128))
    assert n_lat % l == 0
    bf = jnp.bfloat16
    f32 = jnp.float32

    x2d = x.reshape(nb * s, d)
    ctx2d = ctx.reshape(nb * l, d)
    c_all = jnp.concatenate([c, c_ctx[None, :]], axis=0)
    mods_all = _modulation(c_all, w_mod, b_mod).reshape(depth, nb + 1, N_MOD, d)
    cos_t, sin_t = _rope_tables(s, tm)
    head_id = np.arange(LANES) // HEAD_DIM
    gsum = jnp.asarray((head_id[:, None] == head_id[None, :]) / HEAD_DIM, bf)
    tri = jnp.asarray(np.tril(np.ones((tm, tm), np.float32), -1), bf)
    n_lat_tiles = n_lat // tm

    xs = [(x2d, ctx2d)] * groups
    pending = [None] * groups
    for i in range(depth):
        last = i == depth - 1
        with_ctx = not last
        qg = jnp.tile(q_norm_g[i], LANES // HEAD_DIM)[None, :]
        kg = jnp.tile(k_norm_g[i], LANES // HEAD_DIM)[None, :]
        w_in_bf = jnp.concatenate(
            [_permute_heads(w_in[i][:, OFF_AQ:OFF_AK], 1), w_in[i][:, OFF_AK:OFF_CQ],
             _permute_heads(w_in[i][:, OFF_CQ:OFF_CK], 1), w_in[i][:, OFF_CK:]], axis=1).astype(bf)
        w_out_bf = jnp.concatenate(
            [_permute_heads(w_out[i][0:A_Q], 0), w_out[i][A_Q:A_Q + B_CH],
             _permute_heads(w_out[i][A_Q + B_CH:], 0)], axis=0).astype(bf)
        sink2 = sink[i] * LOG2E
        conv_args = (conv_w[i].reshape(CONV_W, B_CH), conv_b[i][None, :], conv_ln_g[i][None, :],
                     conv_ln_b[i][None, :])
        wr32 = jnp.zeros((d, LANES), f32).at[:, :N_GROUPS].set(w_group[i])
        wr32 = wr32.at[:, N_GROUPS:N_GROUPS + N_EXPERTS].set(w_expert[i])
        wr_hi = wr32.astype(bf)
        wr = jnp.concatenate([wr_hi, (wr32 - wr_hi.astype(f32)).astype(bf)], axis=1)
        br = jnp.zeros((1, LANES), f32).at[0, :N_GROUPS].set(b_group[i])
        br = br.at[0, N_GROUPS:N_GROUPS + N_EXPERTS].set(b_expert[i])
        n_tok = n_lat + n_ctx if with_ctx else n_lat
        n_tiles = n_tok // tm
        n_rows_buf = 2 * n_tok + N_EXPERTS * EXPERT_TILE
        for g in range(groups):
            b0 = g * nbg
            mods = jnp.concatenate([mods_all[i, b0:b0 + nbg], mods_all[i, nb:nb + 1]], axis=0)
            x_offsets = (b0 * s // tm, b0 * l // tm)
            x_new, qa, kta, va, hgl, qc, ktc, vc = _inproj(
                xs[g], mods, norm1_g[i][None, :], w_in_bf, gsum, qg, kg, cos_t, sin_t, n=n_lat + n_ctx,
                tm=tm, n_lat_tiles=n_lat_tiles, s=s, nb=nbg, x_offsets=x_offsets, combine=pending[g])
            if pending[g] is not None:
                xs[g] = (x_new,)
            oa, ob, oc = _latent_mixers(qa, kta, va, qc, ktc, vc, sink2, hgl, conv_args,
                                        nb=nbg, s=s, l=l, tq=tq)
            if with_ctx:
                oa, ob, oc = _ctx_mixers(qa, kta, va, qc, ktc, vc, sink2, hgl, conv_args, oa, ob, oc,
                                         nb=nbg, s=s, l=l)
            xall, hp, meta, rw, counts = _outproj(xs[g], oa, ob, oc, mods, norm2_g[i][None, :], w_out_bf,
                                                  wr, br, tri, n=n_lat + n_ctx, tm=tm, n_tiles=n_tiles,
                                                  n_lat_tiles=n_lat_tiles, s=s, nb=nbg, x_offsets=x_offsets)
            dest, te, nvalid = _dispatch_plan(meta, counts, n_rows_buf,
                                              _pick(n_tok, (8192, 4096, 2048, 1024, 512, 256, 128)))
            buf = sc_scatter_rows2(hp, dest, n_rows_buf)
            eo = _experts(buf, te, nvalid, w_gate, w_up, w_down, layer=i)
            y = sc_gather_rows(eo, dest)
            xs[g] = (xall,)
            pending[g] = (y, rw, mods)
    out = None
    for g in range(groups):
        y, rw, mods = pending[g]
        out = _final_combine(y, rw, xs[g][0], mods, final_g[None, :], tm=tm, n_tiles=n_lat_tiles, s=s,
                             out_rows=nb * s, out_tile0=g * n_lat_tiles, prev_out=out)
    return out.reshape(nb, s, d)
```

```python
import functools

import jax
import jax.numpy as jnp
import numpy as np
from jax import lax
from jax.experimental import pallas as pl
from jax.experimental.pallas import tpu as pltpu
from jax.experimental.pallas import tpu_sc as plsc

HEAD_DIM = 64
GRID_W = 64
ROPE_THETA = 10000.0
A_HEADS, A_KV_HEADS = 6, 2
C_HEADS, C_KV_HEADS = 6, 2
B_CH = 256
CONV_W = 31
WINDOW = 128
N_GROUPS = 4
EXPERTS_PER_GROUP = 8
N_EXPERTS = N_GROUPS * EXPERTS_PER_GROUP
TOP_K = 2
N_MOD = 6
EPS = 1e-6
ATTN_SCALE = HEAD_DIM ** -0.5
LOG2E = 1.4426950408889634
Q_SCALE = ATTN_SCALE * LOG2E

A_Q = A_HEADS * HEAD_DIM
A_KV = A_KV_HEADS * HEAD_DIM
C_Q = C_HEADS * HEAD_DIM
C_KV = C_KV_HEADS * HEAD_DIM
D_MIX = A_Q + B_CH + C_Q
OFF_AQ = 0
OFF_AK = OFF_AQ + A_Q
OFF_AV = OFF_AK + A_KV
OFF_BU = OFF_AV + A_KV
OFF_CQ = OFF_BU + 2 * B_CH
OFF_CK = OFF_CQ + C_Q
OFF_CV = OFF_CK + C_KV
D_IN = OFF_CV + C_KV

LANES = 128
SUBLANES = 8
KT_ROWS = LANES
V_W = 2 * LANES
HEAD_PERM = np.concatenate([np.arange(HEAD_DIM) + HEAD_DIM * h for b in range(3) for h in (b, b + 3)])
EXPERT_TILE = 1024
SC_WINDOW = 128
VMEM_LIMIT = 56 * 1024 * 1024
HI_MASK = -65536
BATCH_GROUPS = 2
ROW_CHAIN = 256


def _cparams(*sem):
    return pltpu.CompilerParams(dimension_semantics=sem, vmem_limit_bytes=VMEM_LIMIT)


def _pick(n, cands):
    for c in cands:
        if n % c == 0:
            return c
    raise ValueError(f"no tile in {cands} divides {n}")


def _pack_pairs(x):
    w = x.shape[1] // 2
    lo = lax.bitcast_convert_type(x[:, :w].astype(jnp.bfloat16).astype(jnp.float32), jnp.int32)
    hi = lax.bitcast_convert_type(x[:, w:].astype(jnp.bfloat16).astype(jnp.float32), jnp.int32)
    return (hi & HI_MASK) | lax.shift_right_logical(lo, 16)


def _unpack_pairs(p):
    lo = lax.bitcast_convert_type(lax.shift_left(p, 16), jnp.float32)
    hi = lax.bitcast_convert_type(p & HI_MASK, jnp.float32)
    return lo, hi


def _mod_kernel(c_ref, w_ref, b_ref, o_ref):
    c = c_ref[...]
    a = c * jax.nn.sigmoid(c)
    o_ref[0] = jnp.dot(a, w_ref[0], preferred_element_type=jnp.float32,
                       precision=lax.Precision.HIGHEST) + b_ref[0]


def _modulation(c_all, w_mod, b_mod):
    depth, d, n = w_mod.shape
    r = c_all.shape[0]
    tn = _pick(n, (1024, 512, 256, 128))
    return pl.pallas_call(
        _mod_kernel,
        grid=(depth, n // tn),
        in_specs=[pl.BlockSpec((r, d), lambda l, j: (0, 0)),
                  pl.BlockSpec((1, d, tn), lambda l, j: (l, 0, j)),
                  pl.BlockSpec((1, 1, tn), lambda l, j: (l, 0, j))],
        out_specs=pl.BlockSpec((1, r, tn), lambda l, j: (l, 0, j)),
        out_shape=jax.ShapeDtypeStruct((depth, r, n), jnp.float32),
        compiler_params=_cparams("arbitrary", "arbitrary"),
        name="modulation",
    )(c_all, w_mod, b_mod.reshape(depth, 1, n))


def _head_mean_sq(blk, gsum):
    sq = blk * blk
    hi = sq.astype(jnp.bfloat16)
    lo = (sq - hi.astype(jnp.float32)).astype(jnp.bfloat16)
    return (jnp.dot(hi, gsum, preferred_element_type=jnp.float32)
            + jnp.dot(lo, gsum, preferred_element_type=jnp.float32))


def _x_specs(xs, tm, n_lat_tiles, offsets=(0, 0), tile_of=lambda i: i):
    d = xs[0].shape[1]
    if len(xs) == 1:
        return [pl.BlockSpec((tm, d), lambda i: (tile_of(i), 0))]
    lat0, ctx0 = offsets
    return [pl.BlockSpec((tm, d), lambda i: (lat0 + jnp.minimum(tile_of(i), n_lat_tiles - 1), 0)),
            pl.BlockSpec((tm, d), lambda i: (ctx0 + jnp.maximum(tile_of(i) - n_lat_tiles, 0), 0))]


def _load_x(x_refs, n_lat_tiles, rows=slice(None), tile=None):
    if len(x_refs) == 1:
        return x_refs[0][rows, :]
    tile = pl.program_id(0) if tile is None else tile
    return jnp.where(tile < n_lat_tiles, x_refs[0][rows, :], x_refs[1][rows, :])


def _moe_mix(y0_ref, y1_ref, rw_ref, rows):
    rw = rw_ref[rows, :]
    w0 = rw[:, 0:1]
    w1 = rw[:, 1:2]
    a_lo, a_hi = _unpack_pairs(y0_ref[rows, :])
    b_lo, b_hi = _unpack_pairs(y1_ref[rows, :])
    return jnp.concatenate([a_lo * w0 + b_lo * w1, a_hi * w0 + b_hi * w1], axis=1)


def _inproj_kernel(*refs, n_x, n_lat_tiles, fused_combine):
    x_refs = refs[:n_x]
    refs = refs[n_x:]
    if fused_combine:
        y0_ref, y1_ref, rw_ref, modp_ref = refs[:4]
        refs = refs[4:]
        xo_ref = refs[-1]
        refs = refs[:-1]
    (mod_ref, g_ref, w_ref, gsum_ref, qg_ref, kg_ref, cos_ref, sin_ref,
     qa_ref, kta_ref, va_ref, hgl_ref, qc_ref, ktc_ref, vc_ref) = refs
    tm = qa_ref.shape[0]
    tr = min(tm, ROW_CHAIN)
    bf = jnp.bfloat16
    lane = lax.broadcasted_iota(jnp.int32, (tr, LANES), 1)
    first16 = (lane % 32) < 16
    gsum = gsum_ref[...]
    qg = qg_ref[...]
    kg = kg_ref[...]
    scale = g_ref[...] * (1.0 + mod_ref[0, 1:2, :])
    shift = mod_ref[0, 0:1, :]
    ones = jnp.ones((tr, LANES), bf)

    for r0 in range(0, tm, tr):
        rows = slice(r0, r0 + tr)
        x = _load_x(x_refs, n_lat_tiles, rows)
        if fused_combine:
            x = x + modp_ref[0, 5:6, :] * _moe_mix(y0_ref, y1_ref, rw_ref, rows)
            xo_ref[rows, :] = x
        r = lax.rsqrt(jnp.mean(x * x, axis=-1, keepdims=True) + EPS)
        h = (x * r) * scale + shift
        p = jnp.dot(h.astype(bf), w_ref[...], preferred_element_type=jnp.float32)
        cos = cos_ref[rows, :]
        sin = sin_ref[rows, :]

        def blk(off):
            return p[:, off:off + LANES]

        def rope(t):
            sw = jnp.where(first16, pltpu.roll(t, LANES - 16, axis=1), pltpu.roll(t, 16, axis=1))
            return t * cos + sw * sin

        def norm(t, g):
            return t * lax.rsqrt(_head_mean_sq(t, gsum) + EPS) * g

        def store_kv(kt_ref, v_ref, k_blk, v_blk):
            kt_ref[:, rows] = k_blk.T.astype(bf)
            v_ref[rows, 0:LANES] = v_blk.astype(bf)
            v_ref[rows, LANES:2 * LANES] = ones

        for i in range(A_Q // LANES):
            t = rope(norm(blk(OFF_AQ + i * LANES), qg)) * Q_SCALE
            qa_ref[rows, i * LANES:(i + 1) * LANES] = t.astype(bf)
        store_kv(kta_ref, va_ref, rope(norm(blk(OFF_AK), kg)), blk(OFF_AV))
        for i in range(B_CH // LANES):
            a = blk(OFF_BU + i * LANES)
            gt = blk(OFF_BU + B_CH + i * LANES)
            hgl_ref[rows, i * LANES:(i + 1) * LANES] = a * jax.nn.sigmoid(gt)
        for i in range(C_Q // LANES):
            t = rope(blk(OFF_CQ + i * LANES)) * Q_SCALE
            qc_ref[rows, i * LANES:(i + 1) * LANES] = t.astype(bf)
        store_kv(ktc_ref, vc_ref, rope(blk(OFF_CK)), blk(OFF_CV))


def _inproj(xs, mods, g1, w_in_bf, gsum, qg, kg, cos_t, sin_t, *, n, tm, n_lat_tiles, s, nb, x_offsets=(0, 0),
            combine=None):
    d = xs[0].shape[1]
    s_tiles = s // tm
    n_tiles = n // tm

    def bidx(i):
        return jnp.where(i < n_lat_tiles, (i * tm) // s, nb)

    def ridx(i):
        return jnp.where(i < n_lat_tiles, i % s_tiles, s_tiles)

    row = lambda w: pl.BlockSpec((tm, w), lambda i: (i, 0))
    ktspec = pl.BlockSpec((KT_ROWS, tm), lambda i: (0, i))
    const = lambda a: pl.BlockSpec(a.shape, lambda i: (0,) * a.ndim)
    bf = jnp.bfloat16
    modspec = pl.BlockSpec((1, N_MOD, d), lambda i: (bidx(i), 0, 0))
    in_specs = _x_specs(xs, tm, n_lat_tiles, x_offsets)
    args = list(xs)
    out_specs = [row(A_Q), ktspec, row(V_W), row(B_CH), row(C_Q), ktspec, row(V_W)]
    out_shape = [jax.ShapeDtypeStruct((n, A_Q), bf), jax.ShapeDtypeStruct((KT_ROWS, n), bf),
                 jax.ShapeDtypeStruct((n, V_W), bf), jax.ShapeDtypeStruct((n, B_CH), jnp.float32),
                 jax.ShapeDtypeStruct((n, C_Q), bf), jax.ShapeDtypeStruct((KT_ROWS, n), bf),
                 jax.ShapeDtypeStruct((n, V_W), bf)]
    aliases = {}
    if combine is not None:
        y, rw, mods_prev = combine
        assert len(xs) == 1 and y.shape[0] == 2 * n
        in_specs += [row(d // 2), pl.BlockSpec((tm, d // 2), lambda i: (i + n_tiles, 0)), row(LANES), modspec]
        args += [y, y, rw, mods_prev]
        out_specs.append(row(d))
        out_shape.append(jax.ShapeDtypeStruct((n, d), jnp.float32))
        aliases = {0: len(out_shape) - 1}
    in_specs += [modspec, const(g1), const(w_in_bf), const(gsum), const(qg), const(kg),
                 pl.BlockSpec((tm, LANES), lambda i: (ridx(i), 0)),
                 pl.BlockSpec((tm, LANES), lambda i: (ridx(i), 0))]
    args += [mods, g1, w_in_bf, gsum, qg, kg, cos_t, sin_t]
    outs = pl.pallas_call(
        functools.partial(_inproj_kernel, n_x=len(xs), n_lat_tiles=n_lat_tiles,
                          fused_combine=combine is not None),
        grid=(n_tiles,),
        in_specs=in_specs,
        out_specs=out_specs,
        out_shape=out_shape,
        input_output_aliases=aliases,
        compiler_params=_cparams("arbitrary"),
        name="inproj",
    )(*args)
    if combine is not None:
        return (outs[-1],) + tuple(outs[:-1])
    return (None,) + tuple(outs)


def _stack_heads(q_ref, tq):
    lane = lax.broadcasted_iota(jnp.int32, (tq, LANES), 1)
    lo = lane < HEAD_DIM
    qb = [q_ref[:, i * LANES:(i + 1) * LANES] for i in range(3)]
    zero = jnp.zeros_like(qb[0])
    s0 = jnp.concatenate([jnp.where(lo, t, zero) for t in qb], axis=0)
    s1 = jnp.concatenate([jnp.where(lo, zero, t) for t in qb], axis=0)
    return s0, s1, lo


def _unstack_store(o_ref, o0, o1, lo, tq):
    for i in range(3):
        rows = slice(i * tq, (i + 1) * tq)
        o_ref[:, i * LANES:(i + 1) * LANES] = jnp.where(lo, o0[rows], o1[rows]).astype(jnp.bfloat16)


def _krow(kv):
    del kv
    return slice(0, KT_ROWS)


def _vcol(kv):
    del kv
    return slice(0, V_W)


def _row_max(scores):
    m = None
    for t in scores:
        for c in range(0, t.shape[1], LANES):
            blk = t[:, c:c + LANES]
            m = blk if m is None else jnp.maximum(m, blk)
    return m.max(axis=-1, keepdims=True)


def _softmax_pv(scores, values, extra=None):
    m = _row_max(scores)
    if extra is not None:
        m = jnp.maximum(m, extra)
    acc = None
    for t, v in zip(scores, values):
        c = jnp.dot(jnp.exp2(t - m).astype(jnp.bfloat16), v, preferred_element_type=jnp.float32)
        acc = c if acc is None else acc + c
    den = acc[:, LANES:2 * LANES]
    if extra is not None:
        den = den + jnp.exp2(extra - m)
    return acc[:, 0:LANES] / den


ATTN_CHAIN_ROWS = 128


def _qk(q, kt):
    return jnp.dot(q, kt, preferred_element_type=jnp.float32)


def _attn_a_kernel(q_ref, ktl_ref, ktc_ref, vl_ref, vc_ref, *rest, tq, n_q):
    conv_refs = rest[:7]
    o_ref, ob_ref, sh_ref = rest[7:]
    j = pl.program_id(1)
    conv_steps = _conv_steps(conv_refs, ob_ref, sh_ref, j > 0, j < n_q - 1, tq)
    s0, s1, lo = _stack_heads(q_ref, tq)
    n_chains = 2 * (3 * tq // ATTN_CHAIN_ROWS)
    chain = 0
    conv_done = 0
    outs = []
    for kv, qs in enumerate((s0, s1)):
        krow = _krow(kv)
        vcol = _vcol(kv)
        parts = []
        for r0 in range(0, 3 * tq, ATTN_CHAIN_ROWS):
            qr = qs[r0:r0 + ATTN_CHAIN_ROWS]
            ss = [_qk(qr, ktl_ref[krow, :]), _qk(qr, ktc_ref[krow, :])]
            parts.append(_softmax_pv(ss, [vl_ref[:, vcol], vc_ref[:, vcol]]))
            chain += 1
            while conv_done < chain * len(conv_steps) // n_chains:
                conv_steps[conv_done]()
                conv_done += 1
        outs.append(jnp.concatenate(parts, axis=0))
    _unstack_store(o_ref, outs[0], outs[1], lo, tq)


def _kv_specs(nb, s, l):
    ctx0 = nb * s // l
    return [pl.BlockSpec((KT_ROWS, s), lambda b, j: (0, b)),
            pl.BlockSpec((KT_ROWS, l), lambda b, j: (0, ctx0 + b)),
            pl.BlockSpec((s, V_W), lambda b, j: (b, 0)),
            pl.BlockSpec((l, V_W), lambda b, j: (ctx0 + b, 0))]


def _sink_column(sink_ref, kv, rows):
    return jnp.concatenate([jnp.full((rows, 1), sink_ref[3 * kv + g], jnp.float32) for g in range(3)], axis=0)


def _attn_c_kernel(sink_ref, bias_ref, q_ref, ktl_ref, ktc_ref, vl_ref, vc_ref, o_ref, *, s, blocks):
    j = pl.program_id(1)
    tq = WINDOW
    band = 3 * WINDOW
    bf = jnp.bfloat16
    f32 = jnp.float32
    m3 = 3 * tq
    stacks = [_stack_heads(q_ref.at[blk * tq:(blk + 1) * tq, :], tq) for blk in range(blocks)]
    lo = stacks[0][2]
    starts = [pl.multiple_of(jnp.clip((j * blocks + blk - 1) * WINDOW, 0, s - band), WINDOW)
              for blk in range(blocks)]
    outs = [[None, None] for _ in range(blocks)]
    for kv in range(C_KV_HEADS):
        krow = _krow(kv)
        vcol = _vcol(kv)
        sk = _sink_column(sink_ref, kv, tq)
        sc_all = _qk(jnp.concatenate([st[kv] for st in stacks], axis=0), ktc_ref[krow, :])
        accs, pcs, ms = [], [], []
        for blk in range(blocks):
            start = starts[blk]
            sl = _qk(stacks[blk][kv], ktl_ref[krow, pl.ds(start, band)]) + bias_ref[j * blocks + blk - start // WINDOW]
            sc = sc_all[blk * m3:(blk + 1) * m3]
            m = jnp.maximum(_row_max([sl, sc]), sk)
            accs.append(jnp.dot(jnp.exp2(sl - m).astype(bf), vl_ref[pl.ds(start, band), vcol],
                                preferred_element_type=f32))
            pcs.append(jnp.exp2(sc - m).astype(bf))
            ms.append(m)
        acc_ctx = jnp.dot(jnp.concatenate(pcs, axis=0), vc_ref[:, vcol], preferred_element_type=f32)
        for blk in range(blocks):
            acc = accs[blk] + acc_ctx[blk * m3:(blk + 1) * m3]
            den = acc[:, LANES:2 * LANES] + jnp.exp2(sk - ms[blk])
            outs[blk][kv] = acc[:, 0:LANES] / den
    for blk in range(blocks):
        _unstack_store(o_ref.at[blk * tq:(blk + 1) * tq, :], outs[blk][0], outs[blk][1], lo, tq)


def _window_bias():
    r = np.arange(3 * WINDOW)[:, None] % WINDOW
    col = np.arange(3 * WINDOW)[None, :]
    tabs = [np.where(np.abs(col - r - WINDOW * off) <= WINDOW, 0.0, -np.inf) for off in range(3)]
    return jnp.asarray(np.stack(tabs), jnp.float32)


def _mixers_kernel(*refs, tq, n_q, s):
    sink_ref, bias_ref, qc_ref = refs[0:3]
    kvc_refs = refs[3:7]
    qa_ref = refs[7]
    kva_refs = refs[8:12]
    conv_refs = refs[12:19]
    oa_ref, ob_ref, oc_ref, sh_ref = refs[19:]
    _attn_c_kernel(sink_ref, bias_ref, qc_ref, *kvc_refs, oc_ref, s=s, blocks=tq // WINDOW)
    _attn_a_kernel(qa_ref, *kva_refs, *conv_refs, oa_ref, ob_ref, sh_ref, tq=tq, n_q=n_q)


def _latent_mixers(qa, kta, va, qc, ktc, vc, sink2, hgl, conv_params, *, nb, s, l, tq):
    n = qa.shape[0]
    n_q = s // tq
    bias = _window_bias()
    conv_specs, conv_blk = _conv_specs(n, conv_params, chunks=n_q, base=0, tc=tq)
    qspec = lambda w: pl.BlockSpec((tq, w), lambda b, j: (b * n_q + j, 0))
    bf = jnp.bfloat16
    return pl.pallas_call(
        functools.partial(_mixers_kernel, tq=tq, n_q=n_q, s=s),
        grid=(nb, n_q),
        in_specs=[pl.BlockSpec(memory_space=pltpu.SMEM), pl.BlockSpec(bias.shape, lambda b, j: (0, 0, 0)),
                  qspec(C_Q)] + _kv_specs(nb, s, l) + [qspec(A_Q)] + _kv_specs(nb, s, l) + conv_specs,
        out_specs=[qspec(A_Q), conv_blk, qspec(C_Q)],
        out_shape=[jax.ShapeDtypeStruct((n, A_Q), bf), jax.ShapeDtypeStruct((n, B_CH), bf),
                   jax.ShapeDtypeStruct((n, C_Q), bf)],
        scratch_shapes=[pltpu.VMEM((SUBLANES, tq + 2 * CONV_HALO, B_CH), jnp.float32)],
        compiler_params=_cparams("arbitrary", "arbitrary"),
        name="latent_mixers",
    )(sink2, bias, qc, ktc, ktc, vc, vc, qa, kta, kta, va, va, hgl, hgl, hgl, *conv_params)


def _ctx_mixers_kernel(sink_ref, qa_ref, kta_ref, va_ref, qc_ref, ktc_ref, vc_ref, *rest, l):
    conv_refs = rest[:7]
    oa_ref, ob_ref, oc_ref, sh_ref = rest[-4:]
    conv_steps = _conv_steps(conv_refs, ob_ref, sh_ref, False, False, l)
    for q_ref, kt_ref, v_ref, o_ref, with_sink in ((qa_ref, kta_ref, va_ref, oa_ref, False),
                                                   (qc_ref, ktc_ref, vc_ref, oc_ref, True)):
        s0, s1, lo = _stack_heads(q_ref, l)
        outs = []
        for kv, qs in enumerate((s0, s1)):
            sk = _sink_column(sink_ref, kv, l) if with_sink else None
            outs.append(_softmax_pv([_qk(qs, kt_ref[_krow(kv), :])], [v_ref[:, _vcol(kv)]], sk))
            for step in conv_steps[(2 * with_sink + kv) * len(conv_steps) // 4:
                                   (2 * with_sink + kv + 1) * len(conv_steps) // 4]:
                step()
        _unstack_store(o_ref, outs[0], outs[1], lo, l)


def _ctx_mixers(qa, kta, va, qc, ktc, vc, sink2, hgl, conv_params, oa, ob, oc, *, nb, s, l):
    n = qa.shape[0]
    base = nb * s // l
    row = lambda w: pl.BlockSpec((l, w), lambda b, j: (base + b, 0))
    ktspec = pl.BlockSpec((KT_ROWS, l), lambda b, j: (0, base + b))
    conv_specs, conv_blk = _conv_specs(n, conv_params, chunks=1, base=base, tc=l)
    anyspec = pl.BlockSpec(memory_space=pl.ANY)
    n_in = 7 + len(conv_specs)
    return pl.pallas_call(
        functools.partial(_ctx_mixers_kernel, l=l),
        grid=(nb, 1),
        in_specs=[pl.BlockSpec(memory_space=pltpu.SMEM), row(A_Q), ktspec, row(V_W), row(C_Q), ktspec, row(V_W)]
        + conv_specs + [anyspec, anyspec, anyspec],
        out_specs=[row(A_Q), conv_blk, row(C_Q)],
        out_shape=[jax.ShapeDtypeStruct(a.shape, a.dtype) for a in (oa, ob, oc)],
        scratch_shapes=[pltpu.VMEM((SUBLANES, l + 2 * CONV_HALO, B_CH), jnp.float32)],
        input_output_aliases={n_in: 0, n_in + 1: 1, n_in + 2: 2},
        compiler_params=_cparams("arbitrary", "arbitrary"),
        name="context_mixers",
    )(sink2, qa, kta, va, qc, ktc, vc, hgl, hgl, hgl, *conv_params, oa, ob, oc)


CONV_HALO = 16
CONV_ROWS = 64


def _conv_steps(conv_refs, o_ref, sh_ref, has_prev, has_next, tc):
    prev_ref, cur_ref, next_ref, w_ref, b_ref, g_ref, beta_ref = conv_refs
    rows = tc + 2 * CONV_HALO
    sh_ref[0, 0:CONV_HALO, :] = jnp.where(has_prev, prev_ref[...], 0.0)
    sh_ref[0, CONV_HALO:CONV_HALO + tc, :] = cur_ref[...]
    sh_ref[0, CONV_HALO + tc:rows, :] = jnp.where(has_next, next_ref[...], 0.0)
    for b in range(1, SUBLANES):
        sh_ref[b, 0:rows - SUBLANES, :] = sh_ref[0, b:b + rows - SUBLANES, :]
    base = CONV_HALO - CONV_W // 2

    def step(r0):
        acc = None
        for k in range(CONV_W):
            a, b = divmod(base + k, SUBLANES)
            term = sh_ref[b, SUBLANES * a + r0:SUBLANES * a + r0 + CONV_ROWS, :] * w_ref[k:k + 1, :]
            acc = term if acc is None else acc + term
        hc = acc + b_ref[...]
        mu = jnp.mean(hc, axis=-1, keepdims=True)
        xc = hc - mu
        var = jnp.mean(xc * xc, axis=-1, keepdims=True)
        y = xc * lax.rsqrt(var + EPS) * g_ref[...] + beta_ref[...]
        o_ref[r0:r0 + CONV_ROWS, :] = (y * jax.nn.sigmoid(y)).astype(o_ref.dtype)

    return [functools.partial(step, r0) for r0 in range(0, tc, CONV_ROWS)]


def _conv_specs(n, params, *, chunks, base, tc):
    hb = tc // CONV_HALO
    last_halo = n // CONV_HALO - 1
    idx = lambda b_, j: base + b_ * chunks + j
    blk = pl.BlockSpec((tc, B_CH), lambda b_, j: (idx(b_, j), 0))
    prev_halo = pl.BlockSpec((CONV_HALO, B_CH), lambda b_, j: (jnp.maximum(idx(b_, j) * hb - 1, 0), 0))
    next_halo = pl.BlockSpec((CONV_HALO, B_CH), lambda b_, j: (jnp.minimum((idx(b_, j) + 1) * hb, last_halo), 0))
    const = lambda a: pl.BlockSpec(a.shape, lambda b_, j: (0,) * a.ndim)
    return [prev_halo, blk, next_halo] + [const(a) for a in params], blk


META_ROWS = 8


def _outproj_kernel(*refs, n_x, n_lat_tiles, n_tiles):
    x_refs = refs[:n_x]
    (oa_ref, ob_ref, oc_ref, mod_ref, g_ref, w_ref, wr_ref, br_ref, tri_ref,
     xo_ref, hp_ref, meta_ref, rw_ref, cnt_ref, lg_ref) = refs[n_x:]
    i = pl.program_id(0)
    tm = xo_ref.shape[0]
    f32 = jnp.float32

    @pl.when(i == 0)
    def _():
        cnt_ref[...] = jnp.zeros_like(cnt_ref)
        lg_ref[...] = jnp.zeros_like(lg_ref)

    lat = jnp.concatenate([oa_ref[...], ob_ref[...], oc_ref[...]], axis=1)
    mix = jnp.dot(lat, w_ref[...], preferred_element_type=f32)
    x = _load_x(x_refs, n_lat_tiles, tile=jnp.minimum(i, n_tiles - 1)) + mod_ref[0, 2:3, :] * mix
    xo_ref[...] = x
    r = lax.rsqrt(jnp.mean(x * x, axis=-1, keepdims=True) + EPS)
    h = (x * r) * (g_ref[...] * (1.0 + mod_ref[0, 4:5, :])) + mod_ref[0, 3:4, :]
    h_hi = h.astype(jnp.bfloat16)
    hp_ref[...] = _pack_pairs(h_hi)
    h_lo = (h - h_hi.astype(f32)).astype(jnp.bfloat16)
    r_hi = jnp.dot(h_hi, wr_ref[...], preferred_element_type=f32)
    r_lo = jnp.dot(h_lo, wr_ref[:, 0:LANES], preferred_element_type=f32)
    lg = lg_ref[(i + 1) % 2]
    lg_ref[i % 2] = r_hi[:, 0:LANES] + r_hi[:, LANES:2 * LANES] + r_lo + br_ref[...]

    lane = lax.broadcasted_iota(jnp.int32, (tm, LANES), 1).astype(f32)
    big = float(LANES)
    ninf = -jnp.inf
    glog = jnp.where(lane < N_GROUPS, lg, ninf)
    gmax = glog.max(axis=-1, keepdims=True)
    g_val = 1.0 / jnp.exp(glog - gmax).sum(axis=-1, keepdims=True)
    g_idx = jnp.where(glog == gmax, lane, big).min(axis=-1, keepdims=True)
    e_lo = N_GROUPS + EXPERTS_PER_GROUP * g_idx
    el = jnp.where((lane >= e_lo) & (lane < e_lo + EXPERTS_PER_GROUP), lg, ninf)
    v0 = el.max(axis=-1, keepdims=True)
    i0 = jnp.where(el == v0, lane, big).min(axis=-1, keepdims=True)
    el1 = jnp.where(lane == i0, ninf, el)
    v1 = el1.max(axis=-1, keepdims=True)
    i1 = jnp.where(el1 == v1, lane, big).min(axis=-1, keepdims=True)
    t = jnp.exp(v1 - v0)
    w0 = g_val / (1.0 + t)
    w1 = g_val * t / (1.0 + t)
    e0 = i0 - N_GROUPS
    e1 = i1 - N_GROUPS

    cnt = cnt_ref[0:1, :]
    tri = tri_ref[...]
    ranks = []
    for e in (e0, e1):
        oh = lane == e
        ohf = oh.astype(f32)
        pre = jnp.dot(tri, ohf.astype(jnp.bfloat16), preferred_element_type=f32) + cnt
        ranks.append(jnp.where(oh, pre, 0.0).sum(axis=-1, keepdims=True))
        cnt = cnt + ohf.sum(axis=0, keepdims=True)
    cnt_ref[0:1, :] = jnp.where(i > 0, cnt, cnt_ref[0:1, :])
    rw_ref[...] = jnp.where(lane == 0, w0, jnp.where(lane == 1, w1, 0.0))
    rec = jnp.where(lane == 0, e0, jnp.where(lane == 1, e1, jnp.where(lane == 2, ranks[0],
                    jnp.where(lane == 3, ranks[1], jnp.where(lane == 4, w0, jnp.where(lane == 5, w1, 0.0))))))
    meta_ref[...] = rec.T[0:META_ROWS, :]


def _outproj(xs, oa, ob, oc, mods, g2, w_out_bf, wr, br, tri, *, n, tm, n_tiles, n_lat_tiles, s, nb,
             x_offsets=(0, 0)):
    d = xs[0].shape[1]
    rows = n_tiles * tm
    cur = lambda i: jnp.minimum(i, n_tiles - 1)
    prev = lambda i: jnp.maximum(i - 1, 0)

    def bidx(i):
        return jnp.where(cur(i) < n_lat_tiles, (cur(i) * tm) // s, nb)

    row = lambda w: pl.BlockSpec((tm, w), lambda i: (cur(i), 0))
    const = lambda a: pl.BlockSpec(a.shape, lambda i: (0,) * a.ndim)
    return pl.pallas_call(
        functools.partial(_outproj_kernel, n_x=len(xs), n_lat_tiles=n_lat_tiles, n_tiles=n_tiles),
        grid=(n_tiles + 1,),
        in_specs=_x_specs(xs, tm, n_lat_tiles, x_offsets, cur) + [
                  row(A_Q), row(B_CH), row(C_Q),
                  pl.BlockSpec((1, N_MOD, d), lambda i: (bidx(i), 0, 0)),
                  const(g2), const(w_out_bf), const(wr), const(br), const(tri)],
        out_specs=[row(d), row(d // 2),
                   pl.BlockSpec((META_ROWS, tm), lambda i: (0, prev(i))),
                   pl.BlockSpec((tm, LANES), lambda i: (prev(i), 0)),
                   pl.BlockSpec((8, LANES), lambda i: (0, 0))],
        out_shape=[jax.ShapeDtypeStruct((n, d), jnp.float32),
                   jax.ShapeDtypeStruct((rows, d // 2), jnp.int32),
                   jax.ShapeDtypeStruct((META_ROWS, rows), jnp.float32),
                   jax.ShapeDtypeStruct((rows, LANES), jnp.float32),
                   jax.ShapeDtypeStruct((8, LANES), jnp.float32)],
        scratch_shapes=[pltpu.VMEM((2, tm, LANES), jnp.float32)],
        input_output_aliases={0: 0} if len(xs) == 1 else {},
        compiler_params=_cparams("arbitrary"),
        name="outproj_router",
    )(*xs, oa, ob, oc, mods, g2, w_out_bf, wr, br, tri)


def _sc_mesh():
    return plsc.VectorSubcoreMesh(core_axis_name="core", subcore_axis_name="subcore")


def sc_gather_rows(table, idx2):
    r = idx2.shape[1]
    w = table.shape[1]
    assert r % (2 * SC_WINDOW) == 0
    half = r // SC_WINDOW // 2

    @functools.partial(pl.kernel, out_type=jax.ShapeDtypeStruct((r, w), table.dtype), mesh=_sc_mesh())
    def k(x_hbm, i_hbm, o_hbm):
        def body(i_vmem, o_vmem):
            pltpu.sync_copy(x_hbm.at[i_vmem.at[0]], o_vmem)

        pltpu.emit_pipeline(
            body,
            grid=(2, half),
            in_specs=[pl.BlockSpec((1, SC_WINDOW), lambda c, i: (0, c * half + i))],
            out_specs=[pl.BlockSpec((SC_WINDOW, w), lambda c, i: (c * half + i, 0),
                                    pipeline_mode=pl.Buffered(1))],
            core_axis_name=("core", "subcore"),
            dimension_semantics=(pltpu.PARALLEL, pltpu.PARALLEL),
        )(i_hbm, o_hbm)

    return k(table, idx2)


def sc_scatter_rows2(rows, idx2, n_out):
    r, w = rows.shape
    assert idx2.shape == (1, 2 * r) and r % (2 * SC_WINDOW) == 0
    windows = r // SC_WINDOW
    half = windows // 2

    @functools.partial(pl.kernel, out_type=jax.ShapeDtypeStruct((n_out, w), rows.dtype), mesh=_sc_mesh(),
                       scratch_types=[])
    def k(x_hbm, ia_hbm, ib_hbm, o_hbm):
        def body(x_vmem, ia_vmem, ib_vmem):
            pltpu.sync_copy(x_vmem, o_hbm.at[ia_vmem.at[0]])
            pltpu.sync_copy(x_vmem, o_hbm.at[ib_vmem.at[0]])

        pltpu.emit_pipeline(
            body,
            grid=(2, half),
            in_specs=[pl.BlockSpec((SC_WINDOW, w), lambda c, i: (c * half + i, 0),
                                   pipeline_mode=pl.Buffered(1)),
                      pl.BlockSpec((1, SC_WINDOW), lambda c, i: (0, c * half + i)),
                      pl.BlockSpec((1, SC_WINDOW), lambda c, i: (0, windows + c * half + i))],
            out_specs=[],
            core_axis_name=("core", "subcore"),
            dimension_semantics=(pltpu.PARALLEL, pltpu.PARALLEL),
        )(x_hbm, ia_hbm, ib_hbm)

    return k(rows, idx2, idx2)


def _expert_kernel(te_ref, nv_ref, x_ref, wg_ref, wu_ref, wd_ref, o_ref, wgb_ref, wub_ref, wdb_ref):
    t = pl.program_id(0)
    nvalid = nv_ref[t]

    @pl.when((t == 0) | (te_ref[t] != te_ref[jnp.maximum(t - 1, 0)]))
    def _():
        wgb_ref[...] = wg_ref[0].astype(jnp.bfloat16)
        wub_ref[...] = wu_ref[0].astype(jnp.bfloat16)
        wdb_ref[...] = wd_ref[0].astype(jnp.bfloat16)

    @pl.when(nvalid > 0)
    def _():
        rows = lax.broadcasted_iota(jnp.int32, x_ref.shape, 0)
        lo, hi = _unpack_pairs(jnp.where(rows < nvalid, x_ref[...], 0))
        xb = jnp.concatenate([lo, hi], axis=1).astype(jnp.bfloat16)
        g = jnp.dot(xb, wgb_ref[...], preferred_element_type=jnp.float32)
        u = jnp.dot(xb, wub_ref[...], preferred_element_type=jnp.float32)
        a = (g * jax.nn.sigmoid(g) * u).astype(jnp.bfloat16)
        o_ref[...] = _pack_pairs(jnp.dot(a, wdb_ref[...], preferred_element_type=jnp.float32))

    @pl.when(nvalid == 0)
    def _():
        o_ref[...] = jnp.zeros_like(o_ref)


def _experts(buf, tile_expert, tile_nvalid, wg, wu, wd, *, layer):
    rows, wp = buf.shape
    _, _, d, f = wg.shape
    n_tiles = rows // EXPERT_TILE
    grid_spec = pltpu.PrefetchScalarGridSpec(
        num_scalar_prefetch=2,
        grid=(n_tiles,),
        in_specs=[pl.BlockSpec((EXPERT_TILE, wp), lambda t, te, nv: (t, 0)),
                  pl.BlockSpec((None, 1, d, f), lambda t, te, nv: (layer, te[t], 0, 0)),
                  pl.BlockSpec((None, 1, d, f), lambda t, te, nv: (layer, te[t], 0, 0)),
                  pl.BlockSpec((None, 1, f, d), lambda t, te, nv: (layer, te[t], 0, 0))],
        out_specs=pl.BlockSpec((EXPERT_TILE, wp), lambda t, te, nv: (t, 0)),
        scratch_shapes=[pltpu.VMEM((d, f), jnp.bfloat16), pltpu.VMEM((d, f), jnp.bfloat16),
                        pltpu.VMEM((f, d), jnp.bfloat16)],
    )
    return pl.pallas_call(
        _expert_kernel,
        grid_spec=grid_spec,
        out_shape=jax.ShapeDtypeStruct((rows, wp), jnp.int32),
        compiler_params=_cparams("arbitrary"),
        name="expert_ffn",
    )(tile_expert, tile_nvalid, buf, wg, wu, wd)


def _final_kernel(y0_ref, y1_ref, rw_ref, x_ref, mod_ref, fg_ref, o_ref):
    x = x_ref[...] + mod_ref[0, 5:6, :] * _moe_mix(y0_ref, y1_ref, rw_ref, slice(None))
    r = lax.rsqrt(jnp.mean(x * x, axis=-1, keepdims=True) + EPS)
    o_ref[...] = x * r * fg_ref[...]


def _final_kernel_into(y0_ref, y1_ref, rw_ref, x_ref, mod_ref, fg_ref, prev_ref, o_ref):
    del prev_ref
    _final_kernel(y0_ref, y1_ref, rw_ref, x_ref, mod_ref, fg_ref, o_ref)


def _final_combine(y, rw, xall, mods, final_g, *, tm, n_tiles, s, out_rows, out_tile0, prev_out=None):
    d = xall.shape[1]
    row = lambda w: pl.BlockSpec((tm, w), lambda i: (i, 0))
    in_specs = [row(d // 2),
                pl.BlockSpec((tm, d // 2), lambda i: (i + n_tiles, 0)),
                row(LANES), row(d),
                pl.BlockSpec((1, N_MOD, d), lambda i: ((i * tm) // s, 0, 0)),
                pl.BlockSpec(final_g.shape, lambda i: (0, 0))]
    args = [y, y, rw, xall, mods, final_g]
    aliases = {}
    body = _final_kernel
    if prev_out is not None:
        in_specs.append(pl.BlockSpec(memory_space=pl.ANY))
        args.append(prev_out)
        aliases = {len(args) - 1: 0}
        body = _final_kernel_into
    return pl.pallas_call(
        body,
        grid=(n_tiles,),
        in_specs=in_specs,
        out_specs=pl.BlockSpec((tm, d), lambda i: (out_tile0 + i, 0)),
        out_shape=jax.ShapeDtypeStruct((out_rows, d), jnp.float32),
        input_output_aliases=aliases,
        compiler_params=_cparams("arbitrary"),
        name="moe_combine_final",
    )(*args)


def _dest_kernel(ps_ref, meta_ref, o_ref):
    slot = pl.program_id(0)
    e = meta_ref[pl.ds(slot, 1), :]
    d = meta_ref[pl.ds(TOP_K + slot, 1), :]
    for k in range(N_EXPERTS):
        d = d + jnp.where(e == float(k), ps_ref[k], 0.0)
    o_ref[...] = d.astype(jnp.int32)


def _dest_rows(meta, pstarts, tcols):
    n = meta.shape[1]
    nt = n // tcols
    return pl.pallas_call(
        _dest_kernel,
        grid=(TOP_K, nt),
        in_specs=[pl.BlockSpec(memory_space=pltpu.SMEM),
                  pl.BlockSpec((META_ROWS, tcols), lambda k, i: (0, i))],
        out_specs=pl.BlockSpec((1, tcols), lambda k, i: (0, k * nt + i)),
        out_shape=jax.ShapeDtypeStruct((1, TOP_K * n), jnp.int32),
        compiler_params=_cparams("arbitrary", "arbitrary"),
        name="moe_dest",
    )(pstarts.astype(jnp.float32), meta)


def _dispatch_plan(meta, counts, n_rows_buf, tcols):
    cnt = counts[0, :N_EXPERTS].astype(jnp.int32)
    padded = (cnt + EXPERT_TILE - 1) // EXPERT_TILE * EXPERT_TILE
    pends = jnp.cumsum(padded)
    pstarts = pends - padded
    dest = _dest_rows(meta, pstarts, tcols)
    tile_start = jnp.arange(n_rows_buf // EXPERT_TILE, dtype=jnp.int32) * EXPERT_TILE
    te = jnp.sum((tile_start[:, None] >= pends[None, :]).astype(jnp.int32), axis=1)
    te = jnp.minimum(te, N_EXPERTS - 1)
    onehot = te[:, None] == jnp.arange(N_EXPERTS, dtype=jnp.int32)[None, :]
    cnt_te = jnp.sum(jnp.where(onehot, cnt[None, :], 0), axis=1)
    pstart_te = jnp.sum(jnp.where(onehot, pstarts[None, :], 0), axis=1)
    nvalid = jnp.clip(cnt_te - (tile_start - pstart_te), 0, EXPERT_TILE).astype(jnp.int32)
    return dest, te, nvalid


def _permute_heads(w, axis):
    heads = [lax.slice_in_dim(w, HEAD_DIM * h, HEAD_DIM * (h + 1), axis=axis)
             for h in HEAD_PERM[::HEAD_DIM] // HEAD_DIM]
    return jnp.concatenate(heads, axis=axis)


def _rope_tables(s, tm):
    pos = np.arange(s)
    pos_row = jnp.asarray(pos // GRID_W, jnp.float32)
    pos_col = jnp.asarray(pos % GRID_W, jnp.float32)
    n_freq = HEAD_DIM // 4
    inv = ROPE_THETA ** (-jnp.arange(n_freq, dtype=jnp.float32) / n_freq)
    ang_row = pos_row[:, None] * inv
    ang_col = pos_col[:, None] * inv
    ang = jnp.concatenate([ang_row, ang_row, ang_col, ang_col] * (LANES // HEAD_DIM), axis=-1)
    sign = np.where((np.arange(LANES) % 32) < 16, -1.0, 1.0).astype(np.float32)
    cos_t = jnp.concatenate([jnp.cos(ang), jnp.ones((tm, LANES), jnp.float32)], axis=0)
    sin_t = jnp.concatenate([jnp.sin(ang) * sign, jnp.zeros((tm, LANES), jnp.float32)], axis=0)
    return cos_t, sin_t


def kernel(x, c, ctx, c_ctx, norm1_g, norm2_g, w_mod, b_mod, w_in, q_norm_g, k_norm_g, conv_w, conv_b, conv_ln_g, conv_ln_b, sink, w_out, w_group, b_group, w_expert, b_expert, w_gate, w_up, w_down, final_g):
    nb, s, d = x.shape
    l = ctx.shape[1]
    depth = w_in.shape[0]
    assert w_in.shape[2] == D_IN and w_out.shape[1] == D_MIX
    assert s % GRID_W == 0 and s >= 3 * WINDOW and s % WINDOW == 0 and l % WINDOW == 0
    groups = BATCH_GROUPS if nb % BATCH_GROUPS == 0 else 1
    nbg = nb // groups
    n_lat, n_ctx = nbg * s, nbg * l
    tm = _pick(np.gcd(s, n_ctx), (512, 256, 128))
    tq = _pick(s, (512, 256, 128))
    assert n_lat % l == 0
    bf = jnp.bfloat16
    f32 = jnp.float32

    x2d = x.reshape(nb * s, d)
    ctx2d = ctx.reshape(nb * l, d)
    c_all = jnp.concatenate([c, c_ctx[None, :]], axis=0)
    mods_all = _modulation(c_all, w_mod, b_mod).reshape(depth, nb + 1, N_MOD, d)
    cos_t, sin_t = _rope_tables(s, tm)
    head_id = np.arange(LANES) // HEAD_DIM
    gsum = jnp.asarray((head_id[:, None] == head_id[None, :]) / HEAD_DIM, bf)
    tri = jnp.asarray(np.tril(np.ones((tm, tm), np.float32), -1), bf)
    n_lat_tiles = n_lat // tm

    xs = [(x2d, ctx2d)] * groups
    pending = [None] * groups
    for i in range(depth):
        last = i == depth - 1
        with_ctx = not last
        qg = jnp.tile(q_norm_g[i], LANES // HEAD_DIM)[None, :]
        kg = jnp.tile(k_norm_g[i], LANES // HEAD_DIM)[None, :]
        w_in_bf = jnp.concatenate(
            [_permute_heads(w_in[i][:, OFF_AQ:OFF_AK], 1), w_in[i][:, OFF_AK:OFF_CQ],
             _permute_heads(w_in[i][:, OFF_CQ:OFF_CK], 1), w_in[i][:, OFF_CK:]], axis=1).astype(bf)
        w_out_bf = jnp.concatenate(
            [_permute_heads(w_out[i][0:A_Q], 0), w_out[i][A_Q:A_Q + B_CH],
             _permute_heads(w_out[i][A_Q + B_CH:], 0)], axis=0).astype(bf)
        sink2 = sink[i] * LOG2E
        conv_args = (conv_w[i].reshape(CONV_W, B_CH), conv_b[i][None, :], conv_ln_g[i][None, :],
                     conv_ln_b[i][None, :])
        wr32 = jnp.zeros((d, LANES), f32).at[:, :N_GROUPS].set(w_group[i])
        wr32 = wr32.at[:, N_GROUPS:N_GROUPS + N_EXPERTS].set(w_expert[i])
        wr_hi = wr32.astype(bf)
        wr = jnp.concatenate([wr_hi, (wr32 - wr_hi.astype(f32)).astype(bf)], axis=1)
        br = jnp.zeros((1, LANES), f32).at[0, :N_GROUPS].set(b_group[i])
        br = br.at[0, N_GROUPS:N_GROUPS + N_EXPERTS].set(b_expert[i])
        n_tok = n_lat + n_ctx if with_ctx else n_lat
        n_tiles = n_tok // tm
        n_rows_buf = 2 * n_tok + N_EXPERTS * EXPERT_TILE
        for g in range(groups):
            b0 = g * nbg
            mods = jnp.concatenate([mods_all[i, b0:b0 + nbg], mods_all[i, nb:nb + 1]], axis=0)
            x_offsets = (b0 * s // tm, b0 * l // tm)
            x_new, qa, kta, va, hgl, qc, ktc, vc = _inproj(
                xs[g], mods, norm1_g[i][None, :], w_in_bf, gsum, qg, kg, cos_t, sin_t, n=n_lat + n_ctx,
                tm=tm, n_lat_tiles=n_lat_tiles, s=s, nb=nbg, x_offsets=x_offsets, combine=pending[g])
            if pending[g] is not None:
                xs[g] = (x_new,)
            oa, ob, oc = _latent_mixers(qa, kta, va, qc, ktc, vc, sink2, hgl, conv_args,
                                        nb=nbg, s=s, l=l, tq=tq)
            if with_ctx:
                oa, ob, oc = _ctx_mixers(qa, kta, va, qc, ktc, vc, sink2, hgl, conv_args, oa, ob, oc,
                                         nb=nbg, s=s, l=l)
            xall, hp, meta, rw, counts = _outproj(xs[g], oa, ob, oc, mods, norm2_g[i][None, :], w_out_bf,
                                                  wr, br, tri, n=n_lat + n_ctx, tm=tm, n_tiles=n_tiles,
                                                  n_lat_tiles=n_lat_tiles, s=s, nb=nbg, x_offsets=x_offsets)
            dest, te, nvalid = _dispatch_plan(meta, counts, n_rows_buf,
                                              _pick(n_tok, (8192, 4096, 2048, 1024, 512, 256, 128)))
            buf = sc_scatter_rows2(hp, dest, n_rows_buf)
            eo = _experts(buf, te, nvalid, w_gate, w_up, w_down, layer=i)
            y = sc_gather_rows(eo, dest)
            xs[g] = (xall,)
            pending[g] = (y, rw, mods)
    out = None
    for g in range(groups):
        y, rw, mods = pending[g]
        out = _final_combine(y, rw, xs[g][0], mods, final_g[None, :], tm=tm, n_tiles=n_lat_tiles, s=s,
                             out_rows=nb * s, out_tile0=g * n_lat_tiles, prev_out=out)
    return out.reshape(nb, s, d)
```

```python
import functools

import jax
import jax.numpy as jnp
import numpy as np
from jax import lax
from jax.experimental import pallas as pl
from jax.experimental.pallas import tpu as pltpu
from jax.experimental.pallas import tpu_sc as plsc

HEAD_DIM = 64
GRID_W = 64
ROPE_THETA = 10000.0
A_HEADS, A_KV_HEADS = 6, 2
C_HEADS, C_KV_HEADS = 6, 2
B_CH = 256
CONV_W = 31
WINDOW = 128
N_GROUPS = 4
EXPERTS_PER_GROUP = 8
N_EXPERTS = N_GROUPS * EXPERTS_PER_GROUP
TOP_K = 2
N_MOD = 6
EPS = 1e-6
ATTN_SCALE = HEAD_DIM ** -0.5
LOG2E = 1.4426950408889634
Q_SCALE = ATTN_SCALE * LOG2E

A_Q = A_HEADS * HEAD_DIM
A_KV = A_KV_HEADS * HEAD_DIM
C_Q = C_HEADS * HEAD_DIM
C_KV = C_KV_HEADS * HEAD_DIM
D_MIX = A_Q + B_CH + C_Q
OFF_AQ = 0
OFF_AK = OFF_AQ + A_Q
OFF_AV = OFF_AK + A_KV
OFF_BU = OFF_AV + A_KV
OFF_CQ = OFF_BU + 2 * B_CH
OFF_CK = OFF_CQ + C_Q
OFF_CV = OFF_CK + C_KV
D_IN = OFF_CV + C_KV

LANES = 128
SUBLANES = 8
KT_ROWS = LANES
V_W = 2 * LANES
HEAD_PERM = np.concatenate([np.arange(HEAD_DIM) + HEAD_DIM * h for b in range(3) for h in (b, b + 3)])
EXPERT_TILE = 1536
SC_WINDOW = 128
VMEM_LIMIT = 56 * 1024 * 1024
HI_MASK = -65536
BATCH_GROUPS = 2
ROW_CHAIN = 256


def _cparams(*sem):
    return pltpu.CompilerParams(dimension_semantics=sem, vmem_limit_bytes=VMEM_LIMIT)


def _pick(n, cands):
    for c in cands:
        if n % c == 0:
            return c
    raise ValueError(f"no tile in {cands} divides {n}")


def _pack_pairs(x):
    w = x.shape[1] // 2
    lo = lax.bitcast_convert_type(x[:, :w].astype(jnp.bfloat16).astype(jnp.float32), jnp.int32)
    hi = lax.bitcast_convert_type(x[:, w:].astype(jnp.bfloat16).astype(jnp.float32), jnp.int32)
    return (hi & HI_MASK) | lax.shift_right_logical(lo, 16)


def _unpack_pairs(p):
    lo = lax.bitcast_convert_type(lax.shift_left(p, 16), jnp.float32)
    hi = lax.bitcast_convert_type(p & HI_MASK, jnp.float32)
    return lo, hi


def _mod_kernel(c_ref, w_ref, b_ref, o_ref):
    c = c_ref[...]
    a = c * jax.nn.sigmoid(c)
    o_ref[0] = jnp.dot(a, w_ref[0], preferred_element_type=jnp.float32,
                       precision=lax.Precision.HIGHEST) + b_ref[0]


def _modulation(c_all, w_mod, b_mod):
    depth, d, n = w_mod.shape
    r = c_all.shape[0]
    tn = _pick(n, (1024, 512, 256, 128))
    return pl.pallas_call(
        _mod_kernel,
        grid=(depth, n // tn),
        in_specs=[pl.BlockSpec((r, d), lambda l, j: (0, 0)),
                  pl.BlockSpec((1, d, tn), lambda l, j: (l, 0, j)),
                  pl.BlockSpec((1, 1, tn), lambda l, j: (l, 0, j))],
        out_specs=pl.BlockSpec((1, r, tn), lambda l, j: (l, 0, j)),
        out_shape=jax.ShapeDtypeStruct((depth, r, n), jnp.float32),
        compiler_params=_cparams("arbitrary", "arbitrary"),
        name="modulation",
    )(c_all, w_mod, b_mod.reshape(depth, 1, n))


def _head_mean_sq(blk, gsum):
    sq = blk * blk
    hi = sq.astype(jnp.bfloat16)
    lo = (sq - hi.astype(jnp.float32)).astype(jnp.bfloat16)
    return (jnp.dot(hi, gsum, preferred_element_type=jnp.float32)
            + jnp.dot(lo, gsum, preferred_element_type=jnp.float32))


def _x_specs(xs, tm, n_lat_tiles, offsets=(0, 0), tile_of=lambda i: i):
    d = xs[0].shape[1]
    if len(xs) == 1:
        return [pl.BlockSpec((tm, d), lambda i: (tile_of(i), 0))]
    lat0, ctx0 = offsets
    return [pl.BlockSpec((tm, d), lambda i: (lat0 + jnp.minimum(tile_of(i), n_lat_tiles - 1), 0)),
            pl.BlockSpec((tm, d), lambda i: (ctx0 + jnp.maximum(tile_of(i) - n_lat_tiles, 0), 0))]


def _load_x(x_refs, n_lat_tiles, rows=slice(None), tile=None):
    if len(x_refs) == 1:
        return x_refs[0][rows, :]
    tile = pl.program_id(0) if tile is None else tile
    return jnp.where(tile < n_lat_tiles, x_refs[0][rows, :], x_refs[1][rows, :])


def _moe_mix(y0_ref, y1_ref, rw_ref, rows):
    rw = rw_ref[rows, :]
    w0 = rw[:, 0:1]
    w1 = rw[:, 1:2]
    a_lo, a_hi = _unpack_pairs(y0_ref[rows, :])
    b_lo, b_hi = _unpack_pairs(y1_ref[rows, :])
    return jnp.concatenate([a_lo * w0 + b_lo * w1, a_hi * w0 + b_hi * w1], axis=1)


def _inproj_kernel(*refs, n_x, n_lat_tiles, fused_combine):
    x_refs = refs[:n_x]
    refs = refs[n_x:]
    if fused_combine:
        y0_ref, y1_ref, rw_ref, modp_ref = refs[:4]
        refs = refs[4:]
        xo_ref = refs[-1]
        refs = refs[:-1]
    (mod_ref, g_ref, w_ref, gsum_ref, qg_ref, kg_ref, cos_ref, sin_ref,
     qa_ref, kta_ref, va_ref, hgl_ref, qc_ref, ktc_ref, vc_ref) = refs
    tm = qa_ref.shape[0]
    tr = min(tm, ROW_CHAIN)
    bf = jnp.bfloat16
    lane = lax.broadcasted_iota(jnp.int32, (tr, LANES), 1)
    first16 = (lane % 32) < 16
    gsum = gsum_ref[...]
    qg = qg_ref[...]
    kg = kg_ref[...]
    scale = g_ref[...] * (1.0 + mod_ref[0, 1:2, :])
    shift = mod_ref[0, 0:1, :]
    ones = jnp.ones((tr, LANES), bf)

    for r0 in range(0, tm, tr):
        rows = slice(r0, r0 + tr)
        x = _load_x(x_refs, n_lat_tiles, rows)
        if fused_combine:
            x = x + modp_ref[0, 5:6, :] * _moe_mix(y0_ref, y1_ref, rw_ref, rows)
            xo_ref[rows, :] = x
        r = lax.rsqrt(jnp.mean(x * x, axis=-1, keepdims=True) + EPS)
        h = (x * r) * scale + shift
        p = jnp.dot(h.astype(bf), w_ref[...], preferred_element_type=jnp.float32)
        cos = cos_ref[rows, :]
        sin = sin_ref[rows, :]

        def blk(off):
            return p[:, off:off + LANES]

        def rope(t):
            sw = jnp.where(first16, pltpu.roll(t, LANES - 16, axis=1), pltpu.roll(t, 16, axis=1))
            return t * cos + sw * sin

        def norm(t, g):
            return t * lax.rsqrt(_head_mean_sq(t, gsum) + EPS) * g

        def store_kv(kt_ref, v_ref, k_blk, v_blk):
            kt_ref[:, rows] = k_blk.T.astype(bf)
            v_ref[rows, 0:LANES] = v_blk.astype(bf)
            v_ref[rows, LANES:2 * LANES] = ones

        for i in range(A_Q // LANES):
            t = rope(norm(blk(OFF_AQ + i * LANES), qg)) * Q_SCALE
            qa_ref[rows, i * LANES:(i + 1) * LANES] = t.astype(bf)
        store_kv(kta_ref, va_ref, rope(norm(blk(OFF_AK), kg)), blk(OFF_AV))
        for i in range(B_CH // LANES):
            a = blk(OFF_BU + i * LANES)
            gt = blk(OFF_BU + B_CH + i * LANES)
            hgl_ref[rows, i * LANES:(i + 1) * LANES] = a * jax.nn.sigmoid(gt)
        for i in range(C_Q // LANES):
            t = rope(blk(OFF_CQ + i * LANES)) * Q_SCALE
            qc_ref[rows, i * LANES:(i + 1) * LANES] = t.astype(bf)
        store_kv(ktc_ref, vc_ref, rope(blk(OFF_CK)), blk(OFF_CV))


def _inproj(xs, mods, g1, w_in_bf, gsum, qg, kg, cos_t, sin_t, *, n, tm, n_lat_tiles, s, nb, x_offsets=(0, 0),
            combine=None):
    d = xs[0].shape[1]
    s_tiles = s // tm
    n_tiles = n // tm

    def bidx(i):
        return jnp.where(i < n_lat_tiles, (i * tm) // s, nb)

    def ridx(i):
        return jnp.where(i < n_lat_tiles, i % s_tiles, s_tiles)

    row = lambda w: pl.BlockSpec((tm, w), lambda i: (i, 0))
    ktspec = pl.BlockSpec((KT_ROWS, tm), lambda i: (0, i))
    const = lambda a: pl.BlockSpec(a.shape, lambda i: (0,) * a.ndim)
    bf = jnp.bfloat16
    modspec = pl.BlockSpec((1, N_MOD, d), lambda i: (bidx(i), 0, 0))
    in_specs = _x_specs(xs, tm, n_lat_tiles, x_offsets)
    args = list(xs)
    out_specs = [row(A_Q), ktspec, row(V_W), row(B_CH), row(C_Q), ktspec, row(V_W)]
    out_shape = [jax.ShapeDtypeStruct((n, A_Q), bf), jax.ShapeDtypeStruct((KT_ROWS, n), bf),
                 jax.ShapeDtypeStruct((n, V_W), bf), jax.ShapeDtypeStruct((n, B_CH), jnp.float32),
                 jax.ShapeDtypeStruct((n, C_Q), bf), jax.ShapeDtypeStruct((KT_ROWS, n), bf),
                 jax.ShapeDtypeStruct((n, V_W), bf)]
    aliases = {}
    if combine is not None:
        y, rw, mods_prev = combine
        assert len(xs) == 1 and y.shape[0] == 2 * n
        in_specs += [row(d // 2), pl.BlockSpec((tm, d // 2), lambda i: (i + n_tiles, 0)), row(LANES), modspec]
        args += [y, y, rw, mods_prev]
        out_specs.append(row(d))
        out_shape.append(jax.ShapeDtypeStruct((n, d), jnp.float32))
        aliases = {0: len(out_shape) - 1}
    in_specs += [modspec, const(g1), const(w_in_bf), const(gsum), const(qg), const(kg),
                 pl.BlockSpec((tm, LANES), lambda i: (ridx(i), 0)),
                 pl.BlockSpec((tm, LANES), lambda i: (ridx(i), 0))]
    args += [mods, g1, w_in_bf, gsum, qg, kg, cos_t, sin_t]
    outs = pl.pallas_call(
        functools.partial(_inproj_kernel, n_x=len(xs), n_lat_tiles=n_lat_tiles,
                          fused_combine=combine is not None),
        grid=(n_tiles,),
        in_specs=in_specs,
        out_specs=out_specs,
        out_shape=out_shape,
        input_output_aliases=aliases,
        compiler_params=_cparams("arbitrary"),
        name="inproj",
    )(*args)
    if combine is not None:
        return (outs[-1],) + tuple(outs[:-1])
    return (None,) + tuple(outs)


def _stack_heads(q_ref, tq):
    lane = lax.broadcasted_iota(jnp.int32, (tq, LANES), 1)
    lo = lane < HEAD_DIM
    qb = [q_ref[:, i * LANES:(i + 1) * LANES] for i in range(3)]
    zero = jnp.zeros_like(qb[0])
    s0 = jnp.concatenate([jnp.where(lo, t, zero) for t in qb], axis=0)
    s1 = jnp.concatenate([jnp.where(lo, zero, t) for t in qb], axis=0)
    return s0, s1, lo


def _unstack_store(o_ref, o0, o1, lo, tq):
    for i in range(3):
        rows = slice(i * tq, (i + 1) * tq)
        o_ref[:, i * LANES:(i + 1) * LANES] = jnp.where(lo, o0[rows], o1[rows]).astype(jnp.bfloat16)


def _krow(kv):
    del kv
    return slice(0, KT_ROWS)


def _vcol(kv):
    del kv
    return slice(0, V_W)


def _row_max(scores):
    m = None
    for t in scores:
        for c in range(0, t.shape[1], LANES):
            blk = t[:, c:c + LANES]
            m = blk if m is None else jnp.maximum(m, blk)
    return m.max(axis=-1, keepdims=True)


def _softmax_pv(scores, values, extra=None):
    m = _row_max(scores)
    if extra is not None:
        m = jnp.maximum(m, extra)
    acc = None
    for t, v in zip(scores, values):
        c = jnp.dot(jnp.exp2(t - m).astype(jnp.bfloat16), v, preferred_element_type=jnp.float32)
        acc = c if acc is None else acc + c
    den = acc[:, LANES:2 * LANES]
    if extra is not None:
        den = den + jnp.exp2(extra - m)
    return acc[:, 0:LANES] / den


ATTN_CHAIN_ROWS = 128


def _qk(q, kt):
    return jnp.dot(q, kt, preferred_element_type=jnp.float32)


def _attn_a_kernel(q_ref, ktl_ref, ktc_ref, vl_ref, vc_ref, *rest, tq, n_q):
    conv_refs = rest[:7]
    o_ref, ob_ref, sh_ref = rest[7:]
    j = pl.program_id(1)
    conv_steps = _conv_steps(conv_refs, ob_ref, sh_ref, j > 0, j < n_q - 1, tq)
    s0, s1, lo = _stack_heads(q_ref, tq)
    n_chains = 2 * (3 * tq // ATTN_CHAIN_ROWS)
    chain = 0
    conv_done = 0
    outs = []
    for kv, qs in enumerate((s0, s1)):
        krow = _krow(kv)
        vcol = _vcol(kv)
        parts = []
        for r0 in range(0, 3 * tq, ATTN_CHAIN_ROWS):
            qr = qs[r0:r0 + ATTN_CHAIN_ROWS]
            ss = [_qk(qr, ktl_ref[krow, :]), _qk(qr, ktc_ref[krow, :])]
            parts.append(_softmax_pv(ss, [vl_ref[:, vcol], vc_ref[:, vcol]]))
            chain += 1
            while conv_done < chain * len(conv_steps) // n_chains:
                conv_steps[conv_done]()
                conv_done += 1
        outs.append(jnp.concatenate(parts, axis=0))
    _unstack_store(o_ref, outs[0], outs[1], lo, tq)


def _kv_specs(nb, s, l):
    ctx0 = nb * s // l
    return [pl.BlockSpec((KT_ROWS, s), lambda b, j: (0, b)),
            pl.BlockSpec((KT_ROWS, l), lambda b, j: (0, ctx0 + b)),
            pl.BlockSpec((s, V_W), lambda b, j: (b, 0)),
            pl.BlockSpec((l, V_W), lambda b, j: (ctx0 + b, 0))]


def _sink_column(sink_ref, kv, rows):
    return jnp.concatenate([jnp.full((rows, 1), sink_ref[3 * kv + g], jnp.float32) for g in range(3)], axis=0)


def _attn_c_kernel(sink_ref, bias_ref, q_ref, ktl_ref, ktc_ref, vl_ref, vc_ref, o_ref, *, s, blocks):
    j = pl.program_id(1)
    tq = WINDOW
    band = 3 * WINDOW
    bf = jnp.bfloat16
    f32 = jnp.float32
    m3 = 3 * tq
    stacks = [_stack_heads(q_ref.at[blk * tq:(blk + 1) * tq, :], tq) for blk in range(blocks)]
    lo = stacks[0][2]
    starts = [pl.multiple_of(jnp.clip((j * blocks + blk - 1) * WINDOW, 0, s - band), WINDOW)
              for blk in range(blocks)]
    outs = [[None, None] for _ in range(blocks)]
    for kv in range(C_KV_HEADS):
        krow = _krow(kv)
        vcol = _vcol(kv)
        sk = _sink_column(sink_ref, kv, tq)
        sc_all = _qk(jnp.concatenate([st[kv] for st in stacks], axis=0), ktc_ref[krow, :])
        accs, pcs, ms = [], [], []
        for blk in range(blocks):
            start = starts[blk]
            sl = _qk(stacks[blk][kv], ktl_ref[krow, pl.ds(start, band)]) + bias_ref[j * blocks + blk - start // WINDOW]
            sc = sc_all[blk * m3:(blk + 1) * m3]
            m = jnp.maximum(_row_max([sl, sc]), sk)
            accs.append(jnp.dot(jnp.exp2(sl - m).astype(bf), vl_ref[pl.ds(start, band), vcol],
                                preferred_element_type=f32))
            pcs.append(jnp.exp2(sc - m).astype(bf))
            ms.append(m)
        acc_ctx = jnp.dot(jnp.concatenate(pcs, axis=0), vc_ref[:, vcol], preferred_element_type=f32)
        for blk in range(blocks):
            acc = accs[blk] + acc_ctx[blk * m3:(blk + 1) * m3]
            den = acc[:, LANES:2 * LANES] + jnp.exp2(sk - ms[blk])
            outs[blk][kv] = acc[:, 0:LANES] / den
    for blk in range(blocks):
        _unstack_store(o_ref.at[blk * tq:(blk + 1) * tq, :], outs[blk][0], outs[blk][1], lo, tq)


def _window_bias():
    r = np.arange(3 * WINDOW)[:, None] % WINDOW
    col = np.arange(3 * WINDOW)[None, :]
    tabs = [np.where(np.abs(col - r - WINDOW * off) <= WINDOW, 0.0, -np.inf) for off in range(3)]
    return jnp.asarray(np.stack(tabs), jnp.float32)


def _mixers_kernel(*refs, tq, n_q, s):
    sink_ref, bias_ref, qc_ref = refs[0:3]
    kvc_refs = refs[3:7]
    qa_ref = refs[7]
    kva_refs = refs[8:12]
    conv_refs = refs[12:19]
    oa_ref, ob_ref, oc_ref, sh_ref = refs[19:]
    _attn_c_kernel(sink_ref, bias_ref, qc_ref, *kvc_refs, oc_ref, s=s, blocks=tq // WINDOW)
    _attn_a_kernel(qa_ref, *kva_refs, *conv_refs, oa_ref, ob_ref, sh_ref, tq=tq, n_q=n_q)


def _latent_mixers(qa, kta, va, qc, ktc, vc, sink2, hgl, conv_params, *, nb, s, l, tq):
    n = qa.shape[0]
    n_q = s // tq
    bias = _window_bias()
    conv_specs, conv_blk = _conv_specs(n, conv_params, chunks=n_q, base=0, tc=tq)
    qspec = lambda w: pl.BlockSpec((tq, w), lambda b, j: (b * n_q + j, 0))
    bf = jnp.bfloat16
    return pl.pallas_call(
        functools.partial(_mixers_kernel, tq=tq, n_q=n_q, s=s),
        grid=(nb, n_q),
        in_specs=[pl.BlockSpec(memory_space=pltpu.SMEM), pl.BlockSpec(bias.shape, lambda b, j: (0, 0, 0)),
                  qspec(C_Q)] + _kv_specs(nb, s, l) + [qspec(A_Q)] + _kv_specs(nb, s, l) + conv_specs,
        out_specs=[qspec(A_Q), conv_blk, qspec(C_Q)],
        out_shape=[jax.ShapeDtypeStruct((n, A_Q), bf), jax.ShapeDtypeStruct((n, B_CH), bf),
                   jax.ShapeDtypeStruct((n, C_Q), bf)],
        scratch_shapes=[pltpu.VMEM((SUBLANES, tq + 2 * CONV_HALO, B_CH), jnp.float32)],
        compiler_params=_cparams("arbitrary", "arbitrary"),
        name="latent_mixers",
    )(sink2, bias, qc, ktc, ktc, vc, vc, qa, kta, kta, va, va, hgl, hgl, hgl, *conv_params)


def _ctx_mixers_kernel(sink_ref, qa_ref, kta_ref, va_ref, qc_ref, ktc_ref, vc_ref, *rest, l):
    conv_refs = rest[:7]
    oa_ref, ob_ref, oc_ref, sh_ref = rest[-4:]
    conv_steps = _conv_steps(conv_refs, ob_ref, sh_ref, False, False, l)
    for q_ref, kt_ref, v_ref, o_ref, with_sink in ((qa_ref, kta_ref, va_ref, oa_ref, False),
                                                   (qc_ref, ktc_ref, vc_ref, oc_ref, True)):
        s0, s1, lo = _stack_heads(q_ref, l)
        outs = []
        for kv, qs in enumerate((s0, s1)):
            sk = _sink_column(sink_ref, kv, l) if with_sink else None
            outs.append(_softmax_pv([_qk(qs, kt_ref[_krow(kv), :])], [v_ref[:, _vcol(kv)]], sk))
            for step in conv_steps[(2 * with_sink + kv) * len(conv_steps) // 4:
                                   (2 * with_sink + kv + 1) * len(conv_steps) // 4]:
                step()
        _unstack_store(o_ref, outs[0], outs[1], lo, l)


def _ctx_mixers(qa, kta, va, qc, ktc, vc, sink2, hgl, conv_params, oa, ob, oc, *, nb, s, l):
    n = qa.shape[0]
    base = nb * s // l
    row = lambda w: pl.BlockSpec((l, w), lambda b, j: (base + b, 0))
    ktspec = pl.BlockSpec((KT_ROWS, l), lambda b, j: (0, base + b))
    conv_specs, conv_blk = _conv_specs(n, conv_params, chunks=1, base=base, tc=l)
    anyspec = pl.BlockSpec(memory_space=pl.ANY)
    n_in = 7 + len(conv_specs)
    return pl.pallas_call(
        functools.partial(_ctx_mixers_kernel, l=l),
        grid=(nb, 1),
        in_specs=[pl.BlockSpec(memory_space=pltpu.SMEM), row(A_Q), ktspec, row(V_W), row(C_Q), ktspec, row(V_W)]
        + conv_specs + [anyspec, anyspec, anyspec],
        out_specs=[row(A_Q), conv_blk, row(C_Q)],
        out_shape=[jax.ShapeDtypeStruct(a.shape, a.dtype) for a in (oa, ob, oc)],
        scratch_shapes=[pltpu.VMEM((SUBLANES, l + 2 * CONV_HALO, B_CH), jnp.float32)],
        input_output_aliases={n_in: 0, n_in + 1: 1, n_in + 2: 2},
        compiler_params=_cparams("arbitrary", "arbitrary"),
        name="context_mixers",
    )(sink2, qa, kta, va, qc, ktc, vc, hgl, hgl, hgl, *conv_params, oa, ob, oc)


CONV_HALO = 16
CONV_ROWS = 64


def _conv_steps(conv_refs, o_ref, sh_ref, has_prev, has_next, tc):
    prev_ref, cur_ref, next_ref, w_ref, b_ref, g_ref, beta_ref = conv_refs
    rows = tc + 2 * CONV_HALO
    sh_ref[0, 0:CONV_HALO, :] = jnp.where(has_prev, prev_ref[...], 0.0)
    sh_ref[0, CONV_HALO:CONV_HALO + tc, :] = cur_ref[...]
    sh_ref[0, CONV_HALO + tc:rows, :] = jnp.where(has_next, next_ref[...], 0.0)
    for b in range(1, SUBLANES):
        sh_ref[b, 0:rows - SUBLANES, :] = sh_ref[0, b:b + rows - SUBLANES, :]
    base = CONV_HALO - CONV_W // 2

    def step(r0):
        acc = None
        for k in range(CONV_W):
            a, b = divmod(base + k, SUBLANES)
            term = sh_ref[b, SUBLANES * a + r0:SUBLANES * a + r0 + CONV_ROWS, :] * w_ref[k:k + 1, :]
            acc = term if acc is None else acc + term
        hc = acc + b_ref[...]
        mu = jnp.mean(hc, axis=-1, keepdims=True)
        xc = hc - mu
        var = jnp.mean(xc * xc, axis=-1, keepdims=True)
        y = xc * lax.rsqrt(var + EPS) * g_ref[...] + beta_ref[...]
        o_ref[r0:r0 + CONV_ROWS, :] = (y * jax.nn.sigmoid(y)).astype(o_ref.dtype)

    return [functools.partial(step, r0) for r0 in range(0, tc, CONV_ROWS)]


def _conv_specs(n, params, *, chunks, base, tc):
    hb = tc // CONV_HALO
    last_halo = n // CONV_HALO - 1
    idx = lambda b_, j: base + b_ * chunks + j
    blk = pl.BlockSpec((tc, B_CH), lambda b_, j: (idx(b_, j), 0))
    prev_halo = pl.BlockSpec((CONV_HALO, B_CH), lambda b_, j: (jnp.maximum(idx(b_, j) * hb - 1, 0), 0))
    next_halo = pl.BlockSpec((CONV_HALO, B_CH), lambda b_, j: (jnp.minimum((idx(b_, j) + 1) * hb, last_halo), 0))
    const = lambda a: pl.BlockSpec(a.shape, lambda b_, j: (0,) * a.ndim)
    return [prev_halo, blk, next_halo] + [const(a) for a in params], blk


META_ROWS = 8


def _outproj_kernel(*refs, n_x, n_lat_tiles, n_tiles):
    x_refs = refs[:n_x]
    (oa_ref, ob_ref, oc_ref, mod_ref, g_ref, w_ref, wr_ref, br_ref, tri_ref,
     xo_ref, hp_ref, meta_ref, rw_ref, cnt_ref, lg_ref) = refs[n_x:]
    i = pl.program_id(0)
    tm = xo_ref.shape[0]
    f32 = jnp.float32

    @pl.when(i == 0)
    def _():
        cnt_ref[...] = jnp.zeros_like(cnt_ref)
        lg_ref[...] = jnp.zeros_like(lg_ref)

    lat = jnp.concatenate([oa_ref[...], ob_ref[...], oc_ref[...]], axis=1)
    mix = jnp.dot(lat, w_ref[...], preferred_element_type=f32)
    x = _load_x(x_refs, n_lat_tiles, tile=jnp.minimum(i, n_tiles - 1)) + mod_ref[0, 2:3, :] * mix
    xo_ref[...] = x
    r = lax.rsqrt(jnp.mean(x * x, axis=-1, keepdims=True) + EPS)
    h = (x * r) * (g_ref[...] * (1.0 + mod_ref[0, 4:5, :])) + mod_ref[0, 3:4, :]
    h_hi = h.astype(jnp.bfloat16)
    hp_ref[...] = _pack_pairs(h_hi)
    h_lo = (h - h_hi.astype(f32)).astype(jnp.bfloat16)
    r_hi = jnp.dot(h_hi, wr_ref[...], preferred_element_type=f32)
    r_lo = jnp.dot(h_lo, wr_ref[:, 0:LANES], preferred_element_type=f32)
    lg = lg_ref[(i + 1) % 2]
    lg_ref[i % 2] = r_hi[:, 0:LANES] + r_hi[:, LANES:2 * LANES] + r_lo + br_ref[...]

    lane = lax.broadcasted_iota(jnp.int32, (tm, LANES), 1).astype(f32)
    big = float(LANES)
    ninf = -jnp.inf
    glog = jnp.where(lane < N_GROUPS, lg, ninf)
    gmax = glog.max(axis=-1, keepdims=True)
    g_val = 1.0 / jnp.exp(glog - gmax).sum(axis=-1, keepdims=True)
    g_idx = jnp.where(glog == gmax, lane, big).min(axis=-1, keepdims=True)
    e_lo = N_GROUPS + EXPERTS_PER_GROUP * g_idx
    el = jnp.where((lane >= e_lo) & (lane < e_lo + EXPERTS_PER_GROUP), lg, ninf)
    v0 = el.max(axis=-1, keepdims=True)
    i0 = jnp.where(el == v0, lane, big).min(axis=-1, keepdims=True)
    el1 = jnp.where(lane == i0, ninf, el)
    v1 = el1.max(axis=-1, keepdims=True)
    i1 = jnp.where(el1 == v1, lane, big).min(axis=-1, keepdims=True)
    t = jnp.exp(v1 - v0)
    w0 = g_val / (1.0 + t)
    w1 = g_val * t / (1.0 + t)
    e0 = i0 - N_GROUPS
    e1 = i1 - N_GROUPS

    cnt = cnt_ref[0:1, :]
    tri = tri_ref[...]
    ranks = []
    for e in (e0, e1):
        oh = lane == e
        ohf = oh.astype(f32)
        pre = jnp.dot(tri, ohf.astype(jnp.bfloat16), preferred_element_type=f32) + cnt
        ranks.append(jnp.where(oh, pre, 0.0).sum(axis=-1, keepdims=True))
        cnt = cnt + ohf.sum(axis=0, keepdims=True)
    cnt_ref[0:1, :] = jnp.where(i > 0, cnt, cnt_ref[0:1, :])
    rw_ref[...] = jnp.where(lane == 0, w0, jnp.where(lane == 1, w1, 0.0))
    rec = jnp.where(lane == 0, e0, jnp.where(lane == 1, e1, jnp.where(lane == 2, ranks[0],
                    jnp.where(lane == 3, ranks[1], jnp.where(lane == 4, w0, jnp.where(lane == 5, w1, 0.0))))))
    meta_ref[...] = rec.T[0:META_ROWS, :]


def _outproj(xs, oa, ob, oc, mods, g2, w_out_bf, wr, br, tri, *, n, tm, n_tiles, n_lat_tiles, s, nb,
             x_offsets=(0, 0)):
    d = xs[0].shape[1]
    rows = n_tiles * tm
    cur = lambda i: jnp.minimum(i, n_tiles - 1)
    prev = lambda i: jnp.maximum(i - 1, 0)

    def bidx(i):
        return jnp.where(cur(i) < n_lat_tiles, (cur(i) * tm) // s, nb)

    row = lambda w: pl.BlockSpec((tm, w), lambda i: (cur(i), 0))
    const = lambda a: pl.BlockSpec(a.shape, lambda i: (0,) * a.ndim)
    return pl.pallas_call(
        functools.partial(_outproj_kernel, n_x=len(xs), n_lat_tiles=n_lat_tiles, n_tiles=n_tiles),
        grid=(n_tiles + 1,),
        in_specs=_x_specs(xs, tm, n_lat_tiles, x_offsets, cur) + [
                  row(A_Q), row(B_CH), row(C_Q),
                  pl.BlockSpec((1, N_MOD, d), lambda i: (bidx(i), 0, 0)),
                  const(g2), const(w_out_bf), const(wr), const(br), const(tri)],
        out_specs=[row(d), row(d // 2),
                   pl.BlockSpec((META_ROWS, tm), lambda i: (0, prev(i))),
                   pl.BlockSpec((tm, LANES), lambda i: (prev(i), 0)),
                   pl.BlockSpec((8, LANES), lambda i: (0, 0))],
        out_shape=[jax.ShapeDtypeStruct((n, d), jnp.float32),
                   jax.ShapeDtypeStruct((rows, d // 2), jnp.int32),
                   jax.ShapeDtypeStruct((META_ROWS, rows), jnp.float32),
                   jax.ShapeDtypeStruct((rows, LANES), jnp.float32),
                   jax.ShapeDtypeStruct((8, LANES), jnp.float32)],
        scratch_shapes=[pltpu.VMEM((2, tm, LANES), jnp.float32)],
        input_output_aliases={0: 0} if len(xs) == 1 else {},
        compiler_params=_cparams("arbitrary"),
        name="outproj_router",
    )(*xs, oa, ob, oc, mods, g2, w_out_bf, wr, br, tri)


def _sc_mesh():
    return plsc.VectorSubcoreMesh(core_axis_name="core", subcore_axis_name="subcore")


def sc_gather_rows(table, idx2):
    r = idx2.shape[1]
    w = table.shape[1]
    assert r % (2 * SC_WINDOW) == 0
    half = r // SC_WINDOW // 2

    @functools.partial(pl.kernel, out_type=jax.ShapeDtypeStruct((r, w), table.dtype), mesh=_sc_mesh())
    def k(x_hbm, i_hbm, o_hbm):
        def body(i_vmem, o_vmem):
            pltpu.sync_copy(x_hbm.at[i_vmem.at[0]], o_vmem)

        pltpu.emit_pipeline(
            body,
            grid=(2, half),
            in_specs=[pl.BlockSpec((1, SC_WINDOW), lambda c, i: (0, c * half + i))],
            out_specs=[pl.BlockSpec((SC_WINDOW, w), lambda c, i: (c * half + i, 0),
                                    pipeline_mode=pl.Buffered(1))],
            core_axis_name=("core", "subcore"),
            dimension_semantics=(pltpu.PARALLEL, pltpu.PARALLEL),
        )(i_hbm, o_hbm)

    return k(table, idx2)


def sc_scatter_rows2(rows, idx2, n_out):
    r, w = rows.shape
    assert idx2.shape == (1, 2 * r) and r % (2 * SC_WINDOW) == 0
    windows = r // SC_WINDOW
    half = windows // 2

    @functools.partial(pl.kernel, out_type=jax.ShapeDtypeStruct((n_out, w), rows.dtype), mesh=_sc_mesh(),
                       scratch_types=[])
    def k(x_hbm, ia_hbm, ib_hbm, o_hbm):
        def body(x_vmem, ia_vmem, ib_vmem):
            pltpu.sync_copy(x_vmem, o_hbm.at[ia_vmem.at[0]])
            pltpu.sync_copy(x_vmem, o_hbm.at[ib_vmem.at[0]])

        pltpu.emit_pipeline(
            body,
            grid=(2, half),
            in_specs=[pl.BlockSpec((SC_WINDOW, w), lambda c, i: (c * half + i, 0),
                                   pipeline_mode=pl.Buffered(1)),
                      pl.BlockSpec((1, SC_WINDOW), lambda c, i: (0, c * half + i)),
                      pl.BlockSpec((1, SC_WINDOW), lambda c, i: (0, windows + c * half + i))],
            out_specs=[],
            core_axis_name=("core", "subcore"),
            dimension_semantics=(pltpu.PARALLEL, pltpu.PARALLEL),
        )(x_hbm, ia_hbm, ib_hbm)

    return k(rows, idx2, idx2)


def _expert_kernel(te_ref, nv_ref, x_ref, wg_ref, wu_ref, wd_ref, o_ref, wgb_ref, wub_ref, wdb_ref):
    t = pl.program_id(0)
    nvalid = nv_ref[t]

    @pl.when((t == 0) | (te_ref[t] != te_ref[jnp.maximum(t - 1, 0)]))
    def _():
        wgb_ref[...] = wg_ref[0].astype(jnp.bfloat16)
        wub_ref[...] = wu_ref[0].astype(jnp.bfloat16)
        wdb_ref[...] = wd_ref[0].astype(jnp.bfloat16)

    @pl.when(nvalid > 0)
    def _():
        rows = lax.broadcasted_iota(jnp.int32, x_ref.shape, 0)
        lo, hi = _unpack_pairs(jnp.where(rows < nvalid, x_ref[...], 0))
        xb = jnp.concatenate([lo, hi], axis=1).astype(jnp.bfloat16)
        g = jnp.dot(xb, wgb_ref[...], preferred_element_type=jnp.float32)
        u = jnp.dot(xb, wub_ref[...], preferred_element_type=jnp.float32)
        a = (g * jax.nn.sigmoid(g) * u).astype(jnp.bfloat16)
        o_ref[...] = _pack_pairs(jnp.dot(a, wdb_ref[...], preferred_element_type=jnp.float32))

    @pl.when(nvalid == 0)
    def _():
        o_ref[...] = jnp.zeros_like(o_ref)


def _experts(buf, tile_expert, tile_nvalid, wg, wu, wd, *, layer):
    rows, wp = buf.shape
    _, _, d, f = wg.shape
    n_tiles = rows // EXPERT_TILE
    grid_spec = pltpu.PrefetchScalarGridSpec(
        num_scalar_prefetch=2,
        grid=(n_tiles,),
        in_specs=[pl.BlockSpec((EXPERT_TILE, wp), lambda t, te, nv: (t, 0)),
                  pl.BlockSpec((None, 1, d, f), lambda t, te, nv: (layer, te[t], 0, 0)),
                  pl.BlockSpec((None, 1, d, f), lambda t, te, nv: (layer, te[t], 0, 0)),
                  pl.BlockSpec((None, 1, f, d), lambda t, te, nv: (layer, te[t], 0, 0))],
        out_specs=pl.BlockSpec((EXPERT_TILE, wp), lambda t, te, nv: (t, 0)),
        scratch_shapes=[pltpu.VMEM((d, f), jnp.bfloat16), pltpu.VMEM((d, f), jnp.bfloat16),
                        pltpu.VMEM((f, d), jnp.bfloat16)],
    )
    return pl.pallas_call(
        _expert_kernel,
        grid_spec=grid_spec,
        out_shape=jax.ShapeDtypeStruct((rows, wp), jnp.int32),
        compiler_params=_cparams("arbitrary"),
        name="expert_ffn",
    )(tile_expert, tile_nvalid, buf, wg, wu, wd)


def _final_kernel(y0_ref, y1_ref, rw_ref, x_ref, mod_ref, fg_ref, o_ref):
    x = x_ref[...] + mod_ref[0, 5:6, :] * _moe_mix(y0_ref, y1_ref, rw_ref, slice(None))
    r = lax.rsqrt(jnp.mean(x * x, axis=-1, keepdims=True) + EPS)
    o_ref[...] = x * r * fg_ref[...]


def _final_kernel_into(y0_ref, y1_ref, rw_ref, x_ref, mod_ref, fg_ref, prev_ref, o_ref):
    del prev_ref
    _final_kernel(y0_ref, y1_ref, rw_ref, x_ref, mod_ref, fg_ref, o_ref)


def _final_combine(y, rw, xall, mods, final_g, *, tm, n_tiles, s, out_rows, out_tile0, prev_out=None):
    d = xall.shape[1]
    row = lambda w: pl.BlockSpec((tm, w), lambda i: (i, 0))
    in_specs = [row(d // 2),
                pl.BlockSpec((tm, d // 2), lambda i: (i + n_tiles, 0)),
                row(LANES), row(d),
                pl.BlockSpec((1, N_MOD, d), lambda i: ((i * tm) // s, 0, 0)),
                pl.BlockSpec(final_g.shape, lambda i: (0, 0))]
    args = [y, y, rw, xall, mods, final_g]
    aliases = {}
    body = _final_kernel
    if prev_out is not None:
        in_specs.append(pl.BlockSpec(memory_space=pl.ANY))
        args.append(prev_out)
        aliases = {len(args) - 1: 0}
        body = _final_kernel_into
    return pl.pallas_call(
        body,
        grid=(n_tiles,),
        in_specs=in_specs,
        out_specs=pl.BlockSpec((tm, d), lambda i: (out_tile0 + i, 0)),
        out_shape=jax.ShapeDtypeStruct((out_rows, d), jnp.float32),
        input_output_aliases=aliases,
        compiler_params=_cparams("arbitrary"),
        name="moe_combine_final",
    )(*args)


def _dest_kernel(ps_ref, meta_ref, o_ref):
    slot = pl.program_id(0)
    e = meta_ref[pl.ds(slot, 1), :]
    d = meta_ref[pl.ds(TOP_K + slot, 1), :]
    for k in range(N_EXPERTS):
        d = d + jnp.where(e == float(k), ps_ref[k], 0.0)
    o_ref[...] = d.astype(jnp.int32)


def _dest_rows(meta, pstarts, tcols):
    n = meta.shape[1]
    nt = n // tcols
    return pl.pallas_call(
        _dest_kernel,
        grid=(TOP_K, nt),
        in_specs=[pl.BlockSpec(memory_space=pltpu.SMEM),
                  pl.BlockSpec((META_ROWS, tcols), lambda k, i: (0, i))],
        out_specs=pl.BlockSpec((1, tcols), lambda k, i: (0, k * nt + i)),
        out_shape=jax.ShapeDtypeStruct((1, TOP_K * n), jnp.int32),
        compiler_params=_cparams("arbitrary", "arbitrary"),
        name="moe_dest",
    )(pstarts.astype(jnp.float32), meta)


def _dispatch_plan(meta, counts, n_rows_buf, tcols):
    cnt = counts[0, :N_EXPERTS].astype(jnp.int32)
    padded = (cnt + EXPERT_TILE - 1) // EXPERT_TILE * EXPERT_TILE
    pends = jnp.cumsum(padded)
    pstarts = pends - padded
    dest = _dest_rows(meta, pstarts, tcols)
    tile_start = jnp.arange(n_rows_buf // EXPERT_TILE, dtype=jnp.int32) * EXPERT_TILE
    te = jnp.sum((tile_start[:, None] >= pends[None, :]).astype(jnp.int32), axis=1)
    te = jnp.minimum(te, N_EXPERTS - 1)
    onehot = te[:, None] == jnp.arange(N_EXPERTS, dtype=jnp.int32)[None, :]
    cnt_te = jnp.sum(jnp.where(onehot, cnt[None, :], 0), axis=1)
    pstart_te = jnp.sum(jnp.where(onehot, pstarts[None, :], 0), axis=1)
    nvalid = jnp.clip(cnt_te - (tile_start - pstart_te), 0, EXPERT_TILE).astype(jnp.int32)
    return dest, te, nvalid


def _permute_heads(w, axis):
    heads = [lax.slice_in_dim(w, HEAD_DIM * h, HEAD_DIM * (h + 1), axis=axis)
             for h in HEAD_PERM[::HEAD_DIM] // HEAD_DIM]
    return jnp.concatenate(heads, axis=axis)


def _rope_tables(s, tm):
    pos = np.arange(s)
    pos_row = jnp.asarray(pos // GRID_W, jnp.float32)
    pos_col = jnp.asarray(pos % GRID_W, jnp.float32)
    n_freq = HEAD_DIM // 4
    inv = ROPE_THETA ** (-jnp.arange(n_freq, dtype=jnp.float32) / n_freq)
    ang_row = pos_row[:, None] * inv
    ang_col = pos_col[:, None] * inv
    ang = jnp.concatenate([ang_row, ang_row, ang_col, ang_col] * (LANES // HEAD_DIM), axis=-1)
    sign = np.where((np.arange(LANES) % 32) < 16, -1.0, 1.0).astype(np.float32)
    cos_t = jnp.concatenate([jnp.cos(ang), jnp.ones((tm, LANES), jnp.float32)], axis=0)
    sin_t = jnp.concatenate([jnp.sin(ang) * sign, jnp.zeros((tm, LANES), jnp.float32)], axis=0)
    return cos_t, sin_t


def kernel(x, c, ctx, c_ctx, norm1_g, norm2_g, w_mod, b_mod, w_in, q_norm_g, k_norm_g, conv_w, conv_b, conv_ln_g, conv_ln_b, sink, w_out, w_group, b_group, w_expert, b_expert, w_gate, w_up, w_down, final_g):
    nb, s, d = x.shape
    l = ctx.shape[1]
    depth = w_in.shape[0]
    assert w_in.shape[2] == D_IN and w_out.shape[1] == D_MIX
    assert s % GRID_W == 0 and s >= 3 * WINDOW and s % WINDOW == 0 and l % WINDOW == 0
    groups = BATCH_GROUPS if nb % BATCH_GROUPS == 0 else 1
    nbg = nb // groups
    n_lat, n_ctx = nbg * s, nbg * l
    tm = _pick(np.gcd(s, n_ctx), (512, 256, 128))
    tq = _pick(s, (512, 256, 128))
    assert n_lat % l == 0
    bf = jnp.bfloat16
    f32 = jnp.float32

    x2d = x.reshape(nb * s, d)
    ctx2d = ctx.reshape(nb * l, d)
    c_all = jnp.concatenate([c, c_ctx[None, :]], axis=0)
    mods_all = _modulation(c_all, w_mod, b_mod).reshape(depth, nb + 1, N_MOD, d)
    cos_t, sin_t = _rope_tables(s, tm)
    head_id = np.arange(LANES) // HEAD_DIM
    gsum = jnp.asarray((head_id[:, None] == head_id[None, :]) / HEAD_DIM, bf)
    tri = jnp.asarray(np.tril(np.ones((tm, tm), np.float32), -1), bf)
    n_lat_tiles = n_lat // tm

    xs = [(x2d, ctx2d)] * groups
    pending = [None] * groups
    for i in range(depth):
        last = i == depth - 1
        with_ctx = not last
        qg = jnp.tile(q_norm_g[i], LANES // HEAD_DIM)[None, :]
        kg = jnp.tile(k_norm_g[i], LANES // HEAD_DIM)[None, :]
        w_in_bf = jnp.concatenate(
            [_permute_heads(w_in[i][:, OFF_AQ:OFF_AK], 1), w_in[i][:, OFF_AK:OFF_CQ],
             _permute_heads(w_in[i][:, OFF_CQ:OFF_CK], 1), w_in[i][:, OFF_CK:]], axis=1).astype(bf)
        w_out_bf = jnp.concatenate(
            [_permute_heads(w_out[i][0:A_Q], 0), w_out[i][A_Q:A_Q + B_CH],
             _permute_heads(w_out[i][A_Q + B_CH:], 0)], axis=0).astype(bf)
        sink2 = sink[i] * LOG2E
        conv_args = (conv_w[i].reshape(CONV_W, B_CH), conv_b[i][None, :], conv_ln_g[i][None, :],
                     conv_ln_b[i][None, :])
        wr32 = jnp.zeros((d, LANES), f32).at[:, :N_GROUPS].set(w_group[i])
        wr32 = wr32.at[:, N_GROUPS:N_GROUPS + N_EXPERTS].set(w_expert[i])
        wr_hi = wr32.astype(bf)
        wr = jnp.concatenate([wr_hi, (wr32 - wr_hi.astype(f32)).astype(bf)], axis=1)
        br = jnp.zeros((1, LANES), f32).at[0, :N_GROUPS].set(b_group[i])
        br = br.at[0, N_GROUPS:N_GROUPS + N_EXPERTS].set(b_expert[i])
        n_tok = n_lat + n_ctx if with_ctx else n_lat
        n_tiles = n_tok // tm
        n_rows_buf = 2 * n_tok + N_EXPERTS * EXPERT_TILE
        for g in range(groups):
            b0 = g * nbg
            mods = jnp.concatenate([mods_all[i, b0:b0 + nbg], mods_all[i, nb:nb + 1]], axis=0)
            x_offsets = (b0 * s // tm, b0 * l // tm)
            x_new, qa, kta, va, hgl, qc, ktc, vc = _inproj(
                xs[g], mods, norm1_g[i][None, :], w_in_bf, gsum, qg, kg, cos_t, sin_t, n=n_lat + n_ctx,
                tm=tm, n_lat_tiles=n_lat_tiles, s=s, nb=nbg, x_offsets=x_offsets, combine=pending[g])
            if pending[g] is not None:
                xs[g] = (x_new,)
            oa, ob, oc = _latent_mixers(qa, kta, va, qc, ktc, vc, sink2, hgl, conv_args,
                                        nb=nbg, s=s, l=l, tq=tq)
            if with_ctx:
                oa, ob, oc = _ctx_mixers(qa, kta, va, qc, ktc, vc, sink2, hgl, conv_args, oa, ob, oc,
                                         nb=nbg, s=s, l=l)
            xall, hp, meta, rw, counts = _outproj(xs[g], oa, ob, oc, mods, norm2_g[i][None, :], w_out_bf,
                                                  wr, br, tri, n=n_lat + n_ctx, tm=tm, n_tiles=n_tiles,
                                                  n_lat_tiles=n_lat_tiles, s=s, nb=nbg, x_offsets=x_offsets)
            dest, te, nvalid = _dispatch_plan(meta, counts, n_rows_buf,
                                              _pick(n_tok, (8192, 4096, 2048, 1024, 512, 256, 128)))
            buf = sc_scatter_rows2(hp, dest, n_rows_buf)
            eo = _experts(buf, te, nvalid, w_gate, w_up, w_down, layer=i)
            y = sc_gather_rows(eo, dest)
            xs[g] = (xall,)
            pending[g] = (y, rw, mods)
    out = None
    for g in range(groups):
        y, rw, mods = pending[g]
        out = _final_combine(y, rw, xs[g][0], mods, final_g[None, :], tm=tm, n_tiles=n_lat_tiles, s=s,
                             out_rows=nb * s, out_tile0=g * n_lat_tiles, prev_out=out)
    return out.reshape(nb, s, d)
```

```python
import functools

import jax
import jax.numpy as jnp
import numpy as np
from jax import lax
from jax.experimental import pallas as pl
from jax.experimental.pallas import tpu as pltpu
from jax.experimental.pallas import tpu_sc as plsc

HEAD_DIM = 64
GRID_W = 64
ROPE_THETA = 10000.0
A_HEADS, A_KV_HEADS = 6, 2
C_HEADS, C_KV_HEADS = 6, 2
B_CH = 256
CONV_W = 31
WINDOW = 128
N_GROUPS = 4
EXPERTS_PER_GROUP = 8
N_EXPERTS = N_GROUPS * EXPERTS_PER_GROUP
TOP_K = 2
N_MOD = 6
EPS = 1e-6
ATTN_SCALE = HEAD_DIM ** -0.5
LOG2E = 1.4426950408889634
Q_SCALE = ATTN_SCALE * LOG2E

A_Q = A_HEADS * HEAD_DIM
A_KV = A_KV_HEADS * HEAD_DIM
C_Q = C_HEADS * HEAD_DIM
C_KV = C_KV_HEADS * HEAD_DIM
D_MIX = A_Q + B_CH + C_Q
OFF_AQ = 0
OFF_AK = OFF_AQ + A_Q
OFF_AV = OFF_AK + A_KV
OFF_BU = OFF_AV + A_KV
OFF_CQ = OFF_BU + 2 * B_CH
OFF_CK = OFF_CQ + C_Q
OFF_CV = OFF_CK + C_KV
D_IN = OFF_CV + C_KV

LANES = 128
SUBLANES = 8
KT_ROWS = LANES
V_W = 2 * LANES
HEAD_PERM = np.concatenate([np.arange(HEAD_DIM) + HEAD_DIM * h for b in range(3) for h in (b, b + 3)])
EXPERT_TILE = 1024
SC_WINDOW = 128
VMEM_LIMIT = 56 * 1024 * 1024
HI_MASK = -65536
BATCH_GROUPS = 1
ROW_CHAIN = 256


def _cparams(*sem):
    return pltpu.CompilerParams(dimension_semantics=sem, vmem_limit_bytes=VMEM_LIMIT)


def _pick(n, cands):
    for c in cands:
        if n % c == 0:
            return c
    raise ValueError(f"no tile in {cands} divides {n}")


def _pack_pairs(x):
    w = x.shape[1] // 2
    lo = lax.bitcast_convert_type(x[:, :w].astype(jnp.bfloat16).astype(jnp.float32), jnp.int32)
    hi = lax.bitcast_convert_type(x[:, w:].astype(jnp.bfloat16).astype(jnp.float32), jnp.int32)
    return (hi & HI_MASK) | lax.shift_right_logical(lo, 16)


def _unpack_pairs(p):
    lo = lax.bitcast_convert_type(lax.shift_left(p, 16), jnp.float32)
    hi = lax.bitcast_convert_type(p & HI_MASK, jnp.float32)
    return lo, hi


def _mod_kernel(c_ref, w_ref, b_ref, o_ref):
    c = c_ref[...]
    a = c * jax.nn.sigmoid(c)
    o_ref[0] = jnp.dot(a, w_ref[0], preferred_element_type=jnp.float32,
                       precision=lax.Precision.HIGHEST) + b_ref[0]


def _modulation(c_all, w_mod, b_mod):
    depth, d, n = w_mod.shape
    r = c_all.shape[0]
    tn = _pick(n, (1024, 512, 256, 128))
    return pl.pallas_call(
        _mod_kernel,
        grid=(depth, n // tn),
        in_specs=[pl.BlockSpec((r, d), lambda l, j: (0, 0)),
                  pl.BlockSpec((1, d, tn), lambda l, j: (l, 0, j)),
                  pl.BlockSpec((1, 1, tn), lambda l, j: (l, 0, j))],
        out_specs=pl.BlockSpec((1, r, tn), lambda l, j: (l, 0, j)),
        out_shape=jax.ShapeDtypeStruct((depth, r, n), jnp.float32),
        compiler_params=_cparams("arbitrary", "arbitrary"),
        name="modulation",
    )(c_all, w_mod, b_mod.reshape(depth, 1, n))


def _head_mean_sq(blk, gsum):
    sq = blk * blk
    hi = sq.astype(jnp.bfloat16)
    lo = (sq - hi.astype(jnp.float32)).astype(jnp.bfloat16)
    return (jnp.dot(hi, gsum, preferred_element_type=jnp.float32)
            + jnp.dot(lo, gsum, preferred_element_type=jnp.float32))


def _x_specs(xs, tm, n_lat_tiles, offsets=(0, 0), tile_of=lambda i: i):
    d = xs[0].shape[1]
    if len(xs) == 1:
        return [pl.BlockSpec((tm, d), lambda i: (tile_of(i), 0))]
    lat0, ctx0 = offsets
    return [pl.BlockSpec((tm, d), lambda i: (lat0 + jnp.minimum(tile_of(i), n_lat_tiles - 1), 0)),
            pl.BlockSpec((tm, d), lambda i: (ctx0 + jnp.maximum(tile_of(i) - n_lat_tiles, 0), 0))]


def _load_x(x_refs, n_lat_tiles, rows=slice(None), tile=None):
    if len(x_refs) == 1:
        return x_refs[0][rows, :]
    tile = pl.program_id(0) if tile is None else tile
    return jnp.where(tile < n_lat_tiles, x_refs[0][rows, :], x_refs[1][rows, :])


def _moe_mix(y0_ref, y1_ref, rw_ref, rows):
    rw = rw_ref[rows, :]
    w0 = rw[:, 0:1]
    w1 = rw[:, 1:2]
    a_lo, a_hi = _unpack_pairs(y0_ref[rows, :])
    b_lo, b_hi = _unpack_pairs(y1_ref[rows, :])
    return jnp.concatenate([a_lo * w0 + b_lo * w1, a_hi * w0 + b_hi * w1], axis=1)


def _inproj_kernel(*refs, n_x, n_lat_tiles, fused_combine):
    x_refs = refs[:n_x]
    refs = refs[n_x:]
    if fused_combine:
        y0_ref, y1_ref, rw_ref, modp_ref = refs[:4]
        refs = refs[4:]
        xo_ref = refs[-1]
        refs = refs[:-1]
    (mod_ref, g_ref, w_ref, gsum_ref, qg_ref, kg_ref, cos_ref, sin_ref,
     qa_ref, kta_ref, va_ref, hgl_ref, qc_ref, ktc_ref, vc_ref) = refs
    tm = qa_ref.shape[0]
    tr = min(tm, ROW_CHAIN)
    bf = jnp.bfloat16
    lane = lax.broadcasted_iota(jnp.int32, (tr, LANES), 1)
    first16 = (lane % 32) < 16
    gsum = gsum_ref[...]
    qg = qg_ref[...]
    kg = kg_ref[...]
    scale = g_ref[...] * (1.0 + mod_ref[0, 1:2, :])
    shift = mod_ref[0, 0:1, :]
    ones = jnp.ones((tr, LANES), bf)

    for r0 in range(0, tm, tr):
        rows = slice(r0, r0 + tr)
        x = _load_x(x_refs, n_lat_tiles, rows)
        if fused_combine:
            x = x + modp_ref[0, 5:6, :] * _moe_mix(y0_ref, y1_ref, rw_ref, rows)
            xo_ref[rows, :] = x
        r = lax.rsqrt(jnp.mean(x * x, axis=-1, keepdims=True) + EPS)
        h = (x * r) * scale + shift
        p = jnp.dot(h.astype(bf), w_ref[...], preferred_element_type=jnp.float32)
        cos = cos_ref[rows, :]
        sin = sin_ref[rows, :]

        def blk(off):
            return p[:, off:off + LANES]

        def rope(t):
            sw = jnp.where(first16, pltpu.roll(t, LANES - 16, axis=1), pltpu.roll(t, 16, axis=1))
            return t * cos + sw * sin

        def norm(t, g):
            return t * lax.rsqrt(_head_mean_sq(t, gsum) + EPS) * g

        def store_kv(kt_ref, v_ref, k_blk, v_blk):
            kt_ref[:, rows] = k_blk.T.astype(bf)
            v_ref[rows, 0:LANES] = v_blk.astype(bf)
            v_ref[rows, LANES:2 * LANES] = ones

        for i in range(A_Q // LANES):
            t = rope(norm(blk(OFF_AQ + i * LANES), qg)) * Q_SCALE
            qa_ref[rows, i * LANES:(i + 1) * LANES] = t.astype(bf)
        store_kv(kta_ref, va_ref, rope(norm(blk(OFF_AK), kg)), blk(OFF_AV))
        for i in range(B_CH // LANES):
            a = blk(OFF_BU + i * LANES)
            gt = blk(OFF_BU + B_CH + i * LANES)
            hgl_ref[rows, i * LANES:(i + 1) * LANES] = a * jax.nn.sigmoid(gt)
        for i in range(C_Q // LANES):
            t = rope(blk(OFF_CQ + i * LANES)) * Q_SCALE
            qc_ref[rows, i * LANES:(i + 1) * LANES] = t.astype(bf)
        store_kv(ktc_ref, vc_ref, rope(blk(OFF_CK)), blk(OFF_CV))


def _inproj(xs, mods, g1, w_in_bf, gsum, qg, kg, cos_t, sin_t, *, n, tm, n_lat_tiles, s, nb, x_offsets=(0, 0),
            combine=None):
    d = xs[0].shape[1]
    s_tiles = s // tm
    n_tiles = n // tm

    def bidx(i):
        return jnp.where(i < n_lat_tiles, (i * tm) // s, nb)

    def ridx(i):
        return jnp.where(i < n_lat_tiles, i % s_tiles, s_tiles)

    row = lambda w: pl.BlockSpec((tm, w), lambda i: (i, 0))
    ktspec = pl.BlockSpec((KT_ROWS, tm), lambda i: (0, i))
    const = lambda a: pl.BlockSpec(a.shape, lambda i: (0,) * a.ndim)
    bf = jnp.bfloat16
    modspec = pl.BlockSpec((1, N_MOD, d), lambda i: (bidx(i), 0, 0))
    in_specs = _x_specs(xs, tm, n_lat_tiles, x_offsets)
    args = list(xs)
    out_specs = [row(A_Q), ktspec, row(V_W), row(B_CH), row(C_Q), ktspec, row(V_W)]
    out_shape = [jax.ShapeDtypeStruct((n, A_Q), bf), jax.ShapeDtypeStruct((KT_ROWS, n), bf),
                 jax.ShapeDtypeStruct((n, V_W), bf), jax.ShapeDtypeStruct((n, B_CH), jnp.float32),
                 jax.ShapeDtypeStruct((n, C_Q), bf), jax.ShapeDtypeStruct((KT_ROWS, n), bf),
                 jax.ShapeDtypeStruct((n, V_W), bf)]
    aliases = {}
    if combine is not None:
        y, rw, mods_prev = combine
        assert len(xs) == 1 and y.shape[0] == 2 * n
        in_specs += [row(d // 2), pl.BlockSpec((tm, d // 2), lambda i: (i + n_tiles, 0)), row(LANES), modspec]
        args += [y, y, rw, mods_prev]
        out_specs.append(row(d))
        out_shape.append(jax.ShapeDtypeStruct((n, d), jnp.float32))
        aliases = {0: len(out_shape) - 1}
    in_specs += [modspec, const(g1), const(w_in_bf), const(gsum), const(qg), const(kg),
                 pl.BlockSpec((tm, LANES), lambda i: (ridx(i), 0)),
                 pl.BlockSpec((tm, LANES), lambda i: (ridx(i), 0))]
    args += [mods, g1, w_in_bf, gsum, qg, kg, cos_t, sin_t]
    outs = pl.pallas_call(
        functools.partial(_inproj_kernel, n_x=len(xs), n_lat_tiles=n_lat_tiles,
                          fused_combine=combine is not None),
        grid=(n_tiles,),
        in_specs=in_specs,
        out_specs=out_specs,
        out_shape=out_shape,
        input_output_aliases=aliases,
        compiler_params=_cparams("arbitrary"),
        name="inproj",
    )(*args)
    if combine is not None:
        return (outs[-1],) + tuple(outs[:-1])
    return (None,) + tuple(outs)


def _stack_heads(q_ref, tq):
    lane = lax.broadcasted_iota(jnp.int32, (tq, LANES), 1)
    lo = lane < HEAD_DIM
    qb = [q_ref[:, i * LANES:(i + 1) * LANES] for i in range(3)]
    zero = jnp.zeros_like(qb[0])
    s0 = jnp.concatenate([jnp.where(lo, t, zero) for t in qb], axis=0)
    s1 = jnp.concatenate([jnp.where(lo, zero, t) for t in qb], axis=0)
    return s0, s1, lo


def _unstack_store(o_ref, o0, o1, lo, tq):
    for i in range(3):
        rows = slice(i * tq, (i + 1) * tq)
        o_ref[:, i * LANES:(i + 1) * LANES] = jnp.where(lo, o0[rows], o1[rows]).astype(jnp.bfloat16)


def _krow(kv):
    del kv
    return slice(0, KT_ROWS)


def _vcol(kv):
    del kv
    return slice(0, V_W)


def _row_max(scores):
    m = None
    for t in scores:
        for c in range(0, t.shape[1], LANES):
            blk = t[:, c:c + LANES]
            m = blk if m is None else jnp.maximum(m, blk)
    return m.max(axis=-1, keepdims=True)


def _softmax_pv(scores, values, extra=None):
    m = _row_max(scores)
    if extra is not None:
        m = jnp.maximum(m, extra)
    acc = None
    for t, v in zip(scores, values):
        c = jnp.dot(jnp.exp2(t - m).astype(jnp.bfloat16), v, preferred_element_type=jnp.float32)
        acc = c if acc is None else acc + c
    den = acc[:, LANES:2 * LANES]
    if extra is not None:
        den = den + jnp.exp2(extra - m)
    return acc[:, 0:LANES] / den


ATTN_CHAIN_ROWS = 128


def _qk(q, kt):
    return jnp.dot(q, kt, preferred_element_type=jnp.float32)


def _attn_a_kernel(q_ref, ktl_ref, ktc_ref, vl_ref, vc_ref, *rest, tq, n_q):
    conv_refs = rest[:7]
    o_ref, ob_ref, sh_ref = rest[7:]
    j = pl.program_id(1)
    conv_steps = _conv_steps(conv_refs, ob_ref, sh_ref, j > 0, j < n_q - 1, tq)
    s0, s1, lo = _stack_heads(q_ref, tq)
    n_chains = 2 * (3 * tq // ATTN_CHAIN_ROWS)
    chain = 0
    conv_done = 0
    outs = []
    for kv, qs in enumerate((s0, s1)):
        krow = _krow(kv)
        vcol = _vcol(kv)
        parts = []
        for r0 in range(0, 3 * tq, ATTN_CHAIN_ROWS):
            qr = qs[r0:r0 + ATTN_CHAIN_ROWS]
            ss = [_qk(qr, ktl_ref[krow, :]), _qk(qr, ktc_ref[krow, :])]
            parts.append(_softmax_pv(ss, [vl_ref[:, vcol], vc_ref[:, vcol]]))
            chain += 1
            while conv_done < chain * len(conv_steps) // n_chains:
                conv_steps[conv_done]()
                conv_done += 1
        outs.append(jnp.concatenate(parts, axis=0))
    _unstack_store(o_ref, outs[0], outs[1], lo, tq)


def _kv_specs(nb, s, l):
    ctx0 = nb * s // l
    return [pl.BlockSpec((KT_ROWS, s), lambda b, j: (0, b)),
            pl.BlockSpec((KT_ROWS, l), lambda b, j: (0, ctx0 + b)),
            pl.BlockSpec((s, V_W), lambda b, j: (b, 0)),
            pl.BlockSpec((l, V_W), lambda b, j: (ctx0 + b, 0))]


def _sink_column(sink_ref, kv, rows):
    return jnp.concatenate([jnp.full((rows, 1), sink_ref[3 * kv + g], jnp.float32) for g in range(3)], axis=0)


def _attn_c_kernel(sink_ref, bias_ref, q_ref, ktl_ref, ktc_ref, vl_ref, vc_ref, o_ref, *, s, blocks):
    j = pl.program_id(1)
    tq = WINDOW
    band = 3 * WINDOW
    bf = jnp.bfloat16
    f32 = jnp.float32
    m3 = 3 * tq
    stacks = [_stack_heads(q_ref.at[blk * tq:(blk + 1) * tq, :], tq) for blk in range(blocks)]
    lo = stacks[0][2]
    starts = [pl.multiple_of(jnp.clip((j * blocks + blk - 1) * WINDOW, 0, s - band), WINDOW)
              for blk in range(blocks)]
    outs = [[None, None] for _ in range(blocks)]
    for kv in range(C_KV_HEADS):
        krow = _krow(kv)
        vcol = _vcol(kv)
        sk = _sink_column(sink_ref, kv, tq)
        sc_all = _qk(jnp.concatenate([st[kv] for st in stacks], axis=0), ktc_ref[krow, :])
        accs, pcs, ms = [], [], []
        for blk in range(blocks):
            start = starts[blk]
            sl = _qk(stacks[blk][kv], ktl_ref[krow, pl.ds(start, band)]) + bias_ref[j * blocks + blk - start // WINDOW]
            sc = sc_all[blk * m3:(blk + 1) * m3]
            m = jnp.maximum(_row_max([sl, sc]), sk)
            accs.append(jnp.dot(jnp.exp2(sl - m).astype(bf), vl_ref[pl.ds(start, band), vcol],
                                preferred_element_type=f32))
            pcs.append(jnp.exp2(sc - m).astype(bf))
            ms.append(m)
        acc_ctx = jnp.dot(jnp.concatenate(pcs, axis=0), vc_ref[:, vcol], preferred_element_type=f32)
        for blk in range(blocks):
            acc = accs[blk] + acc_ctx[blk * m3:(blk + 1) * m3]
            den = acc[:, LANES:2 * LANES] + jnp.exp2(sk - ms[blk])
            outs[blk][kv] = acc[:, 0:LANES] / den
    for blk in range(blocks):
        _unstack_store(o_ref.at[blk * tq:(blk + 1) * tq, :], outs[blk][0], outs[blk][1], lo, tq)


def _window_bias():
    r = np.arange(3 * WINDOW)[:, None] % WINDOW
    col = np.arange(3 * WINDOW)[None, :]
    tabs = [np.where(np.abs(col - r - WINDOW * off) <= WINDOW, 0.0, -np.inf) for off in range(3)]
    return jnp.asarray(np.stack(tabs), jnp.float32)


def _mixers_kernel(*refs, tq, n_q, s):
    sink_ref, bias_ref, qc_ref = refs[0:3]
    kvc_refs = refs[3:7]
    qa_ref = refs[7]
    kva_refs = refs[8:12]
    conv_refs = refs[12:19]
    oa_ref, ob_ref, oc_ref, sh_ref = refs[19:]
    _attn_c_kernel(sink_ref, bias_ref, qc_ref, *kvc_refs, oc_ref, s=s, blocks=tq // WINDOW)
    _attn_a_kernel(qa_ref, *kva_refs, *conv_refs, oa_ref, ob_ref, sh_ref, tq=tq, n_q=n_q)


def _latent_mixers(qa, kta, va, qc, ktc, vc, sink2, hgl, conv_params, *, nb, s, l, tq):
    n = qa.shape[0]
    n_q = s // tq
    bias = _window_bias()
    conv_specs, conv_blk = _conv_specs(n, conv_params, chunks=n_q, base=0, tc=tq)
    qspec = lambda w: pl.BlockSpec((tq, w), lambda b, j: (b * n_q + j, 0))
    bf = jnp.bfloat16
    return pl.pallas_call(
        functools.partial(_mixers_kernel, tq=tq, n_q=n_q, s=s),
        grid=(nb, n_q),
        in_specs=[pl.BlockSpec(memory_space=pltpu.SMEM), pl.BlockSpec(bias.shape, lambda b, j: (0, 0, 0)),
                  qspec(C_Q)] + _kv_specs(nb, s, l) + [qspec(A_Q)] + _kv_specs(nb, s, l) + conv_specs,
        out_specs=[qspec(A_Q), conv_blk, qspec(C_Q)],
        out_shape=[jax.ShapeDtypeStruct((n, A_Q), bf), jax.ShapeDtypeStruct((n, B_CH), bf),
                   jax.ShapeDtypeStruct((n, C_Q), bf)],
        scratch_shapes=[pltpu.VMEM((SUBLANES, tq + 2 * CONV_HALO, B_CH), jnp.float32)],
        compiler_params=_cparams("arbitrary", "arbitrary"),
        name="latent_mixers",
    )(sink2, bias, qc, ktc, ktc, vc, vc, qa, kta, kta, va, va, hgl, hgl, hgl, *conv_params)


def _ctx_mixers_kernel(sink_ref, qa_ref, kta_ref, va_ref, qc_ref, ktc_ref, vc_ref, *rest, l):
    conv_refs = rest[:7]
    oa_ref, ob_ref, oc_ref, sh_ref = rest[-4:]
    conv_steps = _conv_steps(conv_refs, ob_ref, sh_ref, False, False, l)
    for q_ref, kt_ref, v_ref, o_ref, with_sink in ((qa_ref, kta_ref, va_ref, oa_ref, False),
                                                   (qc_ref, ktc_ref, vc_ref, oc_ref, True)):
        s0, s1, lo = _stack_heads(q_ref, l)
        outs = []
        for kv, qs in enumerate((s0, s1)):
            sk = _sink_column(sink_ref, kv, l) if with_sink else None
            outs.append(_softmax_pv([_qk(qs, kt_ref[_krow(kv), :])], [v_ref[:, _vcol(kv)]], sk))
            for step in conv_steps[(2 * with_sink + kv) * len(conv_steps) // 4:
                                   (2 * with_sink + kv + 1) * len(conv_steps) // 4]:
                step()
        _unstack_store(o_ref, outs[0], outs[1], lo, l)


def _ctx_mixers(qa, kta, va, qc, ktc, vc, sink2, hgl, conv_params, oa, ob, oc, *, nb, s, l):
    n = qa.shape[0]
    base = nb * s // l
    row = lambda w: pl.BlockSpec((l, w), lambda b, j: (base + b, 0))
    ktspec = pl.BlockSpec((KT_ROWS, l), lambda b, j: (0, base + b))
    conv_specs, conv_blk = _conv_specs(n, conv_params, chunks=1, base=base, tc=l)
    anyspec = pl.BlockSpec(memory_space=pl.ANY)
    n_in = 7 + len(conv_specs)
    return pl.pallas_call(
        functools.partial(_ctx_mixers_kernel, l=l),
        grid=(nb, 1),
        in_specs=[pl.BlockSpec(memory_space=pltpu.SMEM), row(A_Q), ktspec, row(V_W), row(C_Q), ktspec, row(V_W)]
        + conv_specs + [anyspec, anyspec, anyspec],
        out_specs=[row(A_Q), conv_blk, row(C_Q)],
        out_shape=[jax.ShapeDtypeStruct(a.shape, a.dtype) for a in (oa, ob, oc)],
        scratch_shapes=[pltpu.VMEM((SUBLANES, l + 2 * CONV_HALO, B_CH), jnp.float32)],
        input_output_aliases={n_in: 0, n_in + 1: 1, n_in + 2: 2},
        compiler_params=_cparams("arbitrary", "arbitrary"),
        name="context_mixers",
    )(sink2, qa, kta, va, qc, ktc, vc, hgl, hgl, hgl, *conv_params, oa, ob, oc)


CONV_HALO = 16
CONV_ROWS = 64


def _conv_steps(conv_refs, o_ref, sh_ref, has_prev, has_next, tc):
    prev_ref, cur_ref, next_ref, w_ref, b_ref, g_ref, beta_ref = conv_refs
    rows = tc + 2 * CONV_HALO
    sh_ref[0, 0:CONV_HALO, :] = jnp.where(has_prev, prev_ref[...], 0.0)
    sh_ref[0, CONV_HALO:CONV_HALO + tc, :] = cur_ref[...]
    sh_ref[0, CONV_HALO + tc:rows, :] = jnp.where(has_next, next_ref[...], 0.0)
    for b in range(1, SUBLANES):
        sh_ref[b, 0:rows - SUBLANES, :] = sh_ref[0, b:b + rows - SUBLANES, :]
    base = CONV_HALO - CONV_W // 2

    def step(r0):
        acc = None
        for k in range(CONV_W):
            a, b = divmod(base + k, SUBLANES)
            term = sh_ref[b, SUBLANES * a + r0:SUBLANES * a + r0 + CONV_ROWS, :] * w_ref[k:k + 1, :]
            acc = term if acc is None else acc + term
        hc = acc + b_ref[...]
        mu = jnp.mean(hc, axis=-1, keepdims=True)
        xc = hc - mu
        var = jnp.mean(xc * xc, axis=-1, keepdims=True)
        y = xc * lax.rsqrt(var + EPS) * g_ref[...] + beta_ref[...]
        o_ref[r0:r0 + CONV_ROWS, :] = (y * jax.nn.sigmoid(y)).astype(o_ref.dtype)

    return [functools.partial(step, r0) for r0 in range(0, tc, CONV_ROWS)]


def _conv_specs(n, params, *, chunks, base, tc):
    hb = tc // CONV_HALO
    last_halo = n // CONV_HALO - 1
    idx = lambda b_, j: base + b_ * chunks + j
    blk = pl.BlockSpec((tc, B_CH), lambda b_, j: (idx(b_, j), 0))
    prev_halo = pl.BlockSpec((CONV_HALO, B_CH), lambda b_, j: (jnp.maximum(idx(b_, j) * hb - 1, 0), 0))
    next_halo = pl.BlockSpec((CONV_HALO, B_CH), lambda b_, j: (jnp.minimum((idx(b_, j) + 1) * hb, last_halo), 0))
    const = lambda a: pl.BlockSpec(a.shape, lambda b_, j: (0,) * a.ndim)
    return [prev_halo, blk, next_halo] + [const(a) for a in params], blk


META_ROWS = 8


def _outproj_kernel(*refs, n_x, n_lat_tiles, n_tiles):
    x_refs = refs[:n_x]
    (oa_ref, ob_ref, oc_ref, mod_ref, g_ref, w_ref, wr_ref, br_ref, tri_ref,
     xo_ref, hp_ref, meta_ref, rw_ref, cnt_ref, lg_ref) = refs[n_x:]
    i = pl.program_id(0)
    tm = xo_ref.shape[0]
    f32 = jnp.float32

    @pl.when(i == 0)
    def _():
        cnt_ref[...] = jnp.zeros_like(cnt_ref)
        lg_ref[...] = jnp.zeros_like(lg_ref)

    lat = jnp.concatenate([oa_ref[...], ob_ref[...], oc_ref[...]], axis=1)
    mix = jnp.dot(lat, w_ref[...], preferred_element_type=f32)
    x = _load_x(x_refs, n_lat_tiles, tile=jnp.minimum(i, n_tiles - 1)) + mod_ref[0, 2:3, :] * mix
    xo_ref[...] = x
    r = lax.rsqrt(jnp.mean(x * x, axis=-1, keepdims=True) + EPS)
    h = (x * r) * (g_ref[...] * (1.0 + mod_ref[0, 4:5, :])) + mod_ref[0, 3:4, :]
    h_hi = h.astype(jnp.bfloat16)
    hp_ref[...] = _pack_pairs(h_hi)
    h_lo = (h - h_hi.astype(f32)).astype(jnp.bfloat16)
    r_hi = jnp.dot(h_hi, wr_ref[...], preferred_element_type=f32)
    r_lo = jnp.dot(h_lo, wr_ref[:, 0:LANES], preferred_element_type=f32)
    lg = lg_ref[(i + 1) % 2]
    lg_ref[i % 2] = r_hi[:, 0:LANES] + r_hi[:, LANES:2 * LANES] + r_lo + br_ref[...]

    lane = lax.broadcasted_iota(jnp.int32, (tm, LANES), 1).astype(f32)
    big = float(LANES)
    ninf = -jnp.inf
    glog = jnp.where(lane < N_GROUPS, lg, ninf)
    gmax = glog.max(axis=-1, keepdims=True)
    g_val = 1.0 / jnp.exp(glog - gmax).sum(axis=-1, keepdims=True)
    g_idx = jnp.where(glog == gmax, lane, big).min(axis=-1, keepdims=True)
    e_lo = N_GROUPS + EXPERTS_PER_GROUP * g_idx
    el = jnp.where((lane >= e_lo) & (lane < e_lo + EXPERTS_PER_GROUP), lg, ninf)
    v0 = el.max(axis=-1, keepdims=True)
    i0 = jnp.where(el == v0, lane, big).min(axis=-1, keepdims=True)
    el1 = jnp.where(lane == i0, ninf, el)
    v1 = el1.max(axis=-1, keepdims=True)
    i1 = jnp.where(el1 == v1, lane, big).min(axis=-1, keepdims=True)
    t = jnp.exp(v1 - v0)
    w0 = g_val / (1.0 + t)
    w1 = g_val * t / (1.0 + t)
    e0 = i0 - N_GROUPS
    e1 = i1 - N_GROUPS

    cnt = cnt_ref[0:1, :]
    tri = tri_ref[...]
    ranks = []
    for e in (e0, e1):
        oh = lane == e
        ohf = oh.astype(f32)
        pre = jnp.dot(tri, ohf.astype(jnp.bfloat16), preferred_element_type=f32) + cnt
        ranks.append(jnp.where(oh, pre, 0.0).sum(axis=-1, keepdims=True))
        cnt = cnt + ohf.sum(axis=0, keepdims=True)
    cnt_ref[0:1, :] = jnp.where(i > 0, cnt, cnt_ref[0:1, :])
    rw_ref[...] = jnp.where(lane == 0, w0, jnp.where(lane == 1, w1, 0.0))
    rec = jnp.where(lane == 0, e0, jnp.where(lane == 1, e1, jnp.where(lane == 2, ranks[0],
                    jnp.where(lane == 3, ranks[1], jnp.where(lane == 4, w0, jnp.where(lane == 5, w1, 0.0))))))
    meta_ref[...] = rec.T[0:META_ROWS, :]


def _outproj(xs, oa, ob, oc, mods, g2, w_out_bf, wr, br, tri, *, n, tm, n_tiles, n_lat_tiles, s, nb,
             x_offsets=(0, 0)):
    d = xs[0].shape[1]
    rows = n_tiles * tm
    cur = lambda i: jnp.minimum(i, n_tiles - 1)
    prev = lambda i: jnp.maximum(i - 1, 0)

    def bidx(i):
        return jnp.where(cur(i) < n_lat_tiles, (cur(i) * tm) // s, nb)

    row = lambda w: pl.BlockSpec((tm, w), lambda i: (cur(i), 0))
    const = lambda a: pl.BlockSpec(a.shape, lambda i: (0,) * a.ndim)
    return pl.pallas_call(
        functools.partial(_outproj_kernel, n_x=len(xs), n_lat_tiles=n_lat_tiles, n_tiles=n_tiles),
        grid=(n_tiles + 1,),
        in_specs=_x_specs(xs, tm, n_lat_tiles, x_offsets, cur) + [
                  row(A_Q), row(B_CH), row(C_Q),
                  pl.BlockSpec((1, N_MOD, d), lambda i: (bidx(i), 0, 0)),
                  const(g2), const(w_out_bf), const(wr), const(br), const(tri)],
        out_specs=[row(d), row(d // 2),
                   pl.BlockSpec((META_ROWS, tm), lambda i: (0, prev(i))),
                   pl.BlockSpec((tm, LANES), lambda i: (prev(i), 0)),
                   pl.BlockSpec((8, LANES), lambda i: (0, 0))],
        out_shape=[jax.ShapeDtypeStruct((n, d), jnp.float32),
                   jax.ShapeDtypeStruct((rows, d // 2), jnp.int32),
                   jax.ShapeDtypeStruct((META_ROWS, rows), jnp.float32),
                   jax.ShapeDtypeStruct((rows, LANES), jnp.float32),
                   jax.ShapeDtypeStruct((8, LANES), jnp.float32)],
        scratch_shapes=[pltpu.VMEM((2, tm, LANES), jnp.float32)],
        input_output_aliases={0: 0} if len(xs) == 1 else {},
        compiler_params=_cparams("arbitrary"),
        name="outproj_router",
    )(*xs, oa, ob, oc, mods, g2, w_out_bf, wr, br, tri)


def _sc_mesh():
    return plsc.VectorSubcoreMesh(core_axis_name="core", subcore_axis_name="subcore")


def sc_gather_rows(table, idx2):
    r = idx2.shape[1]
    w = table.shape[1]
    assert r % (2 * SC_WINDOW) == 0
    half = r // SC_WINDOW // 2

    @functools.partial(pl.kernel, out_type=jax.ShapeDtypeStruct((r, w), table.dtype), mesh=_sc_mesh())
    def k(x_hbm, i_hbm, o_hbm):
        def body(i_vmem, o_vmem):
            pltpu.sync_copy(x_hbm.at[i_vmem.at[0]], o_vmem)

        pltpu.emit_pipeline(
            body,
            grid=(2, half),
            in_specs=[pl.BlockSpec((1, SC_WINDOW), lambda c, i: (0, c * half + i))],
            out_specs=[pl.BlockSpec((SC_WINDOW, w), lambda c, i: (c * half + i, 0),
                                    pipeline_mode=pl.Buffered(1))],
            core_axis_name=("core", "subcore"),
            dimension_semantics=(pltpu.PARALLEL, pltpu.PARALLEL),
        )(i_hbm, o_hbm)

    return k(table, idx2)


def sc_scatter_rows2(rows, idx2, n_out):
    r, w = rows.shape
    assert idx2.shape == (1, 2 * r) and r % (2 * SC_WINDOW) == 0
    windows = r // SC_WINDOW
    half = windows // 2

    @functools.partial(pl.kernel, out_type=jax.ShapeDtypeStruct((n_out, w), rows.dtype), mesh=_sc_mesh(),
                       scratch_types=[])
    def k(x_hbm, ia_hbm, ib_hbm, o_hbm):
        def body(x_vmem, ia_vmem, ib_vmem):
            pltpu.sync_copy(x_vmem, o_hbm.at[ia_vmem.at[0]])
            pltpu.sync_copy(x_vmem, o_hbm.at[ib_vmem.at[0]])

        pltpu.emit_pipeline(
            body,
            grid=(2, half),
            in_specs=[pl.BlockSpec((SC_WINDOW, w), lambda c, i: (c * half + i, 0),
                                   pipeline_mode=pl.Buffered(1)),
                      pl.BlockSpec((1, SC_WINDOW), lambda c, i: (0, c * half + i)),
                      pl.BlockSpec((1, SC_WINDOW), lambda c, i: (0, windows + c * half + i))],
            out_specs=[],
            core_axis_name=("core", "subcore"),
            dimension_semantics=(pltpu.PARALLEL, pltpu.PARALLEL),
        )(x_hbm, ia_hbm, ib_hbm)

    return k(rows, idx2, idx2)


def _expert_kernel(te_ref, nv_ref, x_ref, wg_ref, wu_ref, wd_ref, o_ref, wgb_ref, wub_ref, wdb_ref):
    t = pl.program_id(0)
    nvalid = nv_ref[t]

    @pl.when((t == 0) | (te_ref[t] != te_ref[jnp.maximum(t - 1, 0)]))
    def _():
        wgb_ref[...] = wg_ref[0].astype(jnp.bfloat16)
        wub_ref[...] = wu_ref[0].astype(jnp.bfloat16)
        wdb_ref[...] = wd_ref[0].astype(jnp.bfloat16)

    @pl.when(nvalid > 0)
    def _():
        rows = lax.broadcasted_iota(jnp.int32, x_ref.shape, 0)
        lo, hi = _unpack_pairs(jnp.where(rows < nvalid, x_ref[...], 0))
        xb = jnp.concatenate([lo, hi], axis=1).astype(jnp.bfloat16)
        g = jnp.dot(xb, wgb_ref[...], preferred_element_type=jnp.float32)
        u = jnp.dot(xb, wub_ref[...], preferred_element_type=jnp.float32)
        a = (g * jax.nn.sigmoid(g) * u).astype(jnp.bfloat16)
        o_ref[...] = _pack_pairs(jnp.dot(a, wdb_ref[...], preferred_element_type=jnp.float32))

    @pl.when(nvalid == 0)
    def _():
        o_ref[...] = jnp.zeros_like(o_ref)


def _experts(buf, tile_expert, tile_nvalid, wg, wu, wd, *, layer):
    rows, wp = buf.shape
    _, _, d, f = wg.shape
    n_tiles = rows // EXPERT_TILE
    grid_spec = pltpu.PrefetchScalarGridSpec(
        num_scalar_prefetch=2,
        grid=(n_tiles,),
        in_specs=[pl.BlockSpec((EXPERT_TILE, wp), lambda t, te, nv: (t, 0)),
                  pl.BlockSpec((None, 1, d, f), lambda t, te, nv: (layer, te[t], 0, 0)),
                  pl.BlockSpec((None, 1, d, f), lambda t, te, nv: (layer, te[t], 0, 0)),
                  pl.BlockSpec((None, 1, f, d), lambda t, te, nv: (layer, te[t], 0, 0))],
        out_specs=pl.BlockSpec((EXPERT_TILE, wp), lambda t, te, nv: (t, 0)),
        scratch_shapes=[pltpu.VMEM((d, f), jnp.bfloat16), pltpu.VMEM((d, f), jnp.bfloat16),
                        pltpu.VMEM((f, d), jnp.bfloat16)],
    )
    return pl.pallas_call(
        _expert_kernel,
        grid_spec=grid_spec,
        out_shape=jax.ShapeDtypeStruct((rows, wp), jnp.int32),
        compiler_params=_cparams("arbitrary"),
        name="expert_ffn",
    )(tile_expert, tile_nvalid, buf, wg, wu, wd)


def _final_kernel(y0_ref, y1_ref, rw_ref, x_ref, mod_ref, fg_ref, o_ref):
    x = x_ref[...] + mod_ref[0, 5:6, :] * _moe_mix(y0_ref, y1_ref, rw_ref, slice(None))
    r = lax.rsqrt(jnp.mean(x * x, axis=-1, keepdims=True) + EPS)
    o_ref[...] = x * r * fg_ref[...]


def _final_kernel_into(y0_ref, y1_ref, rw_ref, x_ref, mod_ref, fg_ref, prev_ref, o_ref):
    del prev_ref
    _final_kernel(y0_ref, y1_ref, rw_ref, x_ref, mod_ref, fg_ref, o_ref)


def _final_combine(y, rw, xall, mods, final_g, *, tm, n_tiles, s, out_rows, out_tile0, prev_out=None):
    d = xall.shape[1]
    row = lambda w: pl.BlockSpec((tm, w), lambda i: (i, 0))
    in_specs = [row(d // 2),
                pl.BlockSpec((tm, d // 2), lambda i: (i + n_tiles, 0)),
                row(LANES), row(d),
                pl.BlockSpec((1, N_MOD, d), lambda i: ((i * tm) // s, 0, 0)),
                pl.BlockSpec(final_g.shape, lambda i: (0, 0))]
    args = [y, y, rw, xall, mods, final_g]
    aliases = {}
    body = _final_kernel
    if prev_out is not None:
        in_specs.append(pl.BlockSpec(memory_space=pl.ANY))
        args.append(prev_out)
        aliases = {len(args) - 1: 0}
        body = _final_kernel_into
    return pl.pallas_call(
        body,
        grid=(n_tiles,),
        in_specs=in_specs,
        out_specs=pl.BlockSpec((tm, d), lambda i: (out_tile0 + i, 0)),
        out_shape=jax.ShapeDtypeStruct((out_rows, d), jnp.float32),
        input_output_aliases=aliases,
        compiler_params=_cparams("arbitrary"),
        name="moe_combine_final",
    )(*args)


def _dest_kernel(ps_ref, meta_ref, o_ref):
    slot = pl.program_id(0)
    e = meta_ref[pl.ds(slot, 1), :]
    d = meta_ref[pl.ds(TOP_K + slot, 1), :]
    for k in range(N_EXPERTS):
        d = d + jnp.where(e == float(k), ps_ref[k], 0.0)
    o_ref[...] = d.astype(jnp.int32)


def _dest_rows(meta, pstarts, tcols):
    n = meta.shape[1]
    nt = n // tcols
    return pl.pallas_call(
        _dest_kernel,
        grid=(TOP_K, nt),
        in_specs=[pl.BlockSpec(memory_space=pltpu.SMEM),
                  pl.BlockSpec((META_ROWS, tcols), lambda k, i: (0, i))],
        out_specs=pl.BlockSpec((1, tcols), lambda k, i: (0, k * nt + i)),
        out_shape=jax.ShapeDtypeStruct((1, TOP_K * n), jnp.int32),
        compiler_params=_cparams("arbitrary", "arbitrary"),
        name="moe_dest",
    )(pstarts.astype(jnp.float32), meta)


def _dispatch_plan(meta, counts, n_rows_buf, tcols):
    cnt = counts[0, :N_EXPERTS].astype(jnp.int32)
    padded = (cnt + EXPERT_TILE - 1) // EXPERT_TILE * EXPERT_TILE
    pends = jnp.cumsum(padded)
    pstarts = pends - padded
    dest = _dest_rows(meta, pstarts, tcols)
    tile_start = jnp.arange(n_rows_buf // EXPERT_TILE, dtype=jnp.int32) * EXPERT_TILE
    te = jnp.sum((tile_start[:, None] >= pends[None, :]).astype(jnp.int32), axis=1)
    te = jnp.minimum(te, N_EXPERTS - 1)
    onehot = te[:, None] == jnp.arange(N_EXPERTS, dtype=jnp.int32)[None, :]
    cnt_te = jnp.sum(jnp.where(onehot, cnt[None, :], 0), axis=1)
    pstart_te = jnp.sum(jnp.where(onehot, pstarts[None, :], 0), axis=1)
    nvalid = jnp.clip(cnt_te - (tile_start - pstart_te), 0, EXPERT_TILE).astype(jnp.int32)
    return dest, te, nvalid


def _permute_heads(w, axis):
    heads = [lax.slice_in_dim(w, HEAD_DIM * h, HEAD_DIM * (h + 1), axis=axis)
             for h in HEAD_PERM[::HEAD_DIM] // HEAD_DIM]
    return jnp.concatenate(heads, axis=axis)


def _rope_tables(s, tm):
    pos = np.arange(s)
    pos_row = jnp.asarray(pos // GRID_W, jnp.float32)
    pos_col = jnp.asarray(pos % GRID_W, jnp.float32)
    n_freq = HEAD_DIM // 4
    inv = ROPE_THETA ** (-jnp.arange(n_freq, dtype=jnp.float32) / n_freq)
    ang_row = pos_row[:, None] * inv
    ang_col = pos_col[:, None] * inv
    ang = jnp.concatenate([ang_row, ang_row, ang_col, ang_col] * (LANES // HEAD_DIM), axis=-1)
    sign = np.where((np.arange(LANES) % 32) < 16, -1.0, 1.0).astype(np.float32)
    cos_t = jnp.concatenate([jnp.cos(ang), jnp.ones((tm, LANES), jnp.float32)], axis=0)
    sin_t = jnp.concatenate([jnp.sin(ang) * sign, jnp.zeros((tm, LANES), jnp.float32)], axis=0)
    return cos_t, sin_t


def kernel(x, c, ctx, c_ctx, norm1_g, norm2_g, w_mod, b_mod, w_in, q_norm_g, k_norm_g, conv_w, conv_b, conv_ln_g, conv_ln_b, sink, w_out, w_group, b_group, w_expert, b_expert, w_gate, w_up, w_down, final_g):
    nb, s, d = x.shape
    l = ctx.shape[1]
    depth = w_in.shape[0]
    assert w_in.shape[2] == D_IN and w_out.shape[1] == D_MIX
    assert s % GRID_W == 0 and s >= 3 * WINDOW and s % WINDOW == 0 and l % WINDOW == 0
    groups = BATCH_GROUPS if nb % BATCH_GROUPS == 0 else 1
    nbg = nb // groups
    n_lat, n_ctx = nbg * s, nbg * l
    tm = _pick(np.gcd(s, n_ctx), (512, 256, 128))
    tq = _pick(s, (512, 256, 128))
    assert n_lat % l == 0
    bf = jnp.bfloat16
    f32 = jnp.float32

    x2d = x.reshape(nb * s, d)
    ctx2d = ctx.reshape(nb * l, d)
    c_all = jnp.concatenate([c, c_ctx[None, :]], axis=0)
    mods_all = _modulation(c_all, w_mod, b_mod).reshape(depth, nb + 1, N_MOD, d)
    cos_t, sin_t = _rope_tables(s, tm)
    head_id = np.arange(LANES) // HEAD_DIM
    gsum = jnp.asarray((head_id[:, None] == head_id[None, :]) / HEAD_DIM, bf)
    tri = jnp.asarray(np.tril(np.ones((tm, tm), np.float32), -1), bf)
    n_lat_tiles = n_lat // tm

    xs = [(x2d, ctx2d)] * groups
    pending = [None] * groups
    for i in range(depth):
        last = i == depth - 1
        with_ctx = not last
        qg = jnp.tile(q_norm_g[i], LANES // HEAD_DIM)[None, :]
        kg = jnp.tile(k_norm_g[i], LANES // HEAD_DIM)[None, :]
        w_in_bf = jnp.concatenate(
            [_permute_heads(w_in[i][:, OFF_AQ:OFF_AK], 1), w_in[i][:, OFF_AK:OFF_CQ],
             _permute_heads(w_in[i][:, OFF_CQ:OFF_CK], 1), w_in[i][:, OFF_CK:]], axis=1).astype(bf)
        w_out_bf = jnp.concatenate(
            [_permute_heads(w_out[i][0:A_Q], 0), w_out[i][A_Q:A_Q + B_CH],
             _permute_heads(w_out[i][A_Q + B_CH:], 0)], axis=0).astype(bf)
        sink2 = sink[i] * LOG2E
        conv_args = (conv_w[i].reshape(CONV_W, B_CH), conv_b[i][None, :], conv_ln_g[i][None, :],
                     conv_ln_b[i][None, :])
        wr32 = jnp.zeros((d, LANES), f32).at[:, :N_GROUPS].set(w_group[i])
        wr32 = wr32.at[:, N_GROUPS:N_GROUPS + N_EXPERTS].set(w_expert[i])
        wr_hi = wr32.astype(bf)
        wr = jnp.concatenate([wr_hi, (wr32 - wr_hi.astype(f32)).astype(bf)], axis=1)
        br = jnp.zeros((1, LANES), f32).at[0, :N_GROUPS].set(b_group[i])
        br = br.at[0, N_GROUPS:N_GROUPS + N_EXPERTS].set(b_expert[i])
        n_tok = n_lat + n_ctx if with_ctx else n_lat
        n_tiles = n_tok // tm
        n_rows_buf = 2 * n_tok + N_EXPERTS * EXPERT_TILE
        for g in range(groups):
            b0 = g * nbg
            mods = jnp.concatenate([mods_all[i, b0:b0 + nbg], mods_all[i, nb:nb + 1]], axis=0)
            x_offsets = (b0 * s // tm, b0 * l // tm)
            x_new, qa, kta, va, hgl, qc, ktc, vc = _inproj(
                xs[g], mods, norm1_g[i][None, :], w_in_bf, gsum, qg, kg, cos_t, sin_t, n=n_lat + n_ctx,
                tm=tm, n_lat_tiles=n_lat_tiles, s=s, nb=nbg, x_offsets=x_offsets, combine=pending[g])
            if pending[g] is not None:
                xs[g] = (x_new,)
            oa, ob, oc = _latent_mixers(qa, kta, va, qc, ktc, vc, sink2, hgl, conv_args,
                                        nb=nbg, s=s, l=l, tq=tq)
            if with_ctx:
                oa, ob, oc = _ctx_mixers(qa, kta, va, qc, ktc, vc, sink2, hgl, conv_args, oa, ob, oc,
                                         nb=nbg, s=s, l=l)
            xall, hp, meta, rw, counts = _outproj(xs[g], oa, ob, oc, mods, norm2_g[i][None, :], w_out_bf,
                                                  wr, br, tri, n=n_lat + n_ctx, tm=tm, n_tiles=n_tiles,
                                                  n_lat_tiles=n_lat_tiles, s=s, nb=nbg, x_offsets=x_offsets)
            dest, te, nvalid = _dispatch_plan(meta, counts, n_rows_buf,
                                              _pick(n_tok, (8192, 4096, 2048, 1024, 512, 256, 128)))
            buf = sc_scatter_rows2(hp, dest, n_rows_buf)
            eo = _experts(buf, te, nvalid, w_gate, w_up, w_down, layer=i)
            y = sc_gather_rows(eo, dest)
            xs[g] = (xall,)
            pending[g] = (y, rw, mods)
    out = None
    for g in range(groups):
        y, rw, mods = pending[g]
        out = _final_combine(y, rw, xs[g][0], mods, final_g[None, :], tm=tm, n_tiles=n_lat_tiles, s=s,
                             out_rows=nb * s, out_tile0=g * n_lat_tiles, prev_out=out)
    return out.reshape(nb, s, d)
```
